```python
import math
import jax, jax.numpy as jnp
from jax import lax
import numpy as np

D_MODEL = 2048
BATCH = 8
SEQ = 2048
DEPTH = 1
DEC_BATCH = 32
DEC_SEQ = 1
PAST_LEN = 16384
PAGE_SIZE = 128

D_CONV = 1024
CONV_W = 31
N_HEADS = 16
HEAD_DIM = 64
N_KV = 4
HPG = N_HEADS // N_KV
CMP_LEN = 32
CMP_STRIDE = 16
CMP_HID = 64
SEL_LEN = 64
SEL_TOP = 16
WINDOW = 512
QBLK = 128
N_BUCKETS = 32
MAX_DIST = 128
N_GROUPS = 4
EXP_PER_GROUP = 8
N_EXPERTS = N_GROUPS * EXP_PER_GROUP
TOP_K = 2
D_EXPERT = 1024
MOE_BLK = 128
EPS = 1e-6
NEG = -1e30
IN_SIZES = (2 * D_CONV, N_HEADS * HEAD_DIM) + (N_KV * HEAD_DIM,) * 6 + (3 * N_HEADS, D_MODEL, D_MODEL)
D_IN = sum(IN_SIZES)

kernel_name = 'hybrid_conv_nsa_hmoe_step'


def rmsnorm(x, g):
    xf = x.astype(jnp.float32)
    y = xf * lax.rsqrt(jnp.mean(xf * xf, axis=-1, keepdims=True) + EPS)
    return (y * g.astype(jnp.float32)).astype(x.dtype)


def layernorm(x, g, b):
    xf = x.astype(jnp.float32)
    mu = jnp.mean(xf, axis=-1, keepdims=True)
    var = jnp.mean(jnp.square(xf - mu), axis=-1, keepdims=True)
    y = (xf - mu) * lax.rsqrt(var + EPS) * g.astype(jnp.float32) + b.astype(jnp.float32)
    return y.astype(x.dtype)


def masked_softmax(s, mask):
    return jax.nn.softmax(jnp.where(mask, s, NEG), axis=-1) * mask


def rel_bucket(dist):
    n = jnp.maximum(dist, 0)
    max_exact = N_BUCKETS // 2
    nf = jnp.maximum(n, 1).astype(jnp.float32)
    large = max_exact + (jnp.log(nf / max_exact) / math.log(MAX_DIST / max_exact)
                         * (N_BUCKETS - max_exact)).astype(jnp.int32)
    return jnp.where(n < max_exact, n, jnp.minimum(large, N_BUCKETS - 1))


def attend(q, k, v, dist, mask, rel_bias):
    B, T = q.shape[:2]
    n = k.shape[1]
    qg = q.reshape(B, T, N_KV, HPG, HEAD_DIM)
    s = jnp.einsum('btghd,bngd->bghtn', qg, k).astype(jnp.float32) * HEAD_DIM ** -0.5
    bias = rel_bias[rel_bucket(dist)].astype(jnp.float32).transpose(2, 0, 1).reshape(N_KV, HPG, T, n)
    p = masked_softmax(s + bias, mask)
    o = jnp.einsum('bghtn,bngd->btghd', p.astype(v.dtype), v)
    return o.reshape(B, T, N_HEADS, HEAD_DIM), p


def window_attend(q, qpos, k, v, kpos, rel_bias):
    d = qpos[:, None] - kpos[None, :]
    mask = (d >= 0) & (d <= WINDOW) & (kpos[None, :] >= 0)
    return attend(q, k, v, d, mask, rel_bias)[0]


def window_prompt(q, kp, vp, rel_bias):
    B, T = q.shape[:2]
    nb = T // QBLK
    span = WINDOW + QBLK
    qb = q.reshape(B, nb, QBLK, N_HEADS, HEAD_DIM).transpose(1, 0, 2, 3, 4)

    def block(args):
        i, q_i = args
        start = i * QBLK
        k_i = lax.dynamic_slice_in_dim(kp, start, span, axis=1)
        v_i = lax.dynamic_slice_in_dim(vp, start, span, axis=1)
        qpos = start + jnp.arange(QBLK)
        kpos = start - WINDOW + jnp.arange(span)
        return window_attend(q_i, qpos, k_i, v_i, kpos, rel_bias)

    o = lax.map(block, (jnp.arange(nb), qb))
    return o.transpose(1, 0, 2, 3, 4).reshape(B, T, N_HEADS, HEAD_DIM)


def compress(k, pe, w1, w2):
    B, L = k.shape[:2]
    r = CMP_LEN // CMP_STRIDE
    n_chunk = L // CMP_STRIDE
    n_cmp = n_chunk - r + 1
    ch = k[:, :n_chunk * CMP_STRIDE].reshape(B, n_chunk, CMP_STRIDE, N_KV, HEAD_DIM)

    def part(j):
        sl = slice(j * CMP_STRIDE, (j + 1) * CMP_STRIDE)
        return jnp.einsum('bcpgd,pdf->bcgf', ch[:, j:j + n_cmp] + pe[sl][:, None, :], w1[sl])

    pre = part(0)
    for j in range(1, r):
        pre = pre + part(j)
    return jnp.einsum('bcgf,fd->bcgd', jax.nn.gelu(pre), w2)


def cmp_branch(q, qpos, k, v, p, rel_bias):
    kc = compress(k, p['cmp_pe_k'], p['cmp_w1_k'], p['cmp_w2_k'])
    vc = compress(v, p['cmp_pe_v'], p['cmp_w1_v'], p['cmp_w2_v'])
    end = jnp.arange(kc.shape[1]) * CMP_STRIDE + CMP_LEN - 1
    d = qpos[:, None] - end[None, :]
    return attend(q, kc, vc, d, d >= 0, rel_bias)


def select_blocks(p_cmp, qpos, length):
    n_cmp = p_cmp.shape[-1]
    n_sel = -(-length // SEL_LEN)
    cs = jnp.arange(n_cmp)[:, None] * CMP_STRIDE
    ss = jnp.arange(n_sel)[None, :] * SEL_LEN
    overlap = ((cs < ss + SEL_LEN) & (cs + CMP_LEN > ss)).astype(jnp.float32)
    imp = jnp.einsum('bghtc,cj->bgtj', p_cmp, overlap)
    j = jnp.arange(n_sel)[None, :]
    cur = (qpos // SEL_LEN)[:, None]
    valid = j * SEL_LEN <= qpos[:, None]
    forced = (j == 0) | (j == cur) | (j == cur - 1)
    score = jnp.where(forced, jnp.inf, jnp.where(valid, imp, -jnp.inf))
    return lax.top_k(score, min(SEL_TOP, n_sel))[1]


def sel_group(q_g, qpos, idx_g, kb_g, vb_g, rb_g):
    tc, kk = idx_g.shape
    kg = kb_g[idx_g]
    vg = vb_g[idx_g]
    kpos = idx_g[..., None] * SEL_LEN + jnp.arange(SEL_LEN)
    d = qpos[:, None, None] - kpos
    s = jnp.einsum('thd,tksd->thks', q_g, kg).astype(jnp.float32) * HEAD_DIM ** -0.5
    s = s + rb_g[rel_bucket(d)].astype(jnp.float32).transpose(0, 3, 1, 2)
    pr = masked_softmax(s.reshape(tc, HPG, kk * SEL_LEN), (d >= 0).reshape(tc, 1, kk * SEL_LEN))
    return jnp.einsum('thn,tnd->thd', pr.astype(vg.dtype), vg.reshape(tc, kk * SEL_LEN, HEAD_DIM))


def sel_branch(q, qpos, idx, k, v, rel_bias):
    B, T = q.shape[:2]
    L = k.shape[1]
    n_sel = -(-L // SEL_LEN)
    pad = n_sel * SEL_LEN - L

    def blocks(a):
        a = jnp.pad(a, ((0, 0), (0, pad), (0, 0), (0, 0)))
        return a.reshape(B, n_sel, SEL_LEN, N_KV, HEAD_DIM).transpose(0, 3, 1, 2, 4)

    kb, vb = blocks(k), blocks(v)
    tc = QBLK if T % QBLK == 0 else T
    nc = T // tc
    qc = q.reshape(B, nc, tc, N_KV, HPG, HEAD_DIM).transpose(0, 1, 3, 2, 4, 5)
    ic = idx.reshape(B, N_KV, nc, tc, -1).transpose(0, 2, 1, 3, 4)
    pc = qpos.reshape(nc, tc)
    rb = rel_bias.reshape(N_BUCKETS, N_KV, HPG).transpose(1, 0, 2)
    group_fn = jax.vmap(sel_group, in_axes=(0, None, 0, 0, 0, 0))

    def per_seq(args):
        q_b, i_b, kb_b, vb_b = args

        def per_chunk(cargs):
            q_c, i_c, p_c = cargs
            return group_fn(q_c, p_c, i_c, kb_b, vb_b, rb)

        return lax.map(per_chunk, (q_b, i_b, pc))

    o = lax.map(per_seq, (qc, ic, kb, vb))
    return o.transpose(0, 1, 3, 2, 4, 5).reshape(B, T, N_HEADS, HEAD_DIM)


def project_in(x, p):
    B, T, _ = x.shape
    h = rmsnorm(x, p['norm_mix'])
    z = h @ p['w_in'] + p['b_in']
    cuts = np.cumsum(IN_SIZES)[:-1].tolist()
    zc, q, kc, vc, ks, vs, kw, vw, zg, ga, gb = jnp.split(z, cuts, axis=-1)
    a, b = jnp.split(zc, 2, axis=-1)
    kv = lambda t: t.reshape(B, T, N_KV, HEAD_DIM)
    u = a * jax.nn.sigmoid(b)
    return (u, q.reshape(B, T, N_HEADS, HEAD_DIM), kv(kc), kv(vc), kv(ks), kv(vs), kv(kw), kv(vw), zg, ga, gb)


def conv_module(u_pad, p):
    c = lax.conv_general_dilated(u_pad, p['conv_w'][:, None, :], (1,), 'VALID',
                                 dimension_numbers=('NWC', 'WIO', 'NWC'),
                                 feature_group_count=D_CONV) + p['conv_b']
    c = jax.nn.silu(layernorm(c, p['conv_ln_g'], p['conv_ln_b']))
    return c @ p['w_conv_out'] + p['b_conv_out']


def grouped_experts(h, eid, w, w_gate, w_up, w_down):
    N, D = h.shape
    A = N * TOP_K
    fe = eid.reshape(A)
    ft = jnp.repeat(jnp.arange(N), TOP_K)
    fw = w.reshape(A)
    order = jnp.argsort(fe)
    se = fe[order]
    counts = jnp.bincount(fe, length=N_EXPERTS)
    padded = (counts + MOE_BLK - 1) // MOE_BLK * MOE_BLK
    seg_end = jnp.cumsum(padded)
    seg_start = seg_end - padded
    start = jnp.cumsum(counts) - counts
    dest = seg_start[se] + jnp.arange(A) - start[se]
    n_blk = (A + N_EXPERTS * (MOE_BLK - 1) + MOE_BLK - 1) // MOE_BLK
    P = n_blk * MOE_BLK
    rows_tok = jnp.full((P,), N, jnp.int32).at[dest].set(ft[order])
    rows_w = jnp.zeros((P,), jnp.float32).at[dest].set(fw[order])
    blk_e = jnp.minimum(jnp.searchsorted(seg_end, jnp.arange(n_blk) * MOE_BLK, side='right'), N_EXPERTS - 1)
    hp = jnp.concatenate([h, jnp.zeros((1, D), h.dtype)], axis=0)
    xs = hp[rows_tok].reshape(n_blk, MOE_BLK, D)

    def expert_blk(args):
        xb, e = args
        return (jax.nn.silu(xb @ w_gate[e]) * (xb @ w_up[e])) @ w_down[e]

    yb = lax.map(expert_blk, (xs, blk_e)).reshape(P, D)
    y = jnp.zeros((N + 1, D), h.dtype).at[rows_tok].add(yb * rows_w[:, None].astype(h.dtype))
    return y[:N]


def moe(h, p):
    N = h.shape[0]
    lg = (h @ p['w_rg']).astype(jnp.float32) + p['b_rg']
    g = jnp.argmax(lg, axis=-1)
    wg = jnp.take_along_axis(jax.nn.softmax(lg, axis=-1), g[:, None], axis=1)[:, 0]
    le = (h @ p['w_re']).astype(jnp.float32).reshape(N, N_GROUPS, EXP_PER_GROUP) + p['b_re']
    le = jnp.take_along_axis(le, g[:, None, None], axis=1)[:, 0]
    pt, et = lax.top_k(jax.nn.softmax(le, axis=-1), TOP_K)
    w = wg[:, None] * pt / jnp.sum(pt, axis=-1, keepdims=True)
    eid = g[:, None] * EXP_PER_GROUP + et
    return grouped_experts(h, eid, w, p['w_gate'], p['w_up'], p['w_down'])


def merge_and_ffn(x, y_conv, o_cmp, o_sel, o_win, zg, ga, gb, p):
    B, T, _ = x.shape
    g = jax.nn.sigmoid(zg).reshape(B, T, 3, N_HEADS, 1)
    o = g[:, :, 0] * o_cmp + g[:, :, 1] * o_sel + g[:, :, 2] * o_win
    y_nsa = o.reshape(B, T, N_HEADS * HEAD_DIM) @ p['w_nsa_out']
    x = x + (jax.nn.sigmoid(ga) * y_conv + jax.nn.sigmoid(gb) * y_nsa) @ p['w_out']
    h2 = rmsnorm(x, p['norm_ffn']).reshape(B * T, D_MODEL)
    return x + moe(h2, p).reshape(B, T, D_MODEL)


def layer_prompt(x, p, rel_bias):
    B, T, _ = x.shape
    u, q, kc, vc, ks, vs, kw, vw, zg, ga, gb = project_in(x, p)
    qpos = jnp.arange(T)
    u_pad = jnp.pad(u, ((0, 0), (CONV_W - 1, 0), (0, 0)))
    y_conv = conv_module(u_pad, p)
    o_cmp, p_cmp = cmp_branch(q, qpos, kc, vc, p, rel_bias)
    idx = select_blocks(p_cmp, qpos, T)
    o_sel = sel_branch(q, qpos, idx, ks, vs, rel_bias)
    wpad = ((0, 0), (WINDOW, 0), (0, 0), (0, 0))
    kw_pad, vw_pad = jnp.pad(kw, wpad), jnp.pad(vw, wpad)
    o_win = window_prompt(q, kw_pad, vw_pad, rel_bias)
    y = merge_and_ffn(x, y_conv, o_cmp, o_sel, o_win, zg, ga, gb, p)
    new_cmp = jnp.stack([kc, vc], axis=2)
    new_sel = jnp.stack([ks, vs], axis=2)
    new_win = jnp.stack([kw_pad[:, -WINDOW:], vw_pad[:, -WINDOW:]], axis=2)
    new_conv = u_pad[:, -(CONV_W - 1):]
    return y, (new_cmp, new_sel, new_win, new_conv)


def gather_pages(pool, page_table):
    g = pool[page_table]
    return g.reshape(page_table.shape[0], page_table.shape[1] * PAGE_SIZE, 2, N_KV, HEAD_DIM)


def layer_sample(x, cmp_pool, sel_pool, win_buf, conv_buf, page_table, p, rel_bias):
    B, T, _ = x.shape
    past_len = page_table.shape[1] * PAGE_SIZE
    u, q, kc, vc, ks, vs, kw, vw, zg, ga, gb = project_in(x, p)
    qpos = past_len + jnp.arange(T)
    u_pad = jnp.concatenate([conv_buf, u], axis=1)
    y_conv = conv_module(u_pad, p)
    new_cmp = jnp.stack([kc, vc], axis=2)
    new_sel = jnp.stack([ks, vs], axis=2)
    cmp_all = jnp.concatenate([gather_pages(cmp_pool, page_table), new_cmp], axis=1)
    sel_all = jnp.concatenate([gather_pages(sel_pool, page_table), new_sel], axis=1)
    o_cmp, p_cmp = cmp_branch(q, qpos, cmp_all[:, :, 0], cmp_all[:, :, 1], p, rel_bias)
    idx = select_blocks(p_cmp, qpos, past_len + T)
    o_sel = sel_branch(q, qpos, idx, sel_all[:, :, 0], sel_all[:, :, 1], rel_bias)
    win_all = jnp.concatenate([win_buf, jnp.stack([kw, vw], axis=2)], axis=1)
    kpos = past_len - WINDOW + jnp.arange(WINDOW + T)
    o_win = window_attend(q, qpos, win_all[:, :, 0], win_all[:, :, 1], kpos, rel_bias)
    y = merge_and_ffn(x, y_conv, o_cmp, o_sel, o_win, zg, ga, gb, p)
    return y, (new_cmp, new_sel, win_all[:, -WINDOW:], u_pad[:, -(CONV_W - 1):])


def setup_inputs(seed: int = 0) -> dict:
    key = jax.random.key(seed)
    keys = iter(jax.random.split(key, 48))

    def nrm(shape, scale):
        return jax.random.normal(next(keys), shape, jnp.float32) * scale

    def gain(shape):
        return 1.0 + nrm(shape, 0.02)

    n_pages = PAST_LEN // PAGE_SIZE
    n_pool = (DEC_BATCH * n_pages * 5 + 3) // 4
    kv_row = (2, N_KV, HEAD_DIM)
    perm = jax.random.permutation(next(keys), n_pool)
    page_table = perm[:DEC_BATCH * n_pages].reshape(DEC_BATCH, n_pages).astype(jnp.int32)
    return {
        'x_prompt': nrm((BATCH, SEQ, D_MODEL), 1.0),
        'x_sample': nrm((DEC_BATCH, DEC_SEQ, D_MODEL), 1.0),
        'cache_cmp_kv': nrm((DEPTH, n_pool, PAGE_SIZE) + kv_row, 1.0),
        'cache_sel_kv': nrm((DEPTH, n_pool, PAGE_SIZE) + kv_row, 1.0),
        'state_win_kv': nrm((DEPTH, DEC_BATCH, WINDOW) + kv_row, 1.0),
        'state_conv': nrm((DEPTH, DEC_BATCH, CONV_W - 1, D_CONV), 1.0),
        'page_table': page_table,
        'rel_bias': nrm((N_BUCKETS, N_HEADS), 0.5),
        'norm_mix': gain((DEPTH, D_MODEL)),
        'w_in': nrm((DEPTH, D_MODEL, D_IN), D_MODEL ** -0.5),
        'b_in': nrm((DEPTH, D_IN), 0.02),
        'conv_w': nrm((DEPTH, CONV_W, D_CONV), CONV_W ** -0.5),
        'conv_b': nrm((DEPTH, D_CONV), 0.02),
        'conv_ln_g': gain((DEPTH, D_CONV)),
        'conv_ln_b': nrm((DEPTH, D_CONV), 0.02),
        'w_conv_out': nrm((DEPTH, D_CONV, D_MODEL), D_CONV ** -0.5),
        'b_conv_out': nrm((DEPTH, D_MODEL), 0.02),
        'cmp_pe_k': nrm((DEPTH, CMP_LEN, HEAD_DIM), 0.1),
        'cmp_w1_k': nrm((DEPTH, CMP_LEN, HEAD_DIM, CMP_HID), (CMP_LEN * HEAD_DIM) ** -0.5),
        'cmp_w2_k': nrm((DEPTH, CMP_HID, HEAD_DIM), CMP_HID ** -0.5),
        'cmp_pe_v': nrm((DEPTH, CMP_LEN, HEAD_DIM), 0.1),
        'cmp_w1_v': nrm((DEPTH, CMP_LEN, HEAD_DIM, CMP_HID), (CMP_LEN * HEAD_DIM) ** -0.5),
        'cmp_w2_v': nrm((DEPTH, CMP_HID, HEAD_DIM), CMP_HID ** -0.5),
        'w_nsa_out': nrm((DEPTH, N_HEADS * HEAD_DIM, D_MODEL), (N_HEADS * HEAD_DIM) ** -0.5),
        'w_out': nrm((DEPTH, D_MODEL, D_MODEL), D_MODEL ** -0.5),
        'norm_ffn': gain((DEPTH, D_MODEL)),
        'w_rg': nrm((DEPTH, D_MODEL, N_GROUPS), D_MODEL ** -0.5),
        'b_rg': nrm((DEPTH, N_GROUPS), 0.01),
        'w_re': nrm((DEPTH, D_MODEL, N_GROUPS * EXP_PER_GROUP), D_MODEL ** -0.5),
        'b_re': nrm((DEPTH, N_GROUPS, EXP_PER_GROUP), 0.01),
        'w_gate': nrm((DEPTH, N_EXPERTS, D_MODEL, D_EXPERT), D_MODEL ** -0.5),
        'w_up': nrm((DEPTH, N_EXPERTS, D_MODEL, D_EXPERT), D_MODEL ** -0.5),
        'w_down': nrm((DEPTH, N_EXPERTS, D_EXPERT, D_MODEL), D_EXPERT ** -0.5),
        'norm_final': gain((D_MODEL,)),
    }


def reference(x_prompt, x_sample, cache_cmp_kv, cache_sel_kv, state_win_kv, state_conv, page_table,
              rel_bias, norm_mix, w_in, b_in, conv_w, conv_b, conv_ln_g, conv_ln_b, w_conv_out, b_conv_out,
              cmp_pe_k, cmp_w1_k, cmp_w2_k, cmp_pe_v, cmp_w1_v, cmp_w2_v, w_nsa_out, w_out, norm_ffn,
              w_rg, b_rg, w_re, b_re, w_gate, w_up, w_down, norm_final):
    xp, xs = x_prompt, x_sample
    st_p, st_s = [], []
    for l in range(DEPTH):
        p = {
            'norm_mix': norm_mix[l], 'w_in': w_in[l], 'b_in': b_in[l],
            'conv_w': conv_w[l], 'conv_b': conv_b[l], 'conv_ln_g': conv_ln_g[l], 'conv_ln_b': conv_ln_b[l],
            'w_conv_out': w_conv_out[l], 'b_conv_out': b_conv_out[l],
            'cmp_pe_k': cmp_pe_k[l], 'cmp_w1_k': cmp_w1_k[l], 'cmp_w2_k': cmp_w2_k[l],
            'cmp_pe_v': cmp_pe_v[l], 'cmp_w1_v': cmp_w1_v[l], 'cmp_w2_v': cmp_w2_v[l],
            'w_nsa_out': w_nsa_out[l], 'w_out': w_out[l], 'norm_ffn': norm_ffn[l],
            'w_rg': w_rg[l], 'b_rg': b_rg[l], 'w_re': w_re[l], 'b_re': b_re[l],
            'w_gate': w_gate[l], 'w_up': w_up[l], 'w_down': w_down[l],
        }
        xp, sp = layer_prompt(xp, p, rel_bias)
        xs, ss = layer_sample(xs, cache_cmp_kv[l], cache_sel_kv[l], state_win_kv[l], state_conv[l],
                              page_table, p, rel_bias)
        st_p.append(sp)
        st_s.append(ss)
    y_prompt = rmsnorm(xp, norm_final)
    y_sample = rmsnorm(xs, norm_final)
    new_cmp_kv_prompt = jnp.stack([s[0] for s in st_p])
    new_cmp_kv_sample = jnp.stack([s[0] for s in st_s])
    new_sel_kv_prompt = jnp.stack([s[1] for s in st_p])
    new_sel_kv_sample = jnp.stack([s[1] for s in st_s])
    new_win_kv_prompt = jnp.stack([s[2] for s in st_p])
    new_win_kv_sample = jnp.stack([s[2] for s in st_s])
    new_conv_prompt = jnp.stack([s[3] for s in st_p])
    new_conv_sample = jnp.stack([s[3] for s in st_s])
    return (y_prompt, y_sample, new_cmp_kv_prompt, new_cmp_kv_sample, new_sel_kv_prompt, new_sel_kv_sample,
            new_win_kv_prompt, new_win_kv_sample, new_conv_prompt, new_conv_sample)
```

```python
import functools
import math

import jax
import jax.numpy as jnp
from jax import lax
from jax.experimental import pallas as pl
from jax.experimental.pallas import tpu as pltpu

F32 = jnp.float32
BF16 = jnp.bfloat16
I32 = jnp.int32

D_MODEL = 2048
D_CONV = 1024
CONV_W = 31
N_HEADS = 16
HEAD_DIM = 64
N_KV = 4
HPG = N_HEADS // N_KV
D_Q = N_HEADS * HEAD_DIM
D_KV = 2 * N_KV * HEAD_DIM
CMP_LEN = 32
CMP_STRIDE = 16
CMP_HID = 64
SEL_LEN = 64
SEL_TOP = 16
WINDOW = 512
N_BUCKETS = 32
MAX_DIST = 128
N_GROUPS = 4
EXP_PER_GROUP = 8
N_EXPERTS = N_GROUPS * EXP_PER_GROUP
TOP_K = 2
D_EXPERT = 1024
EPS = 1e-6
NEG = -1e30
PAGE = 128
LANES = 128
QT = 128
GROUP_ROWS = HPG * QT
MIB = 1024 * 1024


def _params(sem, vmem_mib=48):
    return pltpu.CompilerParams(dimension_semantics=sem, vmem_limit_bytes=vmem_mib * MIB)


def _mm_nt(a, b):
    return lax.dot_general(a, b, (((1,), (1,)), ((), ())), preferred_element_type=F32)


def _mm(a, b):
    return jnp.dot(a, b, preferred_element_type=F32)


def _sigmoid(z):
    return 1.0 / (1.0 + jnp.exp(-z))


def _silu(z):
    return z * _sigmoid(z)


def _gelu_tanh(z):
    return 0.5 * z * (1.0 + jnp.tanh(math.sqrt(2.0 / math.pi) * (z + 0.044715 * (z * z * z))))


def _rmsnorm_kernel(x_ref, g_ref, o_ref):
    x = x_ref[...]
    ms = jnp.mean(x * x, axis=-1, keepdims=True)
    o_ref[...] = (x * lax.rsqrt(ms + EPS) * g_ref[...]).astype(o_ref.dtype)


def _rmsnorm(x, g, tm):
    n, d = x.shape
    return pl.pallas_call(
        _rmsnorm_kernel,
        grid=(n // tm,),
        in_specs=[pl.BlockSpec((tm, d), lambda i: (i, 0)), pl.BlockSpec((1, d), lambda i: (0, 0))],
        out_specs=pl.BlockSpec((tm, d), lambda i: (i, 0)),
        out_shape=jax.ShapeDtypeStruct((n, d), BF16),
        compiler_params=_params(("parallel",)),
        name="rmsnorm",
    )(x, g)


def _proj_kernel(h_ref, w_ref, b_ref, o_ref, *, act):
    z = _mm_nt(h_ref[...], w_ref[...]) + b_ref[...]
    o_ref[...] = act(z).astype(o_ref.dtype)


def _proj(h, wt, b, act, out_dtype, tm, tn, name):
    n, k = h.shape
    nout = wt.shape[0]
    return pl.pallas_call(
        functools.partial(_proj_kernel, act=act),
        grid=(n // tm, nout // tn),
        in_specs=[
            pl.BlockSpec((tm, k), lambda i, j: (i, 0)),
            pl.BlockSpec((tn, k), lambda i, j: (j, 0)),
            pl.BlockSpec((1, tn), lambda i, j: (0, j)),
        ],
        out_specs=pl.BlockSpec((tm, tn), lambda i, j: (i, j)),
        out_shape=jax.ShapeDtypeStruct((n, nout), out_dtype),
        compiler_params=_params(("parallel", "parallel")),
        name=name,
    )(h, wt, b)


def _glu_kernel(h_ref, wa_ref, wb_ref, ba_ref, bb_ref, o_ref):
    h = h_ref[...]
    a = _mm_nt(h, wa_ref[...]) + ba_ref[...]
    b = _mm_nt(h, wb_ref[...]) + bb_ref[...]
    o_ref[...] = a * _sigmoid(b)


def _glu(h, wat, wbt, ba, bb, tm, tn):
    n, k = h.shape
    nout = wat.shape[0]
    wspec = pl.BlockSpec((tn, k), lambda i, j: (j, 0))
    bspec = pl.BlockSpec((1, tn), lambda i, j: (0, j))
    return pl.pallas_call(
        _glu_kernel,
        grid=(n // tm, nout // tn),
        in_specs=[pl.BlockSpec((tm, k), lambda i, j: (i, 0)), wspec, wspec, bspec, bspec],
        out_specs=pl.BlockSpec((tm, tn), lambda i, j: (i, j)),
        out_shape=jax.ShapeDtypeStruct((n, nout), F32),
        compiler_params=_params(("parallel", "parallel")),
        name="glu",
    )(h, wat, wbt, ba, bb)


def _kvt_kernel(h_ref, w_ref, b_ref, o_ref):
    o_ref[0, 0] = _mm_nt(w_ref[...], h_ref[...]) + b_ref[...]


def _kv_transposed(h, wt, bcol, batch, seq, tm):
    n, k = h.shape
    kinds = wt.shape[0] // D_KV
    nt = seq // tm
    return pl.pallas_call(
        _kvt_kernel,
        grid=(n // tm, kinds),
        in_specs=[
            pl.BlockSpec((tm, k), lambda i, j: (i, 0)),
            pl.BlockSpec((D_KV, k), lambda i, j: (j, 0)),
            pl.BlockSpec((D_KV, 1), lambda i, j: (j, 0)),
        ],
        out_specs=pl.BlockSpec((1, 1, D_KV, tm), lambda i, j: (j, i // nt, 0, i % nt)),
        out_shape=jax.ShapeDtypeStruct((kinds, batch, D_KV, seq), F32),
        compiler_params=_params(("parallel", "parallel")),
        name="kv_transposed",
    )(h, wt, bcol)


def _merge_a_kernel(h_ref, c_ref, o1_ref, o2_ref, o3_ref, wga_ref, wgb_ref, wc_ref, wn_ref,
                    bga_ref, bgb_ref, bc_ref, m_ref):
    h = h_ref[...]
    ga = _sigmoid(_mm_nt(h, wga_ref[...]) + bga_ref[...])
    gb = _sigmoid(_mm_nt(h, wgb_ref[...]) + bgb_ref[...])
    yc = _mm(c_ref[...], wc_ref[...]) + bc_ref[...]
    o = (o1_ref[...] + o2_ref[...] + o3_ref[...]).astype(BF16)
    yn = _mm(o, wn_ref[...])
    m_ref[...] = (ga * yc + gb * yn).astype(m_ref.dtype)


def _merge_a(h, c, o1, o2, o3, wgat, wgbt, wc, wn, bga, bgb, bc, tm, tn):
    n, k = h.shape
    tok = lambda width: pl.BlockSpec((tm, width), lambda i, j: (i, 0))
    wt_spec = pl.BlockSpec((tn, k), lambda i, j: (j, 0))
    w_spec = pl.BlockSpec((c.shape[1], tn), lambda i, j: (0, j))
    b_spec = pl.BlockSpec((1, tn), lambda i, j: (0, j))
    return pl.pallas_call(
        _merge_a_kernel,
        grid=(n // tm, D_MODEL // tn),
        in_specs=[tok(k), tok(D_CONV), tok(D_Q), tok(D_Q), tok(D_Q), wt_spec, wt_spec, w_spec, w_spec,
                  b_spec, b_spec, b_spec],
        out_specs=pl.BlockSpec((tm, tn), lambda i, j: (i, j)),
        out_shape=jax.ShapeDtypeStruct((n, D_MODEL), BF16),
        compiler_params=_params(("parallel", "parallel")),
        name="merge_gates",
    )(h, c, o1, o2, o3, wgat, wgbt, wc, wn, bga, bgb, bc)


def _merge_b_kernel(x_ref, m_ref, wo_ref, gn_ref, wr_ref, br_ref, x1_ref, h2_ref, eid_ref, rw_ref):
    x1 = x_ref[...] + _mm(m_ref[...], wo_ref[...])
    x1_ref[...] = x1
    ms = jnp.mean(x1 * x1, axis=-1, keepdims=True)
    h2 = x1 * lax.rsqrt(ms + EPS) * gn_ref[...]
    h2_ref[...] = h2
    logits = _mm_nt(wr_ref[...], h2.astype(BF16)) + br_ref[...]
    tm = logits.shape[1]
    row = lax.broadcasted_iota(I32, (8, tm), 0)
    lg = jnp.where(row < N_GROUPS, logits[0:8], -jnp.inf)
    gmax = jnp.max(lg, axis=0, keepdims=True)
    gidx = jnp.min(jnp.where(lg == gmax, row, 8), axis=0, keepdims=True)
    wg = 1.0 / jnp.sum(jnp.exp(lg - gmax), axis=0, keepdims=True)
    le = jnp.zeros((8, tm), F32)
    for g in range(N_GROUPS):
        le = jnp.where(gidx == g, logits[8 + 8 * g:16 + 8 * g], le)
    ee = jnp.exp(le - jnp.max(le, axis=0, keepdims=True))
    pz = ee / jnp.sum(ee, axis=0, keepdims=True)
    p1 = jnp.max(pz, axis=0, keepdims=True)
    i1 = jnp.min(jnp.where(pz == p1, row, 8), axis=0, keepdims=True)
    pz2 = jnp.where(row == i1, -1.0, pz)
    p2 = jnp.max(pz2, axis=0, keepdims=True)
    i2 = jnp.min(jnp.where(pz2 == p2, row, 8), axis=0, keepdims=True)
    den = p1 + p2
    e1 = gidx * EXP_PER_GROUP + i1
    e2 = gidx * EXP_PER_GROUP + i2
    eid_ref[...] = jnp.where(row == 0, e1, jnp.where(row == 1, e2, 0))
    rw_ref[...] = jnp.where(row == 0, wg * p1 / den, jnp.where(row == 1, wg * p2 / den, 0.0))


def _merge_b(x, m, wo, gn, wr, br, tm):
    n, d = x.shape
    return pl.pallas_call(
        _merge_b_kernel,
        grid=(n // tm,),
        in_specs=[
            pl.BlockSpec((tm, d), lambda i: (i, 0)),
            pl.BlockSpec((tm, d), lambda i: (i, 0)),
            pl.BlockSpec((d, d), lambda i: (0, 0)),
            pl.BlockSpec((1, d), lambda i: (0, 0)),
            pl.BlockSpec((LANES, d), lambda i: (0, 0)),
            pl.BlockSpec((LANES, 1), lambda i: (0, 0)),
        ],
        out_specs=[
            pl.BlockSpec((tm, d), lambda i: (i, 0)),
            pl.BlockSpec((tm, d), lambda i: (i, 0)),
            pl.BlockSpec((8, tm), lambda i: (0, i)),
            pl.BlockSpec((8, tm), lambda i: (0, i)),
        ],
        out_shape=[
            jax.ShapeDtypeStruct((n, d), F32),
            jax.ShapeDtypeStruct((n, d), F32),
            jax.ShapeDtypeStruct((8, n), I32),
            jax.ShapeDtypeStruct((8, n), F32),
        ],
        compiler_params=_params(("parallel",)),
        name="merge_out_router",
    )(x, m, wo, gn, wr, br)


_HALO = 32


def _conv_prompt_kernel(u_ref, prev_ref, w_ref, b_ref, g_ref, beta_ref, o_ref, buf_ref, acc_ref, *, tt):
    i = pl.program_id(1)
    prev = prev_ref[pl.ds(tt - _HALO, _HALO), :]
    buf_ref[pl.ds(0, _HALO), :] = jnp.where(i > 0, prev, 0.0)
    buf_ref[pl.ds(_HALO, tt), :] = u_ref[...]
    off = _HALO - (CONV_W - 1)
    for cc in range(D_CONV // LANES):
        cols = pl.ds(cc * LANES, LANES)
        acc = jnp.zeros((tt, LANES), F32)
        for k in range(CONV_W):
            acc = acc + buf_ref[pl.ds(off + k, tt), cols] * w_ref[pl.ds(k, 1), cols]
        acc_ref[:, cols] = acc + b_ref[:, cols]
    c = acc_ref[...]
    mu = jnp.mean(c, axis=-1, keepdims=True)
    var = jnp.mean(jnp.square(c - mu), axis=-1, keepdims=True)
    y = (c - mu) * lax.rsqrt(var + EPS) * g_ref[...] + beta_ref[...]
    o_ref[...] = _silu(y).astype(o_ref.dtype)


def _conv_prompt(u, w, b, g, beta, batch, seq, tt):
    nt = seq // tt
    vec = pl.BlockSpec((1, D_CONV), lambda bi, i: (0, 0))
    return pl.pallas_call(
        functools.partial(_conv_prompt_kernel, tt=tt),
        grid=(batch, nt),
        in_specs=[
            pl.BlockSpec((tt, D_CONV), lambda bi, i: (bi * nt + i, 0)),
            pl.BlockSpec((tt, D_CONV), lambda bi, i: (bi * nt + jnp.maximum(i - 1, 0), 0)),
            pl.BlockSpec((CONV_W, D_CONV), lambda bi, i: (0, 0)),
            vec, vec, vec,
        ],
        out_specs=pl.BlockSpec((tt, D_CONV), lambda bi, i: (bi * nt + i, 0)),
        out_shape=jax.ShapeDtypeStruct((batch * seq, D_CONV), BF16),
        scratch_shapes=[pltpu.VMEM((tt + _HALO, D_CONV), F32), pltpu.VMEM((tt, D_CONV), F32)],
        compiler_params=_params(("parallel", "parallel")),
        name="conv_prompt",
    )(u, u, w, b, g, beta)


def _conv_sample_kernel(st_ref, u_ref, w_ref, b_ref, g_ref, beta_ref, c_ref, new_ref):
    hist = CONV_W - 1
    u = u_ref[...]
    acc = u * w_ref[pl.ds(hist, 1), :] + b_ref[...]
    for t in range(hist):
        acc = acc + st_ref[t] * w_ref[pl.ds(t, 1), :]
    mu = jnp.mean(acc, axis=-1, keepdims=True)
    var = jnp.mean(jnp.square(acc - mu), axis=-1, keepdims=True)
    y = (acc - mu) * lax.rsqrt(var + EPS) * g_ref[...] + beta_ref[...]
    c_ref[...] = _silu(y).astype(c_ref.dtype)
    for t in range(hist - 1):
        new_ref[t] = st_ref[t + 1]
    new_ref[hist - 1] = u


def _conv_sample(state_t, u, w, b, g, beta):
    hist, bs, _ = state_t.shape
    return pl.pallas_call(
        _conv_sample_kernel,
        out_shape=[jax.ShapeDtypeStruct((bs, D_CONV), BF16), jax.ShapeDtypeStruct((hist, bs, D_CONV), F32)],
        compiler_params=pltpu.CompilerParams(vmem_limit_bytes=48 * MIB),
        name="conv_sample",
    )(state_t, u, w, b, g, beta)


def _bias_lookup(rb_ref, head, dist, valid):
    n = jnp.maximum(dist, 0)
    max_exact = N_BUCKETS // 2
    nf = jnp.maximum(n, 1).astype(F32)
    large = max_exact + (jnp.log(nf / max_exact) / math.log(MAX_DIST / max_exact)
                         * (N_BUCKETS - max_exact)).astype(I32)
    bucket = jnp.where(n < max_exact, n, jnp.minimum(large, N_BUCKETS - 1))
    val = jnp.zeros(dist.shape, F32)
    for k in range(N_BUCKETS):
        val = jnp.where(bucket == k, rb_ref[head, k], val)
    return jnp.where(valid, val, NEG)


def _bias_prompt_kernel(rb_ref, cmp_ref, win_ref, sel_ref, *, nqt, n_chunk):
    g = pl.program_id(0)
    row = lax.broadcasted_iota(I32, (QT, LANES), 0)
    col = lax.broadcasted_iota(I32, (QT, LANES), 1)
    for hh in range(HPG):
        head = g * HPG + hh
        rows = pl.ds(hh * QT, QT)

        def cmp_tile(qt, carry):
            for cb in range(n_chunk // LANES):
                dist = qt * QT + row - (col + cb * LANES) * CMP_STRIDE - (CMP_LEN - 1)
                cmp_ref[0, qt, rows, pl.ds(cb * LANES, LANES)] = _bias_lookup(rb_ref, head, dist, dist >= 0)
            return carry

        lax.fori_loop(0, nqt, cmp_tile, 0)
        for jb in range((WINDOW + QT) // LANES):
            dist = row - (col + jb * LANES) + WINDOW
            win_ref[0, rows, pl.ds(jb * LANES, LANES)] = _bias_lookup(
                rb_ref, head, dist, (dist >= 0) & (dist <= WINDOW))
        for kind in range(3):
            for jb in range(2):
                dist = kind * QT + row - (col + jb * LANES)
                sel_ref[0, kind, rows, pl.ds(jb * LANES, LANES)] = _bias_lookup(rb_ref, head, dist, dist >= 0)
        far = jnp.full((QT, LANES), 2 * MAX_DIST, I32)
        for jb in range(2):
            sel_ref[0, 3, rows, pl.ds(jb * LANES, LANES)] = _bias_lookup(rb_ref, head, far, far >= 0)


def _bias_prompt(rb_t, nqt, n_chunk):
    return pl.pallas_call(
        functools.partial(_bias_prompt_kernel, nqt=nqt, n_chunk=n_chunk),
        grid=(N_KV,),
        in_specs=[pl.BlockSpec(memory_space=pltpu.SMEM)],
        out_specs=[
            pl.BlockSpec((1, nqt, GROUP_ROWS, n_chunk), lambda g: (g, 0, 0, 0)),
            pl.BlockSpec((1, GROUP_ROWS, WINDOW + QT), lambda g: (g, 0, 0)),
            pl.BlockSpec((1, 4, GROUP_ROWS, 2 * LANES), lambda g: (g, 0, 0, 0)),
        ],
        out_shape=[
            jax.ShapeDtypeStruct((N_KV, nqt, GROUP_ROWS, n_chunk), F32),
            jax.ShapeDtypeStruct((N_KV, GROUP_ROWS, WINDOW + QT), F32),
            jax.ShapeDtypeStruct((N_KV, 4, GROUP_ROWS, 2 * LANES), F32),
        ],
        compiler_params=_params(("parallel",)),
        name="bias_maps_prompt",
    )(rb_t)


def _bias_sample_kernel(rb_ref, cmp_ref, win_ref, sel_ref, *, past, n_chunk, n_pages):
    col = lax.broadcasted_iota(I32, (N_HEADS, LANES), 1)
    rb = rb_ref[...]

    def lookup(dist):
        n = jnp.maximum(dist, 0)
        max_exact = N_BUCKETS // 2
        nf = jnp.maximum(n, 1).astype(F32)
        large = max_exact + (jnp.log(nf / max_exact) / math.log(MAX_DIST / max_exact)
                             * (N_BUCKETS - max_exact)).astype(I32)
        bucket = jnp.where(n < max_exact, n, jnp.minimum(large, N_BUCKETS - 1))
        val = jnp.zeros(dist.shape, F32)
        for k in range(N_BUCKETS):
            val = jnp.where(bucket == k, rb[:, k:k + 1], val)
        return jnp.where(dist >= 0, val, NEG)

    def cmp_block(cb, carry):
        dist = past - (col + cb * LANES) * CMP_STRIDE - (CMP_LEN - 1)
        cmp_ref[cb] = lookup(dist)
        return carry

    lax.fori_loop(0, n_chunk // LANES, cmp_block, 0)

    def win_block(jb, carry):
        win_ref[jb] = lookup(WINDOW - (col + jb * LANES))
        return carry

    lax.fori_loop(0, WINDOW // LANES + 1, win_block, 0)

    def sel_page(p, carry):
        sel_ref[p] = lookup(past - (p * PAGE + col))
        return carry

    lax.fori_loop(0, n_pages + 1, sel_page, 0)


def _bias_sample(rb_t, past, n_chunk, n_pages):
    cmp_b, win_b, sel_b = pl.pallas_call(
        functools.partial(_bias_sample_kernel, past=past, n_chunk=n_chunk, n_pages=n_pages),
        out_shape=[
            jax.ShapeDtypeStruct((n_chunk // LANES, N_HEADS, LANES), F32),
            jax.ShapeDtypeStruct((WINDOW // LANES + 1, N_HEADS, LANES), F32),
            jax.ShapeDtypeStruct((n_pages + 1, N_HEADS, LANES), F32),
        ],
        name="bias_maps_sample",
    )(rb_t)
    flat = lambda a: a.transpose(1, 0, 2).reshape(N_HEADS, -1)
    return flat(cmp_b), flat(win_b), sel_b


_CHUNKS_PER_PAGE = PAGE // CMP_STRIDE


def _compress_kernel(pt_ref, *refs, pp):
    page_refs = refs[:pp + 1]
    w0k_ref, w1k_ref, w0v_ref, w1v_ref, pe_ref, w2k_ref, w2v_ref, o_ref, tr_ref, x_ref = refs[pp + 1:]
    del pt_ref
    nch = pp * _CHUNKS_PER_PAGE
    m = nch + 8
    n_slab = D_KV // LANES
    for k in range(pp + 1):
        for s in range(n_slab):
            tr_ref[s, pl.ds(k * PAGE, PAGE), :] = page_refs[k][pl.ds(s * LANES, LANES), :].T
    for s in range(n_slab):
        is_v = s >= n_slab // 2
        w0 = (w0v_ref if is_v else w0k_ref)[...]
        w1 = (w1v_ref if is_v else w1k_ref)[...]
        w2 = (w2v_ref if is_v else w2k_ref)[...]
        for p in range(CMP_STRIDE):
            x_ref[:, pl.ds(p * LANES, LANES)] = tr_ref.at[s][pl.ds(p, m, stride=CMP_STRIDE), :]
        x = x_ref[...]
        pe0 = pe_ref[pl.ds(2 * int(is_v), 1), :]
        pe1 = pe_ref[pl.ds(2 * int(is_v) + 1, 1), :]
        y0 = _mm((x + pe0).astype(BF16), w0)
        y1 = _mm((x + pe1).astype(BF16), w1)
        pre = y0[0:nch] + y1[1:nch + 1]
        o_ref[0, :, pl.ds(s * LANES, LANES)] = _mm(_gelu_tanh(pre).astype(BF16), w2)


def _compress(pages, page_ids, weights, nb, n_pages, pp, paged):
    w0k, w1k, w0v, w1v, pe, w2k, w2v = weights
    steps = n_pages // pp
    nch = pp * _CHUNKS_PER_PAGE

    def page_spec(k):
        if paged:
            return pl.BlockSpec(
                (None, D_KV, PAGE),
                lambda b, s, pt: (pt[b * n_pages + jnp.minimum(s * pp + k, n_pages - 1)], 0, 0))
        return pl.BlockSpec((None, D_KV, PAGE), lambda b, s, pt: (b, 0, jnp.minimum(s * pp + k, n_pages - 1)))

    full = lambda a: pl.BlockSpec(a.shape, lambda b, s, pt: (0,) * a.ndim)
    grid_spec = pltpu.PrefetchScalarGridSpec(
        num_scalar_prefetch=1,
        grid=(nb, steps),
        in_specs=[page_spec(k) for k in range(pp + 1)] + [full(a) for a in weights],
        out_specs=pl.BlockSpec((1, nch, D_KV), lambda b, s, pt: (b, s, 0)),
        scratch_shapes=[
            pltpu.VMEM((D_KV // LANES, (pp + 1) * PAGE, LANES), F32),
            pltpu.VMEM((nch + 8, CMP_STRIDE * LANES), F32),
        ],
    )
    return pl.pallas_call(
        functools.partial(_compress_kernel, pp=pp),
        grid_spec=grid_spec,
        out_shape=jax.ShapeDtypeStruct((nb, n_pages * _CHUNKS_PER_PAGE, D_KV), F32),
        compiler_params=_params(("parallel", "parallel"), 56),
        name="compress_paged" if paged else "compress_prompt",
    )(page_ids, *([pages] * (pp + 1)), w0k, w1k, w0v, w1v, pe, w2k, w2v)


def _compress_weights(pe_k, w1_k, w2_k, pe_v, w1_v, w2_v):
    eye = jnp.eye(2, dtype=F32)

    def first(w1, j):
        w = w1[j * CMP_STRIDE:(j + 1) * CMP_STRIDE]
        return jnp.einsum("pdf,gh->pgdhf", w, eye).reshape(CMP_STRIDE * LANES, 2 * CMP_HID).astype(BF16)

    def pos(pe, j):
        return jnp.tile(pe[j * CMP_STRIDE:(j + 1) * CMP_STRIDE, None, :], (1, 2, 1)).reshape(1, CMP_STRIDE * LANES)

    def second(w2):
        return jnp.einsum("fd,gh->gfhd", w2, eye).reshape(2 * CMP_HID, 2 * HEAD_DIM).astype(BF16)

    pe = jnp.concatenate([pos(pe_k, 0), pos(pe_k, 1), pos(pe_v, 0), pos(pe_v, 1)], axis=0)
    return (first(w1_k, 0), first(w1_k, 1), first(w1_v, 0), first(w1_v, 1), pe, second(w2_k), second(w2_v))


def _stack_heads(q, g):
    return jnp.concatenate(
        [q[:, (g * HPG + hh) * HEAD_DIM:(g * HPG + hh + 1) * HEAD_DIM] for hh in range(HPG)], axis=0)


def _store_heads(o_ref, o, gate, g, branch):
    for hh in range(HPG):
        h = g * HPG + hh
        col = branch * N_HEADS + h
        o_ref[:, pl.ds(h * HEAD_DIM, HEAD_DIM)] = o[hh * QT:(hh + 1) * QT] * gate[:, col:col + 1]


def _cmp_prompt_kernel(q_ref, kc_ref, map_ref, gate_ref, ov_ref, o_ref, sb_ref, *, n_sel, top):
    qt = pl.program_id(1)
    q = q_ref[...]
    kv = kc_ref[0].astype(BF16)
    gate = gate_ref[...]
    ov = ov_ref[...]
    jrow = lax.broadcasted_iota(I32, (n_sel, QT), 0)
    tpos = qt * QT + lax.broadcasted_iota(I32, (n_sel, QT), 1)
    cur = tpos // SEL_LEN
    forced = (jrow == 0) | (jrow == cur) | (jrow == cur - 1)
    valid = jrow * SEL_LEN <= tpos
    selbias = []
    for g in range(N_KV):
        kg = kv[:, g * HEAD_DIM:(g + 1) * HEAD_DIM]
        vg = kv[:, D_KV // 2 + g * HEAD_DIM:D_KV // 2 + (g + 1) * HEAD_DIM]
        bias = map_ref[g, 0]
        s = _mm_nt(_stack_heads(q, g), kg) + bias
        e = jnp.exp(s - jnp.max(s, axis=-1, keepdims=True))
        p = e / jnp.sum(e, axis=-1, keepdims=True) * (bias > 0.5 * NEG).astype(F32)
        pb = p.astype(BF16)
        _store_heads(o_ref, _mm(pb, vg), gate, g, 0)
        imp = jnp.zeros((n_sel, QT), F32)
        for hh in range(HPG):
            imp = imp + _mm_nt(ov, pb[hh * QT:(hh + 1) * QT])
        score = jnp.where(forced, jnp.inf, jnp.where(valid, imp, -jnp.inf))
        rank = jnp.zeros((n_sel, QT), I32)
        for i in range(n_sel):
            si = score[i:i + 1, :]
            ahead = (si > score) | ((si == score) & (jrow > i))
            rank = rank + ahead.astype(I32)
        selbias.append(jnp.where(rank < top, 0.0, NEG))
    sb_ref[...] = jnp.concatenate(selbias, axis=0).T.astype(sb_ref.dtype)


def _cmp_prompt(q, kcmp, cmap, gates, ov_t, batch, seq):
    nqt = seq // QT
    n_chunk = kcmp.shape[1]
    n_sel = ov_t.shape[0]
    top = min(SEL_TOP, n_sel)
    return pl.pallas_call(
        functools.partial(_cmp_prompt_kernel, n_sel=n_sel, top=top),
        grid=(batch, nqt),
        in_specs=[
            pl.BlockSpec((QT, D_Q), lambda b, i: (b * nqt + i, 0)),
            pl.BlockSpec((1, n_chunk, D_KV), lambda b, i: (b, 0, 0)),
            pl.BlockSpec((N_KV, 1, GROUP_ROWS, n_chunk), lambda b, i: (0, i, 0, 0)),
            pl.BlockSpec((QT, LANES), lambda b, i: (b * nqt + i, 0)),
            pl.BlockSpec(ov_t.shape, lambda b, i: (0, 0)),
        ],
        out_specs=[
            pl.BlockSpec((QT, D_Q), lambda b, i: (b * nqt + i, 0)),
            pl.BlockSpec((QT, N_KV * n_sel), lambda b, i: (b * nqt + i, 0)),
        ],
        out_shape=[
            jax.ShapeDtypeStruct((batch * seq, D_Q), F32),
            jax.ShapeDtypeStruct((batch * seq, N_KV * n_sel), BF16),
        ],
        compiler_params=_params(("parallel", "parallel")),
        name="cmp_attn_prompt",
    )(q, kcmp, cmap, gates, ov_t)


_TK = 2 * LANES


def _sel_prompt_kernel(q_ref, kv_ref, sb_ref, map_ref, gate_ref, o_ref, kt_ref, vt_ref, *, n_sel, seq):
    qt = pl.program_id(1)
    aug = 2 * HEAD_DIM

    @pl.when(qt == 0)
    def _():
        blk = lax.broadcasted_iota(I32, (aug - HEAD_DIM, seq), 0)
        pos = lax.broadcasted_iota(I32, (aug - HEAD_DIM, seq), 1)
        onehot = jnp.where(pos // SEL_LEN == blk, 1.0, 0.0).astype(BF16)
        for g in range(N_KV):
            kt_ref[g, pl.ds(0, HEAD_DIM), :] = kv_ref[0, pl.ds(g * HEAD_DIM, HEAD_DIM), :].astype(BF16)
            kt_ref[g, pl.ds(HEAD_DIM, aug - HEAD_DIM), :] = onehot
            vt_ref[g] = kv_ref[0, pl.ds(D_KV // 2 + g * HEAD_DIM, HEAD_DIM), :].astype(BF16)

    q = q_ref[...].astype(F32)
    gate = gate_ref[...]
    sb = sb_ref[...].astype(F32)
    diag = qt // 2
    n_tiles = diag + 1
    for g in range(N_KV):
        sbg = sb[:, g * n_sel:(g + 1) * n_sel]
        pad = jnp.zeros((QT, aug - HEAD_DIM - n_sel), F32)
        qa = jnp.concatenate(
            [jnp.concatenate([q[:, (g * HPG + hh) * HEAD_DIM:(g * HPG + hh + 1) * HEAD_DIM], sbg, pad], axis=1)
             for hh in range(HPG)], axis=0).astype(BF16)

        def tile(kt, carry):
            m_i, l_i, acc = carry
            start = pl.multiple_of(kt * _TK, _TK)
            kind = jnp.where(kt == diag, qt % 2, jnp.where((kt == diag - 1) & (qt % 2 == 0), 2, 3))
            s = _mm(qa, kt_ref[g, :, pl.ds(start, _TK)]) + map_ref[g, kind]
            m_new = jnp.maximum(m_i, jnp.max(s, axis=-1, keepdims=True))
            alpha = jnp.exp(m_i - m_new)
            p = jnp.exp(s - m_new)
            l_new = alpha * l_i + jnp.sum(p, axis=-1, keepdims=True)
            acc = alpha * acc + _mm_nt(p.astype(BF16), vt_ref[g, :, pl.ds(start, _TK)])
            return m_new, l_new, acc

        init = (jnp.full((GROUP_ROWS, 1), -jnp.inf, F32), jnp.zeros((GROUP_ROWS, 1), F32),
                jnp.zeros((GROUP_ROWS, HEAD_DIM), F32))
        _, l_f, acc = lax.fori_loop(0, n_tiles, tile, init)
        _store_heads(o_ref, acc / l_f, gate, g, 1)


def _sel_prompt(q, kv_t, selbias, smap, gates, batch, seq):
    nqt = seq // QT
    n_sel = selbias.shape[1] // N_KV
    return pl.pallas_call(
        functools.partial(_sel_prompt_kernel, n_sel=n_sel, seq=seq),
        grid=(batch, nqt),
        in_specs=[
            pl.BlockSpec((QT, D_Q), lambda b, i: (b * nqt + i, 0)),
            pl.BlockSpec((1, D_KV, seq), lambda b, i: (b, 0, 0)),
            pl.BlockSpec((QT, N_KV * n_sel), lambda b, i: (b * nqt + i, 0)),
            pl.BlockSpec(smap.shape, lambda b, i: (0, 0, 0, 0)),
            pl.BlockSpec((QT, LANES), lambda b, i: (b * nqt + i, 0)),
        ],
        out_specs=pl.BlockSpec((QT, D_Q), lambda b, i: (b * nqt + i, 0)),
        out_shape=jax.ShapeDtypeStruct((batch * seq, D_Q), F32),
        scratch_shapes=[pltpu.VMEM((N_KV, 2 * HEAD_DIM, seq), BF16), pltpu.VMEM((N_KV, HEAD_DIM, seq), BF16)],
        compiler_params=_params(("arbitrary", "arbitrary"), 56),
        name="sel_attn_prompt",
    )(q, kv_t, selbias, smap, gates)


_WIN_TILES = WINDOW // QT + 1


def _win_prompt_kernel(q_ref, *refs):
    kv_refs = refs[:_WIN_TILES]
    map_ref, gate_ref, o_ref = refs[_WIN_TILES:]
    qt = pl.program_id(1)
    q = q_ref[...]
    gate = gate_ref[...]
    span = _WIN_TILES * QT
    col = lax.broadcasted_iota(I32, (GROUP_ROWS, span), 1)
    in_seq = col >= (_WIN_TILES - 1 - qt) * QT
    for g in range(N_KV):
        kt = jnp.concatenate([r[0, pl.ds(g * HEAD_DIM, HEAD_DIM), :] for r in kv_refs], axis=1).astype(BF16)
        vt = jnp.concatenate([r[0, pl.ds(D_KV // 2 + g * HEAD_DIM, HEAD_DIM), :] for r in kv_refs],
                             axis=1).astype(BF16)
        s = _mm(_stack_heads(q, g), kt) + map_ref[g]
        s = jnp.where(in_seq, s, NEG)
        e = jnp.exp(s - jnp.max(s, axis=-1, keepdims=True))
        p = e / jnp.sum(e, axis=-1, keepdims=True)
        _store_heads(o_ref, _mm_nt(p.astype(BF16), vt), gate, g, 2)


def _win_prompt(q, kv_t, wmap, gates, batch, seq):
    nqt = seq // QT

    def kv_spec(k):
        return pl.BlockSpec((1, D_KV, QT), lambda b, i: (b, 0, jnp.maximum(i - (_WIN_TILES - 1) + k, 0)))

    return pl.pallas_call(
        _win_prompt_kernel,
        grid=(batch, nqt),
        in_specs=[pl.BlockSpec((QT, D_Q), lambda b, i: (b * nqt + i, 0))]
        + [kv_spec(k) for k in range(_WIN_TILES)]
        + [pl.BlockSpec(wmap.shape, lambda b, i: (0, 0, 0)),
           pl.BlockSpec((QT, LANES), lambda b, i: (b * nqt + i, 0))],
        out_specs=pl.BlockSpec((QT, D_Q), lambda b, i: (b * nqt + i, 0)),
        out_shape=jax.ShapeDtypeStruct((batch * seq, D_Q), F32),
        compiler_params=_params(("parallel", "parallel")),
        name="win_attn_prompt",
    )(q, *([kv_t] * _WIN_TILES), wmap, gates)


def _group_rows(x_by_group):
    row = lax.broadcasted_iota(I32, x_by_group[0].shape, 0)
    out = x_by_group[0]
    for g in range(1, N_KV):
        out = jnp.where(row // HPG == g, x_by_group[g], out)
    return out


def _cmp_sample_kernel(q_ref, kc_ref, bias_ref, ov_ref, gate_ref, o_ref, idx_ref, *, n_sel, past):
    q = q_ref[0]
    kv = kc_ref[0].astype(BF16)
    bias = bias_ref[...]
    half = D_KV // 2
    s = _group_rows([_mm_nt(q, kv[:, g * HEAD_DIM:(g + 1) * HEAD_DIM]) for g in range(N_KV)]) + bias
    e = jnp.exp(s - jnp.max(s, axis=-1, keepdims=True))
    p = e / jnp.sum(e, axis=-1, keepdims=True) * (bias > 0.5 * NEG).astype(F32)
    pb = p.astype(BF16)
    o = _group_rows([_mm(pb, kv[:, half + g * HEAD_DIM:half + (g + 1) * HEAD_DIM]) for g in range(N_KV)])
    o_ref[0] = o * gate_ref[0]
    imp_h = _mm(pb, ov_ref[...])
    nsp = imp_h.shape[1]
    imp = jnp.concatenate(
        [imp_h[g * HPG:g * HPG + 1] + imp_h[g * HPG + 1:g * HPG + 2] + imp_h[g * HPG + 2:g * HPG + 3]
         + imp_h[g * HPG + 3:g * HPG + 4] for g in range(N_KV)] + [jnp.zeros((8 - N_KV, nsp), F32)], axis=0)
    j = lax.broadcasted_iota(I32, (8, nsp), 1)
    cur = past // SEL_LEN
    forced = (j == 0) | (j == cur) | (j == cur - 1)
    valid = (j * SEL_LEN <= past) & (j < n_sel)
    score = jnp.where(valid, jnp.where(forced, jnp.inf, imp), -jnp.inf)
    slot = lax.broadcasted_iota(I32, (8, LANES), 1)
    res = jnp.zeros((8, LANES), I32)
    for k in range(min(SEL_TOP, n_sel)):
        best = jnp.max(score, axis=-1, keepdims=True)
        pick = jnp.min(jnp.where(score == best, j, nsp), axis=-1, keepdims=True)
        res = jnp.where(slot == k, pick, res)
        score = jnp.where(j == pick, -jnp.inf, score)
    idx_ref[0] = res


def _cmp_sample(q3, kcmp, bias, ov, gate3, n_sel, past):
    bs, n_chunk, _ = kcmp.shape
    return pl.pallas_call(
        functools.partial(_cmp_sample_kernel, n_sel=n_sel, past=past),
        grid=(bs,),
        in_specs=[
            pl.BlockSpec((1, N_HEADS, HEAD_DIM), lambda b: (b, 0, 0)),
            pl.BlockSpec((1, n_chunk, D_KV), lambda b: (b, 0, 0)),
            pl.BlockSpec(bias.shape, lambda b: (0, 0)),
            pl.BlockSpec(ov.shape, lambda b: (0, 0)),
            pl.BlockSpec((1, N_HEADS, HEAD_DIM), lambda b: (b, 0, 0)),
        ],
        out_specs=[
            pl.BlockSpec((1, N_HEADS, HEAD_DIM), lambda b: (b, 0, 0)),
            pl.BlockSpec((1, 8, LANES), lambda b: (b, 0, 0)),
        ],
        out_shape=[
            jax.ShapeDtypeStruct((bs, N_HEADS, HEAD_DIM), F32),
            jax.ShapeDtypeStruct((bs, 8, LANES), I32),
        ],
        compiler_params=_params(("parallel",)),
        name="cmp_attn_sample",
    )(q3, kcmp, bias, ov, gate3)


def _sel_sample_kernel(blk_ref, phys_ref, q_ref, new_ref, bias_ref, gate_ref, *refs, top, n_pages):
    k_refs = refs[:top]
    v_refs = refs[top:2 * top]
    o_ref = refs[2 * top]
    del phys_ref
    b = pl.program_id(0)
    g = pl.program_id(1)
    q = q_ref[0, 0]
    lane = lax.broadcasted_iota(I32, (8, PAGE), 1)
    scores = []
    for k in range(top):
        blk = blk_ref[(b * N_KV + g) * top + k]
        page = jnp.minimum(blk // 2, n_pages)
        s = _mm(q, k_refs[k][...].astype(BF16)) + bias_ref[page]
        ok = (lane // SEL_LEN == blk % 2) & (blk < 2 * n_pages)
        scores.append(jnp.where(ok, s, NEG))
    k_new = new_ref[0, 0, 0:1, :].astype(BF16).astype(F32)
    v_new = new_ref[0, 0, 1:2, :].astype(BF16).astype(F32)
    s_new = jnp.sum(q.astype(F32) * k_new, axis=-1, keepdims=True) + bias_ref[n_pages]
    scores.append(s_new)
    s_all = jnp.concatenate(scores, axis=1)
    e = jnp.exp(s_all - jnp.max(s_all, axis=-1, keepdims=True))
    p = e / jnp.sum(e, axis=-1, keepdims=True)
    o = p[:, top * PAGE:top * PAGE + 1].astype(BF16).astype(F32) * v_new
    for k in range(top):
        o = o + _mm_nt(p[:, k * PAGE:(k + 1) * PAGE].astype(BF16), v_refs[k][...].astype(BF16))
    o_ref[0, 0] = o * gate_ref[0, 0]


def _sel_sample(blk, phys, q4, new4, bias4, gate4, pool4, top, n_pages):
    bs = q4.shape[0]

    def kv_spec(k, is_v):
        return pl.BlockSpec(
            (None, None, HEAD_DIM, PAGE),
            lambda b, g, blk_r, phys_r: (phys_r[(b * N_KV + g) * top + k], g + N_KV * is_v, 0, 0))

    grp = lambda b, g, blk_r, phys_r: (b, g, 0, 0)
    grid_spec = pltpu.PrefetchScalarGridSpec(
        num_scalar_prefetch=2,
        grid=(bs, N_KV),
        in_specs=[
            pl.BlockSpec((1, 1, 8, HEAD_DIM), grp),
            pl.BlockSpec((1, 1, 8, HEAD_DIM), grp),
            pl.BlockSpec((None, n_pages + 1, 8, LANES), lambda b, g, blk_r, phys_r: (g, 0, 0, 0)),
            pl.BlockSpec((1, 1, 8, HEAD_DIM), grp),
        ] + [kv_spec(k, 0) for k in range(top)] + [kv_spec(k, 1) for k in range(top)],
        out_specs=pl.BlockSpec((1, 1, 8, HEAD_DIM), grp),
    )
    return pl.pallas_call(
        functools.partial(_sel_sample_kernel, top=top, n_pages=n_pages),
        grid_spec=grid_spec,
        out_shape=jax.ShapeDtypeStruct((bs, N_KV, 8, HEAD_DIM), F32),
        compiler_params=_params(("parallel", "parallel")),
        name="sel_attn_sample",
    )(blk, phys, q4, new4, bias4, gate4, *([pool4] * (2 * top)))


def _win_sample_kernel(q_ref, new_ref, st_ref, col_ref, bias_ref, gate_ref, o_ref, nst_ref):
    st = st_ref[0]
    half = D_KV // 2
    for g in range(N_KV):
        q = q_ref[0, g]
        bias = bias_ref[g]
        s_old = _mm(q, st[g * HEAD_DIM:(g + 1) * HEAD_DIM, :].astype(BF16)) + bias[:, 0:WINDOW]
        k_new = new_ref[0, g, 0:1, :].astype(BF16).astype(F32)
        v_new = new_ref[0, g, 1:2, :].astype(BF16).astype(F32)
        s_new = jnp.sum(q.astype(F32) * k_new, axis=-1, keepdims=True) + bias[:, WINDOW:WINDOW + LANES]
        s_all = jnp.concatenate([s_old, s_new], axis=1)
        e = jnp.exp(s_all - jnp.max(s_all, axis=-1, keepdims=True))
        p = e / jnp.sum(e, axis=-1, keepdims=True)
        o = _mm_nt(p[:, 0:WINDOW].astype(BF16), st[half + g * HEAD_DIM:half + (g + 1) * HEAD_DIM, :].astype(BF16))
        o = o + p[:, WINDOW:WINDOW + 1].astype(BF16).astype(F32) * v_new
        o_ref[0, g] = o * gate_ref[0, g]
    rolled = pltpu.roll(st, WINDOW - 1, axis=1)
    lane = lax.broadcasted_iota(I32, (D_KV, LANES), 1)
    nst_ref[0, :, pl.ds(0, WINDOW - LANES)] = rolled[:, 0:WINDOW - LANES]
    nst_ref[0, :, pl.ds(WINDOW - LANES, LANES)] = jnp.where(
        lane == LANES - 1, col_ref[0], rolled[:, WINDOW - LANES:WINDOW])


def _win_sample(q4, new4, state_t, newcol, bias4, gate4):
    bs = q4.shape[0]
    grp = pl.BlockSpec((1, N_KV, 8, HEAD_DIM), lambda b: (b, 0, 0, 0))
    return pl.pallas_call(
        _win_sample_kernel,
        grid=(bs,),
        in_specs=[
            grp, grp,
            pl.BlockSpec((1, D_KV, WINDOW), lambda b: (b, 0, 0)),
            pl.BlockSpec((1, D_KV, LANES), lambda b: (b, 0, 0)),
            pl.BlockSpec(bias4.shape, lambda b: (0, 0, 0)),
            grp,
        ],
        out_specs=[grp, pl.BlockSpec((1, D_KV, WINDOW), lambda b: (b, 0, 0))],
        out_shape=[
            jax.ShapeDtypeStruct((bs, N_KV, 8, HEAD_DIM), F32),
            jax.ShapeDtypeStruct((bs, D_KV, WINDOW), F32),
        ],
        compiler_params=_params(("parallel",)),
        name="win_attn_sample",
    )(q4, new4, state_t, newcol, bias4, gate4)


_MOE_ROWS = 256
_GATHER_ROWS = 256


def _row_copy(src, src_row, dst, dst_row, sem):
    return pltpu.make_async_copy(src.at[pl.ds(src_row, 1)], dst.at[pl.ds(dst_row, 1)], sem)


def _gather_kernel(tok_ref, src0_ref, src1_ref, out_ref, sem, *, n0):
    base = pl.program_id(0) * _GATHER_ROWS

    def issue(r, carry):
        t = tok_ref[0, r]

        @pl.when(t < n0)
        def _():
            _row_copy(src0_ref, t, out_ref, base + r, sem).start()

        @pl.when(t >= n0)
        def _():
            _row_copy(src1_ref, t - n0, out_ref, base + r, sem).start()

        return carry

    lax.fori_loop(0, _GATHER_ROWS, issue, 0)

    def drain(r, carry):
        _row_copy(src0_ref, 0, out_ref, base, sem).wait()
        return carry

    lax.fori_loop(0, _GATHER_ROWS, drain, 0)


def _gather_rows(tok, src0, src1):
    p = tok.shape[0]
    d = src0.shape[1]
    steps = p // _GATHER_ROWS
    return pl.pallas_call(
        functools.partial(_gather_kernel, n0=src0.shape[0]),
        grid=(steps,),
        in_specs=[
            pl.BlockSpec((None, 1, _GATHER_ROWS), lambda i: (i, 0, 0), memory_space=pltpu.SMEM),
            pl.BlockSpec(memory_space=pl.ANY),
            pl.BlockSpec(memory_space=pl.ANY),
        ],
        out_specs=pl.BlockSpec(memory_space=pl.ANY),
        out_shape=jax.ShapeDtypeStruct((p, d), src0.dtype),
        scratch_shapes=[pltpu.SemaphoreType.DMA(())],
        compiler_params=pltpu.CompilerParams(dimension_semantics=("arbitrary",), has_side_effects=True),
        name="moe_gather",
    )(tok.reshape(steps, 1, _GATHER_ROWS), src0, src1)


def _expert_kernel(be_ref, nu_ref, x_ref, w_ref, wg_ref, wu_ref, wd_ref, y_ref):
    del be_ref
    i = pl.program_id(0)

    @pl.when(i < nu_ref[0])
    def _():
        x = x_ref[...].astype(BF16)
        a = _silu(_mm(x, wg_ref[...])) * _mm(x, wu_ref[...])
        y_ref[...] = _mm(a.astype(BF16), wd_ref[...]) * w_ref[...]

    @pl.when(i >= nu_ref[0])
    def _():
        y_ref[...] = jnp.zeros(y_ref.shape, y_ref.dtype)


def _experts(blk_e, n_used, xs, rows_w, w_gate, w_up, w_down):
    p, d = xs.shape
    n_blk = p // _MOE_ROWS
    wspec = lambda a: pl.BlockSpec((None,) + a.shape[1:], lambda i, be, nu: (be[i], 0, 0))
    grid_spec = pltpu.PrefetchScalarGridSpec(
        num_scalar_prefetch=2,
        grid=(n_blk,),
        in_specs=[
            pl.BlockSpec((_MOE_ROWS, d), lambda i, be, nu: (i, 0)),
            pl.BlockSpec((_MOE_ROWS, 1), lambda i, be, nu: (i, 0)),
            wspec(w_gate), wspec(w_up), wspec(w_down),
        ],
        out_specs=pl.BlockSpec((_MOE_ROWS, d), lambda i, be, nu: (i, 0)),
    )
    return pl.pallas_call(
        _expert_kernel,
        grid_spec=grid_spec,
        out_shape=jax.ShapeDtypeStruct((p, d), F32),
        compiler_params=_params(("arbitrary",), 56),
        name="moe_experts",
    )(blk_e, n_used, xs, rows_w, w_gate, w_up, w_down)


def _combine_kernel(pos_ref, x_ref, g_ref, yb_ref, o_ref, buf_ref, sem, *, tm):
    def issue(r, carry):
        _row_copy(yb_ref, pos_ref[0, 2 * r], buf_ref.at[0], r, sem).start()
        _row_copy(yb_ref, pos_ref[0, 2 * r + 1], buf_ref.at[1], r, sem).start()
        return carry

    lax.fori_loop(0, tm, issue, 0)

    def drain(r, carry):
        _row_copy(yb_ref, 0, buf_ref.at[0], 0, sem).wait()
        return carry

    lax.fori_loop(0, 2 * tm, drain, 0)
    x = x_ref[...] + (buf_ref[0] + buf_ref[1])
    ms = jnp.mean(x * x, axis=-1, keepdims=True)
    o_ref[...] = x * lax.rsqrt(ms + EPS) * g_ref[...]


def _combine(pos, x1, gfin, yb, tm):
    n, d = x1.shape
    steps = n // tm
    return pl.pallas_call(
        functools.partial(_combine_kernel, tm=tm),
        grid=(steps,),
        in_specs=[
            pl.BlockSpec((None, 1, 2 * tm), lambda i: (i, 0, 0), memory_space=pltpu.SMEM),
            pl.BlockSpec((tm, d), lambda i: (i, 0)),
            pl.BlockSpec((1, d), lambda i: (0, 0)),
            pl.BlockSpec(memory_space=pl.ANY),
        ],
        out_specs=pl.BlockSpec((tm, d), lambda i: (i, 0)),
        out_shape=jax.ShapeDtypeStruct((n, d), F32),
        scratch_shapes=[pltpu.VMEM((2, tm, d), F32), pltpu.SemaphoreType.DMA(())],
        compiler_params=_params(("arbitrary",)),
        name="moe_combine",
    )(pos.reshape(steps, 1, 2 * tm), x1, gfin, yb)


def _dispatch(eid, w):
    a = eid.shape[0]
    tb = _MOE_ROWS
    order = jnp.argsort(eid)
    se = eid[order]
    counts = jnp.bincount(eid, length=N_EXPERTS)
    padded = (counts + tb - 1) // tb * tb
    seg_end = jnp.cumsum(padded)
    seg_start = seg_end - padded
    start = jnp.cumsum(counts) - counts
    dest = (seg_start[se] + jnp.arange(a) - start[se]).astype(I32)
    n_blk = (a + N_EXPERTS * (tb - 1) + tb - 1) // tb
    n_blk = -(-n_blk * tb // _GATHER_ROWS) * _GATHER_ROWS // tb
    p = n_blk * tb
    rows_tok = jnp.zeros((p,), I32).at[dest].set((order // TOP_K).astype(I32))
    rows_w = jnp.zeros((p,), F32).at[dest].set(w[order])
    blk_e = jnp.minimum(jnp.searchsorted(seg_end, jnp.arange(n_blk) * tb, side="right"), N_EXPERTS - 1)
    pos = jnp.zeros((a,), I32).at[order].set(dest)
    n_used = (seg_end[-1] // tb).astype(I32).reshape(1)
    return rows_tok, rows_w.reshape(p, 1), blk_e.astype(I32), pos, n_used


def _overlap(n_chunk, n_sel, n_sel_pad):
    cs = jnp.arange(n_chunk)[:, None] * CMP_STRIDE
    ss = jnp.arange(n_sel_pad)[None, :] * SEL_LEN
    hit = (cs < ss + SEL_LEN) & (cs + CMP_LEN > ss) & (jnp.arange(n_sel_pad)[None, :] < n_sel)
    return hit.astype(BF16)


def kernel(x_prompt, x_sample, cache_cmp_kv, cache_sel_kv, state_win_kv, state_conv, page_table, rel_bias,
           norm_mix, w_in, b_in, conv_w, conv_b, conv_ln_g, conv_ln_b, w_conv_out, b_conv_out,
           cmp_pe_k, cmp_w1_k, cmp_w2_k, cmp_pe_v, cmp_w1_v, cmp_w2_v, w_nsa_out, w_out, norm_ffn,
           w_rg, b_rg, w_re, b_re, w_gate, w_up, w_down, norm_final):
    batch, seq, _ = x_prompt.shape
    bs = x_sample.shape[0]
    n_pool = cache_cmp_kv.shape[1]
    n_pages = page_table.shape[1]
    past = n_pages * PAGE
    n_tok = batch * seq
    l = 0

    wt = w_in[l].T.astype(BF16)
    bias_in = b_in[l]
    cuts = [0, D_CONV, 2 * D_CONV, 2 * D_CONV + D_Q, 2 * D_CONV + D_Q + 3 * D_KV]
    cuts += [cuts[-1] + 3 * N_HEADS, cuts[-1] + 3 * N_HEADS + D_MODEL, cuts[-1] + 3 * N_HEADS + 2 * D_MODEL]
    seg = lambda k: (wt[cuts[k]:cuts[k + 1]], bias_in[cuts[k]:cuts[k + 1]][None, :])
    (wa, ba), (wb, bb), (wq, bq), (wkv, bkv), (wzg, bzg), (wga, bga), (wgb, bgb) = [seg(k) for k in range(7)]
    wzg = jnp.pad(wzg, ((0, LANES - 3 * N_HEADS), (0, 0)))
    bzg = jnp.pad(bzg, ((0, 0), (0, LANES - 3 * N_HEADS)))
    wc = w_conv_out[l].astype(BF16)
    wn = w_nsa_out[l].astype(BF16)
    wo = w_out[l].astype(BF16)
    wr = jnp.zeros((LANES, D_MODEL), F32).at[0:N_GROUPS].set(w_rg[l].T).at[8:8 + N_EXPERTS].set(w_re[l].T)
    br = jnp.zeros((LANES,), F32).at[0:N_GROUPS].set(b_rg[l]).at[8:8 + N_EXPERTS].set(b_re[l].reshape(-1))
    wr, br = wr.astype(BF16), br[:, None]
    cw = _compress_weights(cmp_pe_k[l], cmp_w1_k[l], cmp_w2_k[l], cmp_pe_v[l], cmp_w1_v[l], cmp_w2_v[l])
    rb_t = rel_bias.T
    gmix, gffn, gfin = norm_mix[l][None, :], norm_ffn[l][None, :], norm_final[None, :]
    vec = lambda a: a[l][None, :]

    def project(x, tm):
        h = _rmsnorm(x, gmix, tm)
        u = _glu(h, wa, wb, ba, bb, tm, 512)
        q = _proj(h, wq, bq, lambda z: z * HEAD_DIM ** -0.5, BF16, tm, 512, "proj_q")
        gates = _proj(h, wzg, bzg, _sigmoid, F32, tm, LANES, "proj_head_gates")
        return h, u, q, gates

    xp = x_prompt.reshape(n_tok, D_MODEL)
    xs = x_sample.reshape(bs, D_MODEL)
    tm_p = 512
    hp, up, qp, gates_p = project(xp, tm_p)
    hs, us, qs, gates_s = project(xs, bs)
    kv_p = _kv_transposed(hp, wkv, bkv.reshape(-1, 1), batch, seq, tm_p)
    kv_s = _proj(hs, wkv, bkv, lambda z: z, F32, bs, 512, "proj_kv_sample")

    conv_args = (conv_w[l], vec(conv_b), vec(conv_ln_g), vec(conv_ln_b))
    cp = _conv_prompt(up, *conv_args, batch, seq, 256)
    cs, new_conv_s = _conv_sample(state_conv[l].transpose(1, 0, 2), us, *conv_args)

    nqt = seq // QT
    n_chunk_p = seq // CMP_STRIDE
    n_chunk_s = n_pages * _CHUNKS_PER_PAGE
    cmap, wmap, smap = _bias_prompt(rb_t, nqt, n_chunk_p)
    cbias_s, wbias_s, sbias_s = _bias_sample(rb_t, past, n_chunk_s, n_pages)

    ident = jnp.zeros((1,), I32)
    kcmp_p = _compress(kv_p[0], ident, cw, batch, seq // PAGE, seq // PAGE, False)
    feat_major = lambda a: a.transpose(0, 2, 3, 4, 1)
    pool_cmp = feat_major(cache_cmp_kv[l]).reshape(n_pool, D_KV, PAGE)
    pp_s = min(32, n_pages)
    kcmp_s = _compress(pool_cmp, page_table.reshape(-1), cw, bs, n_pages, pp_s, True)

    n_sel_p = seq // SEL_LEN
    o_cmp_p, selbias = _cmp_prompt(qp, kcmp_p, cmap, gates_p, _overlap(n_chunk_p, n_sel_p, n_sel_p).T,
                                   batch, seq)
    n_sel_s = -(-(past + 1) // SEL_LEN)
    n_sel_pad = -(-n_sel_s // LANES) * LANES
    top_s = min(SEL_TOP, n_sel_s)
    head_gate = lambda k: jnp.broadcast_to(
        gates_s[:, k * N_HEADS:(k + 1) * N_HEADS, None], (bs, N_HEADS, HEAD_DIM))
    qs3 = qs.reshape(bs, N_HEADS, HEAD_DIM)
    o_cmp_s, idx_s = _cmp_sample(qs3, kcmp_s, cbias_s, _overlap(n_chunk_s, n_sel_s, n_sel_pad), head_gate(0),
                                 n_sel_s, past)

    o_sel_p = _sel_prompt(qp, kv_p[1], selbias, smap, gates_p, batch, seq)
    by_group = lambda a: jnp.pad(a.reshape(bs, N_KV, HPG, -1), ((0, 0), (0, 0), (0, 8 - HPG), (0, 0)))
    q4 = by_group(qs3)
    blk = idx_s[:, :N_KV, :top_s].reshape(-1)
    phys = jnp.take_along_axis(
        page_table, jnp.minimum(idx_s[:, :N_KV, :top_s] // 2, n_pages - 1).reshape(bs, -1), axis=1).reshape(-1)
    kv_new = lambda k: jnp.pad(
        kv_s[:, k * D_KV:(k + 1) * D_KV].reshape(bs, 2, N_KV, HEAD_DIM).transpose(0, 2, 1, 3),
        ((0, 0), (0, 0), (0, 6), (0, 0)))
    bias_by_group = lambda a: jnp.pad(
        jnp.moveaxis(a, -2, 0).reshape((N_KV, HPG) + a.shape[:-2] + a.shape[-1:]),
        ((0, 0), (0, 8 - HPG)) + ((0, 0),) * (a.ndim - 1))
    sb4 = jnp.moveaxis(bias_by_group(sbias_s), 1, 2)
    pool_sel = feat_major(cache_sel_kv[l]).reshape(n_pool, 2 * N_KV, HEAD_DIM, PAGE)
    o_sel_s = _sel_sample(blk, phys, q4, kv_new(1), sb4, by_group(head_gate(1)), pool_sel, top_s, n_pages)

    o_win_p = _win_prompt(qp, kv_p[2], wmap, gates_p, batch, seq)
    win_t = feat_major(state_win_kv[l]).reshape(bs, D_KV, WINDOW)
    newcol = jnp.broadcast_to(kv_s[:, 2 * D_KV:3 * D_KV, None], (bs, D_KV, LANES))
    o_win_s, new_win_s = _win_sample(q4, kv_new(2), win_t, newcol, bias_by_group(wbias_s), by_group(head_gate(2)))

    m_p = _merge_a(hp, cp, o_cmp_p, o_sel_p, o_win_p, wga, wgb, wc, wn, bga, bgb, vec(b_conv_out), 512, 512)
    x1_p, h2_p, eid_p, rw_p = _merge_b(xp, m_p, wo, gffn, wr, br, 256)
    from_groups = lambda a: a[:, :, :HPG, :].reshape(bs, D_Q)
    m_s = _merge_a(hs, cs, o_cmp_s.reshape(bs, D_Q), from_groups(o_sel_s), from_groups(o_win_s),
                   wga, wgb, wc, wn, bga, bgb, vec(b_conv_out), bs, 512)
    x1_s, h2_s, eid_s, rw_s = _merge_b(xs, m_s, wo, gffn, wr, br, bs)

    eid = jnp.concatenate([eid_p[:TOP_K].T, eid_s[:TOP_K].T], axis=0).reshape(-1)
    rw = jnp.concatenate([rw_p[:TOP_K].T, rw_s[:TOP_K].T], axis=0).reshape(-1)
    rows_tok, rows_w, blk_e, pos, n_used = _dispatch(eid, rw)
    xg = _gather_rows(rows_tok, h2_p, h2_s)
    yb = _experts(blk_e, n_used, xg, rows_w, w_gate[l].astype(BF16), w_up[l].astype(BF16),
                  w_down[l].astype(BF16))
    y_p = _combine(pos[:TOP_K * n_tok], x1_p, gfin, yb, 256)
    y_s = _combine(pos[TOP_K * n_tok:], x1_s, gfin, yb, bs)

    def cache_rows(a_t, rows):
        return a_t.reshape(a_t.shape[0], 2, N_KV, HEAD_DIM, rows).transpose(0, 4, 1, 2, 3)[None]

    new_row = lambda k: kv_s[:, k * D_KV:(k + 1) * D_KV].reshape(1, bs, 1, 2, N_KV, HEAD_DIM)
    hist = CONV_W - 1
    return (
        y_p.reshape(batch, seq, D_MODEL),
        y_s.reshape(bs, 1, D_MODEL),
        cache_rows(kv_p[0], seq),
        new_row(0),
        cache_rows(kv_p[1], seq),
        new_row(1),
        cache_rows(kv_p[2][:, :, seq - WINDOW:], WINDOW),
        cache_rows(new_win_s, WINDOW),
        up.reshape(batch, seq, D_CONV)[None, :, seq - hist:],
        new_conv_s.transpose(1, 0, 2)[None],
    )
```

```python
import functools
import math

import jax
import jax.numpy as jnp
from jax import lax
from jax.experimental import pallas as pl
from jax.experimental.pallas import tpu as pltpu

F32 = jnp.float32
BF16 = jnp.bfloat16
I32 = jnp.int32

D_MODEL = 2048
D_CONV = 1024
CONV_W = 31
N_HEADS = 16
HEAD_DIM = 64
N_KV = 4
HPG = N_HEADS // N_KV
D_Q = N_HEADS * HEAD_DIM
D_KV = 2 * N_KV * HEAD_DIM
CMP_LEN = 32
CMP_STRIDE = 16
CMP_HID = 64
SEL_LEN = 64
SEL_TOP = 16
WINDOW = 512
N_BUCKETS = 32
MAX_DIST = 128
N_GROUPS = 4
EXP_PER_GROUP = 8
N_EXPERTS = N_GROUPS * EXP_PER_GROUP
TOP_K = 2
D_EXPERT = 1024
EPS = 1e-6
NEG = -1e30
PAGE = 128
LANES = 128
QT = 128
GROUP_ROWS = HPG * QT
MIB = 1024 * 1024


def _params(sem, vmem_mib=48):
    return pltpu.CompilerParams(dimension_semantics=sem, vmem_limit_bytes=vmem_mib * MIB)


def _mm_nt(a, b):
    return lax.dot_general(a, b, (((1,), (1,)), ((), ())), preferred_element_type=F32)


def _mm(a, b):
    return jnp.dot(a, b, preferred_element_type=F32)


def _sigmoid(z):
    return 1.0 / (1.0 + jnp.exp(-z))


def _silu(z):
    return z * _sigmoid(z)


def _gelu_tanh(z):
    return 0.5 * z * (1.0 + jnp.tanh(math.sqrt(2.0 / math.pi) * (z + 0.044715 * (z * z * z))))


def _rmsnorm_kernel(x_ref, g_ref, o_ref):
    x = x_ref[...]
    ms = jnp.mean(x * x, axis=-1, keepdims=True)
    o_ref[...] = (x * lax.rsqrt(ms + EPS) * g_ref[...]).astype(o_ref.dtype)


def _rmsnorm(x, g, tm):
    n, d = x.shape
    return pl.pallas_call(
        _rmsnorm_kernel,
        grid=(n // tm,),
        in_specs=[pl.BlockSpec((tm, d), lambda i: (i, 0)), pl.BlockSpec((1, d), lambda i: (0, 0))],
        out_specs=pl.BlockSpec((tm, d), lambda i: (i, 0)),
        out_shape=jax.ShapeDtypeStruct((n, d), BF16),
        compiler_params=_params(("parallel",)),
        name="rmsnorm",
    )(x, g)


def _proj_kernel(h_ref, w_ref, b_ref, o_ref, *, act):
    z = _mm_nt(h_ref[...], w_ref[...]) + b_ref[...]
    o_ref[...] = act(z).astype(o_ref.dtype)


def _proj(h, wt, b, act, out_dtype, tm, tn, name):
    n, k = h.shape
    nout = wt.shape[0]
    return pl.pallas_call(
        functools.partial(_proj_kernel, act=act),
        grid=(n // tm, nout // tn),
        in_specs=[
            pl.BlockSpec((tm, k), lambda i, j: (i, 0)),
            pl.BlockSpec((tn, k), lambda i, j: (j, 0)),
            pl.BlockSpec((1, tn), lambda i, j: (0, j)),
        ],
        out_specs=pl.BlockSpec((tm, tn), lambda i, j: (i, j)),
        out_shape=jax.ShapeDtypeStruct((n, nout), out_dtype),
        compiler_params=_params(("parallel", "parallel")),
        name=name,
    )(h, wt, b)


def _glu_kernel(h_ref, wa_ref, wb_ref, ba_ref, bb_ref, o_ref):
    h = h_ref[...]
    a = _mm_nt(h, wa_ref[...]) + ba_ref[...]
    b = _mm_nt(h, wb_ref[...]) + bb_ref[...]
    o_ref[...] = a * _sigmoid(b)


def _glu(h, wat, wbt, ba, bb, tm, tn):
    n, k = h.shape
    nout = wat.shape[0]
    wspec = pl.BlockSpec((tn, k), lambda i, j: (j, 0))
    bspec = pl.BlockSpec((1, tn), lambda i, j: (0, j))
    return pl.pallas_call(
        _glu_kernel,
        grid=(n // tm, nout // tn),
        in_specs=[pl.BlockSpec((tm, k), lambda i, j: (i, 0)), wspec, wspec, bspec, bspec],
        out_specs=pl.BlockSpec((tm, tn), lambda i, j: (i, j)),
        out_shape=jax.ShapeDtypeStruct((n, nout), F32),
        compiler_params=_params(("parallel", "parallel")),
        name="glu",
    )(h, wat, wbt, ba, bb)


def _kvt_kernel(h_ref, w_ref, b_ref, o_ref):
    o_ref[0, 0] = _mm_nt(w_ref[...], h_ref[...]) + b_ref[...]


def _kv_transposed(h, wt, bcol, batch, seq, tm):
    n, k = h.shape
    kinds = wt.shape[0] // D_KV
    nt = seq // tm
    return pl.pallas_call(
        _kvt_kernel,
        grid=(n // tm, kinds),
        in_specs=[
            pl.BlockSpec((tm, k), lambda i, j: (i, 0)),
            pl.BlockSpec((D_KV, k), lambda i, j: (j, 0)),
            pl.BlockSpec((D_KV, 1), lambda i, j: (j, 0)),
        ],
        out_specs=pl.BlockSpec((1, 1, D_KV, tm), lambda i, j: (j, i // nt, 0, i % nt)),
        out_shape=jax.ShapeDtypeStruct((kinds, batch, D_KV, seq), F32),
        compiler_params=_params(("parallel", "parallel")),
        name="kv_transposed",
    )(h, wt, bcol)


def _merge_a_kernel(h_ref, c_ref, o1_ref, o2_ref, o3_ref, wga_ref, wgb_ref, wc_ref, wn_ref,
                    bga_ref, bgb_ref, bc_ref, m_ref):
    h = h_ref[...]
    ga = _sigmoid(_mm_nt(h, wga_ref[...]) + bga_ref[...])
    gb = _sigmoid(_mm_nt(h, wgb_ref[...]) + bgb_ref[...])
    yc = _mm(c_ref[...], wc_ref[...]) + bc_ref[...]
    o = (o1_ref[...] + o2_ref[...] + o3_ref[...]).astype(BF16)
    yn = _mm(o, wn_ref[...])
    m_ref[...] = (ga * yc + gb * yn).astype(m_ref.dtype)


def _merge_a(h, c, o1, o2, o3, wgat, wgbt, wc, wn, bga, bgb, bc, tm, tn):
    n, k = h.shape
    tok = lambda width: pl.BlockSpec((tm, width), lambda i, j: (i, 0))
    wt_spec = pl.BlockSpec((tn, k), lambda i, j: (j, 0))
    w_spec = pl.BlockSpec((c.shape[1], tn), lambda i, j: (0, j))
    b_spec = pl.BlockSpec((1, tn), lambda i, j: (0, j))
    return pl.pallas_call(
        _merge_a_kernel,
        grid=(n // tm, D_MODEL // tn),
        in_specs=[tok(k), tok(D_CONV), tok(D_Q), tok(D_Q), tok(D_Q), wt_spec, wt_spec, w_spec, w_spec,
                  b_spec, b_spec, b_spec],
        out_specs=pl.BlockSpec((tm, tn), lambda i, j: (i, j)),
        out_shape=jax.ShapeDtypeStruct((n, D_MODEL), BF16),
        compiler_params=_params(("parallel", "parallel")),
        name="merge_gates",
    )(h, c, o1, o2, o3, wgat, wgbt, wc, wn, bga, bgb, bc)


def _merge_b_kernel(x_ref, m_ref, wo_ref, gn_ref, wr_ref, br_ref, h2_all_ref, x1_ref, h2_ref, eid_ref, rw_ref):
    del h2_all_ref
    x1 = x_ref[...] + _mm(m_ref[...], wo_ref[...])
    x1_ref[...] = x1
    ms = jnp.mean(x1 * x1, axis=-1, keepdims=True)
    h2 = x1 * lax.rsqrt(ms + EPS) * gn_ref[...]
    h2_ref[...] = h2
    logits = _mm_nt(wr_ref[...], h2.astype(BF16)) + br_ref[...]
    tm = logits.shape[1]
    row = lax.broadcasted_iota(I32, (8, tm), 0)
    lg = jnp.where(row < N_GROUPS, logits[0:8], -jnp.inf)
    gmax = jnp.max(lg, axis=0, keepdims=True)
    gidx = jnp.min(jnp.where(lg == gmax, row, 8), axis=0, keepdims=True)
    wg = 1.0 / jnp.sum(jnp.exp(lg - gmax), axis=0, keepdims=True)
    le = jnp.zeros((8, tm), F32)
    for g in range(N_GROUPS):
        le = jnp.where(gidx == g, logits[8 + 8 * g:16 + 8 * g], le)
    ee = jnp.exp(le - jnp.max(le, axis=0, keepdims=True))
    pz = ee / jnp.sum(ee, axis=0, keepdims=True)
    p1 = jnp.max(pz, axis=0, keepdims=True)
    i1 = jnp.min(jnp.where(pz == p1, row, 8), axis=0, keepdims=True)
    pz2 = jnp.where(row == i1, -1.0, pz)
    p2 = jnp.max(pz2, axis=0, keepdims=True)
    i2 = jnp.min(jnp.where(pz2 == p2, row, 8), axis=0, keepdims=True)
    den = p1 + p2
    e1 = gidx * EXP_PER_GROUP + i1
    e2 = gidx * EXP_PER_GROUP + i2
    eid_ref[...] = jnp.where(row == 0, e1, jnp.where(row == 1, e2, 0))
    rw_ref[...] = jnp.where(row == 0, wg * p1 / den, jnp.where(row == 1, wg * p2 / den, 0.0))


def _merge_b(x, m, wo, gn, wr, br, h2_all, row0, tm):
    n, d = x.shape
    blk0 = row0 // tm
    return pl.pallas_call(
        _merge_b_kernel,
        grid=(n // tm,),
        in_specs=[
            pl.BlockSpec((tm, d), lambda i: (i, 0)),
            pl.BlockSpec((tm, d), lambda i: (i, 0)),
            pl.BlockSpec((d, d), lambda i: (0, 0)),
            pl.BlockSpec((1, d), lambda i: (0, 0)),
            pl.BlockSpec((LANES, d), lambda i: (0, 0)),
            pl.BlockSpec((LANES, 1), lambda i: (0, 0)),
            pl.BlockSpec(memory_space=pl.ANY),
        ],
        out_specs=[
            pl.BlockSpec((tm, d), lambda i: (i, 0)),
            pl.BlockSpec((tm, d), lambda i: (blk0 + i, 0)),
            pl.BlockSpec((8, tm), lambda i: (0, i)),
            pl.BlockSpec((8, tm), lambda i: (0, i)),
        ],
        out_shape=[
            jax.ShapeDtypeStruct((n, d), F32),
            jax.ShapeDtypeStruct(h2_all.shape, F32),
            jax.ShapeDtypeStruct((8, n), I32),
            jax.ShapeDtypeStruct((8, n), F32),
        ],
        input_output_aliases={6: 1},
        compiler_params=_params(("parallel",)),
        name="merge_out_router",
    )(x, m, wo, gn, wr, br, h2_all)


_HALO = 32


def _conv_prompt_kernel(u_ref, prev_ref, w_ref, b_ref, g_ref, beta_ref, o_ref, buf_ref, acc_ref, *, tt):
    i = pl.program_id(1)
    prev = prev_ref[pl.ds(tt - _HALO, _HALO), :]
    buf_ref[pl.ds(0, _HALO), :] = jnp.where(i > 0, prev, 0.0)
    buf_ref[pl.ds(_HALO, tt), :] = u_ref[...]
    off = _HALO - (CONV_W - 1)
    for cc in range(D_CONV // LANES):
        cols = pl.ds(cc * LANES, LANES)
        acc = jnp.zeros((tt, LANES), F32)
        for k in range(CONV_W):
            acc = acc + buf_ref[pl.ds(off + k, tt), cols] * w_ref[pl.ds(k, 1), cols]
        acc_ref[:, cols] = acc + b_ref[:, cols]
    c = acc_ref[...]
    mu = jnp.mean(c, axis=-1, keepdims=True)
    var = jnp.mean(jnp.square(c - mu), axis=-1, keepdims=True)
    y = (c - mu) * lax.rsqrt(var + EPS) * g_ref[...] + beta_ref[...]
    o_ref[...] = _silu(y).astype(o_ref.dtype)


def _conv_prompt(u, w, b, g, beta, batch, seq, tt):
    nt = seq // tt
    vec = pl.BlockSpec((1, D_CONV), lambda bi, i: (0, 0))
    return pl.pallas_call(
        functools.partial(_conv_prompt_kernel, tt=tt),
        grid=(batch, nt),
        in_specs=[
            pl.BlockSpec((tt, D_CONV), lambda bi, i: (bi * nt + i, 0)),
            pl.BlockSpec((tt, D_CONV), lambda bi, i: (bi * nt + jnp.maximum(i - 1, 0), 0)),
            pl.BlockSpec((CONV_W, D_CONV), lambda bi, i: (0, 0)),
            vec, vec, vec,
        ],
        out_specs=pl.BlockSpec((tt, D_CONV), lambda bi, i: (bi * nt + i, 0)),
        out_shape=jax.ShapeDtypeStruct((batch * seq, D_CONV), BF16),
        scratch_shapes=[pltpu.VMEM((tt + _HALO, D_CONV), F32), pltpu.VMEM((tt, D_CONV), F32)],
        compiler_params=_params(("parallel", "parallel")),
        name="conv_prompt",
    )(u, u, w, b, g, beta)


def _conv_sample_kernel(st_ref, u_ref, w_ref, b_ref, g_ref, beta_ref, c_ref, new_ref):
    hist = CONV_W - 1
    u = u_ref[...]
    acc = u * w_ref[pl.ds(hist, 1), :] + b_ref[...]
    for t in range(hist):
        acc = acc + st_ref[t] * w_ref[pl.ds(t, 1), :]
    mu = jnp.mean(acc, axis=-1, keepdims=True)
    var = jnp.mean(jnp.square(acc - mu), axis=-1, keepdims=True)
    y = (acc - mu) * lax.rsqrt(var + EPS) * g_ref[...] + beta_ref[...]
    c_ref[...] = _silu(y).astype(c_ref.dtype)
    for t in range(hist - 1):
        new_ref[t] = st_ref[t + 1]
    new_ref[hist - 1] = u


def _conv_sample(state_t, u, w, b, g, beta):
    hist, bs, _ = state_t.shape
    return pl.pallas_call(
        _conv_sample_kernel,
        out_shape=[jax.ShapeDtypeStruct((bs, D_CONV), BF16), jax.ShapeDtypeStruct((hist, bs, D_CONV), F32)],
        compiler_params=pltpu.CompilerParams(vmem_limit_bytes=48 * MIB),
        name="conv_sample",
    )(state_t, u, w, b, g, beta)


def _bias_lookup(rb_ref, head, dist, valid):
    n = jnp.maximum(dist, 0)
    max_exact = N_BUCKETS // 2
    nf = jnp.maximum(n, 1).astype(F32)
    large = max_exact + (jnp.log(nf / max_exact) / math.log(MAX_DIST / max_exact)
                         * (N_BUCKETS - max_exact)).astype(I32)
    bucket = jnp.where(n < max_exact, n, jnp.minimum(large, N_BUCKETS - 1))
    val = jnp.zeros(dist.shape, F32)
    for k in range(N_BUCKETS):
        val = jnp.where(bucket == k, rb_ref[head, k], val)
    return jnp.where(valid, val, NEG)


def _bias_prompt_kernel(rb_ref, cmp_ref, win_ref, sel_ref, *, nqt, n_chunk):
    g = pl.program_id(0)
    row = lax.broadcasted_iota(I32, (QT, LANES), 0)
    col = lax.broadcasted_iota(I32, (QT, LANES), 1)
    for hh in range(HPG):
        head = g * HPG + hh
        rows = pl.ds(hh * QT, QT)

        def cmp_tile(qt, carry):
            for cb in range(n_chunk // LANES):
                dist = qt * QT + row - (col + cb * LANES) * CMP_STRIDE - (CMP_LEN - 1)
                cmp_ref[0, qt, rows, pl.ds(cb * LANES, LANES)] = _bias_lookup(rb_ref, head, dist, dist >= 0)
            return carry

        lax.fori_loop(0, nqt, cmp_tile, 0)
        for jb in range((WINDOW + QT) // LANES):
            dist = row - (col + jb * LANES) + WINDOW
            win_ref[0, rows, pl.ds(jb * LANES, LANES)] = _bias_lookup(
                rb_ref, head, dist, (dist >= 0) & (dist <= WINDOW))
        for kind in range(3):
            for jb in range(2):
                dist = kind * QT + row - (col + jb * LANES)
                sel_ref[0, kind, rows, pl.ds(jb * LANES, LANES)] = _bias_lookup(rb_ref, head, dist, dist >= 0)
        far = jnp.full((QT, LANES), 2 * MAX_DIST, I32)
        for jb in range(2):
            sel_ref[0, 3, rows, pl.ds(jb * LANES, LANES)] = _bias_lookup(rb_ref, head, far, far >= 0)


def _bias_prompt(rb_t, nqt, n_chunk):
    return pl.pallas_call(
        functools.partial(_bias_prompt_kernel, nqt=nqt, n_chunk=n_chunk),
        grid=(N_KV,),
        in_specs=[pl.BlockSpec(memory_space=pltpu.SMEM)],
        out_specs=[
            pl.BlockSpec((1, nqt, GROUP_ROWS, n_chunk), lambda g: (g, 0, 0, 0)),
            pl.BlockSpec((1, GROUP_ROWS, WINDOW + QT), lambda g: (g, 0, 0)),
            pl.BlockSpec((1, 4, GROUP_ROWS, 2 * LANES), lambda g: (g, 0, 0, 0)),
        ],
        out_shape=[
            jax.ShapeDtypeStruct((N_KV, nqt, GROUP_ROWS, n_chunk), F32),
            jax.ShapeDtypeStruct((N_KV, GROUP_ROWS, WINDOW + QT), F32),
            jax.ShapeDtypeStruct((N_KV, 4, GROUP_ROWS, 2 * LANES), F32),
        ],
        compiler_params=_params(("parallel",)),
        name="bias_maps_prompt",
    )(rb_t)


def _bias_sample_kernel(rb_ref, cmp_ref, win_ref, sel_ref, *, past, n_chunk, n_pages):
    col = lax.broadcasted_iota(I32, (N_HEADS, LANES), 1)
    rb = rb_ref[...]

    def lookup(dist):
        n = jnp.maximum(dist, 0)
        max_exact = N_BUCKETS // 2
        nf = jnp.maximum(n, 1).astype(F32)
        large = max_exact + (jnp.log(nf / max_exact) / math.log(MAX_DIST / max_exact)
                             * (N_BUCKETS - max_exact)).astype(I32)
        bucket = jnp.where(n < max_exact, n, jnp.minimum(large, N_BUCKETS - 1))
        val = jnp.zeros(dist.shape, F32)
        for k in range(N_BUCKETS):
            val = jnp.where(bucket == k, rb[:, k:k + 1], val)
        return jnp.where(dist >= 0, val, NEG)

    def cmp_block(cb, carry):
        dist = past - (col + cb * LANES) * CMP_STRIDE - (CMP_LEN - 1)
        cmp_ref[cb] = lookup(dist)
        return carry

    lax.fori_loop(0, n_chunk // LANES, cmp_block, 0)

    def win_block(jb, carry):
        win_ref[jb] = lookup(WINDOW - (col + jb * LANES))
        return carry

    lax.fori_loop(0, WINDOW // LANES + 1, win_block, 0)

    def sel_page(p, carry):
        sel_ref[p] = lookup(past - (p * PAGE + col))
        return carry

    lax.fori_loop(0, n_pages + 1, sel_page, 0)


def _bias_sample(rb_t, past, n_chunk, n_pages):
    cmp_b, win_b, sel_b = pl.pallas_call(
        functools.partial(_bias_sample_kernel, past=past, n_chunk=n_chunk, n_pages=n_pages),
        out_shape=[
            jax.ShapeDtypeStruct((n_chunk // LANES, N_HEADS, LANES), F32),
            jax.ShapeDtypeStruct((WINDOW // LANES + 1, N_HEADS, LANES), F32),
            jax.ShapeDtypeStruct((n_pages + 1, N_HEADS, LANES), F32),
        ],
        name="bias_maps_sample",
    )(rb_t)
    flat = lambda a: a.transpose(1, 0, 2).reshape(N_HEADS, -1)
    return flat(cmp_b), flat(win_b), sel_b


_CHUNKS_PER_PAGE = PAGE // CMP_STRIDE


def _compress_kernel(pt_ref, *refs, pp):
    page_refs = refs[:pp + 1]
    w0k_ref, w1k_ref, w0v_ref, w1v_ref, pe_ref, w2k_ref, w2v_ref, o_ref, tr_ref, x_ref = refs[pp + 1:]
    del pt_ref
    nch = pp * _CHUNKS_PER_PAGE
    m = nch + 8
    n_slab = D_KV // LANES
    for k in range(pp + 1):
        for s in range(n_slab):
            tr_ref[s, pl.ds(k * PAGE, PAGE), :] = page_refs[k][pl.ds(s * LANES, LANES), :].T
    for s in range(n_slab):
        is_v = s >= n_slab // 2
        w0 = (w0v_ref if is_v else w0k_ref)[...]
        w1 = (w1v_ref if is_v else w1k_ref)[...]
        w2 = (w2v_ref if is_v else w2k_ref)[...]
        for p in range(CMP_STRIDE):
            x_ref[:, pl.ds(p * LANES, LANES)] = tr_ref.at[s][pl.ds(p, m, stride=CMP_STRIDE), :]
        x = x_ref[...]
        pe0 = pe_ref[pl.ds(2 * int(is_v), 1), :]
        pe1 = pe_ref[pl.ds(2 * int(is_v) + 1, 1), :]
        y0 = _mm((x + pe0).astype(BF16), w0)
        y1 = _mm((x + pe1).astype(BF16), w1)
        pre = y0[0:nch] + y1[1:nch + 1]
        o_ref[0, :, pl.ds(s * LANES, LANES)] = _mm(_gelu_tanh(pre).astype(BF16), w2)


def _compress(pages, page_ids, weights, nb, n_pages, pp, paged):
    w0k, w1k, w0v, w1v, pe, w2k, w2v = weights
    steps = n_pages // pp
    nch = pp * _CHUNKS_PER_PAGE

    def page_spec(k):
        if paged:
            return pl.BlockSpec(
                (None, D_KV, PAGE),
                lambda b, s, pt: (pt[b * n_pages + jnp.minimum(s * pp + k, n_pages - 1)], 0, 0))
        return pl.BlockSpec((None, D_KV, PAGE), lambda b, s, pt: (b, 0, jnp.minimum(s * pp + k, n_pages - 1)))

    full = lambda a: pl.BlockSpec(a.shape, lambda b, s, pt: (0,) * a.ndim)
    grid_spec = pltpu.PrefetchScalarGridSpec(
        num_scalar_prefetch=1,
        grid=(nb, steps),
        in_specs=[page_spec(k) for k in range(pp + 1)] + [full(a) for a in weights],
        out_specs=pl.BlockSpec((1, nch, D_KV), lambda b, s, pt: (b, s, 0)),
        scratch_shapes=[
            pltpu.VMEM((D_KV // LANES, (pp + 1) * PAGE, LANES), F32),
            pltpu.VMEM((nch + 8, CMP_STRIDE * LANES), F32),
        ],
    )
    return pl.pallas_call(
        functools.partial(_compress_kernel, pp=pp),
        grid_spec=grid_spec,
        out_shape=jax.ShapeDtypeStruct((nb, n_pages * _CHUNKS_PER_PAGE, D_KV), F32),
        compiler_params=_params(("parallel", "parallel"), 56),
        name="compress_paged" if paged else "compress_prompt",
    )(page_ids, *([pages] * (pp + 1)), w0k, w1k, w0v, w1v, pe, w2k, w2v)


def _compress_weights(pe_k, w1_k, w2_k, pe_v, w1_v, w2_v):
    eye = jnp.eye(2, dtype=F32)

    def first(w1, j):
        w = w1[j * CMP_STRIDE:(j + 1) * CMP_STRIDE]
        return jnp.einsum("pdf,gh->pgdhf", w, eye).reshape(CMP_STRIDE * LANES, 2 * CMP_HID).astype(BF16)

    def pos(pe, j):
        return jnp.tile(pe[j * CMP_STRIDE:(j + 1) * CMP_STRIDE, None, :], (1, 2, 1)).reshape(1, CMP_STRIDE * LANES)

    def second(w2):
        return jnp.einsum("fd,gh->gfhd", w2, eye).reshape(2 * CMP_HID, 2 * HEAD_DIM).astype(BF16)

    pe = jnp.concatenate([pos(pe_k, 0), pos(pe_k, 1), pos(pe_v, 0), pos(pe_v, 1)], axis=0)
    return (first(w1_k, 0), first(w1_k, 1), first(w1_v, 0), first(w1_v, 1), pe, second(w2_k), second(w2_v))


def _stack_heads(q, g):
    return jnp.concatenate(
        [q[:, (g * HPG + hh) * HEAD_DIM:(g * HPG + hh + 1) * HEAD_DIM] for hh in range(HPG)], axis=0)


def _store_heads(o_ref, o, gate, g, branch):
    for hh in range(HPG):
        h = g * HPG + hh
        col = branch * N_HEADS + h
        o_ref[:, pl.ds(h * HEAD_DIM, HEAD_DIM)] = o[hh * QT:(hh + 1) * QT] * gate[:, col:col + 1]


def _cmp_prompt_kernel(q_ref, kc_ref, map_ref, gate_ref, ov_ref, o_ref, sb_ref, *, n_sel, top):
    qt = pl.program_id(1)
    q = q_ref[...]
    kv = kc_ref[0].astype(BF16)
    gate = gate_ref[...]
    ov = ov_ref[...]
    jrow = lax.broadcasted_iota(I32, (n_sel, QT), 0)
    tpos = qt * QT + lax.broadcasted_iota(I32, (n_sel, QT), 1)
    cur = tpos // SEL_LEN
    forced = (jrow == 0) | (jrow == cur) | (jrow == cur - 1)
    valid = jrow * SEL_LEN <= tpos
    selbias = []
    for g in range(N_KV):
        kg = kv[:, g * HEAD_DIM:(g + 1) * HEAD_DIM]
        vg = kv[:, D_KV // 2 + g * HEAD_DIM:D_KV // 2 + (g + 1) * HEAD_DIM]
        bias = map_ref[g, 0]
        s = _mm_nt(_stack_heads(q, g), kg) + bias
        e = jnp.exp(s - jnp.max(s, axis=-1, keepdims=True))
        p = e / jnp.sum(e, axis=-1, keepdims=True) * (bias > 0.5 * NEG).astype(F32)
        pb = p.astype(BF16)
        _store_heads(o_ref, _mm(pb, vg), gate, g, 0)
        imp = jnp.zeros((n_sel, QT), F32)
        for hh in range(HPG):
            imp = imp + _mm_nt(ov, pb[hh * QT:(hh + 1) * QT])
        score = jnp.where(forced, jnp.inf, jnp.where(valid, imp, -jnp.inf))
        rank = jnp.zeros((n_sel, QT), I32)
        for i in range(n_sel):
            si = score[i:i + 1, :]
            ahead = (si > score) | ((si == score) & (jrow > i))
            rank = rank + ahead.astype(I32)
        selbias.append(jnp.where(rank < top, 0.0, NEG))
    sb_ref[...] = jnp.concatenate(selbias, axis=0).T.astype(sb_ref.dtype)


def _cmp_prompt(q, kcmp, cmap, gates, ov_t, batch, seq):
    nqt = seq // QT
    n_chunk = kcmp.shape[1]
    n_sel = ov_t.shape[0]
    top = min(SEL_TOP, n_sel)
    return pl.pallas_call(
        functools.partial(_cmp_prompt_kernel, n_sel=n_sel, top=top),
        grid=(batch, nqt),
        in_specs=[
            pl.BlockSpec((QT, D_Q), lambda b, i: (b * nqt + i, 0)),
            pl.BlockSpec((1, n_chunk, D_KV), lambda b, i: (b, 0, 0)),
            pl.BlockSpec((N_KV, 1, GROUP_ROWS, n_chunk), lambda b, i: (0, i, 0, 0)),
            pl.BlockSpec((QT, LANES), lambda b, i: (b * nqt + i, 0)),
            pl.BlockSpec(ov_t.shape, lambda b, i: (0, 0)),
        ],
        out_specs=[
            pl.BlockSpec((QT, D_Q), lambda b, i: (b * nqt + i, 0)),
            pl.BlockSpec((QT, N_KV * n_sel), lambda b, i: (b * nqt + i, 0)),
        ],
        out_shape=[
            jax.ShapeDtypeStruct((batch * seq, D_Q), F32),
            jax.ShapeDtypeStruct((batch * seq, N_KV * n_sel), BF16),
        ],
        compiler_params=_params(("parallel", "parallel")),
        name="cmp_attn_prompt",
    )(q, kcmp, cmap, gates, ov_t)


_TK = 2 * LANES


def _sel_prompt_kernel(q_ref, kv_ref, sb_ref, map_ref, gate_ref, o_ref, kt_ref, vt_ref, qa_ref, *, n_sel, seq):
    qt = pl.program_id(1)
    aug = 2 * HEAD_DIM

    @pl.when(qt == 0)
    def _():
        blk = lax.broadcasted_iota(I32, (aug - HEAD_DIM, seq), 0)
        pos = lax.broadcasted_iota(I32, (aug - HEAD_DIM, seq), 1)
        onehot = jnp.where(pos // SEL_LEN == blk, 1.0, 0.0).astype(BF16)
        for g in range(N_KV):
            kt_ref[g, pl.ds(0, HEAD_DIM), :] = kv_ref[0, pl.ds(g * HEAD_DIM, HEAD_DIM), :].astype(BF16)
            kt_ref[g, pl.ds(HEAD_DIM, aug - HEAD_DIM), :] = onehot
            vt_ref[g] = kv_ref[0, pl.ds(D_KV // 2 + g * HEAD_DIM, HEAD_DIM), :].astype(BF16)

    q = q_ref[...].astype(F32)
    gate = gate_ref[...]
    sb = sb_ref[...].astype(F32)
    diag = qt // 2
    n_tiles = diag + 1
    pad = jnp.zeros((QT, aug - HEAD_DIM - n_sel), F32)
    for g in range(N_KV):
        sbg = sb[:, g * n_sel:(g + 1) * n_sel]
        qa_ref[g] = jnp.concatenate(
            [jnp.concatenate([q[:, (g * HPG + hh) * HEAD_DIM:(g * HPG + hh + 1) * HEAD_DIM], sbg, pad], axis=1)
             for hh in range(HPG)], axis=0).astype(BF16)

    def tile(kt, carry):
        start = pl.multiple_of(kt * _TK, _TK)
        kind = jnp.where(kt == diag, qt % 2, jnp.where((kt == diag - 1) & (qt % 2 == 0), 2, 3))
        out = []
        for g in range(N_KV):
            m_i, l_i, acc = carry[g]
            s = _mm(qa_ref[g], kt_ref[g, :, pl.ds(start, _TK)]) + map_ref[g, kind]
            m_new = jnp.maximum(m_i, jnp.max(s, axis=-1, keepdims=True))
            alpha = jnp.exp(m_i - m_new)
            p = jnp.exp(s - m_new)
            l_new = alpha * l_i + jnp.sum(p, axis=-1, keepdims=True)
            acc = alpha * acc + _mm_nt(p.astype(BF16), vt_ref[g, :, pl.ds(start, _TK)])
            out.append((m_new, l_new, acc))
        return tuple(out)

    init = tuple((jnp.full((GROUP_ROWS, 1), -jnp.inf, F32), jnp.zeros((GROUP_ROWS, 1), F32),
                  jnp.zeros((GROUP_ROWS, HEAD_DIM), F32)) for _ in range(N_KV))
    final = lax.fori_loop(0, n_tiles, tile, init)
    for g in range(N_KV):
        _, l_f, acc = final[g]
        _store_heads(o_ref, acc / l_f, gate, g, 1)


def _sel_prompt(q, kv_t, selbias, smap, gates, batch, seq):
    nqt = seq // QT
    n_sel = selbias.shape[1] // N_KV
    return pl.pallas_call(
        functools.partial(_sel_prompt_kernel, n_sel=n_sel, seq=seq),
        grid=(batch, nqt),
        in_specs=[
            pl.BlockSpec((QT, D_Q), lambda b, i: (b * nqt + i, 0)),
            pl.BlockSpec((1, D_KV, seq), lambda b, i: (b, 0, 0)),
            pl.BlockSpec((QT, N_KV * n_sel), lambda b, i: (b * nqt + i, 0)),
            pl.BlockSpec(smap.shape, lambda b, i: (0, 0, 0, 0)),
            pl.BlockSpec((QT, LANES), lambda b, i: (b * nqt + i, 0)),
        ],
        out_specs=pl.BlockSpec((QT, D_Q), lambda b, i: (b * nqt + i, 0)),
        out_shape=jax.ShapeDtypeStruct((batch * seq, D_Q), F32),
        scratch_shapes=[pltpu.VMEM((N_KV, 2 * HEAD_DIM, seq), BF16), pltpu.VMEM((N_KV, HEAD_DIM, seq), BF16),
                        pltpu.VMEM((N_KV, GROUP_ROWS, 2 * HEAD_DIM), BF16)],
        compiler_params=_params(("arbitrary", "arbitrary"), 56),
        name="sel_attn_prompt",
    )(q, kv_t, selbias, smap, gates)


_WIN_TILES = WINDOW // QT + 1


def _win_prompt_kernel(q_ref, *refs):
    kv_refs = refs[:_WIN_TILES]
    map_ref, gate_ref, o_ref = refs[_WIN_TILES:]
    qt = pl.program_id(1)
    q = q_ref[...]
    gate = gate_ref[...]
    span = _WIN_TILES * QT
    col = lax.broadcasted_iota(I32, (GROUP_ROWS, span), 1)
    in_seq = col >= (_WIN_TILES - 1 - qt) * QT
    for g in range(N_KV):
        kt = jnp.concatenate([r[0, pl.ds(g * HEAD_DIM, HEAD_DIM), :] for r in kv_refs], axis=1).astype(BF16)
        vt = jnp.concatenate([r[0, pl.ds(D_KV // 2 + g * HEAD_DIM, HEAD_DIM), :] for r in kv_refs],
                             axis=1).astype(BF16)
        s = _mm(_stack_heads(q, g), kt) + map_ref[g]
        s = jnp.where(in_seq, s, NEG)
        e = jnp.exp(s - jnp.max(s, axis=-1, keepdims=True))
        p = e / jnp.sum(e, axis=-1, keepdims=True)
        _store_heads(o_ref, _mm_nt(p.astype(BF16), vt), gate, g, 2)


def _win_prompt(q, kv_t, wmap, gates, batch, seq):
    nqt = seq // QT

    def kv_spec(k):
        return pl.BlockSpec((1, D_KV, QT), lambda b, i: (b, 0, jnp.maximum(i - (_WIN_TILES - 1) + k, 0)))

    return pl.pallas_call(
        _win_prompt_kernel,
        grid=(batch, nqt),
        in_specs=[pl.BlockSpec((QT, D_Q), lambda b, i: (b * nqt + i, 0))]
        + [kv_spec(k) for k in range(_WIN_TILES)]
        + [pl.BlockSpec(wmap.shape, lambda b, i: (0, 0, 0)),
           pl.BlockSpec((QT, LANES), lambda b, i: (b * nqt + i, 0))],
        out_specs=pl.BlockSpec((QT, D_Q), lambda b, i: (b * nqt + i, 0)),
        out_shape=jax.ShapeDtypeStruct((batch * seq, D_Q), F32),
        compiler_params=_params(("parallel", "parallel")),
        name="win_attn_prompt",
    )(q, *([kv_t] * _WIN_TILES), wmap, gates)


def _group_rows(x_by_group):
    row = lax.broadcasted_iota(I32, x_by_group[0].shape, 0)
    out = x_by_group[0]
    for g in range(1, N_KV):
        out = jnp.where(row // HPG == g, x_by_group[g], out)
    return out


def _cmp_sample_kernel(q_ref, kc_ref, bias_ref, ov_ref, gate_ref, o_ref, idx_ref, *, n_sel, past):
    q = q_ref[0]
    kv = kc_ref[0].astype(BF16)
    bias = bias_ref[...]
    half = D_KV // 2
    s = _group_rows([_mm_nt(q, kv[:, g * HEAD_DIM:(g + 1) * HEAD_DIM]) for g in range(N_KV)]) + bias
    e = jnp.exp(s - jnp.max(s, axis=-1, keepdims=True))
    p = e / jnp.sum(e, axis=-1, keepdims=True) * (bias > 0.5 * NEG).astype(F32)
    pb = p.astype(BF16)
    o = _group_rows([_mm(pb, kv[:, half + g * HEAD_DIM:half + (g + 1) * HEAD_DIM]) for g in range(N_KV)])
    o_ref[0] = o * gate_ref[0]
    imp_h = _mm(pb, ov_ref[...])
    nsp = imp_h.shape[1]
    imp = jnp.concatenate(
        [imp_h[g * HPG:g * HPG + 1] + imp_h[g * HPG + 1:g * HPG + 2] + imp_h[g * HPG + 2:g * HPG + 3]
         + imp_h[g * HPG + 3:g * HPG + 4] for g in range(N_KV)] + [jnp.zeros((8 - N_KV, nsp), F32)], axis=0)
    j = lax.broadcasted_iota(I32, (8, nsp), 1)
    cur = past // SEL_LEN
    forced = (j == 0) | (j == cur) | (j == cur - 1)
    valid = (j * SEL_LEN <= past) & (j < n_sel)
    score = jnp.where(valid, jnp.where(forced, jnp.inf, imp), -jnp.inf)
    slot = lax.broadcasted_iota(I32, (8, LANES), 1)
    res = jnp.zeros((8, LANES), I32)
    for k in range(min(SEL_TOP, n_sel)):
        best = jnp.max(score, axis=-1, keepdims=True)
        pick = jnp.min(jnp.where(score == best, j, nsp), axis=-1, keepdims=True)
        res = jnp.where(slot == k, pick, res)
        score = jnp.where(j == pick, -jnp.inf, score)
    idx_ref[0] = res


def _cmp_sample(q3, kcmp, bias, ov, gate3, n_sel, past):
    bs, n_chunk, _ = kcmp.shape
    return pl.pallas_call(
        functools.partial(_cmp_sample_kernel, n_sel=n_sel, past=past),
        grid=(bs,),
        in_specs=[
            pl.BlockSpec((1, N_HEADS, HEAD_DIM), lambda b: (b, 0, 0)),
            pl.BlockSpec((1, n_chunk, D_KV), lambda b: (b, 0, 0)),
            pl.BlockSpec(bias.shape, lambda b: (0, 0)),
            pl.BlockSpec(ov.shape, lambda b: (0, 0)),
            pl.BlockSpec((1, N_HEADS, HEAD_DIM), lambda b: (b, 0, 0)),
        ],
        out_specs=[
            pl.BlockSpec((1, N_HEADS, HEAD_DIM), lambda b: (b, 0, 0)),
            pl.BlockSpec((1, 8, LANES), lambda b: (b, 0, 0)),
        ],
        out_shape=[
            jax.ShapeDtypeStruct((bs, N_HEADS, HEAD_DIM), F32),
            jax.ShapeDtypeStruct((bs, 8, LANES), I32),
        ],
        compiler_params=_params(("parallel",)),
        name="cmp_attn_sample",
    )(q3, kcmp, bias, ov, gate3)


def _sel_sample_kernel(blk_ref, phys_ref, q_ref, new_ref, bias_ref, gate_ref, *refs, top, n_pages):
    k_refs = refs[:top]
    v_refs = refs[top:2 * top]
    o_ref = refs[2 * top]
    del phys_ref
    b = pl.program_id(0)
    g = pl.program_id(1)
    q = q_ref[0, 0]
    lane = lax.broadcasted_iota(I32, (8, PAGE), 1)
    scores = []
    for k in range(top):
        blk = blk_ref[(b * N_KV + g) * top + k]
        page = jnp.minimum(blk // 2, n_pages)
        s = _mm(q, k_refs[k][...].astype(BF16)) + bias_ref[page]
        ok = (lane // SEL_LEN == blk % 2) & (blk < 2 * n_pages)
        scores.append(jnp.where(ok, s, NEG))
    k_new = new_ref[0, 0, 0:1, :].astype(BF16).astype(F32)
    v_new = new_ref[0, 0, 1:2, :].astype(BF16).astype(F32)
    s_new = jnp.sum(q.astype(F32) * k_new, axis=-1, keepdims=True) + bias_ref[n_pages]
    scores.append(s_new)
    s_all = jnp.concatenate(scores, axis=1)
    e = jnp.exp(s_all - jnp.max(s_all, axis=-1, keepdims=True))
    p = e / jnp.sum(e, axis=-1, keepdims=True)
    o = p[:, top * PAGE:top * PAGE + 1].astype(BF16).astype(F32) * v_new
    for k in range(top):
        o = o + _mm_nt(p[:, k * PAGE:(k + 1) * PAGE].astype(BF16), v_refs[k][...].astype(BF16))
    o_ref[0, 0] = o * gate_ref[0, 0]


def _sel_sample(blk, phys, q4, new4, bias4, gate4, pool4, top, n_pages):
    bs = q4.shape[0]

    def kv_spec(k, is_v):
        return pl.BlockSpec(
            (None, None, HEAD_DIM, PAGE),
            lambda b, g, blk_r, phys_r: (phys_r[(b * N_KV + g) * top + k], g + N_KV * is_v, 0, 0))

    grp = lambda b, g, blk_r, phys_r: (b, g, 0, 0)
    grid_spec = pltpu.PrefetchScalarGridSpec(
        num_scalar_prefetch=2,
        grid=(bs, N_KV),
        in_specs=[
            pl.BlockSpec((1, 1, 8, HEAD_DIM), grp),
            pl.BlockSpec((1, 1, 8, HEAD_DIM), grp),
            pl.BlockSpec((None, n_pages + 1, 8, LANES), lambda b, g, blk_r, phys_r: (g, 0, 0, 0)),
            pl.BlockSpec((1, 1, 8, HEAD_DIM), grp),
        ] + [kv_spec(k, 0) for k in range(top)] + [kv_spec(k, 1) for k in range(top)],
        out_specs=pl.BlockSpec((1, 1, 8, HEAD_DIM), grp),
    )
    return pl.pallas_call(
        functools.partial(_sel_sample_kernel, top=top, n_pages=n_pages),
        grid_spec=grid_spec,
        out_shape=jax.ShapeDtypeStruct((bs, N_KV, 8, HEAD_DIM), F32),
        compiler_params=_params(("parallel", "parallel")),
        name="sel_attn_sample",
    )(blk, phys, q4, new4, bias4, gate4, *([pool4] * (2 * top)))


def _win_sample_kernel(q_ref, new_ref, st_ref, col_ref, bias_ref, gate_ref, o_ref, nst_ref):
    st = st_ref[0]
    half = D_KV // 2
    for g in range(N_KV):
        q = q_ref[0, g]
        bias = bias_ref[g]
        s_old = _mm(q, st[g * HEAD_DIM:(g + 1) * HEAD_DIM, :].astype(BF16)) + bias[:, 0:WINDOW]
        k_new = new_ref[0, g, 0:1, :].astype(BF16).astype(F32)
        v_new = new_ref[0, g, 1:2, :].astype(BF16).astype(F32)
        s_new = jnp.sum(q.astype(F32) * k_new, axis=-1, keepdims=True) + bias[:, WINDOW:WINDOW + LANES]
        s_all = jnp.concatenate([s_old, s_new], axis=1)
        e = jnp.exp(s_all - jnp.max(s_all, axis=-1, keepdims=True))
        p = e / jnp.sum(e, axis=-1, keepdims=True)
        o = _mm_nt(p[:, 0:WINDOW].astype(BF16), st[half + g * HEAD_DIM:half + (g + 1) * HEAD_DIM, :].astype(BF16))
        o = o + p[:, WINDOW:WINDOW + 1].astype(BF16).astype(F32) * v_new
        o_ref[0, g] = o * gate_ref[0, g]
    rolled = pltpu.roll(st, WINDOW - 1, axis=1)
    lane = lax.broadcasted_iota(I32, (D_KV, LANES), 1)
    nst_ref[0, :, pl.ds(0, WINDOW - LANES)] = rolled[:, 0:WINDOW - LANES]
    nst_ref[0, :, pl.ds(WINDOW - LANES, LANES)] = jnp.where(
        lane == LANES - 1, col_ref[0], rolled[:, WINDOW - LANES:WINDOW])


def _win_sample(q4, new4, state_t, newcol, bias4, gate4):
    bs = q4.shape[0]
    grp = pl.BlockSpec((1, N_KV, 8, HEAD_DIM), lambda b: (b, 0, 0, 0))
    return pl.pallas_call(
        _win_sample_kernel,
        grid=(bs,),
        in_specs=[
            grp, grp,
            pl.BlockSpec((1, D_KV, WINDOW), lambda b: (b, 0, 0)),
            pl.BlockSpec((1, D_KV, LANES), lambda b: (b, 0, 0)),
            pl.BlockSpec(bias4.shape, lambda b: (0, 0, 0)),
            grp,
        ],
        out_specs=[grp, pl.BlockSpec((1, D_KV, WINDOW), lambda b: (b, 0, 0))],
        out_shape=[
            jax.ShapeDtypeStruct((bs, N_KV, 8, HEAD_DIM), F32),
            jax.ShapeDtypeStruct((bs, D_KV, WINDOW), F32),
        ],
        compiler_params=_params(("parallel",)),
        name="win_attn_sample",
    )(q4, new4, state_t, newcol, bias4, gate4)


_MOE_ROWS = 256
_MOE_SLACK = 2 * _MOE_ROWS


def _row_copy(src, src_row, dst, dst_row, sem):
    return pltpu.make_async_copy(src.at[pl.ds(src_row, 1)], dst.at[pl.ds(dst_row, 1)], sem)


def _expert_kernel(be_ref, nu_ref, cur_ref, nxt_ref, h_ref, w_ref, wg_ref, wu_ref, wd_ref, yk_ref,
                   xbuf, ybuf, gsem, ssem, *, n_rows, n_live):
    del be_ref
    i = pl.program_id(0)
    slot = i % 2
    n_used = nu_ref[0]

    def gather(asg_ref, dst_slot):
        def body(r, carry):
            tok = jnp.maximum(asg_ref[0, r], 0) // TOP_K
            _row_copy(h_ref, tok, xbuf.at[dst_slot], r, gsem.at[dst_slot]).start()
            return carry

        lax.fori_loop(0, _MOE_ROWS, body, 0)

    def wait_gather(s):
        def body(r, carry):
            _row_copy(h_ref, 0, xbuf.at[s], 0, gsem.at[s]).wait()
            return carry

        lax.fori_loop(0, _MOE_ROWS, body, 0)

    def scatter(s):
        def body(r, carry):
            a = cur_ref[0, r]
            dst = jnp.where(a >= 0, (a % TOP_K) * n_rows + a // TOP_K, s * n_rows + (n_rows - _MOE_ROWS) + r)
            _row_copy(ybuf.at[s], r, yk_ref, dst, ssem.at[s]).start()
            return carry

        lax.fori_loop(0, _MOE_ROWS, body, 0)

    def wait_scatter(s):
        def body(r, carry):
            _row_copy(ybuf.at[s], 0, yk_ref, 0, ssem.at[s]).wait()
            return carry

        lax.fori_loop(0, _MOE_ROWS, body, 0)

    @pl.when(i == 0)
    def _():
        gather(cur_ref, 0)
        ybuf[1] = jnp.zeros((_MOE_ROWS, ybuf.shape[2]), F32)
        fills = [
            pltpu.make_async_copy(ybuf.at[1, pl.ds(0, min(_MOE_ROWS, n_rows - off))],
                                  yk_ref.at[pl.ds(k * n_rows + off, min(_MOE_ROWS, n_rows - off))], ssem.at[1])
            for k in range(TOP_K) for off in range(n_live, n_rows, _MOE_ROWS)]
        for cp in fills:
            cp.start()
        for cp in fills:
            cp.wait()

    @pl.when(i + 1 < n_used)
    def _():
        gather(nxt_ref, 1 - slot)

    @pl.when(i < n_used)
    def _():
        wait_gather(slot)

        @pl.when(i >= 2)
        def _():
            wait_scatter(slot)

        x = xbuf[slot].astype(BF16)
        a = _silu(_mm(x, wg_ref[...])) * _mm(x, wu_ref[...])
        ybuf[slot] = _mm(a.astype(BF16), wd_ref[...]) * w_ref[...]
        scatter(slot)

        @pl.when(i == n_used - 1)
        def _():
            @pl.when(i >= 1)
            def _():
                wait_scatter(1 - slot)

            wait_scatter(slot)


def _experts(blk_e, n_used, rows_asg, rows_w, h2_all, n_live, w_gate, w_up, w_down):
    n_rows, d = h2_all.shape
    n_blk = blk_e.shape[0]
    asg3 = rows_asg.reshape(n_blk, 1, _MOE_ROWS)
    wspec = lambda a: pl.BlockSpec((None,) + a.shape[1:], lambda i, be, nu: (be[i], 0, 0))
    smem_blk = lambda fn: pl.BlockSpec((None, 1, _MOE_ROWS), fn, memory_space=pltpu.SMEM)
    grid_spec = pltpu.PrefetchScalarGridSpec(
        num_scalar_prefetch=2,
        grid=(n_blk,),
        in_specs=[
            smem_blk(lambda i, be, nu: (i, 0, 0)),
            smem_blk(lambda i, be, nu: (jnp.minimum(i + 1, n_blk - 1), 0, 0)),
            pl.BlockSpec(memory_space=pl.ANY),
            pl.BlockSpec((_MOE_ROWS, 1), lambda i, be, nu: (i, 0)),
            wspec(w_gate), wspec(w_up), wspec(w_down),
        ],
        out_specs=pl.BlockSpec(memory_space=pl.ANY),
        scratch_shapes=[
            pltpu.VMEM((2, _MOE_ROWS, d), F32),
            pltpu.VMEM((2, _MOE_ROWS, d), F32),
            pltpu.SemaphoreType.DMA((2,)),
            pltpu.SemaphoreType.DMA((2,)),
        ],
    )
    return pl.pallas_call(
        functools.partial(_expert_kernel, n_rows=n_rows, n_live=n_live),
        grid_spec=grid_spec,
        out_shape=jax.ShapeDtypeStruct((TOP_K * n_rows, d), F32),
        compiler_params=_params(("arbitrary",), 56),
        name="moe_experts",
    )(blk_e, n_used, asg3, asg3, h2_all, rows_w, w_gate, w_up, w_down)


def _combine_kernel(x_ref, y0_ref, y1_ref, g_ref, o_ref):
    x = x_ref[...] + (y0_ref[...] + y1_ref[...])
    ms = jnp.mean(x * x, axis=-1, keepdims=True)
    o_ref[...] = x * lax.rsqrt(ms + EPS) * g_ref[...]


def _combine(x1, yk, gfin, row0, tm):
    n, d = x1.shape
    yk3 = yk.reshape(TOP_K, -1, d)
    blk0 = row0 // tm
    return pl.pallas_call(
        _combine_kernel,
        grid=(n // tm,),
        in_specs=[
            pl.BlockSpec((tm, d), lambda i: (i, 0)),
            pl.BlockSpec((None, tm, d), lambda i: (0, blk0 + i, 0)),
            pl.BlockSpec((None, tm, d), lambda i: (1, blk0 + i, 0)),
            pl.BlockSpec((1, d), lambda i: (0, 0)),
        ],
        out_specs=pl.BlockSpec((tm, d), lambda i: (i, 0)),
        out_shape=jax.ShapeDtypeStruct((n, d), F32),
        compiler_params=_params(("parallel",)),
        name="moe_combine",
    )(x1, yk3, yk3, gfin)


def _dispatch(eid, w):
    a = eid.shape[0]
    tb = _MOE_ROWS
    order = jnp.argsort(eid).astype(I32)
    counts = jnp.sum((eid[:, None] == jnp.arange(N_EXPERTS)[None, :]).astype(I32), axis=0)
    padded = (counts + tb - 1) // tb * tb
    seg_end = jnp.cumsum(padded)
    seg_start = seg_end - padded
    start = jnp.cumsum(counts) - counts
    n_blk = (a + N_EXPERTS * (tb - 1) + tb - 1) // tb
    blk_e = jnp.minimum(jnp.searchsorted(seg_end, jnp.arange(n_blk) * tb, side="right"), N_EXPERTS - 1)
    slot = jnp.arange(n_blk * tb)
    e_slot = jnp.repeat(blk_e, tb)
    off = slot - seg_start[e_slot]
    valid = off < counts[e_slot]
    asg = order[jnp.clip(start[e_slot] + off, 0, a - 1)]
    rows_asg = jnp.where(valid, asg, -1).astype(I32)
    rows_w = jnp.where(valid, w[asg], 0.0)
    n_used = (seg_end[-1] // tb).astype(I32).reshape(1)
    return rows_asg, rows_w.reshape(-1, 1), blk_e.astype(I32), n_used


def _overlap(n_chunk, n_sel, n_sel_pad):
    cs = jnp.arange(n_chunk)[:, None] * CMP_STRIDE
    ss = jnp.arange(n_sel_pad)[None, :] * SEL_LEN
    hit = (cs < ss + SEL_LEN) & (cs + CMP_LEN > ss) & (jnp.arange(n_sel_pad)[None, :] < n_sel)
    return hit.astype(BF16)


def kernel(x_prompt, x_sample, cache_cmp_kv, cache_sel_kv, state_win_kv, state_conv, page_table, rel_bias,
           norm_mix, w_in, b_in, conv_w, conv_b, conv_ln_g, conv_ln_b, w_conv_out, b_conv_out,
           cmp_pe_k, cmp_w1_k, cmp_w2_k, cmp_pe_v, cmp_w1_v, cmp_w2_v, w_nsa_out, w_out, norm_ffn,
           w_rg, b_rg, w_re, b_re, w_gate, w_up, w_down, norm_final):
    batch, seq, _ = x_prompt.shape
    bs = x_sample.shape[0]
    n_pool = cache_cmp_kv.shape[1]
    n_pages = page_table.shape[1]
    past = n_pages * PAGE
    n_tok = batch * seq
    assert bs + _MOE_ROWS <= _MOE_SLACK and n_tok % bs == 0 and seq % _TK == 0 and bs % 16 == 0
    l = 0

    wt = w_in[l].T.astype(BF16)
    bias_in = b_in[l]
    cuts = [0, D_CONV, 2 * D_CONV, 2 * D_CONV + D_Q, 2 * D_CONV + D_Q + 3 * D_KV]
    cuts += [cuts[-1] + 3 * N_HEADS, cuts[-1] + 3 * N_HEADS + D_MODEL, cuts[-1] + 3 * N_HEADS + 2 * D_MODEL]
    seg = lambda k: (wt[cuts[k]:cuts[k + 1]], bias_in[cuts[k]:cuts[k + 1]][None, :])
    (wa, ba), (wb, bb), (wq, bq), (wkv, bkv), (wzg, bzg), (wga, bga), (wgb, bgb) = [seg(k) for k in range(7)]
    wzg = jnp.pad(wzg, ((0, LANES - 3 * N_HEADS), (0, 0)))
    bzg = jnp.pad(bzg, ((0, 0), (0, LANES - 3 * N_HEADS)))
    wc = w_conv_out[l].astype(BF16)
    wn = w_nsa_out[l].astype(BF16)
    wo = w_out[l].astype(BF16)
    wr = jnp.zeros((LANES, D_MODEL), F32).at[0:N_GROUPS].set(w_rg[l].T).at[8:8 + N_EXPERTS].set(w_re[l].T)
    br = jnp.zeros((LANES,), F32).at[0:N_GROUPS].set(b_rg[l]).at[8:8 + N_EXPERTS].set(b_re[l].reshape(-1))
    wr, br = wr.astype(BF16), br[:, None]
    cw = _compress_weights(cmp_pe_k[l], cmp_w1_k[l], cmp_w2_k[l], cmp_pe_v[l], cmp_w1_v[l], cmp_w2_v[l])
    rb_t = rel_bias.T
    gmix, gffn, gfin = norm_mix[l][None, :], norm_ffn[l][None, :], norm_final[None, :]
    vec = lambda a: a[l][None, :]

    def project(x, tm):
        h = _rmsnorm(x, gmix, tm)
        u = _glu(h, wa, wb, ba, bb, tm, 512)
        q = _proj(h, wq, bq, lambda z: z * HEAD_DIM ** -0.5, BF16, tm, 512, "proj_q")
        gates = _proj(h, wzg, bzg, _sigmoid, F32, tm, LANES, "proj_head_gates")
        return h, u, q, gates

    xp = x_prompt.reshape(n_tok, D_MODEL)
    xs = x_sample.reshape(bs, D_MODEL)
    tm_p = 512
    hp, up, qp, gates_p = project(xp, tm_p)
    hs, us, qs, gates_s = project(xs, bs)
    kv_p = _kv_transposed(hp, wkv, bkv.reshape(-1, 1), batch, seq, tm_p)
    kv_s = _proj(hs, wkv, bkv, lambda z: z, F32, bs, 512, "proj_kv_sample")

    conv_args = (conv_w[l], vec(conv_b), vec(conv_ln_g), vec(conv_ln_b))
    cp = _conv_prompt(up, *conv_args, batch, seq, 256)
    cs, new_conv_s = _conv_sample(state_conv[l].transpose(1, 0, 2), us, *conv_args)

    nqt = seq // QT
    n_chunk_p = seq // CMP_STRIDE
    n_chunk_s = n_pages * _CHUNKS_PER_PAGE
    cmap, wmap, smap = _bias_prompt(rb_t, nqt, n_chunk_p)
    cbias_s, wbias_s, sbias_s = _bias_sample(rb_t, past, n_chunk_s, n_pages)

    ident = jnp.zeros((1,), I32)
    kcmp_p = _compress(kv_p[0], ident, cw, batch, seq // PAGE, seq // PAGE, False)
    feat_major = lambda a: a.transpose(0, 2, 3, 4, 1)
    pool_cmp = feat_major(cache_cmp_kv[l]).reshape(n_pool, D_KV, PAGE)
    pp_s = min(32, n_pages)
    kcmp_s = _compress(pool_cmp, page_table.reshape(-1), cw, bs, n_pages, pp_s, True)

    n_sel_p = seq // SEL_LEN
    o_cmp_p, selbias = _cmp_prompt(qp, kcmp_p, cmap, gates_p, _overlap(n_chunk_p, n_sel_p, n_sel_p).T,
                                   batch, seq)
    n_sel_s = -(-(past + 1) // SEL_LEN)
    n_sel_pad = -(-n_sel_s // LANES) * LANES
    top_s = min(SEL_TOP, n_sel_s)
    head_gate = lambda k: jnp.broadcast_to(
        gates_s[:, k * N_HEADS:(k + 1) * N_HEADS, None], (bs, N_HEADS, HEAD_DIM))
    qs3 = qs.reshape(bs, N_HEADS, HEAD_DIM)
    o_cmp_s, idx_s = _cmp_sample(qs3, kcmp_s, cbias_s, _overlap(n_chunk_s, n_sel_s, n_sel_pad), head_gate(0),
                                 n_sel_s, past)

    o_sel_p = _sel_prompt(qp, kv_p[1], selbias, smap, gates_p, batch, seq)
    by_group = lambda a: jnp.pad(a.reshape(bs, N_KV, HPG, -1), ((0, 0), (0, 0), (0, 8 - HPG), (0, 0)))
    q4 = by_group(qs3)
    blk = idx_s[:, :N_KV, :top_s].reshape(-1)
    phys = jnp.take_along_axis(
        page_table, jnp.minimum(idx_s[:, :N_KV, :top_s] // 2, n_pages - 1).reshape(bs, -1), axis=1).reshape(-1)
    kv_new = lambda k: jnp.pad(
        kv_s[:, k * D_KV:(k + 1) * D_KV].reshape(bs, 2, N_KV, HEAD_DIM).transpose(0, 2, 1, 3),
        ((0, 0), (0, 0), (0, 6), (0, 0)))
    bias_by_group = lambda a: jnp.pad(
        jnp.moveaxis(a, -2, 0).reshape((N_KV, HPG) + a.shape[:-2] + a.shape[-1:]),
        ((0, 0), (0, 8 - HPG)) + ((0, 0),) * (a.ndim - 1))
    sb4 = jnp.moveaxis(bias_by_group(sbias_s), 1, 2)
    pool_sel = feat_major(cache_sel_kv[l]).reshape(n_pool, 2 * N_KV, HEAD_DIM, PAGE)
    o_sel_s = _sel_sample(blk, phys, q4, kv_new(1), sb4, by_group(head_gate(1)), pool_sel, top_s, n_pages)

    o_win_p = _win_prompt(qp, kv_p[2], wmap, gates_p, batch, seq)
    win_t = feat_major(state_win_kv[l]).reshape(bs, D_KV, WINDOW)
    newcol = jnp.broadcast_to(kv_s[:, 2 * D_KV:3 * D_KV, None], (bs, D_KV, LANES))
    o_win_s, new_win_s = _win_sample(q4, kv_new(2), win_t, newcol, bias_by_group(wbias_s), by_group(head_gate(2)))

    m_p = _merge_a(hp, cp, o_cmp_p, o_sel_p, o_win_p, wga, wgb, wc, wn, bga, bgb, vec(b_conv_out), 512, 512)
    h2_all = jnp.zeros((n_tok + _MOE_SLACK, D_MODEL), F32)
    x1_p, h2_all, eid_p, rw_p = _merge_b(xp, m_p, wo, gffn, wr, br, h2_all, 0, 256)
    from_groups = lambda a: a[:, :, :HPG, :].reshape(bs, D_Q)
    m_s = _merge_a(hs, cs, o_cmp_s.reshape(bs, D_Q), from_groups(o_sel_s), from_groups(o_win_s),
                   wga, wgb, wc, wn, bga, bgb, vec(b_conv_out), bs, 512)
    x1_s, h2_all, eid_s, rw_s = _merge_b(xs, m_s, wo, gffn, wr, br, h2_all, n_tok, bs)

    eid = jnp.concatenate([eid_p[:TOP_K].T, eid_s[:TOP_K].T], axis=0).reshape(-1)
    rw = jnp.concatenate([rw_p[:TOP_K].T, rw_s[:TOP_K].T], axis=0).reshape(-1)
    rows_asg, rows_w, blk_e, n_used = _dispatch(eid, rw)
    yk = _experts(blk_e, n_used, rows_asg, rows_w, h2_all, n_tok + bs, w_gate[l].astype(BF16),
                  w_up[l].astype(BF16), w_down[l].astype(BF16))
    y_p = _combine(x1_p, yk, gfin, 0, 256)
    y_s = _combine(x1_s, yk, gfin, n_tok, bs)

    def cache_rows(a_t, rows):
        return a_t.reshape(a_t.shape[0], 2, N_KV, HEAD_DIM, rows).transpose(0, 4, 1, 2, 3)[None]

    new_row = lambda k: kv_s[:, k * D_KV:(k + 1) * D_KV].reshape(1, bs, 1, 2, N_KV, HEAD_DIM)
    hist = CONV_W - 1
    return (
        y_p.reshape(batch, seq, D_MODEL),
        y_s.reshape(bs, 1, D_MODEL),
        cache_rows(kv_p[0], seq),
        new_row(0),
        cache_rows(kv_p[1], seq),
        new_row(1),
        cache_rows(kv_p[2][:, :, seq - WINDOW:], WINDOW),
        cache_rows(new_win_s, WINDOW),
        up.reshape(batch, seq, D_CONV)[None, :, seq - hist:],
        new_conv_s.transpose(1, 0, 2)[None],
    )
```

```python
import functools
import math

import jax
import jax.numpy as jnp
from jax import lax
from jax.experimental import pallas as pl
from jax.experimental.pallas import tpu as pltpu

F32 = jnp.float32
BF16 = jnp.bfloat16
I32 = jnp.int32

D_MODEL = 2048
D_CONV = 1024
CONV_W = 31
N_HEADS = 16
HEAD_DIM = 64
N_KV = 4
HPG = N_HEADS // N_KV
D_Q = N_HEADS * HEAD_DIM
D_KV = 2 * N_KV * HEAD_DIM
CMP_LEN = 32
CMP_STRIDE = 16
CMP_HID = 64
SEL_LEN = 64
SEL_TOP = 16
WINDOW = 512
N_BUCKETS = 32
MAX_DIST = 128
N_GROUPS = 4
EXP_PER_GROUP = 8
N_EXPERTS = N_GROUPS * EXP_PER_GROUP
TOP_K = 2
D_EXPERT = 1024
EPS = 1e-6
NEG = -1e30
PAGE = 128
LANES = 128
QT = 128
GROUP_ROWS = HPG * QT
MIB = 1024 * 1024


def _params(sem, vmem_mib=48):
    return pltpu.CompilerParams(dimension_semantics=sem, vmem_limit_bytes=vmem_mib * MIB)


def _mm_nt(a, b):
    return lax.dot_general(a, b, (((1,), (1,)), ((), ())), preferred_element_type=F32)


def _mm(a, b):
    return jnp.dot(a, b, preferred_element_type=F32)


def _sigmoid(z):
    return 1.0 / (1.0 + jnp.exp(-z))


def _silu(z):
    return z * _sigmoid(z)


def _gelu_tanh(z):
    return 0.5 * z * (1.0 + jnp.tanh(math.sqrt(2.0 / math.pi) * (z + 0.044715 * (z * z * z))))


def _rmsnorm_kernel(x_ref, g_ref, o_ref):
    x = x_ref[...]
    ms = jnp.mean(x * x, axis=-1, keepdims=True)
    o_ref[...] = (x * lax.rsqrt(ms + EPS) * g_ref[...]).astype(o_ref.dtype)


def _rmsnorm(x, g, tm):
    n, d = x.shape
    return pl.pallas_call(
        _rmsnorm_kernel,
        grid=(n // tm,),
        in_specs=[pl.BlockSpec((tm, d), lambda i: (i, 0)), pl.BlockSpec((1, d), lambda i: (0, 0))],
        out_specs=pl.BlockSpec((tm, d), lambda i: (i, 0)),
        out_shape=jax.ShapeDtypeStruct((n, d), BF16),
        compiler_params=_params(("parallel",)),
        name="rmsnorm",
    )(x, g)


def _proj_kernel(h_ref, w_ref, b_ref, o_ref, *, act):
    z = _mm_nt(h_ref[...], w_ref[...]) + b_ref[...]
    o_ref[...] = act(z).astype(o_ref.dtype)


def _proj(h, wt, b, act, out_dtype, tm, tn, name):
    n, k = h.shape
    nout = wt.shape[0]
    return pl.pallas_call(
        functools.partial(_proj_kernel, act=act),
        grid=(n // tm, nout // tn),
        in_specs=[
            pl.BlockSpec((tm, k), lambda i, j: (i, 0)),
            pl.BlockSpec((tn, k), lambda i, j: (j, 0)),
            pl.BlockSpec((1, tn), lambda i, j: (0, j)),
        ],
        out_specs=pl.BlockSpec((tm, tn), lambda i, j: (i, j)),
        out_shape=jax.ShapeDtypeStruct((n, nout), out_dtype),
        compiler_params=_params(("parallel", "parallel")),
        name=name,
    )(h, wt, b)


def _proj_t_kernel(h_ref, w_ref, b_ref, o_ref, *, act):
    z = _mm_nt(w_ref[...], h_ref[...]) + b_ref[...]
    o_ref[...] = act(z).astype(o_ref.dtype)


def _proj_t(h, wt, bcol, act, out_dtype, tm, tn, name):
    n, k = h.shape
    nout = wt.shape[0]
    return pl.pallas_call(
        functools.partial(_proj_t_kernel, act=act),
        grid=(n // tm, nout // tn),
        in_specs=[
            pl.BlockSpec((tm, k), lambda i, j: (i, 0)),
            pl.BlockSpec((tn, k), lambda i, j: (j, 0)),
            pl.BlockSpec((tn, 1), lambda i, j: (j, 0)),
        ],
        out_specs=pl.BlockSpec((tn, tm), lambda i, j: (j, i)),
        out_shape=jax.ShapeDtypeStruct((nout, n), out_dtype),
        compiler_params=_params(("parallel", "parallel")),
        name=name,
    )(h, wt, bcol)


def _glu_kernel(h_ref, wa_ref, wb_ref, ba_ref, bb_ref, o_ref):
    h = h_ref[...]
    a = _mm_nt(h, wa_ref[...]) + ba_ref[...]
    b = _mm_nt(h, wb_ref[...]) + bb_ref[...]
    o_ref[...] = a * _sigmoid(b)


def _glu(h, wat, wbt, ba, bb, tm, tn):
    n, k = h.shape
    nout = wat.shape[0]
    wspec = pl.BlockSpec((tn, k), lambda i, j: (j, 0))
    bspec = pl.BlockSpec((1, tn), lambda i, j: (0, j))
    return pl.pallas_call(
        _glu_kernel,
        grid=(n // tm, nout // tn),
        in_specs=[pl.BlockSpec((tm, k), lambda i, j: (i, 0)), wspec, wspec, bspec, bspec],
        out_specs=pl.BlockSpec((tm, tn), lambda i, j: (i, j)),
        out_shape=jax.ShapeDtypeStruct((n, nout), F32),
        compiler_params=_params(("parallel", "parallel")),
        name="glu",
    )(h, wat, wbt, ba, bb)


def _kvt_kernel(h_ref, w_ref, b_ref, o_ref):
    o_ref[0, 0] = _mm_nt(w_ref[...], h_ref[...]) + b_ref[...]


def _kv_transposed(h, wt, bcol, batch, seq, tm):
    n, k = h.shape
    kinds = wt.shape[0] // D_KV
    nt = seq // tm
    return pl.pallas_call(
        _kvt_kernel,
        grid=(n // tm, kinds),
        in_specs=[
            pl.BlockSpec((tm, k), lambda i, j: (i, 0)),
            pl.BlockSpec((D_KV, k), lambda i, j: (j, 0)),
            pl.BlockSpec((D_KV, 1), lambda i, j: (j, 0)),
        ],
        out_specs=pl.BlockSpec((1, 1, D_KV, tm), lambda i, j: (j, i // nt, 0, i % nt)),
        out_shape=jax.ShapeDtypeStruct((kinds, batch, D_KV, seq), F32),
        compiler_params=_params(("parallel", "parallel")),
        name="kv_transposed",
    )(h, wt, bcol)


def _merge_a_kernel(h_ref, c_ref, o1_ref, o2_ref, o3_ref, wga_ref, wgb_ref, wc_ref, wn_ref,
                    bga_ref, bgb_ref, bc_ref, m_ref):
    h = h_ref[...]
    ga = _sigmoid(_mm_nt(h, wga_ref[...]) + bga_ref[...])
    gb = _sigmoid(_mm_nt(h, wgb_ref[...]) + bgb_ref[...])
    yc = _mm(c_ref[...], wc_ref[...]) + bc_ref[...]
    o = (o1_ref[...] + o2_ref[...] + o3_ref[...]).astype(BF16)
    yn = _mm(o, wn_ref[...])
    m_ref[...] = (ga * yc + gb * yn).astype(m_ref.dtype)


def _merge_a(h, c, o1, o2, o3, wgat, wgbt, wc, wn, bga, bgb, bc, tm, tn):
    n, k = h.shape
    tok = lambda width: pl.BlockSpec((tm, width), lambda i, j: (i, 0))
    wt_spec = pl.BlockSpec((tn, k), lambda i, j: (j, 0))
    w_spec = pl.BlockSpec((c.shape[1], tn), lambda i, j: (0, j))
    b_spec = pl.BlockSpec((1, tn), lambda i, j: (0, j))
    return pl.pallas_call(
        _merge_a_kernel,
        grid=(n // tm, D_MODEL // tn),
        in_specs=[tok(k), tok(D_CONV), tok(D_Q), tok(D_Q), tok(D_Q), wt_spec, wt_spec, w_spec, w_spec,
                  b_spec, b_spec, b_spec],
        out_specs=pl.BlockSpec((tm, tn), lambda i, j: (i, j)),
        out_shape=jax.ShapeDtypeStruct((n, D_MODEL), BF16),
        compiler_params=_params(("parallel", "parallel")),
        name="merge_gates",
    )(h, c, o1, o2, o3, wgat, wgbt, wc, wn, bga, bgb, bc)


def _merge_b_kernel(x_ref, m_ref, wo_ref, gn_ref, wr_ref, br_ref, h2_all_ref, x1_ref, h2_ref, eid_ref, rw_ref,
                    cnt_ref):
    del h2_all_ref
    x1 = x_ref[...] + _mm(m_ref[...], wo_ref[...])
    x1_ref[...] = x1
    ms = jnp.mean(x1 * x1, axis=-1, keepdims=True)
    h2 = x1 * lax.rsqrt(ms + EPS) * gn_ref[...]
    h2_ref[...] = h2
    logits = _mm_nt(wr_ref[...], h2.astype(BF16)) + br_ref[...]
    tm = logits.shape[1]
    row = lax.broadcasted_iota(I32, (8, tm), 0)
    lg = jnp.where(row < N_GROUPS, logits[0:8], -jnp.inf)
    gmax = jnp.max(lg, axis=0, keepdims=True)
    gidx = jnp.min(jnp.where(lg == gmax, row, 8), axis=0, keepdims=True)
    wg = 1.0 / jnp.sum(jnp.exp(lg - gmax), axis=0, keepdims=True)
    le = jnp.zeros((8, tm), F32)
    for g in range(N_GROUPS):
        le = jnp.where(gidx == g, logits[8 + 8 * g:16 + 8 * g], le)
    ee = jnp.exp(le - jnp.max(le, axis=0, keepdims=True))
    pz = ee / jnp.sum(ee, axis=0, keepdims=True)
    p1 = jnp.max(pz, axis=0, keepdims=True)
    i1 = jnp.min(jnp.where(pz == p1, row, 8), axis=0, keepdims=True)
    pz2 = jnp.where(row == i1, -1.0, pz)
    p2 = jnp.max(pz2, axis=0, keepdims=True)
    i2 = jnp.min(jnp.where(pz2 == p2, row, 8), axis=0, keepdims=True)
    den = p1 + p2
    e1 = gidx * EXP_PER_GROUP + i1
    e2 = gidx * EXP_PER_GROUP + i2
    eid_ref[...] = jnp.where(row == 0, e1, jnp.where(row == 1, e2, 0))
    rw_ref[...] = jnp.where(row == 0, wg * p1 / den, jnp.where(row == 1, wg * p2 / den, 0.0))
    erow = lax.broadcasted_iota(I32, (N_EXPERTS, tm), 0)
    hits = (erow == e1).astype(F32) + (erow == e2).astype(F32)
    cnt_ref[...] = jnp.broadcast_to(jnp.sum(hits, axis=1, keepdims=True), (N_EXPERTS, LANES)).astype(I32)


def _merge_b(x, m, wo, gn, wr, br, h2_all, row0, tm):
    n, d = x.shape
    blk0 = row0 // tm
    return pl.pallas_call(
        _merge_b_kernel,
        grid=(n // tm,),
        in_specs=[
            pl.BlockSpec((tm, d), lambda i: (i, 0)),
            pl.BlockSpec((tm, d), lambda i: (i, 0)),
            pl.BlockSpec((d, d), lambda i: (0, 0)),
            pl.BlockSpec((1, d), lambda i: (0, 0)),
            pl.BlockSpec((LANES, d), lambda i: (0, 0)),
            pl.BlockSpec((LANES, 1), lambda i: (0, 0)),
            pl.BlockSpec(memory_space=pl.ANY),
        ],
        out_specs=[
            pl.BlockSpec((tm, d), lambda i: (i, 0)),
            pl.BlockSpec((tm, d), lambda i: (blk0 + i, 0)),
            pl.BlockSpec((8, tm), lambda i: (0, i)),
            pl.BlockSpec((8, tm), lambda i: (0, i)),
            pl.BlockSpec((None, N_EXPERTS, LANES), lambda i: (i, 0, 0)),
        ],
        out_shape=[
            jax.ShapeDtypeStruct((n, d), F32),
            jax.ShapeDtypeStruct(h2_all.shape, F32),
            jax.ShapeDtypeStruct((8, n), I32),
            jax.ShapeDtypeStruct((8, n), F32),
            jax.ShapeDtypeStruct((n // tm, N_EXPERTS, LANES), I32),
        ],
        input_output_aliases={6: 1},
        compiler_params=_params(("parallel",)),
        name="merge_out_router",
    )(x, m, wo, gn, wr, br, h2_all)


_HALO = 32


def _conv_prompt_kernel(u_ref, prev_ref, w_ref, b_ref, g_ref, beta_ref, o_ref, buf_ref, acc_ref, *, tt):
    i = pl.program_id(1)
    prev = prev_ref[pl.ds(tt - _HALO, _HALO), :]
    buf_ref[pl.ds(0, _HALO), :] = jnp.where(i > 0, prev, 0.0)
    buf_ref[pl.ds(_HALO, tt), :] = u_ref[...]
    off = _HALO - (CONV_W - 1)
    for cc in range(D_CONV // LANES):
        cols = pl.ds(cc * LANES, LANES)
        acc = jnp.zeros((tt, LANES), F32)
        for k in range(CONV_W):
            acc = acc + buf_ref[pl.ds(off + k, tt), cols] * w_ref[pl.ds(k, 1), cols]
        acc_ref[:, cols] = acc + b_ref[:, cols]
    c = acc_ref[...]
    mu = jnp.mean(c, axis=-1, keepdims=True)
    var = jnp.mean(jnp.square(c - mu), axis=-1, keepdims=True)
    y = (c - mu) * lax.rsqrt(var + EPS) * g_ref[...] + beta_ref[...]
    o_ref[...] = _silu(y).astype(o_ref.dtype)


def _conv_prompt(u, w, b, g, beta, batch, seq, tt):
    nt = seq // tt
    vec = pl.BlockSpec((1, D_CONV), lambda bi, i: (0, 0))
    return pl.pallas_call(
        functools.partial(_conv_prompt_kernel, tt=tt),
        grid=(batch, nt),
        in_specs=[
            pl.BlockSpec((tt, D_CONV), lambda bi, i: (bi * nt + i, 0)),
            pl.BlockSpec((tt, D_CONV), lambda bi, i: (bi * nt + jnp.maximum(i - 1, 0), 0)),
            pl.BlockSpec((CONV_W, D_CONV), lambda bi, i: (0, 0)),
            vec, vec, vec,
        ],
        out_specs=pl.BlockSpec((tt, D_CONV), lambda bi, i: (bi * nt + i, 0)),
        out_shape=jax.ShapeDtypeStruct((batch * seq, D_CONV), BF16),
        scratch_shapes=[pltpu.VMEM((tt + _HALO, D_CONV), F32), pltpu.VMEM((tt, D_CONV), F32)],
        compiler_params=_params(("parallel", "parallel")),
        name="conv_prompt",
    )(u, u, w, b, g, beta)


def _conv_sample_kernel(st_ref, u_ref, w_ref, b_ref, g_ref, beta_ref, c_ref, new_ref):
    hist = CONV_W - 1
    u = u_ref[...]
    acc = u * w_ref[pl.ds(hist, 1), :] + b_ref[...]
    for t in range(hist):
        acc = acc + st_ref[t] * w_ref[pl.ds(t, 1), :]
    mu = jnp.mean(acc, axis=-1, keepdims=True)
    var = jnp.mean(jnp.square(acc - mu), axis=-1, keepdims=True)
    y = (acc - mu) * lax.rsqrt(var + EPS) * g_ref[...] + beta_ref[...]
    c_ref[...] = _silu(y).astype(c_ref.dtype)
    for t in range(hist - 1):
        new_ref[t] = st_ref[t + 1]
    new_ref[hist - 1] = u


def _conv_sample(state_t, u, w, b, g, beta):
    hist, bs, _ = state_t.shape
    return pl.pallas_call(
        _conv_sample_kernel,
        out_shape=[jax.ShapeDtypeStruct((bs, D_CONV), BF16), jax.ShapeDtypeStruct((hist, bs, D_CONV), F32)],
        compiler_params=pltpu.CompilerParams(vmem_limit_bytes=48 * MIB),
        name="conv_sample",
    )(state_t, u, w, b, g, beta)


def _bias_lookup(rb_ref, head, dist, valid):
    n = jnp.maximum(dist, 0)
    max_exact = N_BUCKETS // 2
    nf = jnp.maximum(n, 1).astype(F32)
    large = max_exact + (jnp.log(nf / max_exact) / math.log(MAX_DIST / max_exact)
                         * (N_BUCKETS - max_exact)).astype(I32)
    bucket = jnp.where(n < max_exact, n, jnp.minimum(large, N_BUCKETS - 1))
    val = jnp.zeros(dist.shape, F32)
    for k in range(N_BUCKETS):
        val = jnp.where(bucket == k, rb_ref[head, k], val)
    return jnp.where(valid, val, NEG)


def _bias_prompt_kernel(rb_ref, cmp_ref, win_ref, sel_ref, *, nqt, n_chunk):
    g = pl.program_id(0)
    key = lax.broadcasted_iota(I32, (LANES, QT), 0)
    qry = lax.broadcasted_iota(I32, (LANES, QT), 1)
    for hh in range(HPG):
        head = g * HPG + hh
        cols = pl.ds(hh * QT, QT)

        def cmp_tile(qt, carry):
            for cb in range(n_chunk // LANES):
                dist = qt * QT + qry - (key + cb * LANES) * CMP_STRIDE - (CMP_LEN - 1)
                cmp_ref[0, qt, pl.ds(cb * LANES, LANES), cols] = _bias_lookup(rb_ref, head, dist, dist >= 0)
            return carry

        lax.fori_loop(0, nqt, cmp_tile, 0)
        for jb in range((WINDOW + QT) // LANES):
            dist = qry - (key + jb * LANES) + WINDOW
            win_ref[0, pl.ds(jb * LANES, LANES), cols] = _bias_lookup(
                rb_ref, head, dist, (dist >= 0) & (dist <= WINDOW))
        for kind in range(3):
            for jb in range(2):
                dist = kind * QT + qry - (key + jb * LANES)
                sel_ref[0, kind, pl.ds(jb * LANES, LANES), cols] = _bias_lookup(rb_ref, head, dist, dist >= 0)
        far = jnp.full((LANES, QT), 2 * MAX_DIST, I32)
        for jb in range(2):
            sel_ref[0, 3, pl.ds(jb * LANES, LANES), cols] = _bias_lookup(rb_ref, head, far, far >= 0)


def _bias_prompt(rb_t, nqt, n_chunk):
    return pl.pallas_call(
        functools.partial(_bias_prompt_kernel, nqt=nqt, n_chunk=n_chunk),
        grid=(N_KV,),
        in_specs=[pl.BlockSpec(memory_space=pltpu.SMEM)],
        out_specs=[
            pl.BlockSpec((1, nqt, n_chunk, GROUP_ROWS), lambda g: (g, 0, 0, 0)),
            pl.BlockSpec((1, WINDOW + QT, GROUP_ROWS), lambda g: (g, 0, 0)),
            pl.BlockSpec((1, 4, 2 * LANES, GROUP_ROWS), lambda g: (g, 0, 0, 0)),
        ],
        out_shape=[
            jax.ShapeDtypeStruct((N_KV, nqt, n_chunk, GROUP_ROWS), F32),
            jax.ShapeDtypeStruct((N_KV, WINDOW + QT, GROUP_ROWS), F32),
            jax.ShapeDtypeStruct((N_KV, 4, 2 * LANES, GROUP_ROWS), F32),
        ],
        compiler_params=_params(("parallel",)),
        name="bias_maps_prompt",
    )(rb_t)


def _bias_sample_kernel(rb_ref, cmp_ref, win_ref, sel_ref, *, past, n_chunk, n_pages):
    col = lax.broadcasted_iota(I32, (N_HEADS, LANES), 1)
    rb = rb_ref[...]

    def lookup(dist):
        n = jnp.maximum(dist, 0)
        max_exact = N_BUCKETS // 2
        nf = jnp.maximum(n, 1).astype(F32)
        large = max_exact + (jnp.log(nf / max_exact) / math.log(MAX_DIST / max_exact)
                             * (N_BUCKETS - max_exact)).astype(I32)
        bucket = jnp.where(n < max_exact, n, jnp.minimum(large, N_BUCKETS - 1))
        val = jnp.zeros(dist.shape, F32)
        for k in range(N_BUCKETS):
            val = jnp.where(bucket == k, rb[:, k:k + 1], val)
        return jnp.where(dist >= 0, val, NEG)

    def cmp_block(cb, carry):
        dist = past - (col + cb * LANES) * CMP_STRIDE - (CMP_LEN - 1)
        cmp_ref[cb] = lookup(dist)
        return carry

    lax.fori_loop(0, n_chunk // LANES, cmp_block, 0)

    def win_block(jb, carry):
        win_ref[jb] = lookup(WINDOW - (col + jb * LANES))
        return carry

    lax.fori_loop(0, WINDOW // LANES + 1, win_block, 0)

    def sel_page(p, carry):
        sel_ref[p] = lookup(past - (p * PAGE + col))
        return carry

    lax.fori_loop(0, n_pages + 1, sel_page, 0)


def _bias_sample(rb_t, past, n_chunk, n_pages):
    cmp_b, win_b, sel_b = pl.pallas_call(
        functools.partial(_bias_sample_kernel, past=past, n_chunk=n_chunk, n_pages=n_pages),
        out_shape=[
            jax.ShapeDtypeStruct((n_chunk // LANES, N_HEADS, LANES), F32),
            jax.ShapeDtypeStruct((WINDOW // LANES + 1, N_HEADS, LANES), F32),
            jax.ShapeDtypeStruct((n_pages + 1, N_HEADS, LANES), F32),
        ],
        name="bias_maps_sample",
    )(rb_t)
    flat = lambda a: a.transpose(1, 0, 2).reshape(N_HEADS, -1)
    return flat(cmp_b), flat(win_b), sel_b


_CHUNKS_PER_PAGE = PAGE // CMP_STRIDE


def _compress_kernel(pt_ref, *refs, pp):
    page_refs = refs[:pp + 1]
    w0k_ref, w1k_ref, w0v_ref, w1v_ref, pe_ref, w2k_ref, w2v_ref, o_ref, tr_ref, x_ref = refs[pp + 1:]
    del pt_ref
    nch = pp * _CHUNKS_PER_PAGE
    m = nch + 8
    n_slab = D_KV // LANES
    for k in range(pp + 1):
        for s in range(n_slab):
            tr_ref[s, pl.ds(k * PAGE, PAGE), :] = page_refs[k][pl.ds(s * LANES, LANES), :].T
    for s in range(n_slab):
        is_v = s >= n_slab // 2
        w0 = (w0v_ref if is_v else w0k_ref)[...]
        w1 = (w1v_ref if is_v else w1k_ref)[...]
        w2 = (w2v_ref if is_v else w2k_ref)[...]
        for p in range(CMP_STRIDE):
            x_ref[:, pl.ds(p * LANES, LANES)] = tr_ref.at[s][pl.ds(p, m, stride=CMP_STRIDE), :]
        x = x_ref[...]
        pe0 = pe_ref[pl.ds(2 * int(is_v), 1), :]
        pe1 = pe_ref[pl.ds(2 * int(is_v) + 1, 1), :]
        y0 = _mm((x + pe0).astype(BF16), w0)
        y1 = _mm((x + pe1).astype(BF16), w1)
        pre = y0[0:nch] + y1[1:nch + 1]
        o_ref[0, :, pl.ds(s * LANES, LANES)] = _mm(_gelu_tanh(pre).astype(BF16), w2)


def _compress(pages, page_ids, weights, nb, n_pages, pp, paged):
    w0k, w1k, w0v, w1v, pe, w2k, w2v = weights
    steps = n_pages // pp
    nch = pp * _CHUNKS_PER_PAGE

    def page_spec(k):
        if paged:
            return pl.BlockSpec(
                (None, D_KV, PAGE),
                lambda b, s, pt: (pt[b * n_pages + jnp.minimum(s * pp + k, n_pages - 1)], 0, 0))
        return pl.BlockSpec((None, D_KV, PAGE), lambda b, s, pt: (b, 0, jnp.minimum(s * pp + k, n_pages - 1)))

    full = lambda a: pl.BlockSpec(a.shape, lambda b, s, pt: (0,) * a.ndim)
    grid_spec = pltpu.PrefetchScalarGridSpec(
        num_scalar_prefetch=1,
        grid=(nb, steps),
        in_specs=[page_spec(k) for k in range(pp + 1)] + [full(a) for a in weights],
        out_specs=pl.BlockSpec((1, nch, D_KV), lambda b, s, pt: (b, s, 0)),
        scratch_shapes=[
            pltpu.VMEM((D_KV // LANES, (pp + 1) * PAGE, LANES), F32),
            pltpu.VMEM((nch + 8, CMP_STRIDE * LANES), F32),
        ],
    )
    return pl.pallas_call(
        functools.partial(_compress_kernel, pp=pp),
        grid_spec=grid_spec,
        out_shape=jax.ShapeDtypeStruct((nb, n_pages * _CHUNKS_PER_PAGE, D_KV), F32),
        compiler_params=_params(("parallel", "parallel"), 56),
        name="compress_paged" if paged else "compress_prompt",
    )(page_ids, *([pages] * (pp + 1)), w0k, w1k, w0v, w1v, pe, w2k, w2v)


def _compress_weights(pe_k, w1_k, w2_k, pe_v, w1_v, w2_v):
    eye = jnp.eye(2, dtype=F32)

    def first(w1, j):
        w = w1[j * CMP_STRIDE:(j + 1) * CMP_STRIDE]
        return jnp.einsum("pdf,gh->pgdhf", w, eye).reshape(CMP_STRIDE * LANES, 2 * CMP_HID).astype(BF16)

    def pos(pe, j):
        return jnp.tile(pe[j * CMP_STRIDE:(j + 1) * CMP_STRIDE, None, :], (1, 2, 1)).reshape(1, CMP_STRIDE * LANES)

    def second(w2):
        return jnp.einsum("fd,gh->gfhd", w2, eye).reshape(2 * CMP_HID, 2 * HEAD_DIM).astype(BF16)

    pe = jnp.concatenate([pos(pe_k, 0), pos(pe_k, 1), pos(pe_v, 0), pos(pe_v, 1)], axis=0)
    return (first(w1_k, 0), first(w1_k, 1), first(w1_v, 0), first(w1_v, 1), pe, second(w2_k), second(w2_v))


def _staggered(n, scores, softmax, values):
    s = {0: scores(0)}
    out = []
    for g in range(n):
        if g + 1 < n:
            s[g + 1] = scores(g + 1)
        out.append(values(g, softmax(g, s.pop(g))))
    return out


def _group_queries(qt_ref, g):
    return jnp.concatenate(
        [qt_ref[pl.ds((g * HPG + hh) * HEAD_DIM, HEAD_DIM), :] for hh in range(HPG)], axis=1)


def _store_heads(o_ref, o_t, gate, g, branch):
    lane = lax.broadcasted_iota(I32, (QT, 2 * HEAD_DIM), 1)
    for pair in range(HPG // 2):
        h = g * HPG + 2 * pair
        col = branch * N_HEADS + h
        both = jnp.concatenate([o_t[:, (2 * pair) * QT:(2 * pair + 1) * QT],
                                o_t[:, (2 * pair + 1) * QT:(2 * pair + 2) * QT]], axis=0)
        gates = jnp.where(lane < HEAD_DIM, gate[:, col:col + 1], gate[:, col + 1:col + 2])
        o_ref[:, pl.ds(h * HEAD_DIM, 2 * HEAD_DIM)] = both.T * gates


def _cmp_prompt_kernel(qt_ref, kc_ref, map_ref, gate_ref, ov_ref, o_ref, sb_ref, *, n_sel, top):
    qt = pl.program_id(1)
    kv = kc_ref[0]
    gate = gate_ref[...]
    ov = ov_ref[...]
    half = D_KV // 2
    vt_all = kv[:, half:].T
    jrow = lax.broadcasted_iota(I32, (n_sel, QT), 0)
    tpos = qt * QT + lax.broadcasted_iota(I32, (n_sel, QT), 1)
    cur = tpos // SEL_LEN
    forced = (jrow == 0) | (jrow == cur) | (jrow == cur - 1)
    valid = jrow * SEL_LEN <= tpos
    def scores(g):
        kg = kv[:, g * HEAD_DIM:(g + 1) * HEAD_DIM].astype(BF16)
        return _mm(kg, _group_queries(qt_ref, g)) + map_ref[g, 0]

    def softmax(g, st):
        e = jnp.exp(st - jnp.max(st, axis=0, keepdims=True))
        p = e / jnp.sum(e, axis=0, keepdims=True) * (map_ref[g, 0] > 0.5 * NEG).astype(F32)
        return p.astype(BF16)

    def values(g, pb):
        vgt = vt_all[g * HEAD_DIM:(g + 1) * HEAD_DIM].astype(BF16)
        _store_heads(o_ref, _mm(vgt, pb), gate, g, 0)
        return pb

    probs = _staggered(N_KV, scores, softmax, values)
    for g in range(N_KV):
        pb = probs[g]
        imp = jnp.zeros((n_sel, QT), F32)
        for hh in range(HPG):
            imp = imp + _mm(ov, pb[:, hh * QT:(hh + 1) * QT])
        score = jnp.where(forced, jnp.inf, jnp.where(valid, imp, -jnp.inf))
        rank = jnp.zeros((n_sel, QT), I32)
        for i in range(n_sel):
            si = score[i:i + 1, :]
            ahead = (si > score) | ((si == score) & (jrow > i))
            rank = rank + ahead.astype(I32)
        sb_ref[pl.ds(g * n_sel, n_sel), :] = jnp.where(rank < top, 0.0, NEG).astype(sb_ref.dtype)


def _cmp_prompt(q_t, kcmp, cmap, gates, ov_t, batch, seq):
    nqt = seq // QT
    n_chunk = kcmp.shape[1]
    n_sel = ov_t.shape[0]
    top = min(SEL_TOP, n_sel)
    return pl.pallas_call(
        functools.partial(_cmp_prompt_kernel, n_sel=n_sel, top=top),
        grid=(batch, nqt),
        in_specs=[
            pl.BlockSpec((D_Q, QT), lambda b, i: (0, b * nqt + i)),
            pl.BlockSpec((1, n_chunk, D_KV), lambda b, i: (b, 0, 0)),
            pl.BlockSpec((N_KV, 1, n_chunk, GROUP_ROWS), lambda b, i: (0, i, 0, 0)),
            pl.BlockSpec((QT, LANES), lambda b, i: (b * nqt + i, 0)),
            pl.BlockSpec(ov_t.shape, lambda b, i: (0, 0)),
        ],
        out_specs=[
            pl.BlockSpec((QT, D_Q), lambda b, i: (b * nqt + i, 0)),
            pl.BlockSpec((None, N_KV * n_sel, QT), lambda b, i: (b * nqt + i, 0, 0)),
        ],
        out_shape=[
            jax.ShapeDtypeStruct((batch * seq, D_Q), F32),
            jax.ShapeDtypeStruct((batch * nqt, N_KV * n_sel, QT), BF16),
        ],
        compiler_params=_params(("parallel", "parallel")),
        name="cmp_attn_prompt",
    )(q_t, kcmp, cmap, gates, ov_t)


_TK = 2 * LANES


def _sel_prompt_kernel(qt_ref, kv_ref, sb_ref, map_ref, gate_ref, o_ref, ka_ref, vt_ref, qa_ref, *, n_sel, seq):
    qt = pl.program_id(1)
    aug = 2 * HEAD_DIM
    half = D_KV // 2

    @pl.when(qt == 0)
    def _():
        pos = lax.broadcasted_iota(I32, (seq, aug - HEAD_DIM), 0)
        blk = lax.broadcasted_iota(I32, (seq, aug - HEAD_DIM), 1)
        onehot = jnp.where(pos // SEL_LEN == blk, 1.0, 0.0)
        k_rows = kv_ref[0, pl.ds(0, half), :].T
        for g in range(N_KV):
            ka_ref[g] = jnp.concatenate([k_rows[:, g * HEAD_DIM:(g + 1) * HEAD_DIM], onehot], axis=1).astype(BF16)
            vt_ref[g] = kv_ref[0, pl.ds(half + g * HEAD_DIM, HEAD_DIM), :].astype(BF16)

    gate = gate_ref[...]
    sb = sb_ref[...]
    diag = qt // 2
    n_tiles = diag + 1
    pad = jnp.zeros((aug - HEAD_DIM - n_sel, GROUP_ROWS), BF16)
    for g in range(N_KV):
        sbg = sb[g * n_sel:(g + 1) * n_sel]
        qa_ref[g] = jnp.concatenate([_group_queries(qt_ref, g), jnp.concatenate([sbg] * HPG, axis=1), pad], axis=0)

    def tile(kt, carry):
        start = pl.multiple_of(kt * _TK, _TK)
        kind = jnp.where(kt == diag, qt % 2, jnp.where((kt == diag - 1) & (qt % 2 == 0), 2, 3))
        def scores(g):
            return _mm(ka_ref[g, pl.ds(start, _TK), :], qa_ref[g]) + map_ref[g, kind]

        def softmax(g, st):
            m_i, l_i, _ = carry[g]
            m_new = jnp.maximum(m_i, jnp.max(st, axis=0, keepdims=True))
            alpha = jnp.exp(m_i - m_new)
            p = jnp.exp(st - m_new)
            return m_new, alpha * l_i + jnp.sum(p, axis=0, keepdims=True), alpha, p.astype(BF16)

        def values(g, sm):
            m_new, l_new, alpha, pb = sm
            return m_new, l_new, alpha * carry[g][2] + _mm(vt_ref[g, :, pl.ds(start, _TK)], pb)

        return tuple(_staggered(N_KV, scores, softmax, values))

    init = tuple((jnp.full((1, GROUP_ROWS), -jnp.inf, F32), jnp.zeros((1, GROUP_ROWS), F32),
                  jnp.zeros((HEAD_DIM, GROUP_ROWS), F32)) for _ in range(N_KV))
    final = lax.fori_loop(0, n_tiles, tile, init)
    for g in range(N_KV):
        _, l_f, acc = final[g]
        _store_heads(o_ref, acc / l_f, gate, g, 1)


def _sel_prompt(q_t, kv_t, selbias, smap, gates, batch, seq):
    nqt = seq // QT
    n_sel = selbias.shape[1] // N_KV
    return pl.pallas_call(
        functools.partial(_sel_prompt_kernel, n_sel=n_sel, seq=seq),
        grid=(batch, nqt),
        in_specs=[
            pl.BlockSpec((D_Q, QT), lambda b, i: (0, b * nqt + i)),
            pl.BlockSpec((1, D_KV, seq), lambda b, i: (b, 0, 0)),
            pl.BlockSpec((None, N_KV * n_sel, QT), lambda b, i: (b * nqt + i, 0, 0)),
            pl.BlockSpec(smap.shape, lambda b, i: (0, 0, 0, 0)),
            pl.BlockSpec((QT, LANES), lambda b, i: (b * nqt + i, 0)),
        ],
        out_specs=pl.BlockSpec((QT, D_Q), lambda b, i: (b * nqt + i, 0)),
        out_shape=jax.ShapeDtypeStruct((batch * seq, D_Q), F32),
        scratch_shapes=[pltpu.VMEM((N_KV, seq, 2 * HEAD_DIM), BF16), pltpu.VMEM((N_KV, HEAD_DIM, seq), BF16),
                        pltpu.VMEM((N_KV, 2 * HEAD_DIM, GROUP_ROWS), BF16)],
        compiler_params=_params(("arbitrary", "arbitrary"), 56),
        name="sel_attn_prompt",
    )(q_t, kv_t, selbias, smap, gates)


_WIN_TILES = WINDOW // QT + 1


def _win_prompt_kernel(qt_ref, *refs):
    kv_refs = refs[:_WIN_TILES]
    map_ref, gate_ref, o_ref = refs[_WIN_TILES:]
    qt = pl.program_id(1)
    gate = gate_ref[...]
    half = D_KV // 2
    span = _WIN_TILES * QT
    key = lax.broadcasted_iota(I32, (span, GROUP_ROWS), 0)
    in_seq = key >= (_WIN_TILES - 1 - qt) * QT
    k_rows = jnp.concatenate([r[0, pl.ds(0, half), :].T for r in kv_refs], axis=0)
    def scores(g):
        kg = k_rows[:, g * HEAD_DIM:(g + 1) * HEAD_DIM].astype(BF16)
        return jnp.where(in_seq, _mm(kg, _group_queries(qt_ref, g)) + map_ref[g], NEG)

    def softmax(g, st):
        e = jnp.exp(st - jnp.max(st, axis=0, keepdims=True))
        return (e / jnp.sum(e, axis=0, keepdims=True)).astype(BF16)

    def values(g, pb):
        vgt = jnp.concatenate([r[0, pl.ds(half + g * HEAD_DIM, HEAD_DIM), :] for r in kv_refs],
                              axis=1).astype(BF16)
        _store_heads(o_ref, _mm(vgt, pb), gate, g, 2)

    _staggered(N_KV, scores, softmax, values)


def _win_prompt(q_t, kv_t, wmap, gates, batch, seq):
    nqt = seq // QT

    def kv_spec(k):
        return pl.BlockSpec((1, D_KV, QT), lambda b, i: (b, 0, jnp.maximum(i - (_WIN_TILES - 1) + k, 0)))

    return pl.pallas_call(
        _win_prompt_kernel,
        grid=(batch, nqt),
        in_specs=[pl.BlockSpec((D_Q, QT), lambda b, i: (0, b * nqt + i))]
        + [kv_spec(k) for k in range(_WIN_TILES)]
        + [pl.BlockSpec(wmap.shape, lambda b, i: (0, 0, 0)),
           pl.BlockSpec((QT, LANES), lambda b, i: (b * nqt + i, 0))],
        out_specs=pl.BlockSpec((QT, D_Q), lambda b, i: (b * nqt + i, 0)),
        out_shape=jax.ShapeDtypeStruct((batch * seq, D_Q), F32),
        compiler_params=_params(("parallel", "parallel")),
        name="win_attn_prompt",
    )(q_t, *([kv_t] * _WIN_TILES), wmap, gates)


def _group_rows(x_by_group):
    row = lax.broadcasted_iota(I32, x_by_group[0].shape, 0)
    out = x_by_group[0]
    for g in range(1, N_KV):
        out = jnp.where(row // HPG == g, x_by_group[g], out)
    return out


def _cmp_sample_kernel(q_ref, kc_ref, bias_ref, ov_ref, gate_ref, o_ref, idx_ref, *, n_sel, past):
    q = q_ref[0]
    kv = kc_ref[0].astype(BF16)
    bias = bias_ref[...]
    half = D_KV // 2
    s = _group_rows([_mm_nt(q, kv[:, g * HEAD_DIM:(g + 1) * HEAD_DIM]) for g in range(N_KV)]) + bias
    e = jnp.exp(s - jnp.max(s, axis=-1, keepdims=True))
    p = e / jnp.sum(e, axis=-1, keepdims=True) * (bias > 0.5 * NEG).astype(F32)
    pb = p.astype(BF16)
    o = _group_rows([_mm(pb, kv[:, half + g * HEAD_DIM:half + (g + 1) * HEAD_DIM]) for g in range(N_KV)])
    o_ref[0] = o * gate_ref[0]
    imp_h = _mm(pb, ov_ref[...])
    nsp = imp_h.shape[1]
    imp = jnp.concatenate(
        [imp_h[g * HPG:g * HPG + 1] + imp_h[g * HPG + 1:g * HPG + 2] + imp_h[g * HPG + 2:g * HPG + 3]
         + imp_h[g * HPG + 3:g * HPG + 4] for g in range(N_KV)] + [jnp.zeros((8 - N_KV, nsp), F32)], axis=0)
    j = lax.broadcasted_iota(I32, (8, nsp), 1)
    cur = past // SEL_LEN
    forced = (j == 0) | (j == cur) | (j == cur - 1)
    valid = (j * SEL_LEN <= past) & (j < n_sel)
    score = jnp.where(valid, jnp.where(forced, jnp.inf, imp), -jnp.inf)
    slot = lax.broadcasted_iota(I32, (8, LANES), 1)
    res = jnp.zeros((8, LANES), I32)
    for k in range(min(SEL_TOP, n_sel)):
        best = jnp.max(score, axis=-1, keepdims=True)
        pick = jnp.min(jnp.where(score == best, j, nsp), axis=-1, keepdims=True)
        res = jnp.where(slot == k, pick, res)
        score = jnp.where(j == pick, -jnp.inf, score)
    idx_ref[0] = res


def _cmp_sample(q3, kcmp, bias, ov, gate3, n_sel, past):
    bs, n_chunk, _ = kcmp.shape
    return pl.pallas_call(
        functools.partial(_cmp_sample_kernel, n_sel=n_sel, past=past),
        grid=(bs,),
        in_specs=[
            pl.BlockSpec((1, N_HEADS, HEAD_DIM), lambda b: (b, 0, 0)),
            pl.BlockSpec((1, n_chunk, D_KV), lambda b: (b, 0, 0)),
            pl.BlockSpec(bias.shape, lambda b: (0, 0)),
            pl.BlockSpec(ov.shape, lambda b: (0, 0)),
            pl.BlockSpec((1, N_HEADS, HEAD_DIM), lambda b: (b, 0, 0)),
        ],
        out_specs=[
            pl.BlockSpec((1, N_HEADS, HEAD_DIM), lambda b: (b, 0, 0)),
            pl.BlockSpec((1, 8, LANES), lambda b: (b, 0, 0)),
        ],
        out_shape=[
            jax.ShapeDtypeStruct((bs, N_HEADS, HEAD_DIM), F32),
            jax.ShapeDtypeStruct((bs, 8, LANES), I32),
        ],
        compiler_params=_params(("parallel",)),
        name="cmp_attn_sample",
    )(q3, kcmp, bias, ov, gate3)


def _sel_sample_kernel(blk_ref, phys_ref, q_ref, new_ref, bias_ref, gate_ref, *refs, top, n_pages):
    k_refs = refs[:top]
    v_refs = refs[top:2 * top]
    o_ref = refs[2 * top]
    del phys_ref
    b = pl.program_id(0)
    g = pl.program_id(1)
    q = q_ref[0, 0]
    lane = lax.broadcasted_iota(I32, (8, PAGE), 1)
    scores = []
    for k in range(top):
        blk = blk_ref[(b * N_KV + g) * top + k]
        page = jnp.minimum(blk // 2, n_pages)
        s = _mm(q, k_refs[k][...].astype(BF16)) + bias_ref[page]
        ok = (lane // SEL_LEN == blk % 2) & (blk < 2 * n_pages)
        scores.append(jnp.where(ok, s, NEG))
    k_new = new_ref[0, 0, 0:1, :].astype(BF16).astype(F32)
    v_new = new_ref[0, 0, 1:2, :].astype(BF16).astype(F32)
    s_new = jnp.sum(q.astype(F32) * k_new, axis=-1, keepdims=True) + bias_ref[n_pages]
    scores.append(s_new)
    s_all = jnp.concatenate(scores, axis=1)
    e = jnp.exp(s_all - jnp.max(s_all, axis=-1, keepdims=True))
    p = e / jnp.sum(e, axis=-1, keepdims=True)
    o = p[:, top * PAGE:top * PAGE + 1].astype(BF16).astype(F32) * v_new
    for k in range(top):
        o = o + _mm_nt(p[:, k * PAGE:(k + 1) * PAGE].astype(BF16), v_refs[k][...].astype(BF16))
    o_ref[0, 0] = o * gate_ref[0, 0]


def _sel_sample(blk, phys, q4, new4, bias4, gate4, pool4, top, n_pages):
    bs = q4.shape[0]

    def kv_spec(k, is_v):
        return pl.BlockSpec(
            (None, None, HEAD_DIM, PAGE),
            lambda b, g, blk_r, phys_r: (phys_r[(b * N_KV + g) * top + k], g + N_KV * is_v, 0, 0))

    grp = lambda b, g, blk_r, phys_r: (b, g, 0, 0)
    grid_spec = pltpu.PrefetchScalarGridSpec(
        num_scalar_prefetch=2,
        grid=(bs, N_KV),
        in_specs=[
            pl.BlockSpec((1, 1, 8, HEAD_DIM), grp),
            pl.BlockSpec((1, 1, 8, HEAD_DIM), grp),
            pl.BlockSpec((None, n_pages + 1, 8, LANES), lambda b, g, blk_r, phys_r: (g, 0, 0, 0)),
            pl.BlockSpec((1, 1, 8, HEAD_DIM), grp),
        ] + [kv_spec(k, 0) for k in range(top)] + [kv_spec(k, 1) for k in range(top)],
        out_specs=pl.BlockSpec((1, 1, 8, HEAD_DIM), grp),
    )
    return pl.pallas_call(
        functools.partial(_sel_sample_kernel, top=top, n_pages=n_pages),
        grid_spec=grid_spec,
        out_shape=jax.ShapeDtypeStruct((bs, N_KV, 8, HEAD_DIM), F32),
        compiler_params=_params(("parallel", "parallel")),
        name="sel_attn_sample",
    )(blk, phys, q4, new4, bias4, gate4, *([pool4] * (2 * top)))


def _win_sample_kernel(q_ref, new_ref, st_ref, col_ref, bias_ref, gate_ref, o_ref, nst_ref):
    st = st_ref[0]
    half = D_KV // 2
    for g in range(N_KV):
        q = q_ref[0, g]
        bias = bias_ref[g]
        s_old = _mm(q, st[g * HEAD_DIM:(g + 1) * HEAD_DIM, :].astype(BF16)) + bias[:, 0:WINDOW]
        k_new = new_ref[0, g, 0:1, :].astype(BF16).astype(F32)
        v_new = new_ref[0, g, 1:2, :].astype(BF16).astype(F32)
        s_new = jnp.sum(q.astype(F32) * k_new, axis=-1, keepdims=True) + bias[:, WINDOW:WINDOW + LANES]
        s_all = jnp.concatenate([s_old, s_new], axis=1)
        e = jnp.exp(s_all - jnp.max(s_all, axis=-1, keepdims=True))
        p = e / jnp.sum(e, axis=-1, keepdims=True)
        o = _mm_nt(p[:, 0:WINDOW].astype(BF16), st[half + g * HEAD_DIM:half + (g + 1) * HEAD_DIM, :].astype(BF16))
        o = o + p[:, WINDOW:WINDOW + 1].astype(BF16).astype(F32) * v_new
        o_ref[0, g] = o * gate_ref[0, g]
    rolled = pltpu.roll(st, WINDOW - 1, axis=1)
    lane = lax.broadcasted_iota(I32, (D_KV, LANES), 1)
    nst_ref[0, :, pl.ds(0, WINDOW - LANES)] = rolled[:, 0:WINDOW - LANES]
    nst_ref[0, :, pl.ds(WINDOW - LANES, LANES)] = jnp.where(
        lane == LANES - 1, col_ref[0], rolled[:, WINDOW - LANES:WINDOW])


def _win_sample(q4, new4, state_t, newcol, bias4, gate4):
    bs = q4.shape[0]
    grp = pl.BlockSpec((1, N_KV, 8, HEAD_DIM), lambda b: (b, 0, 0, 0))
    return pl.pallas_call(
        _win_sample_kernel,
        grid=(bs,),
        in_specs=[
            grp, grp,
            pl.BlockSpec((1, D_KV, WINDOW), lambda b: (b, 0, 0)),
            pl.BlockSpec((1, D_KV, LANES), lambda b: (b, 0, 0)),
            pl.BlockSpec(bias4.shape, lambda b: (0, 0, 0)),
            grp,
        ],
        out_specs=[grp, pl.BlockSpec((1, D_KV, WINDOW), lambda b: (b, 0, 0))],
        out_shape=[
            jax.ShapeDtypeStruct((bs, N_KV, 8, HEAD_DIM), F32),
            jax.ShapeDtypeStruct((bs, D_KV, WINDOW), F32),
        ],
        compiler_params=_params(("parallel",)),
        name="win_attn_sample",
    )(q4, new4, state_t, newcol, bias4, gate4)


_MOE_ROWS = 256
_MOE_SLACK = 2 * _MOE_ROWS


def _row_copy(src, src_row, dst, dst_row, sem):
    return pltpu.make_async_copy(src.at[pl.ds(src_row, 1)], dst.at[pl.ds(dst_row, 1)], sem)


def _expert_kernel(be_ref, nu_ref, cur_ref, nxt_ref, h_ref, w_ref, wg_ref, wu_ref, wd_ref, yk_ref,
                   xbuf, ybuf, gsem, ssem, *, n_rows, n_live):
    del be_ref
    i = pl.program_id(0)
    slot = i % 2
    n_used = nu_ref[0]

    def gather(asg_ref, dst_slot):
        for r in range(_MOE_ROWS):
            tok = jnp.maximum(asg_ref[0, r], 0) // TOP_K
            _row_copy(h_ref, tok, xbuf.at[dst_slot], r, gsem.at[dst_slot]).start()

    def wait_gather(s):
        for r in range(_MOE_ROWS):
            _row_copy(h_ref, 0, xbuf.at[s], 0, gsem.at[s]).wait()

    def scatter(s):
        for r in range(_MOE_ROWS):
            a = cur_ref[0, r]
            dst = jnp.where(a >= 0, (a % TOP_K) * n_rows + a // TOP_K, s * n_rows + (n_rows - _MOE_ROWS) + r)
            _row_copy(ybuf.at[s], r, yk_ref, dst, ssem.at[s]).start()

    def wait_scatter(s):
        for r in range(_MOE_ROWS):
            _row_copy(ybuf.at[s], 0, yk_ref, 0, ssem.at[s]).wait()

    @pl.when(i == 0)
    def _():
        gather(cur_ref, 0)
        ybuf[1] = jnp.zeros((_MOE_ROWS, ybuf.shape[2]), F32)
        fills = [
            pltpu.make_async_copy(ybuf.at[1, pl.ds(0, min(_MOE_ROWS, n_rows - off))],
                                  yk_ref.at[pl.ds(k * n_rows + off, min(_MOE_ROWS, n_rows - off))], ssem.at[1])
            for k in range(TOP_K) for off in range(n_live, n_rows, _MOE_ROWS)]
        for cp in fills:
            cp.start()
        for cp in fills:
            cp.wait()

    @pl.when(i + 1 < n_used)
    def _():
        gather(nxt_ref, 1 - slot)

    @pl.when(i < n_used)
    def _():
        wait_gather(slot)

        @pl.when(i >= 2)
        def _():
            wait_scatter(slot)

        x = xbuf[slot].astype(BF16)
        a = _silu(_mm(x, wg_ref[...])) * _mm(x, wu_ref[...])
        ybuf[slot] = _mm(a.astype(BF16), wd_ref[...]) * w_ref[...]
        scatter(slot)

        @pl.when(i == n_used - 1)
        def _():
            @pl.when(i >= 1)
            def _():
                wait_scatter(1 - slot)

            wait_scatter(slot)


def _experts(blk_e, n_used, rows_asg, rows_w, h2_all, n_live, w_gate, w_up, w_down):
    n_rows, d = h2_all.shape
    n_blk = blk_e.shape[0]
    asg3 = rows_asg.reshape(n_blk, 1, _MOE_ROWS)
    wspec = lambda a: pl.BlockSpec((None,) + a.shape[1:], lambda i, be, nu: (be[i], 0, 0))
    smem_blk = lambda fn: pl.BlockSpec((None, 1, _MOE_ROWS), fn, memory_space=pltpu.SMEM)
    grid_spec = pltpu.PrefetchScalarGridSpec(
        num_scalar_prefetch=2,
        grid=(n_blk,),
        in_specs=[
            smem_blk(lambda i, be, nu: (i, 0, 0)),
            smem_blk(lambda i, be, nu: (jnp.minimum(i + 1, n_blk - 1), 0, 0)),
            pl.BlockSpec(memory_space=pl.ANY),
            pl.BlockSpec((_MOE_ROWS, 1), lambda i, be, nu: (i, 0)),
            wspec(w_gate), wspec(w_up), wspec(w_down),
        ],
        out_specs=pl.BlockSpec(memory_space=pl.ANY),
        scratch_shapes=[
            pltpu.VMEM((2, _MOE_ROWS, d), F32),
            pltpu.VMEM((2, _MOE_ROWS, d), F32),
            pltpu.SemaphoreType.DMA((2,)),
            pltpu.SemaphoreType.DMA((2,)),
        ],
    )
    return pl.pallas_call(
        functools.partial(_expert_kernel, n_rows=n_rows, n_live=n_live),
        grid_spec=grid_spec,
        out_shape=jax.ShapeDtypeStruct((TOP_K * n_rows, d), F32),
        compiler_params=pltpu.CompilerParams(dimension_semantics=("arbitrary",), vmem_limit_bytes=56 * MIB,
                                             disable_bounds_checks=True),
        name="moe_experts",
    )(blk_e, n_used, asg3, asg3, h2_all, rows_w, w_gate, w_up, w_down)


def _combine_kernel(x_ref, y0_ref, y1_ref, g_ref, o_ref):
    x = x_ref[...] + (y0_ref[...] + y1_ref[...])
    ms = jnp.mean(x * x, axis=-1, keepdims=True)
    o_ref[...] = x * lax.rsqrt(ms + EPS) * g_ref[...]


def _combine(x1, yk, gfin, row0, tm):
    n, d = x1.shape
    yk3 = yk.reshape(TOP_K, -1, d)
    blk0 = row0 // tm
    return pl.pallas_call(
        _combine_kernel,
        grid=(n // tm,),
        in_specs=[
            pl.BlockSpec((tm, d), lambda i: (i, 0)),
            pl.BlockSpec((None, tm, d), lambda i: (0, blk0 + i, 0)),
            pl.BlockSpec((None, tm, d), lambda i: (1, blk0 + i, 0)),
            pl.BlockSpec((1, d), lambda i: (0, 0)),
        ],
        out_specs=pl.BlockSpec((tm, d), lambda i: (i, 0)),
        out_shape=jax.ShapeDtypeStruct((n, d), F32),
        compiler_params=_params(("parallel",)),
        name="moe_combine",
    )(x1, yk3, yk3, gfin)


def _dispatch(eid, w, counts):
    a = eid.shape[0]
    tb = _MOE_ROWS
    order = jnp.argsort(eid).astype(I32)
    padded = (counts + tb - 1) // tb * tb
    seg_end = jnp.cumsum(padded)
    seg_start = seg_end - padded
    start = jnp.cumsum(counts) - counts
    n_blk = (a + N_EXPERTS * (tb - 1) + tb - 1) // tb
    blk_e = jnp.minimum(jnp.sum((seg_end[None, :] <= (jnp.arange(n_blk) * tb)[:, None]).astype(I32), axis=1),
                        N_EXPERTS - 1)
    slot = jnp.arange(n_blk * tb)
    e_slot = jnp.repeat(blk_e, tb)
    off = slot - seg_start[e_slot]
    valid = off < counts[e_slot]
    asg = order[jnp.clip(start[e_slot] + off, 0, a - 1)]
    rows_asg = jnp.where(valid, asg, -1).astype(I32)
    rows_w = jnp.where(valid, w[asg], 0.0)
    n_used = (seg_end[-1] // tb).astype(I32).reshape(1)
    return rows_asg, rows_w.reshape(-1, 1), blk_e.astype(I32), n_used


def _overlap(n_chunk, n_sel, n_sel_pad):
    cs = jnp.arange(n_chunk)[:, None] * CMP_STRIDE
    ss = jnp.arange(n_sel_pad)[None, :] * SEL_LEN
    hit = (cs < ss + SEL_LEN) & (cs + CMP_LEN > ss) & (jnp.arange(n_sel_pad)[None, :] < n_sel)
    return hit.astype(BF16)


def kernel(x_prompt, x_sample, cache_cmp_kv, cache_sel_kv, state_win_kv, state_conv, page_table, rel_bias,
           norm_mix, w_in, b_in, conv_w, conv_b, conv_ln_g, conv_ln_b, w_conv_out, b_conv_out,
           cmp_pe_k, cmp_w1_k, cmp_w2_k, cmp_pe_v, cmp_w1_v, cmp_w2_v, w_nsa_out, w_out, norm_ffn,
           w_rg, b_rg, w_re, b_re, w_gate, w_up, w_down, norm_final):
    batch, seq, _ = x_prompt.shape
    bs = x_sample.shape[0]
    n_pool = cache_cmp_kv.shape[1]
    n_pages = page_table.shape[1]
    past = n_pages * PAGE
    n_tok = batch * seq
    assert bs + _MOE_ROWS <= _MOE_SLACK and n_tok % bs == 0 and seq % _TK == 0 and bs % 16 == 0
    l = 0

    wt = w_in[l].T.astype(BF16)
    bias_in = b_in[l]
    cuts = [0, D_CONV, 2 * D_CONV, 2 * D_CONV + D_Q, 2 * D_CONV + D_Q + 3 * D_KV]
    cuts += [cuts[-1] + 3 * N_HEADS, cuts[-1] + 3 * N_HEADS + D_MODEL, cuts[-1] + 3 * N_HEADS + 2 * D_MODEL]
    seg = lambda k: (wt[cuts[k]:cuts[k + 1]], bias_in[cuts[k]:cuts[k + 1]][None, :])
    (wa, ba), (wb, bb), (wq, bq), (wkv, bkv), (wzg, bzg), (wga, bga), (wgb, bgb) = [seg(k) for k in range(7)]
    wzg = jnp.pad(wzg, ((0, LANES - 3 * N_HEADS), (0, 0)))
    bzg = jnp.pad(bzg, ((0, 0), (0, LANES - 3 * N_HEADS)))
    wc = w_conv_out[l].astype(BF16)
    wn = w_nsa_out[l].astype(BF16)
    wo = w_out[l].astype(BF16)
    wr = jnp.zeros((LANES, D_MODEL), F32).at[0:N_GROUPS].set(w_rg[l].T).at[8:8 + N_EXPERTS].set(w_re[l].T)
    br = jnp.zeros((LANES,), F32).at[0:N_GROUPS].set(b_rg[l]).at[8:8 + N_EXPERTS].set(b_re[l].reshape(-1))
    wr, br = wr.astype(BF16), br[:, None]
    cw = _compress_weights(cmp_pe_k[l], cmp_w1_k[l], cmp_w2_k[l], cmp_pe_v[l], cmp_w1_v[l], cmp_w2_v[l])
    rb_t = rel_bias.T
    gmix, gffn, gfin = norm_mix[l][None, :], norm_ffn[l][None, :], norm_final[None, :]
    vec = lambda a: a[l][None, :]

    scale_q = lambda z: z * HEAD_DIM ** -0.5

    def project(x, tm, feature_major_q):
        h = _rmsnorm(x, gmix, tm)
        u = _glu(h, wa, wb, ba, bb, tm, 512)
        if feature_major_q:
            q = _proj_t(h, wq, bq.reshape(-1, 1), scale_q, BF16, tm, 512, "proj_q_t")
        else:
            q = _proj(h, wq, bq, scale_q, BF16, tm, 512, "proj_q")
        gates = _proj(h, wzg, bzg, _sigmoid, F32, tm, LANES, "proj_head_gates")
        return h, u, q, gates

    xp = x_prompt.reshape(n_tok, D_MODEL)
    xs = x_sample.reshape(bs, D_MODEL)
    tm_p = 512
    hp, up, qp, gates_p = project(xp, tm_p, True)
    hs, us, qs, gates_s = project(xs, bs, False)
    kv_p = _kv_transposed(hp, wkv, bkv.reshape(-1, 1), batch, seq, tm_p)
    kv_s = _proj(hs, wkv, bkv, lambda z: z, F32, bs, 512, "proj_kv_sample")

    conv_args = (conv_w[l], vec(conv_b), vec(conv_ln_g), vec(conv_ln_b))
    cp = _conv_prompt(up, *conv_args, batch, seq, 256)
    cs, new_conv_s = _conv_sample(state_conv[l].transpose(1, 0, 2), us, *conv_args)

    nqt = seq // QT
    n_chunk_p = seq // CMP_STRIDE
    n_chunk_s = n_pages * _CHUNKS_PER_PAGE
    cmap, wmap, smap = _bias_prompt(rb_t, nqt, n_chunk_p)
    cbias_s, wbias_s, sbias_s = _bias_sample(rb_t, past, n_chunk_s, n_pages)

    ident = jnp.zeros((1,), I32)
    kcmp_p = _compress(kv_p[0], ident, cw, batch, seq // PAGE, seq // PAGE, False)
    feat_major = lambda a: a.transpose(0, 2, 3, 4, 1)
    pool_cmp = feat_major(cache_cmp_kv[l]).reshape(n_pool, D_KV, PAGE)
    pp_s = min(32, n_pages)
    kcmp_s = _compress(pool_cmp, page_table.reshape(-1), cw, bs, n_pages, pp_s, True)

    n_sel_p = seq // SEL_LEN
    o_cmp_p, selbias = _cmp_prompt(qp, kcmp_p, cmap, gates_p, _overlap(n_chunk_p, n_sel_p, n_sel_p).T,
                                   batch, seq)
    n_sel_s = -(-(past + 1) // SEL_LEN)
    n_sel_pad = -(-n_sel_s // LANES) * LANES
    top_s = min(SEL_TOP, n_sel_s)
    head_gate = lambda k: jnp.broadcast_to(
        gates_s[:, k * N_HEADS:(k + 1) * N_HEADS, None], (bs, N_HEADS, HEAD_DIM))
    qs3 = qs.reshape(bs, N_HEADS, HEAD_DIM)
    o_cmp_s, idx_s = _cmp_sample(qs3, kcmp_s, cbias_s, _overlap(n_chunk_s, n_sel_s, n_sel_pad), head_gate(0),
                                 n_sel_s, past)

    o_sel_p = _sel_prompt(qp, kv_p[1], selbias, smap, gates_p, batch, seq)
    by_group = lambda a: jnp.pad(a.reshape(bs, N_KV, HPG, -1), ((0, 0), (0, 0), (0, 8 - HPG), (0, 0)))
    q4 = by_group(qs3)
    blk = idx_s[:, :N_KV, :top_s].reshape(-1)
    phys = jnp.take_along_axis(
        page_table, jnp.minimum(idx_s[:, :N_KV, :top_s] // 2, n_pages - 1).reshape(bs, -1), axis=1).reshape(-1)
    kv_new = lambda k: jnp.pad(
        kv_s[:, k * D_KV:(k + 1) * D_KV].reshape(bs, 2, N_KV, HEAD_DIM).transpose(0, 2, 1, 3),
        ((0, 0), (0, 0), (0, 6), (0, 0)))
    bias_by_group = lambda a: jnp.pad(
        jnp.moveaxis(a, -2, 0).reshape((N_KV, HPG) + a.shape[:-2] + a.shape[-1:]),
        ((0, 0), (0, 8 - HPG)) + ((0, 0),) * (a.ndim - 1))
    sb4 = jnp.moveaxis(bias_by_group(sbias_s), 1, 2)
    pool_sel = feat_major(cache_sel_kv[l]).reshape(n_pool, 2 * N_KV, HEAD_DIM, PAGE)
    o_sel_s = _sel_sample(blk, phys, q4, kv_new(1), sb4, by_group(head_gate(1)), pool_sel, top_s, n_pages)

    o_win_p = _win_prompt(qp, kv_p[2], wmap, gates_p, batch, seq)
    win_t = feat_major(state_win_kv[l]).reshape(bs, D_KV, WINDOW)
    newcol = jnp.broadcast_to(kv_s[:, 2 * D_KV:3 * D_KV, None], (bs, D_KV, LANES))
    o_win_s, new_win_s = _win_sample(q4, kv_new(2), win_t, newcol, bias_by_group(wbias_s), by_group(head_gate(2)))

    m_p = _merge_a(hp, cp, o_cmp_p, o_sel_p, o_win_p, wga, wgb, wc, wn, bga, bgb, vec(b_conv_out), 512, 512)
    h2_all = jnp.zeros((n_tok + _MOE_SLACK, D_MODEL), F32)
    x1_p, h2_all, eid_p, rw_p, cnt_p = _merge_b(xp, m_p, wo, gffn, wr, br, h2_all, 0, 256)
    from_groups = lambda a: a[:, :, :HPG, :].reshape(bs, D_Q)
    m_s = _merge_a(hs, cs, o_cmp_s.reshape(bs, D_Q), from_groups(o_sel_s), from_groups(o_win_s),
                   wga, wgb, wc, wn, bga, bgb, vec(b_conv_out), bs, 512)
    x1_s, h2_all, eid_s, rw_s, cnt_s = _merge_b(xs, m_s, wo, gffn, wr, br, h2_all, n_tok, bs)

    eid = jnp.concatenate([eid_p[:TOP_K].T, eid_s[:TOP_K].T], axis=0).reshape(-1)
    rw = jnp.concatenate([rw_p[:TOP_K].T, rw_s[:TOP_K].T], axis=0).reshape(-1)
    counts = jnp.sum(cnt_p[:, :, 0], axis=0) + jnp.sum(cnt_s[:, :, 0], axis=0)
    rows_asg, rows_w, blk_e, n_used = _dispatch(eid, rw, counts)
    yk = _experts(blk_e, n_used, rows_asg, rows_w, h2_all, n_tok + bs, w_gate[l].astype(BF16),
                  w_up[l].astype(BF16), w_down[l].astype(BF16))
    y_p = _combine(x1_p, yk, gfin, 0, 256)
    y_s = _combine(x1_s, yk, gfin, n_tok, bs)

    def cache_rows(a_t, rows):
        return a_t.reshape(a_t.shape[0], 2, N_KV, HEAD_DIM, rows).transpose(0, 4, 1, 2, 3)[None]

    new_row = lambda k: kv_s[:, k * D_KV:(k + 1) * D_KV].reshape(1, bs, 1, 2, N_KV, HEAD_DIM)
    hist = CONV_W - 1
    return (
        y_p.reshape(batch, seq, D_MODEL),
        y_s.reshape(bs, 1, D_MODEL),
        cache_rows(kv_p[0], seq),
        new_row(0),
        cache_rows(kv_p[1], seq),
        new_row(1),
        cache_rows(kv_p[2][:, :, seq - WINDOW:], WINDOW),
        cache_rows(new_win_s, WINDOW),
        up.reshape(batch, seq, D_CONV)[None, :, seq - hist:],
        new_conv_s.transpose(1, 0, 2)[None],
    )
```

```python
import functools
import math

import jax
import jax.numpy as jnp
from jax import lax
from jax.experimental import pallas as pl
from jax.experimental.pallas import tpu as pltpu

F32 = jnp.float32
BF16 = jnp.bfloat16
I32 = jnp.int32

D_MODEL = 2048
D_CONV = 1024
CONV_W = 31
N_HEADS = 16
HEAD_DIM = 64
N_KV = 4
HPG = N_HEADS // N_KV
D_Q = N_HEADS * HEAD_DIM
D_KV = 2 * N_KV * HEAD_DIM
CMP_LEN = 32
CMP_STRIDE = 16
CMP_HID = 64
SEL_LEN = 64
SEL_TOP = 16
WINDOW = 512
N_BUCKETS = 32
MAX_DIST = 128
N_GROUPS = 4
EXP_PER_GROUP = 8
N_EXPERTS = N_GROUPS * EXP_PER_GROUP
TOP_K = 2
D_EXPERT = 1024
EPS = 1e-6
NEG = -1e30
PAGE = 128
LANES = 128
QT = 128
GROUP_ROWS = HPG * QT
MIB = 1024 * 1024


def _params(sem, vmem_mib=48):
    return pltpu.CompilerParams(dimension_semantics=sem, vmem_limit_bytes=vmem_mib * MIB)


def _mm_nt(a, b):
    return lax.dot_general(a, b, (((1,), (1,)), ((), ())), preferred_element_type=F32)


def _mm(a, b):
    return jnp.dot(a, b, preferred_element_type=F32)


def _sigmoid(z):
    return 1.0 / (1.0 + jnp.exp(-z))


def _silu(z):
    return z * _sigmoid(z)


def _gelu_tanh(z):
    return 0.5 * z * (1.0 + jnp.tanh(math.sqrt(2.0 / math.pi) * (z + 0.044715 * (z * z * z))))


def _rmsnorm_kernel(x_ref, g_ref, o_ref):
    x = x_ref[...]
    ms = jnp.mean(x * x, axis=-1, keepdims=True)
    o_ref[...] = (x * lax.rsqrt(ms + EPS) * g_ref[...]).astype(o_ref.dtype)


def _rmsnorm(x, g, tm):
    n, d = x.shape
    return pl.pallas_call(
        _rmsnorm_kernel,
        grid=(n // tm,),
        in_specs=[pl.BlockSpec((tm, d), lambda i: (i, 0)), pl.BlockSpec((1, d), lambda i: (0, 0))],
        out_specs=pl.BlockSpec((tm, d), lambda i: (i, 0)),
        out_shape=jax.ShapeDtypeStruct((n, d), BF16),
        compiler_params=_params(("parallel",)),
        name="rmsnorm",
    )(x, g)


def _proj_kernel(h_ref, w_ref, b_ref, o_ref, *, act):
    z = _mm_nt(h_ref[...], w_ref[...]) + b_ref[...]
    o_ref[...] = act(z).astype(o_ref.dtype)


def _proj(h, wt, b, act, out_dtype, tm, tn, name):
    n, k = h.shape
    nout = wt.shape[0]
    return pl.pallas_call(
        functools.partial(_proj_kernel, act=act),
        grid=(n // tm, nout // tn),
        in_specs=[
            pl.BlockSpec((tm, k), lambda i, j: (i, 0)),
            pl.BlockSpec((tn, k), lambda i, j: (j, 0)),
            pl.BlockSpec((1, tn), lambda i, j: (0, j)),
        ],
        out_specs=pl.BlockSpec((tm, tn), lambda i, j: (i, j)),
        out_shape=jax.ShapeDtypeStruct((n, nout), out_dtype),
        compiler_params=_params(("parallel", "parallel")),
        name=name,
    )(h, wt, b)


def _proj_t_kernel(h_ref, w_ref, b_ref, o_ref, *, act):
    z = _mm_nt(w_ref[...], h_ref[...]) + b_ref[...]
    o_ref[...] = act(z).astype(o_ref.dtype)


def _proj_t(h, wt, bcol, act, out_dtype, tm, tn, name):
    n, k = h.shape
    nout = wt.shape[0]
    return pl.pallas_call(
        functools.partial(_proj_t_kernel, act=act),
        grid=(n // tm, nout // tn),
        in_specs=[
            pl.BlockSpec((tm, k), lambda i, j: (i, 0)),
            pl.BlockSpec((tn, k), lambda i, j: (j, 0)),
            pl.BlockSpec((tn, 1), lambda i, j: (j, 0)),
        ],
        out_specs=pl.BlockSpec((tn, tm), lambda i, j: (j, i)),
        out_shape=jax.ShapeDtypeStruct((nout, n), out_dtype),
        compiler_params=_params(("parallel", "parallel")),
        name=name,
    )(h, wt, bcol)


def _glu_kernel(h_ref, wa_ref, wb_ref, ba_ref, bb_ref, o_ref):
    h = h_ref[...]
    a = _mm_nt(h, wa_ref[...]) + ba_ref[...]
    b = _mm_nt(h, wb_ref[...]) + bb_ref[...]
    o_ref[...] = a * _sigmoid(b)


def _glu(h, wat, wbt, ba, bb, tm, tn):
    n, k = h.shape
    nout = wat.shape[0]
    wspec = pl.BlockSpec((tn, k), lambda i, j: (j, 0))
    bspec = pl.BlockSpec((1, tn), lambda i, j: (0, j))
    return pl.pallas_call(
        _glu_kernel,
        grid=(n // tm, nout // tn),
        in_specs=[pl.BlockSpec((tm, k), lambda i, j: (i, 0)), wspec, wspec, bspec, bspec],
        out_specs=pl.BlockSpec((tm, tn), lambda i, j: (i, j)),
        out_shape=jax.ShapeDtypeStruct((n, nout), F32),
        compiler_params=_params(("parallel", "parallel")),
        name="glu",
    )(h, wat, wbt, ba, bb)


def _kvt_kernel(h_ref, w_ref, b_ref, *o_refs):
    h = h_ref[...]
    for kind, o_ref in enumerate(o_refs):
        rows = pl.ds(kind * D_KV, D_KV)
        o_ref[0] = _mm_nt(w_ref[rows, :], h) + b_ref[rows, :]


def _kv_transposed(h, wt, bcol, batch, seq, tm):
    n, k = h.shape
    kinds = wt.shape[0] // D_KV
    nt = seq // tm
    return pl.pallas_call(
        _kvt_kernel,
        grid=(n // tm,),
        in_specs=[
            pl.BlockSpec((tm, k), lambda i: (i, 0)),
            pl.BlockSpec(wt.shape, lambda i: (0, 0)),
            pl.BlockSpec(bcol.shape, lambda i: (0, 0)),
        ],
        out_specs=[pl.BlockSpec((1, D_KV, tm), lambda i: (i // nt, 0, i % nt))] * kinds,
        out_shape=[jax.ShapeDtypeStruct((batch, D_KV, seq), F32)] * kinds,
        compiler_params=_params(("parallel",)),
        name="kv_transposed",
    )(h, wt, bcol)


def _merge_a_kernel(h_ref, c_ref, o1_ref, o2_ref, o3_ref, wga_ref, wgb_ref, wc_ref, wn_ref,
                    bga_ref, bgb_ref, bc_ref, m_ref):
    h = h_ref[...]
    ga = _sigmoid(_mm_nt(h, wga_ref[...]) + bga_ref[...])
    gb = _sigmoid(_mm_nt(h, wgb_ref[...]) + bgb_ref[...])
    yc = _mm(c_ref[...], wc_ref[...]) + bc_ref[...]
    o = (o1_ref[...] + o2_ref[...] + o3_ref[...]).astype(BF16)
    yn = _mm(o, wn_ref[...])
    m_ref[...] = (ga * yc + gb * yn).astype(m_ref.dtype)


def _merge_a(h, c, o1, o2, o3, wgat, wgbt, wc, wn, bga, bgb, bc, tm, tn):
    n, k = h.shape
    tok = lambda width: pl.BlockSpec((tm, width), lambda i, j: (i, 0))
    wt_spec = pl.BlockSpec((tn, k), lambda i, j: (j, 0))
    w_spec = pl.BlockSpec((c.shape[1], tn), lambda i, j: (0, j))
    b_spec = pl.BlockSpec((1, tn), lambda i, j: (0, j))
    return pl.pallas_call(
        _merge_a_kernel,
        grid=(n // tm, D_MODEL // tn),
        in_specs=[tok(k), tok(D_CONV), tok(D_Q), tok(D_Q), tok(D_Q), wt_spec, wt_spec, w_spec, w_spec,
                  b_spec, b_spec, b_spec],
        out_specs=pl.BlockSpec((tm, tn), lambda i, j: (i, j)),
        out_shape=jax.ShapeDtypeStruct((n, D_MODEL), BF16),
        compiler_params=_params(("parallel", "parallel")),
        name="merge_gates",
    )(h, c, o1, o2, o3, wgat, wgbt, wc, wn, bga, bgb, bc)


def _merge_b_kernel(x_ref, m_ref, wo_ref, gn_ref, wr_ref, br_ref, h2_all_ref, x1_ref, h2_ref, eid_ref, rw_ref,
                    cnt_ref):
    del h2_all_ref
    x1 = x_ref[...] + _mm(m_ref[...], wo_ref[...])
    x1_ref[...] = x1
    ms = jnp.mean(x1 * x1, axis=-1, keepdims=True)
    h2 = x1 * lax.rsqrt(ms + EPS) * gn_ref[...]
    h2_ref[...] = h2
    logits = _mm_nt(wr_ref[...], h2.astype(BF16)) + br_ref[...]
    tm = logits.shape[1]
    row = lax.broadcasted_iota(I32, (8, tm), 0)
    lg = jnp.where(row < N_GROUPS, logits[0:8], -jnp.inf)
    gmax = jnp.max(lg, axis=0, keepdims=True)
    gidx = jnp.min(jnp.where(lg == gmax, row, 8), axis=0, keepdims=True)
    wg = 1.0 / jnp.sum(jnp.exp(lg - gmax), axis=0, keepdims=True)
    le = jnp.zeros((8, tm), F32)
    for g in range(N_GROUPS):
        le = jnp.where(gidx == g, logits[8 + 8 * g:16 + 8 * g], le)
    ee = jnp.exp(le - jnp.max(le, axis=0, keepdims=True))
    pz = ee / jnp.sum(ee, axis=0, keepdims=True)
    p1 = jnp.max(pz, axis=0, keepdims=True)
    i1 = jnp.min(jnp.where(pz == p1, row, 8), axis=0, keepdims=True)
    pz2 = jnp.where(row == i1, -1.0, pz)
    p2 = jnp.max(pz2, axis=0, keepdims=True)
    i2 = jnp.min(jnp.where(pz2 == p2, row, 8), axis=0, keepdims=True)
    den = p1 + p2
    e1 = gidx * EXP_PER_GROUP + i1
    e2 = gidx * EXP_PER_GROUP + i2
    eid_ref[...] = jnp.where(row == 0, e1, jnp.where(row == 1, e2, 0))
    rw_ref[...] = jnp.where(row == 0, wg * p1 / den, jnp.where(row == 1, wg * p2 / den, 0.0))
    erow = lax.broadcasted_iota(I32, (N_EXPERTS, tm), 0)
    hits = (erow == e1).astype(F32) + (erow == e2).astype(F32)
    cnt_ref[...] = jnp.broadcast_to(jnp.sum(hits, axis=1, keepdims=True), (N_EXPERTS, LANES)).astype(I32)


def _merge_b(x, m, wo, gn, wr, br, h2_all, row0, tm):
    n, d = x.shape
    blk0 = row0 // tm
    return pl.pallas_call(
        _merge_b_kernel,
        grid=(n // tm,),
        in_specs=[
            pl.BlockSpec((tm, d), lambda i: (i, 0)),
            pl.BlockSpec((tm, d), lambda i: (i, 0)),
            pl.BlockSpec((d, d), lambda i: (0, 0)),
            pl.BlockSpec((1, d), lambda i: (0, 0)),
            pl.BlockSpec((LANES, d), lambda i: (0, 0)),
            pl.BlockSpec((LANES, 1), lambda i: (0, 0)),
            pl.BlockSpec(memory_space=pl.ANY),
        ],
        out_specs=[
            pl.BlockSpec((tm, d), lambda i: (i, 0)),
            pl.BlockSpec((tm, d), lambda i: (blk0 + i, 0)),
            pl.BlockSpec((8, tm), lambda i: (0, i)),
            pl.BlockSpec((8, tm), lambda i: (0, i)),
            pl.BlockSpec((None, N_EXPERTS, LANES), lambda i: (i, 0, 0)),
        ],
        out_shape=[
            jax.ShapeDtypeStruct((n, d), F32),
            jax.ShapeDtypeStruct(h2_all.shape, F32),
            jax.ShapeDtypeStruct((8, n), I32),
            jax.ShapeDtypeStruct((8, n), F32),
            jax.ShapeDtypeStruct((n // tm, N_EXPERTS, LANES), I32),
        ],
        input_output_aliases={6: 1},
        compiler_params=_params(("parallel",)),
        name="merge_out_router",
    )(x, m, wo, gn, wr, br, h2_all)


_HALO = 32


def _conv_prompt_kernel(u_ref, prev_ref, w_ref, b_ref, g_ref, beta_ref, o_ref, buf_ref, acc_ref, sh_ref, *, tt):
    i = pl.program_id(1)
    prev = prev_ref[pl.ds(tt - _HALO, _HALO), :]
    buf_ref[pl.ds(0, _HALO), :] = jnp.where(i > 0, prev, 0.0)
    buf_ref[pl.ds(_HALO, tt), :] = u_ref[...]
    off = _HALO - (CONV_W - 1)
    sub = 8
    rows = tt
    for cc in range(D_CONV // LANES):
        cols = pl.ds(cc * LANES, LANES)
        for r0 in range(0, tt, rows):
            acc = jnp.zeros((rows, LANES), F32)
            for r in range(sub):
                taps = [k for k in range(CONV_W) if (off + k) % sub == r]
                n_shift = (rows + _HALO - r) // sub * sub
                sh_ref[r, pl.ds(0, n_shift), :] = buf_ref[pl.ds(r0 + r, n_shift), cols]
                for k in taps:
                    acc = acc + sh_ref[r, pl.ds(off + k - r, rows), :] * w_ref[pl.ds(k, 1), cols]
            acc_ref[pl.ds(r0, rows), cols] = acc + b_ref[:, cols]
    c = acc_ref[...]
    mu = jnp.mean(c, axis=-1, keepdims=True)
    var = jnp.mean(jnp.square(c - mu), axis=-1, keepdims=True)
    y = (c - mu) * lax.rsqrt(var + EPS) * g_ref[...] + beta_ref[...]
    o_ref[...] = _silu(y).astype(o_ref.dtype)


def _conv_prompt(u, w, b, g, beta, batch, seq, tt):
    nt = seq // tt
    vec = pl.BlockSpec((1, D_CONV), lambda bi, i: (0, 0))
    return pl.pallas_call(
        functools.partial(_conv_prompt_kernel, tt=tt),
        grid=(batch, nt),
        in_specs=[
            pl.BlockSpec((tt, D_CONV), lambda bi, i: (bi * nt + i, 0)),
            pl.BlockSpec((tt, D_CONV), lambda bi, i: (bi * nt + jnp.maximum(i - 1, 0), 0)),
            pl.BlockSpec((CONV_W, D_CONV), lambda bi, i: (0, 0)),
            vec, vec, vec,
        ],
        out_specs=pl.BlockSpec((tt, D_CONV), lambda bi, i: (bi * nt + i, 0)),
        out_shape=jax.ShapeDtypeStruct((batch * seq, D_CONV), BF16),
        scratch_shapes=[pltpu.VMEM((tt + _HALO, D_CONV), F32), pltpu.VMEM((tt, D_CONV), F32),
                        pltpu.VMEM((8, tt + _HALO, LANES), F32)],
        compiler_params=_params(("parallel", "parallel")),
        name="conv_prompt",
    )(u, u, w, b, g, beta)


def _conv_sample_kernel(st_ref, u_ref, w_ref, b_ref, g_ref, beta_ref, c_ref, new_ref):
    hist = CONV_W - 1
    u = u_ref[...]
    acc = u * w_ref[pl.ds(hist, 1), :] + b_ref[...]
    for t in range(hist):
        acc = acc + st_ref[t] * w_ref[pl.ds(t, 1), :]
    mu = jnp.mean(acc, axis=-1, keepdims=True)
    var = jnp.mean(jnp.square(acc - mu), axis=-1, keepdims=True)
    y = (acc - mu) * lax.rsqrt(var + EPS) * g_ref[...] + beta_ref[...]
    c_ref[...] = _silu(y).astype(c_ref.dtype)
    for t in range(hist - 1):
        new_ref[t] = st_ref[t + 1]
    new_ref[hist - 1] = u


def _conv_sample(state_t, u, w, b, g, beta):
    hist, bs, _ = state_t.shape
    return pl.pallas_call(
        _conv_sample_kernel,
        out_shape=[jax.ShapeDtypeStruct((bs, D_CONV), BF16), jax.ShapeDtypeStruct((hist, bs, D_CONV), F32)],
        compiler_params=pltpu.CompilerParams(vmem_limit_bytes=48 * MIB),
        name="conv_sample",
    )(state_t, u, w, b, g, beta)


def _bias_lookup(rb_ref, head, dist, valid):
    n = jnp.maximum(dist, 0)
    max_exact = N_BUCKETS // 2
    nf = jnp.maximum(n, 1).astype(F32)
    large = max_exact + (jnp.log(nf / max_exact) / math.log(MAX_DIST / max_exact)
                         * (N_BUCKETS - max_exact)).astype(I32)
    bucket = jnp.where(n < max_exact, n, jnp.minimum(large, N_BUCKETS - 1))
    val = jnp.zeros(dist.shape, F32)
    for k in range(N_BUCKETS):
        val = jnp.where(bucket == k, rb_ref[head, k], val)
    return jnp.where(valid, val, NEG)


def _bias_prompt_kernel(rb_ref, cmp_ref, win_ref, sel_ref, *, nqt, n_chunk):
    g = pl.program_id(0)
    key = lax.broadcasted_iota(I32, (LANES, QT), 0)
    qry = lax.broadcasted_iota(I32, (LANES, QT), 1)
    for hh in range(HPG):
        head = g * HPG + hh
        cols = pl.ds(hh * QT, QT)

        def cmp_tile(qt, carry):
            for cb in range(n_chunk // LANES):
                dist = qt * QT + qry - (key + cb * LANES) * CMP_STRIDE - (CMP_LEN - 1)
                cmp_ref[0, qt, pl.ds(cb * LANES, LANES), cols] = _bias_lookup(rb_ref, head, dist, dist >= 0)
            return carry

        lax.fori_loop(0, nqt, cmp_tile, 0)
        for jb in range((WINDOW + QT) // LANES):
            dist = qry - (key + jb * LANES) + WINDOW
            win_ref[0, pl.ds(jb * LANES, LANES), cols] = _bias_lookup(
                rb_ref, head, dist, (dist >= 0) & (dist <= WINDOW))
        for kind in range(3):
            for jb in range(2):
                dist = kind * QT + qry - (key + jb * LANES)
                sel_ref[0, kind, pl.ds(jb * LANES, LANES), cols] = _bias_lookup(rb_ref, head, dist, dist >= 0)
        far = jnp.full((LANES, QT), 2 * MAX_DIST, I32)
        for jb in range(2):
            sel_ref[0, 3, pl.ds(jb * LANES, LANES), cols] = _bias_lookup(rb_ref, head, far, far >= 0)


def _bias_prompt(rb_t, nqt, n_chunk):
    return pl.pallas_call(
        functools.partial(_bias_prompt_kernel, nqt=nqt, n_chunk=n_chunk),
        grid=(N_KV,),
        in_specs=[pl.BlockSpec(memory_space=pltpu.SMEM)],
        out_specs=[
            pl.BlockSpec((1, nqt, n_chunk, GROUP_ROWS), lambda g: (g, 0, 0, 0)),
            pl.BlockSpec((1, WINDOW + QT, GROUP_ROWS), lambda g: (g, 0, 0)),
            pl.BlockSpec((1, 4, 2 * LANES, GROUP_ROWS), lambda g: (g, 0, 0, 0)),
        ],
        out_shape=[
            jax.ShapeDtypeStruct((N_KV, nqt, n_chunk, GROUP_ROWS), F32),
            jax.ShapeDtypeStruct((N_KV, WINDOW + QT, GROUP_ROWS), F32),
            jax.ShapeDtypeStruct((N_KV, 4, 2 * LANES, GROUP_ROWS), F32),
        ],
        compiler_params=_params(("parallel",)),
        name="bias_maps_prompt",
    )(rb_t)


def _bias_sample_kernel(rb_ref, cmp_ref, win_ref, sel_ref, *, past, n_chunk, n_pages):
    col = lax.broadcasted_iota(I32, (N_HEADS, LANES), 1)
    rb = rb_ref[...]

    def lookup(dist):
        n = jnp.maximum(dist, 0)
        max_exact = N_BUCKETS // 2
        nf = jnp.maximum(n, 1).astype(F32)
        large = max_exact + (jnp.log(nf / max_exact) / math.log(MAX_DIST / max_exact)
                             * (N_BUCKETS - max_exact)).astype(I32)
        bucket = jnp.where(n < max_exact, n, jnp.minimum(large, N_BUCKETS - 1))
        val = jnp.zeros(dist.shape, F32)
        for k in range(N_BUCKETS):
            val = jnp.where(bucket == k, rb[:, k:k + 1], val)
        return jnp.where(dist >= 0, val, NEG)

    def cmp_block(cb, carry):
        dist = past - (col + cb * LANES) * CMP_STRIDE - (CMP_LEN - 1)
        cmp_ref[cb] = lookup(dist)
        return carry

    lax.fori_loop(0, n_chunk // LANES, cmp_block, 0)

    def win_block(jb, carry):
        win_ref[jb] = lookup(WINDOW - (col + jb * LANES))
        return carry

    lax.fori_loop(0, WINDOW // LANES + 1, win_block, 0)

    def sel_page(p, carry):
        sel_ref[p] = lookup(past - (p * PAGE + col))
        return carry

    lax.fori_loop(0, n_pages + 1, sel_page, 0)


def _bias_sample(rb_t, past, n_chunk, n_pages):
    cmp_b, win_b, sel_b = pl.pallas_call(
        functools.partial(_bias_sample_kernel, past=past, n_chunk=n_chunk, n_pages=n_pages),
        out_shape=[
            jax.ShapeDtypeStruct((n_chunk // LANES, N_HEADS, LANES), F32),
            jax.ShapeDtypeStruct((WINDOW // LANES + 1, N_HEADS, LANES), F32),
            jax.ShapeDtypeStruct((n_pages + 1, N_HEADS, LANES), F32),
        ],
        name="bias_maps_sample",
    )(rb_t)
    flat = lambda a: a.transpose(1, 0, 2).reshape(N_HEADS, -1)
    return flat(cmp_b), flat(win_b), sel_b


_CHUNKS_PER_PAGE = PAGE // CMP_STRIDE
_CHUNK_PITCH = 24


def _compress_kernel(pt_ref, *refs, pp):
    page_refs = refs[:pp + 1]
    w0k_ref, w1k_ref, w0v_ref, w1v_ref, pe_ref, w2k_ref, w2v_ref, o_ref, tr_ref = refs[pp + 1:]
    del pt_ref
    nch = pp * _CHUNKS_PER_PAGE
    m = nch + 8
    n_slab = D_KV // LANES

    def transpose(s):
        for k in range(pp + 1):
            t = page_refs[k][pl.ds(s * LANES, LANES), :].T
            for n in range(_CHUNKS_PER_PAGE):
                row = (k * _CHUNKS_PER_PAGE + n) * _CHUNK_PITCH
                tr_ref[s, pl.ds(row, CMP_STRIDE), :] = t[n * CMP_STRIDE:(n + 1) * CMP_STRIDE]

    def chunks(s, _):
        return jnp.concatenate(
            [tr_ref.at[s][pl.ds(p, m, stride=_CHUNK_PITCH), :] for p in range(CMP_STRIDE)], axis=1)

    def mlp(s, x):
        is_v = s >= n_slab // 2
        w0 = (w0v_ref if is_v else w0k_ref)[...]
        w1 = (w1v_ref if is_v else w1k_ref)[...]
        w2 = (w2v_ref if is_v else w2k_ref)[...]
        pe0 = pe_ref[pl.ds(2 * int(is_v), 1), :]
        pe1 = pe_ref[pl.ds(2 * int(is_v) + 1, 1), :]
        y0 = _mm((x + pe0).astype(BF16), w0)
        y1 = _mm((x + pe1).astype(BF16), w1)
        pre = y0[0:nch] + y1[1:nch + 1]
        o_ref[0, :, pl.ds(s * LANES, LANES)] = _mm(_gelu_tanh(pre).astype(BF16), w2)

    _staggered(n_slab, transpose, chunks, mlp)


def _compress(pages, page_ids, weights, nb, n_pages, pp, paged):
    w0k, w1k, w0v, w1v, pe, w2k, w2v = weights
    steps = n_pages // pp
    nch = pp * _CHUNKS_PER_PAGE

    def page_spec(k):
        if paged:
            return pl.BlockSpec(
                (None, D_KV, PAGE),
                lambda b, s, pt: (pt[b * n_pages + jnp.minimum(s * pp + k, n_pages - 1)], 0, 0))
        return pl.BlockSpec((None, D_KV, PAGE), lambda b, s, pt: (b, 0, jnp.minimum(s * pp + k, n_pages - 1)))

    full = lambda a: pl.BlockSpec(a.shape, lambda b, s, pt: (0,) * a.ndim)
    grid_spec = pltpu.PrefetchScalarGridSpec(
        num_scalar_prefetch=1,
        grid=(nb, steps),
        in_specs=[page_spec(k) for k in range(pp + 1)] + [full(a) for a in weights],
        out_specs=pl.BlockSpec((1, nch, D_KV), lambda b, s, pt: (b, s, 0)),
        scratch_shapes=[pltpu.VMEM((D_KV // LANES, (pp + 1) * _CHUNKS_PER_PAGE * _CHUNK_PITCH, LANES), F32)],
    )
    return pl.pallas_call(
        functools.partial(_compress_kernel, pp=pp),
        grid_spec=grid_spec,
        out_shape=jax.ShapeDtypeStruct((nb, n_pages * _CHUNKS_PER_PAGE, D_KV), F32),
        compiler_params=_params(("parallel", "parallel"), 56),
        name="compress_paged" if paged else "compress_prompt",
    )(page_ids, *([pages] * (pp + 1)), w0k, w1k, w0v, w1v, pe, w2k, w2v)


def _compress_weights(pe_k, w1_k, w2_k, pe_v, w1_v, w2_v):
    eye = jnp.eye(2, dtype=F32)

    def first(w1, j):
        w = w1[j * CMP_STRIDE:(j + 1) * CMP_STRIDE]
        return jnp.einsum("pdf,gh->pgdhf", w, eye).reshape(CMP_STRIDE * LANES, 2 * CMP_HID).astype(BF16)

    def pos(pe, j):
        return jnp.tile(pe[j * CMP_STRIDE:(j + 1) * CMP_STRIDE, None, :], (1, 2, 1)).reshape(1, CMP_STRIDE * LANES)

    def second(w2):
        return jnp.einsum("fd,gh->gfhd", w2, eye).reshape(2 * CMP_HID, 2 * HEAD_DIM).astype(BF16)

    pe = jnp.concatenate([pos(pe_k, 0), pos(pe_k, 1), pos(pe_v, 0), pos(pe_v, 1)], axis=0)
    return (first(w1_k, 0), first(w1_k, 1), first(w1_v, 0), first(w1_v, 1), pe, second(w2_k), second(w2_v))


def _staggered(n, scores, softmax, values):
    s = {0: scores(0)}
    out = []
    for g in range(n):
        if g + 1 < n:
            s[g + 1] = scores(g + 1)
        out.append(values(g, softmax(g, s.pop(g))))
    return out


def _group_queries(qt_ref, g):
    return jnp.concatenate(
        [qt_ref[pl.ds((g * HPG + hh) * HEAD_DIM, HEAD_DIM), :] for hh in range(HPG)], axis=1)


def _store_heads(o_ref, o_t, gate, g, branch):
    lane = lax.broadcasted_iota(I32, (QT, 2 * HEAD_DIM), 1)
    for pair in range(HPG // 2):
        h = g * HPG + 2 * pair
        col = branch * N_HEADS + h
        both = jnp.concatenate([o_t[:, (2 * pair) * QT:(2 * pair + 1) * QT],
                                o_t[:, (2 * pair + 1) * QT:(2 * pair + 2) * QT]], axis=0)
        gates = jnp.where(lane < HEAD_DIM, gate[:, col:col + 1], gate[:, col + 1:col + 2])
        o_ref[:, pl.ds(h * HEAD_DIM, 2 * HEAD_DIM)] = both.T * gates


def _cmp_prompt_kernel(qt_ref, kc_ref, map_ref, gate_ref, ov_ref, o_ref, sb_ref, *, n_sel, top):
    qt = pl.program_id(1)
    kv = kc_ref[0]
    gate = gate_ref[...]
    ov = ov_ref[...]
    half = D_KV // 2
    vt_all = kv[:, half:].T
    jrow = lax.broadcasted_iota(I32, (n_sel, QT), 0)
    tpos = qt * QT + lax.broadcasted_iota(I32, (n_sel, QT), 1)
    cur = tpos // SEL_LEN
    forced = (jrow == 0) | (jrow == cur) | (jrow == cur - 1)
    valid = jrow * SEL_LEN <= tpos
    def scores(g):
        kg = kv[:, g * HEAD_DIM:(g + 1) * HEAD_DIM].astype(BF16)
        return _mm(kg, _group_queries(qt_ref, g)) + map_ref[g, 0]

    def softmax(g, st):
        e = jnp.exp(st - jnp.max(st, axis=0, keepdims=True))
        p = e / jnp.sum(e, axis=0, keepdims=True) * (map_ref[g, 0] > 0.5 * NEG).astype(F32)
        return p.astype(BF16)

    def values(g, pb):
        vgt = vt_all[g * HEAD_DIM:(g + 1) * HEAD_DIM].astype(BF16)
        _store_heads(o_ref, _mm(vgt, pb), gate, g, 0)
        return pb

    probs = _staggered(N_KV, scores, softmax, values)
    for g in range(N_KV):
        pb = probs[g]
        imp = jnp.zeros((n_sel, QT), F32)
        for hh in range(HPG):
            imp = imp + _mm(ov, pb[:, hh * QT:(hh + 1) * QT])
        score = jnp.where(forced, jnp.inf, jnp.where(valid, imp, -jnp.inf))
        rank = jnp.zeros((n_sel, QT), I32)
        for i in range(n_sel):
            si = score[i:i + 1, :]
            ahead = (si > score) | ((si == score) & (jrow > i))
            rank = rank + ahead.astype(I32)
        sb_ref[pl.ds(g * n_sel, n_sel), :] = jnp.where(rank < top, 0.0, NEG).astype(sb_ref.dtype)


def _cmp_prompt(q_t, kcmp, cmap, gates, ov_t, batch, seq):
    nqt = seq // QT
    n_chunk = kcmp.shape[1]
    n_sel = ov_t.shape[0]
    top = min(SEL_TOP, n_sel)
    return pl.pallas_call(
        functools.partial(_cmp_prompt_kernel, n_sel=n_sel, top=top),
        grid=(batch, nqt),
        in_specs=[
            pl.BlockSpec((D_Q, QT), lambda b, i: (0, b * nqt + i)),
            pl.BlockSpec((1, n_chunk, D_KV), lambda b, i: (b, 0, 0)),
            pl.BlockSpec((N_KV, 1, n_chunk, GROUP_ROWS), lambda b, i: (0, i, 0, 0)),
            pl.BlockSpec((QT, LANES), lambda b, i: (b * nqt + i, 0)),
            pl.BlockSpec(ov_t.shape, lambda b, i: (0, 0)),
        ],
        out_specs=[
            pl.BlockSpec((QT, D_Q), lambda b, i: (b * nqt + i, 0)),
            pl.BlockSpec((None, N_KV * n_sel, QT), lambda b, i: (b * nqt + i, 0, 0)),
        ],
        out_shape=[
            jax.ShapeDtypeStruct((batch * seq, D_Q), F32),
            jax.ShapeDtypeStruct((batch * nqt, N_KV * n_sel, QT), BF16),
        ],
        compiler_params=_params(("parallel", "parallel")),
        name="cmp_attn_prompt",
    )(q_t, kcmp, cmap, gates, ov_t)


_TK = 2 * LANES


def _sel_prompt_kernel(qt_ref, kv_ref, sb_ref, map_ref, gate_ref, o_ref, ka_ref, vt_ref, qa_ref, *, n_sel, seq):
    qt = pl.program_id(1)
    aug = 2 * HEAD_DIM
    half = D_KV // 2

    @pl.when(qt == 0)
    def _():
        pos = lax.broadcasted_iota(I32, (seq, aug - HEAD_DIM), 0)
        blk = lax.broadcasted_iota(I32, (seq, aug - HEAD_DIM), 1)
        onehot = jnp.where(pos // SEL_LEN == blk, 1.0, 0.0)
        k_rows = kv_ref[0, pl.ds(0, half), :].T
        for g in range(N_KV):
            ka_ref[g] = jnp.concatenate([k_rows[:, g * HEAD_DIM:(g + 1) * HEAD_DIM], onehot], axis=1).astype(BF16)
            vt_ref[g] = kv_ref[0, pl.ds(half + g * HEAD_DIM, HEAD_DIM), :].astype(BF16)

    gate = gate_ref[...]
    sb = sb_ref[...]
    diag = qt // 2
    n_tiles = diag + 1
    pad = jnp.zeros((aug - HEAD_DIM - n_sel, GROUP_ROWS), BF16)
    for g in range(N_KV):
        sbg = sb[g * n_sel:(g + 1) * n_sel]
        qa_ref[g] = jnp.concatenate([_group_queries(qt_ref, g), jnp.concatenate([sbg] * HPG, axis=1), pad], axis=0)

    def tile(kt, carry):
        start = pl.multiple_of(kt * _TK, _TK)
        kind = jnp.where(kt == diag, qt % 2, jnp.where((kt == diag - 1) & (qt % 2 == 0), 2, 3))
        def scores(g):
            return _mm(ka_ref[g, pl.ds(start, _TK), :], qa_ref[g]) + map_ref[g, kind]

        def softmax(g, st):
            m_i, l_i, _ = carry[g]
            m_new = jnp.maximum(m_i, jnp.max(st, axis=0, keepdims=True))
            alpha = jnp.exp(m_i - m_new)
            p = jnp.exp(st - m_new)
            return m_new, alpha * l_i + jnp.sum(p, axis=0, keepdims=True), alpha, p.astype(BF16)

        def values(g, sm):
            m_new, l_new, alpha, pb = sm
            return m_new, l_new, alpha * carry[g][2] + _mm(vt_ref[g, :, pl.ds(start, _TK)], pb)

        return tuple(_staggered(N_KV, scores, softmax, values))

    init = tuple((jnp.full((1, GROUP_ROWS), -jnp.inf, F32), jnp.zeros((1, GROUP_ROWS), F32),
                  jnp.zeros((HEAD_DIM, GROUP_ROWS), F32)) for _ in range(N_KV))
    final = lax.fori_loop(0, n_tiles, tile, init)
    for g in range(N_KV):
        _, l_f, acc = final[g]
        _store_heads(o_ref, acc / l_f, gate, g, 1)


def _sel_prompt(q_t, kv_t, selbias, smap, gates, batch, seq):
    nqt = seq // QT
    n_sel = selbias.shape[1] // N_KV
    return pl.pallas_call(
        functools.partial(_sel_prompt_kernel, n_sel=n_sel, seq=seq),
        grid=(batch, nqt),
        in_specs=[
            pl.BlockSpec((D_Q, QT), lambda b, i: (0, b * nqt + i)),
            pl.BlockSpec((1, D_KV, seq), lambda b, i: (b, 0, 0)),
            pl.BlockSpec((None, N_KV * n_sel, QT), lambda b, i: (b * nqt + i, 0, 0)),
            pl.BlockSpec(smap.shape, lambda b, i: (0, 0, 0, 0)),
            pl.BlockSpec((QT, LANES), lambda b, i: (b * nqt + i, 0)),
        ],
        out_specs=pl.BlockSpec((QT, D_Q), lambda b, i: (b * nqt + i, 0)),
        out_shape=jax.ShapeDtypeStruct((batch * seq, D_Q), F32),
        scratch_shapes=[pltpu.VMEM((N_KV, seq, 2 * HEAD_DIM), BF16), pltpu.VMEM((N_KV, HEAD_DIM, seq), BF16),
                        pltpu.VMEM((N_KV, 2 * HEAD_DIM, GROUP_ROWS), BF16)],
        compiler_params=_params(("arbitrary", "arbitrary"), 56),
        name="sel_attn_prompt",
    )(q_t, kv_t, selbias, smap, gates)


_WIN_TILES = WINDOW // QT + 1


def _win_prompt_kernel(qt_ref, *refs):
    kv_refs = refs[:_WIN_TILES]
    map_ref, gate_ref, o_ref = refs[_WIN_TILES:]
    qt = pl.program_id(1)
    gate = gate_ref[...]
    half = D_KV // 2
    span = _WIN_TILES * QT
    key = lax.broadcasted_iota(I32, (span, GROUP_ROWS), 0)
    in_seq = key >= (_WIN_TILES - 1 - qt) * QT
    k_rows = jnp.concatenate([r[0, pl.ds(0, half), :].T for r in kv_refs], axis=0)
    def scores(g):
        kg = k_rows[:, g * HEAD_DIM:(g + 1) * HEAD_DIM].astype(BF16)
        return jnp.where(in_seq, _mm(kg, _group_queries(qt_ref, g)) + map_ref[g], NEG)

    def softmax(g, st):
        e = jnp.exp(st - jnp.max(st, axis=0, keepdims=True))
        return (e / jnp.sum(e, axis=0, keepdims=True)).astype(BF16)

    def values(g, pb):
        vgt = jnp.concatenate([r[0, pl.ds(half + g * HEAD_DIM, HEAD_DIM), :] for r in kv_refs],
                              axis=1).astype(BF16)
        _store_heads(o_ref, _mm(vgt, pb), gate, g, 2)

    _staggered(N_KV, scores, softmax, values)


def _win_prompt(q_t, kv_t, wmap, gates, batch, seq):
    nqt = seq // QT

    def kv_spec(k):
        return pl.BlockSpec((1, D_KV, QT), lambda b, i: (b, 0, jnp.maximum(i - (_WIN_TILES - 1) + k, 0)))

    return pl.pallas_call(
        _win_prompt_kernel,
        grid=(batch, nqt),
        in_specs=[pl.BlockSpec((D_Q, QT), lambda b, i: (0, b * nqt + i))]
        + [kv_spec(k) for k in range(_WIN_TILES)]
        + [pl.BlockSpec(wmap.shape, lambda b, i: (0, 0, 0)),
           pl.BlockSpec((QT, LANES), lambda b, i: (b * nqt + i, 0))],
        out_specs=pl.BlockSpec((QT, D_Q), lambda b, i: (b * nqt + i, 0)),
        out_shape=jax.ShapeDtypeStruct((batch * seq, D_Q), F32),
        compiler_params=_params(("parallel", "parallel")),
        name="win_attn_prompt",
    )(q_t, *([kv_t] * _WIN_TILES), wmap, gates)


def _group_rows(x_by_group):
    row = lax.broadcasted_iota(I32, x_by_group[0].shape, 0)
    out = x_by_group[0]
    for g in range(1, N_KV):
        out = jnp.where(row // HPG == g, x_by_group[g], out)
    return out


_CMP_SAMPLE_ROWS = 4


def _cmp_sample_kernel(q_ref, kc_ref, bias_ref, ov_ref, gate_ref, o_ref, idx_ref, *, n_sel, past):
    bias = bias_ref[...]
    half = D_KV // 2
    imps = []
    for bl in range(_CMP_SAMPLE_ROWS):
        q = q_ref[bl]
        kv = kc_ref[bl].astype(BF16)
        s = _group_rows([_mm_nt(q, kv[:, g * HEAD_DIM:(g + 1) * HEAD_DIM]) for g in range(N_KV)]) + bias
        e = jnp.exp(s - jnp.max(s, axis=-1, keepdims=True))
        p = e / jnp.sum(e, axis=-1, keepdims=True) * (bias > 0.5 * NEG).astype(F32)
        pb = p.astype(BF16)
        o = _group_rows([_mm(pb, kv[:, half + g * HEAD_DIM:half + (g + 1) * HEAD_DIM]) for g in range(N_KV)])
        o_ref[bl] = o * gate_ref[bl]
        imp_h = _mm(pb, ov_ref[...])
        imps += [imp_h[g * HPG:g * HPG + 1] + imp_h[g * HPG + 1:g * HPG + 2] + imp_h[g * HPG + 2:g * HPG + 3]
                 + imp_h[g * HPG + 3:g * HPG + 4] for g in range(N_KV)]
    imp = jnp.concatenate(imps, axis=0)
    rows, nsp = imp.shape
    j = lax.broadcasted_iota(I32, (rows, nsp), 1)
    cur = past // SEL_LEN
    forced = (j == 0) | (j == cur) | (j == cur - 1)
    valid = (j * SEL_LEN <= past) & (j < n_sel)
    score = jnp.where(valid, jnp.where(forced, jnp.inf, imp), -jnp.inf)
    slot = lax.broadcasted_iota(I32, (rows, LANES), 1)
    res = jnp.zeros((rows, LANES), I32)
    for k in range(min(SEL_TOP, n_sel)):
        best = jnp.max(score, axis=-1, keepdims=True)
        pick = jnp.min(jnp.where(score == best, j, nsp), axis=-1, keepdims=True)
        res = jnp.where(slot == k, pick, res)
        score = jnp.where(j == pick, -jnp.inf, score)
    idx_ref[...] = res


def _cmp_sample(q3, kcmp, bias, ov, gate3, n_sel, past):
    bs, n_chunk, _ = kcmp.shape
    nb = _CMP_SAMPLE_ROWS
    heads = pl.BlockSpec((nb, N_HEADS, HEAD_DIM), lambda b: (b, 0, 0))
    return pl.pallas_call(
        functools.partial(_cmp_sample_kernel, n_sel=n_sel, past=past),
        grid=(bs // nb,),
        in_specs=[
            heads,
            pl.BlockSpec((nb, n_chunk, D_KV), lambda b: (b, 0, 0)),
            pl.BlockSpec(bias.shape, lambda b: (0, 0)),
            pl.BlockSpec(ov.shape, lambda b: (0, 0)),
            heads,
        ],
        out_specs=[heads, pl.BlockSpec((nb * N_KV, LANES), lambda b: (b, 0))],
        out_shape=[
            jax.ShapeDtypeStruct((bs, N_HEADS, HEAD_DIM), F32),
            jax.ShapeDtypeStruct((bs * N_KV, LANES), I32),
        ],
        compiler_params=_params(("parallel",)),
        name="cmp_attn_sample",
    )(q3, kcmp, bias, ov, gate3)


def _sel_sample_kernel(blk_ref, phys_ref, q_ref, new_ref, bias_ref, gate_ref, *refs, top, n_pages):
    k_refs = refs[:top]
    v_refs = refs[top:2 * top]
    o_ref = refs[2 * top]
    del phys_ref
    b = pl.program_id(0)
    g = pl.program_id(1)
    q = q_ref[0, 0]
    lane = lax.broadcasted_iota(I32, (8, PAGE), 1)
    scores = []
    for k in range(top):
        blk = blk_ref[(b * N_KV + g) * top + k]
        page = jnp.minimum(blk // 2, n_pages)
        s = _mm(q, k_refs[k][...].astype(BF16)) + bias_ref[page]
        ok = (lane // SEL_LEN == blk % 2) & (blk < 2 * n_pages)
        scores.append(jnp.where(ok, s, NEG))
    k_new = new_ref[0, 0, 0:1, :].astype(BF16).astype(F32)
    v_new = new_ref[0, 0, 1:2, :].astype(BF16).astype(F32)
    s_new = jnp.sum(q.astype(F32) * k_new, axis=-1, keepdims=True) + bias_ref[n_pages]
    scores.append(s_new)
    s_all = jnp.concatenate(scores, axis=1)
    e = jnp.exp(s_all - jnp.max(s_all, axis=-1, keepdims=True))
    p = e / jnp.sum(e, axis=-1, keepdims=True)
    o = p[:, top * PAGE:top * PAGE + 1].astype(BF16).astype(F32) * v_new
    for k in range(top):
        o = o + _mm_nt(p[:, k * PAGE:(k + 1) * PAGE].astype(BF16), v_refs[k][...].astype(BF16))
    o_ref[0, 0] = o * gate_ref[0, 0]


def _sel_sample(blk, phys, q4, new4, bias4, gate4, pool4, top, n_pages):
    bs = q4.shape[0]

    def kv_spec(k, is_v):
        return pl.BlockSpec(
            (None, None, HEAD_DIM, PAGE),
            lambda b, g, blk_r, phys_r: (phys_r[(b * N_KV + g) * top + k], g + N_KV * is_v, 0, 0))

    grp = lambda b, g, blk_r, phys_r: (b, g, 0, 0)
    grid_spec = pltpu.PrefetchScalarGridSpec(
        num_scalar_prefetch=2,
        grid=(bs, N_KV),
        in_specs=[
            pl.BlockSpec((1, 1, 8, HEAD_DIM), grp),
            pl.BlockSpec((1, 1, 8, HEAD_DIM), grp),
            pl.BlockSpec((None, n_pages + 1, 8, LANES), lambda b, g, blk_r, phys_r: (g, 0, 0, 0)),
            pl.BlockSpec((1, 1, 8, HEAD_DIM), grp),
        ] + [kv_spec(k, 0) for k in range(top)] + [kv_spec(k, 1) for k in range(top)],
        out_specs=pl.BlockSpec((1, 1, 8, HEAD_DIM), grp),
    )
    return pl.pallas_call(
        functools.partial(_sel_sample_kernel, top=top, n_pages=n_pages),
        grid_spec=grid_spec,
        out_shape=jax.ShapeDtypeStruct((bs, N_KV, 8, HEAD_DIM), F32),
        compiler_params=_params(("parallel", "parallel")),
        name="sel_attn_sample",
    )(blk, phys, q4, new4, bias4, gate4, *([pool4] * (2 * top)))


def _win_sample_kernel(q_ref, new_ref, st_ref, col_ref, bias_ref, gate_ref, o_ref, nst_ref):
    st = st_ref[0]
    half = D_KV // 2
    for g in range(N_KV):
        q = q_ref[0, g]
        bias = bias_ref[g]
        s_old = _mm(q, st[g * HEAD_DIM:(g + 1) * HEAD_DIM, :].astype(BF16)) + bias[:, 0:WINDOW]
        k_new = new_ref[0, g, 0:1, :].astype(BF16).astype(F32)
        v_new = new_ref[0, g, 1:2, :].astype(BF16).astype(F32)
        s_new = jnp.sum(q.astype(F32) * k_new, axis=-1, keepdims=True) + bias[:, WINDOW:WINDOW + LANES]
        s_all = jnp.concatenate([s_old, s_new], axis=1)
        e = jnp.exp(s_all - jnp.max(s_all, axis=-1, keepdims=True))
        p = e / jnp.sum(e, axis=-1, keepdims=True)
        o = _mm_nt(p[:, 0:WINDOW].astype(BF16), st[half + g * HEAD_DIM:half + (g + 1) * HEAD_DIM, :].astype(BF16))
        o = o + p[:, WINDOW:WINDOW + 1].astype(BF16).astype(F32) * v_new
        o_ref[0, g] = o * gate_ref[0, g]
    rolled = pltpu.roll(st, WINDOW - 1, axis=1)
    lane = lax.broadcasted_iota(I32, (D_KV, LANES), 1)
    nst_ref[0, :, pl.ds(0, WINDOW - LANES)] = rolled[:, 0:WINDOW - LANES]
    nst_ref[0, :, pl.ds(WINDOW - LANES, LANES)] = jnp.where(
        lane == LANES - 1, col_ref[0], rolled[:, WINDOW - LANES:WINDOW])


def _win_sample(q4, new4, state_t, newcol, bias4, gate4):
    bs = q4.shape[0]
    grp = pl.BlockSpec((1, N_KV, 8, HEAD_DIM), lambda b: (b, 0, 0, 0))
    return pl.pallas_call(
        _win_sample_kernel,
        grid=(bs,),
        in_specs=[
            grp, grp,
            pl.BlockSpec((1, D_KV, WINDOW), lambda b: (b, 0, 0)),
            pl.BlockSpec((1, D_KV, LANES), lambda b: (b, 0, 0)),
            pl.BlockSpec(bias4.shape, lambda b: (0, 0, 0)),
            grp,
        ],
        out_specs=[grp, pl.BlockSpec((1, D_KV, WINDOW), lambda b: (b, 0, 0))],
        out_shape=[
            jax.ShapeDtypeStruct((bs, N_KV, 8, HEAD_DIM), F32),
            jax.ShapeDtypeStruct((bs, D_KV, WINDOW), F32),
        ],
        compiler_params=_params(("parallel",)),
        name="win_attn_sample",
    )(q4, new4, state_t, newcol, bias4, gate4)


_MOE_ROWS = 256
_MOE_SLACK = 2 * _MOE_ROWS


def _row_copy(src, src_row, dst, dst_row, sem):
    return pltpu.make_async_copy(src.at[pl.ds(src_row, 1)], dst.at[pl.ds(dst_row, 1)], sem)


def _expert_kernel(be_ref, nu_ref, dst_ref, src_ref, nxt_ref, h_ref, w_ref, wg_ref, wu_ref, wd_ref, yk_ref,
                   xbuf, ybuf, gsem, ssem, *, n_rows, n_live):
    del be_ref
    i = pl.program_id(0)
    slot = i % 2
    n_used = nu_ref[0]

    def gather(rows_ref, dst_slot):
        for r in range(_MOE_ROWS):
            _row_copy(h_ref, rows_ref[0, r], xbuf.at[dst_slot], r, gsem.at[dst_slot]).start()

    def wait_gather(s):
        for r in range(_MOE_ROWS):
            _row_copy(h_ref, 0, xbuf.at[s], 0, gsem.at[s]).wait()

    def scatter(s):
        for r in range(_MOE_ROWS):
            _row_copy(ybuf.at[s], r, yk_ref, dst_ref[0, r], ssem.at[s]).start()

    def wait_scatter(s):
        for r in range(_MOE_ROWS):
            _row_copy(ybuf.at[s], 0, yk_ref, 0, ssem.at[s]).wait()

    @pl.when(i == 0)
    def _():
        gather(src_ref, 0)
        ybuf[1] = jnp.zeros((_MOE_ROWS, ybuf.shape[2]), F32)
        fills = [
            pltpu.make_async_copy(ybuf.at[1, pl.ds(0, min(_MOE_ROWS, n_rows - off))],
                                  yk_ref.at[pl.ds(k * n_rows + off, min(_MOE_ROWS, n_rows - off))], ssem.at[1])
            for k in range(TOP_K) for off in range(n_live, n_rows, _MOE_ROWS)]
        for cp in fills:
            cp.start()
        for cp in fills:
            cp.wait()

    @pl.when(i + 1 < n_used)
    def _():
        gather(nxt_ref, 1 - slot)

    @pl.when(i < n_used)
    def _():
        wait_gather(slot)

        @pl.when(i >= 2)
        def _():
            wait_scatter(slot)

        x = xbuf[slot].astype(BF16)
        a = _silu(_mm(x, wg_ref[...])) * _mm(x, wu_ref[...])
        ybuf[slot] = _mm(a.astype(BF16), wd_ref[...]) * w_ref[...]
        scatter(slot)

        @pl.when(i == n_used - 1)
        def _():
            @pl.when(i >= 1)
            def _():
                wait_scatter(1 - slot)

            wait_scatter(slot)


def _experts(blk_e, n_used, rows_src, rows_dst, rows_w, h2_all, n_live, w_gate, w_up, w_down):
    n_rows, d = h2_all.shape
    n_blk = blk_e.shape[0]
    src3 = rows_src.reshape(n_blk, 1, _MOE_ROWS)
    dst3 = rows_dst.reshape(n_blk, 1, _MOE_ROWS)
    wspec = lambda a: pl.BlockSpec((None,) + a.shape[1:], lambda i, be, nu: (be[i], 0, 0))
    smem_blk = lambda fn: pl.BlockSpec((None, 1, _MOE_ROWS), fn, memory_space=pltpu.SMEM)
    grid_spec = pltpu.PrefetchScalarGridSpec(
        num_scalar_prefetch=2,
        grid=(n_blk,),
        in_specs=[
            smem_blk(lambda i, be, nu: (i, 0, 0)),
            smem_blk(lambda i, be, nu: (i, 0, 0)),
            smem_blk(lambda i, be, nu: (jnp.minimum(i + 1, n_blk - 1), 0, 0)),
            pl.BlockSpec(memory_space=pl.ANY),
            pl.BlockSpec((_MOE_ROWS, 1), lambda i, be, nu: (i, 0)),
            wspec(w_gate), wspec(w_up), wspec(w_down),
        ],
        out_specs=pl.BlockSpec(memory_space=pl.ANY),
        scratch_shapes=[
            pltpu.VMEM((2, _MOE_ROWS, d), F32),
            pltpu.VMEM((2, _MOE_ROWS, d), F32),
            pltpu.SemaphoreType.DMA((2,)),
            pltpu.SemaphoreType.DMA((2,)),
        ],
    )
    return pl.pallas_call(
        functools.partial(_expert_kernel, n_rows=n_rows, n_live=n_live),
        grid_spec=grid_spec,
        out_shape=jax.ShapeDtypeStruct((TOP_K * n_rows, d), F32),
        compiler_params=pltpu.CompilerParams(dimension_semantics=("arbitrary",), vmem_limit_bytes=56 * MIB,
                                             disable_bounds_checks=True),
        name="moe_experts",
    )(blk_e, n_used, dst3, src3, src3, h2_all, rows_w, w_gate, w_up, w_down)


def _combine_kernel(x_ref, y0_ref, y1_ref, g_ref, o_ref):
    x = x_ref[...] + (y0_ref[...] + y1_ref[...])
    ms = jnp.mean(x * x, axis=-1, keepdims=True)
    o_ref[...] = x * lax.rsqrt(ms + EPS) * g_ref[...]


def _combine(x1, yk, gfin, row0, tm):
    n, d = x1.shape
    yk3 = yk.reshape(TOP_K, -1, d)
    blk0 = row0 // tm
    return pl.pallas_call(
        _combine_kernel,
        grid=(n // tm,),
        in_specs=[
            pl.BlockSpec((tm, d), lambda i: (i, 0)),
            pl.BlockSpec((None, tm, d), lambda i: (0, blk0 + i, 0)),
            pl.BlockSpec((None, tm, d), lambda i: (1, blk0 + i, 0)),
            pl.BlockSpec((1, d), lambda i: (0, 0)),
        ],
        out_specs=pl.BlockSpec((tm, d), lambda i: (i, 0)),
        out_shape=jax.ShapeDtypeStruct((n, d), F32),
        compiler_params=_params(("parallel",)),
        name="moe_combine",
    )(x1, yk3, yk3, gfin)


def _dispatch(eid, w, counts, n_rows):
    a = eid.shape[0]
    tb = _MOE_ROWS
    order = jnp.argsort(eid).astype(I32)
    padded = (counts + tb - 1) // tb * tb
    seg_end = jnp.cumsum(padded)
    seg_start = seg_end - padded
    start = jnp.cumsum(counts) - counts
    n_blk = (a + N_EXPERTS * (tb - 1) + tb - 1) // tb
    blk_e = jnp.minimum(jnp.sum((seg_end[None, :] <= (jnp.arange(n_blk) * tb)[:, None]).astype(I32), axis=1),
                        N_EXPERTS - 1)
    blk = jnp.arange(n_blk)[:, None]
    r = jnp.arange(tb)[None, :]
    off = (blk * tb - seg_start[blk_e][:, None]) + r
    valid = off < counts[blk_e][:, None]
    asg = order[jnp.clip(start[blk_e][:, None] + off, 0, a - 1)]
    rows_src = jnp.where(valid, asg // TOP_K, 0).astype(I32)
    rows_dst = jnp.where(valid, (asg % TOP_K) * n_rows + asg // TOP_K, (blk % 2) * n_rows + (n_rows - tb) + r)
    rows_w = jnp.where(valid, w[asg], 0.0)
    n_used = (seg_end[-1] // tb).astype(I32).reshape(1)
    return rows_src, rows_dst.astype(I32), rows_w.reshape(-1, 1), blk_e.astype(I32), n_used


def _overlap(n_chunk, n_sel, n_sel_pad):
    cs = jnp.arange(n_chunk)[:, None] * CMP_STRIDE
    ss = jnp.arange(n_sel_pad)[None, :] * SEL_LEN
    hit = (cs < ss + SEL_LEN) & (cs + CMP_LEN > ss) & (jnp.arange(n_sel_pad)[None, :] < n_sel)
    return hit.astype(BF16)


def kernel(x_prompt, x_sample, cache_cmp_kv, cache_sel_kv, state_win_kv, state_conv, page_table, rel_bias,
           norm_mix, w_in, b_in, conv_w, conv_b, conv_ln_g, conv_ln_b, w_conv_out, b_conv_out,
           cmp_pe_k, cmp_w1_k, cmp_w2_k, cmp_pe_v, cmp_w1_v, cmp_w2_v, w_nsa_out, w_out, norm_ffn,
           w_rg, b_rg, w_re, b_re, w_gate, w_up, w_down, norm_final):
    batch, seq, _ = x_prompt.shape
    bs = x_sample.shape[0]
    n_pool = cache_cmp_kv.shape[1]
    n_pages = page_table.shape[1]
    past = n_pages * PAGE
    n_tok = batch * seq
    assert bs + _MOE_ROWS <= _MOE_SLACK and n_tok % bs == 0 and seq % _TK == 0 and bs % 16 == 0
    l = 0

    wt = w_in[l].T.astype(BF16)
    bias_in = b_in[l]
    cuts = [0, D_CONV, 2 * D_CONV, 2 * D_CONV + D_Q, 2 * D_CONV + D_Q + 3 * D_KV]
    cuts += [cuts[-1] + 3 * N_HEADS, cuts[-1] + 3 * N_HEADS + D_MODEL, cuts[-1] + 3 * N_HEADS + 2 * D_MODEL]
    seg = lambda k: (wt[cuts[k]:cuts[k + 1]], bias_in[cuts[k]:cuts[k + 1]][None, :])
    (wa, ba), (wb, bb), (wq, bq), (wkv, bkv), (wzg, bzg), (wga, bga), (wgb, bgb) = [seg(k) for k in range(7)]
    wzg = jnp.pad(wzg, ((0, LANES - 3 * N_HEADS), (0, 0)))
    bzg = jnp.pad(bzg, ((0, 0), (0, LANES - 3 * N_HEADS)))
    wc = w_conv_out[l].astype(BF16)
    wn = w_nsa_out[l].astype(BF16)
    wo = w_out[l].astype(BF16)
    wr = jnp.zeros((LANES, D_MODEL), F32).at[0:N_GROUPS].set(w_rg[l].T).at[8:8 + N_EXPERTS].set(w_re[l].T)
    br = jnp.zeros((LANES,), F32).at[0:N_GROUPS].set(b_rg[l]).at[8:8 + N_EXPERTS].set(b_re[l].reshape(-1))
    wr, br = wr.astype(BF16), br[:, None]
    cw = _compress_weights(cmp_pe_k[l], cmp_w1_k[l], cmp_w2_k[l], cmp_pe_v[l], cmp_w1_v[l], cmp_w2_v[l])
    rb_t = rel_bias.T
    gmix, gffn, gfin = norm_mix[l][None, :], norm_ffn[l][None, :], norm_final[None, :]
    vec = lambda a: a[l][None, :]

    scale_q = lambda z: z * HEAD_DIM ** -0.5

    def project(x, tm, feature_major_q):
        h = _rmsnorm(x, gmix, tm)
        u = _glu(h, wa, wb, ba, bb, tm, 512)
        if feature_major_q:
            q = _proj_t(h, wq, bq.reshape(-1, 1), scale_q, BF16, tm, 512, "proj_q_t")
        else:
            q = _proj(h, wq, bq, scale_q, BF16, tm, 512, "proj_q")
        gates = _proj(h, wzg, bzg, _sigmoid, F32, tm, LANES, "proj_head_gates")
        return h, u, q, gates

    xp = x_prompt.reshape(n_tok, D_MODEL)
    xs = x_sample.reshape(bs, D_MODEL)
    tm_p = 512
    hp, up, qp, gates_p = project(xp, tm_p, True)
    hs, us, qs, gates_s = project(xs, bs, False)
    kv_p = _kv_transposed(hp, wkv, bkv.reshape(-1, 1), batch, seq, tm_p)
    kv_s = _proj(hs, wkv, bkv, lambda z: z, F32, bs, 512, "proj_kv_sample")

    conv_args = (conv_w[l], vec(conv_b), vec(conv_ln_g), vec(conv_ln_b))
    cp = _conv_prompt(up, *conv_args, batch, seq, 256)
    cs, new_conv_s = _conv_sample(state_conv[l].transpose(1, 0, 2), us, *conv_args)

    nqt = seq // QT
    n_chunk_p = seq // CMP_STRIDE
    n_chunk_s = n_pages * _CHUNKS_PER_PAGE
    cmap, wmap, smap = _bias_prompt(rb_t, nqt, n_chunk_p)
    cbias_s, wbias_s, sbias_s = _bias_sample(rb_t, past, n_chunk_s, n_pages)

    ident = jnp.zeros((1,), I32)
    kcmp_p = _compress(kv_p[0], ident, cw, batch, seq // PAGE, seq // PAGE, False)
    feat_major = lambda a: a.transpose(0, 2, 3, 4, 1)
    pool_cmp = feat_major(cache_cmp_kv[l]).reshape(n_pool, D_KV, PAGE)
    pp_s = min(32, n_pages)
    kcmp_s = _compress(pool_cmp, page_table.reshape(-1), cw, bs, n_pages, pp_s, True)

    n_sel_p = seq // SEL_LEN
    o_cmp_p, selbias = _cmp_prompt(qp, kcmp_p, cmap, gates_p, _overlap(n_chunk_p, n_sel_p, n_sel_p).T,
                                   batch, seq)
    n_sel_s = -(-(past + 1) // SEL_LEN)
    n_sel_pad = -(-n_sel_s // LANES) * LANES
    top_s = min(SEL_TOP, n_sel_s)
    head_gate = lambda k: jnp.broadcast_to(
        gates_s[:, k * N_HEADS:(k + 1) * N_HEADS, None], (bs, N_HEADS, HEAD_DIM))
    qs3 = qs.reshape(bs, N_HEADS, HEAD_DIM)
    o_cmp_s, idx_s = _cmp_sample(qs3, kcmp_s, cbias_s, _overlap(n_chunk_s, n_sel_s, n_sel_pad), head_gate(0),
                                 n_sel_s, past)

    o_sel_p = _sel_prompt(qp, kv_p[1], selbias, smap, gates_p, batch, seq)
    by_group = lambda a: jnp.pad(a.reshape(bs, N_KV, HPG, -1), ((0, 0), (0, 0), (0, 8 - HPG), (0, 0)))
    q4 = by_group(qs3)
    blk = idx_s[:, :top_s].reshape(-1)
    phys = jnp.take_along_axis(
        page_table, jnp.minimum(blk // 2, n_pages - 1).reshape(bs, -1), axis=1).reshape(-1)
    kv_new = lambda k: jnp.pad(
        kv_s[:, k * D_KV:(k + 1) * D_KV].reshape(bs, 2, N_KV, HEAD_DIM).transpose(0, 2, 1, 3),
        ((0, 0), (0, 0), (0, 6), (0, 0)))
    bias_by_group = lambda a: jnp.pad(
        jnp.moveaxis(a, -2, 0).reshape((N_KV, HPG) + a.shape[:-2] + a.shape[-1:]),
        ((0, 0), (0, 8 - HPG)) + ((0, 0),) * (a.ndim - 1))
    sb4 = jnp.moveaxis(bias_by_group(sbias_s), 1, 2)
    pool_sel = feat_major(cache_sel_kv[l]).reshape(n_pool, 2 * N_KV, HEAD_DIM, PAGE)
    o_sel_s = _sel_sample(blk, phys, q4, kv_new(1), sb4, by_group(head_gate(1)), pool_sel, top_s, n_pages)

    o_win_p = _win_prompt(qp, kv_p[2], wmap, gates_p, batch, seq)
    win_t = feat_major(state_win_kv[l]).reshape(bs, D_KV, WINDOW)
    newcol = jnp.broadcast_to(kv_s[:, 2 * D_KV:3 * D_KV, None], (bs, D_KV, LANES))
    o_win_s, new_win_s = _win_sample(q4, kv_new(2), win_t, newcol, bias_by_group(wbias_s), by_group(head_gate(2)))

    m_p = _merge_a(hp, cp, o_cmp_p, o_sel_p, o_win_p, wga, wgb, wc, wn, bga, bgb, vec(b_conv_out), 512, 512)
    h2_all = jnp.zeros((n_tok + _MOE_SLACK, D_MODEL), F32)
    x1_p, h2_all, eid_p, rw_p, cnt_p = _merge_b(xp, m_p, wo, gffn, wr, br, h2_all, 0, 256)
    from_groups = lambda a: a[:, :, :HPG, :].reshape(bs, D_Q)
    m_s = _merge_a(hs, cs, o_cmp_s.reshape(bs, D_Q), from_groups(o_sel_s), from_groups(o_win_s),
                   wga, wgb, wc, wn, bga, bgb, vec(b_conv_out), bs, 512)
    x1_s, h2_all, eid_s, rw_s, cnt_s = _merge_b(xs, m_s, wo, gffn, wr, br, h2_all, n_tok, bs)

    eid = jnp.concatenate([eid_p[:TOP_K].T, eid_s[:TOP_K].T], axis=0).reshape(-1)
    rw = jnp.concatenate([rw_p[:TOP_K].T, rw_s[:TOP_K].T], axis=0).reshape(-1)
    counts = jnp.sum(cnt_p[:, :, 0], axis=0) + jnp.sum(cnt_s[:, :, 0], axis=0)
    rows_src, rows_dst, rows_w, blk_e, n_used = _dispatch(eid, rw, counts, h2_all.shape[0])
    yk = _experts(blk_e, n_used, rows_src, rows_dst, rows_w, h2_all, n_tok + bs, w_gate[l].astype(BF16),
                  w_up[l].astype(BF16), w_down[l].astype(BF16))
    y_p = _combine(x1_p, yk, gfin, 0, 256)
    y_s = _combine(x1_s, yk, gfin, n_tok, bs)

    def cache_rows(a_t, rows):
        return a_t.reshape(a_t.shape[0], 2, N_KV, HEAD_DIM, rows).transpose(0, 4, 1, 2, 3)[None]

    new_row = lambda k: kv_s[:, k * D_KV:(k + 1) * D_KV].reshape(1, bs, 1, 2, N_KV, HEAD_DIM)
    hist = CONV_W - 1
    return (
        y_p.reshape(batch, seq, D_MODEL),
        y_s.reshape(bs, 1, D_MODEL),
        cache_rows(kv_p[0], seq),
        new_row(0),
        cache_rows(kv_p[1], seq),
        new_row(1),
        cache_rows(kv_p[2][:, :, seq - WINDOW:], WINDOW),
        cache_rows(new_win_s, WINDOW),
        up.reshape(batch, seq, D_CONV)[None, :, seq - hist:],
        new_conv_s.transpose(1, 0, 2)[None],
    )
```

```python
import functools
import math

import jax
import jax.numpy as jnp
from jax import lax
from jax.experimental import pallas as pl
from jax.experimental.pallas import tpu as pltpu

F32 = jnp.float32
BF16 = jnp.bfloat16
I32 = jnp.int32

D_MODEL = 2048
D_CONV = 1024
CONV_W = 31
N_HEADS = 16
HEAD_DIM = 64
N_KV = 4
HPG = N_HEADS // N_KV
D_Q = N_HEADS * HEAD_DIM
D_KV = 2 * N_KV * HEAD_DIM
CMP_LEN = 32
CMP_STRIDE = 16
CMP_HID = 64
SEL_LEN = 64
SEL_TOP = 16
WINDOW = 512
N_BUCKETS = 32
MAX_DIST = 128
N_GROUPS = 4
EXP_PER_GROUP = 8
N_EXPERTS = N_GROUPS * EXP_PER_GROUP
TOP_K = 2
D_EXPERT = 1024
EPS = 1e-6
NEG = -1e30
PAGE = 128
LANES = 128
QT = 128
GROUP_ROWS = HPG * QT
MIB = 1024 * 1024
LOG2E = math.log2(math.e)


def _params(sem, vmem_mib=48):
    return pltpu.CompilerParams(dimension_semantics=sem, vmem_limit_bytes=vmem_mib * MIB)


def _mm_nt(a, b):
    return lax.dot_general(a, b, (((1,), (1,)), ((), ())), preferred_element_type=F32)


def _mm(a, b):
    return jnp.dot(a, b, preferred_element_type=F32)


def _sigmoid(z):
    return 1.0 / (1.0 + jnp.exp(-z))


def _silu(z):
    return z * _sigmoid(z)


def _gelu_tanh(z):
    return 0.5 * z * (1.0 + jnp.tanh(math.sqrt(2.0 / math.pi) * (z + 0.044715 * (z * z * z))))


def _rmsnorm_kernel(x_ref, g_ref, o_ref):
    x = x_ref[...]
    ms = jnp.mean(x * x, axis=-1, keepdims=True)
    o_ref[...] = (x * lax.rsqrt(ms + EPS) * g_ref[...]).astype(o_ref.dtype)


def _rmsnorm(x, g, tm):
    n, d = x.shape
    return pl.pallas_call(
        _rmsnorm_kernel,
        grid=(n // tm,),
        in_specs=[pl.BlockSpec((tm, d), lambda i: (i, 0)), pl.BlockSpec((1, d), lambda i: (0, 0))],
        out_specs=pl.BlockSpec((tm, d), lambda i: (i, 0)),
        out_shape=jax.ShapeDtypeStruct((n, d), BF16),
        compiler_params=_params(("parallel",)),
        name="rmsnorm",
    )(x, g)


def _proj_kernel(h_ref, w_ref, b_ref, o_ref, *, act):
    z = _mm_nt(h_ref[...], w_ref[...]) + b_ref[...]
    o_ref[...] = act(z).astype(o_ref.dtype)


def _proj(h, wt, b, act, out_dtype, tm, tn, name):
    n, k = h.shape
    nout = wt.shape[0]
    return pl.pallas_call(
        functools.partial(_proj_kernel, act=act),
        grid=(n // tm, nout // tn),
        in_specs=[
            pl.BlockSpec((tm, k), lambda i, j: (i, 0)),
            pl.BlockSpec((tn, k), lambda i, j: (j, 0)),
            pl.BlockSpec((1, tn), lambda i, j: (0, j)),
        ],
        out_specs=pl.BlockSpec((tm, tn), lambda i, j: (i, j)),
        out_shape=jax.ShapeDtypeStruct((n, nout), out_dtype),
        compiler_params=_params(("parallel", "parallel")),
        name=name,
    )(h, wt, b)


def _proj_t_kernel(h_ref, w_ref, b_ref, o_ref, *, act):
    z = _mm_nt(w_ref[...], h_ref[...]) + b_ref[...]
    o_ref[...] = act(z).astype(o_ref.dtype)


def _proj_t(h, wt, bcol, act, out_dtype, tm, tn, name):
    n, k = h.shape
    nout = wt.shape[0]
    return pl.pallas_call(
        functools.partial(_proj_t_kernel, act=act),
        grid=(n // tm, nout // tn),
        in_specs=[
            pl.BlockSpec((tm, k), lambda i, j: (i, 0)),
            pl.BlockSpec((tn, k), lambda i, j: (j, 0)),
            pl.BlockSpec((tn, 1), lambda i, j: (j, 0)),
        ],
        out_specs=pl.BlockSpec((tn, tm), lambda i, j: (j, i)),
        out_shape=jax.ShapeDtypeStruct((nout, n), out_dtype),
        compiler_params=_params(("parallel", "parallel")),
        name=name,
    )(h, wt, bcol)


def _glu_kernel(h_ref, wa_ref, wb_ref, ba_ref, bb_ref, o_ref):
    h = h_ref[...]
    a = _mm_nt(h, wa_ref[...]) + ba_ref[...]
    b = _mm_nt(h, wb_ref[...]) + bb_ref[...]
    o_ref[...] = a * _sigmoid(b)


def _glu(h, wat, wbt, ba, bb, tm, tn):
    n, k = h.shape
    nout = wat.shape[0]
    wspec = pl.BlockSpec((tn, k), lambda i, j: (j, 0))
    bspec = pl.BlockSpec((1, tn), lambda i, j: (0, j))
    return pl.pallas_call(
        _glu_kernel,
        grid=(n // tm, nout // tn),
        in_specs=[pl.BlockSpec((tm, k), lambda i, j: (i, 0)), wspec, wspec, bspec, bspec],
        out_specs=pl.BlockSpec((tm, tn), lambda i, j: (i, j)),
        out_shape=jax.ShapeDtypeStruct((n, nout), F32),
        compiler_params=_params(("parallel", "parallel")),
        name="glu",
    )(h, wat, wbt, ba, bb)


def _kvt_kernel(h_ref, w_ref, b_ref, *o_refs):
    h = h_ref[...]
    for kind, o_ref in enumerate(o_refs):
        rows = pl.ds(kind * D_KV, D_KV)
        o_ref[0] = _mm_nt(w_ref[rows, :], h) + b_ref[rows, :]


def _kv_transposed(h, wt, bcol, batch, seq, tm):
    n, k = h.shape
    kinds = wt.shape[0] // D_KV
    nt = seq // tm
    return pl.pallas_call(
        _kvt_kernel,
        grid=(n // tm,),
        in_specs=[
            pl.BlockSpec((tm, k), lambda i: (i, 0)),
            pl.BlockSpec(wt.shape, lambda i: (0, 0)),
            pl.BlockSpec(bcol.shape, lambda i: (0, 0)),
        ],
        out_specs=[pl.BlockSpec((1, D_KV, tm), lambda i: (i // nt, 0, i % nt))] * kinds,
        out_shape=[jax.ShapeDtypeStruct((batch, D_KV, seq), F32)] * kinds,
        compiler_params=_params(("parallel",)),
        name="kv_transposed",
    )(h, wt, bcol)


def _project_prompt_kernel(x_ref, g_ref, wa_ref, wb_ref, wq_ref, wkv_ref, wzg_ref, ba_ref, bb_ref, bq_ref,
                           bkv_ref, bzg_ref, h_ref, u_ref, qt_ref, gates_ref, *kv_refs, q_scale):
    x = x_ref[...]
    ms = jnp.mean(x * x, axis=-1, keepdims=True)
    h = (x * lax.rsqrt(ms + EPS) * g_ref[...]).astype(BF16)
    h_ref[...] = h
    a = _mm_nt(h, wa_ref[...]) + ba_ref[...]
    b = _mm_nt(h, wb_ref[...]) + bb_ref[...]
    u_ref[...] = a * _sigmoid(b)
    qt_ref[...] = ((_mm_nt(wq_ref[...], h) + bq_ref[...]) * q_scale).astype(qt_ref.dtype)
    gates_ref[...] = _sigmoid(_mm_nt(h, wzg_ref[...]) + bzg_ref[...])
    for kind, o_ref in enumerate(kv_refs):
        rows = pl.ds(kind * D_KV, D_KV)
        o_ref[0] = _mm_nt(wkv_ref[rows, :], h) + bkv_ref[rows, :]


def _project_prompt(x, g, wa, wb, wq, wkv, wzg, ba, bb, bq_col, bkv_col, bzg, q_scale, batch, seq, tm):
    n, d = x.shape
    kinds = wkv.shape[0] // D_KV
    nt = seq // tm
    tok = lambda width: pl.BlockSpec((tm, width), lambda i: (i, 0))
    const = lambda a: pl.BlockSpec(a.shape, lambda i: (0,) * a.ndim, pipeline_mode=pl.Buffered(1))
    consts = (g, wa, wb, wq, wkv, wzg, ba, bb, bq_col, bkv_col, bzg)
    return pl.pallas_call(
        functools.partial(_project_prompt_kernel, q_scale=q_scale),
        grid=(n // tm,),
        in_specs=[tok(d)] + [const(a) for a in consts],
        out_specs=[tok(d), tok(D_CONV), pl.BlockSpec((D_Q, tm), lambda i: (0, i)), tok(LANES)]
        + [pl.BlockSpec((1, D_KV, tm), lambda i: (i // nt, 0, i % nt))] * kinds,
        out_shape=[
            jax.ShapeDtypeStruct((n, d), BF16),
            jax.ShapeDtypeStruct((n, D_CONV), F32),
            jax.ShapeDtypeStruct((D_Q, n), BF16),
            jax.ShapeDtypeStruct((n, LANES), F32),
        ] + [jax.ShapeDtypeStruct((batch, D_KV, seq), F32)] * kinds,
        compiler_params=_params(("parallel",), 56),
        name="project_prompt",
    )(x, *consts)


def _merge_a_kernel(h_ref, c_ref, o1_ref, o2_ref, o3_ref, wga_ref, wgb_ref, wc_ref, wn_ref,
                    bga_ref, bgb_ref, bc_ref, m_ref):
    h = h_ref[...]
    ga = _sigmoid(_mm_nt(h, wga_ref[...]) + bga_ref[...])
    gb = _sigmoid(_mm_nt(h, wgb_ref[...]) + bgb_ref[...])
    yc = _mm(c_ref[...], wc_ref[...]) + bc_ref[...]
    o = (o1_ref[...] + o2_ref[...] + o3_ref[...]).astype(BF16)
    yn = _mm(o, wn_ref[...])
    m_ref[...] = (ga * yc + gb * yn).astype(m_ref.dtype)


def _merge_a(h, c, o1, o2, o3, wgat, wgbt, wc, wn, bga, bgb, bc, tm, tn):
    n, k = h.shape
    tok = lambda width: pl.BlockSpec((tm, width), lambda i, j: (i, 0))
    wt_spec = pl.BlockSpec((tn, k), lambda i, j: (j, 0))
    w_spec = pl.BlockSpec((c.shape[1], tn), lambda i, j: (0, j))
    b_spec = pl.BlockSpec((1, tn), lambda i, j: (0, j))
    return pl.pallas_call(
        _merge_a_kernel,
        grid=(n // tm, D_MODEL // tn),
        in_specs=[tok(k), tok(D_CONV), tok(D_Q), tok(D_Q), tok(D_Q), wt_spec, wt_spec, w_spec, w_spec,
                  b_spec, b_spec, b_spec],
        out_specs=pl.BlockSpec((tm, tn), lambda i, j: (i, j)),
        out_shape=jax.ShapeDtypeStruct((n, D_MODEL), BF16),
        compiler_params=_params(("parallel", "parallel")),
        name="merge_gates",
    )(h, c, o1, o2, o3, wgat, wgbt, wc, wn, bga, bgb, bc)


def _merge_b_kernel(x_ref, m_ref, wo_ref, gn_ref, wr_ref, br_ref, h2_all_ref, x1_ref, h2_ref, eid_ref, rw_ref,
                    cnt_ref):
    del h2_all_ref
    x1 = x_ref[...] + _mm(m_ref[...], wo_ref[...])
    x1_ref[...] = x1
    ms = jnp.mean(x1 * x1, axis=-1, keepdims=True)
    h2 = x1 * lax.rsqrt(ms + EPS) * gn_ref[...]
    h2_ref[...] = h2
    logits = _mm_nt(wr_ref[...], h2.astype(BF16)) + br_ref[...]
    tm = logits.shape[1]
    row = lax.broadcasted_iota(I32, (8, tm), 0)
    lg = jnp.where(row < N_GROUPS, logits[0:8], -jnp.inf)
    gmax = jnp.max(lg, axis=0, keepdims=True)
    gidx = jnp.min(jnp.where(lg == gmax, row, 8), axis=0, keepdims=True)
    wg = 1.0 / jnp.sum(jnp.exp(lg - gmax), axis=0, keepdims=True)
    le = jnp.zeros((8, tm), F32)
    for g in range(N_GROUPS):
        le = jnp.where(gidx == g, logits[8 + 8 * g:16 + 8 * g], le)
    ee = jnp.exp(le - jnp.max(le, axis=0, keepdims=True))
    pz = ee / jnp.sum(ee, axis=0, keepdims=True)
    p1 = jnp.max(pz, axis=0, keepdims=True)
    i1 = jnp.min(jnp.where(pz == p1, row, 8), axis=0, keepdims=True)
    pz2 = jnp.where(row == i1, -1.0, pz)
    p2 = jnp.max(pz2, axis=0, keepdims=True)
    i2 = jnp.min(jnp.where(pz2 == p2, row, 8), axis=0, keepdims=True)
    den = p1 + p2
    e1 = gidx * EXP_PER_GROUP + i1
    e2 = gidx * EXP_PER_GROUP + i2
    eid_ref[...] = jnp.where(row == 0, e1, jnp.where(row == 1, e2, 0))
    rw_ref[...] = jnp.where(row == 0, wg * p1 / den, jnp.where(row == 1, wg * p2 / den, 0.0))
    erow = lax.broadcasted_iota(I32, (N_EXPERTS, tm), 0)
    hits = (erow == e1).astype(F32) + (erow == e2).astype(F32)
    cnt_ref[...] = jnp.broadcast_to(jnp.sum(hits, axis=1, keepdims=True), (N_EXPERTS, LANES)).astype(I32)


def _merge_b(x, m, wo, gn, wr, br, h2_all, row0, tm):
    n, d = x.shape
    blk0 = row0 // tm
    return pl.pallas_call(
        _merge_b_kernel,
        grid=(n // tm,),
        in_specs=[
            pl.BlockSpec((tm, d), lambda i: (i, 0)),
            pl.BlockSpec((tm, d), lambda i: (i, 0)),
            pl.BlockSpec((d, d), lambda i: (0, 0)),
            pl.BlockSpec((1, d), lambda i: (0, 0)),
            pl.BlockSpec((LANES, d), lambda i: (0, 0)),
            pl.BlockSpec((LANES, 1), lambda i: (0, 0)),
            pl.BlockSpec(memory_space=pl.ANY),
        ],
        out_specs=[
            pl.BlockSpec((tm, d), lambda i: (i, 0)),
            pl.BlockSpec((tm, d), lambda i: (blk0 + i, 0)),
            pl.BlockSpec((8, tm), lambda i: (0, i)),
            pl.BlockSpec((8, tm), lambda i: (0, i)),
            pl.BlockSpec((None, N_EXPERTS, LANES), lambda i: (i, 0, 0)),
        ],
        out_shape=[
            jax.ShapeDtypeStruct((n, d), F32),
            jax.ShapeDtypeStruct(h2_all.shape, F32),
            jax.ShapeDtypeStruct((8, n), I32),
            jax.ShapeDtypeStruct((8, n), F32),
            jax.ShapeDtypeStruct((n // tm, N_EXPERTS, LANES), I32),
        ],
        input_output_aliases={6: 1},
        compiler_params=_params(("parallel",)),
        name="merge_out_router",
    )(x, m, wo, gn, wr, br, h2_all)


_HALO = 32


def _conv_prompt_kernel(u_ref, prev_ref, w_ref, b_ref, g_ref, beta_ref, o_ref, buf_ref, acc_ref, sh_ref, *, tt):
    i = pl.program_id(1)
    prev = prev_ref[pl.ds(tt - _HALO, _HALO), :]
    buf_ref[pl.ds(0, _HALO), :] = jnp.where(i > 0, prev, 0.0)
    buf_ref[pl.ds(_HALO, tt), :] = u_ref[...]
    off = _HALO - (CONV_W - 1)
    sub = 8
    rows = tt
    for cc in range(D_CONV // LANES):
        cols = pl.ds(cc * LANES, LANES)
        for r0 in range(0, tt, rows):
            acc = jnp.zeros((rows, LANES), F32)
            for r in range(sub):
                taps = [k for k in range(CONV_W) if (off + k) % sub == r]
                n_shift = (rows + _HALO - r) // sub * sub
                sh_ref[r, pl.ds(0, n_shift), :] = buf_ref[pl.ds(r0 + r, n_shift), cols]
                for k in taps:
                    acc = acc + sh_ref[r, pl.ds(off + k - r, rows), :] * w_ref[pl.ds(k, 1), cols]
            acc_ref[pl.ds(r0, rows), cols] = acc + b_ref[:, cols]
    c = acc_ref[...]
    mu = jnp.mean(c, axis=-1, keepdims=True)
    var = jnp.mean(jnp.square(c - mu), axis=-1, keepdims=True)
    y = (c - mu) * lax.rsqrt(var + EPS) * g_ref[...] + beta_ref[...]
    o_ref[...] = _silu(y).astype(o_ref.dtype)


def _conv_prompt(u, w, b, g, beta, batch, seq, tt):
    nt = seq // tt
    vec = pl.BlockSpec((1, D_CONV), lambda bi, i: (0, 0))
    return pl.pallas_call(
        functools.partial(_conv_prompt_kernel, tt=tt),
        grid=(batch, nt),
        in_specs=[
            pl.BlockSpec((tt, D_CONV), lambda bi, i: (bi * nt + i, 0)),
            pl.BlockSpec((tt, D_CONV), lambda bi, i: (bi * nt + jnp.maximum(i - 1, 0), 0)),
            pl.BlockSpec((CONV_W, D_CONV), lambda bi, i: (0, 0)),
            vec, vec, vec,
        ],
        out_specs=pl.BlockSpec((tt, D_CONV), lambda bi, i: (bi * nt + i, 0)),
        out_shape=jax.ShapeDtypeStruct((batch * seq, D_CONV), BF16),
        scratch_shapes=[pltpu.VMEM((tt + _HALO, D_CONV), F32), pltpu.VMEM((tt, D_CONV), F32),
                        pltpu.VMEM((8, tt + _HALO, LANES), F32)],
        compiler_params=_params(("parallel", "parallel")),
        name="conv_prompt",
    )(u, u, w, b, g, beta)


def _conv_sample_kernel(st_ref, u_ref, w_ref, b_ref, g_ref, beta_ref, c_ref, new_ref):
    hist = CONV_W - 1
    u = u_ref[...]
    acc = u * w_ref[pl.ds(hist, 1), :] + b_ref[...]
    for t in range(hist):
        acc = acc + st_ref[t] * w_ref[pl.ds(t, 1), :]
    mu = jnp.mean(acc, axis=-1, keepdims=True)
    var = jnp.mean(jnp.square(acc - mu), axis=-1, keepdims=True)
    y = (acc - mu) * lax.rsqrt(var + EPS) * g_ref[...] + beta_ref[...]
    c_ref[...] = _silu(y).astype(c_ref.dtype)
    for t in range(hist - 1):
        new_ref[t] = st_ref[t + 1]
    new_ref[hist - 1] = u


def _conv_sample(state_t, u, w, b, g, beta):
    hist, bs, _ = state_t.shape
    return pl.pallas_call(
        _conv_sample_kernel,
        out_shape=[jax.ShapeDtypeStruct((bs, D_CONV), BF16), jax.ShapeDtypeStruct((hist, bs, D_CONV), F32)],
        compiler_params=pltpu.CompilerParams(vmem_limit_bytes=48 * MIB),
        name="conv_sample",
    )(state_t, u, w, b, g, beta)


def _bias_lookup(rb_ref, head, dist, valid):
    n = jnp.maximum(dist, 0)
    max_exact = N_BUCKETS // 2
    nf = jnp.maximum(n, 1).astype(F32)
    large = max_exact + (jnp.log(nf / max_exact) / math.log(MAX_DIST / max_exact)
                         * (N_BUCKETS - max_exact)).astype(I32)
    bucket = jnp.where(n < max_exact, n, jnp.minimum(large, N_BUCKETS - 1))
    val = jnp.zeros(dist.shape, F32)
    for k in range(N_BUCKETS):
        val = jnp.where(bucket == k, rb_ref[head, k], val)
    return jnp.where(valid, val, NEG)


def _bias_prompt_kernel(rb_ref, cmp_ref, win_ref, sel_ref, *, nqt, n_chunk):
    g = pl.program_id(0)
    key = lax.broadcasted_iota(I32, (LANES, QT), 0)
    qry = lax.broadcasted_iota(I32, (LANES, QT), 1)
    for hh in range(HPG):
        head = g * HPG + hh
        cols = pl.ds(hh * QT, QT)
        lookup = lambda dist, valid: _bias_lookup(rb_ref, head, dist, valid) * LOG2E

        def cmp_tile(qt, carry):
            for cb in range(n_chunk // LANES):
                dist = qt * QT + qry - (key + cb * LANES) * CMP_STRIDE - (CMP_LEN - 1)
                cmp_ref[0, qt, pl.ds(cb * LANES, LANES), cols] = lookup(dist, dist >= 0)
            return carry

        lax.fori_loop(0, nqt, cmp_tile, 0)
        for jb in range((WINDOW + QT) // LANES):
            dist = qry - (key + jb * LANES) + WINDOW
            win_ref[0, pl.ds(jb * LANES, LANES), cols] = lookup(dist, (dist >= 0) & (dist <= WINDOW))
        for kind in range(3):
            for jb in range(2):
                dist = kind * QT + qry - (key + jb * LANES)
                sel_ref[0, kind, pl.ds(jb * LANES, LANES), cols] = lookup(dist, dist >= 0)
        far = jnp.full((LANES, QT), 2 * MAX_DIST, I32)
        for jb in range(2):
            sel_ref[0, 3, pl.ds(jb * LANES, LANES), cols] = lookup(far, far >= 0)


def _bias_prompt(rb_t, nqt, n_chunk):
    return pl.pallas_call(
        functools.partial(_bias_prompt_kernel, nqt=nqt, n_chunk=n_chunk),
        grid=(N_KV,),
        in_specs=[pl.BlockSpec(memory_space=pltpu.SMEM)],
        out_specs=[
            pl.BlockSpec((1, nqt, n_chunk, GROUP_ROWS), lambda g: (g, 0, 0, 0)),
            pl.BlockSpec((1, WINDOW + QT, GROUP_ROWS), lambda g: (g, 0, 0)),
            pl.BlockSpec((1, 4, 2 * LANES, GROUP_ROWS), lambda g: (g, 0, 0, 0)),
        ],
        out_shape=[
            jax.ShapeDtypeStruct((N_KV, nqt, n_chunk, GROUP_ROWS), F32),
            jax.ShapeDtypeStruct((N_KV, WINDOW + QT, GROUP_ROWS), F32),
            jax.ShapeDtypeStruct((N_KV, 4, 2 * LANES, GROUP_ROWS), F32),
        ],
        compiler_params=_params(("parallel",)),
        name="bias_maps_prompt",
    )(rb_t)


def _bias_sample_kernel(rb_ref, cmp_ref, win_ref, sel_ref, *, past, n_chunk, n_pages):
    col = lax.broadcasted_iota(I32, (N_HEADS, LANES), 1)
    rb = rb_ref[...]

    def lookup(dist):
        n = jnp.maximum(dist, 0)
        max_exact = N_BUCKETS // 2
        nf = jnp.maximum(n, 1).astype(F32)
        large = max_exact + (jnp.log(nf / max_exact) / math.log(MAX_DIST / max_exact)
                             * (N_BUCKETS - max_exact)).astype(I32)
        bucket = jnp.where(n < max_exact, n, jnp.minimum(large, N_BUCKETS - 1))
        val = jnp.zeros(dist.shape, F32)
        for k in range(N_BUCKETS):
            val = jnp.where(bucket == k, rb[:, k:k + 1], val)
        return jnp.where(dist >= 0, val, NEG)

    def cmp_block(cb, carry):
        dist = past - (col + cb * LANES) * CMP_STRIDE - (CMP_LEN - 1)
        cmp_ref[cb] = lookup(dist)
        return carry

    lax.fori_loop(0, n_chunk // LANES, cmp_block, 0)

    def win_block(jb, carry):
        win_ref[jb] = lookup(WINDOW - (col + jb * LANES))
        return carry

    lax.fori_loop(0, WINDOW // LANES + 1, win_block, 0)

    def sel_page(p, carry):
        sel_ref[p] = lookup(past - (p * PAGE + col))
        return carry

    lax.fori_loop(0, n_pages + 1, sel_page, 0)


def _bias_sample(rb_t, past, n_chunk, n_pages):
    cmp_b, win_b, sel_b = pl.pallas_call(
        functools.partial(_bias_sample_kernel, past=past, n_chunk=n_chunk, n_pages=n_pages),
        out_shape=[
            jax.ShapeDtypeStruct((n_chunk // LANES, N_HEADS, LANES), F32),
            jax.ShapeDtypeStruct((WINDOW // LANES + 1, N_HEADS, LANES), F32),
            jax.ShapeDtypeStruct((n_pages + 1, N_HEADS, LANES), F32),
        ],
        name="bias_maps_sample",
    )(rb_t)
    flat = lambda a: a.transpose(1, 0, 2).reshape(N_HEADS, -1)
    return flat(cmp_b), flat(win_b), sel_b


_CHUNKS_PER_PAGE = PAGE // CMP_STRIDE
_CHUNK_PITCH = 24


def _compress_kernel(pt_ref, *refs, pp):
    page_refs = refs[:pp + 1]
    w0k_ref, w1k_ref, w0v_ref, w1v_ref, pe_ref, w2k_ref, w2v_ref, o_ref, tr_ref = refs[pp + 1:]
    del pt_ref
    nch = pp * _CHUNKS_PER_PAGE
    m = nch + 8
    n_slab = D_KV // LANES

    def transpose(s):
        for k in range(pp + 1):
            t = page_refs[k][pl.ds(s * LANES, LANES), :].T
            for n in range(_CHUNKS_PER_PAGE):
                row = (k * _CHUNKS_PER_PAGE + n) * _CHUNK_PITCH
                tr_ref[s, pl.ds(row, CMP_STRIDE), :] = t[n * CMP_STRIDE:(n + 1) * CMP_STRIDE]

    def chunks(s, _):
        return jnp.concatenate(
            [tr_ref.at[s][pl.ds(p, m, stride=_CHUNK_PITCH), :] for p in range(CMP_STRIDE)], axis=1)

    def mlp(s, x):
        is_v = s >= n_slab // 2
        w0 = (w0v_ref if is_v else w0k_ref)[...]
        w1 = (w1v_ref if is_v else w1k_ref)[...]
        w2 = (w2v_ref if is_v else w2k_ref)[...]
        pe0 = pe_ref[pl.ds(2 * int(is_v), 1), :]
        pe1 = pe_ref[pl.ds(2 * int(is_v) + 1, 1), :]
        y0 = _mm((x + pe0).astype(BF16), w0)
        y1 = _mm((x + pe1).astype(BF16), w1)
        pre = y0[0:nch] + y1[1:nch + 1]
        o_ref[0, :, pl.ds(s * LANES, LANES)] = _mm(_gelu_tanh(pre).astype(BF16), w2)

    _staggered(n_slab, transpose, chunks, mlp)


def _compress(pages, page_ids, weights, nb, n_pages, pp, paged):
    w0k, w1k, w0v, w1v, pe, w2k, w2v = weights
    steps = n_pages // pp
    nch = pp * _CHUNKS_PER_PAGE

    def page_spec(k):
        if paged:
            return pl.BlockSpec(
                (None, D_KV, PAGE),
                lambda b, s, pt: (pt[b * n_pages + jnp.minimum(s * pp + k, n_pages - 1)], 0, 0))
        return pl.BlockSpec((None, D_KV, PAGE), lambda b, s, pt: (b, 0, jnp.minimum(s * pp + k, n_pages - 1)))

    full = lambda a: pl.BlockSpec(a.shape, lambda b, s, pt: (0,) * a.ndim)
    grid_spec = pltpu.PrefetchScalarGridSpec(
        num_scalar_prefetch=1,
        grid=(nb, steps),
        in_specs=[page_spec(k) for k in range(pp + 1)] + [full(a) for a in weights],
        out_specs=pl.BlockSpec((1, nch, D_KV), lambda b, s, pt: (b, s, 0)),
        scratch_shapes=[pltpu.VMEM((D_KV // LANES, (pp + 1) * _CHUNKS_PER_PAGE * _CHUNK_PITCH, LANES), F32)],
    )
    return pl.pallas_call(
        functools.partial(_compress_kernel, pp=pp),
        grid_spec=grid_spec,
        out_shape=jax.ShapeDtypeStruct((nb, n_pages * _CHUNKS_PER_PAGE, D_KV), F32),
        compiler_params=_params(("parallel", "parallel"), 56),
        name="compress_paged" if paged else "compress_prompt",
    )(page_ids, *([pages] * (pp + 1)), w0k, w1k, w0v, w1v, pe, w2k, w2v)


def _compress_weights(pe_k, w1_k, w2_k, pe_v, w1_v, w2_v):
    eye = jnp.eye(2, dtype=F32)

    def first(w1, j):
        w = w1[j * CMP_STRIDE:(j + 1) * CMP_STRIDE]
        return jnp.einsum("pdf,gh->pgdhf", w, eye).reshape(CMP_STRIDE * LANES, 2 * CMP_HID).astype(BF16)

    def pos(pe, j):
        return jnp.tile(pe[j * CMP_STRIDE:(j + 1) * CMP_STRIDE, None, :], (1, 2, 1)).reshape(1, CMP_STRIDE * LANES)

    def second(w2):
        return jnp.einsum("fd,gh->gfhd", w2, eye).reshape(2 * CMP_HID, 2 * HEAD_DIM).astype(BF16)

    pe = jnp.concatenate([pos(pe_k, 0), pos(pe_k, 1), pos(pe_v, 0), pos(pe_v, 1)], axis=0)
    return (first(w1_k, 0), first(w1_k, 1), first(w1_v, 0), first(w1_v, 1), pe, second(w2_k), second(w2_v))


def _staggered(n, scores, softmax, values):
    s = {0: scores(0)}
    out = []
    for g in range(n):
        if g + 1 < n:
            s[g + 1] = scores(g + 1)
        out.append(values(g, softmax(g, s.pop(g))))
    return out


def _group_queries(qt_ref, g):
    return jnp.concatenate(
        [qt_ref[pl.ds((g * HPG + hh) * HEAD_DIM, HEAD_DIM), :] for hh in range(HPG)], axis=1)


def _store_heads(o_ref, o_t, gate, g, branch):
    lane = lax.broadcasted_iota(I32, (QT, 2 * HEAD_DIM), 1)
    for pair in range(HPG // 2):
        h = g * HPG + 2 * pair
        col = branch * N_HEADS + h
        both = jnp.concatenate([o_t[:, (2 * pair) * QT:(2 * pair + 1) * QT],
                                o_t[:, (2 * pair + 1) * QT:(2 * pair + 2) * QT]], axis=0)
        gates = jnp.where(lane < HEAD_DIM, gate[:, col:col + 1], gate[:, col + 1:col + 2])
        o_ref[:, pl.ds(h * HEAD_DIM, 2 * HEAD_DIM)] = both.T * gates


def _cmp_prompt_kernel(qt_ref, kc_ref, map_ref, gate_ref, ov_ref, o_ref, sb_ref, *, n_sel, top):
    qt = pl.program_id(1)
    kv = kc_ref[0]
    gate = gate_ref[...]
    ov = ov_ref[...]
    half = D_KV // 2
    vt_all = kv[:, half:].T
    jrow = lax.broadcasted_iota(I32, (n_sel, QT), 0)
    tpos = qt * QT + lax.broadcasted_iota(I32, (n_sel, QT), 1)
    cur = tpos // SEL_LEN
    forced = (jrow == 0) | (jrow == cur) | (jrow == cur - 1)
    valid = jrow * SEL_LEN <= tpos
    def scores(g):
        kg = kv[:, g * HEAD_DIM:(g + 1) * HEAD_DIM].astype(BF16)
        return _mm(kg, _group_queries(qt_ref, g)) + map_ref[g, 0]

    def softmax(g, st):
        e = jnp.exp2(st - jnp.max(st, axis=0, keepdims=True))
        p = e / jnp.sum(e, axis=0, keepdims=True) * (map_ref[g, 0] > 0.5 * NEG).astype(F32)
        return p.astype(BF16)

    def values(g, pb):
        vgt = vt_all[g * HEAD_DIM:(g + 1) * HEAD_DIM].astype(BF16)
        _store_heads(o_ref, _mm(vgt, pb), gate, g, 0)
        return pb

    probs = _staggered(N_KV, scores, softmax, values)
    for g in range(N_KV):
        pb = probs[g]
        imp = jnp.zeros((n_sel, QT), F32)
        for hh in range(HPG):
            imp = imp + _mm(ov, pb[:, hh * QT:(hh + 1) * QT])
        score = jnp.where(forced, jnp.inf, jnp.where(valid, imp, -jnp.inf))
        rank = jnp.zeros((n_sel, QT), I32)
        for i in range(n_sel):
            si = score[i:i + 1, :]
            ahead = (si > score) | ((si == score) & (jrow > i))
            rank = rank + ahead.astype(I32)
        sb_ref[pl.ds(g * n_sel, n_sel), :] = jnp.where(rank < top, 0.0, NEG).astype(sb_ref.dtype)


def _cmp_prompt(q_t, kcmp, cmap, gates, ov_t, batch, seq):
    nqt = seq // QT
    n_chunk = kcmp.shape[1]
    n_sel = ov_t.shape[0]
    top = min(SEL_TOP, n_sel)
    return pl.pallas_call(
        functools.partial(_cmp_prompt_kernel, n_sel=n_sel, top=top),
        grid=(batch, nqt),
        in_specs=[
            pl.BlockSpec((D_Q, QT), lambda b, i: (0, b * nqt + i)),
            pl.BlockSpec((1, n_chunk, D_KV), lambda b, i: (b, 0, 0)),
            pl.BlockSpec((N_KV, 1, n_chunk, GROUP_ROWS), lambda b, i: (0, i, 0, 0)),
            pl.BlockSpec((QT, LANES), lambda b, i: (b * nqt + i, 0)),
            pl.BlockSpec(ov_t.shape, lambda b, i: (0, 0)),
        ],
        out_specs=[
            pl.BlockSpec((QT, D_Q), lambda b, i: (b * nqt + i, 0)),
            pl.BlockSpec((None, N_KV * n_sel, QT), lambda b, i: (b * nqt + i, 0, 0)),
        ],
        out_shape=[
            jax.ShapeDtypeStruct((batch * seq, D_Q), F32),
            jax.ShapeDtypeStruct((batch * nqt, N_KV * n_sel, QT), BF16),
        ],
        compiler_params=_params(("parallel", "parallel")),
        name="cmp_attn_prompt",
    )(q_t, kcmp, cmap, gates, ov_t)


_TK = 2 * LANES


def _sel_prompt_kernel(qt_ref, kv_ref, sb_ref, map_ref, gate_ref, o_ref, ka_ref, vt_ref, qa_ref, *, n_sel, seq):
    qt = pl.program_id(1)
    aug = 2 * HEAD_DIM
    half = D_KV // 2

    @pl.when(qt == 0)
    def _():
        pos = lax.broadcasted_iota(I32, (seq, aug - HEAD_DIM), 0)
        blk = lax.broadcasted_iota(I32, (seq, aug - HEAD_DIM), 1)
        onehot = jnp.where(pos // SEL_LEN == blk, 1.0, 0.0)
        k_rows = kv_ref[0, pl.ds(0, half), :].T
        for g in range(N_KV):
            ka_ref[g] = jnp.concatenate([k_rows[:, g * HEAD_DIM:(g + 1) * HEAD_DIM], onehot], axis=1).astype(BF16)
            vt_ref[g] = kv_ref[0, pl.ds(half + g * HEAD_DIM, HEAD_DIM), :].astype(BF16)

    gate = gate_ref[...]
    sb = sb_ref[...]
    diag = qt // 2
    n_tiles = diag + 1
    pad = jnp.zeros((aug - HEAD_DIM - n_sel, GROUP_ROWS), BF16)
    for g in range(N_KV):
        sbg = sb[g * n_sel:(g + 1) * n_sel]
        qa_ref[g] = jnp.concatenate([_group_queries(qt_ref, g), jnp.concatenate([sbg] * HPG, axis=1), pad], axis=0)

    def tile(kt, carry):
        start = pl.multiple_of(kt * _TK, _TK)
        kind = jnp.where(kt == diag, qt % 2, jnp.where((kt == diag - 1) & (qt % 2 == 0), 2, 3))
        def scores(g):
            return _mm(ka_ref[g, pl.ds(start, _TK), :], qa_ref[g]) + map_ref[g, kind]

        def softmax(g, st):
            m_i, l_i, _ = carry[g]
            m_new = jnp.maximum(m_i, jnp.max(st, axis=0, keepdims=True))
            alpha = jnp.exp2(m_i - m_new)
            p = jnp.exp2(st - m_new)
            return m_new, alpha * l_i + jnp.sum(p, axis=0, keepdims=True), alpha, p.astype(BF16)

        def values(g, sm):
            m_new, l_new, alpha, pb = sm
            return m_new, l_new, alpha * carry[g][2] + _mm(vt_ref[g, :, pl.ds(start, _TK)], pb)

        return tuple(_staggered(N_KV, scores, softmax, values))

    init = tuple((jnp.full((1, GROUP_ROWS), -jnp.inf, F32), jnp.zeros((1, GROUP_ROWS), F32),
                  jnp.zeros((HEAD_DIM, GROUP_ROWS), F32)) for _ in range(N_KV))
    final = lax.fori_loop(0, n_tiles, tile, init)
    for g in range(N_KV):
        _, l_f, acc = final[g]
        _store_heads(o_ref, acc / l_f, gate, g, 1)


def _sel_prompt(q_t, kv_t, selbias, smap, gates, batch, seq):
    nqt = seq // QT
    n_sel = selbias.shape[1] // N_KV
    return pl.pallas_call(
        functools.partial(_sel_prompt_kernel, n_sel=n_sel, seq=seq),
        grid=(batch, nqt),
        in_specs=[
            pl.BlockSpec((D_Q, QT), lambda b, i: (0, b * nqt + i)),
            pl.BlockSpec((1, D_KV, seq), lambda b, i: (b, 0, 0)),
            pl.BlockSpec((None, N_KV * n_sel, QT), lambda b, i: (b * nqt + i, 0, 0)),
            pl.BlockSpec(smap.shape, lambda b, i: (0, 0, 0, 0)),
            pl.BlockSpec((QT, LANES), lambda b, i: (b * nqt + i, 0)),
        ],
        out_specs=pl.BlockSpec((QT, D_Q), lambda b, i: (b * nqt + i, 0)),
        out_shape=jax.ShapeDtypeStruct((batch * seq, D_Q), F32),
        scratch_shapes=[pltpu.VMEM((N_KV, seq, 2 * HEAD_DIM), BF16), pltpu.VMEM((N_KV, HEAD_DIM, seq), BF16),
                        pltpu.VMEM((N_KV, 2 * HEAD_DIM, GROUP_ROWS), BF16)],
        compiler_params=_params(("arbitrary", "arbitrary"), 56),
        name="sel_attn_prompt",
    )(q_t, kv_t, selbias, smap, gates)


_WIN_TILES = WINDOW // QT + 1


def _win_prompt_kernel(qt_ref, *refs):
    kv_refs = refs[:_WIN_TILES]
    map_ref, gate_ref, o_ref = refs[_WIN_TILES:]
    qt = pl.program_id(1)
    gate = gate_ref[...]
    half = D_KV // 2
    span = _WIN_TILES * QT
    key = lax.broadcasted_iota(I32, (span, GROUP_ROWS), 0)
    in_seq = key >= (_WIN_TILES - 1 - qt) * QT
    k_rows = jnp.concatenate([r[0, pl.ds(0, half), :].T for r in kv_refs], axis=0)
    def scores(g):
        kg = k_rows[:, g * HEAD_DIM:(g + 1) * HEAD_DIM].astype(BF16)
        return jnp.where(in_seq, _mm(kg, _group_queries(qt_ref, g)) + map_ref[g], NEG)

    def softmax(g, st):
        e = jnp.exp2(st - jnp.max(st, axis=0, keepdims=True))
        return (e / jnp.sum(e, axis=0, keepdims=True)).astype(BF16)

    def values(g, pb):
        vgt = jnp.concatenate([r[0, pl.ds(half + g * HEAD_DIM, HEAD_DIM), :] for r in kv_refs],
                              axis=1).astype(BF16)
        _store_heads(o_ref, _mm(vgt, pb), gate, g, 2)

    _staggered(N_KV, scores, softmax, values)


def _win_prompt(q_t, kv_t, wmap, gates, batch, seq):
    nqt = seq // QT

    def kv_spec(k):
        return pl.BlockSpec((1, D_KV, QT), lambda b, i: (b, 0, jnp.maximum(i - (_WIN_TILES - 1) + k, 0)))

    return pl.pallas_call(
        _win_prompt_kernel,
        grid=(batch, nqt),
        in_specs=[pl.BlockSpec((D_Q, QT), lambda b, i: (0, b * nqt + i))]
        + [kv_spec(k) for k in range(_WIN_TILES)]
        + [pl.BlockSpec(wmap.shape, lambda b, i: (0, 0, 0)),
           pl.BlockSpec((QT, LANES), lambda b, i: (b * nqt + i, 0))],
        out_specs=pl.BlockSpec((QT, D_Q), lambda b, i: (b * nqt + i, 0)),
        out_shape=jax.ShapeDtypeStruct((batch * seq, D_Q), F32),
        compiler_params=_params(("parallel", "parallel")),
        name="win_attn_prompt",
    )(q_t, *([kv_t] * _WIN_TILES), wmap, gates)


def _group_rows(x_by_group):
    row = lax.broadcasted_iota(I32, x_by_group[0].shape, 0)
    out = x_by_group[0]
    for g in range(1, N_KV):
        out = jnp.where(row // HPG == g, x_by_group[g], out)
    return out


_CMP_SAMPLE_ROWS = 4


def _cmp_sample_kernel(q_ref, kc_ref, bias_ref, ov_ref, gate_ref, o_ref, idx_ref, *, n_sel, past):
    bias = bias_ref[...]
    half = D_KV // 2
    imps = []
    for bl in range(_CMP_SAMPLE_ROWS):
        q = q_ref[bl]
        kv = kc_ref[bl].astype(BF16)
        s = _group_rows([_mm_nt(q, kv[:, g * HEAD_DIM:(g + 1) * HEAD_DIM]) for g in range(N_KV)]) + bias
        e = jnp.exp(s - jnp.max(s, axis=-1, keepdims=True))
        p = e / jnp.sum(e, axis=-1, keepdims=True) * (bias > 0.5 * NEG).astype(F32)
        pb = p.astype(BF16)
        o = _group_rows([_mm(pb, kv[:, half + g * HEAD_DIM:half + (g + 1) * HEAD_DIM]) for g in range(N_KV)])
        o_ref[bl] = o * gate_ref[bl]
        imp_h = _mm(pb, ov_ref[...])
        imps += [imp_h[g * HPG:g * HPG + 1] + imp_h[g * HPG + 1:g * HPG + 2] + imp_h[g * HPG + 2:g * HPG + 3]
                 + imp_h[g * HPG + 3:g * HPG + 4] for g in range(N_KV)]
    imp = jnp.concatenate(imps, axis=0)
    rows, nsp = imp.shape
    j = lax.broadcasted_iota(I32, (rows, nsp), 1)
    cur = past // SEL_LEN
    forced = (j == 0) | (j == cur) | (j == cur - 1)
    valid = (j * SEL_LEN <= past) & (j < n_sel)
    score = jnp.where(valid, jnp.where(forced, jnp.inf, imp), -jnp.inf)
    slot = lax.broadcasted_iota(I32, (rows, LANES), 1)
    res = jnp.zeros((rows, LANES), I32)
    for k in range(min(SEL_TOP, n_sel)):
        best = jnp.max(score, axis=-1, keepdims=True)
        pick = jnp.min(jnp.where(score == best, j, nsp), axis=-1, keepdims=True)
        res = jnp.where(slot == k, pick, res)
        score = jnp.where(j == pick, -jnp.inf, score)
    idx_ref[...] = res


def _cmp_sample(q3, kcmp, bias, ov, gate3, n_sel, past):
    bs, n_chunk, _ = kcmp.shape
    nb = _CMP_SAMPLE_ROWS
    heads = pl.BlockSpec((nb, N_HEADS, HEAD_DIM), lambda b: (b, 0, 0))
    return pl.pallas_call(
        functools.partial(_cmp_sample_kernel, n_sel=n_sel, past=past),
        grid=(bs // nb,),
        in_specs=[
            heads,
            pl.BlockSpec((nb, n_chunk, D_KV), lambda b: (b, 0, 0)),
            pl.BlockSpec(bias.shape, lambda b: (0, 0)),
            pl.BlockSpec(ov.shape, lambda b: (0, 0)),
            heads,
        ],
        out_specs=[heads, pl.BlockSpec((nb * N_KV, LANES), lambda b: (b, 0))],
        out_shape=[
            jax.ShapeDtypeStruct((bs, N_HEADS, HEAD_DIM), F32),
            jax.ShapeDtypeStruct((bs * N_KV, LANES), I32),
        ],
        compiler_params=_params(("parallel",)),
        name="cmp_attn_sample",
    )(q3, kcmp, bias, ov, gate3)


def _sel_sample_kernel(blk_ref, phys_ref, q_ref, new_ref, bias_ref, gate_ref, *refs, top, n_pages):
    kv_refs = refs[:top]
    k_refs = [r.at[0] for r in kv_refs]
    v_refs = [r.at[1] for r in kv_refs]
    o_ref = refs[top]
    del phys_ref
    b = pl.program_id(0)
    g = pl.program_id(1)
    q = q_ref[0, 0]
    lane = lax.broadcasted_iota(I32, (8, PAGE), 1)
    scores = []
    for k in range(top):
        blk = blk_ref[(b * N_KV + g) * top + k]
        page = jnp.minimum(blk // 2, n_pages)
        s = _mm(q, k_refs[k][...].astype(BF16)) + bias_ref[page]
        ok = (lane // SEL_LEN == blk % 2) & (blk < 2 * n_pages)
        scores.append(jnp.where(ok, s, NEG))
    k_new = new_ref[0, 0, 0:1, :].astype(BF16).astype(F32)
    v_new = new_ref[0, 0, 1:2, :].astype(BF16).astype(F32)
    s_new = jnp.sum(q.astype(F32) * k_new, axis=-1, keepdims=True) + bias_ref[n_pages]
    scores.append(s_new)
    s_all = jnp.concatenate(scores, axis=1)
    e = jnp.exp(s_all - jnp.max(s_all, axis=-1, keepdims=True))
    p = e / jnp.sum(e, axis=-1, keepdims=True)
    o = p[:, top * PAGE:top * PAGE + 1].astype(BF16).astype(F32) * v_new
    for k in range(top):
        o = o + _mm_nt(p[:, k * PAGE:(k + 1) * PAGE].astype(BF16), v_refs[k][...].astype(BF16))
    o_ref[0, 0] = o * gate_ref[0, 0]


def _sel_sample(blk, phys, q4, new4, bias4, gate4, pool5, top, n_pages):
    bs = q4.shape[0]

    def kv_spec(k):
        return pl.BlockSpec(
            (None, 2, None, HEAD_DIM, PAGE),
            lambda b, g, blk_r, phys_r: (phys_r[(b * N_KV + g) * top + k], 0, g, 0, 0))

    grp = lambda b, g, blk_r, phys_r: (b, g, 0, 0)
    grid_spec = pltpu.PrefetchScalarGridSpec(
        num_scalar_prefetch=2,
        grid=(bs, N_KV),
        in_specs=[
            pl.BlockSpec((1, 1, 8, HEAD_DIM), grp),
            pl.BlockSpec((1, 1, 8, HEAD_DIM), grp),
            pl.BlockSpec((None, n_pages + 1, 8, LANES), lambda b, g, blk_r, phys_r: (g, 0, 0, 0)),
            pl.BlockSpec((1, 1, 8, HEAD_DIM), grp),
        ] + [kv_spec(k) for k in range(top)],
        out_specs=pl.BlockSpec((1, 1, 8, HEAD_DIM), grp),
    )
    return pl.pallas_call(
        functools.partial(_sel_sample_kernel, top=top, n_pages=n_pages),
        grid_spec=grid_spec,
        out_shape=jax.ShapeDtypeStruct((bs, N_KV, 8, HEAD_DIM), F32),
        compiler_params=_params(("parallel", "parallel")),
        name="sel_attn_sample",
    )(blk, phys, q4, new4, bias4, gate4, *([pool5] * top))


def _win_sample_kernel(q_ref, new_ref, st_ref, col_ref, bias_ref, gate_ref, o_ref, nst_ref):
    st = st_ref[0]
    half = D_KV // 2
    for g in range(N_KV):
        q = q_ref[0, g]
        bias = bias_ref[g]
        s_old = _mm(q, st[g * HEAD_DIM:(g + 1) * HEAD_DIM, :].astype(BF16)) + bias[:, 0:WINDOW]
        k_new = new_ref[0, g, 0:1, :].astype(BF16).astype(F32)
        v_new = new_ref[0, g, 1:2, :].astype(BF16).astype(F32)
        s_new = jnp.sum(q.astype(F32) * k_new, axis=-1, keepdims=True) + bias[:, WINDOW:WINDOW + LANES]
        s_all = jnp.concatenate([s_old, s_new], axis=1)
        e = jnp.exp(s_all - jnp.max(s_all, axis=-1, keepdims=True))
        p = e / jnp.sum(e, axis=-1, keepdims=True)
        o = _mm_nt(p[:, 0:WINDOW].astype(BF16), st[half + g * HEAD_DIM:half + (g + 1) * HEAD_DIM, :].astype(BF16))
        o = o + p[:, WINDOW:WINDOW + 1].astype(BF16).astype(F32) * v_new
        o_ref[0, g] = o * gate_ref[0, g]
    rolled = pltpu.roll(st, WINDOW - 1, axis=1)
    lane = lax.broadcasted_iota(I32, (D_KV, LANES), 1)
    nst_ref[0, :, pl.ds(0, WINDOW - LANES)] = rolled[:, 0:WINDOW - LANES]
    nst_ref[0, :, pl.ds(WINDOW - LANES, LANES)] = jnp.where(
        lane == LANES - 1, col_ref[0], rolled[:, WINDOW - LANES:WINDOW])


def _win_sample(q4, new4, state_t, newcol, bias4, gate4):
    bs = q4.shape[0]
    grp = pl.BlockSpec((1, N_KV, 8, HEAD_DIM), lambda b: (b, 0, 0, 0))
    return pl.pallas_call(
        _win_sample_kernel,
        grid=(bs,),
        in_specs=[
            grp, grp,
            pl.BlockSpec((1, D_KV, WINDOW), lambda b: (b, 0, 0)),
            pl.BlockSpec((1, D_KV, LANES), lambda b: (b, 0, 0)),
            pl.BlockSpec(bias4.shape, lambda b: (0, 0, 0)),
            grp,
        ],
        out_specs=[grp, pl.BlockSpec((1, D_KV, WINDOW), lambda b: (b, 0, 0))],
        out_shape=[
            jax.ShapeDtypeStruct((bs, N_KV, 8, HEAD_DIM), F32),
            jax.ShapeDtypeStruct((bs, D_KV, WINDOW), F32),
        ],
        compiler_params=_params(("parallel",)),
        name="win_attn_sample",
    )(q4, new4, state_t, newcol, bias4, gate4)


_MOE_ROWS = 256
_MOE_SLACK = 2 * _MOE_ROWS


def _row_copy(src, src_row, dst, dst_row, sem):
    return pltpu.make_async_copy(src.at[pl.ds(src_row, 1)], dst.at[pl.ds(dst_row, 1)], sem)


def _expert_kernel(be_ref, nu_ref, dst_ref, src_ref, nxt_ref, h_ref, w_ref, wg_ref, wu_ref, wd_ref, yk_ref,
                   xbuf, ybuf, gsem, ssem, *, n_rows, n_live):
    del be_ref
    i = pl.program_id(0)
    slot = i % 2
    n_used = nu_ref[0]

    def gather(rows_ref, dst_slot):
        for r in range(_MOE_ROWS):
            _row_copy(h_ref, rows_ref[0, r], xbuf.at[dst_slot], r, gsem.at[dst_slot]).start()

    def wait_gather(s):
        for r in range(_MOE_ROWS):
            _row_copy(h_ref, 0, xbuf.at[s], 0, gsem.at[s]).wait()

    def scatter(s):
        for r in range(_MOE_ROWS):
            _row_copy(ybuf.at[s], r, yk_ref, dst_ref[0, r], ssem.at[s]).start()

    def wait_scatter(s):
        for r in range(_MOE_ROWS):
            _row_copy(ybuf.at[s], 0, yk_ref, 0, ssem.at[s]).wait()

    @pl.when(i == 0)
    def _():
        gather(src_ref, 0)
        ybuf[1] = jnp.zeros((_MOE_ROWS, ybuf.shape[2]), F32)
        fills = [
            pltpu.make_async_copy(ybuf.at[1, pl.ds(0, min(_MOE_ROWS, n_rows - off))],
                                  yk_ref.at[pl.ds(k * n_rows + off, min(_MOE_ROWS, n_rows - off))], ssem.at[1])
            for k in range(TOP_K) for off in range(n_live, n_rows, _MOE_ROWS)]
        for cp in fills:
            cp.start()
        for cp in fills:
            cp.wait()

    @pl.when(i + 1 < n_used)
    def _():
        gather(nxt_ref, 1 - slot)

    @pl.when(i < n_used)
    def _():
        wait_gather(slot)

        @pl.when(i >= 2)
        def _():
            wait_scatter(slot)

        x = xbuf[slot].astype(BF16)
        a = _silu(_mm(x, wg_ref[...])) * _mm(x, wu_ref[...])
        ybuf[slot] = _mm(a.astype(BF16), wd_ref[...]) * w_ref[...]
        scatter(slot)

        @pl.when(i == n_used - 1)
        def _():
            @pl.when(i >= 1)
            def _():
                wait_scatter(1 - slot)

            wait_scatter(slot)


def _experts(blk_e, n_used, rows_src, rows_dst, rows_w, h2_all, n_live, w_gate, w_up, w_down):
    n_rows, d = h2_all.shape
    n_blk = blk_e.shape[0]
    src3 = rows_src.reshape(n_blk, 1, _MOE_ROWS)
    dst3 = rows_dst.reshape(n_blk, 1, _MOE_ROWS)
    wspec = lambda a: pl.BlockSpec((None,) + a.shape[1:], lambda i, be, nu: (be[i], 0, 0))
    smem_blk = lambda fn: pl.BlockSpec((None, 1, _MOE_ROWS), fn, memory_space=pltpu.SMEM)
    grid_spec = pltpu.PrefetchScalarGridSpec(
        num_scalar_prefetch=2,
        grid=(n_blk,),
        in_specs=[
            smem_blk(lambda i, be, nu: (i, 0, 0)),
            smem_blk(lambda i, be, nu: (i, 0, 0)),
            smem_blk(lambda i, be, nu: (jnp.minimum(i + 1, n_blk - 1), 0, 0)),
            pl.BlockSpec(memory_space=pl.ANY),
            pl.BlockSpec((_MOE_ROWS, 1), lambda i, be, nu: (i, 0)),
            wspec(w_gate), wspec(w_up), wspec(w_down),
        ],
        out_specs=pl.BlockSpec(memory_space=pl.ANY),
        scratch_shapes=[
            pltpu.VMEM((2, _MOE_ROWS, d), F32),
            pltpu.VMEM((2, _MOE_ROWS, d), F32),
            pltpu.SemaphoreType.DMA((2,)),
            pltpu.SemaphoreType.DMA((2,)),
        ],
    )
    return pl.pallas_call(
        functools.partial(_expert_kernel, n_rows=n_rows, n_live=n_live),
        grid_spec=grid_spec,
        out_shape=jax.ShapeDtypeStruct((TOP_K * n_rows, d), F32),
        compiler_params=pltpu.CompilerParams(dimension_semantics=("arbitrary",), vmem_limit_bytes=56 * MIB,
                                             disable_bounds_checks=True),
        name="moe_experts",
    )(blk_e, n_used, dst3, src3, src3, h2_all, rows_w, w_gate, w_up, w_down)


def _combine_kernel(x_ref, y0_ref, y1_ref, g_ref, o_ref):
    x = x_ref[...] + (y0_ref[...] + y1_ref[...])
    ms = jnp.mean(x * x, axis=-1, keepdims=True)
    o_ref[...] = x * lax.rsqrt(ms + EPS) * g_ref[...]


def _combine(x1, yk, gfin, row0, tm):
    n, d = x1.shape
    yk3 = yk.reshape(TOP_K, -1, d)
    blk0 = row0 // tm
    return pl.pallas_call(
        _combine_kernel,
        grid=(n // tm,),
        in_specs=[
            pl.BlockSpec((tm, d), lambda i: (i, 0)),
            pl.BlockSpec((None, tm, d), lambda i: (0, blk0 + i, 0)),
            pl.BlockSpec((None, tm, d), lambda i: (1, blk0 + i, 0)),
            pl.BlockSpec((1, d), lambda i: (0, 0)),
        ],
        out_specs=pl.BlockSpec((tm, d), lambda i: (i, 0)),
        out_shape=jax.ShapeDtypeStruct((n, d), F32),
        compiler_params=_params(("parallel",)),
        name="moe_combine",
    )(x1, yk3, yk3, gfin)


def _dispatch(eid, w, counts, n_rows):
    a = eid.shape[0]
    tb = _MOE_ROWS
    order = jnp.argsort(eid).astype(I32)
    padded = (counts + tb - 1) // tb * tb
    seg_end = jnp.cumsum(padded)
    seg_start = seg_end - padded
    start = jnp.cumsum(counts) - counts
    n_blk = (a + N_EXPERTS * (tb - 1) + tb - 1) // tb
    blk_e = jnp.minimum(jnp.sum((seg_end[None, :] <= (jnp.arange(n_blk) * tb)[:, None]).astype(I32), axis=1),
                        N_EXPERTS - 1)
    blk = jnp.arange(n_blk)[:, None]
    r = jnp.arange(tb)[None, :]
    off = (blk * tb - seg_start[blk_e][:, None]) + r
    valid = off < counts[blk_e][:, None]
    asg = order[jnp.clip(start[blk_e][:, None] + off, 0, a - 1)]
    rows_src = jnp.where(valid, asg // TOP_K, 0).astype(I32)
    rows_dst = jnp.where(valid, (asg % TOP_K) * n_rows + asg // TOP_K, (blk % 2) * n_rows + (n_rows - tb) + r)
    rows_w = jnp.where(valid, w[asg], 0.0)
    n_used = (seg_end[-1] // tb).astype(I32).reshape(1)
    return rows_src, rows_dst.astype(I32), rows_w.reshape(-1, 1), blk_e.astype(I32), n_used


def _overlap(n_chunk, n_sel, n_sel_pad):
    cs = jnp.arange(n_chunk)[:, None] * CMP_STRIDE
    ss = jnp.arange(n_sel_pad)[None, :] * SEL_LEN
    hit = (cs < ss + SEL_LEN) & (cs + CMP_LEN > ss) & (jnp.arange(n_sel_pad)[None, :] < n_sel)
    return hit.astype(BF16)


def kernel(x_prompt, x_sample, cache_cmp_kv, cache_sel_kv, state_win_kv, state_conv, page_table, rel_bias,
           norm_mix, w_in, b_in, conv_w, conv_b, conv_ln_g, conv_ln_b, w_conv_out, b_conv_out,
           cmp_pe_k, cmp_w1_k, cmp_w2_k, cmp_pe_v, cmp_w1_v, cmp_w2_v, w_nsa_out, w_out, norm_ffn,
           w_rg, b_rg, w_re, b_re, w_gate, w_up, w_down, norm_final):
    batch, seq, _ = x_prompt.shape
    bs = x_sample.shape[0]
    n_pool = cache_cmp_kv.shape[1]
    n_pages = page_table.shape[1]
    past = n_pages * PAGE
    n_tok = batch * seq
    assert bs + _MOE_ROWS <= _MOE_SLACK and n_tok % bs == 0 and seq % _TK == 0 and bs % 16 == 0
    l = 0

    wt = w_in[l].T.astype(BF16)
    bias_in = b_in[l]
    cuts = [0, D_CONV, 2 * D_CONV, 2 * D_CONV + D_Q, 2 * D_CONV + D_Q + 3 * D_KV]
    cuts += [cuts[-1] + 3 * N_HEADS, cuts[-1] + 3 * N_HEADS + D_MODEL, cuts[-1] + 3 * N_HEADS + 2 * D_MODEL]
    seg = lambda k: (wt[cuts[k]:cuts[k + 1]], bias_in[cuts[k]:cuts[k + 1]][None, :])
    (wa, ba), (wb, bb), (wq, bq), (wkv, bkv), (wzg, bzg), (wga, bga), (wgb, bgb) = [seg(k) for k in range(7)]
    wzg = jnp.pad(wzg, ((0, LANES - 3 * N_HEADS), (0, 0)))
    bzg = jnp.pad(bzg, ((0, 0), (0, LANES - 3 * N_HEADS)))
    wc = w_conv_out[l].astype(BF16)
    wn = w_nsa_out[l].astype(BF16)
    wo = w_out[l].astype(BF16)
    wr = jnp.zeros((LANES, D_MODEL), F32).at[0:N_GROUPS].set(w_rg[l].T).at[8:8 + N_EXPERTS].set(w_re[l].T)
    br = jnp.zeros((LANES,), F32).at[0:N_GROUPS].set(b_rg[l]).at[8:8 + N_EXPERTS].set(b_re[l].reshape(-1))
    wr, br = wr.astype(BF16), br[:, None]
    cw = _compress_weights(cmp_pe_k[l], cmp_w1_k[l], cmp_w2_k[l], cmp_pe_v[l], cmp_w1_v[l], cmp_w2_v[l])
    rb_t = rel_bias.T
    gmix, gffn, gfin = norm_mix[l][None, :], norm_ffn[l][None, :], norm_final[None, :]
    vec = lambda a: a[l][None, :]

    q_scale = HEAD_DIM ** -0.5
    xp = x_prompt.reshape(n_tok, D_MODEL)
    xs = x_sample.reshape(bs, D_MODEL)
    hp, up, qp, gates_p, *kv_p = _project_prompt(
        xp, gmix, wa, wb, wq, wkv, wzg, ba, bb, bq.reshape(-1, 1), bkv.reshape(-1, 1), bzg,
        q_scale * LOG2E, batch, seq, 512)
    hs = _rmsnorm(xs, gmix, bs)
    us = _glu(hs, wa, wb, ba, bb, bs, 512)
    qs = _proj(hs, wq, bq, lambda z: z * q_scale, BF16, bs, 512, "proj_q")
    gates_s = _proj(hs, wzg, bzg, _sigmoid, F32, bs, LANES, "proj_head_gates")
    kv_s = _proj(hs, wkv, bkv, lambda z: z, F32, bs, 512, "proj_kv_sample")

    conv_args = (conv_w[l], vec(conv_b), vec(conv_ln_g), vec(conv_ln_b))
    cp = _conv_prompt(up, *conv_args, batch, seq, 256)
    cs, new_conv_s = _conv_sample(state_conv[l].transpose(1, 0, 2), us, *conv_args)

    nqt = seq // QT
    n_chunk_p = seq // CMP_STRIDE
    n_chunk_s = n_pages * _CHUNKS_PER_PAGE
    cmap, wmap, smap = _bias_prompt(rb_t, nqt, n_chunk_p)
    cbias_s, wbias_s, sbias_s = _bias_sample(rb_t, past, n_chunk_s, n_pages)

    ident = jnp.zeros((1,), I32)
    kcmp_p = _compress(kv_p[0], ident, cw, batch, seq // PAGE, seq // PAGE, False)
    feat_major = lambda a: a.transpose(0, 2, 3, 4, 1)
    pool_cmp = feat_major(cache_cmp_kv[l]).reshape(n_pool, D_KV, PAGE)
    pp_s = min(32, n_pages)
    kcmp_s = _compress(pool_cmp, page_table.reshape(-1), cw, bs, n_pages, pp_s, True)

    n_sel_p = seq // SEL_LEN
    o_cmp_p, selbias = _cmp_prompt(qp, kcmp_p, cmap, gates_p, _overlap(n_chunk_p, n_sel_p, n_sel_p).T,
                                   batch, seq)
    n_sel_s = -(-(past + 1) // SEL_LEN)
    n_sel_pad = -(-n_sel_s // LANES) * LANES
    top_s = min(SEL_TOP, n_sel_s)
    head_gate = lambda k: jnp.broadcast_to(
        gates_s[:, k * N_HEADS:(k + 1) * N_HEADS, None], (bs, N_HEADS, HEAD_DIM))
    qs3 = qs.reshape(bs, N_HEADS, HEAD_DIM)
    o_cmp_s, idx_s = _cmp_sample(qs3, kcmp_s, cbias_s, _overlap(n_chunk_s, n_sel_s, n_sel_pad), head_gate(0),
                                 n_sel_s, past)

    o_sel_p = _sel_prompt(qp, kv_p[1], selbias, smap, gates_p, batch, seq)
    by_group = lambda a: jnp.pad(a.reshape(bs, N_KV, HPG, -1), ((0, 0), (0, 0), (0, 8 - HPG), (0, 0)))
    q4 = by_group(qs3)
    blk = idx_s[:, :top_s].reshape(-1)
    phys = jnp.take_along_axis(
        page_table, jnp.minimum(blk // 2, n_pages - 1).reshape(bs, -1), axis=1).reshape(-1)
    kv_new = lambda k: jnp.pad(
        kv_s[:, k * D_KV:(k + 1) * D_KV].reshape(bs, 2, N_KV, HEAD_DIM).transpose(0, 2, 1, 3),
        ((0, 0), (0, 0), (0, 6), (0, 0)))
    bias_by_group = lambda a: jnp.pad(
        jnp.moveaxis(a, -2, 0).reshape((N_KV, HPG) + a.shape[:-2] + a.shape[-1:]),
        ((0, 0), (0, 8 - HPG)) + ((0, 0),) * (a.ndim - 1))
    sb4 = jnp.moveaxis(bias_by_group(sbias_s), 1, 2)
    pool_sel = feat_major(cache_sel_kv[l])
    o_sel_s = _sel_sample(blk, phys, q4, kv_new(1), sb4, by_group(head_gate(1)), pool_sel, top_s, n_pages)

    o_win_p = _win_prompt(qp, kv_p[2], wmap, gates_p, batch, seq)
    win_t = feat_major(state_win_kv[l]).reshape(bs, D_KV, WINDOW)
    newcol = jnp.broadcast_to(kv_s[:, 2 * D_KV:3 * D_KV, None], (bs, D_KV, LANES))
    o_win_s, new_win_s = _win_sample(q4, kv_new(2), win_t, newcol, bias_by_group(wbias_s), by_group(head_gate(2)))

    m_p = _merge_a(hp, cp, o_cmp_p, o_sel_p, o_win_p, wga, wgb, wc, wn, bga, bgb, vec(b_conv_out), 512, 512)
    h2_all = jnp.zeros((n_tok + _MOE_SLACK, D_MODEL), F32)
    x1_p, h2_all, eid_p, rw_p, cnt_p = _merge_b(xp, m_p, wo, gffn, wr, br, h2_all, 0, 256)
    from_groups = lambda a: a[:, :, :HPG, :].reshape(bs, D_Q)
    m_s = _merge_a(hs, cs, o_cmp_s.reshape(bs, D_Q), from_groups(o_sel_s), from_groups(o_win_s),
                   wga, wgb, wc, wn, bga, bgb, vec(b_conv_out), bs, 512)
    x1_s, h2_all, eid_s, rw_s, cnt_s = _merge_b(xs, m_s, wo, gffn, wr, br, h2_all, n_tok, bs)

    eid = jnp.concatenate([eid_p[:TOP_K].T, eid_s[:TOP_K].T], axis=0).reshape(-1)
    rw = jnp.concatenate([rw_p[:TOP_K].T, rw_s[:TOP_K].T], axis=0).reshape(-1)
    counts = jnp.sum(cnt_p[:, :, 0], axis=0) + jnp.sum(cnt_s[:, :, 0], axis=0)
    rows_src, rows_dst, rows_w, blk_e, n_used = _dispatch(eid, rw, counts, h2_all.shape[0])
    yk = _experts(blk_e, n_used, rows_src, rows_dst, rows_w, h2_all, n_tok + bs, w_gate[l].astype(BF16),
                  w_up[l].astype(BF16), w_down[l].astype(BF16))
    y_p = _combine(x1_p, yk, gfin, 0, 256)
    y_s = _combine(x1_s, yk, gfin, n_tok, bs)

    def cache_rows(a_t, rows):
        return a_t.reshape(a_t.shape[0], 2, N_KV, HEAD_DIM, rows).transpose(0, 4, 1, 2, 3)[None]

    new_row = lambda k: kv_s[:, k * D_KV:(k + 1) * D_KV].reshape(1, bs, 1, 2, N_KV, HEAD_DIM)
    hist = CONV_W - 1
    return (
        y_p.reshape(batch, seq, D_MODEL),
        y_s.reshape(bs, 1, D_MODEL),
        cache_rows(kv_p[0], seq),
        new_row(0),
        cache_rows(kv_p[1], seq),
        new_row(1),
        cache_rows(kv_p[2][:, :, seq - WINDOW:], WINDOW),
        cache_rows(new_win_s, WINDOW),
        up.reshape(batch, seq, D_CONV)[None, :, seq - hist:],
        new_conv_s.transpose(1, 0, 2)[None],
    )
```

```python
import functools
import math

import jax
import jax.numpy as jnp
from jax import lax
from jax.experimental import pallas as pl
from jax.experimental.pallas import tpu as pltpu

F32 = jnp.float32
BF16 = jnp.bfloat16
I32 = jnp.int32

D_MODEL = 2048
D_CONV = 1024
CONV_W = 31
N_HEADS = 16
HEAD_DIM = 64
N_KV = 4
HPG = N_HEADS // N_KV
D_Q = N_HEADS * HEAD_DIM
D_KV = 2 * N_KV * HEAD_DIM
CMP_LEN = 32
CMP_STRIDE = 16
CMP_HID = 64
SEL_LEN = 64
SEL_TOP = 16
WINDOW = 512
N_BUCKETS = 32
MAX_DIST = 128
N_GROUPS = 4
EXP_PER_GROUP = 8
N_EXPERTS = N_GROUPS * EXP_PER_GROUP
TOP_K = 2
D_EXPERT = 1024
EPS = 1e-6
NEG = -1e30
PAGE = 128
LANES = 128
QT = 128
GROUP_ROWS = HPG * QT
MIB = 1024 * 1024
LOG2E = math.log2(math.e)


def _params(sem, vmem_mib=48):
    return pltpu.CompilerParams(dimension_semantics=sem, vmem_limit_bytes=vmem_mib * MIB)


def _mm_nt(a, b):
    return lax.dot_general(a, b, (((1,), (1,)), ((), ())), preferred_element_type=F32)


def _mm(a, b):
    return jnp.dot(a, b, preferred_element_type=F32)


def _sigmoid(z):
    return 1.0 / (1.0 + jnp.exp(-z))


def _silu(z):
    return z * _sigmoid(z)


def _gelu_tanh(z):
    return 0.5 * z * (1.0 + jnp.tanh(math.sqrt(2.0 / math.pi) * (z + 0.044715 * (z * z * z))))


def _rmsnorm_kernel(x_ref, g_ref, o_ref):
    x = x_ref[...]
    ms = jnp.mean(x * x, axis=-1, keepdims=True)
    o_ref[...] = (x * lax.rsqrt(ms + EPS) * g_ref[...]).astype(o_ref.dtype)


def _rmsnorm(x, g, tm):
    n, d = x.shape
    return pl.pallas_call(
        _rmsnorm_kernel,
        grid=(n // tm,),
        in_specs=[pl.BlockSpec((tm, d), lambda i: (i, 0)), pl.BlockSpec((1, d), lambda i: (0, 0))],
        out_specs=pl.BlockSpec((tm, d), lambda i: (i, 0)),
        out_shape=jax.ShapeDtypeStruct((n, d), BF16),
        compiler_params=_params(("parallel",)),
        name="rmsnorm",
    )(x, g)


def _proj_kernel(h_ref, w_ref, b_ref, o_ref, *, act):
    z = _mm_nt(h_ref[...], w_ref[...]) + b_ref[...]
    o_ref[...] = act(z).astype(o_ref.dtype)


def _proj(h, wt, b, act, out_dtype, tm, tn, name):
    n, k = h.shape
    nout = wt.shape[0]
    return pl.pallas_call(
        functools.partial(_proj_kernel, act=act),
        grid=(n // tm, nout // tn),
        in_specs=[
            pl.BlockSpec((tm, k), lambda i, j: (i, 0)),
            pl.BlockSpec((tn, k), lambda i, j: (j, 0)),
            pl.BlockSpec((1, tn), lambda i, j: (0, j)),
        ],
        out_specs=pl.BlockSpec((tm, tn), lambda i, j: (i, j)),
        out_shape=jax.ShapeDtypeStruct((n, nout), out_dtype),
        compiler_params=_params(("parallel", "parallel")),
        name=name,
    )(h, wt, b)


def _proj_t_kernel(h_ref, w_ref, b_ref, o_ref, *, act):
    z = _mm_nt(w_ref[...], h_ref[...]) + b_ref[...]
    o_ref[...] = act(z).astype(o_ref.dtype)


def _proj_t(h, wt, bcol, act, out_dtype, tm, tn, name):
    n, k = h.shape
    nout = wt.shape[0]
    return pl.pallas_call(
        functools.partial(_proj_t_kernel, act=act),
        grid=(n // tm, nout // tn),
        in_specs=[
            pl.BlockSpec((tm, k), lambda i, j: (i, 0)),
            pl.BlockSpec((tn, k), lambda i, j: (j, 0)),
            pl.BlockSpec((tn, 1), lambda i, j: (j, 0)),
        ],
        out_specs=pl.BlockSpec((tn, tm), lambda i, j: (j, i)),
        out_shape=jax.ShapeDtypeStruct((nout, n), out_dtype),
        compiler_params=_params(("parallel", "parallel")),
        name=name,
    )(h, wt, bcol)


def _glu_kernel(h_ref, wa_ref, wb_ref, ba_ref, bb_ref, o_ref):
    h = h_ref[...]
    a = _mm_nt(h, wa_ref[...]) + ba_ref[...]
    b = _mm_nt(h, wb_ref[...]) + bb_ref[...]
    o_ref[...] = a * _sigmoid(b)


def _glu(h, wat, wbt, ba, bb, tm, tn):
    n, k = h.shape
    nout = wat.shape[0]
    wspec = pl.BlockSpec((tn, k), lambda i, j: (j, 0))
    bspec = pl.BlockSpec((1, tn), lambda i, j: (0, j))
    return pl.pallas_call(
        _glu_kernel,
        grid=(n // tm, nout // tn),
        in_specs=[pl.BlockSpec((tm, k), lambda i, j: (i, 0)), wspec, wspec, bspec, bspec],
        out_specs=pl.BlockSpec((tm, tn), lambda i, j: (i, j)),
        out_shape=jax.ShapeDtypeStruct((n, nout), F32),
        compiler_params=_params(("parallel", "parallel")),
        name="glu",
    )(h, wat, wbt, ba, bb)


def _kvt_kernel(h_ref, w_ref, b_ref, *o_refs):
    h = h_ref[...]
    for kind, o_ref in enumerate(o_refs):
        rows = pl.ds(kind * D_KV, D_KV)
        o_ref[0] = _mm_nt(w_ref[rows, :], h) + b_ref[rows, :]


def _kv_transposed(h, wt, bcol, batch, seq, tm):
    n, k = h.shape
    kinds = wt.shape[0] // D_KV
    nt = seq // tm
    return pl.pallas_call(
        _kvt_kernel,
        grid=(n // tm,),
        in_specs=[
            pl.BlockSpec((tm, k), lambda i: (i, 0)),
            pl.BlockSpec(wt.shape, lambda i: (0, 0)),
            pl.BlockSpec(bcol.shape, lambda i: (0, 0)),
        ],
        out_specs=[pl.BlockSpec((1, D_KV, tm), lambda i: (i // nt, 0, i % nt))] * kinds,
        out_shape=[jax.ShapeDtypeStruct((batch, D_KV, seq), F32)] * kinds,
        compiler_params=_params(("parallel",)),
        name="kv_transposed",
    )(h, wt, bcol)


def _project_prompt_kernel(x_ref, g_ref, wa_ref, wb_ref, wq_ref, wkv_ref, wzg_ref, ba_ref, bb_ref, bq_ref,
                           bkv_ref, bzg_ref, h_ref, u_ref, qt_ref, gates_ref, *kv_refs, q_scale):
    x = x_ref[...]
    ms = jnp.mean(x * x, axis=-1, keepdims=True)
    h = (x * lax.rsqrt(ms + EPS) * g_ref[...]).astype(BF16)
    h_ref[...] = h
    a = _mm_nt(h, wa_ref[...]) + ba_ref[...]
    b = _mm_nt(h, wb_ref[...]) + bb_ref[...]
    u_ref[...] = a * _sigmoid(b)
    qt_ref[...] = ((_mm_nt(wq_ref[...], h) + bq_ref[...]) * q_scale).astype(qt_ref.dtype)
    gates_ref[...] = _sigmoid(_mm_nt(h, wzg_ref[...]) + bzg_ref[...])
    for kind, o_ref in enumerate(kv_refs):
        rows = pl.ds(kind * D_KV, D_KV)
        o_ref[0] = _mm_nt(wkv_ref[rows, :], h) + bkv_ref[rows, :]


def _project_prompt(x, g, wa, wb, wq, wkv, wzg, ba, bb, bq_col, bkv_col, bzg, q_scale, batch, seq, tm):
    n, d = x.shape
    kinds = wkv.shape[0] // D_KV
    nt = seq // tm
    tok = lambda width: pl.BlockSpec((tm, width), lambda i: (i, 0))
    const = lambda a: pl.BlockSpec(a.shape, lambda i: (0,) * a.ndim, pipeline_mode=pl.Buffered(1))
    consts = (g, wa, wb, wq, wkv, wzg, ba, bb, bq_col, bkv_col, bzg)
    return pl.pallas_call(
        functools.partial(_project_prompt_kernel, q_scale=q_scale),
        grid=(n // tm,),
        in_specs=[tok(d)] + [const(a) for a in consts],
        out_specs=[tok(d), tok(D_CONV), pl.BlockSpec((D_Q, tm), lambda i: (0, i)), tok(LANES)]
        + [pl.BlockSpec((1, D_KV, tm), lambda i: (i // nt, 0, i % nt))] * kinds,
        out_shape=[
            jax.ShapeDtypeStruct((n, d), BF16),
            jax.ShapeDtypeStruct((n, D_CONV), F32),
            jax.ShapeDtypeStruct((D_Q, n), BF16),
            jax.ShapeDtypeStruct((n, LANES), F32),
        ] + [jax.ShapeDtypeStruct((batch, D_KV, seq), F32)] * kinds,
        compiler_params=_params(("parallel",), 56),
        name="project_prompt",
    )(x, *consts)


def _merge_a_kernel(h_ref, c_ref, o1_ref, o2_ref, o3_ref, wga_ref, wgb_ref, wc_ref, wn_ref,
                    bga_ref, bgb_ref, bc_ref, m_ref):
    h = h_ref[...]
    ga = _sigmoid(_mm_nt(h, wga_ref[...]) + bga_ref[...])
    gb = _sigmoid(_mm_nt(h, wgb_ref[...]) + bgb_ref[...])
    yc = _mm(c_ref[...], wc_ref[...]) + bc_ref[...]
    o = (o1_ref[...] + o2_ref[...] + o3_ref[...]).astype(BF16)
    yn = _mm(o, wn_ref[...])
    m_ref[...] = (ga * yc + gb * yn).astype(m_ref.dtype)


def _merge_a(h, c, o1, o2, o3, wgat, wgbt, wc, wn, bga, bgb, bc, tm, tn):
    n, k = h.shape
    tok = lambda width: pl.BlockSpec((tm, width), lambda i, j: (i, 0))
    wt_spec = pl.BlockSpec((tn, k), lambda i, j: (j, 0))
    w_spec = pl.BlockSpec((c.shape[1], tn), lambda i, j: (0, j))
    b_spec = pl.BlockSpec((1, tn), lambda i, j: (0, j))
    return pl.pallas_call(
        _merge_a_kernel,
        grid=(n // tm, D_MODEL // tn),
        in_specs=[tok(k), tok(D_CONV), tok(D_Q), tok(D_Q), tok(D_Q), wt_spec, wt_spec, w_spec, w_spec,
                  b_spec, b_spec, b_spec],
        out_specs=pl.BlockSpec((tm, tn), lambda i, j: (i, j)),
        out_shape=jax.ShapeDtypeStruct((n, D_MODEL), BF16),
        compiler_params=_params(("parallel", "parallel")),
        name="merge_gates",
    )(h, c, o1, o2, o3, wgat, wgbt, wc, wn, bga, bgb, bc)


def _merge_b_kernel(x_ref, m_ref, wo_ref, gn_ref, wr_ref, br_ref, *rest, n_real):
    outs = rest[-5:]
    h2_ref = outs[1]

    @pl.when(pl.program_id(0) < n_real)
    def _():
        _merge_b_tile(x_ref, m_ref, wo_ref, gn_ref, wr_ref, br_ref, *outs)

    @pl.when(pl.program_id(0) >= n_real)
    def _():
        h2_ref[...] = jnp.zeros(h2_ref.shape, h2_ref.dtype)


def _merge_b_tile(x_ref, m_ref, wo_ref, gn_ref, wr_ref, br_ref, x1_ref, h2_ref, eid_ref, rw_ref, cnt_ref):
    x1 = x_ref[...] + _mm(m_ref[...], wo_ref[...])
    x1_ref[...] = x1
    ms = jnp.mean(x1 * x1, axis=-1, keepdims=True)
    h2 = x1 * lax.rsqrt(ms + EPS) * gn_ref[...]
    h2_ref[...] = h2
    logits = _mm_nt(wr_ref[...], h2.astype(BF16)) + br_ref[...]
    tm = logits.shape[1]
    row = lax.broadcasted_iota(I32, (8, tm), 0)
    lg = jnp.where(row < N_GROUPS, logits[0:8], -jnp.inf)
    gmax = jnp.max(lg, axis=0, keepdims=True)
    gidx = jnp.min(jnp.where(lg == gmax, row, 8), axis=0, keepdims=True)
    wg = 1.0 / jnp.sum(jnp.exp(lg - gmax), axis=0, keepdims=True)
    le = jnp.zeros((8, tm), F32)
    for g in range(N_GROUPS):
        le = jnp.where(gidx == g, logits[8 + 8 * g:16 + 8 * g], le)
    ee = jnp.exp(le - jnp.max(le, axis=0, keepdims=True))
    pz = ee / jnp.sum(ee, axis=0, keepdims=True)
    p1 = jnp.max(pz, axis=0, keepdims=True)
    i1 = jnp.min(jnp.where(pz == p1, row, 8), axis=0, keepdims=True)
    pz2 = jnp.where(row == i1, -1.0, pz)
    p2 = jnp.max(pz2, axis=0, keepdims=True)
    i2 = jnp.min(jnp.where(pz2 == p2, row, 8), axis=0, keepdims=True)
    den = p1 + p2
    e1 = gidx * EXP_PER_GROUP + i1
    e2 = gidx * EXP_PER_GROUP + i2
    eid_ref[...] = jnp.where(row == 0, e1, jnp.where(row == 1, e2, 0))
    rw_ref[...] = jnp.where(row == 0, wg * p1 / den, jnp.where(row == 1, wg * p2 / den, 0.0))
    erow = lax.broadcasted_iota(I32, (N_EXPERTS, tm), 0)
    hits = (erow == e1).astype(F32) + (erow == e2).astype(F32)
    cnt_ref[...] = jnp.broadcast_to(jnp.sum(hits, axis=1, keepdims=True), (N_EXPERTS, LANES)).astype(I32)


def _merge_b(x, m, wo, gn, wr, br, tm, h2_all=None, row0=0, slack=0):
    n, d = x.shape
    blk0 = row0 // tm
    n_real = n // tm
    real = lambda i: jnp.minimum(i, n_real - 1)
    shared = [] if h2_all is None else [h2_all]
    h2_rows = n + slack if h2_all is None else h2_all.shape[0]
    return pl.pallas_call(
        functools.partial(_merge_b_kernel, n_real=n_real),
        grid=((n + slack) // tm,),
        in_specs=[
            pl.BlockSpec((tm, d), lambda i: (real(i), 0)),
            pl.BlockSpec((tm, d), lambda i: (real(i), 0)),
            pl.BlockSpec((d, d), lambda i: (0, 0)),
            pl.BlockSpec((1, d), lambda i: (0, 0)),
            pl.BlockSpec((LANES, d), lambda i: (0, 0)),
            pl.BlockSpec((LANES, 1), lambda i: (0, 0)),
        ] + [pl.BlockSpec(memory_space=pl.ANY)] * len(shared),
        out_specs=[
            pl.BlockSpec((tm, d), lambda i: (real(i), 0)),
            pl.BlockSpec((tm, d), lambda i: (blk0 + i, 0)),
            pl.BlockSpec((8, tm), lambda i: (0, real(i))),
            pl.BlockSpec((8, tm), lambda i: (0, real(i))),
            pl.BlockSpec((None, N_EXPERTS, LANES), lambda i: (real(i), 0, 0)),
        ],
        out_shape=[
            jax.ShapeDtypeStruct((n, d), F32),
            jax.ShapeDtypeStruct((h2_rows, d), F32),
            jax.ShapeDtypeStruct((8, n), I32),
            jax.ShapeDtypeStruct((8, n), F32),
            jax.ShapeDtypeStruct((n_real, N_EXPERTS, LANES), I32),
        ],
        input_output_aliases={6: 1} if shared else {},
        compiler_params=_params(("arbitrary",)),
        name="merge_out_router",
    )(x, m, wo, gn, wr, br, *shared)


_HALO = 32


def _conv_prompt_kernel(u_ref, prev_ref, w_ref, b_ref, g_ref, beta_ref, o_ref, buf_ref, acc_ref, sh_ref, *, tt):
    i = pl.program_id(1)
    prev = prev_ref[pl.ds(tt - _HALO, _HALO), :]
    buf_ref[pl.ds(0, _HALO), :] = jnp.where(i > 0, prev, 0.0)
    buf_ref[pl.ds(_HALO, tt), :] = u_ref[...]
    off = _HALO - (CONV_W - 1)
    sub = 8
    rows = tt
    for cc in range(D_CONV // LANES):
        cols = pl.ds(cc * LANES, LANES)
        for r0 in range(0, tt, rows):
            acc = jnp.zeros((rows, LANES), F32)
            for r in range(sub):
                taps = [k for k in range(CONV_W) if (off + k) % sub == r]
                n_shift = (rows + _HALO - r) // sub * sub
                sh_ref[r, pl.ds(0, n_shift), :] = buf_ref[pl.ds(r0 + r, n_shift), cols]
                for k in taps:
                    acc = acc + sh_ref[r, pl.ds(off + k - r, rows), :] * w_ref[pl.ds(k, 1), cols]
            acc_ref[pl.ds(r0, rows), cols] = acc + b_ref[:, cols]
    c = acc_ref[...]
    mu = jnp.mean(c, axis=-1, keepdims=True)
    var = jnp.mean(jnp.square(c - mu), axis=-1, keepdims=True)
    y = (c - mu) * lax.rsqrt(var + EPS) * g_ref[...] + beta_ref[...]
    o_ref[...] = _silu(y).astype(o_ref.dtype)


def _conv_prompt(u, w, b, g, beta, batch, seq, tt):
    nt = seq // tt
    vec = pl.BlockSpec((1, D_CONV), lambda bi, i: (0, 0))
    return pl.pallas_call(
        functools.partial(_conv_prompt_kernel, tt=tt),
        grid=(batch, nt),
        in_specs=[
            pl.BlockSpec((tt, D_CONV), lambda bi, i: (bi * nt + i, 0)),
            pl.BlockSpec((tt, D_CONV), lambda bi, i: (bi * nt + jnp.maximum(i - 1, 0), 0)),
            pl.BlockSpec((CONV_W, D_CONV), lambda bi, i: (0, 0)),
            vec, vec, vec,
        ],
        out_specs=pl.BlockSpec((tt, D_CONV), lambda bi, i: (bi * nt + i, 0)),
        out_shape=jax.ShapeDtypeStruct((batch * seq, D_CONV), BF16),
        scratch_shapes=[pltpu.VMEM((tt + _HALO, D_CONV), F32), pltpu.VMEM((tt, D_CONV), F32),
                        pltpu.VMEM((8, tt + _HALO, LANES), F32)],
        compiler_params=_params(("parallel", "parallel")),
        name="conv_prompt",
    )(u, u, w, b, g, beta)


def _conv_sample_kernel(st_ref, u_ref, w_ref, b_ref, g_ref, beta_ref, c_ref, new_ref):
    hist = CONV_W - 1
    u = u_ref[...]
    acc = u * w_ref[pl.ds(hist, 1), :] + b_ref[...]
    for t in range(hist):
        acc = acc + st_ref[t] * w_ref[pl.ds(t, 1), :]
    mu = jnp.mean(acc, axis=-1, keepdims=True)
    var = jnp.mean(jnp.square(acc - mu), axis=-1, keepdims=True)
    y = (acc - mu) * lax.rsqrt(var + EPS) * g_ref[...] + beta_ref[...]
    c_ref[...] = _silu(y).astype(c_ref.dtype)
    for t in range(hist - 1):
        new_ref[t] = st_ref[t + 1]
    new_ref[hist - 1] = u


def _conv_sample(state_t, u, w, b, g, beta):
    hist, bs, _ = state_t.shape
    return pl.pallas_call(
        _conv_sample_kernel,
        out_shape=[jax.ShapeDtypeStruct((bs, D_CONV), BF16), jax.ShapeDtypeStruct((hist, bs, D_CONV), F32)],
        compiler_params=pltpu.CompilerParams(vmem_limit_bytes=48 * MIB),
        name="conv_sample",
    )(state_t, u, w, b, g, beta)


def _bias_lookup(rb_ref, head, dist, valid):
    n = jnp.maximum(dist, 0)
    max_exact = N_BUCKETS // 2
    nf = jnp.maximum(n, 1).astype(F32)
    large = max_exact + (jnp.log(nf / max_exact) / math.log(MAX_DIST / max_exact)
                         * (N_BUCKETS - max_exact)).astype(I32)
    bucket = jnp.where(n < max_exact, n, jnp.minimum(large, N_BUCKETS - 1))
    val = jnp.zeros(dist.shape, F32)
    for k in range(N_BUCKETS):
        val = jnp.where(bucket == k, rb_ref[head, k], val)
    return jnp.where(valid, val, NEG)


def _bias_prompt_kernel(rb_ref, cmp_ref, win_ref, sel_ref, *, nqt, n_chunk):
    g = pl.program_id(0)
    key = lax.broadcasted_iota(I32, (LANES, QT), 0)
    qry = lax.broadcasted_iota(I32, (LANES, QT), 1)
    for hh in range(HPG):
        head = g * HPG + hh
        cols = pl.ds(hh * QT, QT)
        lookup = lambda dist, valid: _bias_lookup(rb_ref, head, dist, valid) * LOG2E

        for eb in range(_cmp_map_rows(nqt, n_chunk) // LANES):
            block = key + eb * LANES - _cmp_map_shift(nqt, 0)
            dist = qry - block * CMP_STRIDE - (CMP_LEN - 1)
            cmp_ref[0, pl.ds(eb * LANES, LANES), cols] = lookup(dist, dist >= 0)
        for jb in range((WINDOW + QT) // LANES):
            dist = qry - (key + jb * LANES) + WINDOW
            win_ref[0, pl.ds(jb * LANES, LANES), cols] = lookup(dist, (dist >= 0) & (dist <= WINDOW))
        for kind in range(3):
            for jb in range(2):
                dist = kind * QT + qry - (key + jb * LANES)
                sel_ref[0, kind, pl.ds(jb * LANES, LANES), cols] = lookup(dist, dist >= 0)
        far = jnp.full((LANES, QT), 2 * MAX_DIST, I32)
        for jb in range(2):
            sel_ref[0, 3, pl.ds(jb * LANES, LANES), cols] = lookup(far, far >= 0)


def _cmp_map_shift(nqt, qt):
    return (nqt - 1 - qt) * (QT // CMP_STRIDE)


def _cmp_map_rows(nqt, n_chunk):
    return -(-(n_chunk + _cmp_map_shift(nqt, 0)) // LANES) * LANES


def _bias_prompt(rb_t, nqt, n_chunk):
    ext = _cmp_map_rows(nqt, n_chunk)
    return pl.pallas_call(
        functools.partial(_bias_prompt_kernel, nqt=nqt, n_chunk=n_chunk),
        grid=(N_KV,),
        in_specs=[pl.BlockSpec(memory_space=pltpu.SMEM)],
        out_specs=[
            pl.BlockSpec((1, ext, GROUP_ROWS), lambda g: (g, 0, 0)),
            pl.BlockSpec((1, WINDOW + QT, GROUP_ROWS), lambda g: (g, 0, 0)),
            pl.BlockSpec((1, 4, 2 * LANES, GROUP_ROWS), lambda g: (g, 0, 0, 0)),
        ],
        out_shape=[
            jax.ShapeDtypeStruct((N_KV, ext, GROUP_ROWS), F32),
            jax.ShapeDtypeStruct((N_KV, WINDOW + QT, GROUP_ROWS), F32),
            jax.ShapeDtypeStruct((N_KV, 4, 2 * LANES, GROUP_ROWS), F32),
        ],
        compiler_params=_params(("parallel",)),
        name="bias_maps_prompt",
    )(rb_t)


def _bias_sample_kernel(rb_ref, cmp_ref, win_ref, sel_ref, *, past, n_chunk, n_pages):
    col = lax.broadcasted_iota(I32, (N_HEADS, LANES), 1)
    rb = rb_ref[...]

    def lookup(dist):
        n = jnp.maximum(dist, 0)
        max_exact = N_BUCKETS // 2
        nf = jnp.maximum(n, 1).astype(F32)
        large = max_exact + (jnp.log(nf / max_exact) / math.log(MAX_DIST / max_exact)
                             * (N_BUCKETS - max_exact)).astype(I32)
        bucket = jnp.where(n < max_exact, n, jnp.minimum(large, N_BUCKETS - 1))
        val = jnp.zeros(dist.shape, F32)
        for k in range(N_BUCKETS):
            val = jnp.where(bucket == k, rb[:, k:k + 1], val)
        return jnp.where(dist >= 0, val, NEG)

    def cmp_block(cb, carry):
        dist = past - (col + cb * LANES) * CMP_STRIDE - (CMP_LEN - 1)
        cmp_ref[cb] = lookup(dist)
        return carry

    lax.fori_loop(0, n_chunk // LANES, cmp_block, 0)

    def win_block(jb, carry):
        win_ref[jb] = lookup(WINDOW - (col + jb * LANES))
        return carry

    lax.fori_loop(0, WINDOW // LANES + 1, win_block, 0)

    def sel_page(p, carry):
        sel_ref[p] = lookup(past - (p * PAGE + col))
        return carry

    lax.fori_loop(0, n_pages + 1, sel_page, 0)


def _bias_sample(rb_t, past, n_chunk, n_pages):
    cmp_b, win_b, sel_b = pl.pallas_call(
        functools.partial(_bias_sample_kernel, past=past, n_chunk=n_chunk, n_pages=n_pages),
        out_shape=[
            jax.ShapeDtypeStruct((n_chunk // LANES, N_HEADS, LANES), F32),
            jax.ShapeDtypeStruct((WINDOW // LANES + 1, N_HEADS, LANES), F32),
            jax.ShapeDtypeStruct((n_pages + 1, N_HEADS, LANES), F32),
        ],
        name="bias_maps_sample",
    )(rb_t)
    flat = lambda a: a.transpose(1, 0, 2).reshape(N_HEADS, -1)
    return flat(cmp_b), flat(win_b), sel_b


_CHUNKS_PER_PAGE = PAGE // CMP_STRIDE
_CHUNK_PITCH = 24


def _compress_kernel(pt_ref, *refs, pp):
    page_refs = refs[:pp + 1]
    w0k_ref, w1k_ref, w0v_ref, w1v_ref, pe_ref, w2k_ref, w2v_ref, o_ref, tr_ref = refs[pp + 1:]
    del pt_ref
    nch = pp * _CHUNKS_PER_PAGE
    m = nch + 8
    n_slab = D_KV // LANES

    def transpose(s):
        for k in range(pp + 1):
            t = page_refs[k][pl.ds(s * LANES, LANES), :].T
            for n in range(_CHUNKS_PER_PAGE):
                row = (k * _CHUNKS_PER_PAGE + n) * _CHUNK_PITCH
                tr_ref[s, pl.ds(row, CMP_STRIDE), :] = t[n * CMP_STRIDE:(n + 1) * CMP_STRIDE]

    def chunks(s, _):
        return jnp.concatenate(
            [tr_ref.at[s][pl.ds(p, m, stride=_CHUNK_PITCH), :] for p in range(CMP_STRIDE)], axis=1)

    def mlp(s, x):
        is_v = s >= n_slab // 2
        w0 = (w0v_ref if is_v else w0k_ref)[...]
        w1 = (w1v_ref if is_v else w1k_ref)[...]
        w2 = (w2v_ref if is_v else w2k_ref)[...]
        pe0 = pe_ref[pl.ds(2 * int(is_v), 1), :]
        pe1 = pe_ref[pl.ds(2 * int(is_v) + 1, 1), :]
        y0 = _mm((x + pe0).astype(BF16), w0)
        y1 = _mm((x + pe1).astype(BF16), w1)
        pre = y0[0:nch] + y1[1:nch + 1]
        o_ref[0, :, pl.ds(s * LANES, LANES)] = _mm(_gelu_tanh(pre).astype(BF16), w2)

    _staggered(n_slab, transpose, chunks, mlp)


def _compress(pages, page_ids, weights, nb, n_pages, pp, paged):
    w0k, w1k, w0v, w1v, pe, w2k, w2v = weights
    steps = n_pages // pp
    nch = pp * _CHUNKS_PER_PAGE

    def page_spec(k):
        if paged:
            return pl.BlockSpec(
                (None, D_KV, PAGE),
                lambda b, s, pt: (pt[b * n_pages + jnp.minimum(s * pp + k, n_pages - 1)], 0, 0))
        return pl.BlockSpec((None, D_KV, PAGE), lambda b, s, pt: (b, 0, jnp.minimum(s * pp + k, n_pages - 1)))

    full = lambda a: pl.BlockSpec(a.shape, lambda b, s, pt: (0,) * a.ndim)
    grid_spec = pltpu.PrefetchScalarGridSpec(
        num_scalar_prefetch=1,
        grid=(nb, steps),
        in_specs=[page_spec(k) for k in range(pp + 1)] + [full(a) for a in weights],
        out_specs=pl.BlockSpec((1, nch, D_KV), lambda b, s, pt: (b, s, 0)),
        scratch_shapes=[pltpu.VMEM((D_KV // LANES, (pp + 1) * _CHUNKS_PER_PAGE * _CHUNK_PITCH, LANES), F32)],
    )
    return pl.pallas_call(
        functools.partial(_compress_kernel, pp=pp),
        grid_spec=grid_spec,
        out_shape=jax.ShapeDtypeStruct((nb, n_pages * _CHUNKS_PER_PAGE, D_KV), F32),
        compiler_params=_params(("parallel", "parallel"), 56),
        name="compress_paged" if paged else "compress_prompt",
    )(page_ids, *([pages] * (pp + 1)), w0k, w1k, w0v, w1v, pe, w2k, w2v)


def _compress_weights(pe_k, w1_k, w2_k, pe_v, w1_v, w2_v):
    eye = jnp.eye(2, dtype=F32)

    def first(w1, j):
        w = w1[j * CMP_STRIDE:(j + 1) * CMP_STRIDE]
        return jnp.einsum("pdf,gh->pgdhf", w, eye).reshape(CMP_STRIDE * LANES, 2 * CMP_HID).astype(BF16)

    def pos(pe, j):
        return jnp.tile(pe[j * CMP_STRIDE:(j + 1) * CMP_STRIDE, None, :], (1, 2, 1)).reshape(1, CMP_STRIDE * LANES)

    def second(w2):
        return jnp.einsum("fd,gh->gfhd", w2, eye).reshape(2 * CMP_HID, 2 * HEAD_DIM).astype(BF16)

    pe = jnp.concatenate([pos(pe_k, 0), pos(pe_k, 1), pos(pe_v, 0), pos(pe_v, 1)], axis=0)
    return (first(w1_k, 0), first(w1_k, 1), first(w1_v, 0), first(w1_v, 1), pe, second(w2_k), second(w2_v))


def _staggered(n, scores, softmax, values, ahead=1, lag=0):
    s = {g: scores(g) for g in range(min(ahead, n))}
    p = {}
    out = []
    for g in range(n + lag):
        if g + ahead < n:
            s[g + ahead] = scores(g + ahead)
        if g < n:
            p[g] = softmax(g, s.pop(g))
        if g - lag >= 0:
            out.append(values(g - lag, p.pop(g - lag)))
    return out


def _group_queries(qt_ref, g):
    return jnp.concatenate(
        [qt_ref[pl.ds((g * HPG + hh) * HEAD_DIM, HEAD_DIM), :] for hh in range(HPG)], axis=1)


def _store_heads(o_ref, o_t, gate, g, branch):
    lane = lax.broadcasted_iota(I32, (QT, 2 * HEAD_DIM), 1)
    for pair in range(HPG // 2):
        h = g * HPG + 2 * pair
        col = branch * N_HEADS + h
        both = jnp.concatenate([o_t[:, (2 * pair) * QT:(2 * pair + 1) * QT],
                                o_t[:, (2 * pair + 1) * QT:(2 * pair + 2) * QT]], axis=0)
        gates = jnp.where(lane < HEAD_DIM, gate[:, col:col + 1], gate[:, col + 1:col + 2])
        o_ref[:, pl.ds(h * HEAD_DIM, 2 * HEAD_DIM)] = both.T * gates


def _cmp_prompt_kernel(qt_ref, kc_ref, map_ref, gate_ref, ov_ref, o_ref, sb_ref, *, n_sel, top):
    qt = pl.program_id(1)
    kv = kc_ref[0]
    gate = gate_ref[...]
    ov = ov_ref[...]
    half = D_KV // 2
    n_chunk = kv.shape[0]
    map_rows = pl.ds(pl.multiple_of(_cmp_map_shift(pl.num_programs(1), qt), 8), n_chunk)
    bias = lambda g: map_ref[g, map_rows, :]
    vt_all = kv[:, half:].T
    jrow = lax.broadcasted_iota(I32, (n_sel, QT), 0)
    tpos = qt * QT + lax.broadcasted_iota(I32, (n_sel, QT), 1)
    cur = tpos // SEL_LEN
    forced = (jrow == 0) | (jrow == cur) | (jrow == cur - 1)
    valid = jrow * SEL_LEN <= tpos
    def scores(g):
        kg = kv[:, g * HEAD_DIM:(g + 1) * HEAD_DIM].astype(BF16)
        return _mm(kg, _group_queries(qt_ref, g)) + bias(g)

    def softmax(g, st):
        e = jnp.exp2(st - jnp.max(st, axis=0, keepdims=True))
        p = e / jnp.sum(e, axis=0, keepdims=True) * (bias(g) > 0.5 * NEG).astype(F32)
        return p.astype(BF16)

    def values(g, pb):
        vgt = vt_all[g * HEAD_DIM:(g + 1) * HEAD_DIM].astype(BF16)
        _store_heads(o_ref, _mm(vgt, pb), gate, g, 0)
        return pb

    probs = _staggered(N_KV, scores, softmax, values, ahead=2, lag=1)
    for g in range(N_KV):
        pb = probs[g]
        imp = jnp.zeros((n_sel, QT), F32)
        for hh in range(HPG):
            imp = imp + _mm(ov, pb[:, hh * QT:(hh + 1) * QT])
        score = jnp.where(forced, jnp.inf, jnp.where(valid, imp, -jnp.inf))
        rank = jnp.zeros((n_sel, QT), I32)
        for i in range(n_sel):
            si = score[i:i + 1, :]
            ahead = (si > score) | ((si == score) & (jrow > i))
            rank = rank + ahead.astype(I32)
        sb_ref[pl.ds(g * n_sel, n_sel), :] = jnp.where(rank < top, 0.0, NEG).astype(sb_ref.dtype)


def _cmp_prompt(q_t, kcmp, cmap, gates, ov_t, batch, seq):
    nqt = seq // QT
    n_chunk = kcmp.shape[1]
    n_sel = ov_t.shape[0]
    top = min(SEL_TOP, n_sel)
    return pl.pallas_call(
        functools.partial(_cmp_prompt_kernel, n_sel=n_sel, top=top),
        grid=(batch, nqt),
        in_specs=[
            pl.BlockSpec((D_Q, QT), lambda b, i: (0, b * nqt + i)),
            pl.BlockSpec((1, n_chunk, D_KV), lambda b, i: (b, 0, 0)),
            pl.BlockSpec(cmap.shape, lambda b, i: (0, 0, 0)),
            pl.BlockSpec((QT, LANES), lambda b, i: (b * nqt + i, 0)),
            pl.BlockSpec(ov_t.shape, lambda b, i: (0, 0)),
        ],
        out_specs=[
            pl.BlockSpec((QT, D_Q), lambda b, i: (b * nqt + i, 0)),
            pl.BlockSpec((None, N_KV * n_sel, QT), lambda b, i: (b * nqt + i, 0, 0)),
        ],
        out_shape=[
            jax.ShapeDtypeStruct((batch * seq, D_Q), F32),
            jax.ShapeDtypeStruct((batch * nqt, N_KV * n_sel, QT), BF16),
        ],
        compiler_params=_params(("parallel", "parallel")),
        name="cmp_attn_prompt",
    )(q_t, kcmp, cmap, gates, ov_t)


_TK = 2 * LANES


def _sel_prompt_kernel(qt_ref, kv_ref, sb_ref, map_ref, gate_ref, o_ref, ka_ref, vt_ref, qa_ref, *, n_sel, seq):
    qt = pl.program_id(1)
    aug = 2 * HEAD_DIM
    half = D_KV // 2

    @pl.when(qt == 0)
    def _():
        pos = lax.broadcasted_iota(I32, (seq, aug - HEAD_DIM), 0)
        blk = lax.broadcasted_iota(I32, (seq, aug - HEAD_DIM), 1)
        onehot = jnp.where(pos // SEL_LEN == blk, 1.0, 0.0)
        k_rows = kv_ref[0, pl.ds(0, half), :].T
        for g in range(N_KV):
            ka_ref[g] = jnp.concatenate([k_rows[:, g * HEAD_DIM:(g + 1) * HEAD_DIM], onehot], axis=1).astype(BF16)
            vt_ref[g] = kv_ref[0, pl.ds(half + g * HEAD_DIM, HEAD_DIM), :].astype(BF16)

    gate = gate_ref[...]
    sb = sb_ref[...]
    diag = qt // 2
    n_tiles = diag + 1
    pad = jnp.zeros((aug - HEAD_DIM - n_sel, GROUP_ROWS), BF16)
    for g in range(N_KV):
        sbg = sb[g * n_sel:(g + 1) * n_sel]
        qa_ref[g] = jnp.concatenate([_group_queries(qt_ref, g), jnp.concatenate([sbg] * HPG, axis=1), pad], axis=0)

    def tile(kt, carry):
        start = pl.multiple_of(kt * _TK, _TK)
        kind = jnp.where(kt == diag, qt % 2, jnp.where((kt == diag - 1) & (qt % 2 == 0), 2, 3))
        def scores(g):
            return _mm(ka_ref[g, pl.ds(start, _TK), :], qa_ref[g]) + map_ref[g, kind]

        def softmax(g, st):
            m_i, l_i, _ = carry[g]
            m_new = jnp.maximum(m_i, jnp.max(st, axis=0, keepdims=True))
            alpha = jnp.exp2(m_i - m_new)
            p = jnp.exp2(st - m_new)
            return m_new, alpha * l_i + jnp.sum(p, axis=0, keepdims=True), alpha, p.astype(BF16)

        def values(g, sm):
            m_new, l_new, alpha, pb = sm
            return m_new, l_new, alpha * carry[g][2] + _mm(vt_ref[g, :, pl.ds(start, _TK)], pb)

        return tuple(_staggered(N_KV, scores, softmax, values, ahead=2, lag=1))

    init = tuple((jnp.full((1, GROUP_ROWS), -jnp.inf, F32), jnp.zeros((1, GROUP_ROWS), F32),
                  jnp.zeros((HEAD_DIM, GROUP_ROWS), F32)) for _ in range(N_KV))
    final = lax.fori_loop(0, n_tiles, tile, init)
    for g in range(N_KV):
        _, l_f, acc = final[g]
        _store_heads(o_ref, acc / l_f, gate, g, 1)


def _sel_prompt(q_t, kv_t, selbias, smap, gates, batch, seq):
    nqt = seq // QT
    n_sel = selbias.shape[1] // N_KV
    return pl.pallas_call(
        functools.partial(_sel_prompt_kernel, n_sel=n_sel, seq=seq),
        grid=(batch, nqt),
        in_specs=[
            pl.BlockSpec((D_Q, QT), lambda b, i: (0, b * nqt + i)),
            pl.BlockSpec((1, D_KV, seq), lambda b, i: (b, 0, 0)),
            pl.BlockSpec((None, N_KV * n_sel, QT), lambda b, i: (b * nqt + i, 0, 0)),
            pl.BlockSpec(smap.shape, lambda b, i: (0, 0, 0, 0)),
            pl.BlockSpec((QT, LANES), lambda b, i: (b * nqt + i, 0)),
        ],
        out_specs=pl.BlockSpec((QT, D_Q), lambda b, i: (b * nqt + i, 0)),
        out_shape=jax.ShapeDtypeStruct((batch * seq, D_Q), F32),
        scratch_shapes=[pltpu.VMEM((N_KV, seq, 2 * HEAD_DIM), BF16), pltpu.VMEM((N_KV, HEAD_DIM, seq), BF16),
                        pltpu.VMEM((N_KV, 2 * HEAD_DIM, GROUP_ROWS), BF16)],
        compiler_params=_params(("arbitrary", "arbitrary"), 56),
        name="sel_attn_prompt",
    )(q_t, kv_t, selbias, smap, gates)


_WIN_TILES = WINDOW // QT + 1


def _win_prompt_kernel(qt_ref, *refs):
    kv_refs = refs[:_WIN_TILES]
    map_ref, gate_ref, o_ref = refs[_WIN_TILES:]
    qt = pl.program_id(1)
    gate = gate_ref[...]
    half = D_KV // 2
    span = _WIN_TILES * QT
    key = lax.broadcasted_iota(I32, (span, GROUP_ROWS), 0)
    in_seq = key >= (_WIN_TILES - 1 - qt) * QT
    k_rows = jnp.concatenate([r[0, pl.ds(0, half), :].T for r in kv_refs], axis=0)
    def scores(g):
        kg = k_rows[:, g * HEAD_DIM:(g + 1) * HEAD_DIM].astype(BF16)
        return jnp.where(in_seq, _mm(kg, _group_queries(qt_ref, g)) + map_ref[g], NEG)

    def softmax(g, st):
        e = jnp.exp2(st - jnp.max(st, axis=0, keepdims=True))
        return (e / jnp.sum(e, axis=0, keepdims=True)).astype(BF16)

    def values(g, pb):
        vgt = jnp.concatenate([r[0, pl.ds(half + g * HEAD_DIM, HEAD_DIM), :] for r in kv_refs],
                              axis=1).astype(BF16)
        _store_heads(o_ref, _mm(vgt, pb), gate, g, 2)

    _staggered(N_KV, scores, softmax, values, ahead=2, lag=1)


def _win_prompt(q_t, kv_t, wmap, gates, batch, seq):
    nqt = seq // QT

    def kv_spec(k):
        return pl.BlockSpec((1, D_KV, QT), lambda b, i: (b, 0, jnp.maximum(i - (_WIN_TILES - 1) + k, 0)))

    return pl.pallas_call(
        _win_prompt_kernel,
        grid=(batch, nqt),
        in_specs=[pl.BlockSpec((D_Q, QT), lambda b, i: (0, b * nqt + i))]
        + [kv_spec(k) for k in range(_WIN_TILES)]
        + [pl.BlockSpec(wmap.shape, lambda b, i: (0, 0, 0)),
           pl.BlockSpec((QT, LANES), lambda b, i: (b * nqt + i, 0))],
        out_specs=pl.BlockSpec((QT, D_Q), lambda b, i: (b * nqt + i, 0)),
        out_shape=jax.ShapeDtypeStruct((batch * seq, D_Q), F32),
        compiler_params=_params(("parallel", "parallel")),
        name="win_attn_prompt",
    )(q_t, *([kv_t] * _WIN_TILES), wmap, gates)


def _group_rows(x_by_group):
    row = lax.broadcasted_iota(I32, x_by_group[0].shape, 0)
    out = x_by_group[0]
    for g in range(1, N_KV):
        out = jnp.where(row // HPG == g, x_by_group[g], out)
    return out


_CMP_SAMPLE_ROWS = 4


def _cmp_sample_kernel(q_ref, kc_ref, bias_ref, ov_ref, gate_ref, o_ref, idx_ref, *, n_sel, past):
    bias = bias_ref[...]
    half = D_KV // 2
    imps = []
    for bl in range(_CMP_SAMPLE_ROWS):
        q = q_ref[bl]
        kv = kc_ref[bl].astype(BF16)
        s = _group_rows([_mm_nt(q, kv[:, g * HEAD_DIM:(g + 1) * HEAD_DIM]) for g in range(N_KV)]) + bias
        e = jnp.exp(s - jnp.max(s, axis=-1, keepdims=True))
        p = e / jnp.sum(e, axis=-1, keepdims=True) * (bias > 0.5 * NEG).astype(F32)
        pb = p.astype(BF16)
        o = _group_rows([_mm(pb, kv[:, half + g * HEAD_DIM:half + (g + 1) * HEAD_DIM]) for g in range(N_KV)])
        o_ref[bl] = o * gate_ref[bl]
        imp_h = _mm(pb, ov_ref[...])
        imps += [imp_h[g * HPG:g * HPG + 1] + imp_h[g * HPG + 1:g * HPG + 2] + imp_h[g * HPG + 2:g * HPG + 3]
                 + imp_h[g * HPG + 3:g * HPG + 4] for g in range(N_KV)]
    imp = jnp.concatenate(imps, axis=0)
    rows, nsp = imp.shape
    j = lax.broadcasted_iota(I32, (rows, nsp), 1)
    cur = past // SEL_LEN
    forced = (j == 0) | (j == cur) | (j == cur - 1)
    valid = (j * SEL_LEN <= past) & (j < n_sel)
    score = jnp.where(valid, jnp.where(forced, jnp.inf, imp), -jnp.inf)
    slot = lax.broadcasted_iota(I32, (rows, LANES), 1)
    res = jnp.zeros((rows, LANES), I32)
    for k in range(min(SEL_TOP, n_sel)):
        best = jnp.max(score, axis=-1, keepdims=True)
        pick = jnp.min(jnp.where(score == best, j, nsp), axis=-1, keepdims=True)
        res = jnp.where(slot == k, pick, res)
        score = jnp.where(j == pick, -jnp.inf, score)
    idx_ref[...] = res


def _cmp_sample(q3, kcmp, bias, ov, gate3, n_sel, past):
    bs, n_chunk, _ = kcmp.shape
    nb = _CMP_SAMPLE_ROWS
    heads = pl.BlockSpec((nb, N_HEADS, HEAD_DIM), lambda b: (b, 0, 0))
    return pl.pallas_call(
        functools.partial(_cmp_sample_kernel, n_sel=n_sel, past=past),
        grid=(bs // nb,),
        in_specs=[
            heads,
            pl.BlockSpec((nb, n_chunk, D_KV), lambda b: (b, 0, 0)),
            pl.BlockSpec(bias.shape, lambda b: (0, 0)),
            pl.BlockSpec(ov.shape, lambda b: (0, 0)),
            heads,
        ],
        out_specs=[heads, pl.BlockSpec((nb * N_KV, LANES), lambda b: (b, 0))],
        out_shape=[
            jax.ShapeDtypeStruct((bs, N_HEADS, HEAD_DIM), F32),
            jax.ShapeDtypeStruct((bs * N_KV, LANES), I32),
        ],
        compiler_params=_params(("parallel",)),
        name="cmp_attn_sample",
    )(q3, kcmp, bias, ov, gate3)


def _sel_sample_kernel(blk_ref, phys_ref, q_ref, new_ref, bias_ref, gate_ref, *refs, top, n_pages):
    kv_refs = refs[:top]
    k_refs = [r.at[0] for r in kv_refs]
    v_refs = [r.at[1] for r in kv_refs]
    o_ref = refs[top]
    del phys_ref
    b = pl.program_id(0)
    g = pl.program_id(1)
    q = q_ref[0, 0]
    lane = lax.broadcasted_iota(I32, (8, PAGE), 1)
    scores = []
    for k in range(top):
        blk = blk_ref[(b * N_KV + g) * top + k]
        page = jnp.minimum(blk // 2, n_pages)
        s = _mm(q, k_refs[k][...].astype(BF16)) + bias_ref[page]
        ok = (lane // SEL_LEN == blk % 2) & (blk < 2 * n_pages)
        scores.append(jnp.where(ok, s, NEG))
    k_new = new_ref[0, 0, 0:1, :].astype(BF16).astype(F32)
    v_new = new_ref[0, 0, 1:2, :].astype(BF16).astype(F32)
    s_new = jnp.sum(q.astype(F32) * k_new, axis=-1, keepdims=True) + bias_ref[n_pages]
    scores.append(s_new)
    s_all = jnp.concatenate(scores, axis=1)
    e = jnp.exp(s_all - jnp.max(s_all, axis=-1, keepdims=True))
    p = e / jnp.sum(e, axis=-1, keepdims=True)
    o = p[:, top * PAGE:top * PAGE + 1].astype(BF16).astype(F32) * v_new
    for k in range(top):
        o = o + _mm_nt(p[:, k * PAGE:(k + 1) * PAGE].astype(BF16), v_refs[k][...].astype(BF16))
    o_ref[0, 0] = o * gate_ref[0, 0]


def _sel_sample(blk, phys, q4, new4, bias4, gate4, pool5, top, n_pages):
    bs = q4.shape[0]

    def kv_spec(k):
        return pl.BlockSpec(
            (None, 2, None, HEAD_DIM, PAGE),
            lambda b, g, blk_r, phys_r: (phys_r[(b * N_KV + g) * top + k], 0, g, 0, 0))

    grp = lambda b, g, blk_r, phys_r: (b, g, 0, 0)
    grid_spec = pltpu.PrefetchScalarGridSpec(
        num_scalar_prefetch=2,
        grid=(bs, N_KV),
        in_specs=[
            pl.BlockSpec((1, 1, 8, HEAD_DIM), grp),
            pl.BlockSpec((1, 1, 8, HEAD_DIM), grp),
            pl.BlockSpec((None, n_pages + 1, 8, LANES), lambda b, g, blk_r, phys_r: (g, 0, 0, 0)),
            pl.BlockSpec((1, 1, 8, HEAD_DIM), grp),
        ] + [kv_spec(k) for k in range(top)],
        out_specs=pl.BlockSpec((1, 1, 8, HEAD_DIM), grp),
    )
    return pl.pallas_call(
        functools.partial(_sel_sample_kernel, top=top, n_pages=n_pages),
        grid_spec=grid_spec,
        out_shape=jax.ShapeDtypeStruct((bs, N_KV, 8, HEAD_DIM), F32),
        compiler_params=_params(("parallel", "parallel")),
        name="sel_attn_sample",
    )(blk, phys, q4, new4, bias4, gate4, *([pool5] * top))


def _win_sample_kernel(q_ref, new_ref, st_ref, col_ref, bias_ref, gate_ref, o_ref, nst_ref):
    st = st_ref[0]
    half = D_KV // 2
    for g in range(N_KV):
        q = q_ref[0, g]
        bias = bias_ref[g]
        s_old = _mm(q, st[g * HEAD_DIM:(g + 1) * HEAD_DIM, :].astype(BF16)) + bias[:, 0:WINDOW]
        k_new = new_ref[0, g, 0:1, :].astype(BF16).astype(F32)
        v_new = new_ref[0, g, 1:2, :].astype(BF16).astype(F32)
        s_new = jnp.sum(q.astype(F32) * k_new, axis=-1, keepdims=True) + bias[:, WINDOW:WINDOW + LANES]
        s_all = jnp.concatenate([s_old, s_new], axis=1)
        e = jnp.exp(s_all - jnp.max(s_all, axis=-1, keepdims=True))
        p = e / jnp.sum(e, axis=-1, keepdims=True)
        o = _mm_nt(p[:, 0:WINDOW].astype(BF16), st[half + g * HEAD_DIM:half + (g + 1) * HEAD_DIM, :].astype(BF16))
        o = o + p[:, WINDOW:WINDOW + 1].astype(BF16).astype(F32) * v_new
        o_ref[0, g] = o * gate_ref[0, g]
    rolled = pltpu.roll(st, WINDOW - 1, axis=1)
    lane = lax.broadcasted_iota(I32, (D_KV, LANES), 1)
    nst_ref[0, :, pl.ds(0, WINDOW - LANES)] = rolled[:, 0:WINDOW - LANES]
    nst_ref[0, :, pl.ds(WINDOW - LANES, LANES)] = jnp.where(
        lane == LANES - 1, col_ref[0], rolled[:, WINDOW - LANES:WINDOW])


def _win_sample(q4, new4, state_t, newcol, bias4, gate4):
    bs = q4.shape[0]
    grp = pl.BlockSpec((1, N_KV, 8, HEAD_DIM), lambda b: (b, 0, 0, 0))
    return pl.pallas_call(
        _win_sample_kernel,
        grid=(bs,),
        in_specs=[
            grp, grp,
            pl.BlockSpec((1, D_KV, WINDOW), lambda b: (b, 0, 0)),
            pl.BlockSpec((1, D_KV, LANES), lambda b: (b, 0, 0)),
            pl.BlockSpec(bias4.shape, lambda b: (0, 0, 0)),
            grp,
        ],
        out_specs=[grp, pl.BlockSpec((1, D_KV, WINDOW), lambda b: (b, 0, 0))],
        out_shape=[
            jax.ShapeDtypeStruct((bs, N_KV, 8, HEAD_DIM), F32),
            jax.ShapeDtypeStruct((bs, D_KV, WINDOW), F32),
        ],
        compiler_params=_params(("parallel",)),
        name="win_attn_sample",
    )(q4, new4, state_t, newcol, bias4, gate4)


_MOE_ROWS = 256
_MOE_SLACK = 2 * _MOE_ROWS


def _row_copy(src, src_row, dst, dst_row, sem):
    return pltpu.make_async_copy(src.at[pl.ds(src_row, 1)], dst.at[pl.ds(dst_row, 1)], sem)


def _expert_kernel(be_ref, nu_ref, dst_ref, src_ref, nxt_ref, h_ref, w_ref, wg_ref, wu_ref, wd_ref, yk_ref,
                   xbuf, ybuf, gsem, ssem, *, n_rows, n_live):
    del be_ref
    i = pl.program_id(0)
    slot = i % 2
    n_used = nu_ref[0]

    def gather(rows_ref, dst_slot):
        for r in range(_MOE_ROWS):
            _row_copy(h_ref, rows_ref[0, r], xbuf.at[dst_slot], r, gsem.at[dst_slot]).start()

    def wait_gather(s):
        for r in range(_MOE_ROWS):
            _row_copy(h_ref, 0, xbuf.at[s], 0, gsem.at[s]).wait()

    def scatter(s):
        for r in range(_MOE_ROWS):
            _row_copy(ybuf.at[s], r, yk_ref, dst_ref[0, r], ssem.at[s]).start()

    def wait_scatter(s):
        for r in range(_MOE_ROWS):
            _row_copy(ybuf.at[s], 0, yk_ref, 0, ssem.at[s]).wait()

    @pl.when(i == 0)
    def _():
        gather(src_ref, 0)
        ybuf[1] = jnp.zeros((_MOE_ROWS, ybuf.shape[2]), F32)
        fills = [
            pltpu.make_async_copy(ybuf.at[1, pl.ds(0, min(_MOE_ROWS, n_rows - off))],
                                  yk_ref.at[pl.ds(k * n_rows + off, min(_MOE_ROWS, n_rows - off))], ssem.at[1])
            for k in range(TOP_K) for off in range(n_live, n_rows, _MOE_ROWS)]
        for cp in fills:
            cp.start()
        for cp in fills:
            cp.wait()

    @pl.when(i + 1 < n_used)
    def _():
        gather(nxt_ref, 1 - slot)

    @pl.when(i < n_used)
    def _():
        wait_gather(slot)

        @pl.when(i >= 2)
        def _():
            wait_scatter(slot)

        x = xbuf[slot].astype(BF16)
        a = _silu(_mm(x, wg_ref[...])) * _mm(x, wu_ref[...])
        ybuf[slot] = _mm(a.astype(BF16), wd_ref[...]) * w_ref[...]
        scatter(slot)

        @pl.when(i == n_used - 1)
        def _():
            @pl.when(i >= 1)
            def _():
                wait_scatter(1 - slot)

            wait_scatter(slot)


def _experts(blk_e, n_used, rows_src, rows_dst, rows_w, h2_all, n_live, w_gate, w_up, w_down):
    n_rows, d = h2_all.shape
    n_blk = blk_e.shape[0]
    src3 = rows_src.reshape(n_blk, 1, _MOE_ROWS)
    dst3 = rows_dst.reshape(n_blk, 1, _MOE_ROWS)
    wspec = lambda a: pl.BlockSpec((None,) + a.shape[1:], lambda i, be, nu: (be[i], 0, 0))
    smem_blk = lambda fn: pl.BlockSpec((None, 1, _MOE_ROWS), fn, memory_space=pltpu.SMEM)
    grid_spec = pltpu.PrefetchScalarGridSpec(
        num_scalar_prefetch=2,
        grid=(n_blk,),
        in_specs=[
            smem_blk(lambda i, be, nu: (i, 0, 0)),
            smem_blk(lambda i, be, nu: (i, 0, 0)),
            smem_blk(lambda i, be, nu: (jnp.minimum(i + 1, n_blk - 1), 0, 0)),
            pl.BlockSpec(memory_space=pl.ANY),
            pl.BlockSpec((_MOE_ROWS, 1), lambda i, be, nu: (i, 0)),
            wspec(w_gate), wspec(w_up), wspec(w_down),
        ],
        out_specs=pl.BlockSpec(memory_space=pl.ANY),
        scratch_shapes=[
            pltpu.VMEM((2, _MOE_ROWS, d), F32),
            pltpu.VMEM((2, _MOE_ROWS, d), F32),
            pltpu.SemaphoreType.DMA((2,)),
            pltpu.SemaphoreType.DMA((2,)),
        ],
    )
    return pl.pallas_call(
        functools.partial(_expert_kernel, n_rows=n_rows, n_live=n_live),
        grid_spec=grid_spec,
        out_shape=jax.ShapeDtypeStruct((TOP_K * n_rows, d), F32),
        compiler_params=pltpu.CompilerParams(dimension_semantics=("arbitrary",), vmem_limit_bytes=56 * MIB,
                                             disable_bounds_checks=True),
        name="moe_experts",
    )(blk_e, n_used, dst3, src3, src3, h2_all, rows_w, w_gate, w_up, w_down)


def _combine_kernel(x_ref, y0_ref, y1_ref, g_ref, o_ref):
    x = x_ref[...] + (y0_ref[...] + y1_ref[...])
    ms = jnp.mean(x * x, axis=-1, keepdims=True)
    o_ref[...] = x * lax.rsqrt(ms + EPS) * g_ref[...]


def _combine(x1, yk, gfin, row0, tm):
    n, d = x1.shape
    yk3 = yk.reshape(TOP_K, -1, d)
    blk0 = row0 // tm
    return pl.pallas_call(
        _combine_kernel,
        grid=(n // tm,),
        in_specs=[
            pl.BlockSpec((tm, d), lambda i: (i, 0)),
            pl.BlockSpec((None, tm, d), lambda i: (0, blk0 + i, 0)),
            pl.BlockSpec((None, tm, d), lambda i: (1, blk0 + i, 0)),
            pl.BlockSpec((1, d), lambda i: (0, 0)),
        ],
        out_specs=pl.BlockSpec((tm, d), lambda i: (i, 0)),
        out_shape=jax.ShapeDtypeStruct((n, d), F32),
        compiler_params=_params(("parallel",)),
        name="moe_combine",
    )(x1, yk3, yk3, gfin)


def _dispatch(eid, w, counts, n_rows):
    a = eid.shape[0]
    tb = _MOE_ROWS
    order = jnp.argsort(eid).astype(I32)
    padded = (counts + tb - 1) // tb * tb
    seg_end = jnp.cumsum(padded)
    seg_start = seg_end - padded
    start = jnp.cumsum(counts) - counts
    n_blk = (a + N_EXPERTS * (tb - 1) + tb - 1) // tb
    blk_e = jnp.minimum(jnp.sum((seg_end[None, :] <= (jnp.arange(n_blk) * tb)[:, None]).astype(I32), axis=1),
                        N_EXPERTS - 1)
    blk = jnp.arange(n_blk)[:, None]
    r = jnp.arange(tb)[None, :]
    off = (blk * tb - seg_start[blk_e][:, None]) + r
    valid = off < counts[blk_e][:, None]
    asg = order[jnp.clip(start[blk_e][:, None] + off, 0, a - 1)]
    rows_src = jnp.where(valid, asg // TOP_K, 0).astype(I32)
    rows_dst = jnp.where(valid, (asg % TOP_K) * n_rows + asg // TOP_K, (blk % 2) * n_rows + (n_rows - tb) + r)
    rows_w = jnp.where(valid, w[asg], 0.0)
    n_used = (seg_end[-1] // tb).astype(I32).reshape(1)
    return rows_src, rows_dst.astype(I32), rows_w.reshape(-1, 1), blk_e.astype(I32), n_used


def _overlap(n_chunk, n_sel, n_sel_pad):
    cs = jnp.arange(n_chunk)[:, None] * CMP_STRIDE
    ss = jnp.arange(n_sel_pad)[None, :] * SEL_LEN
    hit = (cs < ss + SEL_LEN) & (cs + CMP_LEN > ss) & (jnp.arange(n_sel_pad)[None, :] < n_sel)
    return hit.astype(BF16)


def kernel(x_prompt, x_sample, cache_cmp_kv, cache_sel_kv, state_win_kv, state_conv, page_table, rel_bias,
           norm_mix, w_in, b_in, conv_w, conv_b, conv_ln_g, conv_ln_b, w_conv_out, b_conv_out,
           cmp_pe_k, cmp_w1_k, cmp_w2_k, cmp_pe_v, cmp_w1_v, cmp_w2_v, w_nsa_out, w_out, norm_ffn,
           w_rg, b_rg, w_re, b_re, w_gate, w_up, w_down, norm_final):
    batch, seq, _ = x_prompt.shape
    bs = x_sample.shape[0]
    n_pool = cache_cmp_kv.shape[1]
    n_pages = page_table.shape[1]
    past = n_pages * PAGE
    n_tok = batch * seq
    assert bs + _MOE_ROWS <= _MOE_SLACK and n_tok % bs == 0 and seq % _TK == 0 and bs % 16 == 0
    l = 0

    wt = w_in[l].T.astype(BF16)
    bias_in = b_in[l]
    cuts = [0, D_CONV, 2 * D_CONV, 2 * D_CONV + D_Q, 2 * D_CONV + D_Q + 3 * D_KV]
    cuts += [cuts[-1] + 3 * N_HEADS, cuts[-1] + 3 * N_HEADS + D_MODEL, cuts[-1] + 3 * N_HEADS + 2 * D_MODEL]
    seg = lambda k: (wt[cuts[k]:cuts[k + 1]], bias_in[cuts[k]:cuts[k + 1]][None, :])
    (wa, ba), (wb, bb), (wq, bq), (wkv, bkv), (wzg, bzg), (wga, bga), (wgb, bgb) = [seg(k) for k in range(7)]
    wzg = jnp.pad(wzg, ((0, LANES - 3 * N_HEADS), (0, 0)))
    bzg = jnp.pad(bzg, ((0, 0), (0, LANES - 3 * N_HEADS)))
    wc = w_conv_out[l].astype(BF16)
    wn = w_nsa_out[l].astype(BF16)
    wo = w_out[l].astype(BF16)
    wr = jnp.zeros((LANES, D_MODEL), F32).at[0:N_GROUPS].set(w_rg[l].T).at[8:8 + N_EXPERTS].set(w_re[l].T)
    br = jnp.zeros((LANES,), F32).at[0:N_GROUPS].set(b_rg[l]).at[8:8 + N_EXPERTS].set(b_re[l].reshape(-1))
    wr, br = wr.astype(BF16), br[:, None]
    cw = _compress_weights(cmp_pe_k[l], cmp_w1_k[l], cmp_w2_k[l], cmp_pe_v[l], cmp_w1_v[l], cmp_w2_v[l])
    rb_t = rel_bias.T
    gmix, gffn, gfin = norm_mix[l][None, :], norm_ffn[l][None, :], norm_final[None, :]
    vec = lambda a: a[l][None, :]

    q_scale = HEAD_DIM ** -0.5
    xp = x_prompt.reshape(n_tok, D_MODEL)
    xs = x_sample.reshape(bs, D_MODEL)
    hp, up, qp, gates_p, *kv_p = _project_prompt(
        xp, gmix, wa, wb, wq, wkv, wzg, ba, bb, bq.reshape(-1, 1), bkv.reshape(-1, 1), bzg,
        q_scale * LOG2E, batch, seq, 512)
    hs = _rmsnorm(xs, gmix, bs)
    us = _glu(hs, wa, wb, ba, bb, bs, 512)
    qs = _proj(hs, wq, bq, lambda z: z * q_scale, BF16, bs, 512, "proj_q")
    gates_s = _proj(hs, wzg, bzg, _sigmoid, F32, bs, LANES, "proj_head_gates")
    kv_s = _proj(hs, wkv, bkv, lambda z: z, F32, bs, 512, "proj_kv_sample")

    conv_args = (conv_w[l], vec(conv_b), vec(conv_ln_g), vec(conv_ln_b))
    cp = _conv_prompt(up, *conv_args, batch, seq, 256)
    cs, new_conv_s = _conv_sample(state_conv[l].transpose(1, 0, 2), us, *conv_args)

    nqt = seq // QT
    n_chunk_p = seq // CMP_STRIDE
    n_chunk_s = n_pages * _CHUNKS_PER_PAGE
    cmap, wmap, smap = _bias_prompt(rb_t, nqt, n_chunk_p)
    cbias_s, wbias_s, sbias_s = _bias_sample(rb_t, past, n_chunk_s, n_pages)

    ident = jnp.zeros((1,), I32)
    kcmp_p = _compress(kv_p[0], ident, cw, batch, seq // PAGE, seq // PAGE, False)
    feat_major = lambda a: a.transpose(0, 2, 3, 4, 1)
    pool_cmp = feat_major(cache_cmp_kv[l]).reshape(n_pool, D_KV, PAGE)
    pp_s = min(32, n_pages)
    kcmp_s = _compress(pool_cmp, page_table.reshape(-1), cw, bs, n_pages, pp_s, True)

    n_sel_p = seq // SEL_LEN
    o_cmp_p, selbias = _cmp_prompt(qp, kcmp_p, cmap, gates_p, _overlap(n_chunk_p, n_sel_p, n_sel_p).T,
                                   batch, seq)
    n_sel_s = -(-(past + 1) // SEL_LEN)
    n_sel_pad = -(-n_sel_s // LANES) * LANES
    top_s = min(SEL_TOP, n_sel_s)
    head_gate = lambda k: jnp.broadcast_to(
        gates_s[:, k * N_HEADS:(k + 1) * N_HEADS, None], (bs, N_HEADS, HEAD_DIM))
    qs3 = qs.reshape(bs, N_HEADS, HEAD_DIM)
    o_cmp_s, idx_s = _cmp_sample(qs3, kcmp_s, cbias_s, _overlap(n_chunk_s, n_sel_s, n_sel_pad), head_gate(0),
                                 n_sel_s, past)

    o_sel_p = _sel_prompt(qp, kv_p[1], selbias, smap, gates_p, batch, seq)
    by_group = lambda a: jnp.pad(a.reshape(bs, N_KV, HPG, -1), ((0, 0), (0, 0), (0, 8 - HPG), (0, 0)))
    q4 = by_group(qs3)
    blk = idx_s[:, :top_s].reshape(-1)
    phys = jnp.take_along_axis(
        page_table, jnp.minimum(blk // 2, n_pages - 1).reshape(bs, -1), axis=1).reshape(-1)
    kv_new = lambda k: jnp.pad(
        kv_s[:, k * D_KV:(k + 1) * D_KV].reshape(bs, 2, N_KV, HEAD_DIM).transpose(0, 2, 1, 3),
        ((0, 0), (0, 0), (0, 6), (0, 0)))
    bias_by_group = lambda a: jnp.pad(
        jnp.moveaxis(a, -2, 0).reshape((N_KV, HPG) + a.shape[:-2] + a.shape[-1:]),
        ((0, 0), (0, 8 - HPG)) + ((0, 0),) * (a.ndim - 1))
    sb4 = jnp.moveaxis(bias_by_group(sbias_s), 1, 2)
    pool_sel = feat_major(cache_sel_kv[l])
    o_sel_s = _sel_sample(blk, phys, q4, kv_new(1), sb4, by_group(head_gate(1)), pool_sel, top_s, n_pages)

    o_win_p = _win_prompt(qp, kv_p[2], wmap, gates_p, batch, seq)
    win_t = feat_major(state_win_kv[l]).reshape(bs, D_KV, WINDOW)
    newcol = jnp.broadcast_to(kv_s[:, 2 * D_KV:3 * D_KV, None], (bs, D_KV, LANES))
    o_win_s, new_win_s = _win_sample(q4, kv_new(2), win_t, newcol, bias_by_group(wbias_s), by_group(head_gate(2)))

    m_p = _merge_a(hp, cp, o_cmp_p, o_sel_p, o_win_p, wga, wgb, wc, wn, bga, bgb, vec(b_conv_out), 512, 512)
    x1_p, h2_all, eid_p, rw_p, cnt_p = _merge_b(xp, m_p, wo, gffn, wr, br, 256, slack=_MOE_SLACK)
    from_groups = lambda a: a[:, :, :HPG, :].reshape(bs, D_Q)
    m_s = _merge_a(hs, cs, o_cmp_s.reshape(bs, D_Q), from_groups(o_sel_s), from_groups(o_win_s),
                   wga, wgb, wc, wn, bga, bgb, vec(b_conv_out), bs, 512)
    x1_s, h2_all, eid_s, rw_s, cnt_s = _merge_b(xs, m_s, wo, gffn, wr, br, bs, h2_all=h2_all, row0=n_tok)

    eid = jnp.concatenate([eid_p[:TOP_K].T, eid_s[:TOP_K].T], axis=0).reshape(-1)
    rw = jnp.concatenate([rw_p[:TOP_K].T, rw_s[:TOP_K].T], axis=0).reshape(-1)
    counts = jnp.sum(cnt_p[:, :, 0], axis=0) + jnp.sum(cnt_s[:, :, 0], axis=0)
    rows_src, rows_dst, rows_w, blk_e, n_used = _dispatch(eid, rw, counts, h2_all.shape[0])
    yk = _experts(blk_e, n_used, rows_src, rows_dst, rows_w, h2_all, n_tok + bs, w_gate[l].astype(BF16),
                  w_up[l].astype(BF16), w_down[l].astype(BF16))
    y_p = _combine(x1_p, yk, gfin, 0, 256)
    y_s = _combine(x1_s, yk, gfin, n_tok, bs)

    def cache_rows(a_t, rows):
        return a_t.reshape(a_t.shape[0], 2, N_KV, HEAD_DIM, rows).transpose(0, 4, 1, 2, 3)[None]

    new_row = lambda k: kv_s[:, k * D_KV:(k + 1) * D_KV].reshape(1, bs, 1, 2, N_KV, HEAD_DIM)
    hist = CONV_W - 1
    return (
        y_p.reshape(batch, seq, D_MODEL),
        y_s.reshape(bs, 1, D_MODEL),
        cache_rows(kv_p[0], seq),
        new_row(0),
        cache_rows(kv_p[1], seq),
        new_row(1),
        cache_rows(kv_p[2][:, :, seq - WINDOW:], WINDOW),
        cache_rows(new_win_s, WINDOW),
        up.reshape(batch, seq, D_CONV)[None, :, seq - hist:],
        new_conv_s.transpose(1, 0, 2)[None],
    )
```

```python
import functools
import math

import jax
import jax.numpy as jnp
from jax import lax
from jax.experimental import pallas as pl
from jax.experimental.pallas import tpu as pltpu

F32 = jnp.float32
BF16 = jnp.bfloat16
I32 = jnp.int32

D_MODEL = 2048
D_CONV = 1024
CONV_W = 31
N_HEADS = 16
HEAD_DIM = 64
N_KV = 4
HPG = N_HEADS // N_KV
D_Q = N_HEADS * HEAD_DIM
D_KV = 2 * N_KV * HEAD_DIM
CMP_LEN = 32
CMP_STRIDE = 16
CMP_HID = 64
SEL_LEN = 64
SEL_TOP = 16
WINDOW = 512
N_BUCKETS = 32
MAX_DIST = 128
N_GROUPS = 4
EXP_PER_GROUP = 8
N_EXPERTS = N_GROUPS * EXP_PER_GROUP
TOP_K = 2
D_EXPERT = 1024
EPS = 1e-6
NEG = -1e30
PAGE = 128
LANES = 128
QT = 128
GROUP_ROWS = HPG * QT
MIB = 1024 * 1024
LOG2E = math.log2(math.e)


def _params(sem, vmem_mib=48):
    return pltpu.CompilerParams(dimension_semantics=sem, vmem_limit_bytes=vmem_mib * MIB)


def _mm_nt(a, b):
    return lax.dot_general(a, b, (((1,), (1,)), ((), ())), preferred_element_type=F32)


def _mm(a, b):
    return jnp.dot(a, b, preferred_element_type=F32)


def _sigmoid(z):
    return 1.0 / (1.0 + jnp.exp(-z))


def _silu(z):
    return z * _sigmoid(z)


def _gelu_tanh(z):
    return 0.5 * z * (1.0 + jnp.tanh(math.sqrt(2.0 / math.pi) * (z + 0.044715 * (z * z * z))))


def _rmsnorm_kernel(x_ref, g_ref, o_ref):
    x = x_ref[...]
    ms = jnp.mean(x * x, axis=-1, keepdims=True)
    o_ref[...] = (x * lax.rsqrt(ms + EPS) * g_ref[...]).astype(o_ref.dtype)


def _rmsnorm(x, g, tm):
    n, d = x.shape
    return pl.pallas_call(
        _rmsnorm_kernel,
        grid=(n // tm,),
        in_specs=[pl.BlockSpec((tm, d), lambda i: (i, 0)), pl.BlockSpec((1, d), lambda i: (0, 0))],
        out_specs=pl.BlockSpec((tm, d), lambda i: (i, 0)),
        out_shape=jax.ShapeDtypeStruct((n, d), BF16),
        compiler_params=_params(("parallel",)),
        name="rmsnorm",
    )(x, g)


def _proj_kernel(h_ref, w_ref, b_ref, o_ref, *, act):
    z = _mm_nt(h_ref[...], w_ref[...]) + b_ref[...]
    o_ref[...] = act(z).astype(o_ref.dtype)


def _proj(h, wt, b, act, out_dtype, tm, tn, name):
    n, k = h.shape
    nout = wt.shape[0]
    return pl.pallas_call(
        functools.partial(_proj_kernel, act=act),
        grid=(n // tm, nout // tn),
        in_specs=[
            pl.BlockSpec((tm, k), lambda i, j: (i, 0)),
            pl.BlockSpec((tn, k), lambda i, j: (j, 0)),
            pl.BlockSpec((1, tn), lambda i, j: (0, j)),
        ],
        out_specs=pl.BlockSpec((tm, tn), lambda i, j: (i, j)),
        out_shape=jax.ShapeDtypeStruct((n, nout), out_dtype),
        compiler_params=_params(("parallel", "parallel")),
        name=name,
    )(h, wt, b)


def _proj_t_kernel(h_ref, w_ref, b_ref, o_ref, *, act):
    z = _mm_nt(w_ref[...], h_ref[...]) + b_ref[...]
    o_ref[...] = act(z).astype(o_ref.dtype)


def _proj_t(h, wt, bcol, act, out_dtype, tm, tn, name):
    n, k = h.shape
    nout = wt.shape[0]
    return pl.pallas_call(
        functools.partial(_proj_t_kernel, act=act),
        grid=(n // tm, nout // tn),
        in_specs=[
            pl.BlockSpec((tm, k), lambda i, j: (i, 0)),
            pl.BlockSpec((tn, k), lambda i, j: (j, 0)),
            pl.BlockSpec((tn, 1), lambda i, j: (j, 0)),
        ],
        out_specs=pl.BlockSpec((tn, tm), lambda i, j: (j, i)),
        out_shape=jax.ShapeDtypeStruct((nout, n), out_dtype),
        compiler_params=_params(("parallel", "parallel")),
        name=name,
    )(h, wt, bcol)


def _glu_kernel(h_ref, wa_ref, wb_ref, ba_ref, bb_ref, o_ref):
    h = h_ref[...]
    a = _mm_nt(h, wa_ref[...]) + ba_ref[...]
    b = _mm_nt(h, wb_ref[...]) + bb_ref[...]
    o_ref[...] = a * _sigmoid(b)


def _glu(h, wat, wbt, ba, bb, tm, tn):
    n, k = h.shape
    nout = wat.shape[0]
    wspec = pl.BlockSpec((tn, k), lambda i, j: (j, 0))
    bspec = pl.BlockSpec((1, tn), lambda i, j: (0, j))
    return pl.pallas_call(
        _glu_kernel,
        grid=(n // tm, nout // tn),
        in_specs=[pl.BlockSpec((tm, k), lambda i, j: (i, 0)), wspec, wspec, bspec, bspec],
        out_specs=pl.BlockSpec((tm, tn), lambda i, j: (i, j)),
        out_shape=jax.ShapeDtypeStruct((n, nout), F32),
        compiler_params=_params(("parallel", "parallel")),
        name="glu",
    )(h, wat, wbt, ba, bb)


def _kvt_kernel(h_ref, w_ref, b_ref, *o_refs):
    h = h_ref[...]
    for kind, o_ref in enumerate(o_refs):
        rows = pl.ds(kind * D_KV, D_KV)
        o_ref[0] = _mm_nt(w_ref[rows, :], h) + b_ref[rows, :]


def _kv_transposed(h, wt, bcol, batch, seq, tm):
    n, k = h.shape
    kinds = wt.shape[0] // D_KV
    nt = seq // tm
    return pl.pallas_call(
        _kvt_kernel,
        grid=(n // tm,),
        in_specs=[
            pl.BlockSpec((tm, k), lambda i: (i, 0)),
            pl.BlockSpec(wt.shape, lambda i: (0, 0)),
            pl.BlockSpec(bcol.shape, lambda i: (0, 0)),
        ],
        out_specs=[pl.BlockSpec((1, D_KV, tm), lambda i: (i // nt, 0, i % nt))] * kinds,
        out_shape=[jax.ShapeDtypeStruct((batch, D_KV, seq), F32)] * kinds,
        compiler_params=_params(("parallel",)),
        name="kv_transposed",
    )(h, wt, bcol)


def _project_prompt_kernel(x_ref, g_ref, wa_ref, wb_ref, wq_ref, wkv_ref, wzg_ref, ba_ref, bb_ref, bq_ref,
                           bkv_ref, bzg_ref, h_ref, u_ref, qt_ref, gates_ref, *kv_refs, q_scale):
    x = x_ref[...]
    ms = jnp.mean(x * x, axis=-1, keepdims=True)
    h = (x * lax.rsqrt(ms + EPS) * g_ref[...]).astype(BF16)
    h_ref[...] = h
    a = _mm_nt(h, wa_ref[...]) + ba_ref[...]
    b = _mm_nt(h, wb_ref[...]) + bb_ref[...]
    u_ref[...] = a * _sigmoid(b)
    qt_ref[...] = ((_mm_nt(wq_ref[...], h) + bq_ref[...]) * q_scale).astype(qt_ref.dtype)
    gates_ref[...] = _sigmoid(_mm_nt(h, wzg_ref[...]) + bzg_ref[...])
    for kind, o_ref in enumerate(kv_refs):
        rows = pl.ds(kind * D_KV, D_KV)
        o_ref[0] = _mm_nt(wkv_ref[rows, :], h) + bkv_ref[rows, :]


def _project_prompt(x, g, wa, wb, wq, wkv, wzg, ba, bb, bq_col, bkv_col, bzg, q_scale, batch, seq, tm):
    n, d = x.shape
    kinds = wkv.shape[0] // D_KV
    nt = seq // tm
    tok = lambda width: pl.BlockSpec((tm, width), lambda i: (i, 0))
    const = lambda a: pl.BlockSpec(a.shape, lambda i: (0,) * a.ndim, pipeline_mode=pl.Buffered(1))
    consts = (g, wa, wb, wq, wkv, wzg, ba, bb, bq_col, bkv_col, bzg)
    return pl.pallas_call(
        functools.partial(_project_prompt_kernel, q_scale=q_scale),
        grid=(n // tm,),
        in_specs=[tok(d)] + [const(a) for a in consts],
        out_specs=[tok(d), tok(D_CONV), pl.BlockSpec((D_Q, tm), lambda i: (0, i)), tok(LANES)]
        + [pl.BlockSpec((1, D_KV, tm), lambda i: (i // nt, 0, i % nt))] * kinds,
        out_shape=[
            jax.ShapeDtypeStruct((n, d), BF16),
            jax.ShapeDtypeStruct((n, D_CONV), F32),
            jax.ShapeDtypeStruct((D_Q, n), BF16),
            jax.ShapeDtypeStruct((n, LANES), F32),
        ] + [jax.ShapeDtypeStruct((batch, D_KV, seq), F32)] * kinds,
        compiler_params=_params(("parallel",), 56),
        name="project_prompt",
    )(x, *consts)


def _merge_a_kernel(h_ref, c_ref, o1_ref, o2_ref, o3_ref, wga_ref, wgb_ref, wc_ref, wn_ref,
                    bga_ref, bgb_ref, bc_ref, *rest):
    n_cast = (len(rest) - 1) // 2
    m_ref = rest[n_cast]
    h = h_ref[...]
    ga = _sigmoid(_mm_nt(h, wga_ref[...]) + bga_ref[...])
    gb = _sigmoid(_mm_nt(h, wgb_ref[...]) + bgb_ref[...])
    yc = _mm(c_ref[...], wc_ref[...]) + bc_ref[...]
    o = (o1_ref[...] + o2_ref[...] + o3_ref[...]).astype(BF16)
    yn = _mm(o, wn_ref[...])
    m_ref[...] = (ga * yc + gb * yn).astype(m_ref.dtype)
    for src, dst in zip(rest[:n_cast], rest[n_cast + 1:]):
        dst[...] = src[...].astype(dst.dtype)


def _rider_specs(riders, steps, index):
    return [pl.BlockSpec((a.shape[0] // steps, a.shape[1]), index) for a in riders]


def _merge_a(h, c, o1, o2, o3, wgat, wgbt, wc, wn, bga, bgb, bc, tm, tn, riders=()):
    n, k = h.shape
    nj = D_MODEL // tn
    tok = lambda width: pl.BlockSpec((tm, width), lambda i, j: (i, 0))
    wt_spec = pl.BlockSpec((tn, k), lambda i, j: (j, 0))
    w_spec = pl.BlockSpec((c.shape[1], tn), lambda i, j: (0, j))
    b_spec = pl.BlockSpec((1, tn), lambda i, j: (0, j))
    ride = _rider_specs(riders, (n // tm) * nj, lambda i, j: (i * nj + j, 0))
    out = pl.pallas_call(
        _merge_a_kernel,
        grid=(n // tm, nj),
        in_specs=[tok(k), tok(D_CONV), tok(D_Q), tok(D_Q), tok(D_Q), wt_spec, wt_spec, w_spec, w_spec,
                  b_spec, b_spec, b_spec] + ride,
        out_specs=[pl.BlockSpec((tm, tn), lambda i, j: (i, j))] + ride,
        out_shape=[jax.ShapeDtypeStruct((n, D_MODEL), BF16)]
        + [jax.ShapeDtypeStruct(a.shape, BF16) for a in riders],
        compiler_params=_params(("parallel", "parallel"), 56),
        name="merge_gates",
    )(h, c, o1, o2, o3, wgat, wgbt, wc, wn, bga, bgb, bc, *riders)
    return out[0], out[1:]


def _merge_b_kernel(x_ref, m_ref, wo_ref, gn_ref, wr_ref, br_ref, *rest, n_real, n_shared, n_riders):
    rider_in = rest[n_shared:n_shared + n_riders]
    outs = rest[n_shared + n_riders:n_shared + n_riders + 5]
    rider_out = rest[n_shared + n_riders + 5:]
    h2_ref = outs[1]

    @pl.when(pl.program_id(0) < n_real)
    def _():
        _merge_b_tile(x_ref, m_ref, wo_ref, gn_ref, wr_ref, br_ref, *outs)
        for src, dst in zip(rider_in, rider_out):
            dst[...] = src[...].astype(dst.dtype)

    @pl.when(pl.program_id(0) >= n_real)
    def _():
        h2_ref[...] = jnp.zeros(h2_ref.shape, h2_ref.dtype)


def _merge_b_tile(x_ref, m_ref, wo_ref, gn_ref, wr_ref, br_ref, x1_ref, h2_ref, eid_ref, rw_ref, cnt_ref):
    x1 = x_ref[...] + _mm(m_ref[...], wo_ref[...])
    x1_ref[...] = x1
    ms = jnp.mean(x1 * x1, axis=-1, keepdims=True)
    h2 = x1 * lax.rsqrt(ms + EPS) * gn_ref[...]
    h2_ref[...] = h2
    logits = _mm_nt(wr_ref[...], h2.astype(BF16)) + br_ref[...]
    tm = logits.shape[1]
    row = lax.broadcasted_iota(I32, (8, tm), 0)
    lg = jnp.where(row < N_GROUPS, logits[0:8], -jnp.inf)
    gmax = jnp.max(lg, axis=0, keepdims=True)
    gidx = jnp.min(jnp.where(lg == gmax, row, 8), axis=0, keepdims=True)
    wg = 1.0 / jnp.sum(jnp.exp(lg - gmax), axis=0, keepdims=True)
    le = jnp.zeros((8, tm), F32)
    for g in range(N_GROUPS):
        le = jnp.where(gidx == g, logits[8 + 8 * g:16 + 8 * g], le)
    ee = jnp.exp(le - jnp.max(le, axis=0, keepdims=True))
    pz = ee / jnp.sum(ee, axis=0, keepdims=True)
    p1 = jnp.max(pz, axis=0, keepdims=True)
    i1 = jnp.min(jnp.where(pz == p1, row, 8), axis=0, keepdims=True)
    pz2 = jnp.where(row == i1, -1.0, pz)
    p2 = jnp.max(pz2, axis=0, keepdims=True)
    i2 = jnp.min(jnp.where(pz2 == p2, row, 8), axis=0, keepdims=True)
    den = p1 + p2
    e1 = gidx * EXP_PER_GROUP + i1
    e2 = gidx * EXP_PER_GROUP + i2
    eid_ref[...] = jnp.where(row == 0, e1, jnp.where(row == 1, e2, 0))
    rw_ref[...] = jnp.where(row == 0, wg * p1 / den, jnp.where(row == 1, wg * p2 / den, 0.0))
    erow = lax.broadcasted_iota(I32, (N_EXPERTS, tm), 0)
    hits = (erow == e1).astype(F32) + (erow == e2).astype(F32)
    cnt_ref[...] = jnp.broadcast_to(jnp.sum(hits, axis=1, keepdims=True), (N_EXPERTS, LANES)).astype(I32)


def _merge_b(x, m, wo, gn, wr, br, tm, h2_all=None, row0=0, slack=0, riders=()):
    n, d = x.shape
    blk0 = row0 // tm
    n_real = n // tm
    real = lambda i: jnp.minimum(i, n_real - 1)
    shared = [] if h2_all is None else [h2_all]
    h2_rows = n + slack if h2_all is None else h2_all.shape[0]
    ride = _rider_specs(riders, n_real, lambda i: (real(i), 0))
    out = pl.pallas_call(
        functools.partial(_merge_b_kernel, n_real=n_real, n_shared=len(shared), n_riders=len(riders)),
        grid=((n + slack) // tm,),
        in_specs=[
            pl.BlockSpec((tm, d), lambda i: (real(i), 0)),
            pl.BlockSpec((tm, d), lambda i: (real(i), 0)),
            pl.BlockSpec((d, d), lambda i: (0, 0)),
            pl.BlockSpec((1, d), lambda i: (0, 0)),
            pl.BlockSpec((LANES, d), lambda i: (0, 0)),
            pl.BlockSpec((LANES, 1), lambda i: (0, 0)),
        ] + [pl.BlockSpec(memory_space=pl.ANY)] * len(shared) + ride,
        out_specs=[
            pl.BlockSpec((tm, d), lambda i: (real(i), 0)),
            pl.BlockSpec((tm, d), lambda i: (blk0 + i, 0)),
            pl.BlockSpec((8, tm), lambda i: (0, real(i))),
            pl.BlockSpec((8, tm), lambda i: (0, real(i))),
            pl.BlockSpec((None, N_EXPERTS, LANES), lambda i: (real(i), 0, 0)),
        ] + ride,
        out_shape=[
            jax.ShapeDtypeStruct((n, d), F32),
            jax.ShapeDtypeStruct((h2_rows, d), F32),
            jax.ShapeDtypeStruct((8, n), I32),
            jax.ShapeDtypeStruct((8, n), F32),
            jax.ShapeDtypeStruct((n_real, N_EXPERTS, LANES), I32),
        ] + [jax.ShapeDtypeStruct(a.shape, BF16) for a in riders],
        input_output_aliases={6: 1} if shared else {},
        compiler_params=_params(("arbitrary",), 56),
        name="merge_out_router",
    )(x, m, wo, gn, wr, br, *shared, *riders)
    return out[:5], out[5:]


_HALO = 32


def _conv_prompt_kernel(u_ref, prev_ref, w_ref, b_ref, g_ref, beta_ref, o_ref, buf_ref, acc_ref, sh_ref, *, tt):
    i = pl.program_id(1)
    prev = prev_ref[pl.ds(tt - _HALO, _HALO), :]
    buf_ref[pl.ds(0, _HALO), :] = jnp.where(i > 0, prev, 0.0)
    buf_ref[pl.ds(_HALO, tt), :] = u_ref[...]
    off = _HALO - (CONV_W - 1)
    sub = 8
    rows = tt
    for cc in range(D_CONV // LANES):
        cols = pl.ds(cc * LANES, LANES)
        for r0 in range(0, tt, rows):
            acc = jnp.zeros((rows, LANES), F32)
            for r in range(sub):
                taps = [k for k in range(CONV_W) if (off + k) % sub == r]
                n_shift = (rows + _HALO - r) // sub * sub
                sh_ref[r, pl.ds(0, n_shift), :] = buf_ref[pl.ds(r0 + r, n_shift), cols]
                for k in taps:
                    acc = acc + sh_ref[r, pl.ds(off + k - r, rows), :] * w_ref[pl.ds(k, 1), cols]
            acc_ref[pl.ds(r0, rows), cols] = acc + b_ref[:, cols]
    c = acc_ref[...]
    mu = jnp.mean(c, axis=-1, keepdims=True)
    var = jnp.mean(jnp.square(c - mu), axis=-1, keepdims=True)
    y = (c - mu) * lax.rsqrt(var + EPS) * g_ref[...] + beta_ref[...]
    o_ref[...] = _silu(y).astype(o_ref.dtype)


def _conv_prompt(u, w, b, g, beta, batch, seq, tt):
    nt = seq // tt
    vec = pl.BlockSpec((1, D_CONV), lambda bi, i: (0, 0))
    return pl.pallas_call(
        functools.partial(_conv_prompt_kernel, tt=tt),
        grid=(batch, nt),
        in_specs=[
            pl.BlockSpec((tt, D_CONV), lambda bi, i: (bi * nt + i, 0)),
            pl.BlockSpec((tt, D_CONV), lambda bi, i: (bi * nt + jnp.maximum(i - 1, 0), 0)),
            pl.BlockSpec((CONV_W, D_CONV), lambda bi, i: (0, 0)),
            vec, vec, vec,
        ],
        out_specs=pl.BlockSpec((tt, D_CONV), lambda bi, i: (bi * nt + i, 0)),
        out_shape=jax.ShapeDtypeStruct((batch * seq, D_CONV), BF16),
        scratch_shapes=[pltpu.VMEM((tt + _HALO, D_CONV), F32), pltpu.VMEM((tt, D_CONV), F32),
                        pltpu.VMEM((8, tt + _HALO, LANES), F32)],
        compiler_params=_params(("parallel", "parallel")),
        name="conv_prompt",
    )(u, u, w, b, g, beta)


def _conv_sample_kernel(st_ref, u_ref, w_ref, b_ref, g_ref, beta_ref, c_ref, new_ref):
    hist = CONV_W - 1
    u = u_ref[...]
    acc = u * w_ref[pl.ds(hist, 1), :] + b_ref[...]
    for t in range(hist):
        acc = acc + st_ref[t] * w_ref[pl.ds(t, 1), :]
    mu = jnp.mean(acc, axis=-1, keepdims=True)
    var = jnp.mean(jnp.square(acc - mu), axis=-1, keepdims=True)
    y = (acc - mu) * lax.rsqrt(var + EPS) * g_ref[...] + beta_ref[...]
    c_ref[...] = _silu(y).astype(c_ref.dtype)
    for t in range(hist - 1):
        new_ref[t] = st_ref[t + 1]
    new_ref[hist - 1] = u


def _conv_sample(state_t, u, w, b, g, beta):
    hist, bs, _ = state_t.shape
    return pl.pallas_call(
        _conv_sample_kernel,
        out_shape=[jax.ShapeDtypeStruct((bs, D_CONV), BF16), jax.ShapeDtypeStruct((hist, bs, D_CONV), F32)],
        compiler_params=pltpu.CompilerParams(vmem_limit_bytes=48 * MIB),
        name="conv_sample",
    )(state_t, u, w, b, g, beta)


def _bias_lookup(rb_ref, head, dist, valid):
    n = jnp.maximum(dist, 0)
    max_exact = N_BUCKETS // 2
    nf = jnp.maximum(n, 1).astype(F32)
    large = max_exact + (jnp.log(nf / max_exact) / math.log(MAX_DIST / max_exact)
                         * (N_BUCKETS - max_exact)).astype(I32)
    bucket = jnp.where(n < max_exact, n, jnp.minimum(large, N_BUCKETS - 1))
    val = jnp.zeros(dist.shape, F32)
    for k in range(N_BUCKETS):
        val = jnp.where(bucket == k, rb_ref[head, k], val)
    return jnp.where(valid, val, NEG)


def _bias_prompt_kernel(rb_ref, cmp_ref, win_ref, sel_ref, *, nqt, n_chunk):
    g = pl.program_id(0)
    key = lax.broadcasted_iota(I32, (LANES, QT), 0)
    qry = lax.broadcasted_iota(I32, (LANES, QT), 1)
    for hh in range(HPG):
        head = g * HPG + hh
        cols = pl.ds(hh * QT, QT)
        lookup = lambda dist, valid: _bias_lookup(rb_ref, head, dist, valid) * LOG2E

        for eb in range(_cmp_map_rows(nqt, n_chunk) // LANES):
            block = key + eb * LANES - _cmp_map_shift(nqt, 0)
            dist = qry - block * CMP_STRIDE - (CMP_LEN - 1)
            cmp_ref[0, pl.ds(eb * LANES, LANES), cols] = lookup(dist, dist >= 0)
        for jb in range((WINDOW + QT) // LANES):
            dist = qry - (key + jb * LANES) + WINDOW
            win_ref[0, pl.ds(jb * LANES, LANES), cols] = lookup(dist, (dist >= 0) & (dist <= WINDOW))
        for kind in range(3):
            for jb in range(2):
                dist = kind * QT + qry - (key + jb * LANES)
                sel_ref[0, kind, pl.ds(jb * LANES, LANES), cols] = lookup(dist, dist >= 0)
        far = jnp.full((LANES, QT), 2 * MAX_DIST, I32)
        for jb in range(2):
            sel_ref[0, 3, pl.ds(jb * LANES, LANES), cols] = lookup(far, far >= 0)


def _cmp_map_shift(nqt, qt):
    return (nqt - 1 - qt) * (QT // CMP_STRIDE)


def _cmp_map_rows(nqt, n_chunk):
    return -(-(n_chunk + _cmp_map_shift(nqt, 0)) // LANES) * LANES


def _bias_prompt(rb_t, nqt, n_chunk):
    ext = _cmp_map_rows(nqt, n_chunk)
    return pl.pallas_call(
        functools.partial(_bias_prompt_kernel, nqt=nqt, n_chunk=n_chunk),
        grid=(N_KV,),
        in_specs=[pl.BlockSpec(memory_space=pltpu.SMEM)],
        out_specs=[
            pl.BlockSpec((1, ext, GROUP_ROWS), lambda g: (g, 0, 0)),
            pl.BlockSpec((1, WINDOW + QT, GROUP_ROWS), lambda g: (g, 0, 0)),
            pl.BlockSpec((1, 4, 2 * LANES, GROUP_ROWS), lambda g: (g, 0, 0, 0)),
        ],
        out_shape=[
            jax.ShapeDtypeStruct((N_KV, ext, GROUP_ROWS), F32),
            jax.ShapeDtypeStruct((N_KV, WINDOW + QT, GROUP_ROWS), F32),
            jax.ShapeDtypeStruct((N_KV, 4, 2 * LANES, GROUP_ROWS), F32),
        ],
        compiler_params=_params(("parallel",)),
        name="bias_maps_prompt",
    )(rb_t)


def _bias_sample_kernel(rb_ref, cmp_ref, win_ref, sel_ref, *, past, n_chunk, n_pages):
    col = lax.broadcasted_iota(I32, (N_HEADS, LANES), 1)
    rb = rb_ref[...]

    def lookup(dist):
        n = jnp.maximum(dist, 0)
        max_exact = N_BUCKETS // 2
        nf = jnp.maximum(n, 1).astype(F32)
        large = max_exact + (jnp.log(nf / max_exact) / math.log(MAX_DIST / max_exact)
                             * (N_BUCKETS - max_exact)).astype(I32)
        bucket = jnp.where(n < max_exact, n, jnp.minimum(large, N_BUCKETS - 1))
        val = jnp.zeros(dist.shape, F32)
        for k in range(N_BUCKETS):
            val = jnp.where(bucket == k, rb[:, k:k + 1], val)
        return jnp.where(dist >= 0, val, NEG)

    def cmp_block(cb, carry):
        dist = past - (col + cb * LANES) * CMP_STRIDE - (CMP_LEN - 1)
        cmp_ref[cb] = lookup(dist)
        return carry

    lax.fori_loop(0, n_chunk // LANES, cmp_block, 0)

    def win_block(jb, carry):
        win_ref[jb] = lookup(WINDOW - (col + jb * LANES))
        return carry

    lax.fori_loop(0, WINDOW // LANES + 1, win_block, 0)

    def sel_page(p, carry):
        sel_ref[p] = lookup(past - (p * PAGE + col))
        return carry

    lax.fori_loop(0, n_pages + 1, sel_page, 0)


def _bias_sample(rb_t, past, n_chunk, n_pages):
    cmp_b, win_b, sel_b = pl.pallas_call(
        functools.partial(_bias_sample_kernel, past=past, n_chunk=n_chunk, n_pages=n_pages),
        out_shape=[
            jax.ShapeDtypeStruct((n_chunk // LANES, N_HEADS, LANES), F32),
            jax.ShapeDtypeStruct((WINDOW // LANES + 1, N_HEADS, LANES), F32),
            jax.ShapeDtypeStruct((n_pages + 1, N_HEADS, LANES), F32),
        ],
        name="bias_maps_sample",
    )(rb_t)
    flat = lambda a: a.transpose(1, 0, 2).reshape(N_HEADS, -1)
    return flat(cmp_b), flat(win_b), sel_b


_CHUNKS_PER_PAGE = PAGE // CMP_STRIDE
_CHUNK_PITCH = 24


def _compress_kernel(pt_ref, *refs, pp):
    page_refs = refs[:pp + 1]
    w0k_ref, w1k_ref, w0v_ref, w1v_ref, pe_ref, w2k_ref, w2v_ref, o_ref, tr_ref = refs[pp + 1:]
    del pt_ref
    nch = pp * _CHUNKS_PER_PAGE
    m = nch + 8
    n_slab = D_KV // LANES

    def transpose(s):
        for k in range(pp + 1):
            t = page_refs[k][pl.ds(s * LANES, LANES), :].T
            for n in range(_CHUNKS_PER_PAGE):
                row = (k * _CHUNKS_PER_PAGE + n) * _CHUNK_PITCH
                tr_ref[s, pl.ds(row, CMP_STRIDE), :] = t[n * CMP_STRIDE:(n + 1) * CMP_STRIDE]

    def chunks(s, _):
        return jnp.concatenate(
            [tr_ref.at[s][pl.ds(p, m, stride=_CHUNK_PITCH), :] for p in range(CMP_STRIDE)], axis=1)

    def mlp(s, x):
        is_v = s >= n_slab // 2
        w0 = (w0v_ref if is_v else w0k_ref)[...]
        w1 = (w1v_ref if is_v else w1k_ref)[...]
        w2 = (w2v_ref if is_v else w2k_ref)[...]
        pe0 = pe_ref[pl.ds(2 * int(is_v), 1), :]
        pe1 = pe_ref[pl.ds(2 * int(is_v) + 1, 1), :]
        y0 = _mm((x + pe0).astype(BF16), w0)
        y1 = _mm((x + pe1).astype(BF16), w1)
        pre = y0[0:nch] + y1[1:nch + 1]
        o_ref[0, :, pl.ds(s * LANES, LANES)] = _mm(_gelu_tanh(pre).astype(BF16), w2)

    _staggered(n_slab, transpose, chunks, mlp)


def _compress(pages, page_ids, weights, nb, n_pages, pp, paged):
    w0k, w1k, w0v, w1v, pe, w2k, w2v = weights
    steps = n_pages // pp
    nch = pp * _CHUNKS_PER_PAGE

    def page_spec(k):
        if paged:
            return pl.BlockSpec(
                (None, D_KV, PAGE),
                lambda b, s, pt: (pt[b * n_pages + jnp.minimum(s * pp + k, n_pages - 1)], 0, 0))
        return pl.BlockSpec((None, D_KV, PAGE), lambda b, s, pt: (b, 0, jnp.minimum(s * pp + k, n_pages - 1)))

    full = lambda a: pl.BlockSpec(a.shape, lambda b, s, pt: (0,) * a.ndim)
    grid_spec = pltpu.PrefetchScalarGridSpec(
        num_scalar_prefetch=1,
        grid=(nb, steps),
        in_specs=[page_spec(k) for k in range(pp + 1)] + [full(a) for a in weights],
        out_specs=pl.BlockSpec((1, nch, D_KV), lambda b, s, pt: (b, s, 0)),
        scratch_shapes=[pltpu.VMEM((D_KV // LANES, (pp + 1) * _CHUNKS_PER_PAGE * _CHUNK_PITCH, LANES), F32)],
    )
    return pl.pallas_call(
        functools.partial(_compress_kernel, pp=pp),
        grid_spec=grid_spec,
        out_shape=jax.ShapeDtypeStruct((nb, n_pages * _CHUNKS_PER_PAGE, D_KV), F32),
        compiler_params=_params(("parallel", "parallel"), 56),
        name="compress_paged" if paged else "compress_prompt",
    )(page_ids, *([pages] * (pp + 1)), w0k, w1k, w0v, w1v, pe, w2k, w2v)


def _compress_weights(pe_k, w1_k, w2_k, pe_v, w1_v, w2_v):
    eye = jnp.eye(2, dtype=F32)

    def first(w1, j):
        w = w1[j * CMP_STRIDE:(j + 1) * CMP_STRIDE]
        return jnp.einsum("pdf,gh->pgdhf", w, eye).reshape(CMP_STRIDE * LANES, 2 * CMP_HID).astype(BF16)

    def pos(pe, j):
        return jnp.tile(pe[j * CMP_STRIDE:(j + 1) * CMP_STRIDE, None, :], (1, 2, 1)).reshape(1, CMP_STRIDE * LANES)

    def second(w2):
        return jnp.einsum("fd,gh->gfhd", w2, eye).reshape(2 * CMP_HID, 2 * HEAD_DIM).astype(BF16)

    pe = jnp.concatenate([pos(pe_k, 0), pos(pe_k, 1), pos(pe_v, 0), pos(pe_v, 1)], axis=0)
    return (first(w1_k, 0), first(w1_k, 1), first(w1_v, 0), first(w1_v, 1), pe, second(w2_k), second(w2_v))


def _staggered(n, scores, softmax, values, ahead=1, lag=0):
    s = {g: scores(g) for g in range(min(ahead, n))}
    p = {}
    out = []
    for g in range(n + lag):
        if g + ahead < n:
            s[g + ahead] = scores(g + ahead)
        if g < n:
            p[g] = softmax(g, s.pop(g))
        if g - lag >= 0:
            out.append(values(g - lag, p.pop(g - lag)))
    return out


def _group_queries(qt_ref, g):
    return jnp.concatenate(
        [qt_ref[pl.ds((g * HPG + hh) * HEAD_DIM, HEAD_DIM), :] for hh in range(HPG)], axis=1)


def _store_heads(o_ref, o_t, gate, g, branch):
    lane = lax.broadcasted_iota(I32, (QT, 2 * HEAD_DIM), 1)
    for pair in range(HPG // 2):
        h = g * HPG + 2 * pair
        col = branch * N_HEADS + h
        both = jnp.concatenate([o_t[:, (2 * pair) * QT:(2 * pair + 1) * QT],
                                o_t[:, (2 * pair + 1) * QT:(2 * pair + 2) * QT]], axis=0)
        gates = jnp.where(lane < HEAD_DIM, gate[:, col:col + 1], gate[:, col + 1:col + 2])
        o_ref[:, pl.ds(h * HEAD_DIM, 2 * HEAD_DIM)] = both.T * gates


def _cmp_prompt_kernel(qt_ref, kc_ref, map_ref, gate_ref, ov_ref, o_ref, sb_ref, *, n_sel, top):
    qt = pl.program_id(1)
    kv = kc_ref[0]
    gate = gate_ref[...]
    ov = ov_ref[...]
    half = D_KV // 2
    n_chunk = kv.shape[0]
    map_rows = pl.ds(pl.multiple_of(_cmp_map_shift(pl.num_programs(1), qt), 8), n_chunk)
    bias = lambda g: map_ref[g, map_rows, :]
    vt_all = kv[:, half:].T
    jrow = lax.broadcasted_iota(I32, (n_sel, QT), 0)
    tpos = qt * QT + lax.broadcasted_iota(I32, (n_sel, QT), 1)
    cur = tpos // SEL_LEN
    forced = (jrow == 0) | (jrow == cur) | (jrow == cur - 1)
    valid = jrow * SEL_LEN <= tpos
    def scores(g):
        kg = kv[:, g * HEAD_DIM:(g + 1) * HEAD_DIM].astype(BF16)
        return _mm(kg, _group_queries(qt_ref, g)) + bias(g)

    def softmax(g, st):
        e = jnp.exp2(st - jnp.max(st, axis=0, keepdims=True))
        p = e / jnp.sum(e, axis=0, keepdims=True) * (bias(g) > 0.5 * NEG).astype(F32)
        return p.astype(BF16)

    def values(g, pb):
        vgt = vt_all[g * HEAD_DIM:(g + 1) * HEAD_DIM].astype(BF16)
        _store_heads(o_ref, _mm(vgt, pb), gate, g, 0)
        return pb

    probs = _staggered(N_KV, scores, softmax, values, ahead=2, lag=1)
    for g in range(N_KV):
        pb = probs[g]
        imp = jnp.zeros((n_sel, QT), F32)
        for hh in range(HPG):
            imp = imp + _mm(ov, pb[:, hh * QT:(hh + 1) * QT])
        score = jnp.where(forced, jnp.inf, jnp.where(valid, imp, -jnp.inf))
        rank = jnp.zeros((n_sel, QT), I32)
        for i in range(n_sel):
            si = score[i:i + 1, :]
            ahead = (si > score) | ((si == score) & (jrow > i))
            rank = rank + ahead.astype(I32)
        sb_ref[pl.ds(g * n_sel, n_sel), :] = jnp.where(rank < top, 0.0, NEG).astype(sb_ref.dtype)


def _cmp_prompt(q_t, kcmp, cmap, gates, ov_t, batch, seq):
    nqt = seq // QT
    n_chunk = kcmp.shape[1]
    n_sel = ov_t.shape[0]
    top = min(SEL_TOP, n_sel)
    return pl.pallas_call(
        functools.partial(_cmp_prompt_kernel, n_sel=n_sel, top=top),
        grid=(batch, nqt),
        in_specs=[
            pl.BlockSpec((D_Q, QT), lambda b, i: (0, b * nqt + i)),
            pl.BlockSpec((1, n_chunk, D_KV), lambda b, i: (b, 0, 0)),
            pl.BlockSpec(cmap.shape, lambda b, i: (0, 0, 0)),
            pl.BlockSpec((QT, LANES), lambda b, i: (b * nqt + i, 0)),
            pl.BlockSpec(ov_t.shape, lambda b, i: (0, 0)),
        ],
        out_specs=[
            pl.BlockSpec((QT, D_Q), lambda b, i: (b * nqt + i, 0)),
            pl.BlockSpec((None, N_KV * n_sel, QT), lambda b, i: (b * nqt + i, 0, 0)),
        ],
        out_shape=[
            jax.ShapeDtypeStruct((batch * seq, D_Q), F32),
            jax.ShapeDtypeStruct((batch * nqt, N_KV * n_sel, QT), BF16),
        ],
        compiler_params=_params(("parallel", "parallel")),
        name="cmp_attn_prompt",
    )(q_t, kcmp, cmap, gates, ov_t)


_TK = 2 * LANES


def _sel_prompt_kernel(qt_ref, kv_ref, sb_ref, map_ref, gate_ref, o_ref, ka_ref, vt_ref, qa_ref, *, n_sel, seq):
    qt = pl.program_id(1)
    aug = 2 * HEAD_DIM
    half = D_KV // 2

    @pl.when(qt == 0)
    def _():
        pos = lax.broadcasted_iota(I32, (seq, aug - HEAD_DIM), 0)
        blk = lax.broadcasted_iota(I32, (seq, aug - HEAD_DIM), 1)
        onehot = jnp.where(pos // SEL_LEN == blk, 1.0, 0.0)
        k_rows = kv_ref[0, pl.ds(0, half), :].T
        for g in range(N_KV):
            ka_ref[g] = jnp.concatenate([k_rows[:, g * HEAD_DIM:(g + 1) * HEAD_DIM], onehot], axis=1).astype(BF16)
            vt_ref[g] = kv_ref[0, pl.ds(half + g * HEAD_DIM, HEAD_DIM), :].astype(BF16)

    gate = gate_ref[...]
    sb = sb_ref[...]
    diag = qt // 2
    n_tiles = diag + 1
    pad = jnp.zeros((aug - HEAD_DIM - n_sel, GROUP_ROWS), BF16)
    for g in range(N_KV):
        sbg = sb[g * n_sel:(g + 1) * n_sel]
        qa_ref[g] = jnp.concatenate([_group_queries(qt_ref, g), jnp.concatenate([sbg] * HPG, axis=1), pad], axis=0)

    def tile(kt, carry):
        start = pl.multiple_of(kt * _TK, _TK)
        kind = jnp.where(kt == diag, qt % 2, jnp.where((kt == diag - 1) & (qt % 2 == 0), 2, 3))
        def scores(g):
            return _mm(ka_ref[g, pl.ds(start, _TK), :], qa_ref[g]) + map_ref[g, kind]

        def softmax(g, st):
            m_i, l_i, _ = carry[g]
            m_new = jnp.maximum(m_i, jnp.max(st, axis=0, keepdims=True))
            alpha = jnp.exp2(m_i - m_new)
            p = jnp.exp2(st - m_new)
            return m_new, alpha * l_i + jnp.sum(p, axis=0, keepdims=True), alpha, p.astype(BF16)

        def values(g, sm):
            m_new, l_new, alpha, pb = sm
            return m_new, l_new, alpha * carry[g][2] + _mm(vt_ref[g, :, pl.ds(start, _TK)], pb)

        return tuple(_staggered(N_KV, scores, softmax, values, ahead=2, lag=1))

    init = tuple((jnp.full((1, GROUP_ROWS), -jnp.inf, F32), jnp.zeros((1, GROUP_ROWS), F32),
                  jnp.zeros((HEAD_DIM, GROUP_ROWS), F32)) for _ in range(N_KV))
    final = lax.fori_loop(0, n_tiles, tile, init)
    for g in range(N_KV):
        _, l_f, acc = final[g]
        _store_heads(o_ref, acc / l_f, gate, g, 1)


def _sel_prompt(q_t, kv_t, selbias, smap, gates, batch, seq):
    nqt = seq // QT
    n_sel = selbias.shape[1] // N_KV
    return pl.pallas_call(
        functools.partial(_sel_prompt_kernel, n_sel=n_sel, seq=seq),
        grid=(batch, nqt),
        in_specs=[
            pl.BlockSpec((D_Q, QT), lambda b, i: (0, b * nqt + i)),
            pl.BlockSpec((1, D_KV, seq), lambda b, i: (b, 0, 0)),
            pl.BlockSpec((None, N_KV * n_sel, QT), lambda b, i: (b * nqt + i, 0, 0)),
            pl.BlockSpec(smap.shape, lambda b, i: (0, 0, 0, 0)),
            pl.BlockSpec((QT, LANES), lambda b, i: (b * nqt + i, 0)),
        ],
        out_specs=pl.BlockSpec((QT, D_Q), lambda b, i: (b * nqt + i, 0)),
        out_shape=jax.ShapeDtypeStruct((batch * seq, D_Q), F32),
        scratch_shapes=[pltpu.VMEM((N_KV, seq, 2 * HEAD_DIM), BF16), pltpu.VMEM((N_KV, HEAD_DIM, seq), BF16),
                        pltpu.VMEM((N_KV, 2 * HEAD_DIM, GROUP_ROWS), BF16)],
        compiler_params=_params(("arbitrary", "arbitrary"), 56),
        name="sel_attn_prompt",
    )(q_t, kv_t, selbias, smap, gates)


_WIN_TILES = WINDOW // QT + 1


def _win_prompt_kernel(qt_ref, *refs):
    kv_refs = refs[:_WIN_TILES]
    map_ref, gate_ref, o_ref = refs[_WIN_TILES:]
    qt = pl.program_id(1)
    gate = gate_ref[...]
    half = D_KV // 2
    span = _WIN_TILES * QT
    key = lax.broadcasted_iota(I32, (span, GROUP_ROWS), 0)
    in_seq = key >= (_WIN_TILES - 1 - qt) * QT
    k_rows = jnp.concatenate([r[0, pl.ds(0, half), :].T for r in kv_refs], axis=0)
    def scores(g):
        kg = k_rows[:, g * HEAD_DIM:(g + 1) * HEAD_DIM].astype(BF16)
        return jnp.where(in_seq, _mm(kg, _group_queries(qt_ref, g)) + map_ref[g], NEG)

    def softmax(g, st):
        e = jnp.exp2(st - jnp.max(st, axis=0, keepdims=True))
        return (e / jnp.sum(e, axis=0, keepdims=True)).astype(BF16)

    def values(g, pb):
        vgt = jnp.concatenate([r[0, pl.ds(half + g * HEAD_DIM, HEAD_DIM), :] for r in kv_refs],
                              axis=1).astype(BF16)
        _store_heads(o_ref, _mm(vgt, pb), gate, g, 2)

    _staggered(N_KV, scores, softmax, values, ahead=2, lag=1)


def _win_prompt(q_t, kv_t, wmap, gates, batch, seq):
    nqt = seq // QT

    def kv_spec(k):
        return pl.BlockSpec((1, D_KV, QT), lambda b, i: (b, 0, jnp.maximum(i - (_WIN_TILES - 1) + k, 0)))

    return pl.pallas_call(
        _win_prompt_kernel,
        grid=(batch, nqt),
        in_specs=[pl.BlockSpec((D_Q, QT), lambda b, i: (0, b * nqt + i))]
        + [kv_spec(k) for k in range(_WIN_TILES)]
        + [pl.BlockSpec(wmap.shape, lambda b, i: (0, 0, 0)),
           pl.BlockSpec((QT, LANES), lambda b, i: (b * nqt + i, 0))],
        out_specs=pl.BlockSpec((QT, D_Q), lambda b, i: (b * nqt + i, 0)),
        out_shape=jax.ShapeDtypeStruct((batch * seq, D_Q), F32),
        compiler_params=_params(("parallel", "parallel")),
        name="win_attn_prompt",
    )(q_t, *([kv_t] * _WIN_TILES), wmap, gates)


def _group_rows(x_by_group):
    row = lax.broadcasted_iota(I32, x_by_group[0].shape, 0)
    out = x_by_group[0]
    for g in range(1, N_KV):
        out = jnp.where(row // HPG == g, x_by_group[g], out)
    return out


_CMP_SAMPLE_ROWS = 4


def _cmp_sample_kernel(q_ref, kc_ref, bias_ref, ov_ref, gate_ref, o_ref, idx_ref, *, n_sel, past):
    bias = bias_ref[...]
    half = D_KV // 2
    imps = []
    for bl in range(_CMP_SAMPLE_ROWS):
        q = q_ref[bl]
        kv = kc_ref[bl].astype(BF16)
        s = _group_rows([_mm_nt(q, kv[:, g * HEAD_DIM:(g + 1) * HEAD_DIM]) for g in range(N_KV)]) + bias
        e = jnp.exp(s - jnp.max(s, axis=-1, keepdims=True))
        p = e / jnp.sum(e, axis=-1, keepdims=True) * (bias > 0.5 * NEG).astype(F32)
        pb = p.astype(BF16)
        o = _group_rows([_mm(pb, kv[:, half + g * HEAD_DIM:half + (g + 1) * HEAD_DIM]) for g in range(N_KV)])
        o_ref[bl] = o * gate_ref[bl]
        imp_h = _mm(pb, ov_ref[...])
        imps += [imp_h[g * HPG:g * HPG + 1] + imp_h[g * HPG + 1:g * HPG + 2] + imp_h[g * HPG + 2:g * HPG + 3]
                 + imp_h[g * HPG + 3:g * HPG + 4] for g in range(N_KV)]
    imp = jnp.concatenate(imps, axis=0)
    rows, nsp = imp.shape
    j = lax.broadcasted_iota(I32, (rows, nsp), 1)
    cur = past // SEL_LEN
    forced = (j == 0) | (j == cur) | (j == cur - 1)
    valid = (j * SEL_LEN <= past) & (j < n_sel)
    score = jnp.where(valid, jnp.where(forced, jnp.inf, imp), -jnp.inf)
    slot = lax.broadcasted_iota(I32, (rows, LANES), 1)
    res = jnp.zeros((rows, LANES), I32)
    for k in range(min(SEL_TOP, n_sel)):
        best = jnp.max(score, axis=-1, keepdims=True)
        pick = jnp.min(jnp.where(score == best, j, nsp), axis=-1, keepdims=True)
        res = jnp.where(slot == k, pick, res)
        score = jnp.where(j == pick, -jnp.inf, score)
    idx_ref[...] = res


def _cmp_sample(q3, kcmp, bias, ov, gate3, n_sel, past):
    bs, n_chunk, _ = kcmp.shape
    nb = _CMP_SAMPLE_ROWS
    heads = pl.BlockSpec((nb, N_HEADS, HEAD_DIM), lambda b: (b, 0, 0))
    return pl.pallas_call(
        functools.partial(_cmp_sample_kernel, n_sel=n_sel, past=past),
        grid=(bs // nb,),
        in_specs=[
            heads,
            pl.BlockSpec((nb, n_chunk, D_KV), lambda b: (b, 0, 0)),
            pl.BlockSpec(bias.shape, lambda b: (0, 0)),
            pl.BlockSpec(ov.shape, lambda b: (0, 0)),
            heads,
        ],
        out_specs=[heads, pl.BlockSpec((nb * N_KV, LANES), lambda b: (b, 0))],
        out_shape=[
            jax.ShapeDtypeStruct((bs, N_HEADS, HEAD_DIM), F32),
            jax.ShapeDtypeStruct((bs * N_KV, LANES), I32),
        ],
        compiler_params=_params(("parallel",)),
        name="cmp_attn_sample",
    )(q3, kcmp, bias, ov, gate3)


def _sel_sample_kernel(blk_ref, phys_ref, q_ref, new_ref, bias_ref, gate_ref, *refs, top, n_pages):
    kv_refs = refs[:top]
    k_refs = [r.at[0] for r in kv_refs]
    v_refs = [r.at[1] for r in kv_refs]
    o_ref = refs[top]
    del phys_ref
    b = pl.program_id(0)
    g = pl.program_id(1)
    q = q_ref[0, 0]
    lane = lax.broadcasted_iota(I32, (8, PAGE), 1)
    scores = []
    for k in range(top):
        blk = blk_ref[(b * N_KV + g) * top + k]
        page = jnp.minimum(blk // 2, n_pages)
        s = _mm(q, k_refs[k][...].astype(BF16)) + bias_ref[page]
        ok = (lane // SEL_LEN == blk % 2) & (blk < 2 * n_pages)
        scores.append(jnp.where(ok, s, NEG))
    k_new = new_ref[0, 0, 0:1, :].astype(BF16).astype(F32)
    v_new = new_ref[0, 0, 1:2, :].astype(BF16).astype(F32)
    s_new = jnp.sum(q.astype(F32) * k_new, axis=-1, keepdims=True) + bias_ref[n_pages]
    scores.append(s_new)
    s_all = jnp.concatenate(scores, axis=1)
    e = jnp.exp(s_all - jnp.max(s_all, axis=-1, keepdims=True))
    p = e / jnp.sum(e, axis=-1, keepdims=True)
    o = p[:, top * PAGE:top * PAGE + 1].astype(BF16).astype(F32) * v_new
    for k in range(top):
        o = o + _mm_nt(p[:, k * PAGE:(k + 1) * PAGE].astype(BF16), v_refs[k][...].astype(BF16))
    o_ref[0, 0] = o * gate_ref[0, 0]


def _sel_sample(blk, phys, q4, new4, bias4, gate4, pool5, top, n_pages):
    bs = q4.shape[0]

    def kv_spec(k):
        return pl.BlockSpec(
            (None, 2, None, HEAD_DIM, PAGE),
            lambda b, g, blk_r, phys_r: (phys_r[(b * N_KV + g) * top + k], 0, g, 0, 0))

    grp = lambda b, g, blk_r, phys_r: (b, g, 0, 0)
    grid_spec = pltpu.PrefetchScalarGridSpec(
        num_scalar_prefetch=2,
        grid=(bs, N_KV),
        in_specs=[
            pl.BlockSpec((1, 1, 8, HEAD_DIM), grp),
            pl.BlockSpec((1, 1, 8, HEAD_DIM), grp),
            pl.BlockSpec((None, n_pages + 1, 8, LANES), lambda b, g, blk_r, phys_r: (g, 0, 0, 0)),
            pl.BlockSpec((1, 1, 8, HEAD_DIM), grp),
        ] + [kv_spec(k) for k in range(top)],
        out_specs=pl.BlockSpec((1, 1, 8, HEAD_DIM), grp),
    )
    return pl.pallas_call(
        functools.partial(_sel_sample_kernel, top=top, n_pages=n_pages),
        grid_spec=grid_spec,
        out_shape=jax.ShapeDtypeStruct((bs, N_KV, 8, HEAD_DIM), F32),
        compiler_params=_params(("parallel", "parallel")),
        name="sel_attn_sample",
    )(blk, phys, q4, new4, bias4, gate4, *([pool5] * top))


def _win_sample_kernel(q_ref, new_ref, st_ref, col_ref, bias_ref, gate_ref, o_ref, nst_ref):
    st = st_ref[0]
    half = D_KV // 2
    for g in range(N_KV):
        q = q_ref[0, g]
        bias = bias_ref[g]
        s_old = _mm(q, st[g * HEAD_DIM:(g + 1) * HEAD_DIM, :].astype(BF16)) + bias[:, 0:WINDOW]
        k_new = new_ref[0, g, 0:1, :].astype(BF16).astype(F32)
        v_new = new_ref[0, g, 1:2, :].astype(BF16).astype(F32)
        s_new = jnp.sum(q.astype(F32) * k_new, axis=-1, keepdims=True) + bias[:, WINDOW:WINDOW + LANES]
        s_all = jnp.concatenate([s_old, s_new], axis=1)
        e = jnp.exp(s_all - jnp.max(s_all, axis=-1, keepdims=True))
        p = e / jnp.sum(e, axis=-1, keepdims=True)
        o = _mm_nt(p[:, 0:WINDOW].astype(BF16), st[half + g * HEAD_DIM:half + (g + 1) * HEAD_DIM, :].astype(BF16))
        o = o + p[:, WINDOW:WINDOW + 1].astype(BF16).astype(F32) * v_new
        o_ref[0, g] = o * gate_ref[0, g]
    rolled = pltpu.roll(st, WINDOW - 1, axis=1)
    lane = lax.broadcasted_iota(I32, (D_KV, LANES), 1)
    nst_ref[0, :, pl.ds(0, WINDOW - LANES)] = rolled[:, 0:WINDOW - LANES]
    nst_ref[0, :, pl.ds(WINDOW - LANES, LANES)] = jnp.where(
        lane == LANES - 1, col_ref[0], rolled[:, WINDOW - LANES:WINDOW])


def _win_sample(q4, new4, state_t, newcol, bias4, gate4):
    bs = q4.shape[0]
    grp = pl.BlockSpec((1, N_KV, 8, HEAD_DIM), lambda b: (b, 0, 0, 0))
    return pl.pallas_call(
        _win_sample_kernel,
        grid=(bs,),
        in_specs=[
            grp, grp,
            pl.BlockSpec((1, D_KV, WINDOW), lambda b: (b, 0, 0)),
            pl.BlockSpec((1, D_KV, LANES), lambda b: (b, 0, 0)),
            pl.BlockSpec(bias4.shape, lambda b: (0, 0, 0)),
            grp,
        ],
        out_specs=[grp, pl.BlockSpec((1, D_KV, WINDOW), lambda b: (b, 0, 0))],
        out_shape=[
            jax.ShapeDtypeStruct((bs, N_KV, 8, HEAD_DIM), F32),
            jax.ShapeDtypeStruct((bs, D_KV, WINDOW), F32),
        ],
        compiler_params=_params(("parallel",)),
        name="win_attn_sample",
    )(q4, new4, state_t, newcol, bias4, gate4)


_MOE_ROWS = 256
_MOE_SLACK = 2 * _MOE_ROWS


def _row_copy(src, src_row, dst, dst_row, sem):
    return pltpu.make_async_copy(src.at[pl.ds(src_row, 1)], dst.at[pl.ds(dst_row, 1)], sem)


def _expert_kernel(be_ref, nu_ref, dst_ref, src_ref, nxt_ref, h_ref, w_ref, wg_ref, wu_ref, wd_ref, yk_ref,
                   xbuf, ybuf, gsem, ssem, *, n_rows, n_live):
    del be_ref
    i = pl.program_id(0)
    slot = i % 2
    n_used = nu_ref[0]

    def gather(rows_ref, dst_slot):
        for r in range(_MOE_ROWS):
            _row_copy(h_ref, rows_ref[0, r], xbuf.at[dst_slot], r, gsem.at[dst_slot]).start()

    def wait_gather(s):
        for r in range(_MOE_ROWS):
            _row_copy(h_ref, 0, xbuf.at[s], 0, gsem.at[s]).wait()

    def scatter(s):
        for r in range(_MOE_ROWS):
            _row_copy(ybuf.at[s], r, yk_ref, dst_ref[0, r], ssem.at[s]).start()

    def wait_scatter(s):
        for r in range(_MOE_ROWS):
            _row_copy(ybuf.at[s], 0, yk_ref, 0, ssem.at[s]).wait()

    @pl.when(i == 0)
    def _():
        gather(src_ref, 0)
        ybuf[1] = jnp.zeros((_MOE_ROWS, ybuf.shape[2]), F32)
        fills = [
            pltpu.make_async_copy(ybuf.at[1, pl.ds(0, min(_MOE_ROWS, n_rows - off))],
                                  yk_ref.at[pl.ds(k * n_rows + off, min(_MOE_ROWS, n_rows - off))], ssem.at[1])
            for k in range(TOP_K) for off in range(n_live, n_rows, _MOE_ROWS)]
        for cp in fills:
            cp.start()
        for cp in fills:
            cp.wait()

    @pl.when(i + 1 < n_used)
    def _():
        gather(nxt_ref, 1 - slot)

    @pl.when(i < n_used)
    def _():
        wait_gather(slot)

        @pl.when(i >= 2)
        def _():
            wait_scatter(slot)

        x = xbuf[slot].astype(BF16)
        a = _silu(_mm(x, wg_ref[...])) * _mm(x, wu_ref[...])
        ybuf[slot] = _mm(a.astype(BF16), wd_ref[...]) * w_ref[...]
        scatter(slot)

        @pl.when(i == n_used - 1)
        def _():
            @pl.when(i >= 1)
            def _():
                wait_scatter(1 - slot)

            wait_scatter(slot)


def _experts(blk_e, n_used, rows_src, rows_dst, rows_w, h2_all, n_live, w_gate, w_up, w_down):
    n_rows, d = h2_all.shape
    n_blk = blk_e.shape[0]
    src3 = rows_src.reshape(n_blk, 1, _MOE_ROWS)
    dst3 = rows_dst.reshape(n_blk, 1, _MOE_ROWS)
    wspec = lambda a: pl.BlockSpec((None,) + a.shape[1:], lambda i, be, nu: (be[i], 0, 0))
    smem_blk = lambda fn: pl.BlockSpec((None, 1, _MOE_ROWS), fn, memory_space=pltpu.SMEM)
    grid_spec = pltpu.PrefetchScalarGridSpec(
        num_scalar_prefetch=2,
        grid=(n_blk,),
        in_specs=[
            smem_blk(lambda i, be, nu: (i, 0, 0)),
            smem_blk(lambda i, be, nu: (i, 0, 0)),
            smem_blk(lambda i, be, nu: (jnp.minimum(i + 1, n_blk - 1), 0, 0)),
            pl.BlockSpec(memory_space=pl.ANY),
            pl.BlockSpec((_MOE_ROWS, 1), lambda i, be, nu: (i, 0)),
            wspec(w_gate), wspec(w_up), wspec(w_down),
        ],
        out_specs=pl.BlockSpec(memory_space=pl.ANY),
        scratch_shapes=[
            pltpu.VMEM((2, _MOE_ROWS, d), F32),
            pltpu.VMEM((2, _MOE_ROWS, d), F32),
            pltpu.SemaphoreType.DMA((2,)),
            pltpu.SemaphoreType.DMA((2,)),
        ],
    )
    return pl.pallas_call(
        functools.partial(_expert_kernel, n_rows=n_rows, n_live=n_live),
        grid_spec=grid_spec,
        out_shape=jax.ShapeDtypeStruct((TOP_K * n_rows, d), F32),
        compiler_params=pltpu.CompilerParams(dimension_semantics=("arbitrary",), vmem_limit_bytes=56 * MIB,
                                             disable_bounds_checks=True),
        name="moe_experts",
    )(blk_e, n_used, dst3, src3, src3, h2_all, rows_w, w_gate, w_up, w_down)


def _combine_kernel(x_ref, y0_ref, y1_ref, g_ref, o_ref):
    x = x_ref[...] + (y0_ref[...] + y1_ref[...])
    ms = jnp.mean(x * x, axis=-1, keepdims=True)
    o_ref[...] = x * lax.rsqrt(ms + EPS) * g_ref[...]


def _combine(x1, yk, gfin, row0, tm):
    n, d = x1.shape
    yk3 = yk.reshape(TOP_K, -1, d)
    blk0 = row0 // tm
    return pl.pallas_call(
        _combine_kernel,
        grid=(n // tm,),
        in_specs=[
            pl.BlockSpec((tm, d), lambda i: (i, 0)),
            pl.BlockSpec((None, tm, d), lambda i: (0, blk0 + i, 0)),
            pl.BlockSpec((None, tm, d), lambda i: (1, blk0 + i, 0)),
            pl.BlockSpec((1, d), lambda i: (0, 0)),
        ],
        out_specs=pl.BlockSpec((tm, d), lambda i: (i, 0)),
        out_shape=jax.ShapeDtypeStruct((n, d), F32),
        compiler_params=_params(("parallel",)),
        name="moe_combine",
    )(x1, yk3, yk3, gfin)


def _dispatch(eid, w, counts, n_rows):
    a = eid.shape[0]
    tb = _MOE_ROWS
    order = jnp.argsort(eid).astype(I32)
    padded = (counts + tb - 1) // tb * tb
    seg_end = jnp.cumsum(padded)
    seg_start = seg_end - padded
    start = jnp.cumsum(counts) - counts
    n_blk = (a + N_EXPERTS * (tb - 1) + tb - 1) // tb
    blk_e = jnp.minimum(jnp.sum((seg_end[None, :] <= (jnp.arange(n_blk) * tb)[:, None]).astype(I32), axis=1),
                        N_EXPERTS - 1)
    blk = jnp.arange(n_blk)[:, None]
    r = jnp.arange(tb)[None, :]
    off = (blk * tb - seg_start[blk_e][:, None]) + r
    valid = off < counts[blk_e][:, None]
    asg = order[jnp.clip(start[blk_e][:, None] + off, 0, a - 1)]
    rows_src = jnp.where(valid, asg // TOP_K, 0).astype(I32)
    rows_dst = jnp.where(valid, (asg % TOP_K) * n_rows + asg // TOP_K, (blk % 2) * n_rows + (n_rows - tb) + r)
    rows_w = jnp.where(valid, w[asg], 0.0)
    n_used = (seg_end[-1] // tb).astype(I32).reshape(1)
    return rows_src, rows_dst.astype(I32), rows_w.reshape(-1, 1), blk_e.astype(I32), n_used


def _overlap(n_chunk, n_sel, n_sel_pad):
    cs = jnp.arange(n_chunk)[:, None] * CMP_STRIDE
    ss = jnp.arange(n_sel_pad)[None, :] * SEL_LEN
    hit = (cs < ss + SEL_LEN) & (cs + CMP_LEN > ss) & (jnp.arange(n_sel_pad)[None, :] < n_sel)
    return hit.astype(BF16)


def kernel(x_prompt, x_sample, cache_cmp_kv, cache_sel_kv, state_win_kv, state_conv, page_table, rel_bias,
           norm_mix, w_in, b_in, conv_w, conv_b, conv_ln_g, conv_ln_b, w_conv_out, b_conv_out,
           cmp_pe_k, cmp_w1_k, cmp_w2_k, cmp_pe_v, cmp_w1_v, cmp_w2_v, w_nsa_out, w_out, norm_ffn,
           w_rg, b_rg, w_re, b_re, w_gate, w_up, w_down, norm_final):
    batch, seq, _ = x_prompt.shape
    bs = x_sample.shape[0]
    n_pool = cache_cmp_kv.shape[1]
    n_pages = page_table.shape[1]
    past = n_pages * PAGE
    n_tok = batch * seq
    assert bs + _MOE_ROWS <= _MOE_SLACK and n_tok % bs == 0 and seq % _TK == 0 and bs % 16 == 0
    l = 0

    wt = w_in[l].T.astype(BF16)
    bias_in = b_in[l]
    cuts = [0, D_CONV, 2 * D_CONV, 2 * D_CONV + D_Q, 2 * D_CONV + D_Q + 3 * D_KV]
    cuts += [cuts[-1] + 3 * N_HEADS, cuts[-1] + 3 * N_HEADS + D_MODEL, cuts[-1] + 3 * N_HEADS + 2 * D_MODEL]
    seg = lambda k: (wt[cuts[k]:cuts[k + 1]], bias_in[cuts[k]:cuts[k + 1]][None, :])
    (wa, ba), (wb, bb), (wq, bq), (wkv, bkv), (wzg, bzg), (wga, bga), (wgb, bgb) = [seg(k) for k in range(7)]
    wzg = jnp.pad(wzg, ((0, LANES - 3 * N_HEADS), (0, 0)))
    bzg = jnp.pad(bzg, ((0, 0), (0, LANES - 3 * N_HEADS)))
    wc = w_conv_out[l].astype(BF16)
    wn = w_nsa_out[l].astype(BF16)
    wo = w_out[l].astype(BF16)
    wr = jnp.zeros((LANES, D_MODEL), F32).at[0:N_GROUPS].set(w_rg[l].T).at[8:8 + N_EXPERTS].set(w_re[l].T)
    br = jnp.zeros((LANES,), F32).at[0:N_GROUPS].set(b_rg[l]).at[8:8 + N_EXPERTS].set(b_re[l].reshape(-1))
    wr, br = wr.astype(BF16), br[:, None]
    cw = _compress_weights(cmp_pe_k[l], cmp_w1_k[l], cmp_w2_k[l], cmp_pe_v[l], cmp_w1_v[l], cmp_w2_v[l])
    rb_t = rel_bias.T
    gmix, gffn, gfin = norm_mix[l][None, :], norm_ffn[l][None, :], norm_final[None, :]
    vec = lambda a: a[l][None, :]

    q_scale = HEAD_DIM ** -0.5
    xp = x_prompt.reshape(n_tok, D_MODEL)
    xs = x_sample.reshape(bs, D_MODEL)
    hp, up, qp, gates_p, *kv_p = _project_prompt(
        xp, gmix, wa, wb, wq, wkv, wzg, ba, bb, bq.reshape(-1, 1), bkv.reshape(-1, 1), bzg,
        q_scale * LOG2E, batch, seq, 512)
    hs = _rmsnorm(xs, gmix, bs)
    us = _glu(hs, wa, wb, ba, bb, bs, 512)
    qs = _proj(hs, wq, bq, lambda z: z * q_scale, BF16, bs, 512, "proj_q")
    gates_s = _proj(hs, wzg, bzg, _sigmoid, F32, bs, LANES, "proj_head_gates")
    kv_s = _proj(hs, wkv, bkv, lambda z: z, F32, bs, 512, "proj_kv_sample")

    conv_args = (conv_w[l], vec(conv_b), vec(conv_ln_g), vec(conv_ln_b))
    cp = _conv_prompt(up, *conv_args, batch, seq, 256)
    cs, new_conv_s = _conv_sample(state_conv[l].transpose(1, 0, 2), us, *conv_args)

    nqt = seq // QT
    n_chunk_p = seq // CMP_STRIDE
    n_chunk_s = n_pages * _CHUNKS_PER_PAGE
    cmap, wmap, smap = _bias_prompt(rb_t, nqt, n_chunk_p)
    cbias_s, wbias_s, sbias_s = _bias_sample(rb_t, past, n_chunk_s, n_pages)

    ident = jnp.zeros((1,), I32)
    kcmp_p = _compress(kv_p[0], ident, cw, batch, seq // PAGE, seq // PAGE, False)
    feat_major = lambda a: a.transpose(0, 2, 3, 4, 1)
    pool_cmp = feat_major(cache_cmp_kv[l]).reshape(n_pool, D_KV, PAGE)
    pp_s = min(32, n_pages)
    kcmp_s = _compress(pool_cmp, page_table.reshape(-1), cw, bs, n_pages, pp_s, True)

    n_sel_p = seq // SEL_LEN
    o_cmp_p, selbias = _cmp_prompt(qp, kcmp_p, cmap, gates_p, _overlap(n_chunk_p, n_sel_p, n_sel_p).T,
                                   batch, seq)
    n_sel_s = -(-(past + 1) // SEL_LEN)
    n_sel_pad = -(-n_sel_s // LANES) * LANES
    top_s = min(SEL_TOP, n_sel_s)
    head_gate = lambda k: jnp.broadcast_to(
        gates_s[:, k * N_HEADS:(k + 1) * N_HEADS, None], (bs, N_HEADS, HEAD_DIM))
    qs3 = qs.reshape(bs, N_HEADS, HEAD_DIM)
    o_cmp_s, idx_s = _cmp_sample(qs3, kcmp_s, cbias_s, _overlap(n_chunk_s, n_sel_s, n_sel_pad), head_gate(0),
                                 n_sel_s, past)

    o_sel_p = _sel_prompt(qp, kv_p[1], selbias, smap, gates_p, batch, seq)
    by_group = lambda a: jnp.pad(a.reshape(bs, N_KV, HPG, -1), ((0, 0), (0, 0), (0, 8 - HPG), (0, 0)))
    q4 = by_group(qs3)
    blk = idx_s[:, :top_s].reshape(-1)
    phys = jnp.take_along_axis(
        page_table, jnp.minimum(blk // 2, n_pages - 1).reshape(bs, -1), axis=1).reshape(-1)
    kv_new = lambda k: jnp.pad(
        kv_s[:, k * D_KV:(k + 1) * D_KV].reshape(bs, 2, N_KV, HEAD_DIM).transpose(0, 2, 1, 3),
        ((0, 0), (0, 0), (0, 6), (0, 0)))
    bias_by_group = lambda a: jnp.pad(
        jnp.moveaxis(a, -2, 0).reshape((N_KV, HPG) + a.shape[:-2] + a.shape[-1:]),
        ((0, 0), (0, 8 - HPG)) + ((0, 0),) * (a.ndim - 1))
    sb4 = jnp.moveaxis(bias_by_group(sbias_s), 1, 2)
    pool_sel = feat_major(cache_sel_kv[l])
    o_sel_s = _sel_sample(blk, phys, q4, kv_new(1), sb4, by_group(head_gate(1)), pool_sel, top_s, n_pages)

    o_win_p = _win_prompt(qp, kv_p[2], wmap, gates_p, batch, seq)
    win_t = feat_major(state_win_kv[l]).reshape(bs, D_KV, WINDOW)
    newcol = jnp.broadcast_to(kv_s[:, 2 * D_KV:3 * D_KV, None], (bs, D_KV, LANES))
    o_win_s, new_win_s = _win_sample(q4, kv_new(2), win_t, newcol, bias_by_group(wbias_s), by_group(head_gate(2)))

    flat = lambda w: w[l].reshape(-1, w.shape[-1])
    m_p, (wg_bf, wu_bf) = _merge_a(hp, cp, o_cmp_p, o_sel_p, o_win_p, wga, wgb, wc, wn, bga, bgb,
                                   vec(b_conv_out), 512, 512, riders=(flat(w_gate), flat(w_up)))
    (x1_p, h2_all, eid_p, rw_p, cnt_p), (wd_bf,) = _merge_b(
        xp, m_p, wo, gffn, wr, br, 256, slack=_MOE_SLACK, riders=(flat(w_down),))
    from_groups = lambda a: a[:, :, :HPG, :].reshape(bs, D_Q)
    m_s, _ = _merge_a(hs, cs, o_cmp_s.reshape(bs, D_Q), from_groups(o_sel_s), from_groups(o_win_s),
                      wga, wgb, wc, wn, bga, bgb, vec(b_conv_out), bs, 512)
    (x1_s, h2_all, eid_s, rw_s, cnt_s), _ = _merge_b(xs, m_s, wo, gffn, wr, br, bs, h2_all=h2_all, row0=n_tok)

    eid = jnp.concatenate([eid_p[:TOP_K].T, eid_s[:TOP_K].T], axis=0).reshape(-1)
    rw = jnp.concatenate([rw_p[:TOP_K].T, rw_s[:TOP_K].T], axis=0).reshape(-1)
    counts = jnp.sum(cnt_p[:, :, 0], axis=0) + jnp.sum(cnt_s[:, :, 0], axis=0)
    rows_src, rows_dst, rows_w, blk_e, n_used = _dispatch(eid, rw, counts, h2_all.shape[0])
    yk = _experts(blk_e, n_used, rows_src, rows_dst, rows_w, h2_all, n_tok + bs,
                  wg_bf.reshape(w_gate.shape[1:]), wu_bf.reshape(w_up.shape[1:]), wd_bf.reshape(w_down.shape[1:]))
    y_p = _combine(x1_p, yk, gfin, 0, 256)
    y_s = _combine(x1_s, yk, gfin, n_tok, bs)

    def cache_rows(a_t, rows):
        return a_t.reshape(a_t.shape[0], 2, N_KV, HEAD_DIM, rows).transpose(0, 4, 1, 2, 3)[None]

    new_row = lambda k: kv_s[:, k * D_KV:(k + 1) * D_KV].reshape(1, bs, 1, 2, N_KV, HEAD_DIM)
    hist = CONV_W - 1
    return (
        y_p.reshape(batch, seq, D_MODEL),
        y_s.reshape(bs, 1, D_MODEL),
        cache_rows(kv_p[0], seq),
        new_row(0),
        cache_rows(kv_p[1], seq),
        new_row(1),
        cache_rows(kv_p[2][:, :, seq - WINDOW:], WINDOW),
        cache_rows(new_win_s, WINDOW),
        up.reshape(batch, seq, D_CONV)[None, :, seq - hist:],
        new_conv_s.transpose(1, 0, 2)[None],
    )
```

```python
import functools
import math

import jax
import jax.numpy as jnp
from jax import lax
from jax.experimental import pallas as pl
from jax.experimental.pallas import tpu as pltpu

F32 = jnp.float32
BF16 = jnp.bfloat16
I32 = jnp.int32

D_MODEL = 2048
D_CONV = 1024
CONV_W = 31
N_HEADS = 16
HEAD_DIM = 64
N_KV = 4
HPG = N_HEADS // N_KV
D_Q = N_HEADS * HEAD_DIM
D_KV = 2 * N_KV * HEAD_DIM
CMP_LEN = 32
CMP_STRIDE = 16
CMP_HID = 64
SEL_LEN = 64
SEL_TOP = 16
WINDOW = 512
N_BUCKETS = 32
MAX_DIST = 128
N_GROUPS = 4
EXP_PER_GROUP = 8
N_EXPERTS = N_GROUPS * EXP_PER_GROUP
TOP_K = 2
D_EXPERT = 1024
EPS = 1e-6
NEG = -1e30
PAGE = 128
LANES = 128
QT = 128
GROUP_ROWS = HPG * QT
MIB = 1024 * 1024
LOG2E = math.log2(math.e)


def _params(sem, vmem_mib=48):
    return pltpu.CompilerParams(dimension_semantics=sem, vmem_limit_bytes=vmem_mib * MIB)


def _mm_nt(a, b):
    return lax.dot_general(a, b, (((1,), (1,)), ((), ())), preferred_element_type=F32)


def _mm(a, b):
    return jnp.dot(a, b, preferred_element_type=F32)


def _sigmoid(z):
    return 1.0 / (1.0 + jnp.exp(-z))


def _silu(z):
    return z * _sigmoid(z)


def _gelu_tanh(z):
    return 0.5 * z * (1.0 + jnp.tanh(math.sqrt(2.0 / math.pi) * (z + 0.044715 * (z * z * z))))


def _rmsnorm_kernel(x_ref, g_ref, o_ref):
    x = x_ref[...]
    ms = jnp.mean(x * x, axis=-1, keepdims=True)
    o_ref[...] = (x * lax.rsqrt(ms + EPS) * g_ref[...]).astype(o_ref.dtype)


def _rmsnorm(x, g, tm):
    n, d = x.shape
    return pl.pallas_call(
        _rmsnorm_kernel,
        grid=(n // tm,),
        in_specs=[pl.BlockSpec((tm, d), lambda i: (i, 0)), pl.BlockSpec((1, d), lambda i: (0, 0))],
        out_specs=pl.BlockSpec((tm, d), lambda i: (i, 0)),
        out_shape=jax.ShapeDtypeStruct((n, d), BF16),
        compiler_params=_params(("parallel",)),
        name="rmsnorm",
    )(x, g)


def _proj_kernel(h_ref, w_ref, b_ref, o_ref, *, act):
    z = _mm_nt(h_ref[...], w_ref[...]) + b_ref[...]
    o_ref[...] = act(z).astype(o_ref.dtype)


def _proj(h, wt, b, act, out_dtype, tm, tn, name):
    n, k = h.shape
    nout = wt.shape[0]
    return pl.pallas_call(
        functools.partial(_proj_kernel, act=act),
        grid=(n // tm, nout // tn),
        in_specs=[
            pl.BlockSpec((tm, k), lambda i, j: (i, 0)),
            pl.BlockSpec((tn, k), lambda i, j: (j, 0)),
            pl.BlockSpec((1, tn), lambda i, j: (0, j)),
        ],
        out_specs=pl.BlockSpec((tm, tn), lambda i, j: (i, j)),
        out_shape=jax.ShapeDtypeStruct((n, nout), out_dtype),
        compiler_params=_params(("parallel", "parallel")),
        name=name,
    )(h, wt, b)


def _proj_t_kernel(h_ref, w_ref, b_ref, o_ref, *, act):
    z = _mm_nt(w_ref[...], h_ref[...]) + b_ref[...]
    o_ref[...] = act(z).astype(o_ref.dtype)


def _proj_t(h, wt, bcol, act, out_dtype, tm, tn, name):
    n, k = h.shape
    nout = wt.shape[0]
    return pl.pallas_call(
        functools.partial(_proj_t_kernel, act=act),
        grid=(n // tm, nout // tn),
        in_specs=[
            pl.BlockSpec((tm, k), lambda i, j: (i, 0)),
            pl.BlockSpec((tn, k), lambda i, j: (j, 0)),
            pl.BlockSpec((tn, 1), lambda i, j: (j, 0)),
        ],
        out_specs=pl.BlockSpec((tn, tm), lambda i, j: (j, i)),
        out_shape=jax.ShapeDtypeStruct((nout, n), out_dtype),
        compiler_params=_params(("parallel", "parallel")),
        name=name,
    )(h, wt, bcol)


def _glu_kernel(h_ref, wa_ref, wb_ref, ba_ref, bb_ref, o_ref):
    h = h_ref[...]
    a = _mm_nt(h, wa_ref[...]) + ba_ref[...]
    b = _mm_nt(h, wb_ref[...]) + bb_ref[...]
    o_ref[...] = a * _sigmoid(b)


def _glu(h, wat, wbt, ba, bb, tm, tn):
    n, k = h.shape
    nout = wat.shape[0]
    wspec = pl.BlockSpec((tn, k), lambda i, j: (j, 0))
    bspec = pl.BlockSpec((1, tn), lambda i, j: (0, j))
    return pl.pallas_call(
        _glu_kernel,
        grid=(n // tm, nout // tn),
        in_specs=[pl.BlockSpec((tm, k), lambda i, j: (i, 0)), wspec, wspec, bspec, bspec],
        out_specs=pl.BlockSpec((tm, tn), lambda i, j: (i, j)),
        out_shape=jax.ShapeDtypeStruct((n, nout), F32),
        compiler_params=_params(("parallel", "parallel")),
        name="glu",
    )(h, wat, wbt, ba, bb)


def _kvt_kernel(h_ref, w_ref, b_ref, *o_refs):
    h = h_ref[...]
    for kind, o_ref in enumerate(o_refs):
        rows = pl.ds(kind * D_KV, D_KV)
        o_ref[0] = _mm_nt(w_ref[rows, :], h) + b_ref[rows, :]


def _kv_transposed(h, wt, bcol, batch, seq, tm):
    n, k = h.shape
    kinds = wt.shape[0] // D_KV
    nt = seq // tm
    return pl.pallas_call(
        _kvt_kernel,
        grid=(n // tm,),
        in_specs=[
            pl.BlockSpec((tm, k), lambda i: (i, 0)),
            pl.BlockSpec(wt.shape, lambda i: (0, 0)),
            pl.BlockSpec(bcol.shape, lambda i: (0, 0)),
        ],
        out_specs=[pl.BlockSpec((1, D_KV, tm), lambda i: (i // nt, 0, i % nt))] * kinds,
        out_shape=[jax.ShapeDtypeStruct((batch, D_KV, seq), F32)] * kinds,
        compiler_params=_params(("parallel",)),
        name="kv_transposed",
    )(h, wt, bcol)


def _project_prompt_kernel(x_ref, g_ref, wa_ref, wb_ref, wq_ref, wkv_ref, wzg_ref, ba_ref, bb_ref, bq_ref,
                           bkv_ref, bzg_ref, *rest, q_scale, n_riders, kinds):
    rider_in = rest[:n_riders]
    h_ref, u_ref, qt_ref, gates_ref = rest[n_riders:n_riders + 4]
    kv_refs = rest[n_riders + 4:n_riders + 4 + kinds]
    rider_out = rest[n_riders + 4 + kinds:]
    for src, dst in zip(rider_in, rider_out):
        dst[...] = src[...].astype(dst.dtype)
    x = x_ref[...]
    ms = jnp.mean(x * x, axis=-1, keepdims=True)
    h = (x * lax.rsqrt(ms + EPS) * g_ref[...]).astype(BF16)
    h_ref[...] = h
    a = _mm_nt(h, wa_ref[...]) + ba_ref[...]
    b = _mm_nt(h, wb_ref[...]) + bb_ref[...]
    u_ref[...] = a * _sigmoid(b)
    qt_ref[...] = ((_mm_nt(wq_ref[...], h) + bq_ref[...]) * q_scale).astype(qt_ref.dtype)
    gates_ref[...] = _sigmoid(_mm_nt(h, wzg_ref[...]) + bzg_ref[...])
    for kind, o_ref in enumerate(kv_refs):
        rows = pl.ds(kind * D_KV, D_KV)
        o_ref[0] = _mm_nt(wkv_ref[rows, :], h) + bkv_ref[rows, :]


def _project_prompt(x, g, wa, wb, wq, wkv, wzg, ba, bb, bq_col, bkv_col, bzg, q_scale, batch, seq, tm,
                    riders=()):
    n, d = x.shape
    kinds = wkv.shape[0] // D_KV
    nt = seq // tm
    tok = lambda width: pl.BlockSpec((tm, width), lambda i: (i, 0))
    const = lambda a: pl.BlockSpec(a.shape, lambda i: (0,) * a.ndim, pipeline_mode=pl.Buffered(1))
    consts = (g, wa, wb, wq, wkv, wzg, ba, bb, bq_col, bkv_col, bzg)
    ride = _rider_specs(riders, n // tm, lambda i: (i, 0))
    out = pl.pallas_call(
        functools.partial(_project_prompt_kernel, q_scale=q_scale, n_riders=len(riders), kinds=kinds),
        grid=(n // tm,),
        in_specs=[tok(d)] + [const(a) for a in consts] + ride,
        out_specs=[tok(d), tok(D_CONV), pl.BlockSpec((D_Q, tm), lambda i: (0, i)), tok(LANES)]
        + [pl.BlockSpec((1, D_KV, tm), lambda i: (i // nt, 0, i % nt))] * kinds + ride,
        out_shape=[
            jax.ShapeDtypeStruct((n, d), BF16),
            jax.ShapeDtypeStruct((n, D_CONV), F32),
            jax.ShapeDtypeStruct((D_Q, n), BF16),
            jax.ShapeDtypeStruct((n, LANES), F32),
        ] + [jax.ShapeDtypeStruct((batch, D_KV, seq), F32)] * kinds
        + [jax.ShapeDtypeStruct(a.shape, BF16) for a in riders],
        compiler_params=_params(("parallel",), 56),
        name="project_prompt",
    )(x, *consts, *riders)
    return out[:4 + kinds], out[4 + kinds:]


def _merge_a_kernel(h_ref, c_ref, o1_ref, o2_ref, o3_ref, wga_ref, wgb_ref, wc_ref, wn_ref,
                    bga_ref, bgb_ref, bc_ref, *rest):
    n_cast = (len(rest) - 1) // 2
    m_ref = rest[n_cast]
    h = h_ref[...]
    ga = _sigmoid(_mm_nt(h, wga_ref[...]) + bga_ref[...])
    gb = _sigmoid(_mm_nt(h, wgb_ref[...]) + bgb_ref[...])
    yc = _mm(c_ref[...], wc_ref[...]) + bc_ref[...]
    o = (o1_ref[...] + o2_ref[...] + o3_ref[...]).astype(BF16)
    yn = _mm(o, wn_ref[...])
    m_ref[...] = (ga * yc + gb * yn).astype(m_ref.dtype)
    for src, dst in zip(rest[:n_cast], rest[n_cast + 1:]):
        dst[...] = src[...].astype(dst.dtype)


def _rider_specs(riders, steps, index):
    return [pl.BlockSpec((a.shape[0] // steps, a.shape[1]), index) for a in riders]


def _merge_a(h, c, o1, o2, o3, wgat, wgbt, wc, wn, bga, bgb, bc, tm, tn, riders=()):
    n, k = h.shape
    nj = D_MODEL // tn
    tok = lambda width: pl.BlockSpec((tm, width), lambda i, j: (i, 0))
    wt_spec = pl.BlockSpec((tn, k), lambda i, j: (j, 0))
    w_spec = pl.BlockSpec((c.shape[1], tn), lambda i, j: (0, j))
    b_spec = pl.BlockSpec((1, tn), lambda i, j: (0, j))
    ride = _rider_specs(riders, (n // tm) * nj, lambda i, j: (i * nj + j, 0))
    out = pl.pallas_call(
        _merge_a_kernel,
        grid=(n // tm, nj),
        in_specs=[tok(k), tok(D_CONV), tok(D_Q), tok(D_Q), tok(D_Q), wt_spec, wt_spec, w_spec, w_spec,
                  b_spec, b_spec, b_spec] + ride,
        out_specs=[pl.BlockSpec((tm, tn), lambda i, j: (i, j))] + ride,
        out_shape=[jax.ShapeDtypeStruct((n, D_MODEL), BF16)]
        + [jax.ShapeDtypeStruct(a.shape, BF16) for a in riders],
        compiler_params=_params(("parallel", "parallel"), 56),
        name="merge_gates",
    )(h, c, o1, o2, o3, wgat, wgbt, wc, wn, bga, bgb, bc, *riders)
    return out[0], out[1:]


def _merge_b_kernel(x_ref, m_ref, wo_ref, gn_ref, wr_ref, br_ref, *rest, n_real, n_shared, n_riders):
    rider_in = rest[n_shared:n_shared + n_riders]
    outs = rest[n_shared + n_riders:n_shared + n_riders + 5]
    rider_out = rest[n_shared + n_riders + 5:]
    h2_ref = outs[1]

    @pl.when(pl.program_id(0) < n_real)
    def _():
        _merge_b_tile(x_ref, m_ref, wo_ref, gn_ref, wr_ref, br_ref, *outs)
        for src, dst in zip(rider_in, rider_out):
            dst[...] = src[...].astype(dst.dtype)

    @pl.when(pl.program_id(0) >= n_real)
    def _():
        h2_ref[...] = jnp.zeros(h2_ref.shape, h2_ref.dtype)


def _merge_b_tile(x_ref, m_ref, wo_ref, gn_ref, wr_ref, br_ref, x1_ref, h2_ref, eid_ref, rw_ref, cnt_ref):
    x1 = x_ref[...] + _mm(m_ref[...], wo_ref[...])
    x1_ref[...] = x1
    ms = jnp.mean(x1 * x1, axis=-1, keepdims=True)
    h2 = x1 * lax.rsqrt(ms + EPS) * gn_ref[...]
    h2_ref[...] = h2
    logits = _mm_nt(wr_ref[...], h2.astype(BF16)) + br_ref[...]
    tm = logits.shape[1]
    row = lax.broadcasted_iota(I32, (8, tm), 0)
    lg = jnp.where(row < N_GROUPS, logits[0:8], -jnp.inf)
    gmax = jnp.max(lg, axis=0, keepdims=True)
    gidx = jnp.min(jnp.where(lg == gmax, row, 8), axis=0, keepdims=True)
    wg = 1.0 / jnp.sum(jnp.exp(lg - gmax), axis=0, keepdims=True)
    le = jnp.zeros((8, tm), F32)
    for g in range(N_GROUPS):
        le = jnp.where(gidx == g, logits[8 + 8 * g:16 + 8 * g], le)
    ee = jnp.exp(le - jnp.max(le, axis=0, keepdims=True))
    pz = ee / jnp.sum(ee, axis=0, keepdims=True)
    p1 = jnp.max(pz, axis=0, keepdims=True)
    i1 = jnp.min(jnp.where(pz == p1, row, 8), axis=0, keepdims=True)
    pz2 = jnp.where(row == i1, -1.0, pz)
    p2 = jnp.max(pz2, axis=0, keepdims=True)
    i2 = jnp.min(jnp.where(pz2 == p2, row, 8), axis=0, keepdims=True)
    den = p1 + p2
    e1 = gidx * EXP_PER_GROUP + i1
    e2 = gidx * EXP_PER_GROUP + i2
    eid_ref[...] = jnp.where(row == 0, e1, jnp.where(row == 1, e2, 0))
    rw_ref[...] = jnp.where(row == 0, wg * p1 / den, jnp.where(row == 1, wg * p2 / den, 0.0))
    erow = lax.broadcasted_iota(I32, (N_EXPERTS, tm), 0)
    hits = (erow == e1).astype(F32) + (erow == e2).astype(F32)
    cnt_ref[...] = jnp.broadcast_to(jnp.sum(hits, axis=1, keepdims=True), (N_EXPERTS, LANES)).astype(I32)


def _merge_b(x, m, wo, gn, wr, br, tm, h2_all=None, row0=0, slack=0, riders=()):
    n, d = x.shape
    blk0 = row0 // tm
    n_real = n // tm
    real = lambda i: jnp.minimum(i, n_real - 1)
    shared = [] if h2_all is None else [h2_all]
    h2_rows = n + slack if h2_all is None else h2_all.shape[0]
    ride = _rider_specs(riders, n_real, lambda i: (real(i), 0))
    out = pl.pallas_call(
        functools.partial(_merge_b_kernel, n_real=n_real, n_shared=len(shared), n_riders=len(riders)),
        grid=((n + slack) // tm,),
        in_specs=[
            pl.BlockSpec((tm, d), lambda i: (real(i), 0)),
            pl.BlockSpec((tm, d), lambda i: (real(i), 0)),
            pl.BlockSpec((d, d), lambda i: (0, 0)),
            pl.BlockSpec((1, d), lambda i: (0, 0)),
            pl.BlockSpec((LANES, d), lambda i: (0, 0)),
            pl.BlockSpec((LANES, 1), lambda i: (0, 0)),
        ] + [pl.BlockSpec(memory_space=pl.ANY)] * len(shared) + ride,
        out_specs=[
            pl.BlockSpec((tm, d), lambda i: (real(i), 0)),
            pl.BlockSpec((tm, d), lambda i: (blk0 + i, 0)),
            pl.BlockSpec((8, tm), lambda i: (0, real(i))),
            pl.BlockSpec((8, tm), lambda i: (0, real(i))),
            pl.BlockSpec((None, N_EXPERTS, LANES), lambda i: (real(i), 0, 0)),
        ] + ride,
        out_shape=[
            jax.ShapeDtypeStruct((n, d), F32),
            jax.ShapeDtypeStruct((h2_rows, d), F32),
            jax.ShapeDtypeStruct((8, n), I32),
            jax.ShapeDtypeStruct((8, n), F32),
            jax.ShapeDtypeStruct((n_real, N_EXPERTS, LANES), I32),
        ] + [jax.ShapeDtypeStruct(a.shape, BF16) for a in riders],
        input_output_aliases={6: 1} if shared else {},
        compiler_params=_params(("arbitrary",), 56),
        name="merge_out_router",
    )(x, m, wo, gn, wr, br, *shared, *riders)
    return out[:5], out[5:]


_HALO = 32


def _conv_prompt_kernel(u_ref, prev_ref, w_ref, b_ref, g_ref, beta_ref, o_ref, buf_ref, acc_ref, sh_ref, *, tt):
    i = pl.program_id(1)
    prev = prev_ref[pl.ds(tt - _HALO, _HALO), :]
    buf_ref[pl.ds(0, _HALO), :] = jnp.where(i > 0, prev, 0.0)
    buf_ref[pl.ds(_HALO, tt), :] = u_ref[...]
    off = _HALO - (CONV_W - 1)
    sub = 8
    rows = tt
    for cc in range(D_CONV // LANES):
        cols = pl.ds(cc * LANES, LANES)
        for r0 in range(0, tt, rows):
            acc = jnp.zeros((rows, LANES), F32)
            for r in range(sub):
                taps = [k for k in range(CONV_W) if (off + k) % sub == r]
                n_shift = (rows + _HALO - r) // sub * sub
                sh_ref[r, pl.ds(0, n_shift), :] = buf_ref[pl.ds(r0 + r, n_shift), cols]
                for k in taps:
                    acc = acc + sh_ref[r, pl.ds(off + k - r, rows), :] * w_ref[pl.ds(k, 1), cols]
            acc_ref[pl.ds(r0, rows), cols] = acc + b_ref[:, cols]
    c = acc_ref[...]
    mu = jnp.mean(c, axis=-1, keepdims=True)
    var = jnp.mean(jnp.square(c - mu), axis=-1, keepdims=True)
    y = (c - mu) * lax.rsqrt(var + EPS) * g_ref[...] + beta_ref[...]
    o_ref[...] = _silu(y).astype(o_ref.dtype)


def _conv_prompt(u, w, b, g, beta, batch, seq, tt):
    nt = seq // tt
    vec = pl.BlockSpec((1, D_CONV), lambda bi, i: (0, 0))
    return pl.pallas_call(
        functools.partial(_conv_prompt_kernel, tt=tt),
        grid=(batch, nt),
        in_specs=[
            pl.BlockSpec((tt, D_CONV), lambda bi, i: (bi * nt + i, 0)),
            pl.BlockSpec((tt, D_CONV), lambda bi, i: (bi * nt + jnp.maximum(i - 1, 0), 0)),
            pl.BlockSpec((CONV_W, D_CONV), lambda bi, i: (0, 0)),
            vec, vec, vec,
        ],
        out_specs=pl.BlockSpec((tt, D_CONV), lambda bi, i: (bi * nt + i, 0)),
        out_shape=jax.ShapeDtypeStruct((batch * seq, D_CONV), BF16),
        scratch_shapes=[pltpu.VMEM((tt + _HALO, D_CONV), F32), pltpu.VMEM((tt, D_CONV), F32),
                        pltpu.VMEM((8, tt + _HALO, LANES), F32)],
        compiler_params=_params(("parallel", "parallel")),
        name="conv_prompt",
    )(u, u, w, b, g, beta)


def _conv_sample_kernel(st_ref, u_ref, w_ref, b_ref, g_ref, beta_ref, c_ref, new_ref):
    hist = CONV_W - 1
    u = u_ref[...]
    acc = u * w_ref[pl.ds(hist, 1), :] + b_ref[...]
    for t in range(hist):
        acc = acc + st_ref[t] * w_ref[pl.ds(t, 1), :]
    mu = jnp.mean(acc, axis=-1, keepdims=True)
    var = jnp.mean(jnp.square(acc - mu), axis=-1, keepdims=True)
    y = (acc - mu) * lax.rsqrt(var + EPS) * g_ref[...] + beta_ref[...]
    c_ref[...] = _silu(y).astype(c_ref.dtype)
    for t in range(hist - 1):
        new_ref[t] = st_ref[t + 1]
    new_ref[hist - 1] = u


def _conv_sample(state_t, u, w, b, g, beta):
    hist, bs, _ = state_t.shape
    return pl.pallas_call(
        _conv_sample_kernel,
        out_shape=[jax.ShapeDtypeStruct((bs, D_CONV), BF16), jax.ShapeDtypeStruct((hist, bs, D_CONV), F32)],
        compiler_params=pltpu.CompilerParams(vmem_limit_bytes=48 * MIB),
        name="conv_sample",
    )(state_t, u, w, b, g, beta)


def _bias_lookup(rb_ref, head, dist, valid):
    n = jnp.maximum(dist, 0)
    max_exact = N_BUCKETS // 2
    nf = jnp.maximum(n, 1).astype(F32)
    large = max_exact + (jnp.log(nf / max_exact) / math.log(MAX_DIST / max_exact)
                         * (N_BUCKETS - max_exact)).astype(I32)
    bucket = jnp.where(n < max_exact, n, jnp.minimum(large, N_BUCKETS - 1))
    val = jnp.zeros(dist.shape, F32)
    for k in range(N_BUCKETS):
        val = jnp.where(bucket == k, rb_ref[head, k], val)
    return jnp.where(valid, val, NEG)


def _bias_prompt_kernel(rb_ref, cmp_ref, win_ref, sel_ref, *, nqt, n_chunk):
    g = pl.program_id(0)
    key = lax.broadcasted_iota(I32, (LANES, QT), 0)
    qry = lax.broadcasted_iota(I32, (LANES, QT), 1)
    for hh in range(HPG):
        head = g * HPG + hh
        cols = pl.ds(hh * QT, QT)
        lookup = lambda dist, valid: _bias_lookup(rb_ref, head, dist, valid) * LOG2E

        for eb in range(_cmp_map_rows(nqt, n_chunk) // LANES):
            block = key + eb * LANES - _cmp_map_shift(nqt, 0)
            dist = qry - block * CMP_STRIDE - (CMP_LEN - 1)
            cmp_ref[0, pl.ds(eb * LANES, LANES), cols] = lookup(dist, dist >= 0)
        for jb in range((WINDOW + QT) // LANES):
            dist = qry - (key + jb * LANES) + WINDOW
            win_ref[0, pl.ds(jb * LANES, LANES), cols] = lookup(dist, (dist >= 0) & (dist <= WINDOW))
        for kind in range(3):
            for jb in range(2):
                dist = kind * QT + qry - (key + jb * LANES)
                sel_ref[0, kind, pl.ds(jb * LANES, LANES), cols] = lookup(dist, dist >= 0)
        far = jnp.full((LANES, QT), 2 * MAX_DIST, I32)
        for jb in range(2):
            sel_ref[0, 3, pl.ds(jb * LANES, LANES), cols] = lookup(far, far >= 0)


def _cmp_map_shift(nqt, qt):
    return (nqt - 1 - qt) * (QT // CMP_STRIDE)


def _cmp_map_rows(nqt, n_chunk):
    return -(-(n_chunk + _cmp_map_shift(nqt, 0)) // LANES) * LANES


def _bias_prompt(rb_t, nqt, n_chunk):
    ext = _cmp_map_rows(nqt, n_chunk)
    return pl.pallas_call(
        functools.partial(_bias_prompt_kernel, nqt=nqt, n_chunk=n_chunk),
        grid=(N_KV,),
        in_specs=[pl.BlockSpec(memory_space=pltpu.SMEM)],
        out_specs=[
            pl.BlockSpec((1, ext, GROUP_ROWS), lambda g: (g, 0, 0)),
            pl.BlockSpec((1, WINDOW + QT, GROUP_ROWS), lambda g: (g, 0, 0)),
            pl.BlockSpec((1, 4, 2 * LANES, GROUP_ROWS), lambda g: (g, 0, 0, 0)),
        ],
        out_shape=[
            jax.ShapeDtypeStruct((N_KV, ext, GROUP_ROWS), F32),
            jax.ShapeDtypeStruct((N_KV, WINDOW + QT, GROUP_ROWS), F32),
            jax.ShapeDtypeStruct((N_KV, 4, 2 * LANES, GROUP_ROWS), F32),
        ],
        compiler_params=_params(("parallel",)),
        name="bias_maps_prompt",
    )(rb_t)


def _bias_sample_kernel(rb_ref, cmp_ref, win_ref, sel_ref, *, past, n_chunk, n_pages):
    col = lax.broadcasted_iota(I32, (N_HEADS, LANES), 1)
    rb = rb_ref[...]

    def lookup(dist):
        n = jnp.maximum(dist, 0)
        max_exact = N_BUCKETS // 2
        nf = jnp.maximum(n, 1).astype(F32)
        large = max_exact + (jnp.log(nf / max_exact) / math.log(MAX_DIST / max_exact)
                             * (N_BUCKETS - max_exact)).astype(I32)
        bucket = jnp.where(n < max_exact, n, jnp.minimum(large, N_BUCKETS - 1))
        val = jnp.zeros(dist.shape, F32)
        for k in range(N_BUCKETS):
            val = jnp.where(bucket == k, rb[:, k:k + 1], val)
        return jnp.where(dist >= 0, val, NEG)

    def cmp_block(cb, carry):
        dist = past - (col + cb * LANES) * CMP_STRIDE - (CMP_LEN - 1)
        cmp_ref[cb] = lookup(dist)
        return carry

    lax.fori_loop(0, n_chunk // LANES, cmp_block, 0)

    def win_block(jb, carry):
        win_ref[jb] = lookup(WINDOW - (col + jb * LANES))
        return carry

    lax.fori_loop(0, WINDOW // LANES + 1, win_block, 0)

    def sel_page(p, carry):
        sel_ref[p] = lookup(past - (p * PAGE + col))
        return carry

    lax.fori_loop(0, n_pages + 1, sel_page, 0)


def _bias_sample(rb_t, past, n_chunk, n_pages):
    cmp_b, win_b, sel_b = pl.pallas_call(
        functools.partial(_bias_sample_kernel, past=past, n_chunk=n_chunk, n_pages=n_pages),
        out_shape=[
            jax.ShapeDtypeStruct((n_chunk // LANES, N_HEADS, LANES), F32),
            jax.ShapeDtypeStruct((WINDOW // LANES + 1, N_HEADS, LANES), F32),
            jax.ShapeDtypeStruct((n_pages + 1, N_HEADS, LANES), F32),
        ],
        name="bias_maps_sample",
    )(rb_t)
    flat = lambda a: a.transpose(1, 0, 2).reshape(N_HEADS, -1)
    return flat(cmp_b), flat(win_b), sel_b


_CHUNKS_PER_PAGE = PAGE // CMP_STRIDE
_CHUNK_PITCH = 24


def _compress_kernel(pt_ref, *refs, pp):
    page_refs = refs[:pp + 1]
    w1k_ref, w1v_ref, pe_ref, w2k_ref, w2v_ref, o_ref, tr_ref = refs[pp + 1:]
    del pt_ref
    nch = pp * _CHUNKS_PER_PAGE
    m = nch + 8
    n_slab = D_KV // LANES

    def transpose(s):
        for k in range(pp + 1):
            t = page_refs[k][pl.ds(s * LANES, LANES), :].T
            for n in range(_CHUNKS_PER_PAGE):
                row = (k * _CHUNKS_PER_PAGE + n) * _CHUNK_PITCH
                tr_ref[s, pl.ds(row, CMP_STRIDE), :] = t[n * CMP_STRIDE:(n + 1) * CMP_STRIDE]

    def chunks(s, _):
        return jnp.concatenate(
            [tr_ref.at[s][pl.ds(p, m, stride=_CHUNK_PITCH), :] for p in range(CMP_STRIDE)], axis=1)

    def mlp(s, x):
        is_v = int(s >= n_slab // 2)
        w1 = (w1v_ref if is_v else w1k_ref)[...]
        w2 = (w2v_ref if is_v else w2k_ref)[...]
        y = _mm(x.astype(BF16), w1)
        y_pe = _mm(pe_ref[...].astype(BF16), w1)
        pe_term = y_pe[2 * is_v:2 * is_v + 1, 0:LANES] + y_pe[2 * is_v + 1:2 * is_v + 2, LANES:2 * LANES]
        pre = y[0:nch, 0:LANES] + y[1:nch + 1, LANES:2 * LANES] + pe_term
        o_ref[0, :, pl.ds(s * LANES, LANES)] = _mm(_gelu_tanh(pre).astype(BF16), w2)

    _staggered(n_slab, transpose, chunks, mlp)


def _compress(pages, page_ids, weights, nb, n_pages, pp, paged):
    steps = n_pages // pp
    nch = pp * _CHUNKS_PER_PAGE

    def page_spec(k):
        if paged:
            return pl.BlockSpec(
                (None, D_KV, PAGE),
                lambda b, s, pt: (pt[b * n_pages + jnp.minimum(s * pp + k, n_pages - 1)], 0, 0))
        return pl.BlockSpec((None, D_KV, PAGE), lambda b, s, pt: (b, 0, jnp.minimum(s * pp + k, n_pages - 1)))

    full = lambda a: pl.BlockSpec(a.shape, lambda b, s, pt: (0,) * a.ndim)
    grid_spec = pltpu.PrefetchScalarGridSpec(
        num_scalar_prefetch=1,
        grid=(nb, steps),
        in_specs=[page_spec(k) for k in range(pp + 1)] + [full(a) for a in weights],
        out_specs=pl.BlockSpec((1, nch, D_KV), lambda b, s, pt: (b, s, 0)),
        scratch_shapes=[pltpu.VMEM((D_KV // LANES, (pp + 1) * _CHUNKS_PER_PAGE * _CHUNK_PITCH, LANES), F32)],
    )
    return pl.pallas_call(
        functools.partial(_compress_kernel, pp=pp),
        grid_spec=grid_spec,
        out_shape=jax.ShapeDtypeStruct((nb, n_pages * _CHUNKS_PER_PAGE, D_KV), F32),
        compiler_params=_params(("parallel", "parallel"), 56),
        name="compress_paged" if paged else "compress_prompt",
    )(page_ids, *([pages] * (pp + 1)), *weights)


def _compress_weights(pe_k, w1_k, w2_k, pe_v, w1_v, w2_v):
    eye = jnp.eye(2, dtype=F32)

    def first(w1, j):
        w = w1[j * CMP_STRIDE:(j + 1) * CMP_STRIDE]
        return jnp.einsum("pdf,gh->pgdhf", w, eye).reshape(CMP_STRIDE * LANES, 2 * CMP_HID).astype(BF16)

    def pos(pe, j):
        return jnp.tile(pe[j * CMP_STRIDE:(j + 1) * CMP_STRIDE, None, :], (1, 2, 1)).reshape(1, CMP_STRIDE * LANES)

    def second(w2):
        return jnp.einsum("fd,gh->gfhd", w2, eye).reshape(2 * CMP_HID, 2 * HEAD_DIM).astype(BF16)

    pe = jnp.concatenate([pos(pe_k, 0), pos(pe_k, 1), pos(pe_v, 0), pos(pe_v, 1)], axis=0)
    pe = jnp.pad(pe, ((0, 8 - pe.shape[0]), (0, 0)))
    both = lambda w1: jnp.concatenate([first(w1, 0), first(w1, 1)], axis=1)
    return (both(w1_k), both(w1_v), pe, second(w2_k), second(w2_v))


def _staggered(n, scores, softmax, values, ahead=1, lag=0):
    s = {g: scores(g) for g in range(min(ahead, n))}
    p = {}
    out = []
    for g in range(n + lag):
        if g + ahead < n:
            s[g + ahead] = scores(g + ahead)
        if g < n:
            p[g] = softmax(g, s.pop(g))
        if g - lag >= 0:
            out.append(values(g - lag, p.pop(g - lag)))
    return out


def _group_queries(qt_ref, g):
    return jnp.concatenate(
        [qt_ref[pl.ds((g * HPG + hh) * HEAD_DIM, HEAD_DIM), :] for hh in range(HPG)], axis=1)


def _store_heads(o_ref, o_t, gate, g, branch):
    lane = lax.broadcasted_iota(I32, (QT, 2 * HEAD_DIM), 1)
    for pair in range(HPG // 2):
        h = g * HPG + 2 * pair
        col = branch * N_HEADS + h
        both = jnp.concatenate([o_t[:, (2 * pair) * QT:(2 * pair + 1) * QT],
                                o_t[:, (2 * pair + 1) * QT:(2 * pair + 2) * QT]], axis=0)
        gates = jnp.where(lane < HEAD_DIM, gate[:, col:col + 1], gate[:, col + 1:col + 2])
        o_ref[:, pl.ds(h * HEAD_DIM, 2 * HEAD_DIM)] = both.T * gates


def _cmp_prompt_kernel(qt_ref, kc_ref, map_ref, gate_ref, ov_ref, o_ref, sb_ref, *, n_sel, top):
    qt = pl.program_id(1)
    kv = kc_ref[0]
    gate = gate_ref[...]
    ov = ov_ref[...]
    half = D_KV // 2
    n_chunk = kv.shape[0]
    map_rows = pl.ds(pl.multiple_of(_cmp_map_shift(pl.num_programs(1), qt), 8), n_chunk)
    bias = lambda g: map_ref[g, map_rows, :]
    vt_all = kv[:, half:].T
    jrow = lax.broadcasted_iota(I32, (n_sel, QT), 0)
    tpos = qt * QT + lax.broadcasted_iota(I32, (n_sel, QT), 1)
    cur = tpos // SEL_LEN
    forced = (jrow == 0) | (jrow == cur) | (jrow == cur - 1)
    valid = jrow * SEL_LEN <= tpos
    def scores(g):
        kg = kv[:, g * HEAD_DIM:(g + 1) * HEAD_DIM].astype(BF16)
        return _mm(kg, _group_queries(qt_ref, g)) + bias(g)

    def softmax(g, st):
        e = jnp.exp2(st - jnp.max(st, axis=0, keepdims=True))
        p = e / jnp.sum(e, axis=0, keepdims=True) * (bias(g) > 0.5 * NEG).astype(F32)
        return p.astype(BF16)

    def values(g, pb):
        vgt = vt_all[g * HEAD_DIM:(g + 1) * HEAD_DIM].astype(BF16)
        _store_heads(o_ref, _mm(vgt, pb), gate, g, 0)
        return pb

    probs = _staggered(N_KV, scores, softmax, values, ahead=2, lag=1)
    for g in range(N_KV):
        pb = probs[g]
        imp = jnp.zeros((n_sel, QT), F32)
        for hh in range(HPG):
            imp = imp + _mm(ov, pb[:, hh * QT:(hh + 1) * QT])
        score = jnp.where(forced, jnp.inf, jnp.where(valid, imp, -jnp.inf))
        rank = jnp.zeros((n_sel, QT), I32)
        for i in range(n_sel):
            si = score[i:i + 1, :]
            ahead = (si > score) | ((si == score) & (jrow > i))
            rank = rank + ahead.astype(I32)
        sb_ref[pl.ds(g * n_sel, n_sel), :] = jnp.where(rank < top, 0.0, NEG).astype(sb_ref.dtype)


def _cmp_prompt(q_t, kcmp, cmap, gates, ov_t, batch, seq):
    nqt = seq // QT
    n_chunk = kcmp.shape[1]
    n_sel = ov_t.shape[0]
    top = min(SEL_TOP, n_sel)
    return pl.pallas_call(
        functools.partial(_cmp_prompt_kernel, n_sel=n_sel, top=top),
        grid=(batch, nqt),
        in_specs=[
            pl.BlockSpec((D_Q, QT), lambda b, i: (0, b * nqt + i)),
            pl.BlockSpec((1, n_chunk, D_KV), lambda b, i: (b, 0, 0)),
            pl.BlockSpec(cmap.shape, lambda b, i: (0, 0, 0)),
            pl.BlockSpec((QT, LANES), lambda b, i: (b * nqt + i, 0)),
            pl.BlockSpec(ov_t.shape, lambda b, i: (0, 0)),
        ],
        out_specs=[
            pl.BlockSpec((QT, D_Q), lambda b, i: (b * nqt + i, 0)),
            pl.BlockSpec((None, N_KV * n_sel, QT), lambda b, i: (b * nqt + i, 0, 0)),
        ],
        out_shape=[
            jax.ShapeDtypeStruct((batch * seq, D_Q), F32),
            jax.ShapeDtypeStruct((batch * nqt, N_KV * n_sel, QT), BF16),
        ],
        compiler_params=_params(("parallel", "parallel")),
        name="cmp_attn_prompt",
    )(q_t, kcmp, cmap, gates, ov_t)


_TK = 2 * LANES


def _sel_prompt_kernel(qt_ref, kv_ref, sb_ref, map_ref, gate_ref, o_ref, ka_ref, vt_ref, qa_ref, *, n_sel, seq):
    qt = pl.program_id(1)
    aug = 2 * HEAD_DIM
    half = D_KV // 2

    @pl.when(qt == 0)
    def _():
        pos = lax.broadcasted_iota(I32, (seq, aug - HEAD_DIM), 0)
        blk = lax.broadcasted_iota(I32, (seq, aug - HEAD_DIM), 1)
        onehot = jnp.where(pos // SEL_LEN == blk, 1.0, 0.0)
        k_rows = kv_ref[0, pl.ds(0, half), :].T
        for g in range(N_KV):
            ka_ref[g] = jnp.concatenate([k_rows[:, g * HEAD_DIM:(g + 1) * HEAD_DIM], onehot], axis=1).astype(BF16)
            vt_ref[g] = kv_ref[0, pl.ds(half + g * HEAD_DIM, HEAD_DIM), :].astype(BF16)

    gate = gate_ref[...]
    sb = sb_ref[...]
    diag = qt // 2
    n_tiles = diag + 1
    pad = jnp.zeros((aug - HEAD_DIM - n_sel, GROUP_ROWS), BF16)
    for g in range(N_KV):
        sbg = sb[g * n_sel:(g + 1) * n_sel]
        qa_ref[g] = jnp.concatenate([_group_queries(qt_ref, g), jnp.concatenate([sbg] * HPG, axis=1), pad], axis=0)

    def tile(kt, carry):
        start = pl.multiple_of(kt * _TK, _TK)
        kind = jnp.where(kt == diag, qt % 2, jnp.where((kt == diag - 1) & (qt % 2 == 0), 2, 3))
        def scores(g):
            return _mm(ka_ref[g, pl.ds(start, _TK), :], qa_ref[g]) + map_ref[g, kind]

        def softmax(g, st):
            m_i, l_i, _ = carry[g]
            m_new = jnp.maximum(m_i, jnp.max(st, axis=0, keepdims=True))
            alpha = jnp.exp2(m_i - m_new)
            p = jnp.exp2(st - m_new)
            return m_new, alpha * l_i + jnp.sum(p, axis=0, keepdims=True), alpha, p.astype(BF16)

        def values(g, sm):
            m_new, l_new, alpha, pb = sm
            return m_new, l_new, alpha * carry[g][2] + _mm(vt_ref[g, :, pl.ds(start, _TK)], pb)

        return tuple(_staggered(N_KV, scores, softmax, values, ahead=2, lag=1))

    init = tuple((jnp.full((1, GROUP_ROWS), -jnp.inf, F32), jnp.zeros((1, GROUP_ROWS), F32),
                  jnp.zeros((HEAD_DIM, GROUP_ROWS), F32)) for _ in range(N_KV))
    final = lax.fori_loop(0, n_tiles, tile, init)
    for g in range(N_KV):
        _, l_f, acc = final[g]
        _store_heads(o_ref, acc / l_f, gate, g, 1)


def _sel_prompt(q_t, kv_t, selbias, smap, gates, batch, seq):
    nqt = seq // QT
    n_sel = selbias.shape[1] // N_KV
    return pl.pallas_call(
        functools.partial(_sel_prompt_kernel, n_sel=n_sel, seq=seq),
        grid=(batch, nqt),
        in_specs=[
            pl.BlockSpec((D_Q, QT), lambda b, i: (0, b * nqt + i)),
            pl.BlockSpec((1, D_KV, seq), lambda b, i: (b, 0, 0)),
            pl.BlockSpec((None, N_KV * n_sel, QT), lambda b, i: (b * nqt + i, 0, 0)),
            pl.BlockSpec(smap.shape, lambda b, i: (0, 0, 0, 0)),
            pl.BlockSpec((QT, LANES), lambda b, i: (b * nqt + i, 0)),
        ],
        out_specs=pl.BlockSpec((QT, D_Q), lambda b, i: (b * nqt + i, 0)),
        out_shape=jax.ShapeDtypeStruct((batch * seq, D_Q), F32),
        scratch_shapes=[pltpu.VMEM((N_KV, seq, 2 * HEAD_DIM), BF16), pltpu.VMEM((N_KV, HEAD_DIM, seq), BF16),
                        pltpu.VMEM((N_KV, 2 * HEAD_DIM, GROUP_ROWS), BF16)],
        compiler_params=_params(("arbitrary", "arbitrary"), 56),
        name="sel_attn_prompt",
    )(q_t, kv_t, selbias, smap, gates)


_WIN_TILES = WINDOW // QT + 1


def _win_prompt_kernel(qt_ref, *refs):
    kv_refs = refs[:_WIN_TILES]
    map_ref, gate_ref, o_ref = refs[_WIN_TILES:]
    qt = pl.program_id(1)
    gate = gate_ref[...]
    half = D_KV // 2
    span = _WIN_TILES * QT
    key = lax.broadcasted_iota(I32, (span, GROUP_ROWS), 0)
    in_seq = key >= (_WIN_TILES - 1 - qt) * QT
    k_rows = jnp.concatenate([r[0, pl.ds(0, half), :].T for r in kv_refs], axis=0)
    def scores(g):
        kg = k_rows[:, g * HEAD_DIM:(g + 1) * HEAD_DIM].astype(BF16)
        return jnp.where(in_seq, _mm(kg, _group_queries(qt_ref, g)) + map_ref[g], NEG)

    def softmax(g, st):
        e = jnp.exp2(st - jnp.max(st, axis=0, keepdims=True))
        return (e / jnp.sum(e, axis=0, keepdims=True)).astype(BF16)

    def values(g, pb):
        vgt = jnp.concatenate([r[0, pl.ds(half + g * HEAD_DIM, HEAD_DIM), :] for r in kv_refs],
                              axis=1).astype(BF16)
        _store_heads(o_ref, _mm(vgt, pb), gate, g, 2)

    _staggered(N_KV, scores, softmax, values, ahead=2, lag=1)


def _win_prompt(q_t, kv_t, wmap, gates, batch, seq):
    nqt = seq // QT

    def kv_spec(k):
        return pl.BlockSpec((1, D_KV, QT), lambda b, i: (b, 0, jnp.maximum(i - (_WIN_TILES - 1) + k, 0)))

    return pl.pallas_call(
        _win_prompt_kernel,
        grid=(batch, nqt),
        in_specs=[pl.BlockSpec((D_Q, QT), lambda b, i: (0, b * nqt + i))]
        + [kv_spec(k) for k in range(_WIN_TILES)]
        + [pl.BlockSpec(wmap.shape, lambda b, i: (0, 0, 0)),
           pl.BlockSpec((QT, LANES), lambda b, i: (b * nqt + i, 0))],
        out_specs=pl.BlockSpec((QT, D_Q), lambda b, i: (b * nqt + i, 0)),
        out_shape=jax.ShapeDtypeStruct((batch * seq, D_Q), F32),
        compiler_params=_params(("parallel", "parallel")),
        name="win_attn_prompt",
    )(q_t, *([kv_t] * _WIN_TILES), wmap, gates)


def _group_rows(x_by_group):
    row = lax.broadcasted_iota(I32, x_by_group[0].shape, 0)
    out = x_by_group[0]
    for g in range(1, N_KV):
        out = jnp.where(row // HPG == g, x_by_group[g], out)
    return out


_CMP_SAMPLE_ROWS = 4


def _cmp_sample_kernel(q_ref, kc_ref, bias_ref, ov_ref, gate_ref, o_ref, idx_ref, *, n_sel, past):
    bias = bias_ref[...]
    half = D_KV // 2
    imps = []
    for bl in range(_CMP_SAMPLE_ROWS):
        q = q_ref[bl]
        kv = kc_ref[bl].astype(BF16)
        s = _group_rows([_mm_nt(q, kv[:, g * HEAD_DIM:(g + 1) * HEAD_DIM]) for g in range(N_KV)]) + bias
        e = jnp.exp(s - jnp.max(s, axis=-1, keepdims=True))
        p = e / jnp.sum(e, axis=-1, keepdims=True) * (bias > 0.5 * NEG).astype(F32)
        pb = p.astype(BF16)
        o = _group_rows([_mm(pb, kv[:, half + g * HEAD_DIM:half + (g + 1) * HEAD_DIM]) for g in range(N_KV)])
        o_ref[bl] = o * gate_ref[bl]
        imp_h = _mm(pb, ov_ref[...])
        imps += [imp_h[g * HPG:g * HPG + 1] + imp_h[g * HPG + 1:g * HPG + 2] + imp_h[g * HPG + 2:g * HPG + 3]
                 + imp_h[g * HPG + 3:g * HPG + 4] for g in range(N_KV)]
    imp = jnp.concatenate(imps, axis=0)
    rows, nsp = imp.shape
    j = lax.broadcasted_iota(I32, (rows, nsp), 1)
    cur = past // SEL_LEN
    forced = (j == 0) | (j == cur) | (j == cur - 1)
    valid = (j * SEL_LEN <= past) & (j < n_sel)
    score = jnp.where(valid, jnp.where(forced, jnp.inf, imp), -jnp.inf)
    slot = lax.broadcasted_iota(I32, (rows, LANES), 1)
    res = jnp.zeros((rows, LANES), I32)
    for k in range(min(SEL_TOP, n_sel)):
        best = jnp.max(score, axis=-1, keepdims=True)
        pick = jnp.min(jnp.where(score == best, j, nsp), axis=-1, keepdims=True)
        res = jnp.where(slot == k, pick, res)
        score = jnp.where(j == pick, -jnp.inf, score)
    idx_ref[...] = res


def _cmp_sample(q3, kcmp, bias, ov, gate3, n_sel, past):
    bs, n_chunk, _ = kcmp.shape
    nb = _CMP_SAMPLE_ROWS
    heads = pl.BlockSpec((nb, N_HEADS, HEAD_DIM), lambda b: (b, 0, 0))
    return pl.pallas_call(
        functools.partial(_cmp_sample_kernel, n_sel=n_sel, past=past),
        grid=(bs // nb,),
        in_specs=[
            heads,
            pl.BlockSpec((nb, n_chunk, D_KV), lambda b: (b, 0, 0)),
            pl.BlockSpec(bias.shape, lambda b: (0, 0)),
            pl.BlockSpec(ov.shape, lambda b: (0, 0)),
            heads,
        ],
        out_specs=[heads, pl.BlockSpec((nb * N_KV, LANES), lambda b: (b, 0))],
        out_shape=[
            jax.ShapeDtypeStruct((bs, N_HEADS, HEAD_DIM), F32),
            jax.ShapeDtypeStruct((bs * N_KV, LANES), I32),
        ],
        compiler_params=_params(("parallel",)),
        name="cmp_attn_sample",
    )(q3, kcmp, bias, ov, gate3)


def _sel_sample_kernel(blk_ref, phys_ref, q_ref, new_ref, bias_ref, gate_ref, *refs, top, n_pages):
    kv_refs = refs[:top]
    k_refs = [r.at[0] for r in kv_refs]
    v_refs = [r.at[1] for r in kv_refs]
    o_ref = refs[top]
    del phys_ref
    b = pl.program_id(0)
    g = pl.program_id(1)
    q = q_ref[0, 0]
    lane = lax.broadcasted_iota(I32, (8, PAGE), 1)
    scores = []
    for k in range(top):
        blk = blk_ref[(b * N_KV + g) * top + k]
        page = jnp.minimum(blk // 2, n_pages)
        s = _mm(q, k_refs[k][...].astype(BF16)) + bias_ref[page]
        ok = (lane // SEL_LEN == blk % 2) & (blk < 2 * n_pages)
        scores.append(jnp.where(ok, s, NEG))
    k_new = new_ref[0, 0, 0:1, :].astype(BF16).astype(F32)
    v_new = new_ref[0, 0, 1:2, :].astype(BF16).astype(F32)
    s_new = jnp.sum(q.astype(F32) * k_new, axis=-1, keepdims=True) + bias_ref[n_pages]
    scores.append(s_new)
    s_all = jnp.concatenate(scores, axis=1)
    e = jnp.exp(s_all - jnp.max(s_all, axis=-1, keepdims=True))
    p = e / jnp.sum(e, axis=-1, keepdims=True)
    o = p[:, top * PAGE:top * PAGE + 1].astype(BF16).astype(F32) * v_new
    for k in range(top):
        o = o + _mm_nt(p[:, k * PAGE:(k + 1) * PAGE].astype(BF16), v_refs[k][...].astype(BF16))
    o_ref[0, 0] = o * gate_ref[0, 0]


def _sel_sample(blk, phys, q4, new4, bias4, gate4, pool5, top, n_pages):
    bs = q4.shape[0]

    def kv_spec(k):
        return pl.BlockSpec(
            (None, 2, None, HEAD_DIM, PAGE),
            lambda b, g, blk_r, phys_r: (phys_r[(b * N_KV + g) * top + k], 0, g, 0, 0))

    grp = lambda b, g, blk_r, phys_r: (b, g, 0, 0)
    grid_spec = pltpu.PrefetchScalarGridSpec(
        num_scalar_prefetch=2,
        grid=(bs, N_KV),
        in_specs=[
            pl.BlockSpec((1, 1, 8, HEAD_DIM), grp),
            pl.BlockSpec((1, 1, 8, HEAD_DIM), grp),
            pl.BlockSpec((None, n_pages + 1, 8, LANES), lambda b, g, blk_r, phys_r: (g, 0, 0, 0)),
            pl.BlockSpec((1, 1, 8, HEAD_DIM), grp),
        ] + [kv_spec(k) for k in range(top)],
        out_specs=pl.BlockSpec((1, 1, 8, HEAD_DIM), grp),
    )
    return pl.pallas_call(
        functools.partial(_sel_sample_kernel, top=top, n_pages=n_pages),
        grid_spec=grid_spec,
        out_shape=jax.ShapeDtypeStruct((bs, N_KV, 8, HEAD_DIM), F32),
        compiler_params=_params(("parallel", "parallel")),
        name="sel_attn_sample",
    )(blk, phys, q4, new4, bias4, gate4, *([pool5] * top))


def _win_sample_kernel(q_ref, new_ref, st_ref, col_ref, bias_ref, gate_ref, o_ref, nst_ref):
    st = st_ref[0]
    half = D_KV // 2
    for g in range(N_KV):
        q = q_ref[0, g]
        bias = bias_ref[g]
        s_old = _mm(q, st[g * HEAD_DIM:(g + 1) * HEAD_DIM, :].astype(BF16)) + bias[:, 0:WINDOW]
        k_new = new_ref[0, g, 0:1, :].astype(BF16).astype(F32)
        v_new = new_ref[0, g, 1:2, :].astype(BF16).astype(F32)
        s_new = jnp.sum(q.astype(F32) * k_new, axis=-1, keepdims=True) + bias[:, WINDOW:WINDOW + LANES]
        s_all = jnp.concatenate([s_old, s_new], axis=1)
        e = jnp.exp(s_all - jnp.max(s_all, axis=-1, keepdims=True))
        p = e / jnp.sum(e, axis=-1, keepdims=True)
        o = _mm_nt(p[:, 0:WINDOW].astype(BF16), st[half + g * HEAD_DIM:half + (g + 1) * HEAD_DIM, :].astype(BF16))
        o = o + p[:, WINDOW:WINDOW + 1].astype(BF16).astype(F32) * v_new
        o_ref[0, g] = o * gate_ref[0, g]
    rolled = pltpu.roll(st, WINDOW - 1, axis=1)
    lane = lax.broadcasted_iota(I32, (D_KV, LANES), 1)
    nst_ref[0, :, pl.ds(0, WINDOW - LANES)] = rolled[:, 0:WINDOW - LANES]
    nst_ref[0, :, pl.ds(WINDOW - LANES, LANES)] = jnp.where(
        lane == LANES - 1, col_ref[0], rolled[:, WINDOW - LANES:WINDOW])


def _win_sample(q4, new4, state_t, newcol, bias4, gate4):
    bs = q4.shape[0]
    grp = pl.BlockSpec((1, N_KV, 8, HEAD_DIM), lambda b: (b, 0, 0, 0))
    return pl.pallas_call(
        _win_sample_kernel,
        grid=(bs,),
        in_specs=[
            grp, grp,
            pl.BlockSpec((1, D_KV, WINDOW), lambda b: (b, 0, 0)),
            pl.BlockSpec((1, D_KV, LANES), lambda b: (b, 0, 0)),
            pl.BlockSpec(bias4.shape, lambda b: (0, 0, 0)),
            grp,
        ],
        out_specs=[grp, pl.BlockSpec((1, D_KV, WINDOW), lambda b: (b, 0, 0))],
        out_shape=[
            jax.ShapeDtypeStruct((bs, N_KV, 8, HEAD_DIM), F32),
            jax.ShapeDtypeStruct((bs, D_KV, WINDOW), F32),
        ],
        compiler_params=_params(("parallel",)),
        name="win_attn_sample",
    )(q4, new4, state_t, newcol, bias4, gate4)


_MOE_ROWS = 256
_MOE_SLACK = 2 * _MOE_ROWS


def _row_copy(src, src_row, dst, dst_row, sem):
    return pltpu.make_async_copy(src.at[pl.ds(src_row, 1)], dst.at[pl.ds(dst_row, 1)], sem)


def _expert_kernel(be_ref, nu_ref, dst_ref, src_ref, nxt_ref, h_ref, w_ref, wg_ref, wu_ref, wd_ref, yk_ref,
                   xbuf, ybuf, gsem, ssem, *, n_rows, n_live):
    del be_ref
    i = pl.program_id(0)
    slot = i % 2
    n_used = nu_ref[0]

    def gather(rows_ref, dst_slot):
        for r in range(_MOE_ROWS):
            _row_copy(h_ref, rows_ref[0, r], xbuf.at[dst_slot], r, gsem.at[dst_slot]).start()

    def wait_gather(s):
        for r in range(_MOE_ROWS):
            _row_copy(h_ref, 0, xbuf.at[s], 0, gsem.at[s]).wait()

    def scatter(s):
        for r in range(_MOE_ROWS):
            _row_copy(ybuf.at[s], r, yk_ref, dst_ref[0, r], ssem.at[s]).start()

    def wait_scatter(s):
        for r in range(_MOE_ROWS):
            _row_copy(ybuf.at[s], 0, yk_ref, 0, ssem.at[s]).wait()

    @pl.when(i == 0)
    def _():
        gather(src_ref, 0)
        ybuf[1] = jnp.zeros((_MOE_ROWS, ybuf.shape[2]), F32)
        fills = [
            pltpu.make_async_copy(ybuf.at[1, pl.ds(0, min(_MOE_ROWS, n_rows - off))],
                                  yk_ref.at[pl.ds(k * n_rows + off, min(_MOE_ROWS, n_rows - off))], ssem.at[1])
            for k in range(TOP_K) for off in range(n_live, n_rows, _MOE_ROWS)]
        for cp in fills:
            cp.start()
        for cp in fills:
            cp.wait()

    @pl.when(i + 1 < n_used)
    def _():
        gather(nxt_ref, 1 - slot)

    @pl.when(i < n_used)
    def _():
        wait_gather(slot)

        @pl.when(i >= 2)
        def _():
            wait_scatter(slot)

        x = xbuf[slot].astype(BF16)
        a = _silu(_mm(x, wg_ref[...])) * _mm(x, wu_ref[...])
        ybuf[slot] = _mm(a.astype(BF16), wd_ref[...]) * w_ref[...]
        scatter(slot)

        @pl.when(i == n_used - 1)
        def _():
            @pl.when(i >= 1)
            def _():
                wait_scatter(1 - slot)

            wait_scatter(slot)


def _experts(blk_e, n_used, rows_src, rows_dst, rows_w, h2_all, n_live, w_gate, w_up, w_down):
    n_rows, d = h2_all.shape
    n_blk = blk_e.shape[0]
    src3 = rows_src.reshape(n_blk, 1, _MOE_ROWS)
    dst3 = rows_dst.reshape(n_blk, 1, _MOE_ROWS)
    wspec = lambda a: pl.BlockSpec((None,) + a.shape[1:], lambda i, be, nu: (be[i], 0, 0))
    smem_blk = lambda fn: pl.BlockSpec((None, 1, _MOE_ROWS), fn, memory_space=pltpu.SMEM)
    grid_spec = pltpu.PrefetchScalarGridSpec(
        num_scalar_prefetch=2,
        grid=(n_blk,),
        in_specs=[
            smem_blk(lambda i, be, nu: (i, 0, 0)),
            smem_blk(lambda i, be, nu: (i, 0, 0)),
            smem_blk(lambda i, be, nu: (jnp.minimum(i + 1, n_blk - 1), 0, 0)),
            pl.BlockSpec(memory_space=pl.ANY),
            pl.BlockSpec((_MOE_ROWS, 1), lambda i, be, nu: (i, 0)),
            wspec(w_gate), wspec(w_up), wspec(w_down),
        ],
        out_specs=pl.BlockSpec(memory_space=pl.ANY),
        scratch_shapes=[
            pltpu.VMEM((2, _MOE_ROWS, d), F32),
            pltpu.VMEM((2, _MOE_ROWS, d), F32),
            pltpu.SemaphoreType.DMA((2,)),
            pltpu.SemaphoreType.DMA((2,)),
        ],
    )
    return pl.pallas_call(
        functools.partial(_expert_kernel, n_rows=n_rows, n_live=n_live),
        grid_spec=grid_spec,
        out_shape=jax.ShapeDtypeStruct((TOP_K * n_rows, d), F32),
        compiler_params=pltpu.CompilerParams(dimension_semantics=("arbitrary",), vmem_limit_bytes=56 * MIB,
                                             disable_bounds_checks=True),
        name="moe_experts",
    )(blk_e, n_used, dst3, src3, src3, h2_all, rows_w, w_gate, w_up, w_down)


def _combine_kernel(x_ref, y0_ref, y1_ref, g_ref, o_ref):
    x = x_ref[...] + (y0_ref[...] + y1_ref[...])
    ms = jnp.mean(x * x, axis=-1, keepdims=True)
    o_ref[...] = x * lax.rsqrt(ms + EPS) * g_ref[...]


def _combine(x1, yk, gfin, row0, tm):
    n, d = x1.shape
    yk3 = yk.reshape(TOP_K, -1, d)
    blk0 = row0 // tm
    return pl.pallas_call(
        _combine_kernel,
        grid=(n // tm,),
        in_specs=[
            pl.BlockSpec((tm, d), lambda i: (i, 0)),
            pl.BlockSpec((None, tm, d), lambda i: (0, blk0 + i, 0)),
            pl.BlockSpec((None, tm, d), lambda i: (1, blk0 + i, 0)),
            pl.BlockSpec((1, d), lambda i: (0, 0)),
        ],
        out_specs=pl.BlockSpec((tm, d), lambda i: (i, 0)),
        out_shape=jax.ShapeDtypeStruct((n, d), F32),
        compiler_params=_params(("parallel",)),
        name="moe_combine",
    )(x1, yk3, yk3, gfin)


def _dispatch(eid, w, counts, n_rows):
    a = eid.shape[0]
    tb = _MOE_ROWS
    order = jnp.argsort(eid).astype(I32)
    padded = (counts + tb - 1) // tb * tb
    seg_end = jnp.cumsum(padded)
    seg_start = seg_end - padded
    start = jnp.cumsum(counts) - counts
    n_blk = (a + N_EXPERTS * (tb - 1) + tb - 1) // tb
    blk_e = jnp.minimum(jnp.sum((seg_end[None, :] <= (jnp.arange(n_blk) * tb)[:, None]).astype(I32), axis=1),
                        N_EXPERTS - 1)
    blk = jnp.arange(n_blk)[:, None]
    r = jnp.arange(tb)[None, :]
    off = (blk * tb - seg_start[blk_e][:, None]) + r
    valid = off < counts[blk_e][:, None]
    asg = order[jnp.clip(start[blk_e][:, None] + off, 0, a - 1)]
    rows_src = jnp.where(valid, asg // TOP_K, 0).astype(I32)
    rows_dst = jnp.where(valid, (asg % TOP_K) * n_rows + asg // TOP_K, (blk % 2) * n_rows + (n_rows - tb) + r)
    rows_w = jnp.where(valid, w[asg], 0.0)
    n_used = (seg_end[-1] // tb).astype(I32).reshape(1)
    return rows_src, rows_dst.astype(I32), rows_w.reshape(-1, 1), blk_e.astype(I32), n_used


def _overlap(n_chunk, n_sel, n_sel_pad):
    cs = jnp.arange(n_chunk)[:, None] * CMP_STRIDE
    ss = jnp.arange(n_sel_pad)[None, :] * SEL_LEN
    hit = (cs < ss + SEL_LEN) & (cs + CMP_LEN > ss) & (jnp.arange(n_sel_pad)[None, :] < n_sel)
    return hit.astype(BF16)


def kernel(x_prompt, x_sample, cache_cmp_kv, cache_sel_kv, state_win_kv, state_conv, page_table, rel_bias,
           norm_mix, w_in, b_in, conv_w, conv_b, conv_ln_g, conv_ln_b, w_conv_out, b_conv_out,
           cmp_pe_k, cmp_w1_k, cmp_w2_k, cmp_pe_v, cmp_w1_v, cmp_w2_v, w_nsa_out, w_out, norm_ffn,
           w_rg, b_rg, w_re, b_re, w_gate, w_up, w_down, norm_final):
    batch, seq, _ = x_prompt.shape
    bs = x_sample.shape[0]
    n_pool = cache_cmp_kv.shape[1]
    n_pages = page_table.shape[1]
    past = n_pages * PAGE
    n_tok = batch * seq
    assert bs + _MOE_ROWS <= _MOE_SLACK and n_tok % bs == 0 and seq % _TK == 0 and bs % 16 == 0
    l = 0

    wt = w_in[l].T.astype(BF16)
    bias_in = b_in[l]
    cuts = [0, D_CONV, 2 * D_CONV, 2 * D_CONV + D_Q, 2 * D_CONV + D_Q + 3 * D_KV]
    cuts += [cuts[-1] + 3 * N_HEADS, cuts[-1] + 3 * N_HEADS + D_MODEL, cuts[-1] + 3 * N_HEADS + 2 * D_MODEL]
    seg = lambda k: (wt[cuts[k]:cuts[k + 1]], bias_in[cuts[k]:cuts[k + 1]][None, :])
    (wa, ba), (wb, bb), (wq, bq), (wkv, bkv), (wzg, bzg), (wga, bga), (wgb, bgb) = [seg(k) for k in range(7)]
    wzg = jnp.pad(wzg, ((0, LANES - 3 * N_HEADS), (0, 0)))
    bzg = jnp.pad(bzg, ((0, 0), (0, LANES - 3 * N_HEADS)))
    wc = w_conv_out[l].astype(BF16)
    wn = w_nsa_out[l].astype(BF16)
    wo = w_out[l].astype(BF16)
    wr = jnp.zeros((LANES, D_MODEL), F32).at[0:N_GROUPS].set(w_rg[l].T).at[8:8 + N_EXPERTS].set(w_re[l].T)
    br = jnp.zeros((LANES,), F32).at[0:N_GROUPS].set(b_rg[l]).at[8:8 + N_EXPERTS].set(b_re[l].reshape(-1))
    wr, br = wr.astype(BF16), br[:, None]
    cw = _compress_weights(cmp_pe_k[l], cmp_w1_k[l], cmp_w2_k[l], cmp_pe_v[l], cmp_w1_v[l], cmp_w2_v[l])
    rb_t = rel_bias.T
    gmix, gffn, gfin = norm_mix[l][None, :], norm_ffn[l][None, :], norm_final[None, :]
    vec = lambda a: a[l][None, :]

    q_scale = HEAD_DIM ** -0.5
    xp = x_prompt.reshape(n_tok, D_MODEL)
    xs = x_sample.reshape(bs, D_MODEL)
    flat = lambda w: w[l].reshape(-1, w.shape[-1])
    (hp, up, qp, gates_p, *kv_p), (wg_bf,) = _project_prompt(
        xp, gmix, wa, wb, wq, wkv, wzg, ba, bb, bq.reshape(-1, 1), bkv.reshape(-1, 1), bzg,
        q_scale * LOG2E, batch, seq, 256, riders=(flat(w_gate),))
    hs = _rmsnorm(xs, gmix, bs)
    us = _glu(hs, wa, wb, ba, bb, bs, 512)
    qs = _proj(hs, wq, bq, lambda z: z * q_scale, BF16, bs, 512, "proj_q")
    gates_s = _proj(hs, wzg, bzg, _sigmoid, F32, bs, LANES, "proj_head_gates")
    kv_s = _proj(hs, wkv, bkv, lambda z: z, F32, bs, 512, "proj_kv_sample")

    conv_args = (conv_w[l], vec(conv_b), vec(conv_ln_g), vec(conv_ln_b))
    cp = _conv_prompt(up, *conv_args, batch, seq, 256)
    cs, new_conv_s = _conv_sample(state_conv[l].transpose(1, 0, 2), us, *conv_args)

    nqt = seq // QT
    n_chunk_p = seq // CMP_STRIDE
    n_chunk_s = n_pages * _CHUNKS_PER_PAGE
    cmap, wmap, smap = _bias_prompt(rb_t, nqt, n_chunk_p)
    cbias_s, wbias_s, sbias_s = _bias_sample(rb_t, past, n_chunk_s, n_pages)

    ident = jnp.zeros((1,), I32)
    kcmp_p = _compress(kv_p[0], ident, cw, batch, seq // PAGE, seq // PAGE, False)
    feat_major = lambda a: a.transpose(0, 2, 3, 4, 1)
    pool_cmp = feat_major(cache_cmp_kv[l]).reshape(n_pool, D_KV, PAGE)
    pp_s = min(32, n_pages)
    kcmp_s = _compress(pool_cmp, page_table.reshape(-1), cw, bs, n_pages, pp_s, True)

    n_sel_p = seq // SEL_LEN
    o_cmp_p, selbias = _cmp_prompt(qp, kcmp_p, cmap, gates_p, _overlap(n_chunk_p, n_sel_p, n_sel_p).T,
                                   batch, seq)
    n_sel_s = -(-(past + 1) // SEL_LEN)
    n_sel_pad = -(-n_sel_s // LANES) * LANES
    top_s = min(SEL_TOP, n_sel_s)
    head_gate = lambda k: jnp.broadcast_to(
        gates_s[:, k * N_HEADS:(k + 1) * N_HEADS, None], (bs, N_HEADS, HEAD_DIM))
    qs3 = qs.reshape(bs, N_HEADS, HEAD_DIM)
    o_cmp_s, idx_s = _cmp_sample(qs3, kcmp_s, cbias_s, _overlap(n_chunk_s, n_sel_s, n_sel_pad), head_gate(0),
                                 n_sel_s, past)

    o_sel_p = _sel_prompt(qp, kv_p[1], selbias, smap, gates_p, batch, seq)
    by_group = lambda a: jnp.pad(a.reshape(bs, N_KV, HPG, -1), ((0, 0), (0, 0), (0, 8 - HPG), (0, 0)))
    q4 = by_group(qs3)
    blk = idx_s[:, :top_s].reshape(-1)
    phys = jnp.take_along_axis(
        page_table, jnp.minimum(blk // 2, n_pages - 1).reshape(bs, -1), axis=1).reshape(-1)
    kv_new = lambda k: jnp.pad(
        kv_s[:, k * D_KV:(k + 1) * D_KV].reshape(bs, 2, N_KV, HEAD_DIM).transpose(0, 2, 1, 3),
        ((0, 0), (0, 0), (0, 6), (0, 0)))
    bias_by_group = lambda a: jnp.pad(
        jnp.moveaxis(a, -2, 0).reshape((N_KV, HPG) + a.shape[:-2] + a.shape[-1:]),
        ((0, 0), (0, 8 - HPG)) + ((0, 0),) * (a.ndim - 1))
    sb4 = jnp.moveaxis(bias_by_group(sbias_s), 1, 2)
    pool_sel = feat_major(cache_sel_kv[l])
    o_sel_s = _sel_sample(blk, phys, q4, kv_new(1), sb4, by_group(head_gate(1)), pool_sel, top_s, n_pages)

    o_win_p = _win_prompt(qp, kv_p[2], wmap, gates_p, batch, seq)
    win_t = feat_major(state_win_kv[l]).reshape(bs, D_KV, WINDOW)
    newcol = jnp.broadcast_to(kv_s[:, 2 * D_KV:3 * D_KV, None], (bs, D_KV, LANES))
    o_win_s, new_win_s = _win_sample(q4, kv_new(2), win_t, newcol, bias_by_group(wbias_s), by_group(head_gate(2)))

    m_p, (wu_bf,) = _merge_a(hp, cp, o_cmp_p, o_sel_p, o_win_p, wga, wgb, wc, wn, bga, bgb,
                             vec(b_conv_out), 512, 512, riders=(flat(w_up),))
    (x1_p, h2_all, eid_p, rw_p, cnt_p), (wd_bf,) = _merge_b(
        xp, m_p, wo, gffn, wr, br, 256, slack=_MOE_SLACK, riders=(flat(w_down),))
    from_groups = lambda a: a[:, :, :HPG, :].reshape(bs, D_Q)
    m_s, _ = _merge_a(hs, cs, o_cmp_s.reshape(bs, D_Q), from_groups(o_sel_s), from_groups(o_win_s),
                      wga, wgb, wc, wn, bga, bgb, vec(b_conv_out), bs, 512)
    (x1_s, h2_all, eid_s, rw_s, cnt_s), _ = _merge_b(xs, m_s, wo, gffn, wr, br, bs, h2_all=h2_all, row0=n_tok)

    eid = jnp.concatenate([eid_p[:TOP_K].T, eid_s[:TOP_K].T], axis=0).reshape(-1)
    rw = jnp.concatenate([rw_p[:TOP_K].T, rw_s[:TOP_K].T], axis=0).reshape(-1)
    counts = jnp.sum(cnt_p[:, :, 0], axis=0) + jnp.sum(cnt_s[:, :, 0], axis=0)
    rows_src, rows_dst, rows_w, blk_e, n_used = _dispatch(eid, rw, counts, h2_all.shape[0])
    yk = _experts(blk_e, n_used, rows_src, rows_dst, rows_w, h2_all, n_tok + bs,
                  wg_bf.reshape(w_gate.shape[1:]), wu_bf.reshape(w_up.shape[1:]), wd_bf.reshape(w_down.shape[1:]))
    y_p = _combine(x1_p, yk, gfin, 0, 256)
    y_s = _combine(x1_s, yk, gfin, n_tok, bs)

    def cache_rows(a_t, rows):
        return a_t.reshape(a_t.shape[0], 2, N_KV, HEAD_DIM, rows).transpose(0, 4, 1, 2, 3)[None]

    new_row = lambda k: kv_s[:, k * D_KV:(k + 1) * D_KV].reshape(1, bs, 1, 2, N_KV, HEAD_DIM)
    hist = CONV_W - 1
    return (
        y_p.reshape(batch, seq, D_MODEL),
        y_s.reshape(bs, 1, D_MODEL),
        cache_rows(kv_p[0], seq),
        new_row(0),
        cache_rows(kv_p[1], seq),
        new_row(1),
        cache_rows(kv_p[2][:, :, seq - WINDOW:], WINDOW),
        cache_rows(new_win_s, WINDOW),
        up.reshape(batch, seq, D_CONV)[None, :, seq - hist:],
        new_conv_s.transpose(1, 0, 2)[None],
    )
```

```python
import functools
import math

import jax
import jax.numpy as jnp
from jax import lax
from jax.experimental import pallas as pl
from jax.experimental.pallas import tpu as pltpu

F32 = jnp.float32
BF16 = jnp.bfloat16
I32 = jnp.int32

D_MODEL = 2048
D_CONV = 1024
CONV_W = 31
N_HEADS = 16
HEAD_DIM = 64
N_KV = 4
HPG = N_HEADS // N_KV
D_Q = N_HEADS * HEAD_DIM
D_KV = 2 * N_KV * HEAD_DIM
CMP_LEN = 32
CMP_STRIDE = 16
CMP_HID = 64
SEL_LEN = 64
SEL_TOP = 16
WINDOW = 512
N_BUCKETS = 32
MAX_DIST = 128
N_GROUPS = 4
EXP_PER_GROUP = 8
N_EXPERTS = N_GROUPS * EXP_PER_GROUP
TOP_K = 2
D_EXPERT = 1024
EPS = 1e-6
NEG = -1e30
PAGE = 128
LANES = 128
QT = 128
GROUP_ROWS = HPG * QT
MIB = 1024 * 1024
LOG2E = math.log2(math.e)


def _params(sem, vmem_mib=48):
    return pltpu.CompilerParams(dimension_semantics=sem, vmem_limit_bytes=vmem_mib * MIB)


def _mm_nt(a, b):
    return lax.dot_general(a, b, (((1,), (1,)), ((), ())), preferred_element_type=F32)


def _mm(a, b):
    return jnp.dot(a, b, preferred_element_type=F32)


def _sigmoid(z):
    return 1.0 / (1.0 + jnp.exp(-z))


def _silu(z):
    return z * _sigmoid(z)


def _gelu_tanh(z):
    return 0.5 * z * (1.0 + jnp.tanh(math.sqrt(2.0 / math.pi) * (z + 0.044715 * (z * z * z))))


def _rmsnorm_kernel(x_ref, g_ref, o_ref):
    x = x_ref[...]
    ms = jnp.mean(x * x, axis=-1, keepdims=True)
    o_ref[...] = (x * lax.rsqrt(ms + EPS) * g_ref[...]).astype(o_ref.dtype)


def _rmsnorm(x, g, tm):
    n, d = x.shape
    return pl.pallas_call(
        _rmsnorm_kernel,
        grid=(n // tm,),
        in_specs=[pl.BlockSpec((tm, d), lambda i: (i, 0)), pl.BlockSpec((1, d), lambda i: (0, 0))],
        out_specs=pl.BlockSpec((tm, d), lambda i: (i, 0)),
        out_shape=jax.ShapeDtypeStruct((n, d), BF16),
        compiler_params=_params(("parallel",)),
        name="rmsnorm",
    )(x, g)


def _proj_kernel(h_ref, w_ref, b_ref, o_ref, *, act):
    z = _mm_nt(h_ref[...], w_ref[...]) + b_ref[...]
    o_ref[...] = act(z).astype(o_ref.dtype)


def _proj(h, wt, b, act, out_dtype, tm, tn, name):
    n, k = h.shape
    nout = wt.shape[0]
    return pl.pallas_call(
        functools.partial(_proj_kernel, act=act),
        grid=(n // tm, nout // tn),
        in_specs=[
            pl.BlockSpec((tm, k), lambda i, j: (i, 0)),
            pl.BlockSpec((tn, k), lambda i, j: (j, 0)),
            pl.BlockSpec((1, tn), lambda i, j: (0, j)),
        ],
        out_specs=pl.BlockSpec((tm, tn), lambda i, j: (i, j)),
        out_shape=jax.ShapeDtypeStruct((n, nout), out_dtype),
        compiler_params=_params(("parallel", "parallel")),
        name=name,
    )(h, wt, b)


def _proj_t_kernel(h_ref, w_ref, b_ref, o_ref, *, act):
    z = _mm_nt(w_ref[...], h_ref[...]) + b_ref[...]
    o_ref[...] = act(z).astype(o_ref.dtype)


def _proj_t(h, wt, bcol, act, out_dtype, tm, tn, name):
    n, k = h.shape
    nout = wt.shape[0]
    return pl.pallas_call(
        functools.partial(_proj_t_kernel, act=act),
        grid=(n // tm, nout // tn),
        in_specs=[
            pl.BlockSpec((tm, k), lambda i, j: (i, 0)),
            pl.BlockSpec((tn, k), lambda i, j: (j, 0)),
            pl.BlockSpec((tn, 1), lambda i, j: (j, 0)),
        ],
        out_specs=pl.BlockSpec((tn, tm), lambda i, j: (j, i)),
        out_shape=jax.ShapeDtypeStruct((nout, n), out_dtype),
        compiler_params=_params(("parallel", "parallel")),
        name=name,
    )(h, wt, bcol)


def _glu_kernel(h_ref, wa_ref, wb_ref, ba_ref, bb_ref, o_ref):
    h = h_ref[...]
    a = _mm_nt(h, wa_ref[...]) + ba_ref[...]
    b = _mm_nt(h, wb_ref[...]) + bb_ref[...]
    o_ref[...] = a * _sigmoid(b)


def _glu(h, wat, wbt, ba, bb, tm, tn):
    n, k = h.shape
    nout = wat.shape[0]
    wspec = pl.BlockSpec((tn, k), lambda i, j: (j, 0))
    bspec = pl.BlockSpec((1, tn), lambda i, j: (0, j))
    return pl.pallas_call(
        _glu_kernel,
        grid=(n // tm, nout // tn),
        in_specs=[pl.BlockSpec((tm, k), lambda i, j: (i, 0)), wspec, wspec, bspec, bspec],
        out_specs=pl.BlockSpec((tm, tn), lambda i, j: (i, j)),
        out_shape=jax.ShapeDtypeStruct((n, nout), F32),
        compiler_params=_params(("parallel", "parallel")),
        name="glu",
    )(h, wat, wbt, ba, bb)


def _kvt_kernel(h_ref, w_ref, b_ref, *o_refs):
    h = h_ref[...]
    for kind, o_ref in enumerate(o_refs):
        rows = pl.ds(kind * D_KV, D_KV)
        o_ref[0] = _mm_nt(w_ref[rows, :], h) + b_ref[rows, :]


def _kv_transposed(h, wt, bcol, batch, seq, tm):
    n, k = h.shape
    kinds = wt.shape[0] // D_KV
    nt = seq // tm
    return pl.pallas_call(
        _kvt_kernel,
        grid=(n // tm,),
        in_specs=[
            pl.BlockSpec((tm, k), lambda i: (i, 0)),
            pl.BlockSpec(wt.shape, lambda i: (0, 0)),
            pl.BlockSpec(bcol.shape, lambda i: (0, 0)),
        ],
        out_specs=[pl.BlockSpec((1, D_KV, tm), lambda i: (i // nt, 0, i % nt))] * kinds,
        out_shape=[jax.ShapeDtypeStruct((batch, D_KV, seq), F32)] * kinds,
        compiler_params=_params(("parallel",)),
        name="kv_transposed",
    )(h, wt, bcol)


def _project_prompt_kernel(x_ref, g_ref, wa_ref, wb_ref, wq_ref, wkv_ref, wzg_ref, ba_ref, bb_ref, bq_ref,
                           bkv_ref, bzg_ref, *rest, q_scale, n_riders, kinds):
    rider_in = rest[:n_riders]
    h_ref, u_ref, qt_ref, gates_ref = rest[n_riders:n_riders + 4]
    kv_refs = rest[n_riders + 4:n_riders + 4 + kinds]
    rider_out = rest[n_riders + 4 + kinds:]
    for src, dst in zip(rider_in, rider_out):
        dst[...] = src[...].astype(dst.dtype)
    x = x_ref[...]
    ms = jnp.mean(x * x, axis=-1, keepdims=True)
    h = (x * lax.rsqrt(ms + EPS) * g_ref[...]).astype(BF16)
    h_ref[...] = h
    a = _mm_nt(h, wa_ref[...]) + ba_ref[...]
    b = _mm_nt(h, wb_ref[...]) + bb_ref[...]
    u_ref[...] = a * _sigmoid(b)
    qt_ref[...] = ((_mm_nt(wq_ref[...], h) + bq_ref[...]) * q_scale).astype(qt_ref.dtype)
    gates_ref[...] = _sigmoid(_mm_nt(h, wzg_ref[...]) + bzg_ref[...])
    for kind, o_ref in enumerate(kv_refs):
        rows = pl.ds(kind * D_KV, D_KV)
        o_ref[0] = _mm_nt(wkv_ref[rows, :], h) + bkv_ref[rows, :]


def _project_prompt(x, g, wa, wb, wq, wkv, wzg, ba, bb, bq_col, bkv_col, bzg, q_scale, batch, seq, tm,
                    riders=()):
    n, d = x.shape
    kinds = wkv.shape[0] // D_KV
    nt = seq // tm
    tok = lambda width: pl.BlockSpec((tm, width), lambda i: (i, 0))
    const = lambda a: pl.BlockSpec(a.shape, lambda i: (0,) * a.ndim, pipeline_mode=pl.Buffered(1))
    consts = (g, wa, wb, wq, wkv, wzg, ba, bb, bq_col, bkv_col, bzg)
    ride = _rider_specs(riders, n // tm, lambda i: (i, 0))
    out = pl.pallas_call(
        functools.partial(_project_prompt_kernel, q_scale=q_scale, n_riders=len(riders), kinds=kinds),
        grid=(n // tm,),
        in_specs=[tok(d)] + [const(a) for a in consts] + ride,
        out_specs=[tok(d), tok(D_CONV), pl.BlockSpec((D_Q, tm), lambda i: (0, i)), tok(LANES)]
        + [pl.BlockSpec((1, D_KV, tm), lambda i: (i // nt, 0, i % nt))] * kinds + ride,
        out_shape=[
            jax.ShapeDtypeStruct((n, d), BF16),
            jax.ShapeDtypeStruct((n, D_CONV), F32),
            jax.ShapeDtypeStruct((D_Q, n), BF16),
            jax.ShapeDtypeStruct((n, LANES), F32),
        ] + [jax.ShapeDtypeStruct((batch, D_KV, seq), F32)] * kinds
        + [jax.ShapeDtypeStruct(a.shape, BF16) for a in riders],
        compiler_params=_params(("parallel",), 56),
        name="project_prompt",
    )(x, *consts, *riders)
    return out[:4 + kinds], out[4 + kinds:]


def _merge_a_kernel(h_ref, c_ref, o1_ref, o2_ref, o3_ref, wga_ref, wgb_ref, wc_ref, wn_ref,
                    bga_ref, bgb_ref, bc_ref, *rest):
    n_cast = (len(rest) - 1) // 2
    m_ref = rest[n_cast]
    h = h_ref[...]
    ga = _sigmoid(_mm_nt(h, wga_ref[...]) + bga_ref[...])
    gb = _sigmoid(_mm_nt(h, wgb_ref[...]) + bgb_ref[...])
    yc = _mm(c_ref[...], wc_ref[...]) + bc_ref[...]
    o = (o1_ref[...] + o2_ref[...] + o3_ref[...]).astype(BF16)
    yn = _mm(o, wn_ref[...])
    m_ref[...] = (ga * yc + gb * yn).astype(m_ref.dtype)
    for src, dst in zip(rest[:n_cast], rest[n_cast + 1:]):
        dst[...] = src[...].astype(dst.dtype)


def _rider_specs(riders, steps, index):
    return [pl.BlockSpec((a.shape[0] // steps, a.shape[1]), index) for a in riders]


def _merge_a(h, c, o1, o2, o3, wgat, wgbt, wc, wn, bga, bgb, bc, tm, tn, riders=()):
    n, k = h.shape
    nj = D_MODEL // tn
    tok = lambda width: pl.BlockSpec((tm, width), lambda i, j: (i, 0))
    wt_spec = pl.BlockSpec((tn, k), lambda i, j: (j, 0))
    w_spec = pl.BlockSpec((c.shape[1], tn), lambda i, j: (0, j))
    b_spec = pl.BlockSpec((1, tn), lambda i, j: (0, j))
    ride = _rider_specs(riders, (n // tm) * nj, lambda i, j: (i * nj + j, 0))
    out = pl.pallas_call(
        _merge_a_kernel,
        grid=(n // tm, nj),
        in_specs=[tok(k), tok(D_CONV), tok(D_Q), tok(D_Q), tok(D_Q), wt_spec, wt_spec, w_spec, w_spec,
                  b_spec, b_spec, b_spec] + ride,
        out_specs=[pl.BlockSpec((tm, tn), lambda i, j: (i, j))] + ride,
        out_shape=[jax.ShapeDtypeStruct((n, D_MODEL), BF16)]
        + [jax.ShapeDtypeStruct(a.shape, BF16) for a in riders],
        compiler_params=_params(("parallel", "parallel"), 56),
        name="merge_gates",
    )(h, c, o1, o2, o3, wgat, wgbt, wc, wn, bga, bgb, bc, *riders)
    return out[0], out[1:]


def _merge_b_kernel(x_ref, m_ref, wo_ref, gn_ref, wr_ref, br_ref, *rest, n_real, n_shared, n_riders):
    rider_in = rest[n_shared:n_shared + n_riders]
    outs = rest[n_shared + n_riders:n_shared + n_riders + 5]
    rider_out = rest[n_shared + n_riders + 5:]
    h2_ref = outs[1]

    @pl.when(pl.program_id(0) < n_real)
    def _():
        _merge_b_tile(x_ref, m_ref, wo_ref, gn_ref, wr_ref, br_ref, *outs)
        for src, dst in zip(rider_in, rider_out):
            dst[...] = src[...].astype(dst.dtype)

    @pl.when(pl.program_id(0) >= n_real)
    def _():
        h2_ref[...] = jnp.zeros(h2_ref.shape, h2_ref.dtype)


def _merge_b_tile(x_ref, m_ref, wo_ref, gn_ref, wr_ref, br_ref, x1_ref, h2_ref, eid_ref, rw_ref, cnt_ref):
    x1 = x_ref[...] + _mm(m_ref[...], wo_ref[...])
    x1_ref[...] = x1
    ms = jnp.mean(x1 * x1, axis=-1, keepdims=True)
    h2 = x1 * lax.rsqrt(ms + EPS) * gn_ref[...]
    h2_ref[...] = h2
    logits = _mm_nt(wr_ref[...], h2.astype(BF16)) + br_ref[...]
    tm = logits.shape[1]
    row = lax.broadcasted_iota(I32, (8, tm), 0)
    lg = jnp.where(row < N_GROUPS, logits[0:8], -jnp.inf)
    gmax = jnp.max(lg, axis=0, keepdims=True)
    gidx = jnp.min(jnp.where(lg == gmax, row, 8), axis=0, keepdims=True)
    wg = 1.0 / jnp.sum(jnp.exp(lg - gmax), axis=0, keepdims=True)
    le = jnp.zeros((8, tm), F32)
    for g in range(N_GROUPS):
        le = jnp.where(gidx == g, logits[8 + 8 * g:16 + 8 * g], le)
    ee = jnp.exp(le - jnp.max(le, axis=0, keepdims=True))
    pz = ee / jnp.sum(ee, axis=0, keepdims=True)
    p1 = jnp.max(pz, axis=0, keepdims=True)
    i1 = jnp.min(jnp.where(pz == p1, row, 8), axis=0, keepdims=True)
    pz2 = jnp.where(row == i1, -1.0, pz)
    p2 = jnp.max(pz2, axis=0, keepdims=True)
    i2 = jnp.min(jnp.where(pz2 == p2, row, 8), axis=0, keepdims=True)
    den = p1 + p2
    e1 = gidx * EXP_PER_GROUP + i1
    e2 = gidx * EXP_PER_GROUP + i2
    eid_ref[...] = jnp.where(row == 0, e1, jnp.where(row == 1, e2, 0))
    rw_ref[...] = jnp.where(row == 0, wg * p1 / den, jnp.where(row == 1, wg * p2 / den, 0.0))
    erow = lax.broadcasted_iota(I32, (N_EXPERTS, tm), 0)
    hits = (erow == e1).astype(F32) + (erow == e2).astype(F32)
    cnt_ref[...] = jnp.broadcast_to(jnp.sum(hits, axis=1, keepdims=True), (N_EXPERTS, LANES)).astype(I32)


def _merge_b(x, m, wo, gn, wr, br, tm, h2_all=None, row0=0, slack=0, riders=()):
    n, d = x.shape
    blk0 = row0 // tm
    n_real = n // tm
    real = lambda i: jnp.minimum(i, n_real - 1)
    shared = [] if h2_all is None else [h2_all]
    h2_rows = n + slack if h2_all is None else h2_all.shape[0]
    ride = _rider_specs(riders, n_real, lambda i: (real(i), 0))
    out = pl.pallas_call(
        functools.partial(_merge_b_kernel, n_real=n_real, n_shared=len(shared), n_riders=len(riders)),
        grid=((n + slack) // tm,),
        in_specs=[
            pl.BlockSpec((tm, d), lambda i: (real(i), 0)),
            pl.BlockSpec((tm, d), lambda i: (real(i), 0)),
            pl.BlockSpec((d, d), lambda i: (0, 0)),
            pl.BlockSpec((1, d), lambda i: (0, 0)),
            pl.BlockSpec((LANES, d), lambda i: (0, 0)),
            pl.BlockSpec((LANES, 1), lambda i: (0, 0)),
        ] + [pl.BlockSpec(memory_space=pl.ANY)] * len(shared) + ride,
        out_specs=[
            pl.BlockSpec((tm, d), lambda i: (real(i), 0)),
            pl.BlockSpec((tm, d), lambda i: (blk0 + i, 0)),
            pl.BlockSpec((8, tm), lambda i: (0, real(i))),
            pl.BlockSpec((8, tm), lambda i: (0, real(i))),
            pl.BlockSpec((None, N_EXPERTS, LANES), lambda i: (real(i), 0, 0)),
        ] + ride,
        out_shape=[
            jax.ShapeDtypeStruct((n, d), F32),
            jax.ShapeDtypeStruct((h2_rows, d), F32),
            jax.ShapeDtypeStruct((8, n), I32),
            jax.ShapeDtypeStruct((8, n), F32),
            jax.ShapeDtypeStruct((n_real, N_EXPERTS, LANES), I32),
        ] + [jax.ShapeDtypeStruct(a.shape, BF16) for a in riders],
        input_output_aliases={6: 1} if shared else {},
        compiler_params=_params(("arbitrary",), 56),
        name="merge_out_router",
    )(x, m, wo, gn, wr, br, *shared, *riders)
    return out[:5], out[5:]


_HALO = 32


def _conv_prompt_kernel(u_ref, prev_ref, w_ref, b_ref, g_ref, beta_ref, o_ref, buf_ref, acc_ref, sh_ref, *, tt):
    i = pl.program_id(1)
    prev = prev_ref[pl.ds(tt - _HALO, _HALO), :]
    buf_ref[pl.ds(0, _HALO), :] = jnp.where(i > 0, prev, 0.0)
    buf_ref[pl.ds(_HALO, tt), :] = u_ref[...]
    off = _HALO - (CONV_W - 1)
    sub = 8
    rows = tt
    for cc in range(D_CONV // LANES):
        cols = pl.ds(cc * LANES, LANES)
        for r0 in range(0, tt, rows):
            acc = jnp.zeros((rows, LANES), F32)
            for r in range(sub):
                taps = [k for k in range(CONV_W) if (off + k) % sub == r]
                n_shift = (rows + _HALO - r) // sub * sub
                sh_ref[r, pl.ds(0, n_shift), :] = buf_ref[pl.ds(r0 + r, n_shift), cols]
                for k in taps:
                    acc = acc + sh_ref[r, pl.ds(off + k - r, rows), :] * w_ref[pl.ds(k, 1), cols]
            acc_ref[pl.ds(r0, rows), cols] = acc + b_ref[:, cols]
    c = acc_ref[...]
    mu = jnp.mean(c, axis=-1, keepdims=True)
    var = jnp.mean(jnp.square(c - mu), axis=-1, keepdims=True)
    y = (c - mu) * lax.rsqrt(var + EPS) * g_ref[...] + beta_ref[...]
    o_ref[...] = _silu(y).astype(o_ref.dtype)


def _conv_prompt(u, w, b, g, beta, batch, seq, tt):
    nt = seq // tt
    vec = pl.BlockSpec((1, D_CONV), lambda bi, i: (0, 0))
    return pl.pallas_call(
        functools.partial(_conv_prompt_kernel, tt=tt),
        grid=(batch, nt),
        in_specs=[
            pl.BlockSpec((tt, D_CONV), lambda bi, i: (bi * nt + i, 0)),
            pl.BlockSpec((tt, D_CONV), lambda bi, i: (bi * nt + jnp.maximum(i - 1, 0), 0)),
            pl.BlockSpec((CONV_W, D_CONV), lambda bi, i: (0, 0)),
            vec, vec, vec,
        ],
        out_specs=pl.BlockSpec((tt, D_CONV), lambda bi, i: (bi * nt + i, 0)),
        out_shape=jax.ShapeDtypeStruct((batch * seq, D_CONV), BF16),
        scratch_shapes=[pltpu.VMEM((tt + _HALO, D_CONV), F32), pltpu.VMEM((tt, D_CONV), F32),
                        pltpu.VMEM((8, tt + _HALO, LANES), F32)],
        compiler_params=_params(("parallel", "parallel")),
        name="conv_prompt",
    )(u, u, w, b, g, beta)


def _conv_sample_kernel(st_ref, u_ref, w_ref, b_ref, g_ref, beta_ref, c_ref, new_ref):
    hist = CONV_W - 1
    u = u_ref[...]
    acc = u * w_ref[pl.ds(hist, 1), :] + b_ref[...]
    for t in range(hist):
        acc = acc + st_ref[t] * w_ref[pl.ds(t, 1), :]
    mu = jnp.mean(acc, axis=-1, keepdims=True)
    var = jnp.mean(jnp.square(acc - mu), axis=-1, keepdims=True)
    y = (acc - mu) * lax.rsqrt(var + EPS) * g_ref[...] + beta_ref[...]
    c_ref[...] = _silu(y).astype(c_ref.dtype)
    for t in range(hist - 1):
        new_ref[t] = st_ref[t + 1]
    new_ref[hist - 1] = u


def _conv_sample(state_t, u, w, b, g, beta):
    hist, bs, _ = state_t.shape
    return pl.pallas_call(
        _conv_sample_kernel,
        out_shape=[jax.ShapeDtypeStruct((bs, D_CONV), BF16), jax.ShapeDtypeStruct((hist, bs, D_CONV), F32)],
        compiler_params=pltpu.CompilerParams(vmem_limit_bytes=48 * MIB),
        name="conv_sample",
    )(state_t, u, w, b, g, beta)


def _bias_lookup(rb_ref, head, dist, valid):
    n = jnp.maximum(dist, 0)
    max_exact = N_BUCKETS // 2
    nf = jnp.maximum(n, 1).astype(F32)
    large = max_exact + (jnp.log(nf / max_exact) / math.log(MAX_DIST / max_exact)
                         * (N_BUCKETS - max_exact)).astype(I32)
    bucket = jnp.where(n < max_exact, n, jnp.minimum(large, N_BUCKETS - 1))
    val = jnp.zeros(dist.shape, F32)
    for k in range(N_BUCKETS):
        val = jnp.where(bucket == k, rb_ref[head, k], val)
    return jnp.where(valid, val, NEG)


def _bias_prompt_kernel(rb_ref, cmp_ref, win_ref, sel_ref, *, nqt, n_chunk):
    g = pl.program_id(0)
    key = lax.broadcasted_iota(I32, (LANES, QT), 0)
    qry = lax.broadcasted_iota(I32, (LANES, QT), 1)
    for hh in range(HPG):
        head = g * HPG + hh
        cols = pl.ds(hh * QT, QT)
        lookup = lambda dist, valid: _bias_lookup(rb_ref, head, dist, valid) * LOG2E

        for eb in range(_cmp_map_rows(nqt, n_chunk) // LANES):
            block = key + eb * LANES - _cmp_map_shift(nqt, 0)
            dist = qry - block * CMP_STRIDE - (CMP_LEN - 1)
            cmp_ref[0, pl.ds(eb * LANES, LANES), cols] = lookup(dist, dist >= 0)
        for jb in range((WINDOW + QT) // LANES):
            dist = qry - (key + jb * LANES) + WINDOW
            win_ref[0, pl.ds(jb * LANES, LANES), cols] = lookup(dist, (dist >= 0) & (dist <= WINDOW))
        for kind in range(3):
            for jb in range(2):
                dist = kind * QT + qry - (key + jb * LANES)
                sel_ref[0, kind, pl.ds(jb * LANES, LANES), cols] = lookup(dist, dist >= 0)
        far = jnp.full((LANES, QT), 2 * MAX_DIST, I32)
        for jb in range(2):
            sel_ref[0, 3, pl.ds(jb * LANES, LANES), cols] = lookup(far, far >= 0)


def _cmp_map_shift(nqt, qt):
    return (nqt - 1 - qt) * (QT // CMP_STRIDE)


def _cmp_map_rows(nqt, n_chunk):
    return -(-(n_chunk + _cmp_map_shift(nqt, 0)) // LANES) * LANES


def _bias_prompt(rb_t, nqt, n_chunk):
    ext = _cmp_map_rows(nqt, n_chunk)
    return pl.pallas_call(
        functools.partial(_bias_prompt_kernel, nqt=nqt, n_chunk=n_chunk),
        grid=(N_KV,),
        in_specs=[pl.BlockSpec(memory_space=pltpu.SMEM)],
        out_specs=[
            pl.BlockSpec((1, ext, GROUP_ROWS), lambda g: (g, 0, 0)),
            pl.BlockSpec((1, WINDOW + QT, GROUP_ROWS), lambda g: (g, 0, 0)),
            pl.BlockSpec((1, 4, 2 * LANES, GROUP_ROWS), lambda g: (g, 0, 0, 0)),
        ],
        out_shape=[
            jax.ShapeDtypeStruct((N_KV, ext, GROUP_ROWS), F32),
            jax.ShapeDtypeStruct((N_KV, WINDOW + QT, GROUP_ROWS), F32),
            jax.ShapeDtypeStruct((N_KV, 4, 2 * LANES, GROUP_ROWS), F32),
        ],
        compiler_params=_params(("parallel",)),
        name="bias_maps_prompt",
    )(rb_t)


def _bias_sample_kernel(rb_ref, cmp_ref, win_ref, sel_ref, *, past, n_chunk, n_pages):
    col = lax.broadcasted_iota(I32, (N_HEADS, LANES), 1)
    rb = rb_ref[...]

    def lookup(dist):
        n = jnp.maximum(dist, 0)
        max_exact = N_BUCKETS // 2
        nf = jnp.maximum(n, 1).astype(F32)
        large = max_exact + (jnp.log(nf / max_exact) / math.log(MAX_DIST / max_exact)
                             * (N_BUCKETS - max_exact)).astype(I32)
        bucket = jnp.where(n < max_exact, n, jnp.minimum(large, N_BUCKETS - 1))
        val = jnp.zeros(dist.shape, F32)
        for k in range(N_BUCKETS):
            val = jnp.where(bucket == k, rb[:, k:k + 1], val)
        return jnp.where(dist >= 0, val, NEG)

    def cmp_block(cb, carry):
        dist = past - (col + cb * LANES) * CMP_STRIDE - (CMP_LEN - 1)
        cmp_ref[cb] = lookup(dist)
        return carry

    lax.fori_loop(0, n_chunk // LANES, cmp_block, 0)

    def win_block(jb, carry):
        win_ref[jb] = lookup(WINDOW - (col + jb * LANES))
        return carry

    lax.fori_loop(0, WINDOW // LANES + 1, win_block, 0)

    def sel_page(p, carry):
        sel_ref[p] = lookup(past - (p * PAGE + col))
        return carry

    lax.fori_loop(0, n_pages + 1, sel_page, 0)


def _bias_sample(rb_t, past, n_chunk, n_pages):
    cmp_b, win_b, sel_b = pl.pallas_call(
        functools.partial(_bias_sample_kernel, past=past, n_chunk=n_chunk, n_pages=n_pages),
        out_shape=[
            jax.ShapeDtypeStruct((n_chunk // LANES, N_HEADS, LANES), F32),
            jax.ShapeDtypeStruct((WINDOW // LANES + 1, N_HEADS, LANES), F32),
            jax.ShapeDtypeStruct((n_pages + 1, N_HEADS, LANES), F32),
        ],
        name="bias_maps_sample",
    )(rb_t)
    flat = lambda a: a.transpose(1, 0, 2).reshape(N_HEADS, -1)
    return flat(cmp_b), flat(win_b), sel_b


_CHUNKS_PER_PAGE = PAGE // CMP_STRIDE
_CHUNK_PITCH = 24


def _compress_kernel(pt_ref, *refs, pp, n_riders):
    page_refs = refs[:pp + 1]
    w1k_ref, w1v_ref, pe_ref, w2k_ref, w2v_ref = refs[pp + 1:pp + 6]
    rider_in = refs[pp + 6:pp + 6 + n_riders]
    o_ref = refs[pp + 6 + n_riders]
    rider_out = refs[pp + 7 + n_riders:pp + 7 + 2 * n_riders]
    tr_ref = refs[pp + 7 + 2 * n_riders]
    del pt_ref
    for src, dst in zip(rider_in, rider_out):
        dst[...] = src[...].astype(dst.dtype)
    nch = pp * _CHUNKS_PER_PAGE
    m = nch + 8
    n_slab = D_KV // LANES

    def transpose(s):
        for k in range(pp + 1):
            t = page_refs[k][pl.ds(s * LANES, LANES), :].T
            for n in range(_CHUNKS_PER_PAGE):
                row = (k * _CHUNKS_PER_PAGE + n) * _CHUNK_PITCH
                tr_ref[s, pl.ds(row, CMP_STRIDE), :] = t[n * CMP_STRIDE:(n + 1) * CMP_STRIDE]

    def chunks(s, _):
        return jnp.concatenate(
            [tr_ref.at[s][pl.ds(p, m, stride=_CHUNK_PITCH), :] for p in range(CMP_STRIDE)], axis=1)

    def mlp(s, x):
        is_v = int(s >= n_slab // 2)
        w1 = (w1v_ref if is_v else w1k_ref)[...]
        w2 = (w2v_ref if is_v else w2k_ref)[...]
        y = _mm(x.astype(BF16), w1)
        y_pe = _mm(pe_ref[...].astype(BF16), w1)
        pe_term = y_pe[2 * is_v:2 * is_v + 1, 0:LANES] + y_pe[2 * is_v + 1:2 * is_v + 2, LANES:2 * LANES]
        pre = y[0:nch, 0:LANES] + y[1:nch + 1, LANES:2 * LANES] + pe_term
        o_ref[0, :, pl.ds(s * LANES, LANES)] = _mm(_gelu_tanh(pre).astype(BF16), w2)

    _staggered(n_slab, transpose, chunks, mlp)


def _compress(pages, page_ids, weights, nb, n_pages, pp, paged, riders=()):
    steps = n_pages // pp
    nch = pp * _CHUNKS_PER_PAGE
    ride = _rider_specs(riders, nb * steps, lambda b, s, pt: (b * steps + s, 0))

    def page_spec(k):
        if paged:
            return pl.BlockSpec(
                (None, D_KV, PAGE),
                lambda b, s, pt: (pt[b * n_pages + jnp.minimum(s * pp + k, n_pages - 1)], 0, 0))
        return pl.BlockSpec((None, D_KV, PAGE), lambda b, s, pt: (b, 0, jnp.minimum(s * pp + k, n_pages - 1)))

    full = lambda a: pl.BlockSpec(a.shape, lambda b, s, pt: (0,) * a.ndim)
    grid_spec = pltpu.PrefetchScalarGridSpec(
        num_scalar_prefetch=1,
        grid=(nb, steps),
        in_specs=[page_spec(k) for k in range(pp + 1)] + [full(a) for a in weights] + ride,
        out_specs=[pl.BlockSpec((1, nch, D_KV), lambda b, s, pt: (b, s, 0))] + ride,
        scratch_shapes=[pltpu.VMEM((D_KV // LANES, (pp + 1) * _CHUNKS_PER_PAGE * _CHUNK_PITCH, LANES), F32)],
    )
    out = pl.pallas_call(
        functools.partial(_compress_kernel, pp=pp, n_riders=len(riders)),
        grid_spec=grid_spec,
        out_shape=[jax.ShapeDtypeStruct((nb, n_pages * _CHUNKS_PER_PAGE, D_KV), F32)]
        + [jax.ShapeDtypeStruct(a.shape, BF16) for a in riders],
        compiler_params=_params(("parallel", "parallel"), 56),
        name="compress_paged" if paged else "compress_prompt",
    )(page_ids, *([pages] * (pp + 1)), *weights, *riders)
    return out[0], out[1:]


def _compress_weights(pe_k, w1_k, w2_k, pe_v, w1_v, w2_v):
    eye = jnp.eye(2, dtype=F32)

    def first(w1, j):
        w = w1[j * CMP_STRIDE:(j + 1) * CMP_STRIDE]
        return jnp.einsum("pdf,gh->pgdhf", w, eye).reshape(CMP_STRIDE * LANES, 2 * CMP_HID).astype(BF16)

    def pos(pe, j):
        return jnp.tile(pe[j * CMP_STRIDE:(j + 1) * CMP_STRIDE, None, :], (1, 2, 1)).reshape(1, CMP_STRIDE * LANES)

    def second(w2):
        return jnp.einsum("fd,gh->gfhd", w2, eye).reshape(2 * CMP_HID, 2 * HEAD_DIM).astype(BF16)

    pe = jnp.concatenate([pos(pe_k, 0), pos(pe_k, 1), pos(pe_v, 0), pos(pe_v, 1)], axis=0)
    pe = jnp.pad(pe, ((0, 8 - pe.shape[0]), (0, 0)))
    both = lambda w1: jnp.concatenate([first(w1, 0), first(w1, 1)], axis=1)
    return (both(w1_k), both(w1_v), pe, second(w2_k), second(w2_v))


def _staggered(n, scores, softmax, values, ahead=1, lag=0):
    s = {g: scores(g) for g in range(min(ahead, n))}
    p = {}
    out = []
    for g in range(n + lag):
        if g + ahead < n:
            s[g + ahead] = scores(g + ahead)
        if g < n:
            p[g] = softmax(g, s.pop(g))
        if g - lag >= 0:
            out.append(values(g - lag, p.pop(g - lag)))
    return out


def _group_queries(qt_ref, g):
    return jnp.concatenate(
        [qt_ref[pl.ds((g * HPG + hh) * HEAD_DIM, HEAD_DIM), :] for hh in range(HPG)], axis=1)


def _store_heads(o_ref, o_t, gate, g, branch):
    lane = lax.broadcasted_iota(I32, (QT, 2 * HEAD_DIM), 1)
    for pair in range(HPG // 2):
        h = g * HPG + 2 * pair
        col = branch * N_HEADS + h
        both = jnp.concatenate([o_t[:, (2 * pair) * QT:(2 * pair + 1) * QT],
                                o_t[:, (2 * pair + 1) * QT:(2 * pair + 2) * QT]], axis=0)
        gates = jnp.where(lane < HEAD_DIM, gate[:, col:col + 1], gate[:, col + 1:col + 2])
        o_ref[:, pl.ds(h * HEAD_DIM, 2 * HEAD_DIM)] = both.T * gates


def _cmp_prompt_kernel(qt_ref, kc_ref, map_ref, gate_ref, ov_ref, o_ref, sb_ref, *, n_sel, top):
    qt = pl.program_id(1)
    kv = kc_ref[0]
    gate = gate_ref[...]
    ov = ov_ref[...]
    half = D_KV // 2
    n_chunk = kv.shape[0]
    map_rows = pl.ds(pl.multiple_of(_cmp_map_shift(pl.num_programs(1), qt), 8), n_chunk)
    bias = lambda g: map_ref[g, map_rows, :]
    vt_all = kv[:, half:].T
    jrow = lax.broadcasted_iota(I32, (n_sel, QT), 0)
    tpos = qt * QT + lax.broadcasted_iota(I32, (n_sel, QT), 1)
    cur = tpos // SEL_LEN
    forced = (jrow == 0) | (jrow == cur) | (jrow == cur - 1)
    valid = jrow * SEL_LEN <= tpos
    def scores(g):
        kg = kv[:, g * HEAD_DIM:(g + 1) * HEAD_DIM].astype(BF16)
        return _mm(kg, _group_queries(qt_ref, g)) + bias(g)

    def softmax(g, st):
        e = jnp.exp2(st - jnp.max(st, axis=0, keepdims=True))
        p = e / jnp.sum(e, axis=0, keepdims=True) * (bias(g) > 0.5 * NEG).astype(F32)
        return p.astype(BF16)

    def values(g, pb):
        vgt = vt_all[g * HEAD_DIM:(g + 1) * HEAD_DIM].astype(BF16)
        _store_heads(o_ref, _mm(vgt, pb), gate, g, 0)
        return pb

    probs = _staggered(N_KV, scores, softmax, values, ahead=2, lag=1)
    for g in range(N_KV):
        pb = probs[g]
        imp = jnp.zeros((n_sel, QT), F32)
        for hh in range(HPG):
            imp = imp + _mm(ov, pb[:, hh * QT:(hh + 1) * QT])
        score = jnp.where(forced, jnp.inf, jnp.where(valid, imp, -jnp.inf))
        rank = jnp.zeros((n_sel, QT), I32)
        for i in range(n_sel):
            si = score[i:i + 1, :]
            ahead = (si > score) | ((si == score) & (jrow > i))
            rank = rank + ahead.astype(I32)
        sb_ref[pl.ds(g * n_sel, n_sel), :] = jnp.where(rank < top, 0.0, NEG).astype(sb_ref.dtype)


def _cmp_prompt(q_t, kcmp, cmap, gates, ov_t, batch, seq):
    nqt = seq // QT
    n_chunk = kcmp.shape[1]
    n_sel = ov_t.shape[0]
    top = min(SEL_TOP, n_sel)
    return pl.pallas_call(
        functools.partial(_cmp_prompt_kernel, n_sel=n_sel, top=top),
        grid=(batch, nqt),
        in_specs=[
            pl.BlockSpec((D_Q, QT), lambda b, i: (0, b * nqt + i)),
            pl.BlockSpec((1, n_chunk, D_KV), lambda b, i: (b, 0, 0)),
            pl.BlockSpec(cmap.shape, lambda b, i: (0, 0, 0)),
            pl.BlockSpec((QT, LANES), lambda b, i: (b * nqt + i, 0)),
            pl.BlockSpec(ov_t.shape, lambda b, i: (0, 0)),
        ],
        out_specs=[
            pl.BlockSpec((QT, D_Q), lambda b, i: (b * nqt + i, 0)),
            pl.BlockSpec((None, N_KV * n_sel, QT), lambda b, i: (b * nqt + i, 0, 0)),
        ],
        out_shape=[
            jax.ShapeDtypeStruct((batch * seq, D_Q), F32),
            jax.ShapeDtypeStruct((batch * nqt, N_KV * n_sel, QT), BF16),
        ],
        compiler_params=_params(("parallel", "parallel")),
        name="cmp_attn_prompt",
    )(q_t, kcmp, cmap, gates, ov_t)


_TK = 2 * LANES


def _sel_prompt_kernel(qt_ref, kv_ref, sb_ref, map_ref, gate_ref, *rest, n_sel, seq, n_riders):
    rider_in = rest[:n_riders]
    o_ref = rest[n_riders]
    rider_out = rest[n_riders + 1:2 * n_riders + 1]
    ka_ref, vt_ref, qa_ref = rest[2 * n_riders + 1:]
    for src, dst in zip(rider_in, rider_out):
        dst[...] = src[...].astype(dst.dtype)
    qt = pl.program_id(1)
    aug = 2 * HEAD_DIM
    half = D_KV // 2

    @pl.when(qt == 0)
    def _():
        pos = lax.broadcasted_iota(I32, (seq, aug - HEAD_DIM), 0)
        blk = lax.broadcasted_iota(I32, (seq, aug - HEAD_DIM), 1)
        onehot = jnp.where(pos // SEL_LEN == blk, 1.0, 0.0)
        k_rows = kv_ref[0, pl.ds(0, half), :].T
        for g in range(N_KV):
            ka_ref[g] = jnp.concatenate([k_rows[:, g * HEAD_DIM:(g + 1) * HEAD_DIM], onehot], axis=1).astype(BF16)
            vt_ref[g] = kv_ref[0, pl.ds(half + g * HEAD_DIM, HEAD_DIM), :].astype(BF16)

    gate = gate_ref[...]
    sb = sb_ref[...]
    diag = qt // 2
    n_tiles = diag + 1
    pad = jnp.zeros((aug - HEAD_DIM - n_sel, GROUP_ROWS), BF16)
    for g in range(N_KV):
        sbg = sb[g * n_sel:(g + 1) * n_sel]
        qa_ref[g] = jnp.concatenate([_group_queries(qt_ref, g), jnp.concatenate([sbg] * HPG, axis=1), pad], axis=0)

    def tile(kt, carry):
        start = pl.multiple_of(kt * _TK, _TK)
        kind = jnp.where(kt == diag, qt % 2, jnp.where((kt == diag - 1) & (qt % 2 == 0), 2, 3))
        def scores(g):
            return _mm(ka_ref[g, pl.ds(start, _TK), :], qa_ref[g]) + map_ref[g, kind]

        def softmax(g, st):
            m_i, l_i, _ = carry[g]
            m_new = jnp.maximum(m_i, jnp.max(st, axis=0, keepdims=True))
            alpha = jnp.exp2(m_i - m_new)
            p = jnp.exp2(st - m_new)
            return m_new, alpha * l_i + jnp.sum(p, axis=0, keepdims=True), alpha, p.astype(BF16)

        def values(g, sm):
            m_new, l_new, alpha, pb = sm
            return m_new, l_new, alpha * carry[g][2] + _mm(vt_ref[g, :, pl.ds(start, _TK)], pb)

        return tuple(_staggered(N_KV, scores, softmax, values, ahead=2, lag=1))

    init = tuple((jnp.full((1, GROUP_ROWS), -jnp.inf, F32), jnp.zeros((1, GROUP_ROWS), F32),
                  jnp.zeros((HEAD_DIM, GROUP_ROWS), F32)) for _ in range(N_KV))
    final = lax.fori_loop(0, n_tiles, tile, init)
    for g in range(N_KV):
        _, l_f, acc = final[g]
        _store_heads(o_ref, acc / l_f, gate, g, 1)


def _sel_prompt(q_t, kv_t, selbias, smap, gates, batch, seq, riders=()):
    nqt = seq // QT
    n_sel = selbias.shape[1] // N_KV
    ride = _rider_specs(riders, batch * nqt, lambda b, i: (b * nqt + i, 0))
    out = pl.pallas_call(
        functools.partial(_sel_prompt_kernel, n_sel=n_sel, seq=seq, n_riders=len(riders)),
        grid=(batch, nqt),
        in_specs=[
            pl.BlockSpec((D_Q, QT), lambda b, i: (0, b * nqt + i)),
            pl.BlockSpec((1, D_KV, seq), lambda b, i: (b, 0, 0)),
            pl.BlockSpec((None, N_KV * n_sel, QT), lambda b, i: (b * nqt + i, 0, 0)),
            pl.BlockSpec(smap.shape, lambda b, i: (0, 0, 0, 0)),
            pl.BlockSpec((QT, LANES), lambda b, i: (b * nqt + i, 0)),
        ] + ride,
        out_specs=[pl.BlockSpec((QT, D_Q), lambda b, i: (b * nqt + i, 0))] + ride,
        out_shape=[jax.ShapeDtypeStruct((batch * seq, D_Q), F32)]
        + [jax.ShapeDtypeStruct(a.shape, BF16) for a in riders],
        scratch_shapes=[pltpu.VMEM((N_KV, seq, 2 * HEAD_DIM), BF16), pltpu.VMEM((N_KV, HEAD_DIM, seq), BF16),
                        pltpu.VMEM((N_KV, 2 * HEAD_DIM, GROUP_ROWS), BF16)],
        compiler_params=_params(("arbitrary", "arbitrary"), 56),
        name="sel_attn_prompt",
    )(q_t, kv_t, selbias, smap, gates, *riders)
    return out[0], out[1:]


_WIN_TILES = WINDOW // QT + 1


def _win_prompt_kernel(qt_ref, *refs):
    kv_refs = refs[:_WIN_TILES]
    map_ref, gate_ref, o_ref = refs[_WIN_TILES:]
    qt = pl.program_id(1)
    gate = gate_ref[...]
    half = D_KV // 2
    span = _WIN_TILES * QT
    key = lax.broadcasted_iota(I32, (span, GROUP_ROWS), 0)
    in_seq = key >= (_WIN_TILES - 1 - qt) * QT
    k_rows = jnp.concatenate([r[0, pl.ds(0, half), :].T for r in kv_refs], axis=0)
    def scores(g):
        kg = k_rows[:, g * HEAD_DIM:(g + 1) * HEAD_DIM].astype(BF16)
        return jnp.where(in_seq, _mm(kg, _group_queries(qt_ref, g)) + map_ref[g], NEG)

    def softmax(g, st):
        e = jnp.exp2(st - jnp.max(st, axis=0, keepdims=True))
        return (e / jnp.sum(e, axis=0, keepdims=True)).astype(BF16)

    def values(g, pb):
        vgt = jnp.concatenate([r[0, pl.ds(half + g * HEAD_DIM, HEAD_DIM), :] for r in kv_refs],
                              axis=1).astype(BF16)
        _store_heads(o_ref, _mm(vgt, pb), gate, g, 2)

    _staggered(N_KV, scores, softmax, values, ahead=2, lag=1)


def _win_prompt(q_t, kv_t, wmap, gates, batch, seq):
    nqt = seq // QT

    def kv_spec(k):
        return pl.BlockSpec((1, D_KV, QT), lambda b, i: (b, 0, jnp.maximum(i - (_WIN_TILES - 1) + k, 0)))

    return pl.pallas_call(
        _win_prompt_kernel,
        grid=(batch, nqt),
        in_specs=[pl.BlockSpec((D_Q, QT), lambda b, i: (0, b * nqt + i))]
        + [kv_spec(k) for k in range(_WIN_TILES)]
        + [pl.BlockSpec(wmap.shape, lambda b, i: (0, 0, 0)),
           pl.BlockSpec((QT, LANES), lambda b, i: (b * nqt + i, 0))],
        out_specs=pl.BlockSpec((QT, D_Q), lambda b, i: (b * nqt + i, 0)),
        out_shape=jax.ShapeDtypeStruct((batch * seq, D_Q), F32),
        compiler_params=_params(("parallel", "parallel")),
        name="win_attn_prompt",
    )(q_t, *([kv_t] * _WIN_TILES), wmap, gates)


def _group_rows(x_by_group):
    row = lax.broadcasted_iota(I32, x_by_group[0].shape, 0)
    out = x_by_group[0]
    for g in range(1, N_KV):
        out = jnp.where(row // HPG == g, x_by_group[g], out)
    return out


_CMP_SAMPLE_ROWS = 4


def _cmp_sample_kernel(q_ref, kc_ref, bias_ref, ov_ref, gate_ref, o_ref, idx_ref, *, n_sel, past):
    bias = bias_ref[...]
    half = D_KV // 2
    imps = []
    for bl in range(_CMP_SAMPLE_ROWS):
        q = q_ref[bl]
        kv = kc_ref[bl].astype(BF16)
        s = _group_rows([_mm_nt(q, kv[:, g * HEAD_DIM:(g + 1) * HEAD_DIM]) for g in range(N_KV)]) + bias
        e = jnp.exp(s - jnp.max(s, axis=-1, keepdims=True))
        p = e / jnp.sum(e, axis=-1, keepdims=True) * (bias > 0.5 * NEG).astype(F32)
        pb = p.astype(BF16)
        o = _group_rows([_mm(pb, kv[:, half + g * HEAD_DIM:half + (g + 1) * HEAD_DIM]) for g in range(N_KV)])
        o_ref[bl] = o * gate_ref[bl]
        imp_h = _mm(pb, ov_ref[...])
        imps += [imp_h[g * HPG:g * HPG + 1] + imp_h[g * HPG + 1:g * HPG + 2] + imp_h[g * HPG + 2:g * HPG + 3]
                 + imp_h[g * HPG + 3:g * HPG + 4] for g in range(N_KV)]
    imp = jnp.concatenate(imps, axis=0)
    rows, nsp = imp.shape
    j = lax.broadcasted_iota(I32, (rows, nsp), 1)
    cur = past // SEL_LEN
    forced = (j == 0) | (j == cur) | (j == cur - 1)
    valid = (j * SEL_LEN <= past) & (j < n_sel)
    score = jnp.where(valid, jnp.where(forced, jnp.inf, imp), -jnp.inf)
    slot = lax.broadcasted_iota(I32, (rows, LANES), 1)
    res = jnp.zeros((rows, LANES), I32)
    for k in range(min(SEL_TOP, n_sel)):
        best = jnp.max(score, axis=-1, keepdims=True)
        pick = jnp.min(jnp.where(score == best, j, nsp), axis=-1, keepdims=True)
        res = jnp.where(slot == k, pick, res)
        score = jnp.where(j == pick, -jnp.inf, score)
    idx_ref[...] = res


def _cmp_sample(q3, kcmp, bias, ov, gate3, n_sel, past):
    bs, n_chunk, _ = kcmp.shape
    nb = _CMP_SAMPLE_ROWS
    heads = pl.BlockSpec((nb, N_HEADS, HEAD_DIM), lambda b: (b, 0, 0))
    return pl.pallas_call(
        functools.partial(_cmp_sample_kernel, n_sel=n_sel, past=past),
        grid=(bs // nb,),
        in_specs=[
            heads,
            pl.BlockSpec((nb, n_chunk, D_KV), lambda b: (b, 0, 0)),
            pl.BlockSpec(bias.shape, lambda b: (0, 0)),
            pl.BlockSpec(ov.shape, lambda b: (0, 0)),
            heads,
        ],
        out_specs=[heads, pl.BlockSpec((nb * N_KV, LANES), lambda b: (b, 0))],
        out_shape=[
            jax.ShapeDtypeStruct((bs, N_HEADS, HEAD_DIM), F32),
            jax.ShapeDtypeStruct((bs * N_KV, LANES), I32),
        ],
        compiler_params=_params(("parallel",)),
        name="cmp_attn_sample",
    )(q3, kcmp, bias, ov, gate3)


def _sel_sample_kernel(blk_ref, phys_ref, q_ref, new_ref, bias_ref, gate_ref, *refs, top, n_pages):
    kv_refs = refs[:top]
    k_refs = [r.at[0] for r in kv_refs]
    v_refs = [r.at[1] for r in kv_refs]
    o_ref = refs[top]
    del phys_ref
    b = pl.program_id(0)
    g = pl.program_id(1)
    q = q_ref[0, 0]
    lane = lax.broadcasted_iota(I32, (8, PAGE), 1)
    scores = []
    for k in range(top):
        blk = blk_ref[(b * N_KV + g) * top + k]
        page = jnp.minimum(blk // 2, n_pages)
        s = _mm(q, k_refs[k][...].astype(BF16)) + bias_ref[page]
        ok = (lane // SEL_LEN == blk % 2) & (blk < 2 * n_pages)
        scores.append(jnp.where(ok, s, NEG))
    k_new = new_ref[0, 0, 0:1, :].astype(BF16).astype(F32)
    v_new = new_ref[0, 0, 1:2, :].astype(BF16).astype(F32)
    s_new = jnp.sum(q.astype(F32) * k_new, axis=-1, keepdims=True) + bias_ref[n_pages]
    scores.append(s_new)
    s_all = jnp.concatenate(scores, axis=1)
    e = jnp.exp(s_all - jnp.max(s_all, axis=-1, keepdims=True))
    p = e / jnp.sum(e, axis=-1, keepdims=True)
    o = p[:, top * PAGE:top * PAGE + 1].astype(BF16).astype(F32) * v_new
    for k in range(top):
        o = o + _mm_nt(p[:, k * PAGE:(k + 1) * PAGE].astype(BF16), v_refs[k][...].astype(BF16))
    o_ref[0, 0] = o * gate_ref[0, 0]


def _sel_sample(blk, phys, q4, new4, bias4, gate4, pool5, top, n_pages):
    bs = q4.shape[0]

    def kv_spec(k):
        return pl.BlockSpec(
            (None, 2, None, HEAD_DIM, PAGE),
            lambda b, g, blk_r, phys_r: (phys_r[(b * N_KV + g) * top + k], 0, g, 0, 0))

    grp = lambda b, g, blk_r, phys_r: (b, g, 0, 0)
    grid_spec = pltpu.PrefetchScalarGridSpec(
        num_scalar_prefetch=2,
        grid=(bs, N_KV),
        in_specs=[
            pl.BlockSpec((1, 1, 8, HEAD_DIM), grp),
            pl.BlockSpec((1, 1, 8, HEAD_DIM), grp),
            pl.BlockSpec((None, n_pages + 1, 8, LANES), lambda b, g, blk_r, phys_r: (g, 0, 0, 0)),
            pl.BlockSpec((1, 1, 8, HEAD_DIM), grp),
        ] + [kv_spec(k) for k in range(top)],
        out_specs=pl.BlockSpec((1, 1, 8, HEAD_DIM), grp),
    )
    return pl.pallas_call(
        functools.partial(_sel_sample_kernel, top=top, n_pages=n_pages),
        grid_spec=grid_spec,
        out_shape=jax.ShapeDtypeStruct((bs, N_KV, 8, HEAD_DIM), F32),
        compiler_params=_params(("parallel", "parallel")),
        name="sel_attn_sample",
    )(blk, phys, q4, new4, bias4, gate4, *([pool5] * top))


def _win_sample_kernel(q_ref, new_ref, st_ref, col_ref, bias_ref, gate_ref, o_ref, nst_ref):
    st = st_ref[0]
    half = D_KV // 2
    for g in range(N_KV):
        q = q_ref[0, g]
        bias = bias_ref[g]
        s_old = _mm(q, st[g * HEAD_DIM:(g + 1) * HEAD_DIM, :].astype(BF16)) + bias[:, 0:WINDOW]
        k_new = new_ref[0, g, 0:1, :].astype(BF16).astype(F32)
        v_new = new_ref[0, g, 1:2, :].astype(BF16).astype(F32)
        s_new = jnp.sum(q.astype(F32) * k_new, axis=-1, keepdims=True) + bias[:, WINDOW:WINDOW + LANES]
        s_all = jnp.concatenate([s_old, s_new], axis=1)
        e = jnp.exp(s_all - jnp.max(s_all, axis=-1, keepdims=True))
        p = e / jnp.sum(e, axis=-1, keepdims=True)
        o = _mm_nt(p[:, 0:WINDOW].astype(BF16), st[half + g * HEAD_DIM:half + (g + 1) * HEAD_DIM, :].astype(BF16))
        o = o + p[:, WINDOW:WINDOW + 1].astype(BF16).astype(F32) * v_new
        o_ref[0, g] = o * gate_ref[0, g]
    rolled = pltpu.roll(st, WINDOW - 1, axis=1)
    lane = lax.broadcasted_iota(I32, (D_KV, LANES), 1)
    nst_ref[0, :, pl.ds(0, WINDOW - LANES)] = rolled[:, 0:WINDOW - LANES]
    nst_ref[0, :, pl.ds(WINDOW - LANES, LANES)] = jnp.where(
        lane == LANES - 1, col_ref[0], rolled[:, WINDOW - LANES:WINDOW])


def _win_sample(q4, new4, state_t, newcol, bias4, gate4):
    bs = q4.shape[0]
    grp = pl.BlockSpec((1, N_KV, 8, HEAD_DIM), lambda b: (b, 0, 0, 0))
    return pl.pallas_call(
        _win_sample_kernel,
        grid=(bs,),
        in_specs=[
            grp, grp,
            pl.BlockSpec((1, D_KV, WINDOW), lambda b: (b, 0, 0)),
            pl.BlockSpec((1, D_KV, LANES), lambda b: (b, 0, 0)),
            pl.BlockSpec(bias4.shape, lambda b: (0, 0, 0)),
            grp,
        ],
        out_specs=[grp, pl.BlockSpec((1, D_KV, WINDOW), lambda b: (b, 0, 0))],
        out_shape=[
            jax.ShapeDtypeStruct((bs, N_KV, 8, HEAD_DIM), F32),
            jax.ShapeDtypeStruct((bs, D_KV, WINDOW), F32),
        ],
        compiler_params=_params(("parallel",)),
        name="win_attn_sample",
    )(q4, new4, state_t, newcol, bias4, gate4)


_MOE_ROWS = 256
_MOE_SLACK = 2 * _MOE_ROWS


def _row_copy(src, src_row, dst, dst_row, sem):
    return pltpu.make_async_copy(src.at[pl.ds(src_row, 1)], dst.at[pl.ds(dst_row, 1)], sem)


def _expert_kernel(be_ref, nu_ref, dst_ref, src_ref, nxt_ref, h_ref, w_ref, wg_ref, wu_ref, wd_ref, yk_ref,
                   xbuf, ybuf, gsem, ssem, *, n_rows, n_live):
    del be_ref
    i = pl.program_id(0)
    slot = i % 2
    n_used = nu_ref[0]

    def gather(rows_ref, dst_slot):
        for r in range(_MOE_ROWS):
            _row_copy(h_ref, rows_ref[0, r], xbuf.at[dst_slot], r, gsem.at[dst_slot]).start()

    def wait_gather(s):
        for r in range(_MOE_ROWS):
            _row_copy(h_ref, 0, xbuf.at[s], 0, gsem.at[s]).wait()

    def scatter(s):
        for r in range(_MOE_ROWS):
            _row_copy(ybuf.at[s], r, yk_ref, dst_ref[0, r], ssem.at[s]).start()

    def wait_scatter(s):
        for r in range(_MOE_ROWS):
            _row_copy(ybuf.at[s], 0, yk_ref, 0, ssem.at[s]).wait()

    @pl.when(i == 0)
    def _():
        gather(src_ref, 0)
        ybuf[1] = jnp.zeros((_MOE_ROWS, ybuf.shape[2]), F32)
        fills = [
            pltpu.make_async_copy(ybuf.at[1, pl.ds(0, min(_MOE_ROWS, n_rows - off))],
                                  yk_ref.at[pl.ds(k * n_rows + off, min(_MOE_ROWS, n_rows - off))], ssem.at[1])
            for k in range(TOP_K) for off in range(n_live, n_rows, _MOE_ROWS)]
        for cp in fills:
            cp.start()
        for cp in fills:
            cp.wait()

    @pl.when(i + 1 < n_used)
    def _():
        gather(nxt_ref, 1 - slot)

    @pl.when(i < n_used)
    def _():
        wait_gather(slot)

        @pl.when(i >= 2)
        def _():
            wait_scatter(slot)

        x = xbuf[slot].astype(BF16)
        a = _silu(_mm(x, wg_ref[...])) * _mm(x, wu_ref[...])
        ybuf[slot] = _mm(a.astype(BF16), wd_ref[...]) * w_ref[...]
        scatter(slot)

        @pl.when(i == n_used - 1)
        def _():
            @pl.when(i >= 1)
            def _():
                wait_scatter(1 - slot)

            wait_scatter(slot)


def _experts(blk_e, n_used, rows_src, rows_dst, rows_w, h2_all, n_live, w_gate, w_up, w_down):
    n_rows, d = h2_all.shape
    n_blk = blk_e.shape[0]
    src3 = rows_src.reshape(n_blk, 1, _MOE_ROWS)
    dst3 = rows_dst.reshape(n_blk, 1, _MOE_ROWS)
    wspec = lambda a: pl.BlockSpec((None,) + a.shape[1:], lambda i, be, nu: (be[i], 0, 0))
    smem_blk = lambda fn: pl.BlockSpec((None, 1, _MOE_ROWS), fn, memory_space=pltpu.SMEM)
    grid_spec = pltpu.PrefetchScalarGridSpec(
        num_scalar_prefetch=2,
        grid=(n_blk,),
        in_specs=[
            smem_blk(lambda i, be, nu: (i, 0, 0)),
            smem_blk(lambda i, be, nu: (i, 0, 0)),
            smem_blk(lambda i, be, nu: (jnp.minimum(i + 1, n_blk - 1), 0, 0)),
            pl.BlockSpec(memory_space=pl.ANY),
            pl.BlockSpec((_MOE_ROWS, 1), lambda i, be, nu: (i, 0)),
            wspec(w_gate), wspec(w_up), wspec(w_down),
        ],
        out_specs=pl.BlockSpec(memory_space=pl.ANY),
        scratch_shapes=[
            pltpu.VMEM((2, _MOE_ROWS, d), F32),
            pltpu.VMEM((2, _MOE_ROWS, d), F32),
            pltpu.SemaphoreType.DMA((2,)),
            pltpu.SemaphoreType.DMA((2,)),
        ],
    )
    return pl.pallas_call(
        functools.partial(_expert_kernel, n_rows=n_rows, n_live=n_live),
        grid_spec=grid_spec,
        out_shape=jax.ShapeDtypeStruct((TOP_K * n_rows, d), F32),
        compiler_params=pltpu.CompilerParams(dimension_semantics=("arbitrary",), vmem_limit_bytes=56 * MIB,
                                             disable_bounds_checks=True),
        name="moe_experts",
    )(blk_e, n_used, dst3, src3, src3, h2_all, rows_w, w_gate, w_up, w_down)


def _combine_kernel(x_ref, y0_ref, y1_ref, g_ref, o_ref):
    x = x_ref[...] + (y0_ref[...] + y1_ref[...])
    ms = jnp.mean(x * x, axis=-1, keepdims=True)
    o_ref[...] = x * lax.rsqrt(ms + EPS) * g_ref[...]


def _combine(x1, yk, gfin, row0, tm):
    n, d = x1.shape
    yk3 = yk.reshape(TOP_K, -1, d)
    blk0 = row0 // tm
    return pl.pallas_call(
        _combine_kernel,
        grid=(n // tm,),
        in_specs=[
            pl.BlockSpec((tm, d), lambda i: (i, 0)),
            pl.BlockSpec((None, tm, d), lambda i: (0, blk0 + i, 0)),
            pl.BlockSpec((None, tm, d), lambda i: (1, blk0 + i, 0)),
            pl.BlockSpec((1, d), lambda i: (0, 0)),
        ],
        out_specs=pl.BlockSpec((tm, d), lambda i: (i, 0)),
        out_shape=jax.ShapeDtypeStruct((n, d), F32),
        compiler_params=_params(("parallel",)),
        name="moe_combine",
    )(x1, yk3, yk3, gfin)


def _dispatch(eid, w, counts, n_rows):
    a = eid.shape[0]
    tb = _MOE_ROWS
    order = jnp.argsort(eid).astype(I32)
    padded = (counts + tb - 1) // tb * tb
    seg_end = jnp.cumsum(padded)
    seg_start = seg_end - padded
    start = jnp.cumsum(counts) - counts
    n_blk = (a + N_EXPERTS * (tb - 1) + tb - 1) // tb
    blk_e = jnp.minimum(jnp.sum((seg_end[None, :] <= (jnp.arange(n_blk) * tb)[:, None]).astype(I32), axis=1),
                        N_EXPERTS - 1)
    blk = jnp.arange(n_blk)[:, None]
    r = jnp.arange(tb)[None, :]
    off = (blk * tb - seg_start[blk_e][:, None]) + r
    valid = off < counts[blk_e][:, None]
    asg = order[jnp.clip(start[blk_e][:, None] + off, 0, a - 1)]
    rows_src = jnp.where(valid, asg // TOP_K, 0).astype(I32)
    rows_dst = jnp.where(valid, (asg % TOP_K) * n_rows + asg // TOP_K, (blk % 2) * n_rows + (n_rows - tb) + r)
    rows_w = jnp.where(valid, w[asg], 0.0)
    n_used = (seg_end[-1] // tb).astype(I32).reshape(1)
    return rows_src, rows_dst.astype(I32), rows_w.reshape(-1, 1), blk_e.astype(I32), n_used


def _overlap(n_chunk, n_sel, n_sel_pad):
    cs = jnp.arange(n_chunk)[:, None] * CMP_STRIDE
    ss = jnp.arange(n_sel_pad)[None, :] * SEL_LEN
    hit = (cs < ss + SEL_LEN) & (cs + CMP_LEN > ss) & (jnp.arange(n_sel_pad)[None, :] < n_sel)
    return hit.astype(BF16)


def kernel(x_prompt, x_sample, cache_cmp_kv, cache_sel_kv, state_win_kv, state_conv, page_table, rel_bias,
           norm_mix, w_in, b_in, conv_w, conv_b, conv_ln_g, conv_ln_b, w_conv_out, b_conv_out,
           cmp_pe_k, cmp_w1_k, cmp_w2_k, cmp_pe_v, cmp_w1_v, cmp_w2_v, w_nsa_out, w_out, norm_ffn,
           w_rg, b_rg, w_re, b_re, w_gate, w_up, w_down, norm_final):
    batch, seq, _ = x_prompt.shape
    bs = x_sample.shape[0]
    n_pool = cache_cmp_kv.shape[1]
    n_pages = page_table.shape[1]
    past = n_pages * PAGE
    n_tok = batch * seq
    assert bs + _MOE_ROWS <= _MOE_SLACK and n_tok % bs == 0 and seq % _TK == 0 and bs % 16 == 0
    l = 0

    wt = w_in[l].T.astype(BF16)
    bias_in = b_in[l]
    cuts = [0, D_CONV, 2 * D_CONV, 2 * D_CONV + D_Q, 2 * D_CONV + D_Q + 3 * D_KV]
    cuts += [cuts[-1] + 3 * N_HEADS, cuts[-1] + 3 * N_HEADS + D_MODEL, cuts[-1] + 3 * N_HEADS + 2 * D_MODEL]
    seg = lambda k: (wt[cuts[k]:cuts[k + 1]], bias_in[cuts[k]:cuts[k + 1]][None, :])
    (wa, ba), (wb, bb), (wq, bq), (wkv, bkv), (wzg, bzg), (wga, bga), (wgb, bgb) = [seg(k) for k in range(7)]
    wzg = jnp.pad(wzg, ((0, LANES - 3 * N_HEADS), (0, 0)))
    bzg = jnp.pad(bzg, ((0, 0), (0, LANES - 3 * N_HEADS)))
    wc = w_conv_out[l].astype(BF16)
    wn = w_nsa_out[l].astype(BF16)
    wo = w_out[l].astype(BF16)
    wr = jnp.zeros((LANES, D_MODEL), F32).at[0:N_GROUPS].set(w_rg[l].T).at[8:8 + N_EXPERTS].set(w_re[l].T)
    br = jnp.zeros((LANES,), F32).at[0:N_GROUPS].set(b_rg[l]).at[8:8 + N_EXPERTS].set(b_re[l].reshape(-1))
    wr, br = wr.astype(BF16), br[:, None]
    cw = _compress_weights(cmp_pe_k[l], cmp_w1_k[l], cmp_w2_k[l], cmp_pe_v[l], cmp_w1_v[l], cmp_w2_v[l])
    rb_t = rel_bias.T
    gmix, gffn, gfin = norm_mix[l][None, :], norm_ffn[l][None, :], norm_final[None, :]
    vec = lambda a: a[l][None, :]

    q_scale = HEAD_DIM ** -0.5
    xp = x_prompt.reshape(n_tok, D_MODEL)
    xs = x_sample.reshape(bs, D_MODEL)
    flat = lambda w: w[l].reshape(-1, w.shape[-1])
    (hp, up, qp, gates_p, *kv_p), (wg_bf,) = _project_prompt(
        xp, gmix, wa, wb, wq, wkv, wzg, ba, bb, bq.reshape(-1, 1), bkv.reshape(-1, 1), bzg,
        q_scale * LOG2E, batch, seq, 256, riders=(flat(w_gate),))
    hs = _rmsnorm(xs, gmix, bs)
    us = _glu(hs, wa, wb, ba, bb, bs, 512)
    qs = _proj(hs, wq, bq, lambda z: z * q_scale, BF16, bs, 512, "proj_q")
    gates_s = _proj(hs, wzg, bzg, _sigmoid, F32, bs, LANES, "proj_head_gates")
    kv_s = _proj(hs, wkv, bkv, lambda z: z, F32, bs, 512, "proj_kv_sample")

    conv_args = (conv_w[l], vec(conv_b), vec(conv_ln_g), vec(conv_ln_b))
    cp = _conv_prompt(up, *conv_args, batch, seq, 256)
    cs, new_conv_s = _conv_sample(state_conv[l].transpose(1, 0, 2), us, *conv_args)

    nqt = seq // QT
    n_chunk_p = seq // CMP_STRIDE
    n_chunk_s = n_pages * _CHUNKS_PER_PAGE
    cmap, wmap, smap = _bias_prompt(rb_t, nqt, n_chunk_p)
    cbias_s, wbias_s, sbias_s = _bias_sample(rb_t, past, n_chunk_s, n_pages)

    ident = jnp.zeros((1,), I32)
    kcmp_p, _ = _compress(kv_p[0], ident, cw, batch, seq // PAGE, seq // PAGE, False)
    feat_major = lambda a: a.transpose(0, 2, 3, 4, 1)
    pool_cmp = feat_major(cache_cmp_kv[l]).reshape(n_pool, D_KV, PAGE)
    pp_s = min(32, n_pages)
    kcmp_s, (wd_bf,) = _compress(pool_cmp, page_table.reshape(-1), cw, bs, n_pages, pp_s, True,
                                 riders=(flat(w_down),))

    n_sel_p = seq // SEL_LEN
    o_cmp_p, selbias = _cmp_prompt(qp, kcmp_p, cmap, gates_p, _overlap(n_chunk_p, n_sel_p, n_sel_p).T,
                                   batch, seq)
    n_sel_s = -(-(past + 1) // SEL_LEN)
    n_sel_pad = -(-n_sel_s // LANES) * LANES
    top_s = min(SEL_TOP, n_sel_s)
    head_gate = lambda k: jnp.broadcast_to(
        gates_s[:, k * N_HEADS:(k + 1) * N_HEADS, None], (bs, N_HEADS, HEAD_DIM))
    qs3 = qs.reshape(bs, N_HEADS, HEAD_DIM)
    o_cmp_s, idx_s = _cmp_sample(qs3, kcmp_s, cbias_s, _overlap(n_chunk_s, n_sel_s, n_sel_pad), head_gate(0),
                                 n_sel_s, past)

    o_sel_p, (wu_bf,) = _sel_prompt(qp, kv_p[1], selbias, smap, gates_p, batch, seq, riders=(flat(w_up),))
    by_group = lambda a: jnp.pad(a.reshape(bs, N_KV, HPG, -1), ((0, 0), (0, 0), (0, 8 - HPG), (0, 0)))
    q4 = by_group(qs3)
    blk = idx_s[:, :top_s].reshape(-1)
    phys = jnp.take_along_axis(
        page_table, jnp.minimum(blk // 2, n_pages - 1).reshape(bs, -1), axis=1).reshape(-1)
    kv_new = lambda k: jnp.pad(
        kv_s[:, k * D_KV:(k + 1) * D_KV].reshape(bs, 2, N_KV, HEAD_DIM).transpose(0, 2, 1, 3),
        ((0, 0), (0, 0), (0, 6), (0, 0)))
    bias_by_group = lambda a: jnp.pad(
        jnp.moveaxis(a, -2, 0).reshape((N_KV, HPG) + a.shape[:-2] + a.shape[-1:]),
        ((0, 0), (0, 8 - HPG)) + ((0, 0),) * (a.ndim - 1))
    sb4 = jnp.moveaxis(bias_by_group(sbias_s), 1, 2)
    pool_sel = feat_major(cache_sel_kv[l])
    o_sel_s = _sel_sample(blk, phys, q4, kv_new(1), sb4, by_group(head_gate(1)), pool_sel, top_s, n_pages)

    o_win_p = _win_prompt(qp, kv_p[2], wmap, gates_p, batch, seq)
    win_t = feat_major(state_win_kv[l]).reshape(bs, D_KV, WINDOW)
    newcol = jnp.broadcast_to(kv_s[:, 2 * D_KV:3 * D_KV, None], (bs, D_KV, LANES))
    o_win_s, new_win_s = _win_sample(q4, kv_new(2), win_t, newcol, bias_by_group(wbias_s), by_group(head_gate(2)))

    m_p, _ = _merge_a(hp, cp, o_cmp_p, o_sel_p, o_win_p, wga, wgb, wc, wn, bga, bgb, vec(b_conv_out), 512, 512)
    (x1_p, h2_all, eid_p, rw_p, cnt_p), _ = _merge_b(xp, m_p, wo, gffn, wr, br, 256, slack=_MOE_SLACK)
    from_groups = lambda a: a[:, :, :HPG, :].reshape(bs, D_Q)
    m_s, _ = _merge_a(hs, cs, o_cmp_s.reshape(bs, D_Q), from_groups(o_sel_s), from_groups(o_win_s),
                      wga, wgb, wc, wn, bga, bgb, vec(b_conv_out), bs, 512)
    (x1_s, h2_all, eid_s, rw_s, cnt_s), _ = _merge_b(xs, m_s, wo, gffn, wr, br, bs, h2_all=h2_all, row0=n_tok)

    eid = jnp.concatenate([eid_p[:TOP_K].T, eid_s[:TOP_K].T], axis=0).reshape(-1)
    rw = jnp.concatenate([rw_p[:TOP_K].T, rw_s[:TOP_K].T], axis=0).reshape(-1)
    counts = jnp.sum(cnt_p[:, :, 0], axis=0) + jnp.sum(cnt_s[:, :, 0], axis=0)
    rows_src, rows_dst, rows_w, blk_e, n_used = _dispatch(eid, rw, counts, h2_all.shape[0])
    yk = _experts(blk_e, n_used, rows_src, rows_dst, rows_w, h2_all, n_tok + bs,
                  wg_bf.reshape(w_gate.shape[1:]), wu_bf.reshape(w_up.shape[1:]), wd_bf.reshape(w_down.shape[1:]))
    y_p = _combine(x1_p, yk, gfin, 0, 256)
    y_s = _combine(x1_s, yk, gfin, n_tok, bs)

    def cache_rows(a_t, rows):
        return a_t.reshape(a_t.shape[0], 2, N_KV, HEAD_DIM, rows).transpose(0, 4, 1, 2, 3)[None]

    new_row = lambda k: kv_s[:, k * D_KV:(k + 1) * D_KV].reshape(1, bs, 1, 2, N_KV, HEAD_DIM)
    hist = CONV_W - 1
    return (
        y_p.reshape(batch, seq, D_MODEL),
        y_s.reshape(bs, 1, D_MODEL),
        cache_rows(kv_p[0], seq),
        new_row(0),
        cache_rows(kv_p[1], seq),
        new_row(1),
        cache_rows(kv_p[2][:, :, seq - WINDOW:], WINDOW),
        cache_rows(new_win_s, WINDOW),
        up.reshape(batch, seq, D_CONV)[None, :, seq - hist:],
        new_conv_s.transpose(1, 0, 2)[None],
    )
```

```python
import functools
import math

import jax
import jax.numpy as jnp
from jax import lax
from jax.experimental import pallas as pl
from jax.experimental.pallas import tpu as pltpu

F32 = jnp.float32
BF16 = jnp.bfloat16
I32 = jnp.int32

D_MODEL = 2048
D_CONV = 1024
CONV_W = 31
N_HEADS = 16
HEAD_DIM = 64
N_KV = 4
HPG = N_HEADS // N_KV
D_Q = N_HEADS * HEAD_DIM
D_KV = 2 * N_KV * HEAD_DIM
CMP_LEN = 32
CMP_STRIDE = 16
CMP_HID = 64
SEL_LEN = 64
SEL_TOP = 16
WINDOW = 512
N_BUCKETS = 32
MAX_DIST = 128
N_GROUPS = 4
EXP_PER_GROUP = 8
N_EXPERTS = N_GROUPS * EXP_PER_GROUP
TOP_K = 2
D_EXPERT = 1024
EPS = 1e-6
NEG = -1e30
PAGE = 128
LANES = 128
QT = 128
GROUP_ROWS = HPG * QT
MIB = 1024 * 1024
LOG2E = math.log2(math.e)


def _params(sem, vmem_mib=48):
    return pltpu.CompilerParams(dimension_semantics=sem, vmem_limit_bytes=vmem_mib * MIB)


def _mm_nt(a, b):
    return lax.dot_general(a, b, (((1,), (1,)), ((), ())), preferred_element_type=F32)


def _mm(a, b):
    return jnp.dot(a, b, preferred_element_type=F32)


def _sigmoid(z):
    return 1.0 / (1.0 + jnp.exp(-z))


def _silu(z):
    return z * _sigmoid(z)


def _gelu_tanh(z):
    return 0.5 * z * (1.0 + jnp.tanh(math.sqrt(2.0 / math.pi) * (z + 0.044715 * (z * z * z))))


def _rmsnorm_kernel(x_ref, g_ref, o_ref):
    x = x_ref[...]
    ms = jnp.mean(x * x, axis=-1, keepdims=True)
    o_ref[...] = (x * lax.rsqrt(ms + EPS) * g_ref[...]).astype(o_ref.dtype)


def _rmsnorm(x, g, tm):
    n, d = x.shape
    return pl.pallas_call(
        _rmsnorm_kernel,
        grid=(n // tm,),
        in_specs=[pl.BlockSpec((tm, d), lambda i: (i, 0)), pl.BlockSpec((1, d), lambda i: (0, 0))],
        out_specs=pl.BlockSpec((tm, d), lambda i: (i, 0)),
        out_shape=jax.ShapeDtypeStruct((n, d), BF16),
        compiler_params=_params(("parallel",)),
        name="rmsnorm",
    )(x, g)


def _proj_kernel(h_ref, w_ref, b_ref, o_ref, *, act):
    z = _mm_nt(h_ref[...], w_ref[...]) + b_ref[...]
    o_ref[...] = act(z).astype(o_ref.dtype)


def _proj(h, wt, b, act, out_dtype, tm, tn, name):
    n, k = h.shape
    nout = wt.shape[0]
    return pl.pallas_call(
        functools.partial(_proj_kernel, act=act),
        grid=(n // tm, nout // tn),
        in_specs=[
            pl.BlockSpec((tm, k), lambda i, j: (i, 0)),
            pl.BlockSpec((tn, k), lambda i, j: (j, 0)),
            pl.BlockSpec((1, tn), lambda i, j: (0, j)),
        ],
        out_specs=pl.BlockSpec((tm, tn), lambda i, j: (i, j)),
        out_shape=jax.ShapeDtypeStruct((n, nout), out_dtype),
        compiler_params=_params(("parallel", "parallel")),
        name=name,
    )(h, wt, b)


def _proj_t_kernel(h_ref, w_ref, b_ref, o_ref, *, act):
    z = _mm_nt(w_ref[...], h_ref[...]) + b_ref[...]
    o_ref[...] = act(z).astype(o_ref.dtype)


def _proj_t(h, wt, bcol, act, out_dtype, tm, tn, name):
    n, k = h.shape
    nout = wt.shape[0]
    return pl.pallas_call(
        functools.partial(_proj_t_kernel, act=act),
        grid=(n // tm, nout // tn),
        in_specs=[
            pl.BlockSpec((tm, k), lambda i, j: (i, 0)),
            pl.BlockSpec((tn, k), lambda i, j: (j, 0)),
            pl.BlockSpec((tn, 1), lambda i, j: (j, 0)),
        ],
        out_specs=pl.BlockSpec((tn, tm), lambda i, j: (j, i)),
        out_shape=jax.ShapeDtypeStruct((nout, n), out_dtype),
        compiler_params=_params(("parallel", "parallel")),
        name=name,
    )(h, wt, bcol)


def _glu_kernel(h_ref, wa_ref, wb_ref, ba_ref, bb_ref, o_ref):
    h = h_ref[...]
    a = _mm_nt(h, wa_ref[...]) + ba_ref[...]
    b = _mm_nt(h, wb_ref[...]) + bb_ref[...]
    o_ref[...] = a * _sigmoid(b)


def _glu(h, wat, wbt, ba, bb, tm, tn):
    n, k = h.shape
    nout = wat.shape[0]
    wspec = pl.BlockSpec((tn, k), lambda i, j: (j, 0))
    bspec = pl.BlockSpec((1, tn), lambda i, j: (0, j))
    return pl.pallas_call(
        _glu_kernel,
        grid=(n // tm, nout // tn),
        in_specs=[pl.BlockSpec((tm, k), lambda i, j: (i, 0)), wspec, wspec, bspec, bspec],
        out_specs=pl.BlockSpec((tm, tn), lambda i, j: (i, j)),
        out_shape=jax.ShapeDtypeStruct((n, nout), F32),
        compiler_params=_params(("parallel", "parallel")),
        name="glu",
    )(h, wat, wbt, ba, bb)


def _kvt_kernel(h_ref, w_ref, b_ref, *o_refs):
    h = h_ref[...]
    for kind, o_ref in enumerate(o_refs):
        rows = pl.ds(kind * D_KV, D_KV)
        o_ref[0] = _mm_nt(w_ref[rows, :], h) + b_ref[rows, :]


def _kv_transposed(h, wt, bcol, batch, seq, tm):
    n, k = h.shape
    kinds = wt.shape[0] // D_KV
    nt = seq // tm
    return pl.pallas_call(
        _kvt_kernel,
        grid=(n // tm,),
        in_specs=[
            pl.BlockSpec((tm, k), lambda i: (i, 0)),
            pl.BlockSpec(wt.shape, lambda i: (0, 0)),
            pl.BlockSpec(bcol.shape, lambda i: (0, 0)),
        ],
        out_specs=[pl.BlockSpec((1, D_KV, tm), lambda i: (i // nt, 0, i % nt))] * kinds,
        out_shape=[jax.ShapeDtypeStruct((batch, D_KV, seq), F32)] * kinds,
        compiler_params=_params(("parallel",)),
        name="kv_transposed",
    )(h, wt, bcol)


def _project_prompt_kernel(x_ref, g_ref, wa_ref, wb_ref, wq_ref, wkv_ref, wzg_ref, ba_ref, bb_ref, bq_ref,
                           bkv_ref, bzg_ref, *rest, q_scale, n_riders, kinds):
    rider_in = rest[:n_riders]
    h_ref, u_ref, qt_ref, gates_ref = rest[n_riders:n_riders + 4]
    kv_refs = rest[n_riders + 4:n_riders + 4 + kinds]
    rider_out = rest[n_riders + 4 + kinds:]
    for src, dst in zip(rider_in, rider_out):
        dst[...] = src[...].astype(dst.dtype)
    x = x_ref[...]
    ms = jnp.mean(x * x, axis=-1, keepdims=True)
    h = (x * lax.rsqrt(ms + EPS) * g_ref[...]).astype(BF16)
    h_ref[...] = h
    a = _mm_nt(h, wa_ref[...]) + ba_ref[...]
    b = _mm_nt(h, wb_ref[...]) + bb_ref[...]
    u_ref[...] = a * _sigmoid(b)
    qt_ref[...] = ((_mm_nt(wq_ref[...], h) + bq_ref[...]) * q_scale).astype(qt_ref.dtype)
    gates_ref[...] = _sigmoid(_mm_nt(h, wzg_ref[...]) + bzg_ref[...])
    for kind, o_ref in enumerate(kv_refs):
        rows = pl.ds(kind * D_KV, D_KV)
        o_ref[0] = _mm_nt(wkv_ref[rows, :], h) + bkv_ref[rows, :]


def _project_prompt(x, g, wa, wb, wq, wkv, wzg, ba, bb, bq_col, bkv_col, bzg, q_scale, batch, seq, tm,
                    riders=()):
    n, d = x.shape
    kinds = wkv.shape[0] // D_KV
    nt = seq // tm
    tok = lambda width: pl.BlockSpec((tm, width), lambda i: (i, 0))
    const = lambda a: pl.BlockSpec(a.shape, lambda i: (0,) * a.ndim, pipeline_mode=pl.Buffered(1))
    consts = (g, wa, wb, wq, wkv, wzg, ba, bb, bq_col, bkv_col, bzg)
    ride = _rider_specs(riders, n // tm, lambda i: (i, 0))
    out = pl.pallas_call(
        functools.partial(_project_prompt_kernel, q_scale=q_scale, n_riders=len(riders), kinds=kinds),
        grid=(n // tm,),
        in_specs=[tok(d)] + [const(a) for a in consts] + ride,
        out_specs=[tok(d), tok(D_CONV), pl.BlockSpec((D_Q, tm), lambda i: (0, i)), tok(LANES)]
        + [pl.BlockSpec((1, D_KV, tm), lambda i: (i // nt, 0, i % nt))] * kinds + ride,
        out_shape=[
            jax.ShapeDtypeStruct((n, d), BF16),
            jax.ShapeDtypeStruct((n, D_CONV), F32),
            jax.ShapeDtypeStruct((D_Q, n), BF16),
            jax.ShapeDtypeStruct((n, LANES), F32),
        ] + [jax.ShapeDtypeStruct((batch, D_KV, seq), F32)] * kinds
        + [jax.ShapeDtypeStruct(a.shape, BF16) for a in riders],
        compiler_params=_params(("parallel",), 56),
        name="project_prompt",
    )(x, *consts, *riders)
    return out[:4 + kinds], out[4 + kinds:]


def _merge_a_kernel(h_ref, c_ref, o1_ref, o2_ref, o3_ref, wga_ref, wgb_ref, wc_ref, wn_ref,
                    bga_ref, bgb_ref, bc_ref, *rest):
    n_cast = (len(rest) - 1) // 2
    m_ref = rest[n_cast]
    h = h_ref[...]
    ga = _sigmoid(_mm_nt(h, wga_ref[...]) + bga_ref[...])
    gb = _sigmoid(_mm_nt(h, wgb_ref[...]) + bgb_ref[...])
    yc = _mm(c_ref[...], wc_ref[...]) + bc_ref[...]
    o = (o1_ref[...] + o2_ref[...] + o3_ref[...]).astype(BF16)
    yn = _mm(o, wn_ref[...])
    m_ref[...] = (ga * yc + gb * yn).astype(m_ref.dtype)
    for src, dst in zip(rest[:n_cast], rest[n_cast + 1:]):
        dst[...] = src[...].astype(dst.dtype)


def _rider_specs(riders, steps, index):
    return [pl.BlockSpec((a.shape[0] // steps, a.shape[1]), index) for a in riders]


def _merge_a(h, c, o1, o2, o3, wgat, wgbt, wc, wn, bga, bgb, bc, tm, tn, riders=()):
    n, k = h.shape
    nj = D_MODEL // tn
    tok = lambda width: pl.BlockSpec((tm, width), lambda i, j: (i, 0))
    wt_spec = pl.BlockSpec((tn, k), lambda i, j: (j, 0))
    w_spec = pl.BlockSpec((c.shape[1], tn), lambda i, j: (0, j))
    b_spec = pl.BlockSpec((1, tn), lambda i, j: (0, j))
    ride = _rider_specs(riders, (n // tm) * nj, lambda i, j: (i * nj + j, 0))
    out = pl.pallas_call(
        _merge_a_kernel,
        grid=(n // tm, nj),
        in_specs=[tok(k), tok(D_CONV), tok(D_Q), tok(D_Q), tok(D_Q), wt_spec, wt_spec, w_spec, w_spec,
                  b_spec, b_spec, b_spec] + ride,
        out_specs=[pl.BlockSpec((tm, tn), lambda i, j: (i, j))] + ride,
        out_shape=[jax.ShapeDtypeStruct((n, D_MODEL), BF16)]
        + [jax.ShapeDtypeStruct(a.shape, BF16) for a in riders],
        compiler_params=_params(("parallel", "parallel"), 56),
        name="merge_gates",
    )(h, c, o1, o2, o3, wgat, wgbt, wc, wn, bga, bgb, bc, *riders)
    return out[0], out[1:]


def _merge_b_kernel(x_ref, m_ref, wo_ref, gn_ref, wr_ref, br_ref, *rest, n_real, n_shared, n_riders):
    rider_in = rest[n_shared:n_shared + n_riders]
    outs = rest[n_shared + n_riders:n_shared + n_riders + 5]
    rider_out = rest[n_shared + n_riders + 5:]
    h2_ref = outs[1]

    @pl.when(pl.program_id(0) < n_real)
    def _():
        _merge_b_tile(x_ref, m_ref, wo_ref, gn_ref, wr_ref, br_ref, *outs)
        for src, dst in zip(rider_in, rider_out):
            dst[...] = src[...].astype(dst.dtype)

    @pl.when(pl.program_id(0) >= n_real)
    def _():
        h2_ref[...] = jnp.zeros(h2_ref.shape, h2_ref.dtype)


def _merge_b_tile(x_ref, m_ref, wo_ref, gn_ref, wr_ref, br_ref, x1_ref, h2_ref, eid_ref, rw_ref, cnt_ref):
    x1 = x_ref[...] + _mm(m_ref[...], wo_ref[...])
    x1_ref[...] = x1
    ms = jnp.mean(x1 * x1, axis=-1, keepdims=True)
    h2 = x1 * lax.rsqrt(ms + EPS) * gn_ref[...]
    h2_ref[...] = h2
    logits = _mm_nt(wr_ref[...], h2.astype(BF16)) + br_ref[...]
    tm = logits.shape[1]
    row = lax.broadcasted_iota(I32, (8, tm), 0)
    lg = jnp.where(row < N_GROUPS, logits[0:8], -jnp.inf)
    gmax = jnp.max(lg, axis=0, keepdims=True)
    gidx = jnp.min(jnp.where(lg == gmax, row, 8), axis=0, keepdims=True)
    wg = 1.0 / jnp.sum(jnp.exp(lg - gmax), axis=0, keepdims=True)
    le = jnp.zeros((8, tm), F32)
    for g in range(N_GROUPS):
        le = jnp.where(gidx == g, logits[8 + 8 * g:16 + 8 * g], le)
    ee = jnp.exp(le - jnp.max(le, axis=0, keepdims=True))
    pz = ee / jnp.sum(ee, axis=0, keepdims=True)
    p1 = jnp.max(pz, axis=0, keepdims=True)
    i1 = jnp.min(jnp.where(pz == p1, row, 8), axis=0, keepdims=True)
    pz2 = jnp.where(row == i1, -1.0, pz)
    p2 = jnp.max(pz2, axis=0, keepdims=True)
    i2 = jnp.min(jnp.where(pz2 == p2, row, 8), axis=0, keepdims=True)
    den = p1 + p2
    e1 = gidx * EXP_PER_GROUP + i1
    e2 = gidx * EXP_PER_GROUP + i2
    eid_ref[...] = jnp.where(row == 0, e1, jnp.where(row == 1, e2, 0))
    rw_ref[...] = jnp.where(row == 0, wg * p1 / den, jnp.where(row == 1, wg * p2 / den, 0.0))
    erow = lax.broadcasted_iota(I32, (N_EXPERTS, tm), 0)
    hits = (erow == e1).astype(F32) + (erow == e2).astype(F32)
    cnt_ref[...] = jnp.broadcast_to(jnp.sum(hits, axis=1, keepdims=True), (N_EXPERTS, LANES)).astype(I32)


def _merge_b(x, m, wo, gn, wr, br, tm, h2_all=None, row0=0, slack=0, riders=()):
    n, d = x.shape
    blk0 = row0 // tm
    n_real = n // tm
    real = lambda i: jnp.minimum(i, n_real - 1)
    shared = [] if h2_all is None else [h2_all]
    h2_rows = n + slack if h2_all is None else h2_all.shape[0]
    ride = _rider_specs(riders, n_real, lambda i: (real(i), 0))
    out = pl.pallas_call(
        functools.partial(_merge_b_kernel, n_real=n_real, n_shared=len(shared), n_riders=len(riders)),
        grid=((n + slack) // tm,),
        in_specs=[
            pl.BlockSpec((tm, d), lambda i: (real(i), 0)),
            pl.BlockSpec((tm, d), lambda i: (real(i), 0)),
            pl.BlockSpec((d, d), lambda i: (0, 0)),
            pl.BlockSpec((1, d), lambda i: (0, 0)),
            pl.BlockSpec((LANES, d), lambda i: (0, 0)),
            pl.BlockSpec((LANES, 1), lambda i: (0, 0)),
        ] + [pl.BlockSpec(memory_space=pl.ANY)] * len(shared) + ride,
        out_specs=[
            pl.BlockSpec((tm, d), lambda i: (real(i), 0)),
            pl.BlockSpec((tm, d), lambda i: (blk0 + i, 0)),
            pl.BlockSpec((8, tm), lambda i: (0, real(i))),
            pl.BlockSpec((8, tm), lambda i: (0, real(i))),
            pl.BlockSpec((None, N_EXPERTS, LANES), lambda i: (real(i), 0, 0)),
        ] + ride,
        out_shape=[
            jax.ShapeDtypeStruct((n, d), F32),
            jax.ShapeDtypeStruct((h2_rows, d), F32),
            jax.ShapeDtypeStruct((8, n), I32),
            jax.ShapeDtypeStruct((8, n), F32),
            jax.ShapeDtypeStruct((n_real, N_EXPERTS, LANES), I32),
        ] + [jax.ShapeDtypeStruct(a.shape, BF16) for a in riders],
        input_output_aliases={6: 1} if shared else {},
        compiler_params=_params(("arbitrary",), 56),
        name="merge_out_router",
    )(x, m, wo, gn, wr, br, *shared, *riders)
    return out[:5], out[5:]


_HALO = 32


def _conv_prompt_kernel(u_ref, prev_ref, w_ref, b_ref, g_ref, beta_ref, o_ref, buf_ref, acc_ref, sh_ref, *, tt):
    i = pl.program_id(1)
    prev = prev_ref[pl.ds(tt - _HALO, _HALO), :]
    buf_ref[pl.ds(0, _HALO), :] = jnp.where(i > 0, prev, 0.0)
    buf_ref[pl.ds(_HALO, tt), :] = u_ref[...]
    off = _HALO - (CONV_W - 1)
    sub = 8
    rows = 128

    def chunk(cc, carry):
        cols = pl.ds(pl.multiple_of(cc * LANES, LANES), LANES)
        for r in range(sub):
            n_shift = (tt + _HALO - r) // sub * sub
            sh_ref[r, pl.ds(0, n_shift), :] = buf_ref[pl.ds(r, n_shift), cols]
        for r0 in range(0, tt, rows):
            acc = jnp.zeros((rows, LANES), F32)
            for k in range(CONV_W):
                r = (off + k) % sub
                acc = acc + sh_ref[r, pl.ds(r0 + off + k - r, rows), :] * w_ref[pl.ds(k, 1), cols]
            acc_ref[pl.ds(r0, rows), cols] = acc + b_ref[:, cols]
        return carry

    lax.fori_loop(0, D_CONV // LANES, chunk, 0)
    c = acc_ref[...]
    mu = jnp.mean(c, axis=-1, keepdims=True)
    var = jnp.mean(jnp.square(c - mu), axis=-1, keepdims=True)
    y = (c - mu) * lax.rsqrt(var + EPS) * g_ref[...] + beta_ref[...]
    o_ref[...] = _silu(y).astype(o_ref.dtype)


def _conv_prompt(u, w, b, g, beta, batch, seq, tt):
    nt = seq // tt
    vec = pl.BlockSpec((1, D_CONV), lambda bi, i: (0, 0))
    return pl.pallas_call(
        functools.partial(_conv_prompt_kernel, tt=tt),
        grid=(batch, nt),
        in_specs=[
            pl.BlockSpec((tt, D_CONV), lambda bi, i: (bi * nt + i, 0)),
            pl.BlockSpec((tt, D_CONV), lambda bi, i: (bi * nt + jnp.maximum(i - 1, 0), 0)),
            pl.BlockSpec((CONV_W, D_CONV), lambda bi, i: (0, 0)),
            vec, vec, vec,
        ],
        out_specs=pl.BlockSpec((tt, D_CONV), lambda bi, i: (bi * nt + i, 0)),
        out_shape=jax.ShapeDtypeStruct((batch * seq, D_CONV), BF16),
        scratch_shapes=[pltpu.VMEM((tt + _HALO, D_CONV), F32), pltpu.VMEM((tt, D_CONV), F32),
                        pltpu.VMEM((8, tt + _HALO, LANES), F32)],
        compiler_params=_params(("parallel", "parallel")),
        name="conv_prompt",
    )(u, u, w, b, g, beta)


def _conv_sample_kernel(st_ref, u_ref, w_ref, b_ref, g_ref, beta_ref, c_ref, new_ref):
    hist = CONV_W - 1
    u = u_ref[...]
    acc = u * w_ref[pl.ds(hist, 1), :] + b_ref[...]
    for t in range(hist):
        acc = acc + st_ref[t] * w_ref[pl.ds(t, 1), :]
    mu = jnp.mean(acc, axis=-1, keepdims=True)
    var = jnp.mean(jnp.square(acc - mu), axis=-1, keepdims=True)
    y = (acc - mu) * lax.rsqrt(var + EPS) * g_ref[...] + beta_ref[...]
    c_ref[...] = _silu(y).astype(c_ref.dtype)
    for t in range(hist - 1):
        new_ref[t] = st_ref[t + 1]
    new_ref[hist - 1] = u


def _conv_sample(state_t, u, w, b, g, beta):
    hist, bs, _ = state_t.shape
    return pl.pallas_call(
        _conv_sample_kernel,
        out_shape=[jax.ShapeDtypeStruct((bs, D_CONV), BF16), jax.ShapeDtypeStruct((hist, bs, D_CONV), F32)],
        compiler_params=pltpu.CompilerParams(vmem_limit_bytes=48 * MIB),
        name="conv_sample",
    )(state_t, u, w, b, g, beta)


def _bias_lookup(rb_ref, head, dist, valid):
    n = jnp.maximum(dist, 0)
    max_exact = N_BUCKETS // 2
    nf = jnp.maximum(n, 1).astype(F32)
    large = max_exact + (jnp.log(nf / max_exact) / math.log(MAX_DIST / max_exact)
                         * (N_BUCKETS - max_exact)).astype(I32)
    bucket = jnp.where(n < max_exact, n, jnp.minimum(large, N_BUCKETS - 1))
    val = jnp.zeros(dist.shape, F32)
    for k in range(N_BUCKETS):
        val = jnp.where(bucket == k, rb_ref[head, k], val)
    return jnp.where(valid, val, NEG)


def _bias_prompt_kernel(rb_ref, cmp_ref, win_ref, sel_ref, *, nqt, n_chunk):
    g = pl.program_id(0)
    key = lax.broadcasted_iota(I32, (LANES, QT), 0)
    qry = lax.broadcasted_iota(I32, (LANES, QT), 1)
    for hh in range(HPG):
        head = g * HPG + hh
        cols = pl.ds(hh * QT, QT)
        lookup = lambda dist, valid: _bias_lookup(rb_ref, head, dist, valid) * LOG2E

        for eb in range(_cmp_map_rows(nqt, n_chunk) // LANES):
            block = key + eb * LANES - _cmp_map_shift(nqt, 0)
            dist = qry - block * CMP_STRIDE - (CMP_LEN - 1)
            cmp_ref[0, pl.ds(eb * LANES, LANES), cols] = lookup(dist, dist >= 0)
        for jb in range((WINDOW + QT) // LANES):
            dist = qry - (key + jb * LANES) + WINDOW
            win_ref[0, pl.ds(jb * LANES, LANES), cols] = lookup(dist, (dist >= 0) & (dist <= WINDOW))
        for kind in range(3):
            for jb in range(2):
                dist = kind * QT + qry - (key + jb * LANES)
                sel_ref[0, kind, pl.ds(jb * LANES, LANES), cols] = lookup(dist, dist >= 0)
        far = jnp.full((LANES, QT), 2 * MAX_DIST, I32)
        for jb in range(2):
            sel_ref[0, 3, pl.ds(jb * LANES, LANES), cols] = lookup(far, far >= 0)


def _cmp_map_shift(nqt, qt):
    return (nqt - 1 - qt) * (QT // CMP_STRIDE)


def _cmp_map_rows(nqt, n_chunk):
    return -(-(n_chunk + _cmp_map_shift(nqt, 0)) // LANES) * LANES


def _bias_prompt(rb_t, nqt, n_chunk):
    ext = _cmp_map_rows(nqt, n_chunk)
    return pl.pallas_call(
        functools.partial(_bias_prompt_kernel, nqt=nqt, n_chunk=n_chunk),
        grid=(N_KV,),
        in_specs=[pl.BlockSpec(memory_space=pltpu.SMEM)],
        out_specs=[
            pl.BlockSpec((1, ext, GROUP_ROWS), lambda g: (g, 0, 0)),
            pl.BlockSpec((1, WINDOW + QT, GROUP_ROWS), lambda g: (g, 0, 0)),
            pl.BlockSpec((1, 4, 2 * LANES, GROUP_ROWS), lambda g: (g, 0, 0, 0)),
        ],
        out_shape=[
            jax.ShapeDtypeStruct((N_KV, ext, GROUP_ROWS), F32),
            jax.ShapeDtypeStruct((N_KV, WINDOW + QT, GROUP_ROWS), F32),
            jax.ShapeDtypeStruct((N_KV, 4, 2 * LANES, GROUP_ROWS), F32),
        ],
        compiler_params=_params(("parallel",)),
        name="bias_maps_prompt",
    )(rb_t)


def _bias_sample_kernel(rb_ref, cmp_ref, win_ref, sel_ref, *, past, n_chunk, n_pages):
    col = lax.broadcasted_iota(I32, (N_HEADS, LANES), 1)
    rb = rb_ref[...]

    def lookup(dist):
        n = jnp.maximum(dist, 0)
        max_exact = N_BUCKETS // 2
        nf = jnp.maximum(n, 1).astype(F32)
        large = max_exact + (jnp.log(nf / max_exact) / math.log(MAX_DIST / max_exact)
                             * (N_BUCKETS - max_exact)).astype(I32)
        bucket = jnp.where(n < max_exact, n, jnp.minimum(large, N_BUCKETS - 1))
        val = jnp.zeros(dist.shape, F32)
        for k in range(N_BUCKETS):
            val = jnp.where(bucket == k, rb[:, k:k + 1], val)
        return jnp.where(dist >= 0, val, NEG)

    def cmp_block(cb, carry):
        dist = past - (col + cb * LANES) * CMP_STRIDE - (CMP_LEN - 1)
        cmp_ref[cb] = lookup(dist)
        return carry

    lax.fori_loop(0, n_chunk // LANES, cmp_block, 0)

    def win_block(jb, carry):
        win_ref[jb] = lookup(WINDOW - (col + jb * LANES))
        return carry

    lax.fori_loop(0, WINDOW // LANES + 1, win_block, 0)

    def sel_page(p, carry):
        sel_ref[p] = lookup(past - (p * PAGE + col))
        return carry

    lax.fori_loop(0, n_pages + 1, sel_page, 0)


def _bias_sample(rb_t, past, n_chunk, n_pages):
    cmp_b, win_b, sel_b = pl.pallas_call(
        functools.partial(_bias_sample_kernel, past=past, n_chunk=n_chunk, n_pages=n_pages),
        out_shape=[
            jax.ShapeDtypeStruct((n_chunk // LANES, N_HEADS, LANES), F32),
            jax.ShapeDtypeStruct((WINDOW // LANES + 1, N_HEADS, LANES), F32),
            jax.ShapeDtypeStruct((n_pages + 1, N_HEADS, LANES), F32),
        ],
        name="bias_maps_sample",
    )(rb_t)
    flat = lambda a: a.transpose(1, 0, 2).reshape(N_HEADS, -1)
    return flat(cmp_b), flat(win_b), sel_b


_CHUNKS_PER_PAGE = PAGE // CMP_STRIDE
_CHUNK_PITCH = 24


def _compress_kernel(pt_ref, *refs, pp, n_riders):
    page_refs = refs[:pp + 1]
    w1k_ref, w1v_ref, pe_ref, w2k_ref, w2v_ref = refs[pp + 1:pp + 6]
    rider_in = refs[pp + 6:pp + 6 + n_riders]
    o_ref = refs[pp + 6 + n_riders]
    rider_out = refs[pp + 7 + n_riders:pp + 7 + 2 * n_riders]
    tr_ref = refs[pp + 7 + 2 * n_riders]
    del pt_ref
    for src, dst in zip(rider_in, rider_out):
        dst[...] = src[...].astype(dst.dtype)
    nch = pp * _CHUNKS_PER_PAGE
    m = nch + 8
    n_slab = D_KV // LANES

    def transpose(s):
        for k in range(pp + 1):
            t = page_refs[k][pl.ds(s * LANES, LANES), :].T
            for n in range(_CHUNKS_PER_PAGE):
                row = (k * _CHUNKS_PER_PAGE + n) * _CHUNK_PITCH
                tr_ref[s, pl.ds(row, CMP_STRIDE), :] = t[n * CMP_STRIDE:(n + 1) * CMP_STRIDE]

    def chunks(s, _):
        return jnp.concatenate(
            [tr_ref.at[s][pl.ds(p, m, stride=_CHUNK_PITCH), :] for p in range(CMP_STRIDE)], axis=1)

    def mlp(s, x):
        is_v = int(s >= n_slab // 2)
        w1 = (w1v_ref if is_v else w1k_ref)[...]
        w2 = (w2v_ref if is_v else w2k_ref)[...]
        y = _mm(x.astype(BF16), w1)
        y_pe = _mm(pe_ref[...].astype(BF16), w1)
        pe_term = y_pe[2 * is_v:2 * is_v + 1, 0:LANES] + y_pe[2 * is_v + 1:2 * is_v + 2, LANES:2 * LANES]
        pre = y[0:nch, 0:LANES] + y[1:nch + 1, LANES:2 * LANES] + pe_term
        o_ref[0, :, pl.ds(s * LANES, LANES)] = _mm(_gelu_tanh(pre).astype(BF16), w2)

    _staggered(n_slab, transpose, chunks, mlp)


def _compress(pages, page_ids, weights, nb, n_pages, pp, paged, riders=()):
    steps = n_pages // pp
    nch = pp * _CHUNKS_PER_PAGE
    ride = _rider_specs(riders, nb * steps, lambda b, s, pt: (b * steps + s, 0))

    def page_spec(k):
        if paged:
            return pl.BlockSpec(
                (None, D_KV, PAGE),
                lambda b, s, pt: (pt[b * n_pages + jnp.minimum(s * pp + k, n_pages - 1)], 0, 0))
        return pl.BlockSpec((None, D_KV, PAGE), lambda b, s, pt: (b, 0, jnp.minimum(s * pp + k, n_pages - 1)))

    full = lambda a: pl.BlockSpec(a.shape, lambda b, s, pt: (0,) * a.ndim)
    grid_spec = pltpu.PrefetchScalarGridSpec(
        num_scalar_prefetch=1,
        grid=(nb, steps),
        in_specs=[page_spec(k) for k in range(pp + 1)] + [full(a) for a in weights] + ride,
        out_specs=[pl.BlockSpec((1, nch, D_KV), lambda b, s, pt: (b, s, 0))] + ride,
        scratch_shapes=[pltpu.VMEM((D_KV // LANES, (pp + 1) * _CHUNKS_PER_PAGE * _CHUNK_PITCH, LANES), F32)],
    )
    out = pl.pallas_call(
        functools.partial(_compress_kernel, pp=pp, n_riders=len(riders)),
        grid_spec=grid_spec,
        out_shape=[jax.ShapeDtypeStruct((nb, n_pages * _CHUNKS_PER_PAGE, D_KV), F32)]
        + [jax.ShapeDtypeStruct(a.shape, BF16) for a in riders],
        compiler_params=_params(("parallel", "parallel"), 56),
        name="compress_paged" if paged else "compress_prompt",
    )(page_ids, *([pages] * (pp + 1)), *weights, *riders)
    return out[0], out[1:]


def _compress_weights(pe_k, w1_k, w2_k, pe_v, w1_v, w2_v):
    eye = jnp.eye(2, dtype=F32)

    def first(w1, j):
        w = w1[j * CMP_STRIDE:(j + 1) * CMP_STRIDE]
        return jnp.einsum("pdf,gh->pgdhf", w, eye).reshape(CMP_STRIDE * LANES, 2 * CMP_HID).astype(BF16)

    def pos(pe, j):
        return jnp.tile(pe[j * CMP_STRIDE:(j + 1) * CMP_STRIDE, None, :], (1, 2, 1)).reshape(1, CMP_STRIDE * LANES)

    def second(w2):
        return jnp.einsum("fd,gh->gfhd", w2, eye).reshape(2 * CMP_HID, 2 * HEAD_DIM).astype(BF16)

    pe = jnp.concatenate([pos(pe_k, 0), pos(pe_k, 1), pos(pe_v, 0), pos(pe_v, 1)], axis=0)
    pe = jnp.pad(pe, ((0, 8 - pe.shape[0]), (0, 0)))
    both = lambda w1: jnp.concatenate([first(w1, 0), first(w1, 1)], axis=1)
    return (both(w1_k), both(w1_v), pe, second(w2_k), second(w2_v))


def _staggered(n, scores, softmax, values, ahead=1, lag=0):
    s = {g: scores(g) for g in range(min(ahead, n))}
    p = {}
    out = []
    for g in range(n + lag):
        if g + ahead < n:
            s[g + ahead] = scores(g + ahead)
        if g < n:
            p[g] = softmax(g, s.pop(g))
        if g - lag >= 0:
            out.append(values(g - lag, p.pop(g - lag)))
    return out


def _group_queries(qt_ref, g):
    return jnp.concatenate(
        [qt_ref[pl.ds((g * HPG + hh) * HEAD_DIM, HEAD_DIM), :] for hh in range(HPG)], axis=1)


def _store_heads(o_ref, o_t, gate, g, branch):
    lane = lax.broadcasted_iota(I32, (QT, 2 * HEAD_DIM), 1)
    for pair in range(HPG // 2):
        h = g * HPG + 2 * pair
        col = branch * N_HEADS + h
        both = jnp.concatenate([o_t[:, (2 * pair) * QT:(2 * pair + 1) * QT],
                                o_t[:, (2 * pair + 1) * QT:(2 * pair + 2) * QT]], axis=0)
        gates = jnp.where(lane < HEAD_DIM, gate[:, col:col + 1], gate[:, col + 1:col + 2])
        o_ref[:, pl.ds(h * HEAD_DIM, 2 * HEAD_DIM)] = both.T * gates


def _cmp_prompt_kernel(qt_ref, kc_ref, map_ref, gate_ref, ov_ref, o_ref, sb_ref, *, n_sel, top):
    qt = pl.program_id(1)
    kv = kc_ref[0]
    gate = gate_ref[...]
    ov = ov_ref[...]
    half = D_KV // 2
    n_chunk = kv.shape[0]
    map_rows = pl.ds(pl.multiple_of(_cmp_map_shift(pl.num_programs(1), qt), 8), n_chunk)
    bias = lambda g: map_ref[g, map_rows, :]
    vt_all = kv[:, half:].T
    jrow = lax.broadcasted_iota(I32, (n_sel, QT), 0)
    tpos = qt * QT + lax.broadcasted_iota(I32, (n_sel, QT), 1)
    cur = tpos // SEL_LEN
    forced = (jrow == 0) | (jrow == cur) | (jrow == cur - 1)
    valid = jrow * SEL_LEN <= tpos
    def scores(g):
        kg = kv[:, g * HEAD_DIM:(g + 1) * HEAD_DIM].astype(BF16)
        return _mm(kg, _group_queries(qt_ref, g)) + bias(g)

    def softmax(g, st):
        e = jnp.exp2(st - jnp.max(st, axis=0, keepdims=True))
        p = e / jnp.sum(e, axis=0, keepdims=True) * (bias(g) > 0.5 * NEG).astype(F32)
        return p.astype(BF16)

    def values(g, pb):
        vgt = vt_all[g * HEAD_DIM:(g + 1) * HEAD_DIM].astype(BF16)
        _store_heads(o_ref, _mm(vgt, pb), gate, g, 0)
        return pb

    probs = _staggered(N_KV, scores, softmax, values, ahead=2, lag=1)
    for g in range(N_KV):
        pb = probs[g]
        imp = jnp.zeros((n_sel, QT), F32)
        for hh in range(HPG):
            imp = imp + _mm(ov, pb[:, hh * QT:(hh + 1) * QT])
        score = jnp.where(forced, jnp.inf, jnp.where(valid, imp, -jnp.inf))
        rank = jnp.zeros((n_sel, QT), I32)
        for i in range(n_sel):
            si = score[i:i + 1, :]
            ahead = (si > score) | ((si == score) & (jrow > i))
            rank = rank + ahead.astype(I32)
        sb_ref[pl.ds(g * n_sel, n_sel), :] = jnp.where(rank < top, 0.0, NEG).astype(sb_ref.dtype)


def _cmp_prompt(q_t, kcmp, cmap, gates, ov_t, batch, seq):
    nqt = seq // QT
    n_chunk = kcmp.shape[1]
    n_sel = ov_t.shape[0]
    top = min(SEL_TOP, n_sel)
    return pl.pallas_call(
        functools.partial(_cmp_prompt_kernel, n_sel=n_sel, top=top),
        grid=(batch, nqt),
        in_specs=[
            pl.BlockSpec((D_Q, QT), lambda b, i: (0, b * nqt + i)),
            pl.BlockSpec((1, n_chunk, D_KV), lambda b, i: (b, 0, 0)),
            pl.BlockSpec(cmap.shape, lambda b, i: (0, 0, 0)),
            pl.BlockSpec((QT, LANES), lambda b, i: (b * nqt + i, 0)),
            pl.BlockSpec(ov_t.shape, lambda b, i: (0, 0)),
        ],
        out_specs=[
            pl.BlockSpec((QT, D_Q), lambda b, i: (b * nqt + i, 0)),
            pl.BlockSpec((None, N_KV * n_sel, QT), lambda b, i: (b * nqt + i, 0, 0)),
        ],
        out_shape=[
            jax.ShapeDtypeStruct((batch * seq, D_Q), F32),
            jax.ShapeDtypeStruct((batch * nqt, N_KV * n_sel, QT), BF16),
        ],
        compiler_params=_params(("parallel", "parallel")),
        name="cmp_attn_prompt",
    )(q_t, kcmp, cmap, gates, ov_t)


_TK = 2 * LANES


def _sel_prompt_kernel(qt_ref, kv_ref, sb_ref, map_ref, gate_ref, *rest, n_sel, seq, n_riders):
    rider_in = rest[:n_riders]
    o_ref = rest[n_riders]
    rider_out = rest[n_riders + 1:2 * n_riders + 1]
    ka_ref, vt_ref, qa_ref = rest[2 * n_riders + 1:]
    for src, dst in zip(rider_in, rider_out):
        dst[...] = src[...].astype(dst.dtype)
    qt = pl.program_id(1)
    aug = 2 * HEAD_DIM
    half = D_KV // 2

    @pl.when(qt == 0)
    def _():
        pos = lax.broadcasted_iota(I32, (seq, aug - HEAD_DIM), 0)
        blk = lax.broadcasted_iota(I32, (seq, aug - HEAD_DIM), 1)
        onehot = jnp.where(pos // SEL_LEN == blk, 1.0, 0.0)
        k_rows = kv_ref[0, pl.ds(0, half), :].T
        for g in range(N_KV):
            ka_ref[g] = jnp.concatenate([k_rows[:, g * HEAD_DIM:(g + 1) * HEAD_DIM], onehot], axis=1).astype(BF16)
            vt_ref[g] = kv_ref[0, pl.ds(half + g * HEAD_DIM, HEAD_DIM), :].astype(BF16)

    gate = gate_ref[...]
    sb = sb_ref[...]
    diag = qt // 2
    n_tiles = diag + 1
    pad = jnp.zeros((aug - HEAD_DIM - n_sel, GROUP_ROWS), BF16)
    for g in range(N_KV):
        sbg = sb[g * n_sel:(g + 1) * n_sel]
        qa_ref[g] = jnp.concatenate([_group_queries(qt_ref, g), jnp.concatenate([sbg] * HPG, axis=1), pad], axis=0)

    def tile(kt, carry):
        start = pl.multiple_of(kt * _TK, _TK)
        kind = jnp.where(kt == diag, qt % 2, jnp.where((kt == diag - 1) & (qt % 2 == 0), 2, 3))
        def scores(g):
            return _mm(ka_ref[g, pl.ds(start, _TK), :], qa_ref[g]) + map_ref[g, kind]

        def softmax(g, st):
            m_i, l_i, _ = carry[g]
            m_new = jnp.maximum(m_i, jnp.max(st, axis=0, keepdims=True))
            alpha = jnp.exp2(m_i - m_new)
            p = jnp.exp2(st - m_new)
            return m_new, alpha * l_i + jnp.sum(p, axis=0, keepdims=True), alpha, p.astype(BF16)

        def values(g, sm):
            m_new, l_new, alpha, pb = sm
            return m_new, l_new, alpha * carry[g][2] + _mm(vt_ref[g, :, pl.ds(start, _TK)], pb)

        return tuple(_staggered(N_KV, scores, softmax, values, ahead=2, lag=1))

    init = tuple((jnp.full((1, GROUP_ROWS), -jnp.inf, F32), jnp.zeros((1, GROUP_ROWS), F32),
                  jnp.zeros((HEAD_DIM, GROUP_ROWS), F32)) for _ in range(N_KV))
    final = lax.fori_loop(0, n_tiles, tile, init)
    for g in range(N_KV):
        _, l_f, acc = final[g]
        _store_heads(o_ref, acc / l_f, gate, g, 1)


def _sel_prompt(q_t, kv_t, selbias, smap, gates, batch, seq, riders=()):
    nqt = seq // QT
    n_sel = selbias.shape[1] // N_KV
    ride = _rider_specs(riders, batch * nqt, lambda b, i: (b * nqt + i, 0))
    out = pl.pallas_call(
        functools.partial(_sel_prompt_kernel, n_sel=n_sel, seq=seq, n_riders=len(riders)),
        grid=(batch, nqt),
        in_specs=[
            pl.BlockSpec((D_Q, QT), lambda b, i: (0, b * nqt + i)),
            pl.BlockSpec((1, D_KV, seq), lambda b, i: (b, 0, 0)),
            pl.BlockSpec((None, N_KV * n_sel, QT), lambda b, i: (b * nqt + i, 0, 0)),
            pl.BlockSpec(smap.shape, lambda b, i: (0, 0, 0, 0)),
            pl.BlockSpec((QT, LANES), lambda b, i: (b * nqt + i, 0)),
        ] + ride,
        out_specs=[pl.BlockSpec((QT, D_Q), lambda b, i: (b * nqt + i, 0))] + ride,
        out_shape=[jax.ShapeDtypeStruct((batch * seq, D_Q), F32)]
        + [jax.ShapeDtypeStruct(a.shape, BF16) for a in riders],
        scratch_shapes=[pltpu.VMEM((N_KV, seq, 2 * HEAD_DIM), BF16), pltpu.VMEM((N_KV, HEAD_DIM, seq), BF16),
                        pltpu.VMEM((N_KV, 2 * HEAD_DIM, GROUP_ROWS), BF16)],
        compiler_params=_params(("arbitrary", "arbitrary"), 56),
        name="sel_attn_prompt",
    )(q_t, kv_t, selbias, smap, gates, *riders)
    return out[0], out[1:]


_WIN_TILES = WINDOW // QT + 1


def _win_prompt_kernel(qt_ref, *refs):
    kv_refs = refs[:_WIN_TILES]
    map_ref, gate_ref, o_ref = refs[_WIN_TILES:]
    qt = pl.program_id(1)
    gate = gate_ref[...]
    half = D_KV // 2
    span = _WIN_TILES * QT
    key = lax.broadcasted_iota(I32, (span, GROUP_ROWS), 0)
    in_seq = key >= (_WIN_TILES - 1 - qt) * QT
    k_rows = jnp.concatenate([r[0, pl.ds(0, half), :].T for r in kv_refs], axis=0)
    def scores(g):
        kg = k_rows[:, g * HEAD_DIM:(g + 1) * HEAD_DIM].astype(BF16)
        return jnp.where(in_seq, _mm(kg, _group_queries(qt_ref, g)) + map_ref[g], NEG)

    def softmax(g, st):
        e = jnp.exp2(st - jnp.max(st, axis=0, keepdims=True))
        return (e / jnp.sum(e, axis=0, keepdims=True)).astype(BF16)

    def values(g, pb):
        vgt = jnp.concatenate([r[0, pl.ds(half + g * HEAD_DIM, HEAD_DIM), :] for r in kv_refs],
                              axis=1).astype(BF16)
        _store_heads(o_ref, _mm(vgt, pb), gate, g, 2)

    _staggered(N_KV, scores, softmax, values, ahead=2, lag=1)


def _win_prompt(q_t, kv_t, wmap, gates, batch, seq):
    nqt = seq // QT

    def kv_spec(k):
        return pl.BlockSpec((1, D_KV, QT), lambda b, i: (b, 0, jnp.maximum(i - (_WIN_TILES - 1) + k, 0)))

    return pl.pallas_call(
        _win_prompt_kernel,
        grid=(batch, nqt),
        in_specs=[pl.BlockSpec((D_Q, QT), lambda b, i: (0, b * nqt + i))]
        + [kv_spec(k) for k in range(_WIN_TILES)]
        + [pl.BlockSpec(wmap.shape, lambda b, i: (0, 0, 0)),
           pl.BlockSpec((QT, LANES), lambda b, i: (b * nqt + i, 0))],
        out_specs=pl.BlockSpec((QT, D_Q), lambda b, i: (b * nqt + i, 0)),
        out_shape=jax.ShapeDtypeStruct((batch * seq, D_Q), F32),
        compiler_params=_params(("parallel", "parallel")),
        name="win_attn_prompt",
    )(q_t, *([kv_t] * _WIN_TILES), wmap, gates)


def _group_rows(x_by_group):
    row = lax.broadcasted_iota(I32, x_by_group[0].shape, 0)
    out = x_by_group[0]
    for g in range(1, N_KV):
        out = jnp.where(row // HPG == g, x_by_group[g], out)
    return out


_CMP_SAMPLE_ROWS = 4


def _cmp_sample_kernel(q_ref, kc_ref, bias_ref, ov_ref, gate_ref, o_ref, idx_ref, *, n_sel, past):
    bias = bias_ref[...]
    half = D_KV // 2
    imps = []
    for bl in range(_CMP_SAMPLE_ROWS):
        q = q_ref[bl]
        kv = kc_ref[bl].astype(BF16)
        s = _group_rows([_mm_nt(q, kv[:, g * HEAD_DIM:(g + 1) * HEAD_DIM]) for g in range(N_KV)]) + bias
        e = jnp.exp(s - jnp.max(s, axis=-1, keepdims=True))
        p = e / jnp.sum(e, axis=-1, keepdims=True) * (bias > 0.5 * NEG).astype(F32)
        pb = p.astype(BF16)
        o = _group_rows([_mm(pb, kv[:, half + g * HEAD_DIM:half + (g + 1) * HEAD_DIM]) for g in range(N_KV)])
        o_ref[bl] = o * gate_ref[bl]
        imp_h = _mm(pb, ov_ref[...])
        imps += [imp_h[g * HPG:g * HPG + 1] + imp_h[g * HPG + 1:g * HPG + 2] + imp_h[g * HPG + 2:g * HPG + 3]
                 + imp_h[g * HPG + 3:g * HPG + 4] for g in range(N_KV)]
    imp = jnp.concatenate(imps, axis=0)
    rows, nsp = imp.shape
    j = lax.broadcasted_iota(I32, (rows, nsp), 1)
    cur = past // SEL_LEN
    forced = (j == 0) | (j == cur) | (j == cur - 1)
    valid = (j * SEL_LEN <= past) & (j < n_sel)
    score = jnp.where(valid, jnp.where(forced, jnp.inf, imp), -jnp.inf)
    slot = lax.broadcasted_iota(I32, (rows, LANES), 1)
    res = jnp.zeros((rows, LANES), I32)
    for k in range(min(SEL_TOP, n_sel)):
        best = jnp.max(score, axis=-1, keepdims=True)
        pick = jnp.min(jnp.where(score == best, j, nsp), axis=-1, keepdims=True)
        res = jnp.where(slot == k, pick, res)
        score = jnp.where(j == pick, -jnp.inf, score)
    idx_ref[...] = res


def _cmp_sample(q3, kcmp, bias, ov, gate3, n_sel, past):
    bs, n_chunk, _ = kcmp.shape
    nb = _CMP_SAMPLE_ROWS
    heads = pl.BlockSpec((nb, N_HEADS, HEAD_DIM), lambda b: (b, 0, 0))
    return pl.pallas_call(
        functools.partial(_cmp_sample_kernel, n_sel=n_sel, past=past),
        grid=(bs // nb,),
        in_specs=[
            heads,
            pl.BlockSpec((nb, n_chunk, D_KV), lambda b: (b, 0, 0)),
            pl.BlockSpec(bias.shape, lambda b: (0, 0)),
            pl.BlockSpec(ov.shape, lambda b: (0, 0)),
            heads,
        ],
        out_specs=[heads, pl.BlockSpec((nb * N_KV, LANES), lambda b: (b, 0))],
        out_shape=[
            jax.ShapeDtypeStruct((bs, N_HEADS, HEAD_DIM), F32),
            jax.ShapeDtypeStruct((bs * N_KV, LANES), I32),
        ],
        compiler_params=_params(("parallel",)),
        name="cmp_attn_sample",
    )(q3, kcmp, bias, ov, gate3)


def _sel_sample_kernel(blk_ref, phys_ref, q_ref, new_ref, bias_ref, gate_ref, *refs, top, n_pages):
    kv_refs = refs[:top]
    k_refs = [r.at[0] for r in kv_refs]
    v_refs = [r.at[1] for r in kv_refs]
    o_ref = refs[top]
    del phys_ref
    b = pl.program_id(0)
    g = pl.program_id(1)
    q = q_ref[0, 0]
    lane = lax.broadcasted_iota(I32, (8, PAGE), 1)
    scores = []
    for k in range(top):
        blk = blk_ref[(b * N_KV + g) * top + k]
        page = jnp.minimum(blk // 2, n_pages)
        s = _mm(q, k_refs[k][...].astype(BF16)) + bias_ref[page]
        ok = (lane // SEL_LEN == blk % 2) & (blk < 2 * n_pages)
        scores.append(jnp.where(ok, s, NEG))
    k_new = new_ref[0, 0, 0:1, :].astype(BF16).astype(F32)
    v_new = new_ref[0, 0, 1:2, :].astype(BF16).astype(F32)
    s_new = jnp.sum(q.astype(F32) * k_new, axis=-1, keepdims=True) + bias_ref[n_pages]
    scores.append(s_new)
    s_all = jnp.concatenate(scores, axis=1)
    e = jnp.exp(s_all - jnp.max(s_all, axis=-1, keepdims=True))
    p = e / jnp.sum(e, axis=-1, keepdims=True)
    o = p[:, top * PAGE:top * PAGE + 1].astype(BF16).astype(F32) * v_new
    for k in range(top):
        o = o + _mm_nt(p[:, k * PAGE:(k + 1) * PAGE].astype(BF16), v_refs[k][...].astype(BF16))
    o_ref[0, 0] = o * gate_ref[0, 0]


def _sel_sample(blk, phys, q4, new4, bias4, gate4, pool5, top, n_pages):
    bs = q4.shape[0]

    def kv_spec(k):
        return pl.BlockSpec(
            (None, 2, None, HEAD_DIM, PAGE),
            lambda b, g, blk_r, phys_r: (phys_r[(b * N_KV + g) * top + k], 0, g, 0, 0))

    grp = lambda b, g, blk_r, phys_r: (b, g, 0, 0)
    grid_spec = pltpu.PrefetchScalarGridSpec(
        num_scalar_prefetch=2,
        grid=(bs, N_KV),
        in_specs=[
            pl.BlockSpec((1, 1, 8, HEAD_DIM), grp),
            pl.BlockSpec((1, 1, 8, HEAD_DIM), grp),
            pl.BlockSpec((None, n_pages + 1, 8, LANES), lambda b, g, blk_r, phys_r: (g, 0, 0, 0)),
            pl.BlockSpec((1, 1, 8, HEAD_DIM), grp),
        ] + [kv_spec(k) for k in range(top)],
        out_specs=pl.BlockSpec((1, 1, 8, HEAD_DIM), grp),
    )
    return pl.pallas_call(
        functools.partial(_sel_sample_kernel, top=top, n_pages=n_pages),
        grid_spec=grid_spec,
        out_shape=jax.ShapeDtypeStruct((bs, N_KV, 8, HEAD_DIM), F32),
        compiler_params=_params(("parallel", "parallel")),
        name="sel_attn_sample",
    )(blk, phys, q4, new4, bias4, gate4, *([pool5] * top))


def _win_sample_kernel(q_ref, new_ref, st_ref, col_ref, bias_ref, gate_ref, o_ref, nst_ref):
    st = st_ref[0]
    half = D_KV // 2
    for g in range(N_KV):
        q = q_ref[0, g]
        bias = bias_ref[g]
        s_old = _mm(q, st[g * HEAD_DIM:(g + 1) * HEAD_DIM, :].astype(BF16)) + bias[:, 0:WINDOW]
        k_new = new_ref[0, g, 0:1, :].astype(BF16).astype(F32)
        v_new = new_ref[0, g, 1:2, :].astype(BF16).astype(F32)
        s_new = jnp.sum(q.astype(F32) * k_new, axis=-1, keepdims=True) + bias[:, WINDOW:WINDOW + LANES]
        s_all = jnp.concatenate([s_old, s_new], axis=1)
        e = jnp.exp(s_all - jnp.max(s_all, axis=-1, keepdims=True))
        p = e / jnp.sum(e, axis=-1, keepdims=True)
        o = _mm_nt(p[:, 0:WINDOW].astype(BF16), st[half + g * HEAD_DIM:half + (g + 1) * HEAD_DIM, :].astype(BF16))
        o = o + p[:, WINDOW:WINDOW + 1].astype(BF16).astype(F32) * v_new
        o_ref[0, g] = o * gate_ref[0, g]
    rolled = pltpu.roll(st, WINDOW - 1, axis=1)
    lane = lax.broadcasted_iota(I32, (D_KV, LANES), 1)
    nst_ref[0, :, pl.ds(0, WINDOW - LANES)] = rolled[:, 0:WINDOW - LANES]
    nst_ref[0, :, pl.ds(WINDOW - LANES, LANES)] = jnp.where(
        lane == LANES - 1, col_ref[0], rolled[:, WINDOW - LANES:WINDOW])


def _win_sample(q4, new4, state_t, newcol, bias4, gate4):
    bs = q4.shape[0]
    grp = pl.BlockSpec((1, N_KV, 8, HEAD_DIM), lambda b: (b, 0, 0, 0))
    return pl.pallas_call(
        _win_sample_kernel,
        grid=(bs,),
        in_specs=[
            grp, grp,
            pl.BlockSpec((1, D_KV, WINDOW), lambda b: (b, 0, 0)),
            pl.BlockSpec((1, D_KV, LANES), lambda b: (b, 0, 0)),
            pl.BlockSpec(bias4.shape, lambda b: (0, 0, 0)),
            grp,
        ],
        out_specs=[grp, pl.BlockSpec((1, D_KV, WINDOW), lambda b: (b, 0, 0))],
        out_shape=[
            jax.ShapeDtypeStruct((bs, N_KV, 8, HEAD_DIM), F32),
            jax.ShapeDtypeStruct((bs, D_KV, WINDOW), F32),
        ],
        compiler_params=_params(("parallel",)),
        name="win_attn_sample",
    )(q4, new4, state_t, newcol, bias4, gate4)


_MOE_ROWS = 256
_MOE_SLACK = 2 * _MOE_ROWS


def _row_copy(src, src_row, dst, dst_row, sem):
    return pltpu.make_async_copy(src.at[pl.ds(src_row, 1)], dst.at[pl.ds(dst_row, 1)], sem)


def _expert_kernel(be_ref, nu_ref, dst_ref, src_ref, nxt_ref, h_ref, w_ref, wg_ref, wu_ref, wd_ref, yk_ref,
                   xbuf, ybuf, gsem, ssem, *, n_rows, n_live):
    del be_ref
    i = pl.program_id(0)
    slot = i % 2
    n_used = nu_ref[0]

    def gather(rows_ref, dst_slot):
        for r in range(_MOE_ROWS):
            _row_copy(h_ref, rows_ref[0, r], xbuf.at[dst_slot], r, gsem.at[dst_slot]).start()

    def wait_gather(s):
        for r in range(_MOE_ROWS):
            _row_copy(h_ref, 0, xbuf.at[s], 0, gsem.at[s]).wait()

    def scatter(s):
        for r in range(_MOE_ROWS):
            _row_copy(ybuf.at[s], r, yk_ref, dst_ref[0, r], ssem.at[s]).start()

    def wait_scatter(s):
        for r in range(_MOE_ROWS):
            _row_copy(ybuf.at[s], 0, yk_ref, 0, ssem.at[s]).wait()

    @pl.when(i == 0)
    def _():
        gather(src_ref, 0)
        ybuf[1] = jnp.zeros((_MOE_ROWS, ybuf.shape[2]), F32)
        fills = [
            pltpu.make_async_copy(ybuf.at[1, pl.ds(0, min(_MOE_ROWS, n_rows - off))],
                                  yk_ref.at[pl.ds(k * n_rows + off, min(_MOE_ROWS, n_rows - off))], ssem.at[1])
            for k in range(TOP_K) for off in range(n_live, n_rows, _MOE_ROWS)]
        for cp in fills:
            cp.start()
        for cp in fills:
            cp.wait()

    @pl.when(i + 1 < n_used)
    def _():
        gather(nxt_ref, 1 - slot)

    @pl.when(i < n_used)
    def _():
        wait_gather(slot)

        @pl.when(i >= 2)
        def _():
            wait_scatter(slot)

        x = xbuf[slot].astype(BF16)
        a = _silu(_mm(x, wg_ref[...])) * _mm(x, wu_ref[...])
        ybuf[slot] = _mm(a.astype(BF16), wd_ref[...]) * w_ref[...]
        scatter(slot)

        @pl.when(i == n_used - 1)
        def _():
            @pl.when(i >= 1)
            def _():
                wait_scatter(1 - slot)

            wait_scatter(slot)


def _experts(blk_e, n_used, rows_src, rows_dst, rows_w, h2_all, n_live, w_gate, w_up, w_down):
    n_rows, d = h2_all.shape
    n_blk = blk_e.shape[0]
    src3 = rows_src.reshape(n_blk, 1, _MOE_ROWS)
    dst3 = rows_dst.reshape(n_blk, 1, _MOE_ROWS)
    wspec = lambda a: pl.BlockSpec((None,) + a.shape[1:], lambda i, be, nu: (be[i], 0, 0))
    smem_blk = lambda fn: pl.BlockSpec((None, 1, _MOE_ROWS), fn, memory_space=pltpu.SMEM)
    grid_spec = pltpu.PrefetchScalarGridSpec(
        num_scalar_prefetch=2,
        grid=(n_blk,),
        in_specs=[
            smem_blk(lambda i, be, nu: (i, 0, 0)),
            smem_blk(lambda i, be, nu: (i, 0, 0)),
            smem_blk(lambda i, be, nu: (jnp.minimum(i + 1, n_blk - 1), 0, 0)),
            pl.BlockSpec(memory_space=pl.ANY),
            pl.BlockSpec((_MOE_ROWS, 1), lambda i, be, nu: (i, 0)),
            wspec(w_gate), wspec(w_up), wspec(w_down),
        ],
        out_specs=pl.BlockSpec(memory_space=pl.ANY),
        scratch_shapes=[
            pltpu.VMEM((2, _MOE_ROWS, d), F32),
            pltpu.VMEM((2, _MOE_ROWS, d), F32),
            pltpu.SemaphoreType.DMA((2,)),
            pltpu.SemaphoreType.DMA((2,)),
        ],
    )
    return pl.pallas_call(
        functools.partial(_expert_kernel, n_rows=n_rows, n_live=n_live),
        grid_spec=grid_spec,
        out_shape=jax.ShapeDtypeStruct((TOP_K * n_rows, d), F32),
        compiler_params=pltpu.CompilerParams(dimension_semantics=("arbitrary",), vmem_limit_bytes=56 * MIB,
                                             disable_bounds_checks=True),
        name="moe_experts",
    )(blk_e, n_used, dst3, src3, src3, h2_all, rows_w, w_gate, w_up, w_down)


def _combine_kernel(x_ref, y0_ref, y1_ref, g_ref, o_ref):
    x = x_ref[...] + (y0_ref[...] + y1_ref[...])
    ms = jnp.mean(x * x, axis=-1, keepdims=True)
    o_ref[...] = x * lax.rsqrt(ms + EPS) * g_ref[...]


def _combine(x1, yk, gfin, row0, tm):
    n, d = x1.shape
    yk3 = yk.reshape(TOP_K, -1, d)
    blk0 = row0 // tm
    return pl.pallas_call(
        _combine_kernel,
        grid=(n // tm,),
        in_specs=[
            pl.BlockSpec((tm, d), lambda i: (i, 0)),
            pl.BlockSpec((None, tm, d), lambda i: (0, blk0 + i, 0)),
            pl.BlockSpec((None, tm, d), lambda i: (1, blk0 + i, 0)),
            pl.BlockSpec((1, d), lambda i: (0, 0)),
        ],
        out_specs=pl.BlockSpec((tm, d), lambda i: (i, 0)),
        out_shape=jax.ShapeDtypeStruct((n, d), F32),
        compiler_params=_params(("parallel",)),
        name="moe_combine",
    )(x1, yk3, yk3, gfin)


def _dispatch(eid, w, counts, n_rows):
    a = eid.shape[0]
    tb = _MOE_ROWS
    order = jnp.argsort(eid).astype(I32)
    padded = (counts + tb - 1) // tb * tb
    seg_end = jnp.cumsum(padded)
    seg_start = seg_end - padded
    start = jnp.cumsum(counts) - counts
    n_blk = (a + N_EXPERTS * (tb - 1) + tb - 1) // tb
    blk_e = jnp.minimum(jnp.sum((seg_end[None, :] <= (jnp.arange(n_blk) * tb)[:, None]).astype(I32), axis=1),
                        N_EXPERTS - 1)
    blk = jnp.arange(n_blk)[:, None]
    r = jnp.arange(tb)[None, :]
    off = (blk * tb - seg_start[blk_e][:, None]) + r
    valid = off < counts[blk_e][:, None]
    asg = order[jnp.clip(start[blk_e][:, None] + off, 0, a - 1)]
    rows_src = jnp.where(valid, asg // TOP_K, 0).astype(I32)
    rows_dst = jnp.where(valid, (asg % TOP_K) * n_rows + asg // TOP_K, (blk % 2) * n_rows + (n_rows - tb) + r)
    rows_w = jnp.where(valid, w[asg], 0.0)
    n_used = (seg_end[-1] // tb).astype(I32).reshape(1)
    return rows_src, rows_dst.astype(I32), rows_w.reshape(-1, 1), blk_e.astype(I32), n_used


def _overlap(n_chunk, n_sel, n_sel_pad):
    cs = jnp.arange(n_chunk)[:, None] * CMP_STRIDE
    ss = jnp.arange(n_sel_pad)[None, :] * SEL_LEN
    hit = (cs < ss + SEL_LEN) & (cs + CMP_LEN > ss) & (jnp.arange(n_sel_pad)[None, :] < n_sel)
    return hit.astype(BF16)


def kernel(x_prompt, x_sample, cache_cmp_kv, cache_sel_kv, state_win_kv, state_conv, page_table, rel_bias,
           norm_mix, w_in, b_in, conv_w, conv_b, conv_ln_g, conv_ln_b, w_conv_out, b_conv_out,
           cmp_pe_k, cmp_w1_k, cmp_w2_k, cmp_pe_v, cmp_w1_v, cmp_w2_v, w_nsa_out, w_out, norm_ffn,
           w_rg, b_rg, w_re, b_re, w_gate, w_up, w_down, norm_final):
    batch, seq, _ = x_prompt.shape
    bs = x_sample.shape[0]
    n_pool = cache_cmp_kv.shape[1]
    n_pages = page_table.shape[1]
    past = n_pages * PAGE
    n_tok = batch * seq
    assert bs + _MOE_ROWS <= _MOE_SLACK and n_tok % bs == 0 and seq % _TK == 0 and bs % 16 == 0
    l = 0

    wt = w_in[l].T.astype(BF16)
    bias_in = b_in[l]
    cuts = [0, D_CONV, 2 * D_CONV, 2 * D_CONV + D_Q, 2 * D_CONV + D_Q + 3 * D_KV]
    cuts += [cuts[-1] + 3 * N_HEADS, cuts[-1] + 3 * N_HEADS + D_MODEL, cuts[-1] + 3 * N_HEADS + 2 * D_MODEL]
    seg = lambda k: (wt[cuts[k]:cuts[k + 1]], bias_in[cuts[k]:cuts[k + 1]][None, :])
    (wa, ba), (wb, bb), (wq, bq), (wkv, bkv), (wzg, bzg), (wga, bga), (wgb, bgb) = [seg(k) for k in range(7)]
    wzg = jnp.pad(wzg, ((0, LANES - 3 * N_HEADS), (0, 0)))
    bzg = jnp.pad(bzg, ((0, 0), (0, LANES - 3 * N_HEADS)))
    wc = w_conv_out[l].astype(BF16)
    wn = w_nsa_out[l].astype(BF16)
    wo = w_out[l].astype(BF16)
    wr = jnp.zeros((LANES, D_MODEL), F32).at[0:N_GROUPS].set(w_rg[l].T).at[8:8 + N_EXPERTS].set(w_re[l].T)
    br = jnp.zeros((LANES,), F32).at[0:N_GROUPS].set(b_rg[l]).at[8:8 + N_EXPERTS].set(b_re[l].reshape(-1))
    wr, br = wr.astype(BF16), br[:, None]
    cw = _compress_weights(cmp_pe_k[l], cmp_w1_k[l], cmp_w2_k[l], cmp_pe_v[l], cmp_w1_v[l], cmp_w2_v[l])
    rb_t = rel_bias.T
    gmix, gffn, gfin = norm_mix[l][None, :], norm_ffn[l][None, :], norm_final[None, :]
    vec = lambda a: a[l][None, :]

    q_scale = HEAD_DIM ** -0.5
    xp = x_prompt.reshape(n_tok, D_MODEL)
    xs = x_sample.reshape(bs, D_MODEL)
    flat = lambda w: w[l].reshape(-1, w.shape[-1])
    (hp, up, qp, gates_p, *kv_p), (wg_bf,) = _project_prompt(
        xp, gmix, wa, wb, wq, wkv, wzg, ba, bb, bq.reshape(-1, 1), bkv.reshape(-1, 1), bzg,
        q_scale * LOG2E, batch, seq, 256, riders=(flat(w_gate),))
    hs = _rmsnorm(xs, gmix, bs)
    us = _glu(hs, wa, wb, ba, bb, bs, 512)
    qs = _proj(hs, wq, bq, lambda z: z * q_scale, BF16, bs, 512, "proj_q")
    gates_s = _proj(hs, wzg, bzg, _sigmoid, F32, bs, LANES, "proj_head_gates")
    kv_s = _proj(hs, wkv, bkv, lambda z: z, F32, bs, 512, "proj_kv_sample")

    conv_args = (conv_w[l], vec(conv_b), vec(conv_ln_g), vec(conv_ln_b))
    cp = _conv_prompt(up, *conv_args, batch, seq, 256)
    cs, new_conv_s = _conv_sample(state_conv[l].transpose(1, 0, 2), us, *conv_args)

    nqt = seq // QT
    n_chunk_p = seq // CMP_STRIDE
    n_chunk_s = n_pages * _CHUNKS_PER_PAGE
    cmap, wmap, smap = _bias_prompt(rb_t, nqt, n_chunk_p)
    cbias_s, wbias_s, sbias_s = _bias_sample(rb_t, past, n_chunk_s, n_pages)

    ident = jnp.zeros((1,), I32)
    kcmp_p, _ = _compress(kv_p[0], ident, cw, batch, seq // PAGE, seq // PAGE, False)
    feat_major = lambda a: a.transpose(0, 2, 3, 4, 1)
    pool_cmp = feat_major(cache_cmp_kv[l]).reshape(n_pool, D_KV, PAGE)
    pp_s = min(32, n_pages)
    kcmp_s, (wd_bf,) = _compress(pool_cmp, page_table.reshape(-1), cw, bs, n_pages, pp_s, True,
                                 riders=(flat(w_down),))

    n_sel_p = seq // SEL_LEN
    o_cmp_p, selbias = _cmp_prompt(qp, kcmp_p, cmap, gates_p, _overlap(n_chunk_p, n_sel_p, n_sel_p).T,
                                   batch, seq)
    n_sel_s = -(-(past + 1) // SEL_LEN)
    n_sel_pad = -(-n_sel_s // LANES) * LANES
    top_s = min(SEL_TOP, n_sel_s)
    head_gate = lambda k: jnp.broadcast_to(
        gates_s[:, k * N_HEADS:(k + 1) * N_HEADS, None], (bs, N_HEADS, HEAD_DIM))
    qs3 = qs.reshape(bs, N_HEADS, HEAD_DIM)
    o_cmp_s, idx_s = _cmp_sample(qs3, kcmp_s, cbias_s, _overlap(n_chunk_s, n_sel_s, n_sel_pad), head_gate(0),
                                 n_sel_s, past)

    o_sel_p, (wu_bf,) = _sel_prompt(qp, kv_p[1], selbias, smap, gates_p, batch, seq, riders=(flat(w_up),))
    by_group = lambda a: jnp.pad(a.reshape(bs, N_KV, HPG, -1), ((0, 0), (0, 0), (0, 8 - HPG), (0, 0)))
    q4 = by_group(qs3)
    blk = idx_s[:, :top_s].reshape(-1)
    phys = jnp.take_along_axis(
        page_table, jnp.minimum(blk // 2, n_pages - 1).reshape(bs, -1), axis=1).reshape(-1)
    kv_new = lambda k: jnp.pad(
        kv_s[:, k * D_KV:(k + 1) * D_KV].reshape(bs, 2, N_KV, HEAD_DIM).transpose(0, 2, 1, 3),
        ((0, 0), (0, 0), (0, 6), (0, 0)))
    bias_by_group = lambda a: jnp.pad(
        jnp.moveaxis(a, -2, 0).reshape((N_KV, HPG) + a.shape[:-2] + a.shape[-1:]),
        ((0, 0), (0, 8 - HPG)) + ((0, 0),) * (a.ndim - 1))
    sb4 = jnp.moveaxis(bias_by_group(sbias_s), 1, 2)
    pool_sel = feat_major(cache_sel_kv[l])
    o_sel_s = _sel_sample(blk, phys, q4, kv_new(1), sb4, by_group(head_gate(1)), pool_sel, top_s, n_pages)

    o_win_p = _win_prompt(qp, kv_p[2], wmap, gates_p, batch, seq)
    win_t = feat_major(state_win_kv[l]).reshape(bs, D_KV, WINDOW)
    newcol = jnp.broadcast_to(kv_s[:, 2 * D_KV:3 * D_KV, None], (bs, D_KV, LANES))
    o_win_s, new_win_s = _win_sample(q4, kv_new(2), win_t, newcol, bias_by_group(wbias_s), by_group(head_gate(2)))

    m_p, _ = _merge_a(hp, cp, o_cmp_p, o_sel_p, o_win_p, wga, wgb, wc, wn, bga, bgb, vec(b_conv_out), 512, 512)
    (x1_p, h2_all, eid_p, rw_p, cnt_p), _ = _merge_b(xp, m_p, wo, gffn, wr, br, 256, slack=_MOE_SLACK)
    from_groups = lambda a: a[:, :, :HPG, :].reshape(bs, D_Q)
    m_s, _ = _merge_a(hs, cs, o_cmp_s.reshape(bs, D_Q), from_groups(o_sel_s), from_groups(o_win_s),
                      wga, wgb, wc, wn, bga, bgb, vec(b_conv_out), bs, 512)
    (x1_s, h2_all, eid_s, rw_s, cnt_s), _ = _merge_b(xs, m_s, wo, gffn, wr, br, bs, h2_all=h2_all, row0=n_tok)

    eid = jnp.concatenate([eid_p[:TOP_K].T, eid_s[:TOP_K].T], axis=0).reshape(-1)
    rw = jnp.concatenate([rw_p[:TOP_K].T, rw_s[:TOP_K].T], axis=0).reshape(-1)
    counts = jnp.sum(cnt_p[:, :, 0], axis=0) + jnp.sum(cnt_s[:, :, 0], axis=0)
    rows_src, rows_dst, rows_w, blk_e, n_used = _dispatch(eid, rw, counts, h2_all.shape[0])
    yk = _experts(blk_e, n_used, rows_src, rows_dst, rows_w, h2_all, n_tok + bs,
                  wg_bf.reshape(w_gate.shape[1:]), wu_bf.reshape(w_up.shape[1:]), wd_bf.reshape(w_down.shape[1:]))
    y_p = _combine(x1_p, yk, gfin, 0, 256)
    y_s = _combine(x1_s, yk, gfin, n_tok, bs)

    def cache_rows(a_t, rows):
        return a_t.reshape(a_t.shape[0], 2, N_KV, HEAD_DIM, rows).transpose(0, 4, 1, 2, 3)[None]

    new_row = lambda k: kv_s[:, k * D_KV:(k + 1) * D_KV].reshape(1, bs, 1, 2, N_KV, HEAD_DIM)
    hist = CONV_W - 1
    return (
        y_p.reshape(batch, seq, D_MODEL),
        y_s.reshape(bs, 1, D_MODEL),
        cache_rows(kv_p[0], seq),
        new_row(0),
        cache_rows(kv_p[1], seq),
        new_row(1),
        cache_rows(kv_p[2][:, :, seq - WINDOW:], WINDOW),
        cache_rows(new_win_s, WINDOW),
        up.reshape(batch, seq, D_CONV)[None, :, seq - hist:],
        new_conv_s.transpose(1, 0, 2)[None],
    )
```

```python
import functools
import math

import jax
import jax.numpy as jnp
from jax import lax
from jax.experimental import pallas as pl
from jax.experimental.pallas import tpu as pltpu

F32 = jnp.float32
BF16 = jnp.bfloat16
I32 = jnp.int32

D_MODEL = 2048
D_CONV = 1024
CONV_W = 31
N_HEADS = 16
HEAD_DIM = 64
N_KV = 4
HPG = N_HEADS // N_KV
D_Q = N_HEADS * HEAD_DIM
D_KV = 2 * N_KV * HEAD_DIM
CMP_LEN = 32
CMP_STRIDE = 16
CMP_HID = 64
SEL_LEN = 64
SEL_TOP = 16
WINDOW = 512
N_BUCKETS = 32
MAX_DIST = 128
N_GROUPS = 4
EXP_PER_GROUP = 8
N_EXPERTS = N_GROUPS * EXP_PER_GROUP
TOP_K = 2
D_EXPERT = 1024
EPS = 1e-6
NEG = -1e30
PAGE = 128
LANES = 128
QT = 128
GROUP_ROWS = HPG * QT
MIB = 1024 * 1024
LOG2E = math.log2(math.e)


def _params(sem, vmem_mib=48):
    return pltpu.CompilerParams(dimension_semantics=sem, vmem_limit_bytes=vmem_mib * MIB)


def _mm_nt(a, b):
    return lax.dot_general(a, b, (((1,), (1,)), ((), ())), preferred_element_type=F32)


def _mm(a, b):
    return jnp.dot(a, b, preferred_element_type=F32)


def _sigmoid(z):
    return 1.0 / (1.0 + jnp.exp(-z))


def _silu(z):
    return z * _sigmoid(z)


def _gelu_tanh(z):
    return 0.5 * z * (1.0 + jnp.tanh(math.sqrt(2.0 / math.pi) * (z + 0.044715 * (z * z * z))))


def _rmsnorm_kernel(x_ref, g_ref, o_ref):
    x = x_ref[...]
    ms = jnp.mean(x * x, axis=-1, keepdims=True)
    o_ref[...] = (x * lax.rsqrt(ms + EPS) * g_ref[...]).astype(o_ref.dtype)


def _rmsnorm(x, g, tm):
    n, d = x.shape
    return pl.pallas_call(
        _rmsnorm_kernel,
        grid=(n // tm,),
        in_specs=[pl.BlockSpec((tm, d), lambda i: (i, 0)), pl.BlockSpec((1, d), lambda i: (0, 0))],
        out_specs=pl.BlockSpec((tm, d), lambda i: (i, 0)),
        out_shape=jax.ShapeDtypeStruct((n, d), BF16),
        compiler_params=_params(("parallel",)),
        name="rmsnorm",
    )(x, g)


def _proj_kernel(h_ref, w_ref, b_ref, o_ref, *, act):
    z = _mm_nt(h_ref[...], w_ref[...]) + b_ref[...]
    o_ref[...] = act(z).astype(o_ref.dtype)


def _proj(h, wt, b, act, out_dtype, tm, tn, name):
    n, k = h.shape
    nout = wt.shape[0]
    return pl.pallas_call(
        functools.partial(_proj_kernel, act=act),
        grid=(n // tm, nout // tn),
        in_specs=[
            pl.BlockSpec((tm, k), lambda i, j: (i, 0)),
            pl.BlockSpec((tn, k), lambda i, j: (j, 0)),
            pl.BlockSpec((1, tn), lambda i, j: (0, j)),
        ],
        out_specs=pl.BlockSpec((tm, tn), lambda i, j: (i, j)),
        out_shape=jax.ShapeDtypeStruct((n, nout), out_dtype),
        compiler_params=_params(("parallel", "parallel")),
        name=name,
    )(h, wt, b)


def _proj_t_kernel(h_ref, w_ref, b_ref, o_ref, *, act):
    z = _mm_nt(w_ref[...], h_ref[...]) + b_ref[...]
    o_ref[...] = act(z).astype(o_ref.dtype)


def _proj_t(h, wt, bcol, act, out_dtype, tm, tn, name):
    n, k = h.shape
    nout = wt.shape[0]
    return pl.pallas_call(
        functools.partial(_proj_t_kernel, act=act),
        grid=(n // tm, nout // tn),
        in_specs=[
            pl.BlockSpec((tm, k), lambda i, j: (i, 0)),
            pl.BlockSpec((tn, k), lambda i, j: (j, 0)),
            pl.BlockSpec((tn, 1), lambda i, j: (j, 0)),
        ],
        out_specs=pl.BlockSpec((tn, tm), lambda i, j: (j, i)),
        out_shape=jax.ShapeDtypeStruct((nout, n), out_dtype),
        compiler_params=_params(("parallel", "parallel")),
        name=name,
    )(h, wt, bcol)


def _glu_kernel(h_ref, wa_ref, wb_ref, ba_ref, bb_ref, o_ref):
    h = h_ref[...]
    a = _mm_nt(h, wa_ref[...]) + ba_ref[...]
    b = _mm_nt(h, wb_ref[...]) + bb_ref[...]
    o_ref[...] = a * _sigmoid(b)


def _glu(h, wat, wbt, ba, bb, tm, tn):
    n, k = h.shape
    nout = wat.shape[0]
    wspec = pl.BlockSpec((tn, k), lambda i, j: (j, 0))
    bspec = pl.BlockSpec((1, tn), lambda i, j: (0, j))
    return pl.pallas_call(
        _glu_kernel,
        grid=(n // tm, nout // tn),
        in_specs=[pl.BlockSpec((tm, k), lambda i, j: (i, 0)), wspec, wspec, bspec, bspec],
        out_specs=pl.BlockSpec((tm, tn), lambda i, j: (i, j)),
        out_shape=jax.ShapeDtypeStruct((n, nout), F32),
        compiler_params=_params(("parallel", "parallel")),
        name="glu",
    )(h, wat, wbt, ba, bb)


def _kvt_kernel(h_ref, w_ref, b_ref, *o_refs):
    h = h_ref[...]
    for kind, o_ref in enumerate(o_refs):
        rows = pl.ds(kind * D_KV, D_KV)
        o_ref[0] = _mm_nt(w_ref[rows, :], h) + b_ref[rows, :]


def _kv_transposed(h, wt, bcol, batch, seq, tm):
    n, k = h.shape
    kinds = wt.shape[0] // D_KV
    nt = seq // tm
    return pl.pallas_call(
        _kvt_kernel,
        grid=(n // tm,),
        in_specs=[
            pl.BlockSpec((tm, k), lambda i: (i, 0)),
            pl.BlockSpec(wt.shape, lambda i: (0, 0)),
            pl.BlockSpec(bcol.shape, lambda i: (0, 0)),
        ],
        out_specs=[pl.BlockSpec((1, D_KV, tm), lambda i: (i // nt, 0, i % nt))] * kinds,
        out_shape=[jax.ShapeDtypeStruct((batch, D_KV, seq), F32)] * kinds,
        compiler_params=_params(("parallel",)),
        name="kv_transposed",
    )(h, wt, bcol)


def _project_prompt_kernel(x_ref, g_ref, wa_ref, wb_ref, wq_ref, wkv_ref, wzg_ref, ba_ref, bb_ref, bq_ref,
                           bkv_ref, bzg_ref, *rest, q_scale, n_riders, kinds):
    rider_in = rest[:n_riders]
    h_ref, u_ref, qt_ref, gates_ref = rest[n_riders:n_riders + 4]
    kv_refs = rest[n_riders + 4:n_riders + 4 + kinds]
    rider_out = rest[n_riders + 4 + kinds:]
    for src, dst in zip(rider_in, rider_out):
        dst[...] = src[...].astype(dst.dtype)
    x = x_ref[...]
    ms = jnp.mean(x * x, axis=-1, keepdims=True)
    h = (x * lax.rsqrt(ms + EPS) * g_ref[...]).astype(BF16)
    h_ref[...] = h
    a = _mm_nt(h, wa_ref[...]) + ba_ref[...]
    b = _mm_nt(h, wb_ref[...]) + bb_ref[...]
    u_ref[...] = a * _sigmoid(b)
    qt_ref[...] = ((_mm_nt(wq_ref[...], h) + bq_ref[...]) * q_scale).astype(qt_ref.dtype)
    gates_ref[...] = _sigmoid(_mm_nt(h, wzg_ref[...]) + bzg_ref[...])
    for kind, o_ref in enumerate(kv_refs):
        rows = pl.ds(kind * D_KV, D_KV)
        o_ref[0] = _mm_nt(wkv_ref[rows, :], h) + bkv_ref[rows, :]


def _project_prompt(x, g, wa, wb, wq, wkv, wzg, ba, bb, bq_col, bkv_col, bzg, q_scale, batch, seq, tm,
                    riders=()):
    n, d = x.shape
    kinds = wkv.shape[0] // D_KV
    nt = seq // tm
    tok = lambda width: pl.BlockSpec((tm, width), lambda i: (i, 0))
    const = lambda a: pl.BlockSpec(a.shape, lambda i: (0,) * a.ndim, pipeline_mode=pl.Buffered(1))
    consts = (g, wa, wb, wq, wkv, wzg, ba, bb, bq_col, bkv_col, bzg)
    ride = _rider_specs(riders, n // tm, lambda i: (i, 0))
    out = pl.pallas_call(
        functools.partial(_project_prompt_kernel, q_scale=q_scale, n_riders=len(riders), kinds=kinds),
        grid=(n // tm,),
        in_specs=[tok(d)] + [const(a) for a in consts] + ride,
        out_specs=[tok(d), tok(D_CONV), pl.BlockSpec((D_Q, tm), lambda i: (0, i)), tok(LANES)]
        + [pl.BlockSpec((1, D_KV, tm), lambda i: (i // nt, 0, i % nt))] * kinds + ride,
        out_shape=[
            jax.ShapeDtypeStruct((n, d), BF16),
            jax.ShapeDtypeStruct((n, D_CONV), F32),
            jax.ShapeDtypeStruct((D_Q, n), BF16),
            jax.ShapeDtypeStruct((n, LANES), F32),
        ] + [jax.ShapeDtypeStruct((batch, D_KV, seq), F32)] * kinds
        + [jax.ShapeDtypeStruct(a.shape, BF16) for a in riders],
        compiler_params=_params(("parallel",), 56),
        name="project_prompt",
    )(x, *consts, *riders)
    return out[:4 + kinds], out[4 + kinds:]


def _merge_a_kernel(h_ref, c_ref, o1_ref, o2_ref, o3_ref, wga_ref, wgb_ref, wc_ref, wn_ref,
                    bga_ref, bgb_ref, bc_ref, *rest):
    n_cast = (len(rest) - 1) // 2
    m_ref = rest[n_cast]
    h = h_ref[...]
    ga = _sigmoid(_mm_nt(h, wga_ref[...]) + bga_ref[...])
    gb = _sigmoid(_mm_nt(h, wgb_ref[...]) + bgb_ref[...])
    yc = _mm(c_ref[...], wc_ref[...]) + bc_ref[...]
    o = (o1_ref[...] + o2_ref[...] + o3_ref[...]).astype(BF16)
    yn = _mm(o, wn_ref[...])
    m_ref[...] = (ga * yc + gb * yn).astype(m_ref.dtype)
    for src, dst in zip(rest[:n_cast], rest[n_cast + 1:]):
        dst[...] = src[...].astype(dst.dtype)


def _rider_specs(riders, steps, index):
    return [pl.BlockSpec((a.shape[0] // steps, a.shape[1]), index) for a in riders]


def _merge_a(h, c, o1, o2, o3, wgat, wgbt, wc, wn, bga, bgb, bc, tm, tn, riders=()):
    n, k = h.shape
    nj = D_MODEL // tn
    tok = lambda width: pl.BlockSpec((tm, width), lambda i, j: (i, 0))
    wt_spec = pl.BlockSpec((tn, k), lambda i, j: (j, 0))
    w_spec = pl.BlockSpec((c.shape[1], tn), lambda i, j: (0, j))
    b_spec = pl.BlockSpec((1, tn), lambda i, j: (0, j))
    ride = _rider_specs(riders, (n // tm) * nj, lambda i, j: (i * nj + j, 0))
    out = pl.pallas_call(
        _merge_a_kernel,
        grid=(n // tm, nj),
        in_specs=[tok(k), tok(D_CONV), tok(D_Q), tok(D_Q), tok(D_Q), wt_spec, wt_spec, w_spec, w_spec,
                  b_spec, b_spec, b_spec] + ride,
        out_specs=[pl.BlockSpec((tm, tn), lambda i, j: (i, j))] + ride,
        out_shape=[jax.ShapeDtypeStruct((n, D_MODEL), BF16)]
        + [jax.ShapeDtypeStruct(a.shape, BF16) for a in riders],
        compiler_params=_params(("parallel", "parallel"), 56),
        name="merge_gates",
    )(h, c, o1, o2, o3, wgat, wgbt, wc, wn, bga, bgb, bc, *riders)
    return out[0], out[1:]


def _merge_b_kernel(x_ref, m_ref, wo_ref, gn_ref, wr_ref, br_ref, *rest, n_real, n_shared, n_riders):
    rider_in = rest[n_shared:n_shared + n_riders]
    outs = rest[n_shared + n_riders:n_shared + n_riders + 5]
    rider_out = rest[n_shared + n_riders + 5:]
    h2_ref = outs[1]

    @pl.when(pl.program_id(0) < n_real)
    def _():
        _merge_b_tile(x_ref, m_ref, wo_ref, gn_ref, wr_ref, br_ref, *outs)
        for src, dst in zip(rider_in, rider_out):
            dst[...] = src[...].astype(dst.dtype)

    @pl.when(pl.program_id(0) >= n_real)
    def _():
        h2_ref[...] = jnp.zeros(h2_ref.shape, h2_ref.dtype)


def _merge_b_tile(x_ref, m_ref, wo_ref, gn_ref, wr_ref, br_ref, x1_ref, h2_ref, eid_ref, rw_ref, cnt_ref):
    x1 = x_ref[...] + _mm(m_ref[...], wo_ref[...])
    x1_ref[...] = x1
    ms = jnp.mean(x1 * x1, axis=-1, keepdims=True)
    h2 = x1 * lax.rsqrt(ms + EPS) * gn_ref[...]
    h2_ref[...] = h2
    logits = _mm_nt(wr_ref[...], h2.astype(BF16)) + br_ref[...]
    tm = logits.shape[1]
    row = lax.broadcasted_iota(I32, (8, tm), 0)
    lg = jnp.where(row < N_GROUPS, logits[0:8], -jnp.inf)
    gmax = jnp.max(lg, axis=0, keepdims=True)
    gidx = jnp.min(jnp.where(lg == gmax, row, 8), axis=0, keepdims=True)
    wg = 1.0 / jnp.sum(jnp.exp(lg - gmax), axis=0, keepdims=True)
    le = jnp.zeros((8, tm), F32)
    for g in range(N_GROUPS):
        le = jnp.where(gidx == g, logits[8 + 8 * g:16 + 8 * g], le)
    ee = jnp.exp(le - jnp.max(le, axis=0, keepdims=True))
    pz = ee / jnp.sum(ee, axis=0, keepdims=True)
    p1 = jnp.max(pz, axis=0, keepdims=True)
    i1 = jnp.min(jnp.where(pz == p1, row, 8), axis=0, keepdims=True)
    pz2 = jnp.where(row == i1, -1.0, pz)
    p2 = jnp.max(pz2, axis=0, keepdims=True)
    i2 = jnp.min(jnp.where(pz2 == p2, row, 8), axis=0, keepdims=True)
    den = p1 + p2
    e1 = gidx * EXP_PER_GROUP + i1
    e2 = gidx * EXP_PER_GROUP + i2
    eid_ref[...] = jnp.where(row == 0, e1, jnp.where(row == 1, e2, 0))
    rw_ref[...] = jnp.where(row == 0, wg * p1 / den, jnp.where(row == 1, wg * p2 / den, 0.0))
    erow = lax.broadcasted_iota(I32, (N_EXPERTS, tm), 0)
    hits = (erow == e1).astype(F32) + (erow == e2).astype(F32)
    cnt_ref[...] = jnp.broadcast_to(jnp.sum(hits, axis=1, keepdims=True), (N_EXPERTS, LANES)).astype(I32)


def _merge_b(x, m, wo, gn, wr, br, tm, h2_all=None, row0=0, slack=0, riders=()):
    n, d = x.shape
    blk0 = row0 // tm
    n_real = n // tm
    real = lambda i: jnp.minimum(i, n_real - 1)
    shared = [] if h2_all is None else [h2_all]
    h2_rows = n + slack if h2_all is None else h2_all.shape[0]
    ride = _rider_specs(riders, n_real, lambda i: (real(i), 0))
    out = pl.pallas_call(
        functools.partial(_merge_b_kernel, n_real=n_real, n_shared=len(shared), n_riders=len(riders)),
        grid=((n + slack) // tm,),
        in_specs=[
            pl.BlockSpec((tm, d), lambda i: (real(i), 0)),
            pl.BlockSpec((tm, d), lambda i: (real(i), 0)),
            pl.BlockSpec((d, d), lambda i: (0, 0)),
            pl.BlockSpec((1, d), lambda i: (0, 0)),
            pl.BlockSpec((LANES, d), lambda i: (0, 0)),
            pl.BlockSpec((LANES, 1), lambda i: (0, 0)),
        ] + [pl.BlockSpec(memory_space=pl.ANY)] * len(shared) + ride,
        out_specs=[
            pl.BlockSpec((tm, d), lambda i: (real(i), 0)),
            pl.BlockSpec((tm, d), lambda i: (blk0 + i, 0)),
            pl.BlockSpec((8, tm), lambda i: (0, real(i))),
            pl.BlockSpec((8, tm), lambda i: (0, real(i))),
            pl.BlockSpec((None, N_EXPERTS, LANES), lambda i: (real(i), 0, 0)),
        ] + ride,
        out_shape=[
            jax.ShapeDtypeStruct((n, d), F32),
            jax.ShapeDtypeStruct((h2_rows, d), F32),
            jax.ShapeDtypeStruct((8, n), I32),
            jax.ShapeDtypeStruct((8, n), F32),
            jax.ShapeDtypeStruct((n_real, N_EXPERTS, LANES), I32),
        ] + [jax.ShapeDtypeStruct(a.shape, BF16) for a in riders],
        input_output_aliases={6: 1} if shared else {},
        compiler_params=_params(("arbitrary",), 56),
        name="merge_out_router",
    )(x, m, wo, gn, wr, br, *shared, *riders)
    return out[:5], out[5:]


_HALO = 32


def _conv_prompt_kernel(u_ref, prev_ref, w_ref, b_ref, g_ref, beta_ref, o_ref, buf_ref, acc_ref, sh_ref, *, tt):
    i = pl.program_id(1)
    prev = prev_ref[pl.ds(tt - _HALO, _HALO), :]
    buf_ref[pl.ds(0, _HALO), :] = jnp.where(i > 0, prev, 0.0)
    buf_ref[pl.ds(_HALO, tt), :] = u_ref[...]
    off = _HALO - (CONV_W - 1)
    sub = 8
    rows = 128

    def chunk(cc, carry):
        cols = pl.ds(pl.multiple_of(cc * LANES, LANES), LANES)
        for r in range(sub):
            n_shift = (tt + _HALO - r) // sub * sub
            sh_ref[r, pl.ds(0, n_shift), :] = buf_ref[pl.ds(r, n_shift), cols]
        for r0 in range(0, tt, rows):
            acc = jnp.zeros((rows, LANES), F32)
            for k in range(CONV_W):
                r = (off + k) % sub
                acc = acc + sh_ref[r, pl.ds(r0 + off + k - r, rows), :] * w_ref[pl.ds(k, 1), cols]
            acc_ref[pl.ds(r0, rows), cols] = acc + b_ref[:, cols]
        return carry

    lax.fori_loop(0, D_CONV // LANES, chunk, 0)
    c = acc_ref[...]
    mu = jnp.mean(c, axis=-1, keepdims=True)
    var = jnp.mean(jnp.square(c - mu), axis=-1, keepdims=True)
    y = (c - mu) * lax.rsqrt(var + EPS) * g_ref[...] + beta_ref[...]
    o_ref[...] = _silu(y).astype(o_ref.dtype)


def _conv_prompt(u, w, b, g, beta, batch, seq, tt):
    nt = seq // tt
    vec = pl.BlockSpec((1, D_CONV), lambda bi, i: (0, 0))
    return pl.pallas_call(
        functools.partial(_conv_prompt_kernel, tt=tt),
        grid=(batch, nt),
        in_specs=[
            pl.BlockSpec((tt, D_CONV), lambda bi, i: (bi * nt + i, 0)),
            pl.BlockSpec((tt, D_CONV), lambda bi, i: (bi * nt + jnp.maximum(i - 1, 0), 0)),
            pl.BlockSpec((CONV_W, D_CONV), lambda bi, i: (0, 0)),
            vec, vec, vec,
        ],
        out_specs=pl.BlockSpec((tt, D_CONV), lambda bi, i: (bi * nt + i, 0)),
        out_shape=jax.ShapeDtypeStruct((batch * seq, D_CONV), BF16),
        scratch_shapes=[pltpu.VMEM((tt + _HALO, D_CONV), F32), pltpu.VMEM((tt, D_CONV), F32),
                        pltpu.VMEM((8, tt + _HALO, LANES), F32)],
        compiler_params=_params(("parallel", "parallel")),
        name="conv_prompt",
    )(u, u, w, b, g, beta)


def _conv_sample_kernel(st_ref, u_ref, w_ref, b_ref, g_ref, beta_ref, c_ref, new_ref):
    hist = CONV_W - 1
    u = u_ref[...]
    acc = u * w_ref[pl.ds(hist, 1), :] + b_ref[...]
    for t in range(hist):
        acc = acc + st_ref[t] * w_ref[pl.ds(t, 1), :]
    mu = jnp.mean(acc, axis=-1, keepdims=True)
    var = jnp.mean(jnp.square(acc - mu), axis=-1, keepdims=True)
    y = (acc - mu) * lax.rsqrt(var + EPS) * g_ref[...] + beta_ref[...]
    c_ref[...] = _silu(y).astype(c_ref.dtype)
    for t in range(hist - 1):
        new_ref[t] = st_ref[t + 1]
    new_ref[hist - 1] = u


def _conv_sample(state_t, u, w, b, g, beta):
    hist, bs, _ = state_t.shape
    return pl.pallas_call(
        _conv_sample_kernel,
        out_shape=[jax.ShapeDtypeStruct((bs, D_CONV), BF16), jax.ShapeDtypeStruct((hist, bs, D_CONV), F32)],
        compiler_params=pltpu.CompilerParams(vmem_limit_bytes=48 * MIB),
        name="conv_sample",
    )(state_t, u, w, b, g, beta)


def _bias_lookup(rb_ref, head, dist, valid):
    n = jnp.maximum(dist, 0)
    max_exact = N_BUCKETS // 2
    nf = jnp.maximum(n, 1).astype(F32)
    large = max_exact + (jnp.log(nf / max_exact) / math.log(MAX_DIST / max_exact)
                         * (N_BUCKETS - max_exact)).astype(I32)
    bucket = jnp.where(n < max_exact, n, jnp.minimum(large, N_BUCKETS - 1))
    val = jnp.zeros(dist.shape, F32)
    for k in range(N_BUCKETS):
        val = jnp.where(bucket == k, rb_ref[head, k], val)
    return jnp.where(valid, val, NEG)


def _bias_prompt_kernel(rb_ref, cmp_ref, win_ref, sel_ref, *, nqt, n_chunk):
    g = pl.program_id(0)
    key = lax.broadcasted_iota(I32, (LANES, QT), 0)
    qry = lax.broadcasted_iota(I32, (LANES, QT), 1)
    for hh in range(HPG):
        head = g * HPG + hh
        cols = pl.ds(hh * QT, QT)
        lookup = lambda dist, valid: _bias_lookup(rb_ref, head, dist, valid) * LOG2E

        for eb in range(_cmp_map_rows(nqt, n_chunk) // LANES):
            block = key + eb * LANES - _cmp_map_shift(nqt, 0)
            dist = qry - block * CMP_STRIDE - (CMP_LEN - 1)
            cmp_ref[0, pl.ds(eb * LANES, LANES), cols] = lookup(dist, dist >= 0)
        for jb in range((WINDOW + QT) // LANES):
            dist = qry - (key + jb * LANES) + WINDOW
            win_ref[0, pl.ds(jb * LANES, LANES), cols] = lookup(dist, (dist >= 0) & (dist <= WINDOW))
        for kind in range(3):
            for jb in range(2):
                dist = kind * QT + qry - (key + jb * LANES)
                sel_ref[0, kind, pl.ds(jb * LANES, LANES), cols] = lookup(dist, dist >= 0)
        far = jnp.full((LANES, QT), 2 * MAX_DIST, I32)
        for jb in range(2):
            sel_ref[0, 3, pl.ds(jb * LANES, LANES), cols] = lookup(far, far >= 0)


def _cmp_map_shift(nqt, qt):
    return (nqt - 1 - qt) * (QT // CMP_STRIDE)


def _cmp_map_rows(nqt, n_chunk):
    return -(-(n_chunk + _cmp_map_shift(nqt, 0)) // LANES) * LANES


def _bias_prompt(rb_t, nqt, n_chunk):
    ext = _cmp_map_rows(nqt, n_chunk)
    return pl.pallas_call(
        functools.partial(_bias_prompt_kernel, nqt=nqt, n_chunk=n_chunk),
        grid=(N_KV,),
        in_specs=[pl.BlockSpec(memory_space=pltpu.SMEM)],
        out_specs=[
            pl.BlockSpec((1, ext, GROUP_ROWS), lambda g: (g, 0, 0)),
            pl.BlockSpec((1, WINDOW + QT, GROUP_ROWS), lambda g: (g, 0, 0)),
            pl.BlockSpec((1, 4, 2 * LANES, GROUP_ROWS), lambda g: (g, 0, 0, 0)),
        ],
        out_shape=[
            jax.ShapeDtypeStruct((N_KV, ext, GROUP_ROWS), F32),
            jax.ShapeDtypeStruct((N_KV, WINDOW + QT, GROUP_ROWS), F32),
            jax.ShapeDtypeStruct((N_KV, 4, 2 * LANES, GROUP_ROWS), F32),
        ],
        compiler_params=_params(("parallel",)),
        name="bias_maps_prompt",
    )(rb_t)


def _bias_sample_kernel(rb_ref, cmp_ref, win_ref, sel_ref, *, past, n_chunk, n_pages):
    col = lax.broadcasted_iota(I32, (N_HEADS, LANES), 1)
    rb = rb_ref[...]

    def lookup(dist):
        n = jnp.maximum(dist, 0)
        max_exact = N_BUCKETS // 2
        nf = jnp.maximum(n, 1).astype(F32)
        large = max_exact + (jnp.log(nf / max_exact) / math.log(MAX_DIST / max_exact)
                             * (N_BUCKETS - max_exact)).astype(I32)
        bucket = jnp.where(n < max_exact, n, jnp.minimum(large, N_BUCKETS - 1))
        val = jnp.zeros(dist.shape, F32)
        for k in range(N_BUCKETS):
            val = jnp.where(bucket == k, rb[:, k:k + 1], val)
        return jnp.where(dist >= 0, val, NEG)

    def cmp_block(cb, carry):
        dist = past - (col + cb * LANES) * CMP_STRIDE - (CMP_LEN - 1)
        cmp_ref[cb] = lookup(dist)
        return carry

    lax.fori_loop(0, n_chunk // LANES, cmp_block, 0)

    def win_block(jb, carry):
        win_ref[jb] = lookup(WINDOW - (col + jb * LANES))
        return carry

    lax.fori_loop(0, WINDOW // LANES + 1, win_block, 0)

    def sel_page(p, carry):
        sel_ref[p] = lookup(past - (p * PAGE + col))
        return carry

    lax.fori_loop(0, n_pages + 1, sel_page, 0)


def _bias_sample(rb_t, past, n_chunk, n_pages):
    cmp_b, win_b, sel_b = pl.pallas_call(
        functools.partial(_bias_sample_kernel, past=past, n_chunk=n_chunk, n_pages=n_pages),
        out_shape=[
            jax.ShapeDtypeStruct((n_chunk // LANES, N_HEADS, LANES), F32),
            jax.ShapeDtypeStruct((WINDOW // LANES + 1, N_HEADS, LANES), F32),
            jax.ShapeDtypeStruct((n_pages + 1, N_HEADS, LANES), F32),
        ],
        name="bias_maps_sample",
    )(rb_t)
    flat = lambda a: a.transpose(1, 0, 2).reshape(N_HEADS, -1)
    return flat(cmp_b), flat(win_b), sel_b


_CHUNKS_PER_PAGE = PAGE // CMP_STRIDE
_CHUNK_PITCH = 24


def _compress_kernel(pt_ref, *refs, pp, n_riders):
    page_refs = refs[:pp + 1]
    w1k_ref, w1v_ref, pe_ref, w2k_ref, w2v_ref = refs[pp + 1:pp + 6]
    rider_in = refs[pp + 6:pp + 6 + n_riders]
    o_ref = refs[pp + 6 + n_riders]
    rider_out = refs[pp + 7 + n_riders:pp + 7 + 2 * n_riders]
    tr_ref = refs[pp + 7 + 2 * n_riders]
    del pt_ref
    for src, dst in zip(rider_in, rider_out):
        dst[...] = src[...].astype(dst.dtype)
    nch = pp * _CHUNKS_PER_PAGE
    m = nch + 8
    n_slab = D_KV // LANES

    def transpose(s):
        for k in range(pp + 1):
            t = page_refs[k][pl.ds(s * LANES, LANES), :].T
            for n in range(_CHUNKS_PER_PAGE):
                row = (k * _CHUNKS_PER_PAGE + n) * _CHUNK_PITCH
                tr_ref[s, pl.ds(row, CMP_STRIDE), :] = t[n * CMP_STRIDE:(n + 1) * CMP_STRIDE]

    def chunks(s, _):
        return jnp.concatenate(
            [tr_ref.at[s][pl.ds(p, m, stride=_CHUNK_PITCH), :] for p in range(CMP_STRIDE)], axis=1)

    def mlp(s, x):
        is_v = int(s >= n_slab // 2)
        w1 = (w1v_ref if is_v else w1k_ref)[...]
        w2 = (w2v_ref if is_v else w2k_ref)[...]
        y = _mm(x.astype(BF16), w1)
        y_pe = _mm(pe_ref[...].astype(BF16), w1)
        pe_term = y_pe[2 * is_v:2 * is_v + 1, 0:LANES] + y_pe[2 * is_v + 1:2 * is_v + 2, LANES:2 * LANES]
        pre = y[0:nch, 0:LANES] + y[1:nch + 1, LANES:2 * LANES] + pe_term
        o_ref[0, :, pl.ds(s * LANES, LANES)] = _mm(_gelu_tanh(pre).astype(BF16), w2)

    _staggered(n_slab, transpose, chunks, mlp)


def _compress(pages, page_ids, weights, nb, n_pages, pp, paged, riders=()):
    steps = n_pages // pp
    nch = pp * _CHUNKS_PER_PAGE
    ride = _rider_specs(riders, nb * steps, lambda b, s, pt: (b * steps + s, 0))

    def page_spec(k):
        if paged:
            return pl.BlockSpec(
                (None, D_KV, PAGE),
                lambda b, s, pt: (pt[b * n_pages + jnp.minimum(s * pp + k, n_pages - 1)], 0, 0))
        return pl.BlockSpec((None, D_KV, PAGE), lambda b, s, pt: (b, 0, jnp.minimum(s * pp + k, n_pages - 1)))

    full = lambda a: pl.BlockSpec(a.shape, lambda b, s, pt: (0,) * a.ndim)
    grid_spec = pltpu.PrefetchScalarGridSpec(
        num_scalar_prefetch=1,
        grid=(nb, steps),
        in_specs=[page_spec(k) for k in range(pp + 1)] + [full(a) for a in weights] + ride,
        out_specs=[pl.BlockSpec((1, nch, D_KV), lambda b, s, pt: (b, s, 0))] + ride,
        scratch_shapes=[pltpu.VMEM((D_KV // LANES, (pp + 1) * _CHUNKS_PER_PAGE * _CHUNK_PITCH, LANES), F32)],
    )
    out = pl.pallas_call(
        functools.partial(_compress_kernel, pp=pp, n_riders=len(riders)),
        grid_spec=grid_spec,
        out_shape=[jax.ShapeDtypeStruct((nb, n_pages * _CHUNKS_PER_PAGE, D_KV), F32)]
        + [jax.ShapeDtypeStruct(a.shape, BF16) for a in riders],
        compiler_params=_params(("parallel", "parallel"), 56),
        name="compress_paged" if paged else "compress_prompt",
    )(page_ids, *([pages] * (pp + 1)), *weights, *riders)
    return out[0], out[1:]


def _compress_weights(pe_k, w1_k, w2_k, pe_v, w1_v, w2_v):
    eye = jnp.eye(2, dtype=F32)

    def first(w1, j):
        w = w1[j * CMP_STRIDE:(j + 1) * CMP_STRIDE]
        return jnp.einsum("pdf,gh->pgdhf", w, eye).reshape(CMP_STRIDE * LANES, 2 * CMP_HID).astype(BF16)

    def pos(pe, j):
        return jnp.tile(pe[j * CMP_STRIDE:(j + 1) * CMP_STRIDE, None, :], (1, 2, 1)).reshape(1, CMP_STRIDE * LANES)

    def second(w2):
        return jnp.einsum("fd,gh->gfhd", w2, eye).reshape(2 * CMP_HID, 2 * HEAD_DIM).astype(BF16)

    pe = jnp.concatenate([pos(pe_k, 0), pos(pe_k, 1), pos(pe_v, 0), pos(pe_v, 1)], axis=0)
    pe = jnp.pad(pe, ((0, 8 - pe.shape[0]), (0, 0)))
    both = lambda w1: jnp.concatenate([first(w1, 0), first(w1, 1)], axis=1)
    return (both(w1_k), both(w1_v), pe, second(w2_k), second(w2_v))


def _staggered(n, scores, softmax, values, ahead=1, lag=0):
    s = {g: scores(g) for g in range(min(ahead, n))}
    p = {}
    out = []
    for g in range(n + lag):
        if g + ahead < n:
            s[g + ahead] = scores(g + ahead)
        if g < n:
            p[g] = softmax(g, s.pop(g))
        if g - lag >= 0:
            out.append(values(g - lag, p.pop(g - lag)))
    return out


def _group_queries(qt_ref, g):
    return jnp.concatenate(
        [qt_ref[pl.ds((g * HPG + hh) * HEAD_DIM, HEAD_DIM), :] for hh in range(HPG)], axis=1)


def _store_heads(o_ref, o_t, gate, g, branch):
    lane = lax.broadcasted_iota(I32, (QT, 2 * HEAD_DIM), 1)
    for pair in range(HPG // 2):
        h = g * HPG + 2 * pair
        col = branch * N_HEADS + h
        both = jnp.concatenate([o_t[:, (2 * pair) * QT:(2 * pair + 1) * QT],
                                o_t[:, (2 * pair + 1) * QT:(2 * pair + 2) * QT]], axis=0)
        gates = jnp.where(lane < HEAD_DIM, gate[:, col:col + 1], gate[:, col + 1:col + 2])
        o_ref[:, pl.ds(h * HEAD_DIM, 2 * HEAD_DIM)] = both.T * gates


def _cmp_prompt_kernel(qt_ref, kc_ref, map_ref, gate_ref, ov_ref, o_ref, sb_ref, *, n_sel, top):
    qt = pl.program_id(1)
    kv = kc_ref[0]
    gate = gate_ref[...]
    ov = ov_ref[...]
    half = D_KV // 2
    n_chunk = kv.shape[0]
    map_rows = pl.ds(pl.multiple_of(_cmp_map_shift(pl.num_programs(1), qt), 8), n_chunk)
    bias = lambda g: map_ref[g, map_rows, :]
    vt_all = kv[:, half:].T
    jrow = lax.broadcasted_iota(I32, (n_sel, QT), 0)
    tpos = qt * QT + lax.broadcasted_iota(I32, (n_sel, QT), 1)
    cur = tpos // SEL_LEN
    forced = (jrow == 0) | (jrow == cur) | (jrow == cur - 1)
    valid = jrow * SEL_LEN <= tpos
    def scores(g):
        kg = kv[:, g * HEAD_DIM:(g + 1) * HEAD_DIM].astype(BF16)
        return _mm(kg, _group_queries(qt_ref, g)) + bias(g)

    def softmax(g, st):
        e = jnp.exp2(st - jnp.max(st, axis=0, keepdims=True))
        p = e / jnp.sum(e, axis=0, keepdims=True) * (bias(g) > 0.5 * NEG).astype(F32)
        return p.astype(BF16)

    def values(g, pb):
        vgt = vt_all[g * HEAD_DIM:(g + 1) * HEAD_DIM].astype(BF16)
        _store_heads(o_ref, _mm(vgt, pb), gate, g, 0)
        return pb

    probs = _staggered(N_KV, scores, softmax, values, ahead=2, lag=1)
    for g in range(N_KV):
        pb = probs[g]
        imp = jnp.zeros((n_sel, QT), F32)
        for hh in range(HPG):
            imp = imp + _mm(ov, pb[:, hh * QT:(hh + 1) * QT])
        score = jnp.where(forced, jnp.inf, jnp.where(valid, imp, -jnp.inf))
        rank = jnp.zeros((n_sel, QT), I32)
        for i in range(n_sel):
            si = score[i:i + 1, :]
            ahead = (si > score) | ((si == score) & (jrow > i))
            rank = rank + ahead.astype(I32)
        sb_ref[pl.ds(g * n_sel, n_sel), :] = jnp.where(rank < top, 0.0, NEG).astype(sb_ref.dtype)


def _cmp_prompt(q_t, kcmp, cmap, gates, ov_t, batch, seq):
    nqt = seq // QT
    n_chunk = kcmp.shape[1]
    n_sel = ov_t.shape[0]
    top = min(SEL_TOP, n_sel)
    return pl.pallas_call(
        functools.partial(_cmp_prompt_kernel, n_sel=n_sel, top=top),
        grid=(batch, nqt),
        in_specs=[
            pl.BlockSpec((D_Q, QT), lambda b, i: (0, b * nqt + i)),
            pl.BlockSpec((1, n_chunk, D_KV), lambda b, i: (b, 0, 0)),
            pl.BlockSpec(cmap.shape, lambda b, i: (0, 0, 0)),
            pl.BlockSpec((QT, LANES), lambda b, i: (b * nqt + i, 0)),
            pl.BlockSpec(ov_t.shape, lambda b, i: (0, 0)),
        ],
        out_specs=[
            pl.BlockSpec((QT, D_Q), lambda b, i: (b * nqt + i, 0)),
            pl.BlockSpec((None, N_KV * n_sel, QT), lambda b, i: (b * nqt + i, 0, 0)),
        ],
        out_shape=[
            jax.ShapeDtypeStruct((batch * seq, D_Q), F32),
            jax.ShapeDtypeStruct((batch * nqt, N_KV * n_sel, QT), BF16),
        ],
        compiler_params=_params(("parallel", "parallel")),
        name="cmp_attn_prompt",
    )(q_t, kcmp, cmap, gates, ov_t)


_TK = 2 * LANES


def _sel_prompt_kernel(qt_ref, kv_ref, sb_ref, map_ref, gate_ref, *rest, n_sel, seq, n_riders):
    rider_in = rest[:n_riders]
    o_ref = rest[n_riders]
    rider_out = rest[n_riders + 1:2 * n_riders + 1]
    ka_ref, vt_ref, qa_ref = rest[2 * n_riders + 1:]
    for src, dst in zip(rider_in, rider_out):
        dst[...] = src[...].astype(dst.dtype)
    qt = pl.program_id(1)
    aug = 2 * HEAD_DIM
    half = D_KV // 2

    @pl.when(qt == 0)
    def _():
        pos = lax.broadcasted_iota(I32, (seq, aug - HEAD_DIM), 0)
        blk = lax.broadcasted_iota(I32, (seq, aug - HEAD_DIM), 1)
        onehot = jnp.where(pos // SEL_LEN == blk, 1.0, 0.0)
        k_rows = kv_ref[0, pl.ds(0, half), :].T
        for g in range(N_KV):
            ka_ref[g] = jnp.concatenate([k_rows[:, g * HEAD_DIM:(g + 1) * HEAD_DIM], onehot], axis=1).astype(BF16)
            vt_ref[g] = kv_ref[0, pl.ds(half + g * HEAD_DIM, HEAD_DIM), :].astype(BF16)

    gate = gate_ref[...]
    sb = sb_ref[...]
    diag = qt // 2
    n_tiles = diag + 1
    pad = jnp.zeros((aug - HEAD_DIM - n_sel, GROUP_ROWS), BF16)
    for g in range(N_KV):
        sbg = sb[g * n_sel:(g + 1) * n_sel]
        qa_ref[g] = jnp.concatenate([_group_queries(qt_ref, g), jnp.concatenate([sbg] * HPG, axis=1), pad], axis=0)

    def tile(kt, carry):
        start = pl.multiple_of(kt * _TK, _TK)
        kind = jnp.where(kt == diag, qt % 2, jnp.where((kt == diag - 1) & (qt % 2 == 0), 2, 3))
        def scores(g):
            return _mm(ka_ref[g, pl.ds(start, _TK), :], qa_ref[g]) + map_ref[g, kind]

        def softmax(g, st):
            m_i, l_i, _ = carry[g]
            m_new = jnp.maximum(m_i, jnp.max(st, axis=0, keepdims=True))
            alpha = jnp.exp2(m_i - m_new)
            p = jnp.exp2(st - m_new)
            return m_new, alpha * l_i + jnp.sum(p, axis=0, keepdims=True), alpha, p.astype(BF16)

        def values(g, sm):
            m_new, l_new, alpha, pb = sm
            return m_new, l_new, alpha * carry[g][2] + _mm(vt_ref[g, :, pl.ds(start, _TK)], pb)

        return tuple(_staggered(N_KV, scores, softmax, values, ahead=2, lag=1))

    init = tuple((jnp.full((1, GROUP_ROWS), -jnp.inf, F32), jnp.zeros((1, GROUP_ROWS), F32),
                  jnp.zeros((HEAD_DIM, GROUP_ROWS), F32)) for _ in range(N_KV))
    final = lax.fori_loop(0, n_tiles, tile, init)
    for g in range(N_KV):
        _, l_f, acc = final[g]
        _store_heads(o_ref, acc / l_f, gate, g, 1)


def _sel_prompt(q_t, kv_t, selbias, smap, gates, batch, seq, riders=()):
    nqt = seq // QT
    n_sel = selbias.shape[1] // N_KV
    ride = _rider_specs(riders, batch * nqt, lambda b, i: (b * nqt + i, 0))
    out = pl.pallas_call(
        functools.partial(_sel_prompt_kernel, n_sel=n_sel, seq=seq, n_riders=len(riders)),
        grid=(batch, nqt),
        in_specs=[
            pl.BlockSpec((D_Q, QT), lambda b, i: (0, b * nqt + i)),
            pl.BlockSpec((1, D_KV, seq), lambda b, i: (b, 0, 0)),
            pl.BlockSpec((None, N_KV * n_sel, QT), lambda b, i: (b * nqt + i, 0, 0)),
            pl.BlockSpec(smap.shape, lambda b, i: (0, 0, 0, 0)),
            pl.BlockSpec((QT, LANES), lambda b, i: (b * nqt + i, 0)),
        ] + ride,
        out_specs=[pl.BlockSpec((QT, D_Q), lambda b, i: (b * nqt + i, 0))] + ride,
        out_shape=[jax.ShapeDtypeStruct((batch * seq, D_Q), F32)]
        + [jax.ShapeDtypeStruct(a.shape, BF16) for a in riders],
        scratch_shapes=[pltpu.VMEM((N_KV, seq, 2 * HEAD_DIM), BF16), pltpu.VMEM((N_KV, HEAD_DIM, seq), BF16),
                        pltpu.VMEM((N_KV, 2 * HEAD_DIM, GROUP_ROWS), BF16)],
        compiler_params=_params(("arbitrary", "arbitrary"), 56),
        name="sel_attn_prompt",
    )(q_t, kv_t, selbias, smap, gates, *riders)
    return out[0], out[1:]


_WIN_TILES = WINDOW // QT + 1


def _win_prompt_kernel(qt_ref, *refs):
    kv_refs = refs[:_WIN_TILES]
    map_ref, gate_ref, o_ref = refs[_WIN_TILES:]
    qt = pl.program_id(1)
    gate = gate_ref[...]
    half = D_KV // 2
    span = _WIN_TILES * QT
    key = lax.broadcasted_iota(I32, (span, GROUP_ROWS), 0)
    in_seq = key >= (_WIN_TILES - 1 - qt) * QT
    k_rows = jnp.concatenate([r[0, pl.ds(0, half), :].T for r in kv_refs], axis=0)
    def scores(g):
        kg = k_rows[:, g * HEAD_DIM:(g + 1) * HEAD_DIM].astype(BF16)
        return jnp.where(in_seq, _mm(kg, _group_queries(qt_ref, g)) + map_ref[g], NEG)

    def softmax(g, st):
        e = jnp.exp2(st - jnp.max(st, axis=0, keepdims=True))
        return (e / jnp.sum(e, axis=0, keepdims=True)).astype(BF16)

    def values(g, pb):
        vgt = jnp.concatenate([r[0, pl.ds(half + g * HEAD_DIM, HEAD_DIM), :] for r in kv_refs],
                              axis=1).astype(BF16)
        _store_heads(o_ref, _mm(vgt, pb), gate, g, 2)

    _staggered(N_KV, scores, softmax, values, ahead=2, lag=1)


def _win_prompt(q_t, kv_t, wmap, gates, batch, seq):
    nqt = seq // QT

    def kv_spec(k):
        return pl.BlockSpec((1, D_KV, QT), lambda b, i: (b, 0, jnp.maximum(i - (_WIN_TILES - 1) + k, 0)))

    return pl.pallas_call(
        _win_prompt_kernel,
        grid=(batch, nqt),
        in_specs=[pl.BlockSpec((D_Q, QT), lambda b, i: (0, b * nqt + i))]
        + [kv_spec(k) for k in range(_WIN_TILES)]
        + [pl.BlockSpec(wmap.shape, lambda b, i: (0, 0, 0)),
           pl.BlockSpec((QT, LANES), lambda b, i: (b * nqt + i, 0))],
        out_specs=pl.BlockSpec((QT, D_Q), lambda b, i: (b * nqt + i, 0)),
        out_shape=jax.ShapeDtypeStruct((batch * seq, D_Q), F32),
        compiler_params=_params(("parallel", "parallel")),
        name="win_attn_prompt",
    )(q_t, *([kv_t] * _WIN_TILES), wmap, gates)


def _group_rows(x_by_group):
    row = lax.broadcasted_iota(I32, x_by_group[0].shape, 0)
    out = x_by_group[0]
    for g in range(1, N_KV):
        out = jnp.where(row // HPG == g, x_by_group[g], out)
    return out


_CMP_SAMPLE_ROWS = 4


def _cmp_sample_kernel(q_ref, kc_ref, bias_ref, ov_ref, gate_ref, o_ref, idx_ref, *, n_sel, past):
    bias = bias_ref[...]
    half = D_KV // 2
    imps = []
    for bl in range(_CMP_SAMPLE_ROWS):
        q = q_ref[bl]
        kv = kc_ref[bl].astype(BF16)
        s = _group_rows([_mm_nt(q, kv[:, g * HEAD_DIM:(g + 1) * HEAD_DIM]) for g in range(N_KV)]) + bias
        e = jnp.exp(s - jnp.max(s, axis=-1, keepdims=True))
        p = e / jnp.sum(e, axis=-1, keepdims=True) * (bias > 0.5 * NEG).astype(F32)
        pb = p.astype(BF16)
        o = _group_rows([_mm(pb, kv[:, half + g * HEAD_DIM:half + (g + 1) * HEAD_DIM]) for g in range(N_KV)])
        o_ref[bl] = o * gate_ref[bl]
        imp_h = _mm(pb, ov_ref[...])
        imps += [imp_h[g * HPG:g * HPG + 1] + imp_h[g * HPG + 1:g * HPG + 2] + imp_h[g * HPG + 2:g * HPG + 3]
                 + imp_h[g * HPG + 3:g * HPG + 4] for g in range(N_KV)]
    imp = jnp.concatenate(imps, axis=0)
    rows, nsp = imp.shape
    j = lax.broadcasted_iota(I32, (rows, nsp), 1)
    cur = past // SEL_LEN
    forced = (j == 0) | (j == cur) | (j == cur - 1)
    valid = (j * SEL_LEN <= past) & (j < n_sel)
    score = jnp.where(valid, jnp.where(forced, jnp.inf, imp), -jnp.inf)
    slot = lax.broadcasted_iota(I32, (rows, LANES), 1)
    res = jnp.zeros((rows, LANES), I32)
    for k in range(min(SEL_TOP, n_sel)):
        best = jnp.max(score, axis=-1, keepdims=True)
        pick = jnp.min(jnp.where(score == best, j, nsp), axis=-1, keepdims=True)
        res = jnp.where(slot == k, pick, res)
        score = jnp.where(j == pick, -jnp.inf, score)
    idx_ref[...] = res


def _cmp_sample(q3, kcmp, bias, ov, gate3, n_sel, past):
    bs, n_chunk, _ = kcmp.shape
    nb = _CMP_SAMPLE_ROWS
    heads = pl.BlockSpec((nb, N_HEADS, HEAD_DIM), lambda b: (b, 0, 0))
    return pl.pallas_call(
        functools.partial(_cmp_sample_kernel, n_sel=n_sel, past=past),
        grid=(bs // nb,),
        in_specs=[
            heads,
            pl.BlockSpec((nb, n_chunk, D_KV), lambda b: (b, 0, 0)),
            pl.BlockSpec(bias.shape, lambda b: (0, 0)),
            pl.BlockSpec(ov.shape, lambda b: (0, 0)),
            heads,
        ],
        out_specs=[heads, pl.BlockSpec((nb * N_KV, LANES), lambda b: (b, 0))],
        out_shape=[
            jax.ShapeDtypeStruct((bs, N_HEADS, HEAD_DIM), F32),
            jax.ShapeDtypeStruct((bs * N_KV, LANES), I32),
        ],
        compiler_params=_params(("parallel",)),
        name="cmp_attn_sample",
    )(q3, kcmp, bias, ov, gate3)


def _sel_sample_kernel(blk_ref, phys_ref, q_ref, new_ref, bias_ref, gate_ref, pool_ref, o_ref, kv_buf, sem, *,
                       top, n_pages):
    b = pl.program_id(0)
    g = pl.program_id(1)
    step = b * N_KV + g
    n_steps = pl.num_programs(0) * N_KV
    slot = step % 2

    def page_copy(page, grp, s, k):
        return pltpu.make_async_copy(pool_ref.at[page, pl.ds(0, 2), grp], kv_buf.at[s, k], sem.at[s])

    def fetch(st, s):
        for k in range(top):
            page_copy(phys_ref[st * top + k], st % N_KV, s, k).start()

    @pl.when(step == 0)
    def _():
        fetch(step, 0)

    @pl.when(step + 1 < n_steps)
    def _():
        fetch(step + 1, 1 - slot)

    for k in range(top):
        page_copy(0, 0, slot, k).wait()
    k_refs = [kv_buf.at[slot, k, 0] for k in range(top)]
    v_refs = [kv_buf.at[slot, k, 1] for k in range(top)]
    q = q_ref[0, 0]
    lane = lax.broadcasted_iota(I32, (8, PAGE), 1)
    scores = []
    for k in range(top):
        blk = blk_ref[(b * N_KV + g) * top + k]
        page = jnp.minimum(blk // 2, n_pages)
        s = _mm(q, k_refs[k][...].astype(BF16)) + bias_ref[page]
        ok = (lane // SEL_LEN == blk % 2) & (blk < 2 * n_pages)
        scores.append(jnp.where(ok, s, NEG))
    k_new = new_ref[0, 0, 0:1, :].astype(BF16).astype(F32)
    v_new = new_ref[0, 0, 1:2, :].astype(BF16).astype(F32)
    s_new = jnp.sum(q.astype(F32) * k_new, axis=-1, keepdims=True) + bias_ref[n_pages]
    scores.append(s_new)
    s_all = jnp.concatenate(scores, axis=1)
    e = jnp.exp(s_all - jnp.max(s_all, axis=-1, keepdims=True))
    p = e / jnp.sum(e, axis=-1, keepdims=True)
    o = p[:, top * PAGE:top * PAGE + 1].astype(BF16).astype(F32) * v_new
    for k in range(top):
        o = o + _mm_nt(p[:, k * PAGE:(k + 1) * PAGE].astype(BF16), v_refs[k][...].astype(BF16))
    o_ref[0, 0] = o * gate_ref[0, 0]


def _sel_sample(blk, phys, q4, new4, bias4, gate4, pool5, top, n_pages):
    bs = q4.shape[0]
    grp = lambda b, g, blk_r, phys_r: (b, g, 0, 0)
    grid_spec = pltpu.PrefetchScalarGridSpec(
        num_scalar_prefetch=2,
        grid=(bs, N_KV),
        in_specs=[
            pl.BlockSpec((1, 1, 8, HEAD_DIM), grp),
            pl.BlockSpec((1, 1, 8, HEAD_DIM), grp),
            pl.BlockSpec((None, n_pages + 1, 8, LANES), lambda b, g, blk_r, phys_r: (g, 0, 0, 0)),
            pl.BlockSpec((1, 1, 8, HEAD_DIM), grp),
            pl.BlockSpec(memory_space=pl.ANY),
        ],
        out_specs=pl.BlockSpec((1, 1, 8, HEAD_DIM), grp),
        scratch_shapes=[pltpu.VMEM((2, top, 2, HEAD_DIM, PAGE), F32), pltpu.SemaphoreType.DMA((2,))],
    )
    return pl.pallas_call(
        functools.partial(_sel_sample_kernel, top=top, n_pages=n_pages),
        grid_spec=grid_spec,
        out_shape=jax.ShapeDtypeStruct((bs, N_KV, 8, HEAD_DIM), F32),
        compiler_params=_params(("arbitrary", "arbitrary")),
        name="sel_attn_sample",
    )(blk, phys, q4, new4, bias4, gate4, pool5)


def _win_sample_kernel(q_ref, new_ref, st_ref, col_ref, bias_ref, gate_ref, o_ref, nst_ref):
    st = st_ref[0]
    half = D_KV // 2
    for g in range(N_KV):
        q = q_ref[0, g]
        bias = bias_ref[g]
        s_old = _mm(q, st[g * HEAD_DIM:(g + 1) * HEAD_DIM, :].astype(BF16)) + bias[:, 0:WINDOW]
        k_new = new_ref[0, g, 0:1, :].astype(BF16).astype(F32)
        v_new = new_ref[0, g, 1:2, :].astype(BF16).astype(F32)
        s_new = jnp.sum(q.astype(F32) * k_new, axis=-1, keepdims=True) + bias[:, WINDOW:WINDOW + LANES]
        s_all = jnp.concatenate([s_old, s_new], axis=1)
        e = jnp.exp(s_all - jnp.max(s_all, axis=-1, keepdims=True))
        p = e / jnp.sum(e, axis=-1, keepdims=True)
        o = _mm_nt(p[:, 0:WINDOW].astype(BF16), st[half + g * HEAD_DIM:half + (g + 1) * HEAD_DIM, :].astype(BF16))
        o = o + p[:, WINDOW:WINDOW + 1].astype(BF16).astype(F32) * v_new
        o_ref[0, g] = o * gate_ref[0, g]
    rolled = pltpu.roll(st, WINDOW - 1, axis=1)
    lane = lax.broadcasted_iota(I32, (D_KV, LANES), 1)
    nst_ref[0, :, pl.ds(0, WINDOW - LANES)] = rolled[:, 0:WINDOW - LANES]
    nst_ref[0, :, pl.ds(WINDOW - LANES, LANES)] = jnp.where(
        lane == LANES - 1, col_ref[0], rolled[:, WINDOW - LANES:WINDOW])


def _win_sample(q4, new4, state_t, newcol, bias4, gate4):
    bs = q4.shape[0]
    grp = pl.BlockSpec((1, N_KV, 8, HEAD_DIM), lambda b: (b, 0, 0, 0))
    return pl.pallas_call(
        _win_sample_kernel,
        grid=(bs,),
        in_specs=[
            grp, grp,
            pl.BlockSpec((1, D_KV, WINDOW), lambda b: (b, 0, 0)),
            pl.BlockSpec((1, D_KV, LANES), lambda b: (b, 0, 0)),
            pl.BlockSpec(bias4.shape, lambda b: (0, 0, 0)),
            grp,
        ],
        out_specs=[grp, pl.BlockSpec((1, D_KV, WINDOW), lambda b: (b, 0, 0))],
        out_shape=[
            jax.ShapeDtypeStruct((bs, N_KV, 8, HEAD_DIM), F32),
            jax.ShapeDtypeStruct((bs, D_KV, WINDOW), F32),
        ],
        compiler_params=_params(("parallel",)),
        name="win_attn_sample",
    )(q4, new4, state_t, newcol, bias4, gate4)


_MOE_ROWS = 256
_MOE_SLACK = 2 * _MOE_ROWS


def _row_copy(src, src_row, dst, dst_row, sem):
    return pltpu.make_async_copy(src.at[pl.ds(src_row, 1)], dst.at[pl.ds(dst_row, 1)], sem)


def _expert_kernel(be_ref, nu_ref, dst_ref, src_ref, nxt_ref, h_ref, w_ref, wg_ref, wu_ref, wd_ref, yk_ref,
                   xbuf, ybuf, gsem, ssem, *, n_rows, n_live):
    del be_ref
    i = pl.program_id(0)
    slot = i % 2
    n_used = nu_ref[0]

    def gather(rows_ref, dst_slot):
        for r in range(_MOE_ROWS):
            _row_copy(h_ref, rows_ref[0, r], xbuf.at[dst_slot], r, gsem.at[dst_slot]).start()

    def wait_gather(s):
        for r in range(_MOE_ROWS):
            _row_copy(h_ref, 0, xbuf.at[s], 0, gsem.at[s]).wait()

    def scatter(s):
        for r in range(_MOE_ROWS):
            _row_copy(ybuf.at[s], r, yk_ref, dst_ref[0, r], ssem.at[s]).start()

    def wait_scatter(s):
        for r in range(_MOE_ROWS):
            _row_copy(ybuf.at[s], 0, yk_ref, 0, ssem.at[s]).wait()

    @pl.when(i == 0)
    def _():
        gather(src_ref, 0)
        ybuf[1] = jnp.zeros((_MOE_ROWS, ybuf.shape[2]), F32)
        fills = [
            pltpu.make_async_copy(ybuf.at[1, pl.ds(0, min(_MOE_ROWS, n_rows - off))],
                                  yk_ref.at[pl.ds(k * n_rows + off, min(_MOE_ROWS, n_rows - off))], ssem.at[1])
            for k in range(TOP_K) for off in range(n_live, n_rows, _MOE_ROWS)]
        for cp in fills:
            cp.start()
        for cp in fills:
            cp.wait()

    @pl.when(i + 1 < n_used)
    def _():
        gather(nxt_ref, 1 - slot)

    @pl.when(i < n_used)
    def _():
        wait_gather(slot)

        @pl.when(i >= 2)
        def _():
            wait_scatter(slot)

        x = xbuf[slot].astype(BF16)
        a = _silu(_mm(x, wg_ref[...])) * _mm(x, wu_ref[...])
        ybuf[slot] = _mm(a.astype(BF16), wd_ref[...]) * w_ref[...]
        scatter(slot)

        @pl.when(i == n_used - 1)
        def _():
            @pl.when(i >= 1)
            def _():
                wait_scatter(1 - slot)

            wait_scatter(slot)


def _experts(blk_e, n_used, rows_src, rows_dst, rows_w, h2_all, n_live, w_gate, w_up, w_down):
    n_rows, d = h2_all.shape
    n_blk = blk_e.shape[0]
    src3 = rows_src.reshape(n_blk, 1, _MOE_ROWS)
    dst3 = rows_dst.reshape(n_blk, 1, _MOE_ROWS)
    wspec = lambda a: pl.BlockSpec((None,) + a.shape[1:], lambda i, be, nu: (be[i], 0, 0))
    smem_blk = lambda fn: pl.BlockSpec((None, 1, _MOE_ROWS), fn, memory_space=pltpu.SMEM)
    grid_spec = pltpu.PrefetchScalarGridSpec(
        num_scalar_prefetch=2,
        grid=(n_blk,),
        in_specs=[
            smem_blk(lambda i, be, nu: (i, 0, 0)),
            smem_blk(lambda i, be, nu: (i, 0, 0)),
            smem_blk(lambda i, be, nu: (jnp.minimum(i + 1, n_blk - 1), 0, 0)),
            pl.BlockSpec(memory_space=pl.ANY),
            pl.BlockSpec((_MOE_ROWS, 1), lambda i, be, nu: (i, 0)),
            wspec(w_gate), wspec(w_up), wspec(w_down),
        ],
        out_specs=pl.BlockSpec(memory_space=pl.ANY),
        scratch_shapes=[
            pltpu.VMEM((2, _MOE_ROWS, d), F32),
            pltpu.VMEM((2, _MOE_ROWS, d), F32),
            pltpu.SemaphoreType.DMA((2,)),
            pltpu.SemaphoreType.DMA((2,)),
        ],
    )
    return pl.pallas_call(
        functools.partial(_expert_kernel, n_rows=n_rows, n_live=n_live),
        grid_spec=grid_spec,
        out_shape=jax.ShapeDtypeStruct((TOP_K * n_rows, d), F32),
        compiler_params=pltpu.CompilerParams(dimension_semantics=("arbitrary",), vmem_limit_bytes=56 * MIB,
                                             disable_bounds_checks=True),
        name="moe_experts",
    )(blk_e, n_used, dst3, src3, src3, h2_all, rows_w, w_gate, w_up, w_down)


def _combine_kernel(x_ref, y0_ref, y1_ref, g_ref, o_ref):
    x = x_ref[...] + (y0_ref[...] + y1_ref[...])
    ms = jnp.mean(x * x, axis=-1, keepdims=True)
    o_ref[...] = x * lax.rsqrt(ms + EPS) * g_ref[...]


def _combine(x1, yk, gfin, row0, tm):
    n, d = x1.shape
    yk3 = yk.reshape(TOP_K, -1, d)
    blk0 = row0 // tm
    return pl.pallas_call(
        _combine_kernel,
        grid=(n // tm,),
        in_specs=[
            pl.BlockSpec((tm, d), lambda i: (i, 0)),
            pl.BlockSpec((None, tm, d), lambda i: (0, blk0 + i, 0)),
            pl.BlockSpec((None, tm, d), lambda i: (1, blk0 + i, 0)),
            pl.BlockSpec((1, d), lambda i: (0, 0)),
        ],
        out_specs=pl.BlockSpec((tm, d), lambda i: (i, 0)),
        out_shape=jax.ShapeDtypeStruct((n, d), F32),
        compiler_params=_params(("parallel",)),
        name="moe_combine",
    )(x1, yk3, yk3, gfin)


def _dispatch(eid, w, counts, n_rows):
    a = eid.shape[0]
    tb = _MOE_ROWS
    order = jnp.argsort(eid).astype(I32)
    padded = (counts + tb - 1) // tb * tb
    seg_end = jnp.cumsum(padded)
    seg_start = seg_end - padded
    start = jnp.cumsum(counts) - counts
    n_blk = (a + N_EXPERTS * (tb - 1) + tb - 1) // tb
    blk_e = jnp.minimum(jnp.sum((seg_end[None, :] <= (jnp.arange(n_blk) * tb)[:, None]).astype(I32), axis=1),
                        N_EXPERTS - 1)
    blk = jnp.arange(n_blk)[:, None]
    r = jnp.arange(tb)[None, :]
    off = (blk * tb - seg_start[blk_e][:, None]) + r
    valid = off < counts[blk_e][:, None]
    asg = order[jnp.clip(start[blk_e][:, None] + off, 0, a - 1)]
    rows_src = jnp.where(valid, asg // TOP_K, 0).astype(I32)
    rows_dst = jnp.where(valid, (asg % TOP_K) * n_rows + asg // TOP_K, (blk % 2) * n_rows + (n_rows - tb) + r)
    rows_w = jnp.where(valid, w[asg], 0.0)
    n_used = (seg_end[-1] // tb).astype(I32).reshape(1)
    return rows_src, rows_dst.astype(I32), rows_w.reshape(-1, 1), blk_e.astype(I32), n_used


def _overlap(n_chunk, n_sel, n_sel_pad):
    cs = jnp.arange(n_chunk)[:, None] * CMP_STRIDE
    ss = jnp.arange(n_sel_pad)[None, :] * SEL_LEN
    hit = (cs < ss + SEL_LEN) & (cs + CMP_LEN > ss) & (jnp.arange(n_sel_pad)[None, :] < n_sel)
    return hit.astype(BF16)


def kernel(x_prompt, x_sample, cache_cmp_kv, cache_sel_kv, state_win_kv, state_conv, page_table, rel_bias,
           norm_mix, w_in, b_in, conv_w, conv_b, conv_ln_g, conv_ln_b, w_conv_out, b_conv_out,
           cmp_pe_k, cmp_w1_k, cmp_w2_k, cmp_pe_v, cmp_w1_v, cmp_w2_v, w_nsa_out, w_out, norm_ffn,
           w_rg, b_rg, w_re, b_re, w_gate, w_up, w_down, norm_final):
    batch, seq, _ = x_prompt.shape
    bs = x_sample.shape[0]
    n_pool = cache_cmp_kv.shape[1]
    n_pages = page_table.shape[1]
    past = n_pages * PAGE
    n_tok = batch * seq
    assert bs + _MOE_ROWS <= _MOE_SLACK and n_tok % bs == 0 and seq % _TK == 0 and bs % 16 == 0
    l = 0

    wt = w_in[l].T.astype(BF16)
    bias_in = b_in[l]
    cuts = [0, D_CONV, 2 * D_CONV, 2 * D_CONV + D_Q, 2 * D_CONV + D_Q + 3 * D_KV]
    cuts += [cuts[-1] + 3 * N_HEADS, cuts[-1] + 3 * N_HEADS + D_MODEL, cuts[-1] + 3 * N_HEADS + 2 * D_MODEL]
    seg = lambda k: (wt[cuts[k]:cuts[k + 1]], bias_in[cuts[k]:cuts[k + 1]][None, :])
    (wa, ba), (wb, bb), (wq, bq), (wkv, bkv), (wzg, bzg), (wga, bga), (wgb, bgb) = [seg(k) for k in range(7)]
    wzg = jnp.pad(wzg, ((0, LANES - 3 * N_HEADS), (0, 0)))
    bzg = jnp.pad(bzg, ((0, 0), (0, LANES - 3 * N_HEADS)))
    wc = w_conv_out[l].astype(BF16)
    wn = w_nsa_out[l].astype(BF16)
    wo = w_out[l].astype(BF16)
    wr = jnp.zeros((LANES, D_MODEL), F32).at[0:N_GROUPS].set(w_rg[l].T).at[8:8 + N_EXPERTS].set(w_re[l].T)
    br = jnp.zeros((LANES,), F32).at[0:N_GROUPS].set(b_rg[l]).at[8:8 + N_EXPERTS].set(b_re[l].reshape(-1))
    wr, br = wr.astype(BF16), br[:, None]
    cw = _compress_weights(cmp_pe_k[l], cmp_w1_k[l], cmp_w2_k[l], cmp_pe_v[l], cmp_w1_v[l], cmp_w2_v[l])
    rb_t = rel_bias.T
    gmix, gffn, gfin = norm_mix[l][None, :], norm_ffn[l][None, :], norm_final[None, :]
    vec = lambda a: a[l][None, :]

    q_scale = HEAD_DIM ** -0.5
    xp = x_prompt.reshape(n_tok, D_MODEL)
    xs = x_sample.reshape(bs, D_MODEL)
    flat = lambda w: w[l].reshape(-1, w.shape[-1])
    (hp, up, qp, gates_p, *kv_p), (wg_bf,) = _project_prompt(
        xp, gmix, wa, wb, wq, wkv, wzg, ba, bb, bq.reshape(-1, 1), bkv.reshape(-1, 1), bzg,
        q_scale * LOG2E, batch, seq, 256, riders=(flat(w_gate),))
    hs = _rmsnorm(xs, gmix, bs)
    us = _glu(hs, wa, wb, ba, bb, bs, 512)
    qs = _proj(hs, wq, bq, lambda z: z * q_scale, BF16, bs, 512, "proj_q")
    gates_s = _proj(hs, wzg, bzg, _sigmoid, F32, bs, LANES, "proj_head_gates")
    kv_s = _proj(hs, wkv, bkv, lambda z: z, F32, bs, 512, "proj_kv_sample")

    conv_args = (conv_w[l], vec(conv_b), vec(conv_ln_g), vec(conv_ln_b))
    cp = _conv_prompt(up, *conv_args, batch, seq, 256)
    cs, new_conv_s = _conv_sample(state_conv[l].transpose(1, 0, 2), us, *conv_args)

    nqt = seq // QT
    n_chunk_p = seq // CMP_STRIDE
    n_chunk_s = n_pages * _CHUNKS_PER_PAGE
    cmap, wmap, smap = _bias_prompt(rb_t, nqt, n_chunk_p)
    cbias_s, wbias_s, sbias_s = _bias_sample(rb_t, past, n_chunk_s, n_pages)

    ident = jnp.zeros((1,), I32)
    kcmp_p, _ = _compress(kv_p[0], ident, cw, batch, seq // PAGE, seq // PAGE, False)
    feat_major = lambda a: a.transpose(0, 2, 3, 4, 1)
    pool_cmp = feat_major(cache_cmp_kv[l]).reshape(n_pool, D_KV, PAGE)
    pp_s = min(32, n_pages)
    kcmp_s, (wd_bf,) = _compress(pool_cmp, page_table.reshape(-1), cw, bs, n_pages, pp_s, True,
                                 riders=(flat(w_down),))

    n_sel_p = seq // SEL_LEN
    o_cmp_p, selbias = _cmp_prompt(qp, kcmp_p, cmap, gates_p, _overlap(n_chunk_p, n_sel_p, n_sel_p).T,
                                   batch, seq)
    n_sel_s = -(-(past + 1) // SEL_LEN)
    n_sel_pad = -(-n_sel_s // LANES) * LANES
    top_s = min(SEL_TOP, n_sel_s)
    head_gate = lambda k: jnp.broadcast_to(
        gates_s[:, k * N_HEADS:(k + 1) * N_HEADS, None], (bs, N_HEADS, HEAD_DIM))
    qs3 = qs.reshape(bs, N_HEADS, HEAD_DIM)
    o_cmp_s, idx_s = _cmp_sample(qs3, kcmp_s, cbias_s, _overlap(n_chunk_s, n_sel_s, n_sel_pad), head_gate(0),
                                 n_sel_s, past)

    o_sel_p, (wu_bf,) = _sel_prompt(qp, kv_p[1], selbias, smap, gates_p, batch, seq, riders=(flat(w_up),))
    by_group = lambda a: jnp.pad(a.reshape(bs, N_KV, HPG, -1), ((0, 0), (0, 0), (0, 8 - HPG), (0, 0)))
    q4 = by_group(qs3)
    blk = idx_s[:, :top_s].reshape(-1)
    phys = jnp.take_along_axis(
        page_table, jnp.minimum(blk // 2, n_pages - 1).reshape(bs, -1), axis=1).reshape(-1)
    kv_new = lambda k: jnp.pad(
        kv_s[:, k * D_KV:(k + 1) * D_KV].reshape(bs, 2, N_KV, HEAD_DIM).transpose(0, 2, 1, 3),
        ((0, 0), (0, 0), (0, 6), (0, 0)))
    bias_by_group = lambda a: jnp.pad(
        jnp.moveaxis(a, -2, 0).reshape((N_KV, HPG) + a.shape[:-2] + a.shape[-1:]),
        ((0, 0), (0, 8 - HPG)) + ((0, 0),) * (a.ndim - 1))
    sb4 = jnp.moveaxis(bias_by_group(sbias_s), 1, 2)
    pool_sel = feat_major(cache_sel_kv[l])
    o_sel_s = _sel_sample(blk, phys, q4, kv_new(1), sb4, by_group(head_gate(1)), pool_sel, top_s, n_pages)

    o_win_p = _win_prompt(qp, kv_p[2], wmap, gates_p, batch, seq)
    win_t = feat_major(state_win_kv[l]).reshape(bs, D_KV, WINDOW)
    newcol = jnp.broadcast_to(kv_s[:, 2 * D_KV:3 * D_KV, None], (bs, D_KV, LANES))
    o_win_s, new_win_s = _win_sample(q4, kv_new(2), win_t, newcol, bias_by_group(wbias_s), by_group(head_gate(2)))

    m_p, _ = _merge_a(hp, cp, o_cmp_p, o_sel_p, o_win_p, wga, wgb, wc, wn, bga, bgb, vec(b_conv_out), 512, 512)
    (x1_p, h2_all, eid_p, rw_p, cnt_p), _ = _merge_b(xp, m_p, wo, gffn, wr, br, 256, slack=_MOE_SLACK)
    from_groups = lambda a: a[:, :, :HPG, :].reshape(bs, D_Q)
    m_s, _ = _merge_a(hs, cs, o_cmp_s.reshape(bs, D_Q), from_groups(o_sel_s), from_groups(o_win_s),
                      wga, wgb, wc, wn, bga, bgb, vec(b_conv_out), bs, 512)
    (x1_s, h2_all, eid_s, rw_s, cnt_s), _ = _merge_b(xs, m_s, wo, gffn, wr, br, bs, h2_all=h2_all, row0=n_tok)

    eid = jnp.concatenate([eid_p[:TOP_K].T, eid_s[:TOP_K].T], axis=0).reshape(-1)
    rw = jnp.concatenate([rw_p[:TOP_K].T, rw_s[:TOP_K].T], axis=0).reshape(-1)
    counts = jnp.sum(cnt_p[:, :, 0], axis=0) + jnp.sum(cnt_s[:, :, 0], axis=0)
    rows_src, rows_dst, rows_w, blk_e, n_used = _dispatch(eid, rw, counts, h2_all.shape[0])
    yk = _experts(blk_e, n_used, rows_src, rows_dst, rows_w, h2_all, n_tok + bs,
                  wg_bf.reshape(w_gate.shape[1:]), wu_bf.reshape(w_up.shape[1:]), wd_bf.reshape(w_down.shape[1:]))
    y_p = _combine(x1_p, yk, gfin, 0, 256)
    y_s = _combine(x1_s, yk, gfin, n_tok, bs)

    def cache_rows(a_t, rows):
        return a_t.reshape(a_t.shape[0], 2, N_KV, HEAD_DIM, rows).transpose(0, 4, 1, 2, 3)[None]

    new_row = lambda k: kv_s[:, k * D_KV:(k + 1) * D_KV].reshape(1, bs, 1, 2, N_KV, HEAD_DIM)
    hist = CONV_W - 1
    return (
        y_p.reshape(batch, seq, D_MODEL),
        y_s.reshape(bs, 1, D_MODEL),
        cache_rows(kv_p[0], seq),
        new_row(0),
        cache_rows(kv_p[1], seq),
        new_row(1),
        cache_rows(kv_p[2][:, :, seq - WINDOW:], WINDOW),
        cache_rows(new_win_s, WINDOW),
        up.reshape(batch, seq, D_CONV)[None, :, seq - hist:],
        new_conv_s.transpose(1, 0, 2)[None],
    )
```

```python
import functools
import math

import jax
import jax.numpy as jnp
from jax import lax
from jax.experimental import pallas as pl
from jax.experimental.pallas import tpu as pltpu

F32 = jnp.float32
BF16 = jnp.bfloat16
I32 = jnp.int32

D_MODEL = 2048
D_CONV = 1024
CONV_W = 31
N_HEADS = 16
HEAD_DIM = 64
N_KV = 4
HPG = N_HEADS // N_KV
D_Q = N_HEADS * HEAD_DIM
D_KV = 2 * N_KV * HEAD_DIM
CMP_LEN = 32
CMP_STRIDE = 16
CMP_HID = 64
SEL_LEN = 64
SEL_TOP = 16
WINDOW = 512
N_BUCKETS = 32
MAX_DIST = 128
N_GROUPS = 4
EXP_PER_GROUP = 8
N_EXPERTS = N_GROUPS * EXP_PER_GROUP
TOP_K = 2
D_EXPERT = 1024
EPS = 1e-6
NEG = -1e30
PAGE = 128
LANES = 128
QT = 128
GROUP_ROWS = HPG * QT
MIB = 1024 * 1024
LOG2E = math.log2(math.e)


def _params(sem, vmem_mib=48):
    return pltpu.CompilerParams(dimension_semantics=sem, vmem_limit_bytes=vmem_mib * MIB)


def _mm_nt(a, b):
    return lax.dot_general(a, b, (((1,), (1,)), ((), ())), preferred_element_type=F32)


def _mm(a, b):
    return jnp.dot(a, b, preferred_element_type=F32)


def _sigmoid(z):
    return 1.0 / (1.0 + jnp.exp(-z))


def _silu(z):
    return z * _sigmoid(z)


def _gelu_tanh(z):
    return 0.5 * z * (1.0 + jnp.tanh(math.sqrt(2.0 / math.pi) * (z + 0.044715 * (z * z * z))))


def _rmsnorm_kernel(x_ref, g_ref, o_ref):
    x = x_ref[...]
    ms = jnp.mean(x * x, axis=-1, keepdims=True)
    o_ref[...] = (x * lax.rsqrt(ms + EPS) * g_ref[...]).astype(o_ref.dtype)


def _rmsnorm(x, g, tm):
    n, d = x.shape
    return pl.pallas_call(
        _rmsnorm_kernel,
        grid=(n // tm,),
        in_specs=[pl.BlockSpec((tm, d), lambda i: (i, 0)), pl.BlockSpec((1, d), lambda i: (0, 0))],
        out_specs=pl.BlockSpec((tm, d), lambda i: (i, 0)),
        out_shape=jax.ShapeDtypeStruct((n, d), BF16),
        compiler_params=_params(("parallel",)),
        name="rmsnorm",
    )(x, g)


def _proj_kernel(h_ref, w_ref, b_ref, o_ref, *, act):
    z = _mm_nt(h_ref[...], w_ref[...]) + b_ref[...]
    o_ref[...] = act(z).astype(o_ref.dtype)


def _proj(h, wt, b, act, out_dtype, tm, tn, name):
    n, k = h.shape
    nout = wt.shape[0]
    return pl.pallas_call(
        functools.partial(_proj_kernel, act=act),
        grid=(n // tm, nout // tn),
        in_specs=[
            pl.BlockSpec((tm, k), lambda i, j: (i, 0)),
            pl.BlockSpec((tn, k), lambda i, j: (j, 0)),
            pl.BlockSpec((1, tn), lambda i, j: (0, j)),
        ],
        out_specs=pl.BlockSpec((tm, tn), lambda i, j: (i, j)),
        out_shape=jax.ShapeDtypeStruct((n, nout), out_dtype),
        compiler_params=_params(("parallel", "parallel")),
        name=name,
    )(h, wt, b)


def _glu_kernel(h_ref, wa_ref, wb_ref, ba_ref, bb_ref, o_ref):
    h = h_ref[...]
    a = _mm_nt(h, wa_ref[...]) + ba_ref[...]
    b = _mm_nt(h, wb_ref[...]) + bb_ref[...]
    o_ref[...] = a * _sigmoid(b)


def _glu(h, wat, wbt, ba, bb, tm, tn):
    n, k = h.shape
    nout = wat.shape[0]
    wspec = pl.BlockSpec((tn, k), lambda i, j: (j, 0))
    bspec = pl.BlockSpec((1, tn), lambda i, j: (0, j))
    return pl.pallas_call(
        _glu_kernel,
        grid=(n // tm, nout // tn),
        in_specs=[pl.BlockSpec((tm, k), lambda i, j: (i, 0)), wspec, wspec, bspec, bspec],
        out_specs=pl.BlockSpec((tm, tn), lambda i, j: (i, j)),
        out_shape=jax.ShapeDtypeStruct((n, nout), F32),
        compiler_params=_params(("parallel", "parallel")),
        name="glu",
    )(h, wat, wbt, ba, bb)


def _project_prompt_kernel(x_ref, g_ref, wa_ref, wb_ref, wq_ref, wkv_ref, wzg_ref, ba_ref, bb_ref, bq_ref,
                           bkv_ref, bzg_ref, *rest, q_scale, n_riders, kinds):
    rider_in = rest[:n_riders]
    h_ref, u_ref, qt_ref, gates_ref = rest[n_riders:n_riders + 4]
    kv_refs = rest[n_riders + 4:n_riders + 4 + kinds]
    rider_out = rest[n_riders + 4 + kinds:]
    for src, dst in zip(rider_in, rider_out):
        dst[...] = src[...].astype(dst.dtype)
    x = x_ref[...]
    ms = jnp.mean(x * x, axis=-1, keepdims=True)
    h = (x * lax.rsqrt(ms + EPS) * g_ref[...]).astype(BF16)
    h_ref[...] = h
    a = _mm_nt(h, wa_ref[...]) + ba_ref[...]
    b = _mm_nt(h, wb_ref[...]) + bb_ref[...]
    u_ref[...] = a * _sigmoid(b)
    qt_ref[...] = ((_mm_nt(wq_ref[...], h) + bq_ref[...]) * q_scale).astype(qt_ref.dtype)
    gates_ref[...] = _sigmoid(_mm_nt(h, wzg_ref[...]) + bzg_ref[...])
    for kind, o_ref in enumerate(kv_refs):
        rows = pl.ds(kind * D_KV, D_KV)
        o_ref[0] = _mm_nt(wkv_ref[rows, :], h) + bkv_ref[rows, :]


def _project_prompt(x, g, wa, wb, wq, wkv, wzg, ba, bb, bq_col, bkv_col, bzg, q_scale, batch, seq, tm,
                    riders=()):
    n, d = x.shape
    kinds = wkv.shape[0] // D_KV
    nt = seq // tm
    tok = lambda width: pl.BlockSpec((tm, width), lambda i: (i, 0))
    const = lambda a: pl.BlockSpec(a.shape, lambda i: (0,) * a.ndim, pipeline_mode=pl.Buffered(1))
    consts = (g, wa, wb, wq, wkv, wzg, ba, bb, bq_col, bkv_col, bzg)
    ride = _rider_specs(riders, n // tm, lambda i: (i, 0))
    out = pl.pallas_call(
        functools.partial(_project_prompt_kernel, q_scale=q_scale, n_riders=len(riders), kinds=kinds),
        grid=(n // tm,),
        in_specs=[tok(d)] + [const(a) for a in consts] + ride,
        out_specs=[tok(d), tok(D_CONV), pl.BlockSpec((D_Q, tm), lambda i: (0, i)), tok(LANES)]
        + [pl.BlockSpec((1, D_KV, tm), lambda i: (i // nt, 0, i % nt))] * kinds + ride,
        out_shape=[
            jax.ShapeDtypeStruct((n, d), BF16),
            jax.ShapeDtypeStruct((n, D_CONV), F32),
            jax.ShapeDtypeStruct((D_Q, n), BF16),
            jax.ShapeDtypeStruct((n, LANES), F32),
        ] + [jax.ShapeDtypeStruct((batch, D_KV, seq), F32)] * kinds
        + [jax.ShapeDtypeStruct(a.shape, BF16) for a in riders],
        compiler_params=_params(("parallel",), 56),
        name="project_prompt",
    )(x, *consts, *riders)
    return out[:4 + kinds], out[4 + kinds:]


def _merge_a_kernel(h_ref, c_ref, o1_ref, o2_ref, o3_ref, wga_ref, wgb_ref, wc_ref, wn_ref,
                    bga_ref, bgb_ref, bc_ref, *rest):
    n_cast = (len(rest) - 1) // 2
    m_ref = rest[n_cast]
    h = h_ref[...]
    ga = _sigmoid(_mm_nt(h, wga_ref[...]) + bga_ref[...])
    gb = _sigmoid(_mm_nt(h, wgb_ref[...]) + bgb_ref[...])
    yc = _mm(c_ref[...], wc_ref[...]) + bc_ref[...]
    o = (o1_ref[...] + o2_ref[...] + o3_ref[...]).astype(BF16)
    yn = _mm(o, wn_ref[...])
    m_ref[...] = (ga * yc + gb * yn).astype(m_ref.dtype)
    for src, dst in zip(rest[:n_cast], rest[n_cast + 1:]):
        dst[...] = src[...].astype(dst.dtype)


def _rider_specs(riders, steps, index):
    return [pl.BlockSpec((a.shape[0] // steps, a.shape[1]), index) for a in riders]


def _merge_a(h, c, o1, o2, o3, wgat, wgbt, wc, wn, bga, bgb, bc, tm, tn, riders=()):
    n, k = h.shape
    nj = D_MODEL // tn
    tok = lambda width: pl.BlockSpec((tm, width), lambda i, j: (i, 0))
    wt_spec = pl.BlockSpec((tn, k), lambda i, j: (j, 0))
    w_spec = pl.BlockSpec((c.shape[1], tn), lambda i, j: (0, j))
    b_spec = pl.BlockSpec((1, tn), lambda i, j: (0, j))
    ride = _rider_specs(riders, (n // tm) * nj, lambda i, j: (i * nj + j, 0))
    out = pl.pallas_call(
        _merge_a_kernel,
        grid=(n // tm, nj),
        in_specs=[tok(k), tok(D_CONV), tok(D_Q), tok(D_Q), tok(D_Q), wt_spec, wt_spec, w_spec, w_spec,
                  b_spec, b_spec, b_spec] + ride,
        out_specs=[pl.BlockSpec((tm, tn), lambda i, j: (i, j))] + ride,
        out_shape=[jax.ShapeDtypeStruct((n, D_MODEL), BF16)]
        + [jax.ShapeDtypeStruct(a.shape, BF16) for a in riders],
        compiler_params=_params(("parallel", "parallel"), 56),
        name="merge_gates",
    )(h, c, o1, o2, o3, wgat, wgbt, wc, wn, bga, bgb, bc, *riders)
    return out[0], out[1:]


def _merge_b_kernel(x_ref, m_ref, wo_ref, gn_ref, wr_ref, br_ref, *rest, n_real, n_shared, n_riders):
    rider_in = rest[n_shared:n_shared + n_riders]
    outs = rest[n_shared + n_riders:n_shared + n_riders + 5]
    rider_out = rest[n_shared + n_riders + 5:]
    h2_ref = outs[1]

    @pl.when(pl.program_id(0) < n_real)
    def _():
        _merge_b_tile(x_ref, m_ref, wo_ref, gn_ref, wr_ref, br_ref, *outs)
        for src, dst in zip(rider_in, rider_out):
            dst[...] = src[...].astype(dst.dtype)

    @pl.when(pl.program_id(0) >= n_real)
    def _():
        h2_ref[...] = jnp.zeros(h2_ref.shape, h2_ref.dtype)


def _merge_b_tile(x_ref, m_ref, wo_ref, gn_ref, wr_ref, br_ref, x1_ref, h2_ref, eid_ref, rw_ref, cnt_ref):
    x1 = x_ref[...] + _mm(m_ref[...], wo_ref[...])
    x1_ref[...] = x1
    ms = jnp.mean(x1 * x1, axis=-1, keepdims=True)
    h2 = x1 * lax.rsqrt(ms + EPS) * gn_ref[...]
    h2_ref[...] = h2
    logits = _mm_nt(wr_ref[...], h2.astype(BF16)) + br_ref[...]
    tm = logits.shape[1]
    row = lax.broadcasted_iota(I32, (8, tm), 0)
    lg = jnp.where(row < N_GROUPS, logits[0:8], -jnp.inf)
    gmax = jnp.max(lg, axis=0, keepdims=True)
    gidx = jnp.min(jnp.where(lg == gmax, row, 8), axis=0, keepdims=True)
    wg = 1.0 / jnp.sum(jnp.exp(lg - gmax), axis=0, keepdims=True)
    le = jnp.zeros((8, tm), F32)
    for g in range(N_GROUPS):
        le = jnp.where(gidx == g, logits[8 + 8 * g:16 + 8 * g], le)
    ee = jnp.exp(le - jnp.max(le, axis=0, keepdims=True))
    pz = ee / jnp.sum(ee, axis=0, keepdims=True)
    p1 = jnp.max(pz, axis=0, keepdims=True)
    i1 = jnp.min(jnp.where(pz == p1, row, 8), axis=0, keepdims=True)
    pz2 = jnp.where(row == i1, -1.0, pz)
    p2 = jnp.max(pz2, axis=0, keepdims=True)
    i2 = jnp.min(jnp.where(pz2 == p2, row, 8), axis=0, keepdims=True)
    den = p1 + p2
    e1 = gidx * EXP_PER_GROUP + i1
    e2 = gidx * EXP_PER_GROUP + i2
    eid_ref[...] = jnp.where(row == 0, e1, jnp.where(row == 1, e2, 0))
    rw_ref[...] = jnp.where(row == 0, wg * p1 / den, jnp.where(row == 1, wg * p2 / den, 0.0))
    erow = lax.broadcasted_iota(I32, (N_EXPERTS, tm), 0)
    hits = (erow == e1).astype(F32) + (erow == e2).astype(F32)
    cnt_ref[...] = jnp.broadcast_to(jnp.sum(hits, axis=1, keepdims=True), (N_EXPERTS, LANES)).astype(I32)


def _merge_b(x, m, wo, gn, wr, br, tm, h2_all=None, row0=0, slack=0, riders=()):
    n, d = x.shape
    blk0 = row0 // tm
    n_real = n // tm
    real = lambda i: jnp.minimum(i, n_real - 1)
    shared = [] if h2_all is None else [h2_all]
    h2_rows = n + slack if h2_all is None else h2_all.shape[0]
    ride = _rider_specs(riders, n_real, lambda i: (real(i), 0))
    out = pl.pallas_call(
        functools.partial(_merge_b_kernel, n_real=n_real, n_shared=len(shared), n_riders=len(riders)),
        grid=((n + slack) // tm,),
        in_specs=[
            pl.BlockSpec((tm, d), lambda i: (real(i), 0)),
            pl.BlockSpec((tm, d), lambda i: (real(i), 0)),
            pl.BlockSpec((d, d), lambda i: (0, 0)),
            pl.BlockSpec((1, d), lambda i: (0, 0)),
            pl.BlockSpec((LANES, d), lambda i: (0, 0)),
            pl.BlockSpec((LANES, 1), lambda i: (0, 0)),
        ] + [pl.BlockSpec(memory_space=pl.ANY)] * len(shared) + ride,
        out_specs=[
            pl.BlockSpec((tm, d), lambda i: (real(i), 0)),
            pl.BlockSpec((tm, d), lambda i: (blk0 + i, 0)),
            pl.BlockSpec((8, tm), lambda i: (0, real(i))),
            pl.BlockSpec((8, tm), lambda i: (0, real(i))),
            pl.BlockSpec((None, N_EXPERTS, LANES), lambda i: (real(i), 0, 0)),
        ] + ride,
        out_shape=[
            jax.ShapeDtypeStruct((n, d), F32),
            jax.ShapeDtypeStruct((h2_rows, d), F32),
            jax.ShapeDtypeStruct((8, n), I32),
            jax.ShapeDtypeStruct((8, n), F32),
            jax.ShapeDtypeStruct((n_real, N_EXPERTS, LANES), I32),
        ] + [jax.ShapeDtypeStruct(a.shape, BF16) for a in riders],
        input_output_aliases={6: 1} if shared else {},
        compiler_params=_params(("arbitrary",), 56),
        name="merge_out_router",
    )(x, m, wo, gn, wr, br, *shared, *riders)
    return out[:5], out[5:]


_HALO = 32


def _conv_prompt_kernel(u_ref, prev_ref, w_ref, b_ref, g_ref, beta_ref, o_ref, buf_ref, acc_ref, sh_ref, *, tt):
    i = pl.program_id(1)
    prev = prev_ref[pl.ds(tt - _HALO, _HALO), :]
    buf_ref[pl.ds(0, _HALO), :] = jnp.where(i > 0, prev, 0.0)
    buf_ref[pl.ds(_HALO, tt), :] = u_ref[...]
    off = _HALO - (CONV_W - 1)
    sub = 8
    rows = 128

    def chunk(cc, carry):
        cols = pl.ds(pl.multiple_of(cc * LANES, LANES), LANES)
        for r in range(sub):
            n_shift = (tt + _HALO - r) // sub * sub
            sh_ref[r, pl.ds(0, n_shift), :] = buf_ref[pl.ds(r, n_shift), cols]
        for r0 in range(0, tt, rows):
            acc = jnp.zeros((rows, LANES), F32)
            for k in range(CONV_W):
                r = (off + k) % sub
                acc = acc + sh_ref[r, pl.ds(r0 + off + k - r, rows), :] * w_ref[pl.ds(k, 1), cols]
            acc_ref[pl.ds(r0, rows), cols] = acc + b_ref[:, cols]
        return carry

    lax.fori_loop(0, D_CONV // LANES, chunk, 0)
    c = acc_ref[...]
    mu = jnp.mean(c, axis=-1, keepdims=True)
    var = jnp.mean(jnp.square(c - mu), axis=-1, keepdims=True)
    y = (c - mu) * lax.rsqrt(var + EPS) * g_ref[...] + beta_ref[...]
    o_ref[...] = _silu(y).astype(o_ref.dtype)


def _conv_prompt(u, w, b, g, beta, batch, seq, tt):
    nt = seq // tt
    vec = pl.BlockSpec((1, D_CONV), lambda bi, i: (0, 0))
    return pl.pallas_call(
        functools.partial(_conv_prompt_kernel, tt=tt),
        grid=(batch, nt),
        in_specs=[
            pl.BlockSpec((tt, D_CONV), lambda bi, i: (bi * nt + i, 0)),
            pl.BlockSpec((tt, D_CONV), lambda bi, i: (bi * nt + jnp.maximum(i - 1, 0), 0)),
            pl.BlockSpec((CONV_W, D_CONV), lambda bi, i: (0, 0)),
            vec, vec, vec,
        ],
        out_specs=pl.BlockSpec((tt, D_CONV), lambda bi, i: (bi * nt + i, 0)),
        out_shape=jax.ShapeDtypeStruct((batch * seq, D_CONV), BF16),
        scratch_shapes=[pltpu.VMEM((tt + _HALO, D_CONV), F32), pltpu.VMEM((tt, D_CONV), F32),
                        pltpu.VMEM((8, tt + _HALO, LANES), F32)],
        compiler_params=_params(("parallel", "parallel")),
        name="conv_prompt",
    )(u, u, w, b, g, beta)


def _conv_sample_kernel(st_ref, u_ref, w_ref, b_ref, g_ref, beta_ref, c_ref, new_ref):
    hist = CONV_W - 1
    u = u_ref[...]
    acc = u * w_ref[pl.ds(hist, 1), :] + b_ref[...]
    for t in range(hist):
        acc = acc + st_ref[t] * w_ref[pl.ds(t, 1), :]
    mu = jnp.mean(acc, axis=-1, keepdims=True)
    var = jnp.mean(jnp.square(acc - mu), axis=-1, keepdims=True)
    y = (acc - mu) * lax.rsqrt(var + EPS) * g_ref[...] + beta_ref[...]
    c_ref[...] = _silu(y).astype(c_ref.dtype)
    for t in range(hist - 1):
        new_ref[t] = st_ref[t + 1]
    new_ref[hist - 1] = u


def _conv_sample(state_t, u, w, b, g, beta):
    hist, bs, _ = state_t.shape
    return pl.pallas_call(
        _conv_sample_kernel,
        out_shape=[jax.ShapeDtypeStruct((bs, D_CONV), BF16), jax.ShapeDtypeStruct((hist, bs, D_CONV), F32)],
        compiler_params=pltpu.CompilerParams(vmem_limit_bytes=48 * MIB),
        name="conv_sample",
    )(state_t, u, w, b, g, beta)


def _bias_lookup(rb_ref, head, dist, valid):
    n = jnp.maximum(dist, 0)
    max_exact = N_BUCKETS // 2
    nf = jnp.maximum(n, 1).astype(F32)
    large = max_exact + (jnp.log(nf / max_exact) / math.log(MAX_DIST / max_exact)
                         * (N_BUCKETS - max_exact)).astype(I32)
    bucket = jnp.where(n < max_exact, n, jnp.minimum(large, N_BUCKETS - 1))
    val = jnp.zeros(dist.shape, F32)
    for k in range(N_BUCKETS):
        val = jnp.where(bucket == k, rb_ref[head, k], val)
    return jnp.where(valid, val, NEG)


def _bias_prompt_kernel(rb_ref, cmp_ref, win_ref, sel_ref, *, nqt, n_chunk):
    g = pl.program_id(0)
    key = lax.broadcasted_iota(I32, (LANES, QT), 0)
    qry = lax.broadcasted_iota(I32, (LANES, QT), 1)
    for hh in range(HPG):
        head = g * HPG + hh
        cols = pl.ds(hh * QT, QT)
        lookup = lambda dist, valid: _bias_lookup(rb_ref, head, dist, valid) * LOG2E

        for eb in range(_cmp_map_rows(nqt, n_chunk) // LANES):
            block = key + eb * LANES - _cmp_map_shift(nqt, 0)
            dist = qry - block * CMP_STRIDE - (CMP_LEN - 1)
            cmp_ref[0, pl.ds(eb * LANES, LANES), cols] = lookup(dist, dist >= 0)
        for jb in range((WINDOW + QT) // LANES):
            dist = qry - (key + jb * LANES) + WINDOW
            win_ref[0, pl.ds(jb * LANES, LANES), cols] = lookup(dist, (dist >= 0) & (dist <= WINDOW))
        for kind in range(3):
            for jb in range(2):
                dist = kind * QT + qry - (key + jb * LANES)
                sel_ref[0, kind, pl.ds(jb * LANES, LANES), cols] = lookup(dist, dist >= 0)
        far = jnp.full((LANES, QT), 2 * MAX_DIST, I32)
        for jb in range(2):
            sel_ref[0, 3, pl.ds(jb * LANES, LANES), cols] = lookup(far, far >= 0)


def _cmp_map_shift(nqt, qt):
    return (nqt - 1 - qt) * (QT // CMP_STRIDE)


def _cmp_map_rows(nqt, n_chunk):
    return -(-(n_chunk + _cmp_map_shift(nqt, 0)) // LANES) * LANES


def _bias_prompt(rb_t, nqt, n_chunk):
    ext = _cmp_map_rows(nqt, n_chunk)
    return pl.pallas_call(
        functools.partial(_bias_prompt_kernel, nqt=nqt, n_chunk=n_chunk),
        grid=(N_KV,),
        in_specs=[pl.BlockSpec(memory_space=pltpu.SMEM)],
        out_specs=[
            pl.BlockSpec((1, ext, GROUP_ROWS), lambda g: (g, 0, 0)),
            pl.BlockSpec((1, WINDOW + QT, GROUP_ROWS), lambda g: (g, 0, 0)),
            pl.BlockSpec((1, 4, 2 * LANES, GROUP_ROWS), lambda g: (g, 0, 0, 0)),
        ],
        out_shape=[
            jax.ShapeDtypeStruct((N_KV, ext, GROUP_ROWS), F32),
            jax.ShapeDtypeStruct((N_KV, WINDOW + QT, GROUP_ROWS), F32),
            jax.ShapeDtypeStruct((N_KV, 4, 2 * LANES, GROUP_ROWS), F32),
        ],
        compiler_params=_params(("parallel",)),
        name="bias_maps_prompt",
    )(rb_t)


def _bias_sample_kernel(rb_ref, cmp_ref, win_ref, sel_ref, *, past, n_chunk, n_pages):
    col = lax.broadcasted_iota(I32, (N_HEADS, LANES), 1)
    rb = rb_ref[...]

    def lookup(dist):
        n = jnp.maximum(dist, 0)
        max_exact = N_BUCKETS // 2
        nf = jnp.maximum(n, 1).astype(F32)
        large = max_exact + (jnp.log(nf / max_exact) / math.log(MAX_DIST / max_exact)
                             * (N_BUCKETS - max_exact)).astype(I32)
        bucket = jnp.where(n < max_exact, n, jnp.minimum(large, N_BUCKETS - 1))
        val = jnp.zeros(dist.shape, F32)
        for k in range(N_BUCKETS):
            val = jnp.where(bucket == k, rb[:, k:k + 1], val)
        return jnp.where(dist >= 0, val, NEG)

    def cmp_block(cb, carry):
        dist = past - (col + cb * LANES) * CMP_STRIDE - (CMP_LEN - 1)
        cmp_ref[cb] = lookup(dist)
        return carry

    lax.fori_loop(0, n_chunk // LANES, cmp_block, 0)

    def win_block(jb, carry):
        win_ref[jb] = lookup(WINDOW - (col + jb * LANES))
        return carry

    lax.fori_loop(0, WINDOW // LANES + 1, win_block, 0)

    def sel_page(p, carry):
        sel_ref[p] = lookup(past - (p * PAGE + col))
        return carry

    lax.fori_loop(0, n_pages + 1, sel_page, 0)


def _bias_sample(rb_t, past, n_chunk, n_pages):
    cmp_b, win_b, sel_b = pl.pallas_call(
        functools.partial(_bias_sample_kernel, past=past, n_chunk=n_chunk, n_pages=n_pages),
        out_shape=[
            jax.ShapeDtypeStruct((n_chunk // LANES, N_HEADS, LANES), F32),
            jax.ShapeDtypeStruct((WINDOW // LANES + 1, N_HEADS, LANES), F32),
            jax.ShapeDtypeStruct((n_pages + 1, N_HEADS, LANES), F32),
        ],
        name="bias_maps_sample",
    )(rb_t)
    flat = lambda a: a.transpose(1, 0, 2).reshape(N_HEADS, -1)
    return flat(cmp_b), flat(win_b), sel_b


_CHUNKS_PER_PAGE = PAGE // CMP_STRIDE
_CHUNK_PITCH = 24


def _compress_kernel(pt_ref, *refs, pp, n_riders):
    page_refs = refs[:pp + 1]
    w1k_ref, w1v_ref, pe_ref, w2k_ref, w2v_ref = refs[pp + 1:pp + 6]
    rider_in = refs[pp + 6:pp + 6 + n_riders]
    o_ref = refs[pp + 6 + n_riders]
    rider_out = refs[pp + 7 + n_riders:pp + 7 + 2 * n_riders]
    tr_ref = refs[pp + 7 + 2 * n_riders]
    del pt_ref
    for src, dst in zip(rider_in, rider_out):
        dst[...] = src[...].astype(dst.dtype)
    nch = pp * _CHUNKS_PER_PAGE
    m = nch + 8
    n_slab = D_KV // LANES

    def transpose(s):
        for k in range(pp + 1):
            t = page_refs[k][pl.ds(s * LANES, LANES), :].T
            for n in range(_CHUNKS_PER_PAGE):
                row = (k * _CHUNKS_PER_PAGE + n) * _CHUNK_PITCH
                tr_ref[s, pl.ds(row, CMP_STRIDE), :] = t[n * CMP_STRIDE:(n + 1) * CMP_STRIDE]

    def chunks(s, _):
        return jnp.concatenate(
            [tr_ref.at[s][pl.ds(p, m, stride=_CHUNK_PITCH), :] for p in range(CMP_STRIDE)], axis=1)

    def mlp(s, x):
        is_v = int(s >= n_slab // 2)
        w1 = (w1v_ref if is_v else w1k_ref)[...]
        w2 = (w2v_ref if is_v else w2k_ref)[...]
        y = _mm(x.astype(BF16), w1)
        y_pe = _mm(pe_ref[...].astype(BF16), w1)
        pe_term = y_pe[2 * is_v:2 * is_v + 1, 0:LANES] + y_pe[2 * is_v + 1:2 * is_v + 2, LANES:2 * LANES]
        pre = y[0:nch, 0:LANES] + y[1:nch + 1, LANES:2 * LANES] + pe_term
        o_ref[0, :, pl.ds(s * LANES, LANES)] = _mm(_gelu_tanh(pre).astype(BF16), w2)

    _staggered(n_slab, transpose, chunks, mlp)


def _compress(pages, page_ids, weights, nb, n_pages, pp, paged, riders=()):
    steps = n_pages // pp
    nch = pp * _CHUNKS_PER_PAGE
    ride = _rider_specs(riders, nb * steps, lambda b, s, pt: (b * steps + s, 0))

    def page_spec(k):
        if paged:
            return pl.BlockSpec(
                (None, D_KV, PAGE),
                lambda b, s, pt: (pt[b * n_pages + jnp.minimum(s * pp + k, n_pages - 1)], 0, 0))
        return pl.BlockSpec((None, D_KV, PAGE), lambda b, s, pt: (b, 0, jnp.minimum(s * pp + k, n_pages - 1)))

    full = lambda a: pl.BlockSpec(a.shape, lambda b, s, pt: (0,) * a.ndim)
    grid_spec = pltpu.PrefetchScalarGridSpec(
        num_scalar_prefetch=1,
        grid=(nb, steps),
        in_specs=[page_spec(k) for k in range(pp + 1)] + [full(a) for a in weights] + ride,
        out_specs=[pl.BlockSpec((1, nch, D_KV), lambda b, s, pt: (b, s, 0))] + ride,
        scratch_shapes=[pltpu.VMEM((D_KV // LANES, (pp + 1) * _CHUNKS_PER_PAGE * _CHUNK_PITCH, LANES), F32)],
    )
    out = pl.pallas_call(
        functools.partial(_compress_kernel, pp=pp, n_riders=len(riders)),
        grid_spec=grid_spec,
        out_shape=[jax.ShapeDtypeStruct((nb, n_pages * _CHUNKS_PER_PAGE, D_KV), F32)]
        + [jax.ShapeDtypeStruct(a.shape, BF16) for a in riders],
        compiler_params=_params(("parallel", "parallel"), 56),
        name="compress_paged" if paged else "compress_prompt",
    )(page_ids, *([pages] * (pp + 1)), *weights, *riders)
    return out[0], out[1:]


def _compress_weights(pe_k, w1_k, w2_k, pe_v, w1_v, w2_v):
    eye = jnp.eye(2, dtype=F32)

    def first(w1, j):
        w = w1[j * CMP_STRIDE:(j + 1) * CMP_STRIDE]
        return jnp.einsum("pdf,gh->pgdhf", w, eye).reshape(CMP_STRIDE * LANES, 2 * CMP_HID).astype(BF16)

    def pos(pe, j):
        return jnp.tile(pe[j * CMP_STRIDE:(j + 1) * CMP_STRIDE, None, :], (1, 2, 1)).reshape(1, CMP_STRIDE * LANES)

    def second(w2):
        return jnp.einsum("fd,gh->gfhd", w2, eye).reshape(2 * CMP_HID, 2 * HEAD_DIM).astype(BF16)

    pe = jnp.concatenate([pos(pe_k, 0), pos(pe_k, 1), pos(pe_v, 0), pos(pe_v, 1)], axis=0)
    pe = jnp.pad(pe, ((0, 8 - pe.shape[0]), (0, 0)))
    both = lambda w1: jnp.concatenate([first(w1, 0), first(w1, 1)], axis=1)
    return (both(w1_k), both(w1_v), pe, second(w2_k), second(w2_v))


def _staggered(n, scores, softmax, values, ahead=1, lag=0):
    s = {g: scores(g) for g in range(min(ahead, n))}
    p = {}
    out = []
    for g in range(n + lag):
        if g + ahead < n:
            s[g + ahead] = scores(g + ahead)
        if g < n:
            p[g] = softmax(g, s.pop(g))
        if g - lag >= 0:
            out.append(values(g - lag, p.pop(g - lag)))
    return out


def _group_queries(qt_ref, g):
    return jnp.concatenate(
        [qt_ref[pl.ds((g * HPG + hh) * HEAD_DIM, HEAD_DIM), :] for hh in range(HPG)], axis=1)


def _store_heads(o_ref, o_t, gate, g, branch):
    lane = lax.broadcasted_iota(I32, (QT, 2 * HEAD_DIM), 1)
    for pair in range(HPG // 2):
        h = g * HPG + 2 * pair
        col = branch * N_HEADS + h
        both = jnp.concatenate([o_t[:, (2 * pair) * QT:(2 * pair + 1) * QT],
                                o_t[:, (2 * pair + 1) * QT:(2 * pair + 2) * QT]], axis=0)
        gates = jnp.where(lane < HEAD_DIM, gate[:, col:col + 1], gate[:, col + 1:col + 2])
        o_ref[:, pl.ds(h * HEAD_DIM, 2 * HEAD_DIM)] = both.T * gates


def _cmp_prompt_kernel(qt_ref, kc_ref, map_ref, gate_ref, ov_ref, o_ref, sb_ref, *, n_sel, top):
    qt = pl.program_id(1)
    kv = kc_ref[0]
    gate = gate_ref[...]
    ov = ov_ref[...]
    half = D_KV // 2
    n_chunk = kv.shape[0]
    map_rows = pl.ds(pl.multiple_of(_cmp_map_shift(pl.num_programs(1), qt), 8), n_chunk)
    bias = lambda g: map_ref[g, map_rows, :]
    vt_all = kv[:, half:].T
    jrow = lax.broadcasted_iota(I32, (n_sel, QT), 0)
    tpos = qt * QT + lax.broadcasted_iota(I32, (n_sel, QT), 1)
    cur = tpos // SEL_LEN
    forced = (jrow == 0) | (jrow == cur) | (jrow == cur - 1)
    valid = jrow * SEL_LEN <= tpos
    def scores(g):
        kg = kv[:, g * HEAD_DIM:(g + 1) * HEAD_DIM].astype(BF16)
        return _mm(kg, _group_queries(qt_ref, g)) + bias(g)

    def softmax(g, st):
        e = jnp.exp2(st - jnp.max(st, axis=0, keepdims=True))
        p = e / jnp.sum(e, axis=0, keepdims=True) * (bias(g) > 0.5 * NEG).astype(F32)
        return p.astype(BF16)

    def values(g, pb):
        vgt = vt_all[g * HEAD_DIM:(g + 1) * HEAD_DIM].astype(BF16)
        _store_heads(o_ref, _mm(vgt, pb), gate, g, 0)
        return pb

    probs = _staggered(N_KV, scores, softmax, values, ahead=2, lag=1)
    for g in range(N_KV):
        pb = probs[g]
        imp = jnp.zeros((n_sel, QT), F32)
        for hh in range(HPG):
            imp = imp + _mm(ov, pb[:, hh * QT:(hh + 1) * QT])
        score = jnp.where(forced, jnp.inf, jnp.where(valid, imp, -jnp.inf))
        rank = jnp.zeros((n_sel, QT), I32)
        for i in range(n_sel):
            si = score[i:i + 1, :]
            ahead = (si > score) | ((si == score) & (jrow > i))
            rank = rank + ahead.astype(I32)
        sb_ref[pl.ds(g * n_sel, n_sel), :] = jnp.where(rank < top, 0.0, NEG).astype(sb_ref.dtype)


def _cmp_prompt(q_t, kcmp, cmap, gates, ov_t, batch, seq):
    nqt = seq // QT
    n_chunk = kcmp.shape[1]
    n_sel = ov_t.shape[0]
    top = min(SEL_TOP, n_sel)
    return pl.pallas_call(
        functools.partial(_cmp_prompt_kernel, n_sel=n_sel, top=top),
        grid=(batch, nqt),
        in_specs=[
            pl.BlockSpec((D_Q, QT), lambda b, i: (0, b * nqt + i)),
            pl.BlockSpec((1, n_chunk, D_KV), lambda b, i: (b, 0, 0)),
            pl.BlockSpec(cmap.shape, lambda b, i: (0, 0, 0)),
            pl.BlockSpec((QT, LANES), lambda b, i: (b * nqt + i, 0)),
            pl.BlockSpec(ov_t.shape, lambda b, i: (0, 0)),
        ],
        out_specs=[
            pl.BlockSpec((QT, D_Q), lambda b, i: (b * nqt + i, 0)),
            pl.BlockSpec((None, N_KV * n_sel, QT), lambda b, i: (b * nqt + i, 0, 0)),
        ],
        out_shape=[
            jax.ShapeDtypeStruct((batch * seq, D_Q), F32),
            jax.ShapeDtypeStruct((batch * nqt, N_KV * n_sel, QT), BF16),
        ],
        compiler_params=_params(("parallel", "parallel")),
        name="cmp_attn_prompt",
    )(q_t, kcmp, cmap, gates, ov_t)


_TK = 2 * LANES


def _sel_prompt_kernel(qt_ref, kv_ref, sb_ref, map_ref, gate_ref, *rest, n_sel, seq, n_riders):
    rider_in = rest[:n_riders]
    o_ref = rest[n_riders]
    rider_out = rest[n_riders + 1:2 * n_riders + 1]
    ka_ref, vt_ref, qa_ref = rest[2 * n_riders + 1:]
    for src, dst in zip(rider_in, rider_out):
        dst[...] = src[...].astype(dst.dtype)
    qt = pl.program_id(1)
    aug = 2 * HEAD_DIM
    half = D_KV // 2

    @pl.when(qt == 0)
    def _():
        pos = lax.broadcasted_iota(I32, (seq, aug - HEAD_DIM), 0)
        blk = lax.broadcasted_iota(I32, (seq, aug - HEAD_DIM), 1)
        onehot = jnp.where(pos // SEL_LEN == blk, 1.0, 0.0)
        k_rows = kv_ref[0, pl.ds(0, half), :].T
        for g in range(N_KV):
            ka_ref[g] = jnp.concatenate([k_rows[:, g * HEAD_DIM:(g + 1) * HEAD_DIM], onehot], axis=1).astype(BF16)
            vt_ref[g] = kv_ref[0, pl.ds(half + g * HEAD_DIM, HEAD_DIM), :].astype(BF16)

    gate = gate_ref[...]
    sb = sb_ref[...]
    diag = qt // 2
    n_tiles = diag + 1
    pad = jnp.zeros((aug - HEAD_DIM - n_sel, GROUP_ROWS), BF16)
    for g in range(N_KV):
        sbg = sb[g * n_sel:(g + 1) * n_sel]
        qa_ref[g] = jnp.concatenate([_group_queries(qt_ref, g), jnp.concatenate([sbg] * HPG, axis=1), pad], axis=0)

    def tile(kt, carry):
        start = pl.multiple_of(kt * _TK, _TK)
        kind = jnp.where(kt == diag, qt % 2, jnp.where((kt == diag - 1) & (qt % 2 == 0), 2, 3))
        def scores(g):
            return _mm(ka_ref[g, pl.ds(start, _TK), :], qa_ref[g]) + map_ref[g, kind]

        def softmax(g, st):
            m_i, l_i, _ = carry[g]
            m_new = jnp.maximum(m_i, jnp.max(st, axis=0, keepdims=True))
            alpha = jnp.exp2(m_i - m_new)
            p = jnp.exp2(st - m_new)
            return m_new, alpha * l_i + jnp.sum(p, axis=0, keepdims=True), alpha, p.astype(BF16)

        def values(g, sm):
            m_new, l_new, alpha, pb = sm
            return m_new, l_new, alpha * carry[g][2] + _mm(vt_ref[g, :, pl.ds(start, _TK)], pb)

        return tuple(_staggered(N_KV, scores, softmax, values, ahead=2, lag=1))

    init = tuple((jnp.full((1, GROUP_ROWS), -jnp.inf, F32), jnp.zeros((1, GROUP_ROWS), F32),
                  jnp.zeros((HEAD_DIM, GROUP_ROWS), F32)) for _ in range(N_KV))
    final = lax.fori_loop(0, n_tiles, tile, init)
    for g in range(N_KV):
        _, l_f, acc = final[g]
        _store_heads(o_ref, acc / l_f, gate, g, 1)


def _sel_prompt(q_t, kv_t, selbias, smap, gates, batch, seq, riders=()):
    nqt = seq // QT
    n_sel = selbias.shape[1] // N_KV
    ride = _rider_specs(riders, batch * nqt, lambda b, i: (b * nqt + i, 0))
    out = pl.pallas_call(
        functools.partial(_sel_prompt_kernel, n_sel=n_sel, seq=seq, n_riders=len(riders)),
        grid=(batch, nqt),
        in_specs=[
            pl.BlockSpec((D_Q, QT), lambda b, i: (0, b * nqt + i)),
            pl.BlockSpec((1, D_KV, seq), lambda b, i: (b, 0, 0)),
            pl.BlockSpec((None, N_KV * n_sel, QT), lambda b, i: (b * nqt + i, 0, 0)),
            pl.BlockSpec(smap.shape, lambda b, i: (0, 0, 0, 0)),
            pl.BlockSpec((QT, LANES), lambda b, i: (b * nqt + i, 0)),
        ] + ride,
        out_specs=[pl.BlockSpec((QT, D_Q), lambda b, i: (b * nqt + i, 0))] + ride,
        out_shape=[jax.ShapeDtypeStruct((batch * seq, D_Q), F32)]
        + [jax.ShapeDtypeStruct(a.shape, BF16) for a in riders],
        scratch_shapes=[pltpu.VMEM((N_KV, seq, 2 * HEAD_DIM), BF16), pltpu.VMEM((N_KV, HEAD_DIM, seq), BF16),
                        pltpu.VMEM((N_KV, 2 * HEAD_DIM, GROUP_ROWS), BF16)],
        compiler_params=_params(("arbitrary", "arbitrary"), 56),
        name="sel_attn_prompt",
    )(q_t, kv_t, selbias, smap, gates, *riders)
    return out[0], out[1:]


_WIN_TILES = WINDOW // QT + 1


def _win_prompt_kernel(qt_ref, *refs):
    kv_refs = refs[:_WIN_TILES]
    map_ref, gate_ref, o_ref = refs[_WIN_TILES:]
    qt = pl.program_id(1)
    gate = gate_ref[...]
    half = D_KV // 2
    span = _WIN_TILES * QT
    key = lax.broadcasted_iota(I32, (span, GROUP_ROWS), 0)
    in_seq = key >= (_WIN_TILES - 1 - qt) * QT
    k_rows = jnp.concatenate([r[0, pl.ds(0, half), :].T for r in kv_refs], axis=0)
    def scores(g):
        kg = k_rows[:, g * HEAD_DIM:(g + 1) * HEAD_DIM].astype(BF16)
        return jnp.where(in_seq, _mm(kg, _group_queries(qt_ref, g)) + map_ref[g], NEG)

    def softmax(g, st):
        e = jnp.exp2(st - jnp.max(st, axis=0, keepdims=True))
        return (e / jnp.sum(e, axis=0, keepdims=True)).astype(BF16)

    def values(g, pb):
        vgt = jnp.concatenate([r[0, pl.ds(half + g * HEAD_DIM, HEAD_DIM), :] for r in kv_refs],
                              axis=1).astype(BF16)
        _store_heads(o_ref, _mm(vgt, pb), gate, g, 2)

    _staggered(N_KV, scores, softmax, values, ahead=2, lag=1)


def _win_prompt(q_t, kv_t, wmap, gates, batch, seq):
    nqt = seq // QT

    def kv_spec(k):
        return pl.BlockSpec((1, D_KV, QT), lambda b, i: (b, 0, jnp.maximum(i - (_WIN_TILES - 1) + k, 0)))

    return pl.pallas_call(
        _win_prompt_kernel,
        grid=(batch, nqt),
        in_specs=[pl.BlockSpec((D_Q, QT), lambda b, i: (0, b * nqt + i))]
        + [kv_spec(k) for k in range(_WIN_TILES)]
        + [pl.BlockSpec(wmap.shape, lambda b, i: (0, 0, 0)),
           pl.BlockSpec((QT, LANES), lambda b, i: (b * nqt + i, 0))],
        out_specs=pl.BlockSpec((QT, D_Q), lambda b, i: (b * nqt + i, 0)),
        out_shape=jax.ShapeDtypeStruct((batch * seq, D_Q), F32),
        compiler_params=_params(("parallel", "parallel")),
        name="win_attn_prompt",
    )(q_t, *([kv_t] * _WIN_TILES), wmap, gates)


def _group_rows(x_by_group):
    row = lax.broadcasted_iota(I32, x_by_group[0].shape, 0)
    out = x_by_group[0]
    for g in range(1, N_KV):
        out = jnp.where(row // HPG == g, x_by_group[g], out)
    return out


_CMP_SAMPLE_ROWS = 4


def _cmp_sample_kernel(q_ref, kc_ref, bias_ref, ov_ref, gate_ref, o_ref, idx_ref, *, n_sel, past):
    bias = bias_ref[...]
    half = D_KV // 2
    imps = []
    for bl in range(_CMP_SAMPLE_ROWS):
        q = q_ref[bl]
        kv = kc_ref[bl].astype(BF16)
        s = _group_rows([_mm_nt(q, kv[:, g * HEAD_DIM:(g + 1) * HEAD_DIM]) for g in range(N_KV)]) + bias
        e = jnp.exp(s - jnp.max(s, axis=-1, keepdims=True))
        p = e / jnp.sum(e, axis=-1, keepdims=True) * (bias > 0.5 * NEG).astype(F32)
        pb = p.astype(BF16)
        o = _group_rows([_mm(pb, kv[:, half + g * HEAD_DIM:half + (g + 1) * HEAD_DIM]) for g in range(N_KV)])
        o_ref[bl] = o * gate_ref[bl]
        imp_h = _mm(pb, ov_ref[...])
        imps += [imp_h[g * HPG:g * HPG + 1] + imp_h[g * HPG + 1:g * HPG + 2] + imp_h[g * HPG + 2:g * HPG + 3]
                 + imp_h[g * HPG + 3:g * HPG + 4] for g in range(N_KV)]
    imp = jnp.concatenate(imps, axis=0)
    rows, nsp = imp.shape
    j = lax.broadcasted_iota(I32, (rows, nsp), 1)
    cur = past // SEL_LEN
    forced = (j == 0) | (j == cur) | (j == cur - 1)
    valid = (j * SEL_LEN <= past) & (j < n_sel)
    score = jnp.where(valid, jnp.where(forced, jnp.inf, imp), -jnp.inf)
    slot = lax.broadcasted_iota(I32, (rows, LANES), 1)
    res = jnp.zeros((rows, LANES), I32)
    for k in range(min(SEL_TOP, n_sel)):
        best = jnp.max(score, axis=-1, keepdims=True)
        pick = jnp.min(jnp.where(score == best, j, nsp), axis=-1, keepdims=True)
        res = jnp.where(slot == k, pick, res)
        score = jnp.where(j == pick, -jnp.inf, score)
    idx_ref[...] = res


def _cmp_sample(q3, kcmp, bias, ov, gate3, n_sel, past):
    bs, n_chunk, _ = kcmp.shape
    nb = _CMP_SAMPLE_ROWS
    heads = pl.BlockSpec((nb, N_HEADS, HEAD_DIM), lambda b: (b, 0, 0))
    return pl.pallas_call(
        functools.partial(_cmp_sample_kernel, n_sel=n_sel, past=past),
        grid=(bs // nb,),
        in_specs=[
            heads,
            pl.BlockSpec((nb, n_chunk, D_KV), lambda b: (b, 0, 0)),
            pl.BlockSpec(bias.shape, lambda b: (0, 0)),
            pl.BlockSpec(ov.shape, lambda b: (0, 0)),
            heads,
        ],
        out_specs=[heads, pl.BlockSpec((nb * N_KV, LANES), lambda b: (b, 0))],
        out_shape=[
            jax.ShapeDtypeStruct((bs, N_HEADS, HEAD_DIM), F32),
            jax.ShapeDtypeStruct((bs * N_KV, LANES), I32),
        ],
        compiler_params=_params(("parallel",)),
        name="cmp_attn_sample",
    )(q3, kcmp, bias, ov, gate3)


def _sel_sample_kernel(blk_ref, phys_ref, q_ref, new_ref, bias_ref, gate_ref, *refs, top, n_pages):
    kv_refs = refs[:top]
    k_refs = [r.at[0] for r in kv_refs]
    v_refs = [r.at[1] for r in kv_refs]
    o_ref = refs[top]
    del phys_ref
    b = pl.program_id(0)
    g = pl.program_id(1)
    q = q_ref[0, 0]
    lane = lax.broadcasted_iota(I32, (8, PAGE), 1)
    scores = []
    for k in range(top):
        blk = blk_ref[(b * N_KV + g) * top + k]
        page = jnp.minimum(blk // 2, n_pages)
        s = _mm(q, k_refs[k][...].astype(BF16)) + bias_ref[page]
        ok = (lane // SEL_LEN == blk % 2) & (blk < 2 * n_pages)
        scores.append(jnp.where(ok, s, NEG))
    k_new = new_ref[0, 0, 0:1, :].astype(BF16).astype(F32)
    v_new = new_ref[0, 0, 1:2, :].astype(BF16).astype(F32)
    s_new = jnp.sum(q.astype(F32) * k_new, axis=-1, keepdims=True) + bias_ref[n_pages]
    scores.append(s_new)
    s_all = jnp.concatenate(scores, axis=1)
    e = jnp.exp(s_all - jnp.max(s_all, axis=-1, keepdims=True))
    p = e / jnp.sum(e, axis=-1, keepdims=True)
    o = p[:, top * PAGE:top * PAGE + 1].astype(BF16).astype(F32) * v_new
    for k in range(top):
        o = o + _mm_nt(p[:, k * PAGE:(k + 1) * PAGE].astype(BF16), v_refs[k][...].astype(BF16))
    o_ref[0, 0] = o * gate_ref[0, 0]


def _sel_sample(blk, phys, q4, new4, bias4, gate4, pool5, top, n_pages):
    bs = q4.shape[0]

    def kv_spec(k):
        return pl.BlockSpec(
            (None, 2, None, HEAD_DIM, PAGE),
            lambda b, g, blk_r, phys_r: (phys_r[(b * N_KV + g) * top + k], 0, g, 0, 0))

    grp = lambda b, g, blk_r, phys_r: (b, g, 0, 0)
    grid_spec = pltpu.PrefetchScalarGridSpec(
        num_scalar_prefetch=2,
        grid=(bs, N_KV),
        in_specs=[
            pl.BlockSpec((1, 1, 8, HEAD_DIM), grp),
            pl.BlockSpec((1, 1, 8, HEAD_DIM), grp),
            pl.BlockSpec((None, n_pages + 1, 8, LANES), lambda b, g, blk_r, phys_r: (g, 0, 0, 0)),
            pl.BlockSpec((1, 1, 8, HEAD_DIM), grp),
        ] + [kv_spec(k) for k in range(top)],
        out_specs=pl.BlockSpec((1, 1, 8, HEAD_DIM), grp),
    )
    return pl.pallas_call(
        functools.partial(_sel_sample_kernel, top=top, n_pages=n_pages),
        grid_spec=grid_spec,
        out_shape=jax.ShapeDtypeStruct((bs, N_KV, 8, HEAD_DIM), F32),
        compiler_params=_params(("parallel", "parallel")),
        name="sel_attn_sample",
    )(blk, phys, q4, new4, bias4, gate4, *([pool5] * top))


def _win_sample_kernel(q_ref, new_ref, st_ref, col_ref, bias_ref, gate_ref, o_ref, nst_ref):
    st = st_ref[0]
    half = D_KV // 2
    for g in range(N_KV):
        q = q_ref[0, g]
        bias = bias_ref[g]
        s_old = _mm(q, st[g * HEAD_DIM:(g + 1) * HEAD_DIM, :].astype(BF16)) + bias[:, 0:WINDOW]
        k_new = new_ref[0, g, 0:1, :].astype(BF16).astype(F32)
        v_new = new_ref[0, g, 1:2, :].astype(BF16).astype(F32)
        s_new = jnp.sum(q.astype(F32) * k_new, axis=-1, keepdims=True) + bias[:, WINDOW:WINDOW + LANES]
        s_all = jnp.concatenate([s_old, s_new], axis=1)
        e = jnp.exp(s_all - jnp.max(s_all, axis=-1, keepdims=True))
        p = e / jnp.sum(e, axis=-1, keepdims=True)
        o = _mm_nt(p[:, 0:WINDOW].astype(BF16), st[half + g * HEAD_DIM:half + (g + 1) * HEAD_DIM, :].astype(BF16))
        o = o + p[:, WINDOW:WINDOW + 1].astype(BF16).astype(F32) * v_new
        o_ref[0, g] = o * gate_ref[0, g]
    rolled = pltpu.roll(st, WINDOW - 1, axis=1)
    lane = lax.broadcasted_iota(I32, (D_KV, LANES), 1)
    nst_ref[0, :, pl.ds(0, WINDOW - LANES)] = rolled[:, 0:WINDOW - LANES]
    nst_ref[0, :, pl.ds(WINDOW - LANES, LANES)] = jnp.where(
        lane == LANES - 1, col_ref[0], rolled[:, WINDOW - LANES:WINDOW])


def _win_sample(q4, new4, state_t, newcol, bias4, gate4):
    bs = q4.shape[0]
    grp = pl.BlockSpec((1, N_KV, 8, HEAD_DIM), lambda b: (b, 0, 0, 0))
    return pl.pallas_call(
        _win_sample_kernel,
        grid=(bs,),
        in_specs=[
            grp, grp,
            pl.BlockSpec((1, D_KV, WINDOW), lambda b: (b, 0, 0)),
            pl.BlockSpec((1, D_KV, LANES), lambda b: (b, 0, 0)),
            pl.BlockSpec(bias4.shape, lambda b: (0, 0, 0)),
            grp,
        ],
        out_specs=[grp, pl.BlockSpec((1, D_KV, WINDOW), lambda b: (b, 0, 0))],
        out_shape=[
            jax.ShapeDtypeStruct((bs, N_KV, 8, HEAD_DIM), F32),
            jax.ShapeDtypeStruct((bs, D_KV, WINDOW), F32),
        ],
        compiler_params=_params(("parallel",)),
        name="win_attn_sample",
    )(q4, new4, state_t, newcol, bias4, gate4)


_MOE_ROWS = 256
_MOE_SLACK = 2 * _MOE_ROWS


def _row_copy(src, src_row, dst, dst_row, sem):
    return pltpu.make_async_copy(src.at[pl.ds(src_row, 1)], dst.at[pl.ds(dst_row, 1)], sem)


def _expert_kernel(be_ref, nu_ref, dst_ref, src_ref, nxt_ref, h_ref, w_ref, wg_ref, wu_ref, wd_ref, yk_ref,
                   xbuf, ybuf, gsem, ssem, *, n_rows, n_live):
    del be_ref
    i = pl.program_id(0)
    slot = i % 2
    n_used = nu_ref[0]

    def gather(rows_ref, dst_slot):
        for r in range(_MOE_ROWS):
            _row_copy(h_ref, rows_ref[0, r], xbuf.at[dst_slot], r, gsem.at[dst_slot]).start()

    def wait_gather(s):
        for r in range(_MOE_ROWS):
            _row_copy(h_ref, 0, xbuf.at[s], 0, gsem.at[s]).wait()

    def scatter(s):
        for r in range(_MOE_ROWS):
            _row_copy(ybuf.at[s], r, yk_ref, dst_ref[0, r], ssem.at[s]).start()

    def wait_scatter(s):
        for r in range(_MOE_ROWS):
            _row_copy(ybuf.at[s], 0, yk_ref, 0, ssem.at[s]).wait()

    @pl.when(i == 0)
    def _():
        gather(src_ref, 0)
        ybuf[1] = jnp.zeros((_MOE_ROWS, ybuf.shape[2]), F32)
        fills = [
            pltpu.make_async_copy(ybuf.at[1, pl.ds(0, min(_MOE_ROWS, n_rows - off))],
                                  yk_ref.at[pl.ds(k * n_rows + off, min(_MOE_ROWS, n_rows - off))], ssem.at[1])
            for k in range(TOP_K) for off in range(n_live, n_rows, _MOE_ROWS)]
        for cp in fills:
            cp.start()
        for cp in fills:
            cp.wait()

    @pl.when(i + 1 < n_used)
    def _():
        gather(nxt_ref, 1 - slot)

    @pl.when(i < n_used)
    def _():
        wait_gather(slot)

        @pl.when(i >= 2)
        def _():
            wait_scatter(slot)

        x = xbuf[slot].astype(BF16)
        a = _silu(_mm(x, wg_ref[...])) * _mm(x, wu_ref[...])
        ybuf[slot] = _mm(a.astype(BF16), wd_ref[...]) * w_ref[...]
        scatter(slot)

        @pl.when(i == n_used - 1)
        def _():
            @pl.when(i >= 1)
            def _():
                wait_scatter(1 - slot)

            wait_scatter(slot)


def _experts(blk_e, n_used, rows_src, rows_dst, rows_w, h2_all, n_live, w_gate, w_up, w_down):
    n_rows, d = h2_all.shape
    n_blk = blk_e.shape[0]
    src3 = rows_src.reshape(n_blk, 1, _MOE_ROWS)
    dst3 = rows_dst.reshape(n_blk, 1, _MOE_ROWS)
    wspec = lambda a: pl.BlockSpec((None,) + a.shape[1:], lambda i, be, nu: (be[i], 0, 0))
    smem_blk = lambda fn: pl.BlockSpec((None, 1, _MOE_ROWS), fn, memory_space=pltpu.SMEM)
    grid_spec = pltpu.PrefetchScalarGridSpec(
        num_scalar_prefetch=2,
        grid=(n_blk,),
        in_specs=[
            smem_blk(lambda i, be, nu: (i, 0, 0)),
            smem_blk(lambda i, be, nu: (i, 0, 0)),
            smem_blk(lambda i, be, nu: (jnp.minimum(i + 1, n_blk - 1), 0, 0)),
            pl.BlockSpec(memory_space=pl.ANY),
            pl.BlockSpec((_MOE_ROWS, 1), lambda i, be, nu: (i, 0)),
            wspec(w_gate), wspec(w_up), wspec(w_down),
        ],
        out_specs=pl.BlockSpec(memory_space=pl.ANY),
        scratch_shapes=[
            pltpu.VMEM((2, _MOE_ROWS, d), F32),
            pltpu.VMEM((2, _MOE_ROWS, d), F32),
            pltpu.SemaphoreType.DMA((2,)),
            pltpu.SemaphoreType.DMA((2,)),
        ],
    )
    return pl.pallas_call(
        functools.partial(_expert_kernel, n_rows=n_rows, n_live=n_live),
        grid_spec=grid_spec,
        out_shape=jax.ShapeDtypeStruct((TOP_K * n_rows, d), F32),
        compiler_params=pltpu.CompilerParams(dimension_semantics=("arbitrary",), vmem_limit_bytes=56 * MIB,
                                             disable_bounds_checks=True),
        name="moe_experts",
    )(blk_e, n_used, dst3, src3, src3, h2_all, rows_w, w_gate, w_up, w_down)


def _combine_kernel(x_ref, y0_ref, y1_ref, g_ref, o_ref):
    x = x_ref[...] + (y0_ref[...] + y1_ref[...])
    ms = jnp.mean(x * x, axis=-1, keepdims=True)
    o_ref[...] = x * lax.rsqrt(ms + EPS) * g_ref[...]


def _combine(x1, yk, gfin, row0, tm):
    n, d = x1.shape
    yk3 = yk.reshape(TOP_K, -1, d)
    blk0 = row0 // tm
    return pl.pallas_call(
        _combine_kernel,
        grid=(n // tm,),
        in_specs=[
            pl.BlockSpec((tm, d), lambda i: (i, 0)),
            pl.BlockSpec((None, tm, d), lambda i: (0, blk0 + i, 0)),
            pl.BlockSpec((None, tm, d), lambda i: (1, blk0 + i, 0)),
            pl.BlockSpec((1, d), lambda i: (0, 0)),
        ],
        out_specs=pl.BlockSpec((tm, d), lambda i: (i, 0)),
        out_shape=jax.ShapeDtypeStruct((n, d), F32),
        compiler_params=_params(("parallel",)),
        name="moe_combine",
    )(x1, yk3, yk3, gfin)


def _dispatch(eid, w, counts, n_rows):
    a = eid.shape[0]
    tb = _MOE_ROWS
    order = jnp.argsort(eid).astype(I32)
    padded = (counts + tb - 1) // tb * tb
    seg_end = jnp.cumsum(padded)
    seg_start = seg_end - padded
    start = jnp.cumsum(counts) - counts
    n_blk = (a + N_EXPERTS * (tb - 1) + tb - 1) // tb
    blk_e = jnp.minimum(jnp.sum((seg_end[None, :] <= (jnp.arange(n_blk) * tb)[:, None]).astype(I32), axis=1),
                        N_EXPERTS - 1)
    blk = jnp.arange(n_blk)[:, None]
    r = jnp.arange(tb)[None, :]
    off = (blk * tb - seg_start[blk_e][:, None]) + r
    valid = off < counts[blk_e][:, None]
    asg = order[jnp.clip(start[blk_e][:, None] + off, 0, a - 1)]
    rows_src = jnp.where(valid, asg // TOP_K, 0).astype(I32)
    rows_dst = jnp.where(valid, (asg % TOP_K) * n_rows + asg // TOP_K, (blk % 2) * n_rows + (n_rows - tb) + r)
    rows_w = jnp.where(valid, w[asg], 0.0)
    n_used = (seg_end[-1] // tb).astype(I32).reshape(1)
    return rows_src, rows_dst.astype(I32), rows_w.reshape(-1, 1), blk_e.astype(I32), n_used


def _overlap(n_chunk, n_sel, n_sel_pad):
    cs = jnp.arange(n_chunk)[:, None] * CMP_STRIDE
    ss = jnp.arange(n_sel_pad)[None, :] * SEL_LEN
    hit = (cs < ss + SEL_LEN) & (cs + CMP_LEN > ss) & (jnp.arange(n_sel_pad)[None, :] < n_sel)
    return hit.astype(BF16)


def kernel(x_prompt, x_sample, cache_cmp_kv, cache_sel_kv, state_win_kv, state_conv, page_table, rel_bias,
           norm_mix, w_in, b_in, conv_w, conv_b, conv_ln_g, conv_ln_b, w_conv_out, b_conv_out,
           cmp_pe_k, cmp_w1_k, cmp_w2_k, cmp_pe_v, cmp_w1_v, cmp_w2_v, w_nsa_out, w_out, norm_ffn,
           w_rg, b_rg, w_re, b_re, w_gate, w_up, w_down, norm_final):
    batch, seq, _ = x_prompt.shape
    bs = x_sample.shape[0]
    n_pool = cache_cmp_kv.shape[1]
    n_pages = page_table.shape[1]
    past = n_pages * PAGE
    n_tok = batch * seq
    assert bs + _MOE_ROWS <= _MOE_SLACK and n_tok % bs == 0 and seq % _TK == 0 and bs % 16 == 0
    l = 0

    wt = w_in[l].T.astype(BF16)
    bias_in = b_in[l]
    cuts = [0, D_CONV, 2 * D_CONV, 2 * D_CONV + D_Q, 2 * D_CONV + D_Q + 3 * D_KV]
    cuts += [cuts[-1] + 3 * N_HEADS, cuts[-1] + 3 * N_HEADS + D_MODEL, cuts[-1] + 3 * N_HEADS + 2 * D_MODEL]
    seg = lambda k: (wt[cuts[k]:cuts[k + 1]], bias_in[cuts[k]:cuts[k + 1]][None, :])
    (wa, ba), (wb, bb), (wq, bq), (wkv, bkv), (wzg, bzg), (wga, bga), (wgb, bgb) = [seg(k) for k in range(7)]
    wzg = jnp.pad(wzg, ((0, LANES - 3 * N_HEADS), (0, 0)))
    bzg = jnp.pad(bzg, ((0, 0), (0, LANES - 3 * N_HEADS)))
    wc = w_conv_out[l].astype(BF16)
    wn = w_nsa_out[l].astype(BF16)
    wo = w_out[l].astype(BF16)
    wr = jnp.zeros((LANES, D_MODEL), F32).at[0:N_GROUPS].set(w_rg[l].T).at[8:8 + N_EXPERTS].set(w_re[l].T)
    br = jnp.zeros((LANES,), F32).at[0:N_GROUPS].set(b_rg[l]).at[8:8 + N_EXPERTS].set(b_re[l].reshape(-1))
    wr, br = wr.astype(BF16), br[:, None]
    cw = _compress_weights(cmp_pe_k[l], cmp_w1_k[l], cmp_w2_k[l], cmp_pe_v[l], cmp_w1_v[l], cmp_w2_v[l])
    rb_t = rel_bias.T
    gmix, gffn, gfin = norm_mix[l][None, :], norm_ffn[l][None, :], norm_final[None, :]
    vec = lambda a: a[l][None, :]

    q_scale = HEAD_DIM ** -0.5
    xp = x_prompt.reshape(n_tok, D_MODEL)
    xs = x_sample.reshape(bs, D_MODEL)
    flat = lambda w: w[l].reshape(-1, w.shape[-1])
    (hp, up, qp, gates_p, *kv_p), (wg_bf,) = _project_prompt(
        xp, gmix, wa, wb, wq, wkv, wzg, ba, bb, bq.reshape(-1, 1), bkv.reshape(-1, 1), bzg,
        q_scale * LOG2E, batch, seq, 256, riders=(flat(w_gate),))
    hs = _rmsnorm(xs, gmix, bs)
    us = _glu(hs, wa, wb, ba, bb, bs, 512)
    qs = _proj(hs, wq, bq, lambda z: z * q_scale, BF16, bs, 512, "proj_q")
    gates_s = _proj(hs, wzg, bzg, _sigmoid, F32, bs, LANES, "proj_head_gates")
    kv_s = _proj(hs, wkv, bkv, lambda z: z, F32, bs, 512, "proj_kv_sample")

    conv_args = (conv_w[l], vec(conv_b), vec(conv_ln_g), vec(conv_ln_b))
    cp = _conv_prompt(up, *conv_args, batch, seq, 256)
    cs, new_conv_s = _conv_sample(state_conv[l].transpose(1, 0, 2), us, *conv_args)

    nqt = seq // QT
    n_chunk_p = seq // CMP_STRIDE
    n_chunk_s = n_pages * _CHUNKS_PER_PAGE
    cmap, wmap, smap = _bias_prompt(rb_t, nqt, n_chunk_p)
    cbias_s, wbias_s, sbias_s = _bias_sample(rb_t, past, n_chunk_s, n_pages)

    ident = jnp.zeros((1,), I32)
    kcmp_p, _ = _compress(kv_p[0], ident, cw, batch, seq // PAGE, seq // PAGE, False)
    feat_major = lambda a: a.transpose(0, 2, 3, 4, 1)
    pool_cmp = feat_major(cache_cmp_kv[l]).reshape(n_pool, D_KV, PAGE)
    pp_s = min(32, n_pages)
    kcmp_s, (wd_bf,) = _compress(pool_cmp, page_table.reshape(-1), cw, bs, n_pages, pp_s, True,
                                 riders=(flat(w_down),))

    n_sel_p = seq // SEL_LEN
    o_cmp_p, selbias = _cmp_prompt(qp, kcmp_p, cmap, gates_p, _overlap(n_chunk_p, n_sel_p, n_sel_p).T,
                                   batch, seq)
    n_sel_s = -(-(past + 1) // SEL_LEN)
    n_sel_pad = -(-n_sel_s // LANES) * LANES
    top_s = min(SEL_TOP, n_sel_s)
    head_gate = lambda k: jnp.broadcast_to(
        gates_s[:, k * N_HEADS:(k + 1) * N_HEADS, None], (bs, N_HEADS, HEAD_DIM))
    qs3 = qs.reshape(bs, N_HEADS, HEAD_DIM)
    o_cmp_s, idx_s = _cmp_sample(qs3, kcmp_s, cbias_s, _overlap(n_chunk_s, n_sel_s, n_sel_pad), head_gate(0),
                                 n_sel_s, past)

    o_sel_p, (wu_bf,) = _sel_prompt(qp, kv_p[1], selbias, smap, gates_p, batch, seq, riders=(flat(w_up),))
    by_group = lambda a: jnp.pad(a.reshape(bs, N_KV, HPG, -1), ((0, 0), (0, 0), (0, 8 - HPG), (0, 0)))
    q4 = by_group(qs3)
    blk = idx_s[:, :top_s].reshape(-1)
    phys = jnp.take_along_axis(
        page_table, jnp.minimum(blk // 2, n_pages - 1).reshape(bs, -1), axis=1).reshape(-1)
    kv_new = lambda k: jnp.pad(
        kv_s[:, k * D_KV:(k + 1) * D_KV].reshape(bs, 2, N_KV, HEAD_DIM).transpose(0, 2, 1, 3),
        ((0, 0), (0, 0), (0, 6), (0, 0)))
    bias_by_group = lambda a: jnp.pad(
        jnp.moveaxis(a, -2, 0).reshape((N_KV, HPG) + a.shape[:-2] + a.shape[-1:]),
        ((0, 0), (0, 8 - HPG)) + ((0, 0),) * (a.ndim - 1))
    sb4 = jnp.moveaxis(bias_by_group(sbias_s), 1, 2)
    pool_sel = feat_major(cache_sel_kv[l])
    o_sel_s = _sel_sample(blk, phys, q4, kv_new(1), sb4, by_group(head_gate(1)), pool_sel, top_s, n_pages)

    o_win_p = _win_prompt(qp, kv_p[2], wmap, gates_p, batch, seq)
    win_t = feat_major(state_win_kv[l]).reshape(bs, D_KV, WINDOW)
    newcol = jnp.broadcast_to(kv_s[:, 2 * D_KV:3 * D_KV, None], (bs, D_KV, LANES))
    o_win_s, new_win_s = _win_sample(q4, kv_new(2), win_t, newcol, bias_by_group(wbias_s), by_group(head_gate(2)))

    m_p, _ = _merge_a(hp, cp, o_cmp_p, o_sel_p, o_win_p, wga, wgb, wc, wn, bga, bgb, vec(b_conv_out), 512, 1024)
    (x1_p, h2_all, eid_p, rw_p, cnt_p), _ = _merge_b(xp, m_p, wo, gffn, wr, br, 256, slack=_MOE_SLACK)
    from_groups = lambda a: a[:, :, :HPG, :].reshape(bs, D_Q)
    m_s, _ = _merge_a(hs, cs, o_cmp_s.reshape(bs, D_Q), from_groups(o_sel_s), from_groups(o_win_s),
                      wga, wgb, wc, wn, bga, bgb, vec(b_conv_out), bs, 512)
    (x1_s, h2_all, eid_s, rw_s, cnt_s), _ = _merge_b(xs, m_s, wo, gffn, wr, br, bs, h2_all=h2_all, row0=n_tok)

    eid = jnp.concatenate([eid_p[:TOP_K].T, eid_s[:TOP_K].T], axis=0).reshape(-1)
    rw = jnp.concatenate([rw_p[:TOP_K].T, rw_s[:TOP_K].T], axis=0).reshape(-1)
    counts = jnp.sum(cnt_p[:, :, 0], axis=0) + jnp.sum(cnt_s[:, :, 0], axis=0)
    rows_src, rows_dst, rows_w, blk_e, n_used = _dispatch(eid, rw, counts, h2_all.shape[0])
    yk = _experts(blk_e, n_used, rows_src, rows_dst, rows_w, h2_all, n_tok + bs,
                  wg_bf.reshape(w_gate.shape[1:]), wu_bf.reshape(w_up.shape[1:]), wd_bf.reshape(w_down.shape[1:]))
    y_p = _combine(x1_p, yk, gfin, 0, 256)
    y_s = _combine(x1_s, yk, gfin, n_tok, bs)

    def cache_rows(a_t, rows):
        return a_t.reshape(a_t.shape[0], 2, N_KV, HEAD_DIM, rows).transpose(0, 4, 1, 2, 3)[None]

    new_row = lambda k: kv_s[:, k * D_KV:(k + 1) * D_KV].reshape(1, bs, 1, 2, N_KV, HEAD_DIM)
    hist = CONV_W - 1
    return (
        y_p.reshape(batch, seq, D_MODEL),
        y_s.reshape(bs, 1, D_MODEL),
        cache_rows(kv_p[0], seq),
        new_row(0),
        cache_rows(kv_p[1], seq),
        new_row(1),
        cache_rows(kv_p[2][:, :, seq - WINDOW:], WINDOW),
        cache_rows(new_win_s, WINDOW),
        up.reshape(batch, seq, D_CONV)[None, :, seq - hist:],
        new_conv_s.transpose(1, 0, 2)[None],
    )
```

```python
import functools
import math

import jax
import jax.numpy as jnp
from jax import lax
from jax.experimental import pallas as pl
from jax.experimental.pallas import tpu as pltpu

F32 = jnp.float32
BF16 = jnp.bfloat16
I32 = jnp.int32

D_MODEL = 2048
D_CONV = 1024
CONV_W = 31
N_HEADS = 16
HEAD_DIM = 64
N_KV = 4
HPG = N_HEADS // N_KV
D_Q = N_HEADS * HEAD_DIM
D_KV = 2 * N_KV * HEAD_DIM
CMP_LEN = 32
CMP_STRIDE = 16
CMP_HID = 64
SEL_LEN = 64
SEL_TOP = 16
WINDOW = 512
N_BUCKETS = 32
MAX_DIST = 128
N_GROUPS = 4
EXP_PER_GROUP = 8
N_EXPERTS = N_GROUPS * EXP_PER_GROUP
TOP_K = 2
D_EXPERT = 1024
EPS = 1e-6
NEG = -1e30
PAGE = 128
LANES = 128
QT = 128
GROUP_ROWS = HPG * QT
MIB = 1024 * 1024
LOG2E = math.log2(math.e)


def _params(sem, vmem_mib=48):
    return pltpu.CompilerParams(dimension_semantics=sem, vmem_limit_bytes=vmem_mib * MIB)


def _mm_nt(a, b):
    return lax.dot_general(a, b, (((1,), (1,)), ((), ())), preferred_element_type=F32)


def _mm(a, b):
    return jnp.dot(a, b, preferred_element_type=F32)


def _sigmoid(z):
    return 1.0 / (1.0 + jnp.exp(-z))


def _silu(z):
    return z * _sigmoid(z)


def _gelu_tanh(z):
    return 0.5 * z * (1.0 + jnp.tanh(math.sqrt(2.0 / math.pi) * (z + 0.044715 * (z * z * z))))


def _rmsnorm_kernel(x_ref, g_ref, o_ref):
    x = x_ref[...]
    ms = jnp.mean(x * x, axis=-1, keepdims=True)
    o_ref[...] = (x * lax.rsqrt(ms + EPS) * g_ref[...]).astype(o_ref.dtype)


def _rmsnorm(x, g, tm):
    n, d = x.shape
    return pl.pallas_call(
        _rmsnorm_kernel,
        grid=(n // tm,),
        in_specs=[pl.BlockSpec((tm, d), lambda i: (i, 0)), pl.BlockSpec((1, d), lambda i: (0, 0))],
        out_specs=pl.BlockSpec((tm, d), lambda i: (i, 0)),
        out_shape=jax.ShapeDtypeStruct((n, d), BF16),
        compiler_params=_params(("parallel",)),
        name="rmsnorm",
    )(x, g)


def _proj_kernel(h_ref, w_ref, b_ref, o_ref, *, act):
    z = _mm_nt(h_ref[...], w_ref[...]) + b_ref[...]
    o_ref[...] = act(z).astype(o_ref.dtype)


def _proj(h, wt, b, act, out_dtype, tm, tn, name):
    n, k = h.shape
    nout = wt.shape[0]
    return pl.pallas_call(
        functools.partial(_proj_kernel, act=act),
        grid=(n // tm, nout // tn),
        in_specs=[
            pl.BlockSpec((tm, k), lambda i, j: (i, 0)),
            pl.BlockSpec((tn, k), lambda i, j: (j, 0)),
            pl.BlockSpec((1, tn), lambda i, j: (0, j)),
        ],
        out_specs=pl.BlockSpec((tm, tn), lambda i, j: (i, j)),
        out_shape=jax.ShapeDtypeStruct((n, nout), out_dtype),
        compiler_params=_params(("parallel", "parallel")),
        name=name,
    )(h, wt, b)


def _glu_kernel(h_ref, wa_ref, wb_ref, ba_ref, bb_ref, o_ref):
    h = h_ref[...]
    a = _mm_nt(h, wa_ref[...]) + ba_ref[...]
    b = _mm_nt(h, wb_ref[...]) + bb_ref[...]
    o_ref[...] = a * _sigmoid(b)


def _glu(h, wat, wbt, ba, bb, tm, tn):
    n, k = h.shape
    nout = wat.shape[0]
    wspec = pl.BlockSpec((tn, k), lambda i, j: (j, 0))
    bspec = pl.BlockSpec((1, tn), lambda i, j: (0, j))
    return pl.pallas_call(
        _glu_kernel,
        grid=(n // tm, nout // tn),
        in_specs=[pl.BlockSpec((tm, k), lambda i, j: (i, 0)), wspec, wspec, bspec, bspec],
        out_specs=pl.BlockSpec((tm, tn), lambda i, j: (i, j)),
        out_shape=jax.ShapeDtypeStruct((n, nout), F32),
        compiler_params=_params(("parallel", "parallel")),
        name="glu",
    )(h, wat, wbt, ba, bb)


def _project_prompt_kernel(x_ref, g_ref, wa_ref, wb_ref, wq_ref, wkv_ref, wzg_ref, ba_ref, bb_ref, bq_ref,
                           bkv_ref, bzg_ref, *rest, q_scale, n_riders, kinds):
    rider_in = rest[:n_riders]
    h_ref, u_ref, qt_ref, gates_ref = rest[n_riders:n_riders + 4]
    kv_refs = rest[n_riders + 4:n_riders + 4 + kinds]
    rider_out = rest[n_riders + 4 + kinds:]
    for src, dst in zip(rider_in, rider_out):
        dst[...] = src[...].astype(dst.dtype)
    x = x_ref[...]
    ms = jnp.mean(x * x, axis=-1, keepdims=True)
    h = (x * lax.rsqrt(ms + EPS) * g_ref[...]).astype(BF16)
    h_ref[...] = h
    a = _mm_nt(h, wa_ref[...]) + ba_ref[...]
    b = _mm_nt(h, wb_ref[...]) + bb_ref[...]
    u_ref[...] = a * _sigmoid(b)
    qt_ref[...] = ((_mm_nt(wq_ref[...], h) + bq_ref[...]) * q_scale).astype(qt_ref.dtype)
    gates_ref[...] = _sigmoid(_mm_nt(h, wzg_ref[...]) + bzg_ref[...])
    for kind, o_ref in enumerate(kv_refs):
        rows = pl.ds(kind * D_KV, D_KV)
        o_ref[0] = _mm_nt(wkv_ref[rows, :], h) + bkv_ref[rows, :]


def _project_prompt(x, g, wa, wb, wq, wkv, wzg, ba, bb, bq_col, bkv_col, bzg, q_scale, batch, seq, tm,
                    riders=()):
    n, d = x.shape
    kinds = wkv.shape[0] // D_KV
    nt = seq // tm
    tok = lambda width: pl.BlockSpec((tm, width), lambda i: (i, 0))
    const = lambda a: pl.BlockSpec(a.shape, lambda i: (0,) * a.ndim, pipeline_mode=pl.Buffered(1))
    consts = (g, wa, wb, wq, wkv, wzg, ba, bb, bq_col, bkv_col, bzg)
    ride = _rider_specs(riders, n // tm, lambda i: (i, 0))
    out = pl.pallas_call(
        functools.partial(_project_prompt_kernel, q_scale=q_scale, n_riders=len(riders), kinds=kinds),
        grid=(n // tm,),
        in_specs=[tok(d)] + [const(a) for a in consts] + ride,
        out_specs=[tok(d), tok(D_CONV), pl.BlockSpec((D_Q, tm), lambda i: (0, i)), tok(LANES)]
        + [pl.BlockSpec((1, D_KV, tm), lambda i: (i // nt, 0, i % nt))] * kinds + ride,
        out_shape=[
            jax.ShapeDtypeStruct((n, d), BF16),
            jax.ShapeDtypeStruct((n, D_CONV), F32),
            jax.ShapeDtypeStruct((D_Q, n), BF16),
            jax.ShapeDtypeStruct((n, LANES), F32),
        ] + [jax.ShapeDtypeStruct((batch, D_KV, seq), F32)] * kinds
        + [jax.ShapeDtypeStruct(a.shape, BF16) for a in riders],
        compiler_params=_params(("parallel",), 56),
        name="project_prompt",
    )(x, *consts, *riders)
    return out[:4 + kinds], out[4 + kinds:]


def _merge_a_kernel(h_ref, c_ref, o1_ref, o2_ref, o3_ref, wga_ref, wgb_ref, wc_ref, wn_ref,
                    bga_ref, bgb_ref, bc_ref, *rest):
    n_cast = (len(rest) - 1) // 2
    m_ref = rest[n_cast]
    h = h_ref[...]
    ga = _sigmoid(_mm_nt(h, wga_ref[...]) + bga_ref[...])
    gb = _sigmoid(_mm_nt(h, wgb_ref[...]) + bgb_ref[...])
    yc = _mm(c_ref[...], wc_ref[...]) + bc_ref[...]
    o = (o1_ref[...] + o2_ref[...] + o3_ref[...]).astype(BF16)
    yn = _mm(o, wn_ref[...])
    m_ref[...] = (ga * yc + gb * yn).astype(m_ref.dtype)
    for src, dst in zip(rest[:n_cast], rest[n_cast + 1:]):
        dst[...] = src[...].astype(dst.dtype)


def _rider_specs(riders, steps, index):
    return [pl.BlockSpec((a.shape[0] // steps, a.shape[1]), index) for a in riders]


def _merge_a(h, c, o1, o2, o3, wgat, wgbt, wc, wn, bga, bgb, bc, tm, tn, riders=()):
    n, k = h.shape
    nj = D_MODEL // tn
    tok = lambda width: pl.BlockSpec((tm, width), lambda i, j: (i, 0))
    wt_spec = pl.BlockSpec((tn, k), lambda i, j: (j, 0))
    w_spec = pl.BlockSpec((c.shape[1], tn), lambda i, j: (0, j))
    b_spec = pl.BlockSpec((1, tn), lambda i, j: (0, j))
    ride = _rider_specs(riders, (n // tm) * nj, lambda i, j: (i * nj + j, 0))
    out = pl.pallas_call(
        _merge_a_kernel,
        grid=(n // tm, nj),
        in_specs=[tok(k), tok(D_CONV), tok(D_Q), tok(D_Q), tok(D_Q), wt_spec, wt_spec, w_spec, w_spec,
                  b_spec, b_spec, b_spec] + ride,
        out_specs=[pl.BlockSpec((tm, tn), lambda i, j: (i, j))] + ride,
        out_shape=[jax.ShapeDtypeStruct((n, D_MODEL), BF16)]
        + [jax.ShapeDtypeStruct(a.shape, BF16) for a in riders],
        compiler_params=_params(("parallel", "parallel"), 56),
        name="merge_gates",
    )(h, c, o1, o2, o3, wgat, wgbt, wc, wn, bga, bgb, bc, *riders)
    return out[0], out[1:]


def _merge_b_kernel(x_ref, m_ref, wo_ref, gn_ref, wr_ref, br_ref, *rest, n_real, n_shared, n_riders):
    rider_in = rest[n_shared:n_shared + n_riders]
    outs = rest[n_shared + n_riders:n_shared + n_riders + 5]
    rider_out = rest[n_shared + n_riders + 5:]
    h2_ref = outs[1]

    @pl.when(pl.program_id(0) < n_real)
    def _():
        _merge_b_tile(x_ref, m_ref, wo_ref, gn_ref, wr_ref, br_ref, *outs)
        for src, dst in zip(rider_in, rider_out):
            dst[...] = src[...].astype(dst.dtype)

    @pl.when(pl.program_id(0) >= n_real)
    def _():
        h2_ref[...] = jnp.zeros(h2_ref.shape, h2_ref.dtype)


def _merge_b_tile(x_ref, m_ref, wo_ref, gn_ref, wr_ref, br_ref, x1_ref, h2_ref, eid_ref, rw_ref, cnt_ref):
    x1 = x_ref[...] + _mm(m_ref[...], wo_ref[...])
    x1_ref[...] = x1
    ms = jnp.mean(x1 * x1, axis=-1, keepdims=True)
    h2 = x1 * lax.rsqrt(ms + EPS) * gn_ref[...]
    h2_ref[...] = h2
    logits = _mm_nt(wr_ref[...], h2.astype(BF16)) + br_ref[...]
    tm = logits.shape[1]
    row = lax.broadcasted_iota(I32, (8, tm), 0)
    lg = jnp.where(row < N_GROUPS, logits[0:8], -jnp.inf)
    gmax = jnp.max(lg, axis=0, keepdims=True)
    gidx = jnp.min(jnp.where(lg == gmax, row, 8), axis=0, keepdims=True)
    wg = 1.0 / jnp.sum(jnp.exp(lg - gmax), axis=0, keepdims=True)
    le = jnp.zeros((8, tm), F32)
    for g in range(N_GROUPS):
        le = jnp.where(gidx == g, logits[8 + 8 * g:16 + 8 * g], le)
    ee = jnp.exp(le - jnp.max(le, axis=0, keepdims=True))
    pz = ee / jnp.sum(ee, axis=0, keepdims=True)
    p1 = jnp.max(pz, axis=0, keepdims=True)
    i1 = jnp.min(jnp.where(pz == p1, row, 8), axis=0, keepdims=True)
    pz2 = jnp.where(row == i1, -1.0, pz)
    p2 = jnp.max(pz2, axis=0, keepdims=True)
    i2 = jnp.min(jnp.where(pz2 == p2, row, 8), axis=0, keepdims=True)
    den = p1 + p2
    e1 = gidx * EXP_PER_GROUP + i1
    e2 = gidx * EXP_PER_GROUP + i2
    eid_ref[...] = jnp.where(row == 0, e1, jnp.where(row == 1, e2, 0))
    rw_ref[...] = jnp.where(row == 0, wg * p1 / den, jnp.where(row == 1, wg * p2 / den, 0.0))
    erow = lax.broadcasted_iota(I32, (N_EXPERTS, tm), 0)
    hits = (erow == e1).astype(F32) + (erow == e2).astype(F32)
    cnt_ref[...] = jnp.broadcast_to(jnp.sum(hits, axis=1, keepdims=True), (N_EXPERTS, LANES)).astype(I32)


def _merge_b(x, m, wo, gn, wr, br, tm, h2_all=None, row0=0, slack=0, riders=()):
    n, d = x.shape
    blk0 = row0 // tm
    n_real = n // tm
    real = lambda i: jnp.minimum(i, n_real - 1)
    shared = [] if h2_all is None else [h2_all]
    h2_rows = n + slack if h2_all is None else h2_all.shape[0]
    ride = _rider_specs(riders, n_real, lambda i: (real(i), 0))
    out = pl.pallas_call(
        functools.partial(_merge_b_kernel, n_real=n_real, n_shared=len(shared), n_riders=len(riders)),
        grid=((n + slack) // tm,),
        in_specs=[
            pl.BlockSpec((tm, d), lambda i: (real(i), 0)),
            pl.BlockSpec((tm, d), lambda i: (real(i), 0)),
            pl.BlockSpec((d, d), lambda i: (0, 0)),
            pl.BlockSpec((1, d), lambda i: (0, 0)),
            pl.BlockSpec((LANES, d), lambda i: (0, 0)),
            pl.BlockSpec((LANES, 1), lambda i: (0, 0)),
        ] + [pl.BlockSpec(memory_space=pl.ANY)] * len(shared) + ride,
        out_specs=[
            pl.BlockSpec((tm, d), lambda i: (real(i), 0)),
            pl.BlockSpec((tm, d), lambda i: (blk0 + i, 0)),
            pl.BlockSpec((8, tm), lambda i: (0, real(i))),
            pl.BlockSpec((8, tm), lambda i: (0, real(i))),
            pl.BlockSpec((None, N_EXPERTS, LANES), lambda i: (real(i), 0, 0)),
        ] + ride,
        out_shape=[
            jax.ShapeDtypeStruct((n, d), F32),
            jax.ShapeDtypeStruct((h2_rows, d), F32),
            jax.ShapeDtypeStruct((8, n), I32),
            jax.ShapeDtypeStruct((8, n), F32),
            jax.ShapeDtypeStruct((n_real, N_EXPERTS, LANES), I32),
        ] + [jax.ShapeDtypeStruct(a.shape, BF16) for a in riders],
        input_output_aliases={6: 1} if shared else {},
        compiler_params=_params(("arbitrary",), 56),
        name="merge_out_router",
    )(x, m, wo, gn, wr, br, *shared, *riders)
    return out[:5], out[5:]


_HALO = 32


def _conv_prompt_kernel(u_ref, prev_ref, w_ref, b_ref, g_ref, beta_ref, o_ref, buf_ref, acc_ref, sh_ref, *, tt):
    i = pl.program_id(1)
    prev = prev_ref[pl.ds(tt - _HALO, _HALO), :]
    buf_ref[pl.ds(0, _HALO), :] = jnp.where(i > 0, prev, 0.0)
    buf_ref[pl.ds(_HALO, tt), :] = u_ref[...]
    off = _HALO - (CONV_W - 1)
    sub = 8
    rows = 128

    def chunk(cc, carry):
        cols = pl.ds(pl.multiple_of(cc * LANES, LANES), LANES)
        for r in range(sub):
            n_shift = (tt + _HALO - r) // sub * sub
            sh_ref[r, pl.ds(0, n_shift), :] = buf_ref[pl.ds(r, n_shift), cols]
        for r0 in range(0, tt, rows):
            acc = jnp.zeros((rows, LANES), F32)
            for k in range(CONV_W):
                r = (off + k) % sub
                acc = acc + sh_ref[r, pl.ds(r0 + off + k - r, rows), :] * w_ref[pl.ds(k, 1), cols]
            acc_ref[pl.ds(r0, rows), cols] = acc + b_ref[:, cols]
        return carry

    lax.fori_loop(0, D_CONV // LANES, chunk, 0)
    c = acc_ref[...]
    mu = jnp.mean(c, axis=-1, keepdims=True)
    var = jnp.mean(jnp.square(c - mu), axis=-1, keepdims=True)
    y = (c - mu) * lax.rsqrt(var + EPS) * g_ref[...] + beta_ref[...]
    o_ref[...] = _silu(y).astype(o_ref.dtype)


def _conv_prompt(u, w, b, g, beta, batch, seq, tt):
    nt = seq // tt
    vec = pl.BlockSpec((1, D_CONV), lambda bi, i: (0, 0))
    return pl.pallas_call(
        functools.partial(_conv_prompt_kernel, tt=tt),
        grid=(batch, nt),
        in_specs=[
            pl.BlockSpec((tt, D_CONV), lambda bi, i: (bi * nt + i, 0)),
            pl.BlockSpec((tt, D_CONV), lambda bi, i: (bi * nt + jnp.maximum(i - 1, 0), 0)),
            pl.BlockSpec((CONV_W, D_CONV), lambda bi, i: (0, 0)),
            vec, vec, vec,
        ],
        out_specs=pl.BlockSpec((tt, D_CONV), lambda bi, i: (bi * nt + i, 0)),
        out_shape=jax.ShapeDtypeStruct((batch * seq, D_CONV), BF16),
        scratch_shapes=[pltpu.VMEM((tt + _HALO, D_CONV), F32), pltpu.VMEM((tt, D_CONV), F32),
                        pltpu.VMEM((8, tt + _HALO, LANES), F32)],
        compiler_params=_params(("parallel", "parallel")),
        name="conv_prompt",
    )(u, u, w, b, g, beta)


def _conv_sample_kernel(st_ref, u_ref, w_ref, b_ref, g_ref, beta_ref, c_ref, new_ref):
    hist = CONV_W - 1
    u = u_ref[...]
    acc = u * w_ref[pl.ds(hist, 1), :] + b_ref[...]
    for t in range(hist):
        acc = acc + st_ref[t] * w_ref[pl.ds(t, 1), :]
    mu = jnp.mean(acc, axis=-1, keepdims=True)
    var = jnp.mean(jnp.square(acc - mu), axis=-1, keepdims=True)
    y = (acc - mu) * lax.rsqrt(var + EPS) * g_ref[...] + beta_ref[...]
    c_ref[...] = _silu(y).astype(c_ref.dtype)
    for t in range(hist - 1):
        new_ref[t] = st_ref[t + 1]
    new_ref[hist - 1] = u


def _conv_sample(state_t, u, w, b, g, beta):
    hist, bs, _ = state_t.shape
    return pl.pallas_call(
        _conv_sample_kernel,
        out_shape=[jax.ShapeDtypeStruct((bs, D_CONV), BF16), jax.ShapeDtypeStruct((hist, bs, D_CONV), F32)],
        compiler_params=pltpu.CompilerParams(vmem_limit_bytes=48 * MIB),
        name="conv_sample",
    )(state_t, u, w, b, g, beta)


def _bias_lookup(rb_ref, head, dist, valid):
    n = jnp.maximum(dist, 0)
    max_exact = N_BUCKETS // 2
    nf = jnp.maximum(n, 1).astype(F32)
    large = max_exact + (jnp.log(nf / max_exact) / math.log(MAX_DIST / max_exact)
                         * (N_BUCKETS - max_exact)).astype(I32)
    bucket = jnp.where(n < max_exact, n, jnp.minimum(large, N_BUCKETS - 1))
    val = jnp.zeros(dist.shape, F32)
    for k in range(N_BUCKETS):
        val = jnp.where(bucket == k, rb_ref[head, k], val)
    return jnp.where(valid, val, NEG)


def _bias_prompt_kernel(rb_ref, cmp_ref, win_ref, sel_ref, *, nqt, n_chunk):
    g = pl.program_id(0)
    key = lax.broadcasted_iota(I32, (LANES, QT), 0)
    qry = lax.broadcasted_iota(I32, (LANES, QT), 1)
    for hh in range(HPG):
        head = g * HPG + hh
        cols = pl.ds(hh * QT, QT)
        lookup = lambda dist, valid: _bias_lookup(rb_ref, head, dist, valid) * LOG2E

        for eb in range(_cmp_map_rows(nqt, n_chunk) // LANES):
            block = key + eb * LANES - _cmp_map_shift(nqt, 0)
            dist = qry - block * CMP_STRIDE - (CMP_LEN - 1)
            cmp_ref[0, pl.ds(eb * LANES, LANES), cols] = lookup(dist, dist >= 0)
        for jb in range((WINDOW + QT) // LANES):
            dist = qry - (key + jb * LANES) + WINDOW
            win_ref[0, pl.ds(jb * LANES, LANES), cols] = lookup(dist, (dist >= 0) & (dist <= WINDOW))
        for kind in range(3):
            for jb in range(2):
                dist = kind * QT + qry - (key + jb * LANES)
                sel_ref[0, kind, pl.ds(jb * LANES, LANES), cols] = lookup(dist, dist >= 0)
        far = jnp.full((LANES, QT), 2 * MAX_DIST, I32)
        for jb in range(2):
            sel_ref[0, 3, pl.ds(jb * LANES, LANES), cols] = lookup(far, far >= 0)


def _cmp_map_shift(nqt, qt):
    return (nqt - 1 - qt) * (QT // CMP_STRIDE)


def _cmp_map_rows(nqt, n_chunk):
    return -(-(n_chunk + _cmp_map_shift(nqt, 0)) // LANES) * LANES


def _bias_prompt(rb_t, nqt, n_chunk):
    ext = _cmp_map_rows(nqt, n_chunk)
    return pl.pallas_call(
        functools.partial(_bias_prompt_kernel, nqt=nqt, n_chunk=n_chunk),
        grid=(N_KV,),
        in_specs=[pl.BlockSpec(memory_space=pltpu.SMEM)],
        out_specs=[
            pl.BlockSpec((1, ext, GROUP_ROWS), lambda g: (g, 0, 0)),
            pl.BlockSpec((1, WINDOW + QT, GROUP_ROWS), lambda g: (g, 0, 0)),
            pl.BlockSpec((1, 4, 2 * LANES, GROUP_ROWS), lambda g: (g, 0, 0, 0)),
        ],
        out_shape=[
            jax.ShapeDtypeStruct((N_KV, ext, GROUP_ROWS), F32),
            jax.ShapeDtypeStruct((N_KV, WINDOW + QT, GROUP_ROWS), F32),
            jax.ShapeDtypeStruct((N_KV, 4, 2 * LANES, GROUP_ROWS), F32),
        ],
        compiler_params=_params(("parallel",)),
        name="bias_maps_prompt",
    )(rb_t)


def _bias_sample_kernel(rb_ref, cmp_ref, win_ref, sel_ref, *, past, n_chunk, n_pages):
    col = lax.broadcasted_iota(I32, (N_HEADS, LANES), 1)
    rb = rb_ref[...]

    def lookup(dist):
        n = jnp.maximum(dist, 0)
        max_exact = N_BUCKETS // 2
        nf = jnp.maximum(n, 1).astype(F32)
        large = max_exact + (jnp.log(nf / max_exact) / math.log(MAX_DIST / max_exact)
                             * (N_BUCKETS - max_exact)).astype(I32)
        bucket = jnp.where(n < max_exact, n, jnp.minimum(large, N_BUCKETS - 1))
        val = jnp.zeros(dist.shape, F32)
        for k in range(N_BUCKETS):
            val = jnp.where(bucket == k, rb[:, k:k + 1], val)
        return jnp.where(dist >= 0, val, NEG)

    def cmp_block(cb, carry):
        dist = past - (col + cb * LANES) * CMP_STRIDE - (CMP_LEN - 1)
        cmp_ref[cb] = lookup(dist)
        return carry

    lax.fori_loop(0, n_chunk // LANES, cmp_block, 0)

    def win_block(jb, carry):
        win_ref[jb] = lookup(WINDOW - (col + jb * LANES))
        return carry

    lax.fori_loop(0, WINDOW // LANES + 1, win_block, 0)

    def sel_page(p, carry):
        sel_ref[p] = lookup(past - (p * PAGE + col))
        return carry

    lax.fori_loop(0, n_pages + 1, sel_page, 0)


def _bias_sample(rb_t, past, n_chunk, n_pages):
    cmp_b, win_b, sel_b = pl.pallas_call(
        functools.partial(_bias_sample_kernel, past=past, n_chunk=n_chunk, n_pages=n_pages),
        out_shape=[
            jax.ShapeDtypeStruct((n_chunk // LANES, N_HEADS, LANES), F32),
            jax.ShapeDtypeStruct((WINDOW // LANES + 1, N_HEADS, LANES), F32),
            jax.ShapeDtypeStruct((n_pages + 1, N_HEADS, LANES), F32),
        ],
        name="bias_maps_sample",
    )(rb_t)
    flat = lambda a: a.transpose(1, 0, 2).reshape(N_HEADS, -1)
    return flat(cmp_b), flat(win_b), sel_b


_CHUNKS_PER_PAGE = PAGE // CMP_STRIDE
_CHUNK_PITCH = 24


def _compress_kernel(pt_ref, *refs, pp, n_riders):
    page_refs = refs[:pp + 1]
    w1k_ref, w1v_ref, pe_ref, w2k_ref, w2v_ref = refs[pp + 1:pp + 6]
    rider_in = refs[pp + 6:pp + 6 + n_riders]
    o_ref = refs[pp + 6 + n_riders]
    rider_out = refs[pp + 7 + n_riders:pp + 7 + 2 * n_riders]
    tr_ref = refs[pp + 7 + 2 * n_riders]
    del pt_ref
    for src, dst in zip(rider_in, rider_out):
        dst[...] = src[...].astype(dst.dtype)
    nch = pp * _CHUNKS_PER_PAGE
    m = nch + 8
    n_slab = D_KV // LANES

    def transpose(s):
        for k in range(pp + 1):
            t = page_refs[k][pl.ds(s * LANES, LANES), :].T
            for n in range(_CHUNKS_PER_PAGE):
                row = (k * _CHUNKS_PER_PAGE + n) * _CHUNK_PITCH
                tr_ref[s, pl.ds(row, CMP_STRIDE), :] = t[n * CMP_STRIDE:(n + 1) * CMP_STRIDE]

    def chunks(s, _):
        return jnp.concatenate(
            [tr_ref.at[s][pl.ds(p, m, stride=_CHUNK_PITCH), :] for p in range(CMP_STRIDE)], axis=1)

    def mlp(s, x):
        is_v = int(s >= n_slab // 2)
        w1 = (w1v_ref if is_v else w1k_ref)[...]
        w2 = (w2v_ref if is_v else w2k_ref)[...]
        y = _mm(x.astype(BF16), w1)
        y_pe = _mm(pe_ref[...].astype(BF16), w1)
        pe_term = y_pe[2 * is_v:2 * is_v + 1, 0:LANES] + y_pe[2 * is_v + 1:2 * is_v + 2, LANES:2 * LANES]
        pre = y[0:nch, 0:LANES] + y[1:nch + 1, LANES:2 * LANES] + pe_term
        o_ref[0, :, pl.ds(s * LANES, LANES)] = _mm(_gelu_tanh(pre).astype(BF16), w2)

    _staggered(n_slab, transpose, chunks, mlp)


def _compress(pages, page_ids, weights, nb, n_pages, pp, paged, riders=()):
    steps = n_pages // pp
    nch = pp * _CHUNKS_PER_PAGE
    ride = _rider_specs(riders, nb * steps, lambda b, s, pt: (b * steps + s, 0))

    def page_spec(k):
        if paged:
            return pl.BlockSpec(
                (None, D_KV, PAGE),
                lambda b, s, pt: (pt[b * n_pages + jnp.minimum(s * pp + k, n_pages - 1)], 0, 0))
        return pl.BlockSpec((None, D_KV, PAGE), lambda b, s, pt: (b, 0, jnp.minimum(s * pp + k, n_pages - 1)))

    full = lambda a: pl.BlockSpec(a.shape, lambda b, s, pt: (0,) * a.ndim)
    grid_spec = pltpu.PrefetchScalarGridSpec(
        num_scalar_prefetch=1,
        grid=(nb, steps),
        in_specs=[page_spec(k) for k in range(pp + 1)] + [full(a) for a in weights] + ride,
        out_specs=[pl.BlockSpec((1, nch, D_KV), lambda b, s, pt: (b, s, 0))] + ride,
        scratch_shapes=[pltpu.VMEM((D_KV // LANES, (pp + 1) * _CHUNKS_PER_PAGE * _CHUNK_PITCH, LANES), F32)],
    )
    out = pl.pallas_call(
        functools.partial(_compress_kernel, pp=pp, n_riders=len(riders)),
        grid_spec=grid_spec,
        out_shape=[jax.ShapeDtypeStruct((nb, n_pages * _CHUNKS_PER_PAGE, D_KV), F32)]
        + [jax.ShapeDtypeStruct(a.shape, BF16) for a in riders],
        compiler_params=_params(("parallel", "parallel"), 56),
        name="compress_paged" if paged else "compress_prompt",
    )(page_ids, *([pages] * (pp + 1)), *weights, *riders)
    return out[0], out[1:]


def _compress_weights(pe_k, w1_k, w2_k, pe_v, w1_v, w2_v):
    eye = jnp.eye(2, dtype=F32)

    def first(w1, j):
        w = w1[j * CMP_STRIDE:(j + 1) * CMP_STRIDE]
        return jnp.einsum("pdf,gh->pgdhf", w, eye).reshape(CMP_STRIDE * LANES, 2 * CMP_HID).astype(BF16)

    def pos(pe, j):
        return jnp.tile(pe[j * CMP_STRIDE:(j + 1) * CMP_STRIDE, None, :], (1, 2, 1)).reshape(1, CMP_STRIDE * LANES)

    def second(w2):
        return jnp.einsum("fd,gh->gfhd", w2, eye).reshape(2 * CMP_HID, 2 * HEAD_DIM).astype(BF16)

    pe = jnp.concatenate([pos(pe_k, 0), pos(pe_k, 1), pos(pe_v, 0), pos(pe_v, 1)], axis=0)
    pe = jnp.pad(pe, ((0, 8 - pe.shape[0]), (0, 0)))
    both = lambda w1: jnp.concatenate([first(w1, 0), first(w1, 1)], axis=1)
    return (both(w1_k), both(w1_v), pe, second(w2_k), second(w2_v))


def _staggered(n, scores, softmax, values, ahead=1, lag=0):
    s = {g: scores(g) for g in range(min(ahead, n))}
    p = {}
    out = []
    for g in range(n + lag):
        if g + ahead < n:
            s[g + ahead] = scores(g + ahead)
        if g < n:
            p[g] = softmax(g, s.pop(g))
        if g - lag >= 0:
            out.append(values(g - lag, p.pop(g - lag)))
    return out


def _group_queries(qt_ref, g):
    return jnp.concatenate(
        [qt_ref[pl.ds((g * HPG + hh) * HEAD_DIM, HEAD_DIM), :] for hh in range(HPG)], axis=1)


def _store_heads(o_ref, o_t, gate, g, branch):
    lane = lax.broadcasted_iota(I32, (QT, 2 * HEAD_DIM), 1)
    for pair in range(HPG // 2):
        h = g * HPG + 2 * pair
        col = branch * N_HEADS + h
        both = jnp.concatenate([o_t[:, (2 * pair) * QT:(2 * pair + 1) * QT],
                                o_t[:, (2 * pair + 1) * QT:(2 * pair + 2) * QT]], axis=0)
        gates = jnp.where(lane < HEAD_DIM, gate[:, col:col + 1], gate[:, col + 1:col + 2])
        o_ref[:, pl.ds(h * HEAD_DIM, 2 * HEAD_DIM)] = both.T * gates


def _cmp_prompt_kernel(qt_ref, kc_ref, map_ref, gate_ref, ov_ref, o_ref, sb_ref, *, n_sel, top):
    qt = pl.program_id(1)
    kv = kc_ref[0]
    gate = gate_ref[...]
    ov = ov_ref[...]
    half = D_KV // 2
    n_chunk = kv.shape[0]
    map_rows = pl.ds(pl.multiple_of(_cmp_map_shift(pl.num_programs(1), qt), 8), n_chunk)
    bias = lambda g: map_ref[g, map_rows, :]
    vt_all = kv[:, half:].T
    jrow = lax.broadcasted_iota(I32, (n_sel, QT), 0)
    tpos = qt * QT + lax.broadcasted_iota(I32, (n_sel, QT), 1)
    cur = tpos // SEL_LEN
    forced = (jrow == 0) | (jrow == cur) | (jrow == cur - 1)
    valid = jrow * SEL_LEN <= tpos
    def scores(g):
        kg = kv[:, g * HEAD_DIM:(g + 1) * HEAD_DIM].astype(BF16)
        return _mm(kg, _group_queries(qt_ref, g)) + bias(g)

    def softmax(g, st):
        e = jnp.exp2(st - jnp.max(st, axis=0, keepdims=True))
        p = e / jnp.sum(e, axis=0, keepdims=True) * (bias(g) > 0.5 * NEG).astype(F32)
        return p.astype(BF16)

    def values(g, pb):
        vgt = vt_all[g * HEAD_DIM:(g + 1) * HEAD_DIM].astype(BF16)
        _store_heads(o_ref, _mm(vgt, pb), gate, g, 0)
        return pb

    probs = _staggered(N_KV, scores, softmax, values, ahead=2, lag=1)
    for g in range(N_KV):
        pb = probs[g]
        imp = jnp.zeros((n_sel, QT), F32)
        for hh in range(HPG):
            imp = imp + _mm(ov, pb[:, hh * QT:(hh + 1) * QT])
        score = jnp.where(forced, jnp.inf, jnp.where(valid, imp, -jnp.inf))
        rank = jnp.zeros((n_sel, QT), I32)
        for i in range(n_sel):
            si = score[i:i + 1, :]
            ahead = (si > score) | ((si == score) & (jrow > i))
            rank = rank + ahead.astype(I32)
        sb_ref[pl.ds(g * n_sel, n_sel), :] = jnp.where(rank < top, 0.0, NEG).astype(sb_ref.dtype)


def _cmp_prompt(q_t, kcmp, cmap, gates, ov_t, batch, seq):
    nqt = seq // QT
    n_chunk = kcmp.shape[1]
    n_sel = ov_t.shape[0]
    top = min(SEL_TOP, n_sel)
    return pl.pallas_call(
        functools.partial(_cmp_prompt_kernel, n_sel=n_sel, top=top),
        grid=(batch, nqt),
        in_specs=[
            pl.BlockSpec((D_Q, QT), lambda b, i: (0, b * nqt + i)),
            pl.BlockSpec((1, n_chunk, D_KV), lambda b, i: (b, 0, 0)),
            pl.BlockSpec(cmap.shape, lambda b, i: (0, 0, 0)),
            pl.BlockSpec((QT, LANES), lambda b, i: (b * nqt + i, 0)),
            pl.BlockSpec(ov_t.shape, lambda b, i: (0, 0)),
        ],
        out_specs=[
            pl.BlockSpec((QT, D_Q), lambda b, i: (b * nqt + i, 0)),
            pl.BlockSpec((None, N_KV * n_sel, QT), lambda b, i: (b * nqt + i, 0, 0)),
        ],
        out_shape=[
            jax.ShapeDtypeStruct((batch * seq, D_Q), F32),
            jax.ShapeDtypeStruct((batch * nqt, N_KV * n_sel, QT), BF16),
        ],
        compiler_params=_params(("parallel", "parallel")),
        name="cmp_attn_prompt",
    )(q_t, kcmp, cmap, gates, ov_t)


_TK = 2 * LANES


def _sel_prompt_kernel(qt_ref, kv_ref, sb_ref, map_ref, gate_ref, *rest, n_sel, seq, n_riders):
    rider_in = rest[:n_riders]
    o_ref = rest[n_riders]
    rider_out = rest[n_riders + 1:2 * n_riders + 1]
    ka_ref, vt_ref, qa_ref = rest[2 * n_riders + 1:]
    for src, dst in zip(rider_in, rider_out):
        dst[...] = src[...].astype(dst.dtype)
    qt = pl.program_id(1)
    aug = 2 * HEAD_DIM
    half = D_KV // 2

    @pl.when(qt == 0)
    def _():
        pos = lax.broadcasted_iota(I32, (seq, aug - HEAD_DIM), 0)
        blk = lax.broadcasted_iota(I32, (seq, aug - HEAD_DIM), 1)
        onehot = jnp.where(pos // SEL_LEN == blk, 1.0, 0.0)
        k_rows = kv_ref[0, pl.ds(0, half), :].T
        for g in range(N_KV):
            ka_ref[g] = jnp.concatenate([k_rows[:, g * HEAD_DIM:(g + 1) * HEAD_DIM], onehot], axis=1).astype(BF16)
            vt_ref[g] = kv_ref[0, pl.ds(half + g * HEAD_DIM, HEAD_DIM), :].astype(BF16)

    gate = gate_ref[...]
    sb = sb_ref[...]
    diag = qt // 2
    n_tiles = diag + 1
    pad = jnp.zeros((aug - HEAD_DIM - n_sel, GROUP_ROWS), BF16)
    for g in range(N_KV):
        sbg = sb[g * n_sel:(g + 1) * n_sel]
        qa_ref[g] = jnp.concatenate([_group_queries(qt_ref, g), jnp.concatenate([sbg] * HPG, axis=1), pad], axis=0)

    def tile(kt, carry):
        start = pl.multiple_of(kt * _TK, _TK)
        kind = jnp.where(kt == diag, qt % 2, jnp.where((kt == diag - 1) & (qt % 2 == 0), 2, 3))
        def scores(g):
            return _mm(ka_ref[g, pl.ds(start, _TK), :], qa_ref[g]) + map_ref[g, kind]

        def softmax(g, st):
            m_i, l_i, _ = carry[g]
            m_new = jnp.maximum(m_i, jnp.max(st, axis=0, keepdims=True))
            alpha = jnp.exp2(m_i - m_new)
            p = jnp.exp2(st - m_new)
            return m_new, alpha * l_i + jnp.sum(p, axis=0, keepdims=True), alpha, p.astype(BF16)

        def values(g, sm):
            m_new, l_new, alpha, pb = sm
            return m_new, l_new, alpha * carry[g][2] + _mm(vt_ref[g, :, pl.ds(start, _TK)], pb)

        return tuple(_staggered(N_KV, scores, softmax, values, ahead=2, lag=1))

    init = tuple((jnp.full((1, GROUP_ROWS), -jnp.inf, F32), jnp.zeros((1, GROUP_ROWS), F32),
                  jnp.zeros((HEAD_DIM, GROUP_ROWS), F32)) for _ in range(N_KV))
    final = lax.fori_loop(0, n_tiles, tile, init)
    for g in range(N_KV):
        _, l_f, acc = final[g]
        _store_heads(o_ref, acc / l_f, gate, g, 1)


def _sel_prompt(q_t, kv_t, selbias, smap, gates, batch, seq, riders=()):
    nqt = seq // QT
    n_sel = selbias.shape[1] // N_KV
    ride = _rider_specs(riders, batch * nqt, lambda b, i: (b * nqt + i, 0))
    out = pl.pallas_call(
        functools.partial(_sel_prompt_kernel, n_sel=n_sel, seq=seq, n_riders=len(riders)),
        grid=(batch, nqt),
        in_specs=[
            pl.BlockSpec((D_Q, QT), lambda b, i: (0, b * nqt + i)),
            pl.BlockSpec((1, D_KV, seq), lambda b, i: (b, 0, 0)),
            pl.BlockSpec((None, N_KV * n_sel, QT), lambda b, i: (b * nqt + i, 0, 0)),
            pl.BlockSpec(smap.shape, lambda b, i: (0, 0, 0, 0)),
            pl.BlockSpec((QT, LANES), lambda b, i: (b * nqt + i, 0)),
        ] + ride,
        out_specs=[pl.BlockSpec((QT, D_Q), lambda b, i: (b * nqt + i, 0))] + ride,
        out_shape=[jax.ShapeDtypeStruct((batch * seq, D_Q), F32)]
        + [jax.ShapeDtypeStruct(a.shape, BF16) for a in riders],
        scratch_shapes=[pltpu.VMEM((N_KV, seq, 2 * HEAD_DIM), BF16), pltpu.VMEM((N_KV, HEAD_DIM, seq), BF16),
                        pltpu.VMEM((N_KV, 2 * HEAD_DIM, GROUP_ROWS), BF16)],
        compiler_params=_params(("arbitrary", "arbitrary"), 56),
        name="sel_attn_prompt",
    )(q_t, kv_t, selbias, smap, gates, *riders)
    return out[0], out[1:]


_WIN_TILES = WINDOW // QT + 1


def _win_prompt_kernel(qt_ref, *refs):
    kv_refs = refs[:_WIN_TILES]
    map_ref, gate_ref, o_ref = refs[_WIN_TILES:]
    qt = pl.program_id(1)
    gate = gate_ref[...]
    half = D_KV // 2
    span = _WIN_TILES * QT
    key = lax.broadcasted_iota(I32, (span, GROUP_ROWS), 0)
    in_seq = key >= (_WIN_TILES - 1 - qt) * QT
    k_rows = jnp.concatenate([r[0, pl.ds(0, half), :].T for r in kv_refs], axis=0)
    def scores(g):
        kg = k_rows[:, g * HEAD_DIM:(g + 1) * HEAD_DIM].astype(BF16)
        return jnp.where(in_seq, _mm(kg, _group_queries(qt_ref, g)) + map_ref[g], NEG)

    def softmax(g, st):
        e = jnp.exp2(st - jnp.max(st, axis=0, keepdims=True))
        return (e / jnp.sum(e, axis=0, keepdims=True)).astype(BF16)

    def values(g, pb):
        vgt = jnp.concatenate([r[0, pl.ds(half + g * HEAD_DIM, HEAD_DIM), :] for r in kv_refs],
                              axis=1).astype(BF16)
        _store_heads(o_ref, _mm(vgt, pb), gate, g, 2)

    _staggered(N_KV, scores, softmax, values, ahead=2, lag=1)


def _win_prompt(q_t, kv_t, wmap, gates, batch, seq):
    nqt = seq // QT

    def kv_spec(k):
        return pl.BlockSpec((1, D_KV, QT), lambda b, i: (b, 0, jnp.maximum(i - (_WIN_TILES - 1) + k, 0)))

    return pl.pallas_call(
        _win_prompt_kernel,
        grid=(batch, nqt),
        in_specs=[pl.BlockSpec((D_Q, QT), lambda b, i: (0, b * nqt + i))]
        + [kv_spec(k) for k in range(_WIN_TILES)]
        + [pl.BlockSpec(wmap.shape, lambda b, i: (0, 0, 0)),
           pl.BlockSpec((QT, LANES), lambda b, i: (b * nqt + i, 0))],
        out_specs=pl.BlockSpec((QT, D_Q), lambda b, i: (b * nqt + i, 0)),
        out_shape=jax.ShapeDtypeStruct((batch * seq, D_Q), F32),
        compiler_params=_params(("parallel", "parallel")),
        name="win_attn_prompt",
    )(q_t, *([kv_t] * _WIN_TILES), wmap, gates)


def _group_rows(x_by_group):
    row = lax.broadcasted_iota(I32, x_by_group[0].shape, 0)
    out = x_by_group[0]
    for g in range(1, N_KV):
        out = jnp.where(row // HPG == g, x_by_group[g], out)
    return out


_CMP_SAMPLE_ROWS = 4


def _cmp_sample_kernel(q_ref, kc_ref, bias_ref, ov_ref, gate_ref, o_ref, idx_ref, *, n_sel, past):
    bias = bias_ref[...]
    half = D_KV // 2
    imps = []
    for bl in range(_CMP_SAMPLE_ROWS):
        q = q_ref[bl]
        kv = kc_ref[bl].astype(BF16)
        s = _group_rows([_mm_nt(q, kv[:, g * HEAD_DIM:(g + 1) * HEAD_DIM]) for g in range(N_KV)]) + bias
        e = jnp.exp(s - jnp.max(s, axis=-1, keepdims=True))
        p = e / jnp.sum(e, axis=-1, keepdims=True) * (bias > 0.5 * NEG).astype(F32)
        pb = p.astype(BF16)
        o = _group_rows([_mm(pb, kv[:, half + g * HEAD_DIM:half + (g + 1) * HEAD_DIM]) for g in range(N_KV)])
        o_ref[bl] = o * gate_ref[bl]
        imp_h = _mm(pb, ov_ref[...])
        imps += [imp_h[g * HPG:g * HPG + 1] + imp_h[g * HPG + 1:g * HPG + 2] + imp_h[g * HPG + 2:g * HPG + 3]
                 + imp_h[g * HPG + 3:g * HPG + 4] for g in range(N_KV)]
    imp = jnp.concatenate(imps, axis=0)
    rows, nsp = imp.shape
    j = lax.broadcasted_iota(I32, (rows, nsp), 1)
    cur = past // SEL_LEN
    forced = (j == 0) | (j == cur) | (j == cur - 1)
    valid = (j * SEL_LEN <= past) & (j < n_sel)
    score = jnp.where(valid, jnp.where(forced, jnp.inf, imp), -jnp.inf)
    slot = lax.broadcasted_iota(I32, (rows, LANES), 1)
    res = jnp.zeros((rows, LANES), I32)
    for k in range(min(SEL_TOP, n_sel)):
        best = jnp.max(score, axis=-1, keepdims=True)
        pick = jnp.min(jnp.where(score == best, j, nsp), axis=-1, keepdims=True)
        res = jnp.where(slot == k, pick, res)
        score = jnp.where(j == pick, -jnp.inf, score)
    idx_ref[...] = res


def _cmp_sample(q3, kcmp, bias, ov, gate3, n_sel, past):
    bs, n_chunk, _ = kcmp.shape
    nb = _CMP_SAMPLE_ROWS
    heads = pl.BlockSpec((nb, N_HEADS, HEAD_DIM), lambda b: (b, 0, 0))
    return pl.pallas_call(
        functools.partial(_cmp_sample_kernel, n_sel=n_sel, past=past),
        grid=(bs // nb,),
        in_specs=[
            heads,
            pl.BlockSpec((nb, n_chunk, D_KV), lambda b: (b, 0, 0)),
            pl.BlockSpec(bias.shape, lambda b: (0, 0)),
            pl.BlockSpec(ov.shape, lambda b: (0, 0)),
            heads,
        ],
        out_specs=[heads, pl.BlockSpec((nb * N_KV, LANES), lambda b: (b, 0))],
        out_shape=[
            jax.ShapeDtypeStruct((bs, N_HEADS, HEAD_DIM), F32),
            jax.ShapeDtypeStruct((bs * N_KV, LANES), I32),
        ],
        compiler_params=_params(("parallel",)),
        name="cmp_attn_sample",
    )(q3, kcmp, bias, ov, gate3)


def _sel_sample_kernel(blk_ref, phys_ref, q_ref, new_ref, bias_ref, gate_ref, *refs, top, n_pages):
    kv_refs = refs[:top]
    k_refs = [r.at[0] for r in kv_refs]
    v_refs = [r.at[1] for r in kv_refs]
    o_ref = refs[top]
    del phys_ref
    b = pl.program_id(0)
    g = pl.program_id(1)
    q = q_ref[0, 0]
    lane = lax.broadcasted_iota(I32, (8, PAGE), 1)
    scores = []
    for k in range(top):
        blk = blk_ref[(b * N_KV + g) * top + k]
        page = jnp.minimum(blk // 2, n_pages)
        s = _mm(q, k_refs[k][...].astype(BF16)) + bias_ref[page]
        ok = (lane // SEL_LEN == blk % 2) & (blk < 2 * n_pages)
        scores.append(jnp.where(ok, s, NEG))
    k_new = new_ref[0, 0, 0:1, :].astype(BF16).astype(F32)
    v_new = new_ref[0, 0, 1:2, :].astype(BF16).astype(F32)
    s_new = jnp.sum(q.astype(F32) * k_new, axis=-1, keepdims=True) + bias_ref[n_pages]
    scores.append(s_new)
    s_all = jnp.concatenate(scores, axis=1)
    e = jnp.exp(s_all - jnp.max(s_all, axis=-1, keepdims=True))
    p = e / jnp.sum(e, axis=-1, keepdims=True)
    o = p[:, top * PAGE:top * PAGE + 1].astype(BF16).astype(F32) * v_new
    for k in range(top):
        o = o + _mm_nt(p[:, k * PAGE:(k + 1) * PAGE].astype(BF16), v_refs[k][...].astype(BF16))
    o_ref[0, 0] = o * gate_ref[0, 0]


def _sel_sample(blk, phys, q4, new4, bias4, gate4, pool5, top, n_pages):
    bs = q4.shape[0]

    def kv_spec(k):
        return pl.BlockSpec(
            (None, 2, None, HEAD_DIM, PAGE),
            lambda b, g, blk_r, phys_r: (phys_r[(b * N_KV + g) * top + k], 0, g, 0, 0))

    grp = lambda b, g, blk_r, phys_r: (b, g, 0, 0)
    grid_spec = pltpu.PrefetchScalarGridSpec(
        num_scalar_prefetch=2,
        grid=(bs, N_KV),
        in_specs=[
            pl.BlockSpec((1, 1, 8, HEAD_DIM), grp),
            pl.BlockSpec((1, 1, 8, HEAD_DIM), grp),
            pl.BlockSpec((None, n_pages + 1, 8, LANES), lambda b, g, blk_r, phys_r: (g, 0, 0, 0)),
            pl.BlockSpec((1, 1, 8, HEAD_DIM), grp),
        ] + [kv_spec(k) for k in range(top)],
        out_specs=pl.BlockSpec((1, 1, 8, HEAD_DIM), grp),
    )
    return pl.pallas_call(
        functools.partial(_sel_sample_kernel, top=top, n_pages=n_pages),
        grid_spec=grid_spec,
        out_shape=jax.ShapeDtypeStruct((bs, N_KV, 8, HEAD_DIM), F32),
        compiler_params=_params(("parallel", "parallel")),
        name="sel_attn_sample",
    )(blk, phys, q4, new4, bias4, gate4, *([pool5] * top))


def _win_sample_kernel(q_ref, new_ref, st_ref, col_ref, bias_ref, gate_ref, o_ref, nst_ref):
    st = st_ref[0]
    half = D_KV // 2
    for g in range(N_KV):
        q = q_ref[0, g]
        bias = bias_ref[g]
        s_old = _mm(q, st[g * HEAD_DIM:(g + 1) * HEAD_DIM, :].astype(BF16)) + bias[:, 0:WINDOW]
        k_new = new_ref[0, g, 0:1, :].astype(BF16).astype(F32)
        v_new = new_ref[0, g, 1:2, :].astype(BF16).astype(F32)
        s_new = jnp.sum(q.astype(F32) * k_new, axis=-1, keepdims=True) + bias[:, WINDOW:WINDOW + LANES]
        s_all = jnp.concatenate([s_old, s_new], axis=1)
        e = jnp.exp(s_all - jnp.max(s_all, axis=-1, keepdims=True))
        p = e / jnp.sum(e, axis=-1, keepdims=True)
        o = _mm_nt(p[:, 0:WINDOW].astype(BF16), st[half + g * HEAD_DIM:half + (g + 1) * HEAD_DIM, :].astype(BF16))
        o = o + p[:, WINDOW:WINDOW + 1].astype(BF16).astype(F32) * v_new
        o_ref[0, g] = o * gate_ref[0, g]
    rolled = pltpu.roll(st, WINDOW - 1, axis=1)
    lane = lax.broadcasted_iota(I32, (D_KV, LANES), 1)
    nst_ref[0, :, pl.ds(0, WINDOW - LANES)] = rolled[:, 0:WINDOW - LANES]
    nst_ref[0, :, pl.ds(WINDOW - LANES, LANES)] = jnp.where(
        lane == LANES - 1, col_ref[0], rolled[:, WINDOW - LANES:WINDOW])


def _win_sample(q4, new4, state_t, newcol, bias4, gate4):
    bs = q4.shape[0]
    grp = pl.BlockSpec((1, N_KV, 8, HEAD_DIM), lambda b: (b, 0, 0, 0))
    return pl.pallas_call(
        _win_sample_kernel,
        grid=(bs,),
        in_specs=[
            grp, grp,
            pl.BlockSpec((1, D_KV, WINDOW), lambda b: (b, 0, 0)),
            pl.BlockSpec((1, D_KV, LANES), lambda b: (b, 0, 0)),
            pl.BlockSpec(bias4.shape, lambda b: (0, 0, 0)),
            grp,
        ],
        out_specs=[grp, pl.BlockSpec((1, D_KV, WINDOW), lambda b: (b, 0, 0))],
        out_shape=[
            jax.ShapeDtypeStruct((bs, N_KV, 8, HEAD_DIM), F32),
            jax.ShapeDtypeStruct((bs, D_KV, WINDOW), F32),
        ],
        compiler_params=_params(("parallel",)),
        name="win_attn_sample",
    )(q4, new4, state_t, newcol, bias4, gate4)


_MOE_ROWS = 256
_MOE_SLACK = 2 * _MOE_ROWS


def _row_copy(src, src_row, dst, dst_row, sem):
    return pltpu.make_async_copy(src.at[pl.ds(src_row, 1)], dst.at[pl.ds(dst_row, 1)], sem)


def _expert_kernel(be_ref, nu_ref, dst_ref, src_ref, nxt_ref, h_ref, w_ref, wg_ref, wu_ref, wd_ref, yk_ref,
                   xbuf, ybuf, gsem, ssem, *, n_rows, n_live):
    del be_ref
    i = pl.program_id(0)
    slot = i % 2
    n_used = nu_ref[0]

    def gather(rows_ref, dst_slot):
        for r in range(_MOE_ROWS):
            _row_copy(h_ref, rows_ref[0, r], xbuf.at[dst_slot], r, gsem.at[dst_slot]).start()

    def wait_gather(s):
        for r in range(_MOE_ROWS):
            _row_copy(h_ref, 0, xbuf.at[s], 0, gsem.at[s]).wait()

    def scatter(s):
        for r in range(_MOE_ROWS):
            _row_copy(ybuf.at[s], r, yk_ref, dst_ref[0, r], ssem.at[s]).start()

    def wait_scatter(s):
        for r in range(_MOE_ROWS):
            _row_copy(ybuf.at[s], 0, yk_ref, 0, ssem.at[s]).wait()

    @pl.when(i == 0)
    def _():
        gather(src_ref, 0)
        ybuf[1] = jnp.zeros((_MOE_ROWS, ybuf.shape[2]), F32)
        fills = [
            pltpu.make_async_copy(ybuf.at[1, pl.ds(0, min(_MOE_ROWS, n_rows - off))],
                                  yk_ref.at[pl.ds(k * n_rows + off, min(_MOE_ROWS, n_rows - off))], ssem.at[1])
            for k in range(TOP_K) for off in range(n_live, n_rows, _MOE_ROWS)]
        for cp in fills:
            cp.start()
        for cp in fills:
            cp.wait()

    @pl.when(i + 1 < n_used)
    def _():
        gather(nxt_ref, 1 - slot)

    @pl.when(i < n_used)
    def _():
        wait_gather(slot)

        @pl.when(i >= 2)
        def _():
            wait_scatter(slot)

        x = xbuf[slot].astype(BF16)
        a = _silu(_mm(x, wg_ref[...])) * _mm(x, wu_ref[...])
        ybuf[slot] = _mm(a.astype(BF16), wd_ref[...]) * w_ref[...]
        scatter(slot)

        @pl.when(i == n_used - 1)
        def _():
            @pl.when(i >= 1)
            def _():
                wait_scatter(1 - slot)

            wait_scatter(slot)


def _experts(blk_e, n_used, rows_src, rows_dst, rows_w, h2_all, n_live, w_gate, w_up, w_down):
    n_rows, d = h2_all.shape
    n_blk = blk_e.shape[0]
    src3 = rows_src.reshape(n_blk, 1, _MOE_ROWS)
    dst3 = rows_dst.reshape(n_blk, 1, _MOE_ROWS)
    wspec = lambda a: pl.BlockSpec((None,) + a.shape[1:], lambda i, be, nu: (be[i], 0, 0))
    smem_blk = lambda fn: pl.BlockSpec((None, 1, _MOE_ROWS), fn, memory_space=pltpu.SMEM)
    grid_spec = pltpu.PrefetchScalarGridSpec(
        num_scalar_prefetch=2,
        grid=(n_blk,),
        in_specs=[
            smem_blk(lambda i, be, nu: (i, 0, 0)),
            smem_blk(lambda i, be, nu: (i, 0, 0)),
            smem_blk(lambda i, be, nu: (jnp.minimum(i + 1, n_blk - 1), 0, 0)),
            pl.BlockSpec(memory_space=pl.ANY),
            pl.BlockSpec((_MOE_ROWS, 1), lambda i, be, nu: (i, 0)),
            wspec(w_gate), wspec(w_up), wspec(w_down),
        ],
        out_specs=pl.BlockSpec(memory_space=pl.ANY),
        scratch_shapes=[
            pltpu.VMEM((2, _MOE_ROWS, d), F32),
            pltpu.VMEM((2, _MOE_ROWS, d), F32),
            pltpu.SemaphoreType.DMA((2,)),
            pltpu.SemaphoreType.DMA((2,)),
        ],
    )
    return pl.pallas_call(
        functools.partial(_expert_kernel, n_rows=n_rows, n_live=n_live),
        grid_spec=grid_spec,
        out_shape=jax.ShapeDtypeStruct((TOP_K * n_rows, d), F32),
        compiler_params=pltpu.CompilerParams(dimension_semantics=("arbitrary",), vmem_limit_bytes=56 * MIB,
                                             disable_bounds_checks=True),
        name="moe_experts",
    )(blk_e, n_used, dst3, src3, src3, h2_all, rows_w, w_gate, w_up, w_down)


def _combine_kernel(x_ref, y0_ref, y1_ref, g_ref, o_ref):
    x = x_ref[...] + (y0_ref[...] + y1_ref[...])
    ms = jnp.mean(x * x, axis=-1, keepdims=True)
    o_ref[...] = x * lax.rsqrt(ms + EPS) * g_ref[...]


def _combine(x1, yk, gfin, row0, tm):
    n, d = x1.shape
    yk3 = yk.reshape(TOP_K, -1, d)
    blk0 = row0 // tm
    return pl.pallas_call(
        _combine_kernel,
        grid=(n // tm,),
        in_specs=[
            pl.BlockSpec((tm, d), lambda i: (i, 0)),
            pl.BlockSpec((None, tm, d), lambda i: (0, blk0 + i, 0)),
            pl.BlockSpec((None, tm, d), lambda i: (1, blk0 + i, 0)),
            pl.BlockSpec((1, d), lambda i: (0, 0)),
        ],
        out_specs=pl.BlockSpec((tm, d), lambda i: (i, 0)),
        out_shape=jax.ShapeDtypeStruct((n, d), F32),
        compiler_params=_params(("parallel",)),
        name="moe_combine",
    )(x1, yk3, yk3, gfin)


def _dispatch(eid, w, counts, n_rows):
    a = eid.shape[0]
    tb = _MOE_ROWS
    order = jnp.argsort(eid).astype(I32)
    padded = (counts + tb - 1) // tb * tb
    seg_end = jnp.cumsum(padded)
    seg_start = seg_end - padded
    start = jnp.cumsum(counts) - counts
    n_blk = (a + N_EXPERTS * (tb - 1) + tb - 1) // tb
    blk_e = jnp.minimum(jnp.sum((seg_end[None, :] <= (jnp.arange(n_blk) * tb)[:, None]).astype(I32), axis=1),
                        N_EXPERTS - 1)
    blk = jnp.arange(n_blk)[:, None]
    r = jnp.arange(tb)[None, :]
    off = (blk * tb - seg_start[blk_e][:, None]) + r
    valid = off < counts[blk_e][:, None]
    asg = order[jnp.clip(start[blk_e][:, None] + off, 0, a - 1)]
    rows_src = jnp.where(valid, asg // TOP_K, 0).astype(I32)
    rows_dst = jnp.where(valid, (asg % TOP_K) * n_rows + asg // TOP_K, (blk % 2) * n_rows + (n_rows - tb) + r)
    rows_w = jnp.where(valid, w[asg], 0.0)
    n_used = (seg_end[-1] // tb).astype(I32).reshape(1)
    return rows_src, rows_dst.astype(I32), rows_w.reshape(-1, 1), blk_e.astype(I32), n_used


def _overlap(n_chunk, n_sel, n_sel_pad):
    cs = jnp.arange(n_chunk)[:, None] * CMP_STRIDE
    ss = jnp.arange(n_sel_pad)[None, :] * SEL_LEN
    hit = (cs < ss + SEL_LEN) & (cs + CMP_LEN > ss) & (jnp.arange(n_sel_pad)[None, :] < n_sel)
    return hit.astype(BF16)


def kernel(x_prompt, x_sample, cache_cmp_kv, cache_sel_kv, state_win_kv, state_conv, page_table, rel_bias,
           norm_mix, w_in, b_in, conv_w, conv_b, conv_ln_g, conv_ln_b, w_conv_out, b_conv_out,
           cmp_pe_k, cmp_w1_k, cmp_w2_k, cmp_pe_v, cmp_w1_v, cmp_w2_v, w_nsa_out, w_out, norm_ffn,
           w_rg, b_rg, w_re, b_re, w_gate, w_up, w_down, norm_final):
    batch, seq, _ = x_prompt.shape
    bs = x_sample.shape[0]
    n_pool = cache_cmp_kv.shape[1]
    n_pages = page_table.shape[1]
    past = n_pages * PAGE
    n_tok = batch * seq
    assert bs + _MOE_ROWS <= _MOE_SLACK and n_tok % bs == 0 and seq % _TK == 0 and bs % 16 == 0
    l = 0

    wt = w_in[l].T.astype(BF16)
    bias_in = b_in[l]
    cuts = [0, D_CONV, 2 * D_CONV, 2 * D_CONV + D_Q, 2 * D_CONV + D_Q + 3 * D_KV]
    cuts += [cuts[-1] + 3 * N_HEADS, cuts[-1] + 3 * N_HEADS + D_MODEL, cuts[-1] + 3 * N_HEADS + 2 * D_MODEL]
    seg = lambda k: (wt[cuts[k]:cuts[k + 1]], bias_in[cuts[k]:cuts[k + 1]][None, :])
    (wa, ba), (wb, bb), (wq, bq), (wkv, bkv), (wzg, bzg), (wga, bga), (wgb, bgb) = [seg(k) for k in range(7)]
    wzg = jnp.pad(wzg, ((0, LANES - 3 * N_HEADS), (0, 0)))
    bzg = jnp.pad(bzg, ((0, 0), (0, LANES - 3 * N_HEADS)))
    wc = w_conv_out[l].astype(BF16)
    wn = w_nsa_out[l].astype(BF16)
    wo = w_out[l].astype(BF16)
    wr = jnp.zeros((LANES, D_MODEL), F32).at[0:N_GROUPS].set(w_rg[l].T).at[8:8 + N_EXPERTS].set(w_re[l].T)
    br = jnp.zeros((LANES,), F32).at[0:N_GROUPS].set(b_rg[l]).at[8:8 + N_EXPERTS].set(b_re[l].reshape(-1))
    wr, br = wr.astype(BF16), br[:, None]
    cw = _compress_weights(cmp_pe_k[l], cmp_w1_k[l], cmp_w2_k[l], cmp_pe_v[l], cmp_w1_v[l], cmp_w2_v[l])
    rb_t = rel_bias.T
    gmix, gffn, gfin = norm_mix[l][None, :], norm_ffn[l][None, :], norm_final[None, :]
    vec = lambda a: a[l][None, :]

    q_scale = HEAD_DIM ** -0.5
    xp = x_prompt.reshape(n_tok, D_MODEL)
    xs = x_sample.reshape(bs, D_MODEL)
    flat = lambda w: w[l].reshape(-1, w.shape[-1])
    (hp, up, qp, gates_p, *kv_p), (wg_bf,) = _project_prompt(
        xp, gmix, wa, wb, wq, wkv, wzg, ba, bb, bq.reshape(-1, 1), bkv.reshape(-1, 1), bzg,
        q_scale * LOG2E, batch, seq, 256, riders=(flat(w_gate),))
    hs = _rmsnorm(xs, gmix, bs)
    us = _glu(hs, wa, wb, ba, bb, bs, 512)
    qs = _proj(hs, wq, bq, lambda z: z * q_scale, BF16, bs, 512, "proj_q")
    gates_s = _proj(hs, wzg, bzg, _sigmoid, F32, bs, LANES, "proj_head_gates")
    kv_s = _proj(hs, wkv, bkv, lambda z: z, F32, bs, 512, "proj_kv_sample")

    conv_args = (conv_w[l], vec(conv_b), vec(conv_ln_g), vec(conv_ln_b))
    cp = _conv_prompt(up, *conv_args, batch, seq, 256)
    cs, new_conv_s = _conv_sample(state_conv[l].transpose(1, 0, 2), us, *conv_args)

    nqt = seq // QT
    n_chunk_p = seq // CMP_STRIDE
    n_chunk_s = n_pages * _CHUNKS_PER_PAGE
    cmap, wmap, smap = _bias_prompt(rb_t, nqt, n_chunk_p)
    cbias_s, wbias_s, sbias_s = _bias_sample(rb_t, past, n_chunk_s, n_pages)

    ident = jnp.zeros((1,), I32)
    kcmp_p, _ = _compress(kv_p[0], ident, cw, batch, seq // PAGE, seq // PAGE, False)
    feat_major = lambda a: a.transpose(0, 2, 3, 4, 1)
    pool_cmp = feat_major(cache_cmp_kv[l]).reshape(n_pool, D_KV, PAGE)
    pp_s = min(32, n_pages)
    kcmp_s, (wd_bf,) = _compress(pool_cmp, page_table.reshape(-1), cw, bs, n_pages, pp_s, True,
                                 riders=(flat(w_down),))

    n_sel_p = seq // SEL_LEN
    o_cmp_p, selbias = _cmp_prompt(qp, kcmp_p, cmap, gates_p, _overlap(n_chunk_p, n_sel_p, n_sel_p).T,
                                   batch, seq)
    n_sel_s = -(-(past + 1) // SEL_LEN)
    n_sel_pad = -(-n_sel_s // LANES) * LANES
    top_s = min(SEL_TOP, n_sel_s)
    head_gate = lambda k: jnp.broadcast_to(
        gates_s[:, k * N_HEADS:(k + 1) * N_HEADS, None], (bs, N_HEADS, HEAD_DIM))
    qs3 = qs.reshape(bs, N_HEADS, HEAD_DIM)
    o_cmp_s, idx_s = _cmp_sample(qs3, kcmp_s, cbias_s, _overlap(n_chunk_s, n_sel_s, n_sel_pad), head_gate(0),
                                 n_sel_s, past)

    o_sel_p, (wu_bf,) = _sel_prompt(qp, kv_p[1], selbias, smap, gates_p, batch, seq, riders=(flat(w_up),))
    by_group = lambda a: jnp.pad(a.reshape(bs, N_KV, HPG, -1), ((0, 0), (0, 0), (0, 8 - HPG), (0, 0)))
    q4 = by_group(qs3)
    blk = idx_s[:, :top_s].reshape(-1)
    phys = jnp.take_along_axis(
        page_table, jnp.minimum(blk // 2, n_pages - 1).reshape(bs, -1), axis=1).reshape(-1)
    kv_new = lambda k: jnp.pad(
        kv_s[:, k * D_KV:(k + 1) * D_KV].reshape(bs, 2, N_KV, HEAD_DIM).transpose(0, 2, 1, 3),
        ((0, 0), (0, 0), (0, 6), (0, 0)))
    bias_by_group = lambda a: jnp.pad(
        jnp.moveaxis(a, -2, 0).reshape((N_KV, HPG) + a.shape[:-2] + a.shape[-1:]),
        ((0, 0), (0, 8 - HPG)) + ((0, 0),) * (a.ndim - 1))
    sb4 = jnp.moveaxis(bias_by_group(sbias_s), 1, 2)
    pool_sel = feat_major(cache_sel_kv[l])
    o_sel_s = _sel_sample(blk, phys, q4, kv_new(1), sb4, by_group(head_gate(1)), pool_sel, top_s, n_pages)

    o_win_p = _win_prompt(qp, kv_p[2], wmap, gates_p, batch, seq)
    win_t = feat_major(state_win_kv[l]).reshape(bs, D_KV, WINDOW)
    newcol = jnp.broadcast_to(kv_s[:, 2 * D_KV:3 * D_KV, None], (bs, D_KV, LANES))
    o_win_s, new_win_s = _win_sample(q4, kv_new(2), win_t, newcol, bias_by_group(wbias_s), by_group(head_gate(2)))

    m_p, _ = _merge_a(hp, cp, o_cmp_p, o_sel_p, o_win_p, wga, wgb, wc, wn, bga, bgb, vec(b_conv_out), 512, 1024)
    (x1_p, h2_all, eid_p, rw_p, cnt_p), _ = _merge_b(xp, m_p, wo, gffn, wr, br, 512, slack=_MOE_SLACK)
    from_groups = lambda a: a[:, :, :HPG, :].reshape(bs, D_Q)
    m_s, _ = _merge_a(hs, cs, o_cmp_s.reshape(bs, D_Q), from_groups(o_sel_s), from_groups(o_win_s),
                      wga, wgb, wc, wn, bga, bgb, vec(b_conv_out), bs, 512)
    (x1_s, h2_all, eid_s, rw_s, cnt_s), _ = _merge_b(xs, m_s, wo, gffn, wr, br, bs, h2_all=h2_all, row0=n_tok)

    eid = jnp.concatenate([eid_p[:TOP_K].T, eid_s[:TOP_K].T], axis=0).reshape(-1)
    rw = jnp.concatenate([rw_p[:TOP_K].T, rw_s[:TOP_K].T], axis=0).reshape(-1)
    counts = jnp.sum(cnt_p[:, :, 0], axis=0) + jnp.sum(cnt_s[:, :, 0], axis=0)
    rows_src, rows_dst, rows_w, blk_e, n_used = _dispatch(eid, rw, counts, h2_all.shape[0])
    yk = _experts(blk_e, n_used, rows_src, rows_dst, rows_w, h2_all, n_tok + bs,
                  wg_bf.reshape(w_gate.shape[1:]), wu_bf.reshape(w_up.shape[1:]), wd_bf.reshape(w_down.shape[1:]))
    y_p = _combine(x1_p, yk, gfin, 0, 256)
    y_s = _combine(x1_s, yk, gfin, n_tok, bs)

    def cache_rows(a_t, rows):
        return a_t.reshape(a_t.shape[0], 2, N_KV, HEAD_DIM, rows).transpose(0, 4, 1, 2, 3)[None]

    new_row = lambda k: kv_s[:, k * D_KV:(k + 1) * D_KV].reshape(1, bs, 1, 2, N_KV, HEAD_DIM)
    hist = CONV_W - 1
    return (
        y_p.reshape(batch, seq, D_MODEL),
        y_s.reshape(bs, 1, D_MODEL),
        cache_rows(kv_p[0], seq),
        new_row(0),
        cache_rows(kv_p[1], seq),
        new_row(1),
        cache_rows(kv_p[2][:, :, seq - WINDOW:], WINDOW),
        cache_rows(new_win_s, WINDOW),
        up.reshape(batch, seq, D_CONV)[None, :, seq - hist:],
        new_conv_s.transpose(1, 0, 2)[None],
    )
```
